```python
import math
import jax, jax.numpy as jnp
from jax import lax
import numpy as np

D_MODEL = 1024
BATCH = 16
SEQ = 2048
DEPTH = 1

MIX_WIDTH = D_MODEL
SB_WIDTH = D_MODEL // 2
SB_HEADS = 8
SB_HEAD_DIM = SB_WIDTH // SB_HEADS
POOL_WIDTH = MIX_WIDTH - SB_WIDTH
POOL_WINDOWS = (2, 4, 8, 16)
POOL_GROUPS = len(POOL_WINDOWS)
POOL_GROUP_DIM = POOL_WIDTH // POOL_GROUPS
IN_WIDTH = 3 * SB_WIDTH + POOL_WIDTH
D_FF = -(-8 * D_MODEL // (3 * 256)) * 256
Q_BLOCK = 128
N_MOD = 6
EPS = 1e-6

kernel_name = "hybrid_stickbreak_pool_block"


def rmsnorm(x, g):
    xf = x.astype(jnp.float32)
    y = xf * lax.rsqrt(jnp.mean(xf * xf, axis=-1, keepdims=True) + EPS)
    return (y * g.astype(jnp.float32)).astype(x.dtype)


def stick_breaking_attention(q, k, v):
    S = q.shape[2]
    inv_sqrt = 1.0 / math.sqrt(q.shape[-1])
    outs = []
    for i in range(S // Q_BLOCK):
        L = (i + 1) * Q_BLOCK
        qb = q[:, :, i * Q_BLOCK:L].astype(jnp.float32)
        kb = k[:, :, :L].astype(jnp.float32)
        vb = v[:, :, :L].astype(jnp.float32)
        z = jnp.einsum('bhqd,bhkd->bhqk', qb, kb) * inv_sqrt
        t_idx = i * Q_BLOCK + jnp.arange(Q_BLOCK)[:, None]
        s_idx = jnp.arange(L)[None, :]
        mask = s_idx < t_idx
        log1m = jnp.where(mask, -jax.nn.softplus(z), 0.0)
        after = lax.cumsum(log1m, axis=3, reverse=True) - log1m
        logw = jax.nn.log_sigmoid(z) + after
        w = jnp.where(mask, jnp.exp(jnp.where(mask, logw, 0.0)), 0.0)
        outs.append(jnp.einsum('bhqk,bhkd->bhqd', w, vb))
    return jnp.concatenate(outs, axis=2).astype(v.dtype)


def pooling_mixer(u, w_pool, pool_scale):
    B, S, P = u.shape
    uf = u.astype(jnp.float32)
    cs = jnp.concatenate([jnp.zeros((B, 1, P), jnp.float32), jnp.cumsum(uf, axis=1)], axis=1)
    t = jnp.arange(S)
    parts = []
    for g, win in enumerate(POOL_WINDOWS):
        sl = slice(g * POOL_GROUP_DIM, (g + 1) * POOL_GROUP_DIM)
        lo = jnp.maximum(t + 1 - win, 0)
        cnt = (t + 1 - lo).astype(jnp.float32)
        csg = cs[..., sl]
        mean = (csg[:, 1:] - csg[:, lo]) / cnt[None, :, None]
        parts.append(mean - uf[..., sl])
    pooled = jnp.stack(parts, axis=2)
    y = jnp.einsum('bsgc,gcd->bsgd', pooled, w_pool.astype(jnp.float32)).reshape(B, S, P)
    return (y * pool_scale.astype(jnp.float32)).astype(u.dtype)


def _fwd_setup_inputs(seed: int = 0) -> dict:
    key = jax.random.key(seed)
    ks = jax.random.split(key, 16)
    f32 = jnp.float32

    def nrm(k, shape, fan_in):
        return jax.random.normal(k, shape, f32) * fan_in ** -0.5

    def gain(k):
        return 1.0 + 0.05 * jax.random.normal(k, (DEPTH, D_MODEL), f32)

    return {
        "x": jax.random.normal(ks[0], (BATCH, SEQ, D_MODEL), f32),
        "c": jax.random.normal(ks[1], (BATCH, D_MODEL), f32),
        "w_cond": nrm(ks[2], (DEPTH, D_MODEL, N_MOD * D_MODEL), D_MODEL),
        "b_cond": 0.01 * jax.random.normal(ks[3], (DEPTH, N_MOD * D_MODEL), f32),
        "g_mix_pre": gain(ks[4]),
        "g_mix_post": gain(ks[5]),
        "w_in": nrm(ks[6], (DEPTH, D_MODEL, IN_WIDTH), D_MODEL),
        "w_pool": nrm(ks[7], (DEPTH, POOL_GROUPS, POOL_GROUP_DIM, POOL_GROUP_DIM), POOL_GROUP_DIM),
        "pool_scale": 1.0 + 0.1 * jax.random.normal(ks[8], (DEPTH, POOL_WIDTH), f32),
        "w_out": nrm(ks[9], (DEPTH, MIX_WIDTH, D_MODEL), MIX_WIDTH),
        "g_ffn_pre": gain(ks[10]),
        "g_ffn_post": gain(ks[11]),
        "w_gate": nrm(ks[12], (DEPTH, D_MODEL, D_FF), D_MODEL),
        "w_up": nrm(ks[13], (DEPTH, D_MODEL, D_FF), D_MODEL),
        "w_down": nrm(ks[14], (DEPTH, D_FF, D_MODEL), D_FF),
    }


def _fwd_reference(x, c, w_cond, b_cond, g_mix_pre, g_mix_post, w_in, w_pool, pool_scale,
              w_out, g_ffn_pre, g_ffn_post, w_gate, w_up, w_down):
    B, S, D = x.shape
    for l in range(DEPTH):
        mod = jax.nn.silu(c) @ w_cond[l] + b_cond[l]
        shift_m, scale_m, gate_m, shift_f, scale_f, gate_f = [
            m[:, None, :] for m in jnp.split(mod, N_MOD, axis=-1)]

        h = rmsnorm(x, g_mix_pre[l]) * (1.0 + scale_m) + shift_m
        proj = h @ w_in[l]
        q, k, v, u = jnp.split(proj, [SB_WIDTH, 2 * SB_WIDTH, 3 * SB_WIDTH], axis=-1)
        to_heads = lambda a: a.reshape(B, S, SB_HEADS, SB_HEAD_DIM).transpose(0, 2, 1, 3)
        attn = stick_breaking_attention(to_heads(q), to_heads(k), to_heads(v))
        attn = attn.transpose(0, 2, 1, 3).reshape(B, S, SB_WIDTH)
        pool = pooling_mixer(u, w_pool[l], pool_scale[l])
        mix = jnp.concatenate([attn, pool], axis=-1) @ w_out[l]
        x = x + gate_m * rmsnorm(mix, g_mix_post[l])

        h = rmsnorm(x, g_ffn_pre[l]) * (1.0 + scale_f) + shift_f
        f = (jax.nn.silu(h @ w_gate[l]) * (h @ w_up[l])) @ w_down[l]
        x = x + gate_f * rmsnorm(f, g_ffn_post[l])
    return x


import jax as _jax
import jax.numpy as _jnp

TWIN_FORMAT = 'train_step'
FWD_PARAMS = ['x', 'c', 'w_cond', 'b_cond', 'g_mix_pre', 'g_mix_post', 'w_in', 'w_pool', 'pool_scale', 'w_out', 'g_ffn_pre', 'g_ffn_post', 'w_gate', 'w_up', 'w_down']
TWIN_WEIGHTS = ['w_cond', 'b_cond', 'g_mix_pre', 'g_mix_post', 'w_in', 'w_pool', 'pool_scale', 'w_out', 'g_ffn_pre', 'g_ffn_post', 'w_gate', 'w_up', 'w_down']
TWIN_DIFF_INPUT = 'x'
TWIN_INPUTS = ['x', 'c', 'w_cond', 'b_cond', 'g_mix_pre', 'g_mix_post', 'w_in', 'w_pool', 'pool_scale', 'w_out', 'g_ffn_pre', 'g_ffn_post', 'w_gate', 'w_up', 'w_down', 'loss_target', 'm_w_cond', 'm_b_cond', 'm_g_mix_pre', 'm_g_mix_post', 'm_w_in', 'm_w_pool', 'm_pool_scale', 'm_w_out', 'm_g_ffn_pre', 'm_g_ffn_post', 'm_w_gate', 'm_w_up', 'm_w_down', 'v_w_cond', 'v_b_cond', 'v_g_mix_pre', 'v_g_mix_post', 'v_w_in', 'v_w_pool', 'v_pool_scale', 'v_w_out', 'v_g_ffn_pre', 'v_g_ffn_post', 'v_w_gate', 'v_w_up', 'v_w_down']
TWIN_OUTPUTS = ['loss', 'grad_x', 'grad_w_cond', 'grad_b_cond', 'grad_g_mix_pre', 'grad_g_mix_post', 'grad_w_in', 'grad_w_pool', 'grad_pool_scale', 'grad_w_out', 'grad_g_ffn_pre', 'grad_g_ffn_post', 'grad_w_gate', 'grad_w_up', 'grad_w_down', 'delta_w_cond', 'delta_b_cond', 'delta_g_mix_pre', 'delta_g_mix_post', 'delta_w_in', 'delta_w_pool', 'delta_pool_scale', 'delta_w_out', 'delta_g_ffn_pre', 'delta_g_ffn_post', 'delta_w_gate', 'delta_w_up', 'delta_w_down', 'new_m_w_cond', 'new_m_b_cond', 'new_m_g_mix_pre', 'new_m_g_mix_post', 'new_m_w_in', 'new_m_w_pool', 'new_m_pool_scale', 'new_m_w_out', 'new_m_g_ffn_pre', 'new_m_g_ffn_post', 'new_m_w_gate', 'new_m_w_up', 'new_m_w_down', 'new_v_w_cond', 'new_v_b_cond', 'new_v_g_mix_pre', 'new_v_g_mix_post', 'new_v_w_in', 'new_v_w_pool', 'new_v_pool_scale', 'new_v_w_out', 'new_v_g_ffn_pre', 'new_v_g_ffn_post', 'new_v_w_gate', 'new_v_w_up', 'new_v_w_down']
TWIN_LEAF_KINDS = {'loss': 'loss', 'grad_x': 'grad_x', 'grad_w_cond': 'grad_w', 'grad_b_cond': 'grad_w', 'grad_g_mix_pre': 'grad_w', 'grad_g_mix_post': 'grad_w', 'grad_w_in': 'grad_w', 'grad_w_pool': 'grad_w', 'grad_pool_scale': 'grad_w', 'grad_w_out': 'grad_w', 'grad_g_ffn_pre': 'grad_w', 'grad_g_ffn_post': 'grad_w', 'grad_w_gate': 'grad_w', 'grad_w_up': 'grad_w', 'grad_w_down': 'grad_w', 'delta_w_cond': 'delta_w', 'delta_b_cond': 'delta_w', 'delta_g_mix_pre': 'delta_w', 'delta_g_mix_post': 'delta_w', 'delta_w_in': 'delta_w', 'delta_w_pool': 'delta_w', 'delta_pool_scale': 'delta_w', 'delta_w_out': 'delta_w', 'delta_g_ffn_pre': 'delta_w', 'delta_g_ffn_post': 'delta_w', 'delta_w_gate': 'delta_w', 'delta_w_up': 'delta_w', 'delta_w_down': 'delta_w', 'new_m_w_cond': 'new_m', 'new_m_b_cond': 'new_m', 'new_m_g_mix_pre': 'new_m', 'new_m_g_mix_post': 'new_m', 'new_m_w_in': 'new_m', 'new_m_w_pool': 'new_m', 'new_m_pool_scale': 'new_m', 'new_m_w_out': 'new_m', 'new_m_g_ffn_pre': 'new_m', 'new_m_g_ffn_post': 'new_m', 'new_m_w_gate': 'new_m', 'new_m_w_up': 'new_m', 'new_m_w_down': 'new_m', 'new_v_w_cond': 'new_v', 'new_v_b_cond': 'new_v', 'new_v_g_mix_pre': 'new_v', 'new_v_g_mix_post': 'new_v', 'new_v_w_in': 'new_v', 'new_v_w_pool': 'new_v', 'new_v_pool_scale': 'new_v', 'new_v_w_out': 'new_v', 'new_v_g_ffn_pre': 'new_v', 'new_v_g_ffn_post': 'new_v', 'new_v_w_gate': 'new_v', 'new_v_w_up': 'new_v', 'new_v_w_down': 'new_v'}


def _forward(args):
    return _fwd_reference(*[args[k] for k in FWD_PARAMS])


def _output_shape():
    out = _jax.eval_shape(lambda: _forward(_fwd_setup_inputs(0)))
    return out.shape, out.dtype

N_MICROBATCH = 1
ADAM_LR = 0.001
ADAM_B1 = 0.9
ADAM_B2 = 0.999
ADAM_EPS = 1e-08
ADAM_WD = 0.01
ADAM_STEP = 10
PER_EXAMPLE_BATCH_AXIS = {'x': 0, 'c': 0, 'loss_target': 0}
SHARED_INPUTS = []
_WEIGHT_DTYPES = {'w_cond': _jnp.float32, 'b_cond': _jnp.float32, 'g_mix_pre': _jnp.float32, 'g_mix_post': _jnp.float32, 'w_in': _jnp.float32, 'w_pool': _jnp.float32, 'pool_scale': _jnp.float32, 'w_out': _jnp.float32, 'g_ffn_pre': _jnp.float32, 'g_ffn_post': _jnp.float32, 'w_gate': _jnp.float32, 'w_up': _jnp.float32, 'w_down': _jnp.float32}
MOMENT_SCALE = {'w_cond': 3.777439e+00, 'b_cond': 7.078180e+00, 'g_mix_pre': 3.235633e-01, 'g_mix_post': 1.547640e+01, 'w_in': 1.110121e+00, 'w_pool': 4.001338e-01, 'pool_scale': 4.827422e-01, 'w_out': 1.620795e+00, 'g_ffn_pre': 5.341113e-01, 'g_ffn_post': 1.539685e+01, 'w_gate': 3.843960e-01, 'w_up': 5.430322e-01, 'w_down': 9.299515e-01}


def _to_microbatches(a, axis):
    t = _jnp.moveaxis(a, axis, 0)
    t = t.reshape((N_MICROBATCH, t.shape[0] // N_MICROBATCH) + t.shape[1:])
    return _jnp.moveaxis(t, 1, axis + 1)


def setup_inputs(seed: int = 0) -> dict:
    inp = _fwd_setup_inputs(seed)
    key = _jax.random.fold_in(_jax.random.key(seed), 7919)
    shape, _ = _output_shape()
    out = dict(inp)
    out["loss_target"] = _jax.random.normal(_jax.random.fold_in(key, 0), shape, _jnp.float32)
    for i, name in enumerate(TWIN_WEIGHTS):
        w = inp[name].astype(_jnp.float32)
        if MOMENT_SCALE is None:
            s = _jnp.sqrt(_jnp.mean(_jnp.square(w)) + 1e-30)
        else:
            s = MOMENT_SCALE[name]
        km, kv = _jax.random.split(_jax.random.fold_in(key, i + 1))
        out[name] = w
        out["m_" + name] = s * _jax.random.normal(km, w.shape, _jnp.float32)
        out["v_" + name] = (s * s) * _jax.random.uniform(kv, w.shape, _jnp.float32, 0.5, 1.5)
    if N_MICROBATCH > 1:
        for name, axis in PER_EXAMPLE_BATCH_AXIS.items():
            out[name] = _to_microbatches(out[name], axis)
    return {'x': out['x'], 'c': out['c'], 'w_cond': out['w_cond'], 'b_cond': out['b_cond'], 'g_mix_pre': out['g_mix_pre'], 'g_mix_post': out['g_mix_post'], 'w_in': out['w_in'], 'w_pool': out['w_pool'], 'pool_scale': out['pool_scale'], 'w_out': out['w_out'], 'g_ffn_pre': out['g_ffn_pre'], 'g_ffn_post': out['g_ffn_post'], 'w_gate': out['w_gate'], 'w_up': out['w_up'], 'w_down': out['w_down'], 'loss_target': out['loss_target'], 'm_w_cond': out['m_w_cond'], 'm_b_cond': out['m_b_cond'], 'm_g_mix_pre': out['m_g_mix_pre'], 'm_g_mix_post': out['m_g_mix_post'], 'm_w_in': out['m_w_in'], 'm_w_pool': out['m_w_pool'], 'm_pool_scale': out['m_pool_scale'], 'm_w_out': out['m_w_out'], 'm_g_ffn_pre': out['m_g_ffn_pre'], 'm_g_ffn_post': out['m_g_ffn_post'], 'm_w_gate': out['m_w_gate'], 'm_w_up': out['m_w_up'], 'm_w_down': out['m_w_down'], 'v_w_cond': out['v_w_cond'], 'v_b_cond': out['v_b_cond'], 'v_g_mix_pre': out['v_g_mix_pre'], 'v_g_mix_post': out['v_g_mix_post'], 'v_w_in': out['v_w_in'], 'v_w_pool': out['v_w_pool'], 'v_pool_scale': out['v_pool_scale'], 'v_w_out': out['v_w_out'], 'v_g_ffn_pre': out['v_g_ffn_pre'], 'v_g_ffn_post': out['v_g_ffn_post'], 'v_w_gate': out['v_w_gate'], 'v_w_up': out['v_w_up'], 'v_w_down': out['v_w_down']}


def _loss(weights, diff, rest, loss_target):
    with _jax.named_scope("forward"):
        args = {**rest, TWIN_DIFF_INPUT: diff, **{k: w.astype(_WEIGHT_DTYPES[k]) for k, w in weights.items()}}
        y = _forward(args)
    with _jax.named_scope("loss_head"):
        err = _jnp.square(y.astype(_jnp.float32) - loss_target)
        return 0.5 * _jnp.sum(_jnp.mean(err, axis=-1)) if err.ndim else 0.5 * err


def _adamw(w, g, m, v):
    m = ADAM_B1 * m + (1.0 - ADAM_B1) * g
    v = ADAM_B2 * v + (1.0 - ADAM_B2) * _jnp.square(g)
    m_hat = m / (1.0 - ADAM_B1 ** ADAM_STEP)
    v_hat = v / (1.0 - ADAM_B2 ** ADAM_STEP)
    delta = -ADAM_LR * (m_hat / (_jnp.sqrt(v_hat) + ADAM_EPS) + ADAM_WD * w)
    return delta, m, v


def reference(x, c, w_cond, b_cond, g_mix_pre, g_mix_post, w_in, w_pool, pool_scale, w_out, g_ffn_pre, g_ffn_post, w_gate, w_up, w_down, loss_target, m_w_cond, m_b_cond, m_g_mix_pre, m_g_mix_post, m_w_in, m_w_pool, m_pool_scale, m_w_out, m_g_ffn_pre, m_g_ffn_post, m_w_gate, m_w_up, m_w_down, v_w_cond, v_b_cond, v_g_mix_pre, v_g_mix_post, v_w_in, v_w_pool, v_pool_scale, v_w_out, v_g_ffn_pre, v_g_ffn_post, v_w_gate, v_w_up, v_w_down):
    given = dict(x=x, c=c, w_cond=w_cond, b_cond=b_cond, g_mix_pre=g_mix_pre, g_mix_post=g_mix_post, w_in=w_in, w_pool=w_pool, pool_scale=pool_scale, w_out=w_out, g_ffn_pre=g_ffn_pre, g_ffn_post=g_ffn_post, w_gate=w_gate, w_up=w_up, w_down=w_down, loss_target=loss_target, m_w_cond=m_w_cond, m_b_cond=m_b_cond, m_g_mix_pre=m_g_mix_pre, m_g_mix_post=m_g_mix_post, m_w_in=m_w_in, m_w_pool=m_w_pool, m_pool_scale=m_pool_scale, m_w_out=m_w_out, m_g_ffn_pre=m_g_ffn_pre, m_g_ffn_post=m_g_ffn_post, m_w_gate=m_w_gate, m_w_up=m_w_up, m_w_down=m_w_down, v_w_cond=v_w_cond, v_b_cond=v_b_cond, v_g_mix_pre=v_g_mix_pre, v_g_mix_post=v_g_mix_post, v_w_in=v_w_in, v_w_pool=v_w_pool, v_pool_scale=v_pool_scale, v_w_out=v_w_out, v_g_ffn_pre=v_g_ffn_pre, v_g_ffn_post=v_g_ffn_post, v_w_gate=v_w_gate, v_w_up=v_w_up, v_w_down=v_w_down)
    weights = {n: given[n] for n in TWIN_WEIGHTS}
    shared = {n: given[n] for n in SHARED_INPUTS}
    per_example = {n: given[n] for n in ['x', 'c']}
    grad_fn = _jax.value_and_grad(_loss, argnums=(0, 1))

    def one_microbatch(ex, loss_target):
        ex = dict(ex)
        diff = ex.pop(TWIN_DIFF_INPUT)
        return grad_fn(weights, diff, {**shared, **ex}, loss_target)

    if N_MICROBATCH == 1:
        loss, (grad_w, grad_x) = one_microbatch(per_example, given["loss_target"])
    else:
        def body(carry, xs):
            loss_sum, grad_sum = carry
            l_k, (gw_k, gx_k) = one_microbatch(xs[0], xs[1])
            with _jax.named_scope("update"):
                return (loss_sum + l_k, _jax.tree.map(_jnp.add, grad_sum, gw_k)), gx_k

        init = (_jnp.zeros((), _jnp.float32), _jax.tree.map(_jnp.zeros_like, weights))
        (loss, grad_w), grad_x = _jax.lax.scan(body, init, (per_example, given["loss_target"]))
    with _jax.named_scope("update"):
        delta_w, new_m, new_v = {}, {}, {}
        for n in TWIN_WEIGHTS:
            delta_w[n], new_m[n], new_v[n] = _adamw(weights[n], grad_w[n], given["m_" + n], given["v_" + n])
    return (loss, grad_x, *[grad_w[n] for n in TWIN_WEIGHTS], *[delta_w[n] for n in TWIN_WEIGHTS],
            *[new_m[n] for n in TWIN_WEIGHTS], *[new_v[n] for n in TWIN_WEIGHTS])
```

```python
import functools
import math

import jax
import jax.numpy as jnp
from jax import lax
from jax.experimental import pallas as pl
from jax.experimental.pallas import tpu as pltpu

F32 = jnp.float32
BF16 = jnp.bfloat16
MESH = pl.DeviceIdType.MESH

N_DEV = 8
HEAD_DIM = 64
LANES = 128
POOL_WINDOWS = (2, 4, 8, 16)
POOL_GROUP_DIM = 128
N_MOD = 6
EPS = 1e-6
ATT_TILE = 256
VMEM_LIMIT = 56 * 1024 * 1024

ADAM_LR = 0.001
ADAM_B1 = 0.9
ADAM_B2 = 0.999
ADAM_EPS = 1e-08
ADAM_WD = 0.01
ADAM_STEP = 10


def _params(**kw):
    return pltpu.CompilerParams(vmem_limit_bytes=VMEM_LIMIT, **kw)


def _dot_nn(a, b):
    return jnp.dot(a, b, preferred_element_type=F32)


def _dot_nt(a, b):
    return lax.dot_general(a, b, (((1,), (1,)), ((), ())), preferred_element_type=F32)


def _dot_tn(a, b):
    return lax.dot_general(a, b, (((0,), (0,)), ((), ())), preferred_element_type=F32)


def _mesh_pos():
    return lax.axis_index("x"), lax.axis_index("y"), lax.axis_index("c")


def _all_gather(srcs, out_shapes, dests, name):
    n = len(srcs)

    def body(*refs):
        src = refs[:n]
        outs = refs[n:n + len(out_shapes)]
        send_sems, recv_sems, local_sems = refs[n + len(out_shapes):]
        x, y, c = _mesh_pos()
        me, sibling = (x, y, c), (x, y, 1 - c)
        chips = [(1 - x, y), (x, 1 - y), (1 - x, 1 - y)]

        def slot(i, dev):
            oi, prefix = dests[i]
            px, py, pc = dev
            return outs[oi].at[prefix + (4 * px + 2 * py + pc,)]

        def copy(i, k, block, to, from_src=False):
            return pltpu.make_async_remote_copy(
                src_ref=src[i] if from_src else slot(i, block), dst_ref=slot(i, block),
                send_sem=send_sems.at[i, k], recv_sem=recv_sems.at[i, k],
                device_id=to, device_id_type=MESH)

        mine = [pltpu.make_async_copy(src[i], slot(i, me), local_sems.at[i]) for i in range(n)]
        for cp in mine:
            cp.start()
        first = []
        for i in range(n):
            first.append(copy(i, 0, me, sibling, from_src=True))
            first += [copy(i, 1 + j, me, (*chip, c), from_src=True) for j, chip in enumerate(chips)]
        for cp in first:
            cp.start()
        passed = []
        for j, chip in enumerate(chips):
            for i in range(n):
                copy(i, 1 + j, (*chip, c), me).wait_recv()
                fwd = copy(i, 4 + j, (*chip, c), sibling)
                fwd.start()
                passed.append(fwd)
        for i in range(n):
            copy(i, 0, sibling, me).wait_recv()
            for j, chip in enumerate(chips):
                copy(i, 4 + j, (*chip, 1 - c), me).wait_recv()
        for cp in first + passed:
            cp.wait_send()
        for cp in mine:
            cp.wait()

    any_spec = pl.BlockSpec(memory_space=pl.ANY)
    return pl.pallas_call(
        body, name=name,
        out_shape=tuple(out_shapes),
        in_specs=[any_spec] * n,
        out_specs=tuple([any_spec] * len(out_shapes)),
        scratch_shapes=[pltpu.SemaphoreType.DMA((n, 7)), pltpu.SemaphoreType.DMA((n, 7)),
                        pltpu.SemaphoreType.DMA((n,))],
    )(*srcs)


def _exchange(parts, axis, name):
    n = len(parts)
    counts = [p.shape[0] * p.shape[1] * p.shape[3] for p in parts]
    total = sum(counts)

    def body(*refs):
        src = refs[:n]
        dst = refs[n:2 * n]
        send_sems, recv_sems = refs[2 * n:]
        pos = dict(zip("xyc", _mesh_pos()))
        my = pos[axis]
        partner = tuple(1 - pos[a] if a == axis else pos[a] for a in "xyc")
        copies = []
        k = 0
        for i in range(n):
            m_n, a_n, _, b_n = parts[i].shape[:4]
            for m in range(m_n):
                for a in range(a_n):
                    for b in range(b_n):
                        copies.append(pltpu.make_async_remote_copy(
                            src_ref=src[i].at[m, a, 1 - my, b], dst_ref=dst[i].at[m, a, b],
                            send_sem=send_sems.at[k], recv_sem=recv_sems.at[k],
                            device_id=partner, device_id_type=MESH))
                        k += 1
        for cp in copies:
            cp.start()
        for cp in copies:
            cp.wait_send()
        for cp in copies:
            cp.wait_recv()

    any_spec = pl.BlockSpec(memory_space=pl.ANY)
    outs = tuple(jax.ShapeDtypeStruct(p.shape[:2] + p.shape[3:], p.dtype) for p in parts)
    return pl.pallas_call(
        body, name=name, out_shape=outs,
        in_specs=[any_spec] * n, out_specs=tuple([any_spec] * n),
        scratch_shapes=[pltpu.SemaphoreType.DMA((total,)), pltpu.SemaphoreType.DMA((total,))],
    )(*parts)


def _add_half(part, recv, my, emit_bf16, name):
    m_n, a_n, _, b_n, r, cdim = part.shape

    def body(my_ref, p_ref, r_ref, *outs):
        s = p_ref[...] + r_ref[...].astype(F32)
        outs[0][...] = s
        if emit_bf16:
            outs[1][...] = s.astype(BF16)

    blk5 = (None, None, None, r, cdim)
    out_shape = [jax.ShapeDtypeStruct((m_n, a_n, b_n, r, cdim), F32)]
    out_specs = [pl.BlockSpec(blk5, lambda m, a, b, s: (m, a, b, 0, 0))]
    if emit_bf16:
        out_shape.append(jax.ShapeDtypeStruct((m_n, a_n, b_n, r, cdim), BF16))
        out_specs.append(pl.BlockSpec(blk5, lambda m, a, b, s: (m, a, b, 0, 0)))
    return pl.pallas_call(
        body, name=name, out_shape=tuple(out_shape),
        grid_spec=pltpu.PrefetchScalarGridSpec(
            num_scalar_prefetch=1, grid=(m_n, a_n, b_n),
            in_specs=[pl.BlockSpec((None, None, None, None, r, cdim), lambda m, a, b, s: (m, a, s[0], b, 0, 0)),
                      pl.BlockSpec(blk5, lambda m, a, b, s: (m, a, b, 0, 0))],
            out_specs=tuple(out_specs)),
        compiler_params=_params(),
    )(my, part, recv)


def _reduce_scatter(grads, tag):
    x, y, c = _mesh_pos()
    cur = [g.reshape(g.shape[0], 4, 2, 1, *g.shape[2:]) for g in grads]
    send = cur
    out = None
    for stage, (axis, my, a_n, b_n) in enumerate((("c", c, 1, 2), ("x", x, 1, 1), ("y", y, None, None))):
        recv = _exchange(send, axis, f"rs_{tag}_xchg_{axis}")
        my1 = jnp.reshape(my, (1,)).astype(jnp.int32)
        last = stage == 2
        sums = [_add_half(p, r, my1, not last, f"rs_{tag}_add_{axis}{i}") for i, (p, r) in enumerate(zip(cur, recv))]
        if last:
            out = [s[0].reshape(s[0].shape[0], *s[0].shape[3:]) for s in sums]
        else:
            shp = lambda t: t.reshape(t.shape[0], a_n, 2, b_n, *t.shape[3:])
            cur = [shp(s[0]) for s in sums]
            send = [shp(s[1]) for s in sums]
    return out


def _matmul(a, b, mode, out_dtype, tm, tn, tk, name):
    ga = a.shape[0] if a.ndim == 3 else None
    gb = b.shape[0] if b.ndim == 3 else None
    a2, b2 = a.shape[-2:], b.shape[-2:]
    if mode == "nn":
        (m, k), n = a2, b2[1]
    elif mode == "nt":
        (m, k), n = a2, b2[0]
    else:
        (k, m), n = a2, b2[1]
    assert m % tm == 0 and n % tn == 0 and k % tk == 0, (name, m, n, k)
    nk = k // tk
    g_n = ga or 1
    batch_out = mode == "tn" and ga is not None
    n_red = nk if batch_out else nk * g_n
    dot = {"nn": _dot_nn, "nt": _dot_nt, "tn": _dot_tn}[mode]
    acc_in_out = out_dtype == F32

    def body(a_ref, b_ref, o_ref, *scratch):
        p = dot(a_ref[...], b_ref[...])
        if n_red == 1:
            o_ref[...] = p.astype(out_dtype)
            return
        acc = o_ref if acc_in_out else scratch[0]
        kk = pl.program_id(3) if batch_out else pl.program_id(2) * nk + pl.program_id(3)

        @pl.when(kk == 0)
        def _():
            acc[...] = p

        @pl.when(kk > 0)
        def _():
            acc[...] += p

        if not acc_in_out:
            @pl.when(kk == n_red - 1)
            def _():
                o_ref[...] = acc[...].astype(out_dtype)

    def order(ids):
        return ids if batch_out else (ids[2], ids[0], ids[1], ids[3])

    def a_idx(*ids):
        g, i, j, kq = order(ids)
        blk = {"nn": (i, kq), "nt": (i, kq), "tn": (kq, i)}[mode]
        return (g,) + blk if ga is not None else blk

    def b_idx(*ids):
        g, i, j, kq = order(ids)
        blk = {"nn": (kq, j), "nt": (j, kq), "tn": (kq, j)}[mode]
        return (g,) + blk if gb is not None else blk

    def o_idx(*ids):
        g, i, j, kq = order(ids)
        return (g, i, j) if batch_out else (i, j)

    a_blk = {"nn": (tm, tk), "nt": (tm, tk), "tn": (tk, tm)}[mode]
    b_blk = {"nn": (tk, tn), "nt": (tn, tk), "tn": (tk, tn)}[mode]
    if ga is not None:
        a_blk = (None,) + a_blk
    if gb is not None:
        b_blk = (None,) + b_blk
    if batch_out:
        out_shape = jax.ShapeDtypeStruct((g_n, m, n), out_dtype)
        o_blk = (None, tm, tn)
        grid = (g_n, m // tm, n // tn, nk)
    else:
        out_shape = jax.ShapeDtypeStruct((m, n), out_dtype)
        o_blk = (tm, tn)
        grid = (m // tm, n // tn, g_n, nk)
    scratch = [] if (acc_in_out or n_red == 1) else [pltpu.VMEM((tm, tn), F32)]
    return pl.pallas_call(
        body, name=name, out_shape=out_shape, grid=grid,
        in_specs=[pl.BlockSpec(a_blk, a_idx), pl.BlockSpec(b_blk, b_idx)],
        out_specs=pl.BlockSpec(o_blk, o_idx),
        scratch_shapes=scratch, compiler_params=_params(),
    )(a, b)


EW_TILE = 256


def _rms(v):
    return lax.rsqrt(jnp.mean(v * v, axis=-1, keepdims=True) + EPS)


def _rms_bwd(dhat, vh, r):
    return r * (dhat - vh * jnp.mean(dhat * vh, axis=-1, keepdims=True))


def _tok_spec(tm, d):
    return pl.BlockSpec((tm, d), lambda i: (i, 0))


def _vec_spec(d):
    return pl.BlockSpec((1, d), lambda i: (0, 0))


def _mod_spec(tiles_per_seq, d):
    return pl.BlockSpec((None, N_MOD, d), lambda i: (i // tiles_per_seq, 0, 0))


def _seq_acc_spec(tiles_per_seq, d):
    return pl.BlockSpec((None, 1, d), lambda i: (i // tiles_per_seq, 0, 0))


def _acc(ref, val, first):
    @pl.when(first)
    def _():
        ref[...] = val

    @pl.when(jnp.logical_not(first))
    def _():
        ref[...] += val


def _colsum(v):
    return jnp.sum(v, axis=0, keepdims=True)


def _pre_mix(x2, g_pre, mod, seq):
    t, d = x2.shape
    tm = EW_TILE

    def body(x_ref, g_ref, mod_ref, h_ref):
        xv = x_ref[...]
        n = xv * _rms(xv) * g_ref[...]
        h_ref[...] = (n * (1.0 + mod_ref[1:2, :]) + mod_ref[0:1, :]).astype(BF16)

    return pl.pallas_call(
        body, name="pre_mix", out_shape=jax.ShapeDtypeStruct((t, d), BF16), grid=(t // tm,),
        in_specs=[_tok_spec(tm, d), _vec_spec(d), _mod_spec(seq // tm, d)],
        out_specs=_tok_spec(tm, d), compiler_params=_params(),
    )(x2, g_pre, mod)


def _mid(mix, x2, g_post, g_pre, mod, seq):
    t, d = x2.shape
    tm = EW_TILE

    def body(mix_ref, x_ref, gpost_ref, gpre_ref, mod_ref, x1_ref, h2_ref):
        mv = mix_ref[...]
        x1 = x_ref[...] + mod_ref[2:3, :] * (mv * _rms(mv) * gpost_ref[...])
        x1_ref[...] = x1
        n = x1 * _rms(x1) * gpre_ref[...]
        h2_ref[...] = (n * (1.0 + mod_ref[4:5, :]) + mod_ref[3:4, :]).astype(BF16)

    return pl.pallas_call(
        body, name="mid", grid=(t // tm,),
        out_shape=(jax.ShapeDtypeStruct((t, d), F32), jax.ShapeDtypeStruct((t, d), BF16)),
        in_specs=[_tok_spec(tm, d), _tok_spec(tm, d), _vec_spec(d), _vec_spec(d), _mod_spec(seq // tm, d)],
        out_specs=(_tok_spec(tm, d), _tok_spec(tm, d)), compiler_params=_params(),
    )(mix, x2, g_post, g_pre, mod)


def _post(f, x1, target, g_post, mod, seq):
    t, d = x1.shape
    tm = EW_TILE
    tps = seq // tm

    def body(f_ref, x1_ref, tgt_ref, g_ref, mod_ref, loss_ref, dy_ref, df_ref, dgate_ref, gg_ref):
        i = pl.program_id(0)
        fv = f_ref[...]
        r = _rms(fv)
        fh = fv * r
        nf = fh * g_ref[...]
        gate = mod_ref[5:6, :]
        err = x1_ref[...] + gate * nf - tgt_ref[...]
        _acc(loss_ref, jnp.sum(_colsum(err * err), axis=1, keepdims=True) * jnp.ones((1, LANES), F32), i == 0)
        dy = err * (1.0 / d)
        dy_ref[...] = dy
        _acc(dgate_ref, _colsum(dy * nf), i % tps == 0)
        dn = dy * gate
        _acc(gg_ref, _colsum(dn * fh), i == 0)
        df_ref[...] = _rms_bwd(dn * g_ref[...], fh, r).astype(BF16)

    n_seq = t // seq
    return pl.pallas_call(
        body, name="post", grid=(t // tm,),
        out_shape=(jax.ShapeDtypeStruct((1, LANES), F32), jax.ShapeDtypeStruct((t, d), F32),
                   jax.ShapeDtypeStruct((t, d), BF16), jax.ShapeDtypeStruct((n_seq, 1, d), F32),
                   jax.ShapeDtypeStruct((1, d), F32)),
        in_specs=[_tok_spec(tm, d), _tok_spec(tm, d), _tok_spec(tm, d), _vec_spec(d), _mod_spec(tps, d)],
        out_specs=(pl.BlockSpec((1, LANES), lambda i: (0, 0)), _tok_spec(tm, d), _tok_spec(tm, d),
                   _seq_acc_spec(tps, d), _vec_spec(d)),
        compiler_params=_params(),
    )(f, x1, target, g_post, mod)


def _bwd_mid(dh2, dy, x1, mix, g_pre, g_post, mod, seq):
    t, d = x1.shape
    tm = EW_TILE
    tps = seq // tm

    def body(dh2_ref, dy_ref, x1_ref, mix_ref, gpre_ref, gpost_ref, mod_ref,
             dx1_ref, dmix_ref, dshift_ref, dscale_ref, dgate_ref, ggpre_ref, ggpost_ref):
        i = pl.program_id(0)
        seq_first = i % tps == 0
        dh = dh2_ref[...]
        x1 = x1_ref[...]
        r = _rms(x1)
        xh = x1 * r
        gpre = gpre_ref[...]
        _acc(dshift_ref, _colsum(dh), seq_first)
        _acc(dscale_ref, _colsum(dh * xh * gpre), seq_first)
        dn = dh * (1.0 + mod_ref[4:5, :])
        _acc(ggpre_ref, _colsum(dn * xh), i == 0)
        dx1 = dy_ref[...] + _rms_bwd(dn * gpre, xh, r)
        dx1_ref[...] = dx1
        mv = mix_ref[...]
        rm = _rms(mv)
        mh = mv * rm
        gpost = gpost_ref[...]
        _acc(dgate_ref, _colsum(dx1 * mh * gpost), seq_first)
        dnm = dx1 * mod_ref[2:3, :]
        _acc(ggpost_ref, _colsum(dnm * mh), i == 0)
        dmix_ref[...] = _rms_bwd(dnm * gpost, mh, rm).astype(BF16)

    n_seq = t // seq
    seq_sds = jax.ShapeDtypeStruct((n_seq, 1, d), F32)
    vec_sds = jax.ShapeDtypeStruct((1, d), F32)
    return pl.pallas_call(
        body, name="bwd_mid", grid=(t // tm,),
        out_shape=(jax.ShapeDtypeStruct((t, d), F32), jax.ShapeDtypeStruct((t, d), BF16),
                   seq_sds, seq_sds, seq_sds, vec_sds, vec_sds),
        in_specs=[_tok_spec(tm, d)] * 4 + [_vec_spec(d), _vec_spec(d), _mod_spec(tps, d)],
        out_specs=(_tok_spec(tm, d), _tok_spec(tm, d), _seq_acc_spec(tps, d), _seq_acc_spec(tps, d),
                   _seq_acc_spec(tps, d), _vec_spec(d), _vec_spec(d)),
        compiler_params=_params(),
    )(dh2, dy, x1, mix, g_pre, g_post, mod)


def _bwd_pre(dh1, dx1, x2, g_pre, mod, seq):
    t, d = x2.shape
    tm = EW_TILE
    tps = seq // tm

    def body(dh_ref, dx1_ref, x_ref, g_ref, mod_ref, gx_ref, dshift_ref, dscale_ref, gg_ref):
        i = pl.program_id(0)
        seq_first = i % tps == 0
        dh = dh_ref[...]
        xv = x_ref[...]
        r = _rms(xv)
        xh = xv * r
        g = g_ref[...]
        _acc(dshift_ref, _colsum(dh), seq_first)
        _acc(dscale_ref, _colsum(dh * xh * g), seq_first)
        dn = dh * (1.0 + mod_ref[1:2, :])
        _acc(gg_ref, _colsum(dn * xh), i == 0)
        gx_ref[...] = dx1_ref[...] + _rms_bwd(dn * g, xh, r)

    n_seq = t // seq
    seq_sds = jax.ShapeDtypeStruct((n_seq, 1, d), F32)
    return pl.pallas_call(
        body, name="bwd_pre", grid=(t // tm,),
        out_shape=(jax.ShapeDtypeStruct((t, d), F32), seq_sds, seq_sds, jax.ShapeDtypeStruct((1, d), F32)),
        in_specs=[_tok_spec(tm, d)] * 3 + [_vec_spec(d), _mod_spec(tps, d)],
        out_specs=(_tok_spec(tm, d), _seq_acc_spec(tps, d), _seq_acc_spec(tps, d), _vec_spec(d)),
        compiler_params=_params(),
    )(dh1, dx1, x2, g_pre, mod)


def _ffn_up(h2, wgu, tm, tn):
    t, d = h2.shape
    f = wgu.shape[1]

    def body(h_ref, w_ref, gu_ref, act_ref):
        h = h_ref[...]
        g = _dot_nt(h, w_ref[0])
        u = _dot_nt(h, w_ref[1])
        gu_ref[0] = g.astype(BF16)
        gu_ref[1] = u.astype(BF16)
        act_ref[...] = (g * jax.nn.sigmoid(g) * u).astype(BF16)

    return pl.pallas_call(
        body, name="ffn_up", grid=(t // tm, f // tn),
        out_shape=(jax.ShapeDtypeStruct((2, t, f), BF16), jax.ShapeDtypeStruct((t, f), BF16)),
        in_specs=[pl.BlockSpec((tm, d), lambda i, j: (i, 0)), pl.BlockSpec((2, tn, d), lambda i, j: (0, j, 0))],
        out_specs=(pl.BlockSpec((2, tm, tn), lambda i, j: (0, i, j)), pl.BlockSpec((tm, tn), lambda i, j: (i, j))),
        compiler_params=_params(),
    )(h2, wgu)


def _ffn_act_bwd(df, wd, gu, tm, tn):
    t, d = df.shape
    f = wd.shape[0]

    def body(df_ref, w_ref, gu_ref, dgu_ref):
        da = _dot_nt(df_ref[...], w_ref[...])
        g = gu_ref[0].astype(F32)
        u = gu_ref[1].astype(F32)
        s = jax.nn.sigmoid(g)
        silu = g * s
        dgu_ref[0] = (da * u * (s + silu * (1.0 - s))).astype(BF16)
        dgu_ref[1] = (da * silu).astype(BF16)

    return pl.pallas_call(
        body, name="ffn_act_bwd", grid=(t // tm, f // tn),
        out_shape=jax.ShapeDtypeStruct((2, t, f), BF16),
        in_specs=[pl.BlockSpec((tm, d), lambda i, j: (i, 0)), pl.BlockSpec((tn, d), lambda i, j: (j, 0)),
                  pl.BlockSpec((2, tm, tn), lambda i, j: (0, i, j))],
        out_specs=pl.BlockSpec((2, tm, tn), lambda i, j: (0, i, j)),
        compiler_params=_params(),
    )(df, wd, gu)


def _sb_tile(qm, k2, mask, tri):
    z = _dot_nt(qm, k2) * (1.0 / math.sqrt(HEAD_DIM))
    e = jnp.exp(-jnp.abs(z))
    l = jnp.log(1.0 + e)
    log1m = -(jnp.maximum(z, 0.0) + l)
    if mask is not None:
        log1m = jnp.where(mask, log1m, 0.0)
    logsig = jnp.minimum(z, 0.0) - l
    hi = log1m.astype(BF16)
    lo = (log1m - hi.astype(F32)).astype(BF16)
    after = _dot_nn(hi, tri) + _dot_nn(lo, tri)
    return z, e, log1m, logsig, after


def _attn_fwd(proj, tri_after, n_seq, seq):
    t = proj.shape[0]
    tq = ATT_TILE
    n_pair = (proj.shape[1] // 4) // LANES

    def body(q_ref, k_ref, v_ref, tri_ref, o_ref, cs_ref, oacc, cmat, carry):
        lane = lax.broadcasted_iota(jnp.int32, (1, LANES), 1)
        tri = tri_ref[...]
        row = lax.broadcasted_iota(jnp.int32, (tq, tq), 0)
        col = lax.broadcasted_iota(jnp.int32, (tq, tq), 1)
        diag = col < row

        def q_tile(qi, _):
            r0 = pl.multiple_of(qi * tq, tq)
            q2 = q_ref[pl.ds(r0, tq), :]
            for hh in range(2):
                head = (lane < HEAD_DIM) if hh == 0 else (lane >= HEAD_DIM)
                qm = jnp.where(head, q2, jnp.zeros_like(q2))
                carry[...] = jnp.zeros_like(carry)
                cmat[...] = jnp.zeros_like(cmat)
                oacc[hh] = jnp.zeros((tq, LANES), F32)

                def tile(kb, mask, qm=qm, hh=hh):
                    c0 = pl.multiple_of(kb * tq, tq)
                    k2 = k_ref[pl.ds(c0, tq), :]
                    v2 = v_ref[pl.ds(c0, tq), :]
                    _, _, log1m, logsig, after = _sb_tile(qm, k2, mask, tri)
                    cur = carry[...]
                    w = jnp.exp(logsig + after + cur)
                    if mask is not None:
                        w = jnp.where(mask, w, 0.0)
                    oacc[hh] += _dot_nn(w.astype(BF16), v2)
                    cmat[...] = jnp.where(lane == kb, cur, cmat[...])
                    carry[...] = cur + jnp.sum(log1m, axis=1, keepdims=True)

                tile(qi, diag)

                def off_diag(j, _, tile=tile):
                    tile(qi - 1 - j, None)
                    return 0

                lax.fori_loop(0, qi, off_diag, 0)
                cs_ref[pl.ds(r0, tq), hh * LANES:(hh + 1) * LANES] = cmat[...]
            o_ref[pl.ds(r0, tq), :] = jnp.where(lane < HEAD_DIM, oacc[0], oacc[1]).astype(BF16)
            return 0

        lax.fori_loop(0, seq // tq, q_tile, 0)

    blk = lambda off: pl.BlockSpec((seq, LANES), lambda b, p: (b, off + p))
    return pl.pallas_call(
        body, name="attn_fwd", grid=(n_seq, n_pair),
        out_shape=(jax.ShapeDtypeStruct((t, n_pair * LANES), BF16),
                   jax.ShapeDtypeStruct((t, n_pair * 2 * LANES), F32)),
        in_specs=[blk(0), blk(n_pair), blk(2 * n_pair), pl.BlockSpec((tq, tq), lambda b, p: (0, 0))],
        out_specs=(pl.BlockSpec((seq, LANES), lambda b, p: (b, p)),
                   pl.BlockSpec((seq, 2 * LANES), lambda b, p: (b, p))),
        scratch_shapes=[pltpu.VMEM((2, tq, LANES), F32), pltpu.VMEM((tq, LANES), F32), pltpu.VMEM((tq, 1), F32)],
        compiler_params=_params(),
    )(proj, proj, proj, tri_after)


def _attn_bwd(proj, dcat, cstats, tri_after, tri_incl, n_seq, seq):
    t = proj.shape[0]
    tq = ATT_TILE
    width = proj.shape[1] // 4
    n_pair = width // LANES
    inv = 1.0 / math.sqrt(HEAD_DIM)

    def body(q_ref, k_ref, v_ref, do_ref, cs_ref, tria_ref, trii_ref, out_ref, dq_acc, dk_acc, dv_acc, ecarry):
        lane = lax.broadcasted_iota(jnp.int32, (1, LANES), 1)
        tri_a = tria_ref[...]
        tri_i = trii_ref[...]
        row = lax.broadcasted_iota(jnp.int32, (tq, tq), 0)
        col = lax.broadcasted_iota(jnp.int32, (tq, tq), 1)
        diag = col < row
        dq_acc[...] = jnp.zeros_like(dq_acc)
        dk_acc[...] = jnp.zeros_like(dk_acc)
        dv_acc[...] = jnp.zeros_like(dv_acc)

        def q_tile(qi, _):
            r0 = pl.multiple_of(qi * tq, tq)
            q2 = q_ref[pl.ds(r0, tq), :]
            do2 = do_ref[pl.ds(r0, tq), :]
            for hh in range(2):
                head = (lane < HEAD_DIM) if hh == 0 else (lane >= HEAD_DIM)
                qm = jnp.where(head, q2, jnp.zeros_like(q2))
                dom = jnp.where(head, do2, jnp.zeros_like(do2))
                cs = cs_ref[pl.ds(r0, tq), hh * LANES:(hh + 1) * LANES]
                ecarry[...] = jnp.zeros_like(ecarry)

                def tile(kb, mask, qm=qm, dom=dom, cs=cs, head=head):
                    c0 = pl.multiple_of(kb * tq, tq)
                    k2 = k_ref[pl.ds(c0, tq), :]
                    v2 = v_ref[pl.ds(c0, tq), :]
                    km = jnp.where(head, k2, jnp.zeros_like(k2))
                    z, e, _, logsig, after = _sb_tile(qm, k2, mask, tri_a)
                    cur = jnp.sum(jnp.where(lane == kb, cs, 0.0), axis=1, keepdims=True)
                    w = jnp.exp(logsig + after + cur)
                    if mask is not None:
                        w = jnp.where(mask, w, 0.0)
                    ee = w * _dot_nt(dom, v2)
                    hi = ee.astype(BF16)
                    lo = (ee - hi.astype(F32)).astype(BF16)
                    einc = _dot_nn(hi, tri_i) + _dot_nn(lo, tri_i) + ecarry[...]
                    sig = jnp.where(z >= 0.0, 1.0, e) / (1.0 + e)
                    dz = (ee - sig * einc) * inv
                    if mask is not None:
                        dz = jnp.where(mask, dz, 0.0)
                    dzb = dz.astype(BF16)
                    dq_acc[pl.ds(r0, tq), :] += _dot_nn(dzb, km)
                    dk_acc[pl.ds(c0, tq), :] += _dot_tn(dzb, qm)
                    dv_acc[pl.ds(c0, tq), :] += _dot_tn(w.astype(BF16), dom)
                    ecarry[...] += jnp.sum(ee, axis=1, keepdims=True)

                def off_diag(kb, _, tile=tile):
                    tile(kb, None)
                    return 0

                lax.fori_loop(0, qi, off_diag, 0)
                tile(qi, diag)
            return 0

        lax.fori_loop(0, seq // tq, q_tile, 0)
        out_ref[0] = dq_acc[...].astype(BF16)
        out_ref[1] = dk_acc[...].astype(BF16)
        out_ref[2] = dv_acc[...].astype(BF16)

    blk = lambda off: pl.BlockSpec((seq, LANES), lambda b, p: (b, off + p))
    tri_spec = pl.BlockSpec((tq, tq), lambda b, p: (0, 0))
    return pl.pallas_call(
        body, name="attn_bwd", grid=(n_seq, n_pair),
        out_shape=jax.ShapeDtypeStruct((4, t, width), BF16),
        in_specs=[blk(0), blk(n_pair), blk(2 * n_pair), pl.BlockSpec((seq, LANES), lambda b, p: (b, p)),
                  pl.BlockSpec((seq, 2 * LANES), lambda b, p: (b, p)), tri_spec, tri_spec],
        out_specs=pl.BlockSpec((3, seq, LANES), lambda b, p: (0, b, p)),
        scratch_shapes=[pltpu.VMEM((seq, LANES), F32)] * 3 + [pltpu.VMEM((tq, 1), F32)],
        compiler_params=_params(),
    )(proj, proj, proj, dcat, cstats, tri_after, tri_incl)


def _window_terms(g, rows):
    win = jnp.where(g == 0, POOL_WINDOWS[0], jnp.where(g == 1, POOL_WINDOWS[1],
                    jnp.where(g == 2, POOL_WINDOWS[2], POOL_WINDOWS[3])))
    cnt = jnp.minimum(rows + 1, win).astype(F32)
    return win, cnt


def _window_sum(v, g, rows, forward):
    s_len = v.shape[0]
    sums = []
    s = v
    for step in range(len(POOL_WINDOWS)):
        sh = 1 << step
        if forward:
            shifted = jnp.where(rows < s_len - sh, pltpu.roll(s, s_len - sh, axis=0), 0.0)
        else:
            shifted = jnp.where(rows >= sh, pltpu.roll(s, sh, axis=0), 0.0)
        s = s + shifted
        sums.append(s)
    return jnp.where(g == 0, sums[0], jnp.where(g == 1, sums[1], jnp.where(g == 2, sums[2], sums[3])))


def _pooled(u, g, rows):
    _, cnt = _window_terms(g, rows)
    return _window_sum(u, g, rows, forward=False) / cnt - u


def _pool_fwd(proj, w_pool, pool_scale, n_seq, seq):
    t = proj.shape[0]
    n_grp = len(POOL_WINDOWS)
    u_off = 3 * (proj.shape[1] // 4) // LANES

    def body(u_ref, w_ref, s_ref, o_ref):
        g = pl.program_id(1)
        rows = lax.broadcasted_iota(jnp.int32, (seq, 1), 0)
        pooled = _pooled(u_ref[...].astype(F32), g, rows)
        y = _dot_nn(pooled.astype(BF16), w_ref[...].astype(BF16))
        o_ref[...] = (y * s_ref[...]).astype(BF16)

    return pl.pallas_call(
        body, name="pool_fwd", grid=(n_seq, n_grp),
        out_shape=jax.ShapeDtypeStruct((t, n_grp * POOL_GROUP_DIM), BF16),
        in_specs=[pl.BlockSpec((seq, LANES), lambda b, g: (b, u_off + g)),
                  pl.BlockSpec((None, POOL_GROUP_DIM, POOL_GROUP_DIM), lambda b, g: (g, 0, 0)),
                  pl.BlockSpec((1, POOL_GROUP_DIM), lambda b, g: (0, g))],
        out_specs=pl.BlockSpec((seq, LANES), lambda b, g: (b, g)),
        compiler_params=_params(),
    )(proj, w_pool, pool_scale)


def _pool_bwd(proj, dcat, w_pool, pool_scale, dqkv, n_seq, seq):
    n_grp = len(POOL_WINDOWS)
    width = proj.shape[1] // 4
    u_off = 3 * width // LANES
    dp_off = width // LANES

    def body(u_ref, dp_ref, w_ref, s_ref, alias_ref, du_ref, gw_ref, gs_ref):
        del alias_ref
        g = pl.program_id(0)
        b = pl.program_id(1)
        rows = lax.broadcasted_iota(jnp.int32, (seq, 1), 0)
        pooled = _pooled(u_ref[...].astype(F32), g, rows)
        pb = pooled.astype(BF16)
        wb = w_ref[...].astype(BF16)
        z = _dot_nn(pb, wb)
        dp = dp_ref[...].astype(F32)
        _acc(gs_ref, _colsum(dp * z), b == 0)
        dys = (dp * s_ref[...]).astype(BF16)
        _acc(gw_ref, _dot_tn(pb, dys), b == 0)
        dpooled = _dot_nt(dys, wb)
        _, cnt = _window_terms(g, rows)
        du = _window_sum(dpooled / cnt, g, rows, forward=True) - dpooled
        du_ref[...] = du.astype(BF16)

    t = proj.shape[0]
    return pl.pallas_call(
        body, name="pool_bwd", grid=(n_grp, n_seq),
        out_shape=(jax.ShapeDtypeStruct(dqkv.shape, BF16),
                   jax.ShapeDtypeStruct((n_grp, POOL_GROUP_DIM, POOL_GROUP_DIM), F32),
                   jax.ShapeDtypeStruct((1, n_grp * POOL_GROUP_DIM), F32)),
        in_specs=[pl.BlockSpec((seq, LANES), lambda g, b: (b, u_off + g)),
                  pl.BlockSpec((seq, LANES), lambda g, b: (b, dp_off + g)),
                  pl.BlockSpec((None, POOL_GROUP_DIM, POOL_GROUP_DIM), lambda g, b: (g, 0, 0)),
                  pl.BlockSpec((1, POOL_GROUP_DIM), lambda g, b: (0, g)),
                  pl.BlockSpec(memory_space=pl.ANY)],
        out_specs=(pl.BlockSpec((None, seq, LANES), lambda g, b: (3, b, g)),
                   pl.BlockSpec((None, POOL_GROUP_DIM, POOL_GROUP_DIM), lambda g, b: (g, 0, 0)),
                   pl.BlockSpec((1, POOL_GROUP_DIM), lambda g, b: (0, g))),
        input_output_aliases={4: 0},
        compiler_params=_params(),
    )(proj, dcat, w_pool, pool_scale, dqkv)


def _cond_fwd(c_all, w_cond, b_cols):
    n, _ = c_all.shape
    cols = w_cond.shape[1]

    def body(c_ref, w_ref, b_ref, o_ref):
        cv = c_ref[...]
        a = cv * jax.nn.sigmoid(cv)
        o_ref[...] = jnp.dot(a, w_ref[...], preferred_element_type=F32,
                             precision=lax.Precision.HIGHEST) + b_ref[...]

    return pl.pallas_call(
        body, name="cond_fwd", out_shape=jax.ShapeDtypeStruct((n, cols), F32),
        compiler_params=_params(),
    )(c_all, w_cond, b_cols)


def _cond_bwd(c_all, dmod_all, dmod_cols):
    n, d = c_all.shape
    cols = dmod_cols.shape[1]

    def body(c_ref, dm_ref, dmc_ref, gw_ref, gb_ref):
        cv = c_ref[...]
        a = cv * jax.nn.sigmoid(cv)
        gw_ref[...] = lax.dot_general(a, dmc_ref[...], (((0,), (0,)), ((), ())),
                                      preferred_element_type=F32, precision=lax.Precision.HIGHEST)
        gb_ref[...] = _colsum(dm_ref[...])

    return pl.pallas_call(
        body, name="cond_bwd",
        out_shape=(jax.ShapeDtypeStruct((d, cols), F32), jax.ShapeDtypeStruct((1, dmod_all.shape[1]), F32)),
        compiler_params=_params(),
    )(c_all, dmod_all, dmod_cols)


def _adamw_math(w, g, m, v):
    m = ADAM_B1 * m + (1.0 - ADAM_B1) * g
    v = ADAM_B2 * v + (1.0 - ADAM_B2) * (g * g)
    m_hat = m / (1.0 - ADAM_B1 ** ADAM_STEP)
    v_hat = v / (1.0 - ADAM_B2 ** ADAM_STEP)
    delta = -ADAM_LR * (m_hat / (jnp.sqrt(v_hat) + ADAM_EPS) + ADAM_WD * w)
    return delta, m, v


def _adamw(w, g, m, v, rows, name):
    r, cdim = w.shape

    def body(w_ref, g_ref, m_ref, v_ref, d_ref, nm_ref, nv_ref):
        d_ref[...], nm_ref[...], nv_ref[...] = _adamw_math(w_ref[...], g_ref[...], m_ref[...], v_ref[...])

    spec = pl.BlockSpec((rows, cdim), lambda i: (i, 0))
    sds = jax.ShapeDtypeStruct((r, cdim), F32)
    return pl.pallas_call(
        body, name=name, grid=(r // rows,), out_shape=(sds, sds, sds),
        in_specs=[spec] * 4, out_specs=(spec, spec, spec), compiler_params=_params(),
    )(w, g, m, v)


def _adamw_small(ws, gparts, ms, vs, name):
    n = len(ws)

    def body(*refs):
        w_r, g_r, m_r, v_r = refs[:n], refs[n:2 * n], refs[2 * n:3 * n], refs[3 * n:4 * n]
        outs = refs[4 * n:]
        for i in range(n):
            g = g_r[i][0]
            for dev in range(1, g_r[i].shape[0]):
                g = g + g_r[i][dev]
            delta, m, v = _adamw_math(w_r[i][...], g, m_r[i][...], v_r[i][...])
            outs[i][...] = g
            outs[n + i][...] = delta
            outs[2 * n + i][...] = m
            outs[3 * n + i][...] = v

    sds = [jax.ShapeDtypeStruct(w.shape, F32) for w in ws]
    return pl.pallas_call(
        body, name=name, out_shape=tuple(sds * 4), compiler_params=_params(),
    )(*ws, *gparts, *ms, *vs)


def kernel(x, c, w_cond, b_cond, g_mix_pre, g_mix_post, w_in, w_pool, pool_scale, w_out, g_ffn_pre, g_ffn_post, w_gate, w_up, w_down, loss_target, m_w_cond, m_b_cond, m_g_mix_pre, m_g_mix_post, m_w_in, m_w_pool, m_pool_scale, m_w_out, m_g_ffn_pre, m_g_ffn_post, m_w_gate, m_w_up, m_w_down, v_w_cond, v_b_cond, v_g_mix_pre, v_g_mix_post, v_w_in, v_w_pool, v_pool_scale, v_w_out, v_g_ffn_pre, v_g_ffn_post, v_w_gate, v_w_up, v_w_down):
    n_seq, seq, d = x.shape
    t = n_seq * seq
    xi, yi, ci = _mesh_pos()
    me = 4 * xi + 2 * yi + ci
    x2 = x.reshape(t, d)
    tgt2 = loss_target.reshape(t, d)
    in_rows = w_in.shape[2]
    out_rows = w_out.shape[1]
    ff_rows = w_gate.shape[2]
    ff = N_DEV * ff_rows
    cond_cols = w_cond.shape[2]

    win_t = w_in[0].T.astype(BF16)
    wout_s = w_out[0].astype(BF16)
    wg_t = w_gate[0].T.astype(BF16)
    wu_t = w_up[0].T.astype(BF16)
    wd_s = w_down[0].astype(BF16)
    c_all, win_g, wout_g, wgu_g, wd_g = _all_gather(
        [c, win_t, wout_s, wg_t, wu_t, wd_s],
        [jax.ShapeDtypeStruct((N_DEV, n_seq, d), F32), jax.ShapeDtypeStruct((N_DEV, in_rows, d), BF16),
         jax.ShapeDtypeStruct((N_DEV, out_rows, d), BF16), jax.ShapeDtypeStruct((2, N_DEV, ff_rows, d), BF16),
         jax.ShapeDtypeStruct((N_DEV, ff_rows, d), BF16)],
        [(0, ()), (1, ()), (2, ()), (3, (0,)), (3, (1,)), (4, ())], "ag_weights")
    c_all = c_all.reshape(N_DEV * n_seq, d)
    win_full = win_g.reshape(N_DEV * in_rows, d)
    wout_full = wout_g.reshape(N_DEV * out_rows, d)
    wgu_full = wgu_g.reshape(2, ff, d)
    wd_full = wd_g.reshape(ff, d)

    b_cols = lax.dynamic_slice_in_dim(b_cond, me * cond_cols, cond_cols, axis=1)
    mod_cols = _cond_fwd(c_all, w_cond[0], b_cols)
    (mod_g,) = _all_gather([mod_cols], [jax.ShapeDtypeStruct((N_DEV,) + mod_cols.shape, F32)], [(0, ())], "ag_mod")
    mod_mine = lax.dynamic_slice_in_dim(mod_g, me * n_seq, n_seq, axis=1)
    mod = jnp.transpose(mod_mine, (1, 0, 2)).reshape(n_seq, N_MOD, d)

    h1 = _pre_mix(x2, g_mix_pre, mod, seq)
    proj = _matmul(h1, win_full, "nt", BF16, 1024, 512, d, "proj")
    tq = ATT_TILE
    ids = jnp.arange(tq)
    tri_after = (ids[:, None] > ids[None, :]).astype(BF16)
    tri_incl = (ids[:, None] <= ids[None, :]).astype(BF16)
    attn, cstats = _attn_fwd(proj, tri_after, n_seq, seq)
    pool = _pool_fwd(proj, w_pool[0], pool_scale, n_seq, seq)
    cat = jnp.stack([attn, pool])
    mix = _matmul(cat, wout_full.reshape(2, d // 2, d), "nn", F32, 512, d, d // 2, "mix")
    x1, h2 = _mid(mix, x2, g_mix_post, g_ffn_pre, mod, seq)
    gu, act = _ffn_up(h2, wgu_full, 512, ff // 2)
    f = _matmul(act, wd_full, "nn", F32, 512, d, ff, "ffn_down")
    loss_sum, dy, df, dgate_f, gg_ffn_post = _post(f, x1, tgt2, g_ffn_post, mod, seq)
    loss = lax.psum(loss_sum[0, 0] * (0.5 / d), ("x", "y", "c"))

    dgu = _ffn_act_bwd(df, wd_full, gu, 512, ff // 2)
    gwd = _matmul(act, df, "tn", F32, ff // 2, d, 1024, "grad_w_down")
    gwgu = _matmul(dgu, h2, "tn", F32, ff // 2, d, 1024, "grad_w_gate_up")
    dh2 = _matmul(dgu, wgu_full, "nn", F32, 512, d, ff, "dh2")
    dx1, dmix, dshift_f, dscale_f, dgate_m, gg_ffn_pre, gg_mix_post = _bwd_mid(
        dh2, dy, x1, mix, g_ffn_pre, g_mix_post, mod, seq)
    dcat = _matmul(dmix, wout_full, "nt", BF16, 1024, 512, d, "dcat")
    gwout = _matmul(cat, dmix, "tn", F32, d // 2, d, 1024, "grad_w_out")
    dqkv = _attn_bwd(proj, dcat, cstats, tri_after, tri_incl, n_seq, seq)
    dproj, gw_pool, gs_pool = _pool_bwd(proj, dcat, w_pool[0], pool_scale, dqkv, n_seq, seq)
    gwin = _matmul(dproj, h1, "tn", F32, d // 2, d, 1024, "grad_w_in")
    dh1 = _matmul(dproj, win_full.reshape(4, d // 2, d), "nn", F32, 512, d, d // 2, "dh1")
    grad_x, dshift_m, dscale_m, gg_mix_pre = _bwd_pre(dh1, dx1, x2, g_mix_pre, mod, seq)

    r_win, r_wout, r_wgu, r_wd = _reduce_scatter(
        [gwin.reshape(1, N_DEV, in_rows, d), gwout.reshape(1, N_DEV, out_rows, d),
         gwgu.reshape(2, N_DEV, ff_rows, d), gwd.reshape(1, N_DEV, ff_rows, d)], "w")
    grad_w_in = r_win[0].T
    grad_w_out = r_wout[0]
    grad_w_gate = r_wgu[0].T
    grad_w_up = r_wgu[1].T
    grad_w_down = r_wd[0]

    dmod = jnp.concatenate([dshift_m, dscale_m, dgate_m, dshift_f, dscale_f, dgate_f], axis=1)
    small = jnp.concatenate([gg_mix_pre, gg_mix_post, gg_ffn_pre, gg_ffn_post,
                             jnp.pad(gs_pool, ((0, 0), (0, d - gs_pool.shape[1]))),
                             jnp.zeros((3, d), F32),
                             gw_pool.reshape(-1, d), dmod.reshape(n_seq * N_MOD, d)], axis=0)
    n_gw = gw_pool.size // d
    (small_g,) = _all_gather([small], [jax.ShapeDtypeStruct((N_DEV,) + small.shape, F32)], [(0, ())], "ag_small")
    dmod_all = small_g[:, 8 + n_gw:, :].reshape(N_DEV * n_seq, N_MOD * d)
    dmod_cols = lax.dynamic_slice_in_dim(dmod_all, me * cond_cols, cond_cols, axis=1)
    grad_w_cond, grad_b_cond = _cond_bwd(c_all, dmod_all, dmod_cols)

    small_ws = [g_mix_pre, g_mix_post, g_ffn_pre, g_ffn_post, pool_scale, w_pool.reshape(-1, POOL_GROUP_DIM)]
    small_ms = [m_g_mix_pre, m_g_mix_post, m_g_ffn_pre, m_g_ffn_post, m_pool_scale, m_w_pool.reshape(-1, POOL_GROUP_DIM)]
    small_vs = [v_g_mix_pre, v_g_mix_post, v_g_ffn_pre, v_g_ffn_post, v_pool_scale, v_w_pool.reshape(-1, POOL_GROUP_DIM)]
    small_gparts = [small_g[:, 0:1, :], small_g[:, 1:2, :], small_g[:, 2:3, :], small_g[:, 3:4, :],
                    small_g[:, 4:5, :pool_scale.shape[1]],
                    small_g[:, 8:8 + n_gw, :].reshape(N_DEV, -1, POOL_GROUP_DIM)]
    so = _adamw_small(small_ws, small_gparts, small_ms, small_vs, "adamw_small")
    ns = len(small_ws)
    sg, sdl, sm, sv = so[:ns], so[ns:2 * ns], so[2 * ns:3 * ns], so[3 * ns:]
    pool_shape = w_pool.shape
    fix = lambda lst: [lst[0], lst[1], lst[2], lst[3], lst[4], lst[5].reshape(pool_shape)]
    sg, sdl, sm, sv = fix(sg), fix(sdl), fix(sm), fix(sv)

    def big(w, g, m, v, rows, name):
        dl, nm, nv = _adamw(w[0], g, m[0], v[0], rows, name)
        return g[None], dl[None], nm[None], nv[None]

    o_cond = big(w_cond, grad_w_cond, m_w_cond, v_w_cond, 256, "adamw_w_cond")
    o_bcond = _adamw(b_cond, grad_b_cond, m_b_cond, v_b_cond, 1, "adamw_b_cond")
    o_bcond = (grad_b_cond,) + tuple(o_bcond)
    o_in = big(w_in, grad_w_in, m_w_in, v_w_in, 256, "adamw_w_in")
    o_out = big(w_out, grad_w_out, m_w_out, v_w_out, out_rows, "adamw_w_out")
    o_gate = big(w_gate, grad_w_gate, m_w_gate, v_w_gate, 256, "adamw_w_gate")
    o_up = big(w_up, grad_w_up, m_w_up, v_w_up, 256, "adamw_w_up")
    o_down = big(w_down, grad_w_down, m_w_down, v_w_down, ff_rows, "adamw_w_down")

    def pick(k):
        small_k = [sg, sdl, sm, sv][k]
        return [o_cond[k], o_bcond[k], small_k[0], small_k[1], o_in[k], small_k[5], small_k[4], o_out[k],
                small_k[2], small_k[3], o_gate[k], o_up[k], o_down[k]]

    return (loss, grad_x.reshape(n_seq, seq, d), *pick(0), *pick(1), *pick(2), *pick(3))
```

```python
import functools
import math

import jax
import jax.numpy as jnp
from jax import lax
from jax.experimental import pallas as pl
from jax.experimental.pallas import tpu as pltpu

F32 = jnp.float32
BF16 = jnp.bfloat16
MESH = pl.DeviceIdType.MESH

N_DEV = 8
HEAD_DIM = 64
LANES = 128
POOL_WINDOWS = (2, 4, 8, 16)
POOL_GROUP_DIM = 128
N_MOD = 6
EPS = 1e-6
ATT_TILE = 256
VMEM_LIMIT = 56 * 1024 * 1024

ADAM_LR = 0.001
ADAM_B1 = 0.9
ADAM_B2 = 0.999
ADAM_EPS = 1e-08
ADAM_WD = 0.01
ADAM_STEP = 10


def _params(**kw):
    return pltpu.CompilerParams(vmem_limit_bytes=VMEM_LIMIT, **kw)


def _dot_nn(a, b):
    return jnp.dot(a, b, preferred_element_type=F32)


def _dot_nt(a, b):
    return lax.dot_general(a, b, (((1,), (1,)), ((), ())), preferred_element_type=F32)


def _dot_tn(a, b):
    return lax.dot_general(a, b, (((0,), (0,)), ((), ())), preferred_element_type=F32)


def _mesh_pos():
    return lax.axis_index("x"), lax.axis_index("y"), lax.axis_index("c")


def _all_gather(srcs, out_shapes, dests, name):
    n = len(srcs)

    def body(*refs):
        src = refs[:n]
        outs = refs[n:n + len(out_shapes)]
        send_sems, recv_sems, local_sems = refs[n + len(out_shapes):]
        x, y, c = _mesh_pos()
        me, sibling = (x, y, c), (x, y, 1 - c)
        chips = [(1 - x, y), (x, 1 - y), (1 - x, 1 - y)]

        def slot(i, dev):
            oi, prefix = dests[i]
            px, py, pc = dev
            return outs[oi].at[prefix + (4 * px + 2 * py + pc,)]

        def copy(i, k, block, to, from_src=False):
            return pltpu.make_async_remote_copy(
                src_ref=src[i] if from_src else slot(i, block), dst_ref=slot(i, block),
                send_sem=send_sems.at[i, k], recv_sem=recv_sems.at[i, k],
                device_id=to, device_id_type=MESH)

        mine = [pltpu.make_async_copy(src[i], slot(i, me), local_sems.at[i]) for i in range(n)]
        for cp in mine:
            cp.start()
        first = []
        for i in range(n):
            first.append(copy(i, 0, me, sibling, from_src=True))
            first += [copy(i, 1 + j, me, (*chip, c), from_src=True) for j, chip in enumerate(chips)]
        for cp in first:
            cp.start()
        passed = []
        for j, chip in enumerate(chips):
            for i in range(n):
                copy(i, 1 + j, (*chip, c), me).wait_recv()
                fwd = copy(i, 4 + j, (*chip, c), sibling)
                fwd.start()
                passed.append(fwd)
        for i in range(n):
            copy(i, 0, sibling, me).wait_recv()
            for j, chip in enumerate(chips):
                copy(i, 4 + j, (*chip, 1 - c), me).wait_recv()
        for cp in first + passed:
            cp.wait_send()
        for cp in mine:
            cp.wait()

    any_spec = pl.BlockSpec(memory_space=pl.ANY)
    return pl.pallas_call(
        body, name=name,
        out_shape=tuple(out_shapes),
        in_specs=[any_spec] * n,
        out_specs=tuple([any_spec] * len(out_shapes)),
        scratch_shapes=[pltpu.SemaphoreType.DMA((n, 7)), pltpu.SemaphoreType.DMA((n, 7)),
                        pltpu.SemaphoreType.DMA((n,))],
    )(*srcs)


def _exchange(parts, axis, name):
    n = len(parts)
    counts = [p.shape[0] * p.shape[1] * p.shape[3] for p in parts]
    total = sum(counts)

    def body(*refs):
        src = refs[:n]
        dst = refs[n:2 * n]
        send_sems, recv_sems = refs[2 * n:]
        pos = dict(zip("xyc", _mesh_pos()))
        my = pos[axis]
        partner = tuple(1 - pos[a] if a == axis else pos[a] for a in "xyc")
        copies = []
        k = 0
        for i in range(n):
            m_n, a_n, _, b_n = parts[i].shape[:4]
            for m in range(m_n):
                for a in range(a_n):
                    for b in range(b_n):
                        copies.append(pltpu.make_async_remote_copy(
                            src_ref=src[i].at[m, a, 1 - my, b], dst_ref=dst[i].at[m, a, b],
                            send_sem=send_sems.at[k], recv_sem=recv_sems.at[k],
                            device_id=partner, device_id_type=MESH))
                        k += 1
        for cp in copies:
            cp.start()
        for cp in copies:
            cp.wait_send()
        for cp in copies:
            cp.wait_recv()

    any_spec = pl.BlockSpec(memory_space=pl.ANY)
    outs = tuple(jax.ShapeDtypeStruct(p.shape[:2] + p.shape[3:], p.dtype) for p in parts)
    return pl.pallas_call(
        body, name=name, out_shape=outs,
        in_specs=[any_spec] * n, out_specs=tuple([any_spec] * n),
        scratch_shapes=[pltpu.SemaphoreType.DMA((total,)), pltpu.SemaphoreType.DMA((total,))],
    )(*parts)


def _add_half(part, recv, my, emit_bf16, name):
    m_n, a_n, _, b_n, r, cdim = part.shape

    def body(my_ref, p_ref, r_ref, *outs):
        s = p_ref[...] + r_ref[...].astype(F32)
        outs[0][...] = s
        if emit_bf16:
            outs[1][...] = s.astype(BF16)

    blk5 = (None, None, None, r, cdim)
    out_shape = [jax.ShapeDtypeStruct((m_n, a_n, b_n, r, cdim), F32)]
    out_specs = [pl.BlockSpec(blk5, lambda m, a, b, s: (m, a, b, 0, 0))]
    if emit_bf16:
        out_shape.append(jax.ShapeDtypeStruct((m_n, a_n, b_n, r, cdim), BF16))
        out_specs.append(pl.BlockSpec(blk5, lambda m, a, b, s: (m, a, b, 0, 0)))
    return pl.pallas_call(
        body, name=name, out_shape=tuple(out_shape),
        grid_spec=pltpu.PrefetchScalarGridSpec(
            num_scalar_prefetch=1, grid=(m_n, a_n, b_n),
            in_specs=[pl.BlockSpec((None, None, None, None, r, cdim), lambda m, a, b, s: (m, a, s[0], b, 0, 0)),
                      pl.BlockSpec(blk5, lambda m, a, b, s: (m, a, b, 0, 0))],
            out_specs=tuple(out_specs)),
        compiler_params=_params(),
    )(my, part, recv)


def _reduce_scatter(grads, tag):
    x, y, c = _mesh_pos()
    cur = [g.reshape(g.shape[0], 4, 2, 1, *g.shape[2:]) for g in grads]
    send = cur
    out = None
    for stage, (axis, my, a_n, b_n) in enumerate((("c", c, 1, 2), ("x", x, 1, 1), ("y", y, None, None))):
        recv = _exchange(send, axis, f"rs_{tag}_xchg_{axis}")
        my1 = jnp.reshape(my, (1,)).astype(jnp.int32)
        last = stage == 2
        sums = [_add_half(p, r, my1, not last, f"rs_{tag}_add_{axis}{i}") for i, (p, r) in enumerate(zip(cur, recv))]
        if last:
            out = [s[0].reshape(s[0].shape[0], *s[0].shape[3:]) for s in sums]
        else:
            shp = lambda t: t.reshape(t.shape[0], a_n, 2, b_n, *t.shape[3:])
            cur = [shp(s[0]) for s in sums]
            send = [shp(s[1]) for s in sums]
    return out


def _matmul(a, b, mode, out_dtype, tm, tn, tk, name):
    ga = a.shape[0] if a.ndim == 3 else None
    gb = b.shape[0] if b.ndim == 3 else None
    a2, b2 = a.shape[-2:], b.shape[-2:]
    if mode == "nn":
        (m, k), n = a2, b2[1]
    elif mode == "nt":
        (m, k), n = a2, b2[0]
    else:
        (k, m), n = a2, b2[1]
    assert m % tm == 0 and n % tn == 0 and k % tk == 0, (name, m, n, k)
    nk = k // tk
    g_n = ga or 1
    batch_out = mode == "tn" and ga is not None
    n_red = nk if batch_out else nk * g_n
    dot = {"nn": _dot_nn, "nt": _dot_nt, "tn": _dot_tn}[mode]
    acc_in_out = out_dtype == F32

    def body(a_ref, b_ref, o_ref, *scratch):
        p = dot(a_ref[...], b_ref[...])
        if n_red == 1:
            o_ref[...] = p.astype(out_dtype)
            return
        acc = o_ref if acc_in_out else scratch[0]
        kk = pl.program_id(3) if batch_out else pl.program_id(2) * nk + pl.program_id(3)

        @pl.when(kk == 0)
        def _():
            acc[...] = p

        @pl.when(kk > 0)
        def _():
            acc[...] += p

        if not acc_in_out:
            @pl.when(kk == n_red - 1)
            def _():
                o_ref[...] = acc[...].astype(out_dtype)

    def order(ids):
        return ids if batch_out else (ids[2], ids[0], ids[1], ids[3])

    def a_idx(*ids):
        g, i, j, kq = order(ids)
        blk = {"nn": (i, kq), "nt": (i, kq), "tn": (kq, i)}[mode]
        return (g,) + blk if ga is not None else blk

    def b_idx(*ids):
        g, i, j, kq = order(ids)
        blk = {"nn": (kq, j), "nt": (j, kq), "tn": (kq, j)}[mode]
        return (g,) + blk if gb is not None else blk

    def o_idx(*ids):
        g, i, j, kq = order(ids)
        return (g, i, j) if batch_out else (i, j)

    a_blk = {"nn": (tm, tk), "nt": (tm, tk), "tn": (tk, tm)}[mode]
    b_blk = {"nn": (tk, tn), "nt": (tn, tk), "tn": (tk, tn)}[mode]
    if ga is not None:
        a_blk = (None,) + a_blk
    if gb is not None:
        b_blk = (None,) + b_blk
    if batch_out:
        out_shape = jax.ShapeDtypeStruct((g_n, m, n), out_dtype)
        o_blk = (None, tm, tn)
        grid = (g_n, m // tm, n // tn, nk)
    else:
        out_shape = jax.ShapeDtypeStruct((m, n), out_dtype)
        o_blk = (tm, tn)
        grid = (m // tm, n // tn, g_n, nk)
    scratch = [] if (acc_in_out or n_red == 1) else [pltpu.VMEM((tm, tn), F32)]
    return pl.pallas_call(
        body, name=name, out_shape=out_shape, grid=grid,
        in_specs=[pl.BlockSpec(a_blk, a_idx), pl.BlockSpec(b_blk, b_idx)],
        out_specs=pl.BlockSpec(o_blk, o_idx),
        scratch_shapes=scratch, compiler_params=_params(),
    )(a, b)


EW_TILE = 256


def _rms(v):
    return lax.rsqrt(jnp.mean(v * v, axis=-1, keepdims=True) + EPS)


def _rms_bwd(dhat, vh, r):
    return r * (dhat - vh * jnp.mean(dhat * vh, axis=-1, keepdims=True))


def _tok_spec(tm, d):
    return pl.BlockSpec((tm, d), lambda i: (i, 0))


def _vec_spec(d):
    return pl.BlockSpec((1, d), lambda i: (0, 0))


def _mod_spec(tiles_per_seq, d):
    return pl.BlockSpec((None, N_MOD, d), lambda i: (i // tiles_per_seq, 0, 0))


def _seq_acc_spec(tiles_per_seq, d):
    return pl.BlockSpec((None, 1, d), lambda i: (i // tiles_per_seq, 0, 0))


def _acc(ref, val, first):
    @pl.when(first)
    def _():
        ref[...] = val

    @pl.when(jnp.logical_not(first))
    def _():
        ref[...] += val


def _colsum(v):
    return jnp.sum(v, axis=0, keepdims=True)


def _pre_mix(x2, g_pre, mod, seq):
    t, d = x2.shape
    tm = EW_TILE

    def body(x_ref, g_ref, mod_ref, h_ref):
        xv = x_ref[...]
        n = xv * _rms(xv) * g_ref[...]
        h_ref[...] = (n * (1.0 + mod_ref[1:2, :]) + mod_ref[0:1, :]).astype(BF16)

    return pl.pallas_call(
        body, name="pre_mix", out_shape=jax.ShapeDtypeStruct((t, d), BF16), grid=(t // tm,),
        in_specs=[_tok_spec(tm, d), _vec_spec(d), _mod_spec(seq // tm, d)],
        out_specs=_tok_spec(tm, d), compiler_params=_params(),
    )(x2, g_pre, mod)


def _mid(mix, x2, g_post, g_pre, mod, seq):
    t, d = x2.shape
    tm = EW_TILE

    def body(mix_ref, x_ref, gpost_ref, gpre_ref, mod_ref, x1_ref, h2_ref):
        mv = mix_ref[...]
        x1 = x_ref[...] + mod_ref[2:3, :] * (mv * _rms(mv) * gpost_ref[...])
        x1_ref[...] = x1
        n = x1 * _rms(x1) * gpre_ref[...]
        h2_ref[...] = (n * (1.0 + mod_ref[4:5, :]) + mod_ref[3:4, :]).astype(BF16)

    return pl.pallas_call(
        body, name="mid", grid=(t // tm,),
        out_shape=(jax.ShapeDtypeStruct((t, d), F32), jax.ShapeDtypeStruct((t, d), BF16)),
        in_specs=[_tok_spec(tm, d), _tok_spec(tm, d), _vec_spec(d), _vec_spec(d), _mod_spec(seq // tm, d)],
        out_specs=(_tok_spec(tm, d), _tok_spec(tm, d)), compiler_params=_params(),
    )(mix, x2, g_post, g_pre, mod)


def _post(f, x1, target, g_post, mod, seq):
    t, d = x1.shape
    tm = EW_TILE
    tps = seq // tm

    def body(f_ref, x1_ref, tgt_ref, g_ref, mod_ref, loss_ref, dy_ref, df_ref, dgate_ref, gg_ref):
        i = pl.program_id(0)
        fv = f_ref[...]
        r = _rms(fv)
        fh = fv * r
        nf = fh * g_ref[...]
        gate = mod_ref[5:6, :]
        err = x1_ref[...] + gate * nf - tgt_ref[...]
        _acc(loss_ref, jnp.sum(_colsum(err * err), axis=1, keepdims=True) * jnp.ones((1, LANES), F32), i == 0)
        dy = err * (1.0 / d)
        dy_ref[...] = dy
        _acc(dgate_ref, _colsum(dy * nf), i % tps == 0)
        dn = dy * gate
        _acc(gg_ref, _colsum(dn * fh), i == 0)
        df_ref[...] = _rms_bwd(dn * g_ref[...], fh, r).astype(BF16)

    n_seq = t // seq
    return pl.pallas_call(
        body, name="post", grid=(t // tm,),
        out_shape=(jax.ShapeDtypeStruct((1, LANES), F32), jax.ShapeDtypeStruct((t, d), F32),
                   jax.ShapeDtypeStruct((t, d), BF16), jax.ShapeDtypeStruct((n_seq, 1, d), F32),
                   jax.ShapeDtypeStruct((1, d), F32)),
        in_specs=[_tok_spec(tm, d), _tok_spec(tm, d), _tok_spec(tm, d), _vec_spec(d), _mod_spec(tps, d)],
        out_specs=(pl.BlockSpec((1, LANES), lambda i: (0, 0)), _tok_spec(tm, d), _tok_spec(tm, d),
                   _seq_acc_spec(tps, d), _vec_spec(d)),
        compiler_params=_params(),
    )(f, x1, target, g_post, mod)


def _bwd_mid(dh2, dy, x1, mix, g_pre, g_post, mod, seq):
    t, d = x1.shape
    tm = EW_TILE
    tps = seq // tm

    def body(dh2_ref, dy_ref, x1_ref, mix_ref, gpre_ref, gpost_ref, mod_ref,
             dx1_ref, dmix_ref, dshift_ref, dscale_ref, dgate_ref, ggpre_ref, ggpost_ref):
        i = pl.program_id(0)
        seq_first = i % tps == 0
        dh = dh2_ref[...]
        x1 = x1_ref[...]
        r = _rms(x1)
        xh = x1 * r
        gpre = gpre_ref[...]
        _acc(dshift_ref, _colsum(dh), seq_first)
        _acc(dscale_ref, _colsum(dh * xh * gpre), seq_first)
        dn = dh * (1.0 + mod_ref[4:5, :])
        _acc(ggpre_ref, _colsum(dn * xh), i == 0)
        dx1 = dy_ref[...] + _rms_bwd(dn * gpre, xh, r)
        dx1_ref[...] = dx1
        mv = mix_ref[...]
        rm = _rms(mv)
        mh = mv * rm
        gpost = gpost_ref[...]
        _acc(dgate_ref, _colsum(dx1 * mh * gpost), seq_first)
        dnm = dx1 * mod_ref[2:3, :]
        _acc(ggpost_ref, _colsum(dnm * mh), i == 0)
        dmix_ref[...] = _rms_bwd(dnm * gpost, mh, rm).astype(BF16)

    n_seq = t // seq
    seq_sds = jax.ShapeDtypeStruct((n_seq, 1, d), F32)
    vec_sds = jax.ShapeDtypeStruct((1, d), F32)
    return pl.pallas_call(
        body, name="bwd_mid", grid=(t // tm,),
        out_shape=(jax.ShapeDtypeStruct((t, d), F32), jax.ShapeDtypeStruct((t, d), BF16),
                   seq_sds, seq_sds, seq_sds, vec_sds, vec_sds),
        in_specs=[_tok_spec(tm, d)] * 4 + [_vec_spec(d), _vec_spec(d), _mod_spec(tps, d)],
        out_specs=(_tok_spec(tm, d), _tok_spec(tm, d), _seq_acc_spec(tps, d), _seq_acc_spec(tps, d),
                   _seq_acc_spec(tps, d), _vec_spec(d), _vec_spec(d)),
        compiler_params=_params(),
    )(dh2, dy, x1, mix, g_pre, g_post, mod)


def _bwd_pre(dh1, dx1, x2, g_pre, mod, seq):
    t, d = x2.shape
    tm = EW_TILE
    tps = seq // tm

    def body(dh_ref, dx1_ref, x_ref, g_ref, mod_ref, gx_ref, dshift_ref, dscale_ref, gg_ref):
        i = pl.program_id(0)
        seq_first = i % tps == 0
        dh = dh_ref[...]
        xv = x_ref[...]
        r = _rms(xv)
        xh = xv * r
        g = g_ref[...]
        _acc(dshift_ref, _colsum(dh), seq_first)
        _acc(dscale_ref, _colsum(dh * xh * g), seq_first)
        dn = dh * (1.0 + mod_ref[1:2, :])
        _acc(gg_ref, _colsum(dn * xh), i == 0)
        gx_ref[...] = dx1_ref[...] + _rms_bwd(dn * g, xh, r)

    n_seq = t // seq
    seq_sds = jax.ShapeDtypeStruct((n_seq, 1, d), F32)
    return pl.pallas_call(
        body, name="bwd_pre", grid=(t // tm,),
        out_shape=(jax.ShapeDtypeStruct((t, d), F32), seq_sds, seq_sds, jax.ShapeDtypeStruct((1, d), F32)),
        in_specs=[_tok_spec(tm, d)] * 3 + [_vec_spec(d), _mod_spec(tps, d)],
        out_specs=(_tok_spec(tm, d), _seq_acc_spec(tps, d), _seq_acc_spec(tps, d), _vec_spec(d)),
        compiler_params=_params(),
    )(dh1, dx1, x2, g_pre, mod)


def _ffn_up(h2, wgu, tm, tn):
    t, d = h2.shape
    f = wgu.shape[1]

    def body(h_ref, w_ref, gu_ref, act_ref):
        h = h_ref[...]
        g = _dot_nt(h, w_ref[0])
        u = _dot_nt(h, w_ref[1])
        gu_ref[0] = g.astype(BF16)
        gu_ref[1] = u.astype(BF16)
        act_ref[...] = (g * jax.nn.sigmoid(g) * u).astype(BF16)

    return pl.pallas_call(
        body, name="ffn_up", grid=(t // tm, f // tn),
        out_shape=(jax.ShapeDtypeStruct((2, t, f), BF16), jax.ShapeDtypeStruct((t, f), BF16)),
        in_specs=[pl.BlockSpec((tm, d), lambda i, j: (i, 0)), pl.BlockSpec((2, tn, d), lambda i, j: (0, j, 0))],
        out_specs=(pl.BlockSpec((2, tm, tn), lambda i, j: (0, i, j)), pl.BlockSpec((tm, tn), lambda i, j: (i, j))),
        compiler_params=_params(),
    )(h2, wgu)


def _ffn_act_bwd(df, wd, gu, tm, tn):
    t, d = df.shape
    f = wd.shape[0]

    def body(df_ref, w_ref, gu_ref, dgu_ref):
        da = _dot_nt(df_ref[...], w_ref[...])
        g = gu_ref[0].astype(F32)
        u = gu_ref[1].astype(F32)
        s = jax.nn.sigmoid(g)
        silu = g * s
        dgu_ref[0] = (da * u * (s + silu * (1.0 - s))).astype(BF16)
        dgu_ref[1] = (da * silu).astype(BF16)

    return pl.pallas_call(
        body, name="ffn_act_bwd", grid=(t // tm, f // tn),
        out_shape=jax.ShapeDtypeStruct((2, t, f), BF16),
        in_specs=[pl.BlockSpec((tm, d), lambda i, j: (i, 0)), pl.BlockSpec((tn, d), lambda i, j: (j, 0)),
                  pl.BlockSpec((2, tm, tn), lambda i, j: (0, i, j))],
        out_specs=pl.BlockSpec((2, tm, tn), lambda i, j: (0, i, j)),
        compiler_params=_params(),
    )(df, wd, gu)


SIGN_BIT = 0x80000000
Q_SCALE = 1.0 / math.sqrt(HEAD_DIM)


def _split_dot(v, tri2):
    hi = v.astype(BF16)
    lo = (v - hi.astype(F32)).astype(BF16)
    return _dot_nn(jnp.concatenate([hi, lo], axis=1), tri2)


def _sb_tile(qs, k2, mask, ntri2, cur):
    z = _dot_nt(qs, k2)
    neg_abs = lax.bitcast_convert_type(lax.bitcast_convert_type(z, jnp.uint32) | jnp.uint32(SIGN_BIT), F32)
    sp = jnp.maximum(z, 0.0) + jnp.log(1.0 + jnp.exp(neg_abs))
    if mask is not None:
        sp = jnp.where(mask, sp, 0.0)
    w = jnp.exp(z + _split_dot(sp, ntri2) + cur)
    if mask is not None:
        w = jnp.where(mask, w, 0.0)
    return z, sp, w


def _stack_heads(v, lane, scale=None):
    if scale is not None:
        v = v * jnp.asarray(scale, v.dtype)
    zero = jnp.zeros_like(v)
    return jnp.concatenate([jnp.where(lane < HEAD_DIM, v, zero), jnp.where(lane >= HEAD_DIM, v, zero)], axis=0)


def _diag_mask(tq):
    row = lax.broadcasted_iota(jnp.int32, (2 * tq, tq), 0)
    col = lax.broadcasted_iota(jnp.int32, (2 * tq, tq), 1)
    return col < jnp.where(row >= tq, row - tq, row)


def _attn_fwd(proj, tri_after, n_seq, seq):
    t = proj.shape[0]
    tq = ATT_TILE
    n_pair = (proj.shape[1] // 4) // LANES

    def body(q_ref, k_ref, v_ref, tri_ref, o_ref, cs_ref, oacc, cmat, carry):
        lane = lax.broadcasted_iota(jnp.int32, (1, LANES), 1)
        ntri2 = tri_ref[...]
        diag = _diag_mask(tq)

        def q_tile(qi, _):
            r0 = pl.multiple_of(qi * tq, tq)
            qs = _stack_heads(q_ref[pl.ds(r0, tq), :], lane, Q_SCALE)
            carry[...] = jnp.zeros_like(carry)
            cmat[...] = jnp.zeros_like(cmat)
            oacc[...] = jnp.zeros_like(oacc)

            def run_tiles(tiles):
                cur = carry[...]
                cm = cmat[...]
                pv = None
                for kb, mask in tiles:
                    c0 = pl.multiple_of(kb * tq, tq)
                    _, sp, w = _sb_tile(qs, k_ref[pl.ds(c0, tq), :], mask, ntri2, cur)
                    p = _dot_nn(w.astype(BF16), v_ref[pl.ds(c0, tq), :])
                    pv = p if pv is None else pv + p
                    cm = jnp.where(lane == kb, cur, cm)
                    cur = cur - jnp.sum(sp, axis=1, keepdims=True)
                oacc[...] += pv
                cmat[...] = cm
                carry[...] = cur

            odd = qi % 2

            @pl.when(odd == 0)
            def _():
                run_tiles([(qi, diag)])

            @pl.when(odd == 1)
            def _():
                run_tiles([(qi, diag), (qi - 1, None)])

            def pair(j, _):
                kb = qi - 1 - odd - 2 * j
                run_tiles([(kb, None), (kb - 1, None)])
                return 0

            lax.fori_loop(0, qi // 2, pair, 0)
            cs_ref[pl.ds(r0, tq), 0:LANES] = cmat[0:tq, :]
            cs_ref[pl.ds(r0, tq), LANES:2 * LANES] = cmat[tq:2 * tq, :]
            o_ref[pl.ds(r0, tq), :] = jnp.where(lane < HEAD_DIM, oacc[0:tq, :], oacc[tq:2 * tq, :]).astype(BF16)
            return 0

        lax.fori_loop(0, seq // tq, q_tile, 0)

    blk = lambda off: pl.BlockSpec((seq, LANES), lambda b, p: (b, off + p))
    return pl.pallas_call(
        body, name="attn_fwd", grid=(n_seq, n_pair),
        out_shape=(jax.ShapeDtypeStruct((t, n_pair * LANES), BF16),
                   jax.ShapeDtypeStruct((t, n_pair * 2 * LANES), F32)),
        in_specs=[blk(0), blk(n_pair), blk(2 * n_pair), pl.BlockSpec((2 * tq, tq), lambda b, p: (0, 0))],
        out_specs=(pl.BlockSpec((seq, LANES), lambda b, p: (b, p)),
                   pl.BlockSpec((seq, 2 * LANES), lambda b, p: (b, p))),
        scratch_shapes=[pltpu.VMEM((2 * tq, LANES), F32), pltpu.VMEM((2 * tq, LANES), F32),
                        pltpu.VMEM((2 * tq, 1), F32)],
        compiler_params=_params(),
    )(proj, proj, proj, tri_after)


def _attn_bwd(proj, dcat, cstats, tri_after, tri_incl, n_seq, seq):
    t = proj.shape[0]
    tq = ATT_TILE
    width = proj.shape[1] // 4
    n_pair = width // LANES
    inv = 1.0 / math.sqrt(HEAD_DIM)

    def body(q_ref, k_ref, v_ref, do_ref, cs_ref, tria_ref, trii_ref, out_ref, dq_acc, dk_acc, dv_acc, ecarry):
        lane = lax.broadcasted_iota(jnp.int32, (1, LANES), 1)
        ntri2 = tria_ref[...]
        tri_i2 = trii_ref[...]
        diag = _diag_mask(tq)
        dk_acc[...] = jnp.zeros_like(dk_acc)
        dv_acc[...] = jnp.zeros_like(dv_acc)

        def q_tile(qi, _):
            r0 = pl.multiple_of(qi * tq, tq)
            qs = _stack_heads(q_ref[pl.ds(r0, tq), :], lane, Q_SCALE)
            dos = _stack_heads(do_ref[pl.ds(r0, tq), :], lane)
            cs = jnp.concatenate([cs_ref[pl.ds(r0, tq), 0:LANES], cs_ref[pl.ds(r0, tq), LANES:2 * LANES]], axis=0)
            ecarry[...] = jnp.zeros_like(ecarry)
            dq_acc[...] = jnp.zeros_like(dq_acc)

            def run_tiles(tiles):
                ec = ecarry[...]
                dq = None
                for kb, mask in tiles:
                    c0 = pl.multiple_of(kb * tq, tq)
                    k2 = k_ref[pl.ds(c0, tq), :]
                    v2 = v_ref[pl.ds(c0, tq), :]
                    cur = jnp.sum(jnp.where(lane == kb, cs, 0.0), axis=1, keepdims=True)
                    z, sp, w = _sb_tile(qs, k2, mask, ntri2, cur)
                    ee = w * _dot_nt(dos, v2)
                    einc = _split_dot(ee, tri_i2) + ec
                    dz = ee - jnp.exp(z - sp) * einc
                    if mask is not None:
                        dz = jnp.where(mask, dz, 0.0)
                    dzb = dz.astype(BF16)
                    p = _dot_nn(dzb, k2)
                    dq = p if dq is None else dq + p
                    dk_acc[pl.ds(c0, tq), :] += _dot_tn(dzb, qs)
                    dv_acc[pl.ds(c0, tq), :] += _dot_tn(w.astype(BF16), dos)
                    ec = ec + jnp.sum(ee, axis=1, keepdims=True)
                dq_acc[...] += dq
                ecarry[...] = ec

            def pair(j, _):
                run_tiles([(2 * j, None), (2 * j + 1, None)])
                return 0

            lax.fori_loop(0, qi // 2, pair, 0)
            odd = qi % 2

            @pl.when(odd == 0)
            def _():
                run_tiles([(qi, diag)])

            @pl.when(odd == 1)
            def _():
                run_tiles([(qi - 1, None), (qi, diag)])

            dq = jnp.where(lane < HEAD_DIM, dq_acc[0:tq, :], dq_acc[tq:2 * tq, :])
            out_ref[0, pl.ds(r0, tq), :] = (dq * Q_SCALE).astype(BF16)
            return 0

        lax.fori_loop(0, seq // tq, q_tile, 0)
        out_ref[1] = dk_acc[...].astype(BF16)
        out_ref[2] = dv_acc[...].astype(BF16)

    blk = lambda off: pl.BlockSpec((seq, LANES), lambda b, p: (b, off + p))
    tri_spec = pl.BlockSpec((2 * tq, tq), lambda b, p: (0, 0))
    return pl.pallas_call(
        body, name="attn_bwd", grid=(n_seq, n_pair),
        out_shape=jax.ShapeDtypeStruct((4, t, width), BF16),
        in_specs=[blk(0), blk(n_pair), blk(2 * n_pair), pl.BlockSpec((seq, LANES), lambda b, p: (b, p)),
                  pl.BlockSpec((seq, 2 * LANES), lambda b, p: (b, p)), tri_spec, tri_spec],
        out_specs=pl.BlockSpec((3, seq, LANES), lambda b, p: (0, b, p)),
        scratch_shapes=[pltpu.VMEM((2 * tq, LANES), F32), pltpu.VMEM((seq, LANES), F32),
                        pltpu.VMEM((seq, LANES), F32), pltpu.VMEM((2 * tq, 1), F32)],
        compiler_params=_params(),
    )(proj, proj, proj, dcat, cstats, tri_after, tri_incl)


def _window_terms(g, rows):
    win = jnp.where(g == 0, POOL_WINDOWS[0], jnp.where(g == 1, POOL_WINDOWS[1],
                    jnp.where(g == 2, POOL_WINDOWS[2], POOL_WINDOWS[3])))
    cnt = jnp.minimum(rows + 1, win).astype(F32)
    return win, cnt


def _window_sum(v, g, rows, forward):
    s_len = v.shape[0]
    sums = []
    s = v
    for step in range(len(POOL_WINDOWS)):
        sh = 1 << step
        if forward:
            shifted = jnp.where(rows < s_len - sh, pltpu.roll(s, s_len - sh, axis=0), 0.0)
        else:
            shifted = jnp.where(rows >= sh, pltpu.roll(s, sh, axis=0), 0.0)
        s = s + shifted
        sums.append(s)
    return jnp.where(g == 0, sums[0], jnp.where(g == 1, sums[1], jnp.where(g == 2, sums[2], sums[3])))


def _pooled(u, g, rows):
    _, cnt = _window_terms(g, rows)
    return _window_sum(u, g, rows, forward=False) / cnt - u


def _pool_fwd(proj, w_pool, pool_scale, n_seq, seq):
    t = proj.shape[0]
    n_grp = len(POOL_WINDOWS)
    u_off = 3 * (proj.shape[1] // 4) // LANES

    def body(u_ref, w_ref, s_ref, o_ref):
        g = pl.program_id(1)
        rows = lax.broadcasted_iota(jnp.int32, (seq, 1), 0)
        pooled = _pooled(u_ref[...].astype(F32), g, rows)
        y = _dot_nn(pooled.astype(BF16), w_ref[...].astype(BF16))
        o_ref[...] = (y * s_ref[...]).astype(BF16)

    return pl.pallas_call(
        body, name="pool_fwd", grid=(n_seq, n_grp),
        out_shape=jax.ShapeDtypeStruct((t, n_grp * POOL_GROUP_DIM), BF16),
        in_specs=[pl.BlockSpec((seq, LANES), lambda b, g: (b, u_off + g)),
                  pl.BlockSpec((None, POOL_GROUP_DIM, POOL_GROUP_DIM), lambda b, g: (g, 0, 0)),
                  pl.BlockSpec((1, POOL_GROUP_DIM), lambda b, g: (0, g))],
        out_specs=pl.BlockSpec((seq, LANES), lambda b, g: (b, g)),
        compiler_params=_params(),
    )(proj, w_pool, pool_scale)


def _pool_bwd(proj, dcat, w_pool, pool_scale, dqkv, n_seq, seq):
    n_grp = len(POOL_WINDOWS)
    width = proj.shape[1] // 4
    u_off = 3 * width // LANES
    dp_off = width // LANES

    def body(u_ref, dp_ref, w_ref, s_ref, alias_ref, du_ref, gw_ref, gs_ref):
        del alias_ref
        g = pl.program_id(0)
        b = pl.program_id(1)
        rows = lax.broadcasted_iota(jnp.int32, (seq, 1), 0)
        pooled = _pooled(u_ref[...].astype(F32), g, rows)
        pb = pooled.astype(BF16)
        wb = w_ref[...].astype(BF16)
        z = _dot_nn(pb, wb)
        dp = dp_ref[...].astype(F32)
        _acc(gs_ref, _colsum(dp * z), b == 0)
        dys = (dp * s_ref[...]).astype(BF16)
        _acc(gw_ref, _dot_tn(pb, dys), b == 0)
        dpooled = _dot_nt(dys, wb)
        _, cnt = _window_terms(g, rows)
        du = _window_sum(dpooled / cnt, g, rows, forward=True) - dpooled
        du_ref[...] = du.astype(BF16)

    t = proj.shape[0]
    return pl.pallas_call(
        body, name="pool_bwd", grid=(n_grp, n_seq),
        out_shape=(jax.ShapeDtypeStruct(dqkv.shape, BF16),
                   jax.ShapeDtypeStruct((n_grp, POOL_GROUP_DIM, POOL_GROUP_DIM), F32),
                   jax.ShapeDtypeStruct((1, n_grp * POOL_GROUP_DIM), F32)),
        in_specs=[pl.BlockSpec((seq, LANES), lambda g, b: (b, u_off + g)),
                  pl.BlockSpec((seq, LANES), lambda g, b: (b, dp_off + g)),
                  pl.BlockSpec((None, POOL_GROUP_DIM, POOL_GROUP_DIM), lambda g, b: (g, 0, 0)),
                  pl.BlockSpec((1, POOL_GROUP_DIM), lambda g, b: (0, g)),
                  pl.BlockSpec(memory_space=pl.ANY)],
        out_specs=(pl.BlockSpec((None, seq, LANES), lambda g, b: (3, b, g)),
                   pl.BlockSpec((None, POOL_GROUP_DIM, POOL_GROUP_DIM), lambda g, b: (g, 0, 0)),
                   pl.BlockSpec((1, POOL_GROUP_DIM), lambda g, b: (0, g))),
        input_output_aliases={4: 0},
        compiler_params=_params(),
    )(proj, dcat, w_pool, pool_scale, dqkv)


def _cond_fwd(c_all, w_cond, b_cols):
    n, _ = c_all.shape
    cols = w_cond.shape[1]

    def body(c_ref, w_ref, b_ref, o_ref):
        cv = c_ref[...]
        a = cv * jax.nn.sigmoid(cv)
        o_ref[...] = jnp.dot(a, w_ref[...], preferred_element_type=F32,
                             precision=lax.Precision.HIGHEST) + b_ref[...]

    return pl.pallas_call(
        body, name="cond_fwd", out_shape=jax.ShapeDtypeStruct((n, cols), F32),
        compiler_params=_params(),
    )(c_all, w_cond, b_cols)


def _cond_bwd(c_all, dmod_all, dmod_cols):
    n, d = c_all.shape
    cols = dmod_cols.shape[1]

    def body(c_ref, dm_ref, dmc_ref, gw_ref, gb_ref):
        cv = c_ref[...]
        a = cv * jax.nn.sigmoid(cv)
        gw_ref[...] = lax.dot_general(a, dmc_ref[...], (((0,), (0,)), ((), ())),
                                      preferred_element_type=F32, precision=lax.Precision.HIGHEST)
        gb_ref[...] = _colsum(dm_ref[...])

    return pl.pallas_call(
        body, name="cond_bwd",
        out_shape=(jax.ShapeDtypeStruct((d, cols), F32), jax.ShapeDtypeStruct((1, dmod_all.shape[1]), F32)),
        compiler_params=_params(),
    )(c_all, dmod_all, dmod_cols)


def _adamw_math(w, g, m, v):
    m = ADAM_B1 * m + (1.0 - ADAM_B1) * g
    v = ADAM_B2 * v + (1.0 - ADAM_B2) * (g * g)
    m_hat = m / (1.0 - ADAM_B1 ** ADAM_STEP)
    v_hat = v / (1.0 - ADAM_B2 ** ADAM_STEP)
    delta = -ADAM_LR * (m_hat / (jnp.sqrt(v_hat) + ADAM_EPS) + ADAM_WD * w)
    return delta, m, v


def _adamw(w, g, m, v, rows, name):
    r, cdim = w.shape

    def body(w_ref, g_ref, m_ref, v_ref, d_ref, nm_ref, nv_ref):
        d_ref[...], nm_ref[...], nv_ref[...] = _adamw_math(w_ref[...], g_ref[...], m_ref[...], v_ref[...])

    spec = pl.BlockSpec((rows, cdim), lambda i: (i, 0))
    sds = jax.ShapeDtypeStruct((r, cdim), F32)
    return pl.pallas_call(
        body, name=name, grid=(r // rows,), out_shape=(sds, sds, sds),
        in_specs=[spec] * 4, out_specs=(spec, spec, spec), compiler_params=_params(),
    )(w, g, m, v)


def _adamw_small(ws, gparts, ms, vs, name):
    n = len(ws)

    def body(*refs):
        w_r, g_r, m_r, v_r = refs[:n], refs[n:2 * n], refs[2 * n:3 * n], refs[3 * n:4 * n]
        outs = refs[4 * n:]
        for i in range(n):
            g = g_r[i][0]
            for dev in range(1, g_r[i].shape[0]):
                g = g + g_r[i][dev]
            delta, m, v = _adamw_math(w_r[i][...], g, m_r[i][...], v_r[i][...])
            outs[i][...] = g
            outs[n + i][...] = delta
            outs[2 * n + i][...] = m
            outs[3 * n + i][...] = v

    sds = [jax.ShapeDtypeStruct(w.shape, F32) for w in ws]
    return pl.pallas_call(
        body, name=name, out_shape=tuple(sds * 4), compiler_params=_params(),
    )(*ws, *gparts, *ms, *vs)


def kernel(x, c, w_cond, b_cond, g_mix_pre, g_mix_post, w_in, w_pool, pool_scale, w_out, g_ffn_pre, g_ffn_post, w_gate, w_up, w_down, loss_target, m_w_cond, m_b_cond, m_g_mix_pre, m_g_mix_post, m_w_in, m_w_pool, m_pool_scale, m_w_out, m_g_ffn_pre, m_g_ffn_post, m_w_gate, m_w_up, m_w_down, v_w_cond, v_b_cond, v_g_mix_pre, v_g_mix_post, v_w_in, v_w_pool, v_pool_scale, v_w_out, v_g_ffn_pre, v_g_ffn_post, v_w_gate, v_w_up, v_w_down):
    n_seq, seq, d = x.shape
    t = n_seq * seq
    xi, yi, ci = _mesh_pos()
    me = 4 * xi + 2 * yi + ci
    x2 = x.reshape(t, d)
    tgt2 = loss_target.reshape(t, d)
    in_rows = w_in.shape[2]
    out_rows = w_out.shape[1]
    ff_rows = w_gate.shape[2]
    ff = N_DEV * ff_rows
    cond_cols = w_cond.shape[2]

    win_t = w_in[0].T.astype(BF16)
    wout_s = w_out[0].astype(BF16)
    wg_t = w_gate[0].T.astype(BF16)
    wu_t = w_up[0].T.astype(BF16)
    wd_s = w_down[0].astype(BF16)
    c_all, win_g, wout_g, wgu_g, wd_g = _all_gather(
        [c, win_t, wout_s, wg_t, wu_t, wd_s],
        [jax.ShapeDtypeStruct((N_DEV, n_seq, d), F32), jax.ShapeDtypeStruct((N_DEV, in_rows, d), BF16),
         jax.ShapeDtypeStruct((N_DEV, out_rows, d), BF16), jax.ShapeDtypeStruct((2, N_DEV, ff_rows, d), BF16),
         jax.ShapeDtypeStruct((N_DEV, ff_rows, d), BF16)],
        [(0, ()), (1, ()), (2, ()), (3, (0,)), (3, (1,)), (4, ())], "ag_weights")
    c_all = c_all.reshape(N_DEV * n_seq, d)
    win_full = win_g.reshape(N_DEV * in_rows, d)
    wout_full = wout_g.reshape(N_DEV * out_rows, d)
    wgu_full = wgu_g.reshape(2, ff, d)
    wd_full = wd_g.reshape(ff, d)

    b_cols = lax.dynamic_slice_in_dim(b_cond, me * cond_cols, cond_cols, axis=1)
    mod_cols = _cond_fwd(c_all, w_cond[0], b_cols)
    (mod_g,) = _all_gather([mod_cols], [jax.ShapeDtypeStruct((N_DEV,) + mod_cols.shape, F32)], [(0, ())], "ag_mod")
    mod_mine = lax.dynamic_slice_in_dim(mod_g, me * n_seq, n_seq, axis=1)
    mod = jnp.transpose(mod_mine, (1, 0, 2)).reshape(n_seq, N_MOD, d)

    h1 = _pre_mix(x2, g_mix_pre, mod, seq)
    proj = _matmul(h1, win_full, "nt", BF16, 1024, 512, d, "proj")
    tq = ATT_TILE
    ids = jnp.arange(tq)
    tri_after = jnp.tile(-(ids[:, None] >= ids[None, :]).astype(BF16), (2, 1))
    tri_incl = jnp.tile((ids[:, None] <= ids[None, :]).astype(BF16), (2, 1))
    attn, cstats = _attn_fwd(proj, tri_after, n_seq, seq)
    pool = _pool_fwd(proj, w_pool[0], pool_scale, n_seq, seq)
    cat = jnp.stack([attn, pool])
    mix = _matmul(cat, wout_full.reshape(2, d // 2, d), "nn", F32, 512, d, d // 2, "mix")
    x1, h2 = _mid(mix, x2, g_mix_post, g_ffn_pre, mod, seq)
    gu, act = _ffn_up(h2, wgu_full, 512, ff // 2)
    f = _matmul(act, wd_full, "nn", F32, 512, d, ff, "ffn_down")
    loss_sum, dy, df, dgate_f, gg_ffn_post = _post(f, x1, tgt2, g_ffn_post, mod, seq)
    loss = lax.psum(loss_sum[0, 0] * (0.5 / d), ("x", "y", "c"))

    dgu = _ffn_act_bwd(df, wd_full, gu, 512, ff // 2)
    gwd = _matmul(act, df, "tn", F32, ff // 2, d, 1024, "grad_w_down")
    gwgu = _matmul(dgu, h2, "tn", F32, ff // 2, d, 1024, "grad_w_gate_up")
    dh2 = _matmul(dgu, wgu_full, "nn", F32, 512, d, ff, "dh2")
    dx1, dmix, dshift_f, dscale_f, dgate_m, gg_ffn_pre, gg_mix_post = _bwd_mid(
        dh2, dy, x1, mix, g_ffn_pre, g_mix_post, mod, seq)
    dcat = _matmul(dmix, wout_full, "nt", BF16, 1024, 512, d, "dcat")
    gwout = _matmul(cat, dmix, "tn", F32, d // 2, d, 1024, "grad_w_out")
    dqkv = _attn_bwd(proj, dcat, cstats, tri_after, tri_incl, n_seq, seq)
    dproj, gw_pool, gs_pool = _pool_bwd(proj, dcat, w_pool[0], pool_scale, dqkv, n_seq, seq)
    gwin = _matmul(dproj, h1, "tn", F32, d // 2, d, 1024, "grad_w_in")
    dh1 = _matmul(dproj, win_full.reshape(4, d // 2, d), "nn", F32, 512, d, d // 2, "dh1")
    grad_x, dshift_m, dscale_m, gg_mix_pre = _bwd_pre(dh1, dx1, x2, g_mix_pre, mod, seq)

    r_win, r_wout, r_wgu, r_wd = _reduce_scatter(
        [gwin.reshape(1, N_DEV, in_rows, d), gwout.reshape(1, N_DEV, out_rows, d),
         gwgu.reshape(2, N_DEV, ff_rows, d), gwd.reshape(1, N_DEV, ff_rows, d)], "w")
    grad_w_in = r_win[0].T
    grad_w_out = r_wout[0]
    grad_w_gate = r_wgu[0].T
    grad_w_up = r_wgu[1].T
    grad_w_down = r_wd[0]

    dmod = jnp.concatenate([dshift_m, dscale_m, dgate_m, dshift_f, dscale_f, dgate_f], axis=1)
    small = jnp.concatenate([gg_mix_pre, gg_mix_post, gg_ffn_pre, gg_ffn_post,
                             jnp.pad(gs_pool, ((0, 0), (0, d - gs_pool.shape[1]))),
                             jnp.zeros((3, d), F32),
                             gw_pool.reshape(-1, d), dmod.reshape(n_seq * N_MOD, d)], axis=0)
    n_gw = gw_pool.size // d
    (small_g,) = _all_gather([small], [jax.ShapeDtypeStruct((N_DEV,) + small.shape, F32)], [(0, ())], "ag_small")
    dmod_all = small_g[:, 8 + n_gw:, :].reshape(N_DEV * n_seq, N_MOD * d)
    dmod_cols = lax.dynamic_slice_in_dim(dmod_all, me * cond_cols, cond_cols, axis=1)
    grad_w_cond, grad_b_cond = _cond_bwd(c_all, dmod_all, dmod_cols)

    small_ws = [g_mix_pre, g_mix_post, g_ffn_pre, g_ffn_post, pool_scale, w_pool.reshape(-1, POOL_GROUP_DIM)]
    small_ms = [m_g_mix_pre, m_g_mix_post, m_g_ffn_pre, m_g_ffn_post, m_pool_scale, m_w_pool.reshape(-1, POOL_GROUP_DIM)]
    small_vs = [v_g_mix_pre, v_g_mix_post, v_g_ffn_pre, v_g_ffn_post, v_pool_scale, v_w_pool.reshape(-1, POOL_GROUP_DIM)]
    small_gparts = [small_g[:, 0:1, :], small_g[:, 1:2, :], small_g[:, 2:3, :], small_g[:, 3:4, :],
                    small_g[:, 4:5, :pool_scale.shape[1]],
                    small_g[:, 8:8 + n_gw, :].reshape(N_DEV, -1, POOL_GROUP_DIM)]
    so = _adamw_small(small_ws, small_gparts, small_ms, small_vs, "adamw_small")
    ns = len(small_ws)
    sg, sdl, sm, sv = so[:ns], so[ns:2 * ns], so[2 * ns:3 * ns], so[3 * ns:]
    pool_shape = w_pool.shape
    fix = lambda lst: [lst[0], lst[1], lst[2], lst[3], lst[4], lst[5].reshape(pool_shape)]
    sg, sdl, sm, sv = fix(sg), fix(sdl), fix(sm), fix(sv)

    def big(w, g, m, v, rows, name):
        dl, nm, nv = _adamw(w[0], g, m[0], v[0], rows, name)
        return g[None], dl[None], nm[None], nv[None]

    o_cond = big(w_cond, grad_w_cond, m_w_cond, v_w_cond, 256, "adamw_w_cond")
    o_bcond = _adamw(b_cond, grad_b_cond, m_b_cond, v_b_cond, 1, "adamw_b_cond")
    o_bcond = (grad_b_cond,) + tuple(o_bcond)
    o_in = big(w_in, grad_w_in, m_w_in, v_w_in, 256, "adamw_w_in")
    o_out = big(w_out, grad_w_out, m_w_out, v_w_out, out_rows, "adamw_w_out")
    o_gate = big(w_gate, grad_w_gate, m_w_gate, v_w_gate, 256, "adamw_w_gate")
    o_up = big(w_up, grad_w_up, m_w_up, v_w_up, 256, "adamw_w_up")
    o_down = big(w_down, grad_w_down, m_w_down, v_w_down, ff_rows, "adamw_w_down")

    def pick(k):
        small_k = [sg, sdl, sm, sv][k]
        return [o_cond[k], o_bcond[k], small_k[0], small_k[1], o_in[k], small_k[5], small_k[4], o_out[k],
                small_k[2], small_k[3], o_gate[k], o_up[k], o_down[k]]

    return (loss, grad_x.reshape(n_seq, seq, d), *pick(0), *pick(1), *pick(2), *pick(3))
```

```python
import functools
import math

import jax
import jax.numpy as jnp
from jax import lax
from jax.experimental import pallas as pl
from jax.experimental.pallas import tpu as pltpu

F32 = jnp.float32
BF16 = jnp.bfloat16
MESH = pl.DeviceIdType.MESH

N_DEV = 8
HEAD_DIM = 64
LANES = 128
POOL_WINDOWS = (2, 4, 8, 16)
POOL_GROUP_DIM = 128
N_MOD = 6
EPS = 1e-6
ATT_TILE = 256
VMEM_LIMIT = 56 * 1024 * 1024

ADAM_LR = 0.001
ADAM_B1 = 0.9
ADAM_B2 = 0.999
ADAM_EPS = 1e-08
ADAM_WD = 0.01
ADAM_STEP = 10


def _params(**kw):
    return pltpu.CompilerParams(vmem_limit_bytes=VMEM_LIMIT, **kw)


def _dot_nn(a, b):
    return jnp.dot(a, b, preferred_element_type=F32)


def _dot_nt(a, b):
    return lax.dot_general(a, b, (((1,), (1,)), ((), ())), preferred_element_type=F32)


def _dot_tn(a, b):
    return lax.dot_general(a, b, (((0,), (0,)), ((), ())), preferred_element_type=F32)


def _mesh_pos():
    return lax.axis_index("x"), lax.axis_index("y"), lax.axis_index("c")


def _ag_phases(dests, src, outs, send_sems, recv_sems, local_sems):
    n = len(src)
    x, y, c = _mesh_pos()
    me, sibling = (x, y, c), (x, y, 1 - c)
    chips = [(1 - x, y), (x, 1 - y), (1 - x, 1 - y)]

    def slot(i, dev):
        oi, prefix = dests[i]
        px, py, pc = dev
        return outs[oi].at[prefix + (4 * px + 2 * py + pc,)]

    def copy(i, k, block, to, from_src=False):
        return pltpu.make_async_remote_copy(
            src_ref=src[i] if from_src else slot(i, block), dst_ref=slot(i, block),
            send_sem=send_sems.at[i, k], recv_sem=recv_sems.at[i, k],
            device_id=to, device_id_type=MESH)

    def mine(i):
        return pltpu.make_async_copy(src[i], slot(i, me), local_sems.at[i])

    def first(i):
        return [copy(i, 0, me, sibling, from_src=True)] + [
            copy(i, 1 + j, me, (*chip, c), from_src=True) for j, chip in enumerate(chips)]

    def passed(i, j):
        return copy(i, 4 + j, (*chips[j], c), sibling)

    def start():
        for i in range(n):
            mine(i).start()
        for i in range(n):
            for cp in first(i):
                cp.start()

    def forward():
        for j, chip in enumerate(chips):
            for i in range(n):
                copy(i, 1 + j, (*chip, c), me).wait_recv()
                passed(i, j).start()

    def finish():
        for i in range(n):
            copy(i, 0, sibling, me).wait_recv()
            for j, chip in enumerate(chips):
                copy(i, 4 + j, (*chip, 1 - c), me).wait_recv()
        for i in range(n):
            for cp in first(i) + [passed(i, j) for j in range(3)]:
                cp.wait_send()
            mine(i).wait()

    return start, forward, finish


def _ag_scratch(n):
    return [pltpu.SemaphoreType.DMA((n, 7)), pltpu.SemaphoreType.DMA((n, 7)), pltpu.SemaphoreType.DMA((n,))]


def _all_gather(srcs, out_shapes, dests, name):
    n = len(srcs)

    def body(*refs):
        src = refs[:n]
        outs = refs[n:n + len(out_shapes)]
        start, forward, finish = _ag_phases(dests, src, outs, *refs[n + len(out_shapes):])
        start()
        forward()
        finish()

    any_spec = pl.BlockSpec(memory_space=pl.ANY)
    return pl.pallas_call(
        body, name=name,
        out_shape=tuple(out_shapes),
        in_specs=[any_spec] * n,
        out_specs=tuple([any_spec] * len(out_shapes)),
        scratch_shapes=_ag_scratch(n),
    )(*srcs)


def _exchange(parts, axis, name):
    n = len(parts)
    counts = [p.shape[0] * p.shape[1] * p.shape[3] for p in parts]
    total = sum(counts)

    def body(*refs):
        src = refs[:n]
        dst = refs[n:2 * n]
        send_sems, recv_sems = refs[2 * n:]
        pos = dict(zip("xyc", _mesh_pos()))
        my = pos[axis]
        partner = tuple(1 - pos[a] if a == axis else pos[a] for a in "xyc")
        copies = []
        k = 0
        for i in range(n):
            m_n, a_n, _, b_n = parts[i].shape[:4]
            for m in range(m_n):
                for a in range(a_n):
                    for b in range(b_n):
                        copies.append(pltpu.make_async_remote_copy(
                            src_ref=src[i].at[m, a, 1 - my, b], dst_ref=dst[i].at[m, a, b],
                            send_sem=send_sems.at[k], recv_sem=recv_sems.at[k],
                            device_id=partner, device_id_type=MESH))
                        k += 1
        for cp in copies:
            cp.start()
        for cp in copies:
            cp.wait_send()
        for cp in copies:
            cp.wait_recv()

    any_spec = pl.BlockSpec(memory_space=pl.ANY)
    outs = tuple(jax.ShapeDtypeStruct(p.shape[:2] + p.shape[3:], p.dtype) for p in parts)
    return pl.pallas_call(
        body, name=name, out_shape=outs,
        in_specs=[any_spec] * n, out_specs=tuple([any_spec] * n),
        scratch_shapes=[pltpu.SemaphoreType.DMA((total,)), pltpu.SemaphoreType.DMA((total,))],
    )(*parts)


def _add_half(part, recv, my, emit_bf16, name):
    m_n, a_n, _, b_n, r, cdim = part.shape

    def body(my_ref, p_ref, r_ref, *outs):
        s = p_ref[...] + r_ref[...].astype(F32)
        outs[0][...] = s
        if emit_bf16:
            outs[1][...] = s.astype(BF16)

    blk5 = (None, None, None, r, cdim)
    out_shape = [jax.ShapeDtypeStruct((m_n, a_n, b_n, r, cdim), F32)]
    out_specs = [pl.BlockSpec(blk5, lambda m, a, b, s: (m, a, b, 0, 0))]
    if emit_bf16:
        out_shape.append(jax.ShapeDtypeStruct((m_n, a_n, b_n, r, cdim), BF16))
        out_specs.append(pl.BlockSpec(blk5, lambda m, a, b, s: (m, a, b, 0, 0)))
    return pl.pallas_call(
        body, name=name, out_shape=tuple(out_shape),
        grid_spec=pltpu.PrefetchScalarGridSpec(
            num_scalar_prefetch=1, grid=(m_n, a_n, b_n),
            in_specs=[pl.BlockSpec((None, None, None, None, r, cdim), lambda m, a, b, s: (m, a, s[0], b, 0, 0)),
                      pl.BlockSpec(blk5, lambda m, a, b, s: (m, a, b, 0, 0))],
            out_specs=tuple(out_specs)),
        compiler_params=_params(),
    )(my, part, recv)


def _rs_core_stage(grads, tag):
    c = lax.axis_index("c")
    cur = [g.reshape(g.shape[0], 4, 2, 1, *g.shape[2:]) for g in grads]
    recv = _exchange(cur, "c", f"rs_{tag}_xchg_c")
    my1 = jnp.reshape(c, (1,)).astype(jnp.int32)
    sums = [_add_half(p, r, my1, True, f"rs_{tag}_add_c{i}") for i, (p, r) in enumerate(zip(cur, recv))]
    flat = lambda s: s.reshape(s.shape[0], 4, *s.shape[3:])
    return [flat(s[0]) for s in sums], [flat(s[1]) for s in sums]


def _rs_chip_phases(shapes, src, dst, send_sems, recv_sems):
    x, y, c = _mesh_pos()
    chips = [(1 - x, y), (x, 1 - y), (1 - x, 1 - y)]

    def copies():
        out = []
        n = 0
        for i, shp in enumerate(shapes):
            for m in range(shp[0]):
                for k, (px, py) in enumerate(chips):
                    out.append(pltpu.make_async_remote_copy(
                        src_ref=src[i].at[m, 2 * px + py], dst_ref=dst[i].at[m, k],
                        send_sem=send_sems.at[n], recv_sem=recv_sems.at[n],
                        device_id=(px, py, c), device_id_type=MESH))
                    n += 1
        return out

    def start():
        for cp in copies():
            cp.start()

    def finish():
        for cp in copies():
            cp.wait_send()
        for cp in copies():
            cp.wait_recv()

    return start, finish


def _rs_chip_out(sends):
    return [jax.ShapeDtypeStruct((s.shape[0], 3) + s.shape[2:], s.dtype) for s in sends]


def _rs_chip_scratch(sends):
    total = sum(3 * s.shape[0] for s in sends)
    return [pltpu.SemaphoreType.DMA((total,)), pltpu.SemaphoreType.DMA((total,))]


def _rs_chip_exchange(sends, name):
    n = len(sends)
    shapes = [s.shape for s in sends]

    def body(*refs):
        start, finish = _rs_chip_phases(shapes, refs[:n], refs[n:2 * n], *refs[2 * n:])
        start()
        finish()

    any_spec = pl.BlockSpec(memory_space=pl.ANY)
    return pl.pallas_call(
        body, name=name, out_shape=tuple(_rs_chip_out(sends)),
        in_specs=[any_spec] * n, out_specs=tuple([any_spec] * n),
        scratch_shapes=_rs_chip_scratch(sends),
    )(*sends)


def _rs_final(mine, recv, name):
    m_n, _, r, cdim = mine.shape
    x, y, _ = _mesh_pos()
    chip = jnp.reshape(2 * x + y, (1,)).astype(jnp.int32)

    def body(chip_ref, p_ref, r_ref, o_ref):
        del chip_ref
        o_ref[...] = ((p_ref[...] + r_ref[0].astype(F32)) + r_ref[1].astype(F32)) + r_ref[2].astype(F32)

    return pl.pallas_call(
        body, name=name, out_shape=jax.ShapeDtypeStruct((m_n, r, cdim), F32),
        grid_spec=pltpu.PrefetchScalarGridSpec(
            num_scalar_prefetch=1, grid=(m_n,),
            in_specs=[pl.BlockSpec((None, None, r, cdim), lambda m, s: (m, s[0], 0, 0)),
                      pl.BlockSpec((None, 3, r, cdim), lambda m, s: (m, 0, 0, 0))],
            out_specs=pl.BlockSpec((None, r, cdim), lambda m, s: (m, 0, 0))),
        compiler_params=_params(),
    )(chip, mine, recv)


def _matmul(a, b, mode, out_dtype, tm, tn, tk, name):
    ga = a.shape[0] if a.ndim == 3 else None
    gb = b.shape[0] if b.ndim == 3 else None
    a2, b2 = a.shape[-2:], b.shape[-2:]
    if mode == "nn":
        (m, k), n = a2, b2[1]
    elif mode == "nt":
        (m, k), n = a2, b2[0]
    else:
        (k, m), n = a2, b2[1]
    assert m % tm == 0 and n % tn == 0 and k % tk == 0, (name, m, n, k)
    nk = k // tk
    g_n = ga or 1
    batch_out = mode == "tn" and ga is not None
    n_red = nk if batch_out else nk * g_n
    dot = {"nn": _dot_nn, "nt": _dot_nt, "tn": _dot_tn}[mode]
    acc_in_out = out_dtype == F32

    def body(a_ref, b_ref, o_ref, *scratch):
        p = dot(a_ref[...], b_ref[...])
        if n_red == 1:
            o_ref[...] = p.astype(out_dtype)
            return
        acc = o_ref if acc_in_out else scratch[0]
        kk = pl.program_id(3) if batch_out else pl.program_id(2) * nk + pl.program_id(3)

        @pl.when(kk == 0)
        def _():
            acc[...] = p

        @pl.when(kk > 0)
        def _():
            acc[...] += p

        if not acc_in_out:
            @pl.when(kk == n_red - 1)
            def _():
                o_ref[...] = acc[...].astype(out_dtype)

    def order(ids):
        return ids if batch_out else (ids[2], ids[0], ids[1], ids[3])

    def a_idx(*ids):
        g, i, j, kq = order(ids)
        blk = {"nn": (i, kq), "nt": (i, kq), "tn": (kq, i)}[mode]
        return (g,) + blk if ga is not None else blk

    def b_idx(*ids):
        g, i, j, kq = order(ids)
        blk = {"nn": (kq, j), "nt": (j, kq), "tn": (kq, j)}[mode]
        return (g,) + blk if gb is not None else blk

    def o_idx(*ids):
        g, i, j, kq = order(ids)
        return (g, i, j) if batch_out else (i, j)

    a_blk = {"nn": (tm, tk), "nt": (tm, tk), "tn": (tk, tm)}[mode]
    b_blk = {"nn": (tk, tn), "nt": (tn, tk), "tn": (tk, tn)}[mode]
    if ga is not None:
        a_blk = (None,) + a_blk
    if gb is not None:
        b_blk = (None,) + b_blk
    if batch_out:
        out_shape = jax.ShapeDtypeStruct((g_n, m, n), out_dtype)
        o_blk = (None, tm, tn)
        grid = (g_n, m // tm, n // tn, nk)
    else:
        out_shape = jax.ShapeDtypeStruct((m, n), out_dtype)
        o_blk = (tm, tn)
        grid = (m // tm, n // tn, g_n, nk)
    scratch = [] if (acc_in_out or n_red == 1) else [pltpu.VMEM((tm, tn), F32)]
    return pl.pallas_call(
        body, name=name, out_shape=out_shape, grid=grid,
        in_specs=[pl.BlockSpec(a_blk, a_idx), pl.BlockSpec(b_blk, b_idx)],
        out_specs=pl.BlockSpec(o_blk, o_idx),
        scratch_shapes=scratch, compiler_params=_params(),
    )(a, b)


EW_TILE = 256


def _rms(v):
    return lax.rsqrt(jnp.mean(v * v, axis=-1, keepdims=True) + EPS)


def _rms_bwd(dhat, vh, r):
    return r * (dhat - vh * jnp.mean(dhat * vh, axis=-1, keepdims=True))


def _tok_spec(tm, d):
    return pl.BlockSpec((tm, d), lambda i: (i, 0))


def _vec_spec(d):
    return pl.BlockSpec((1, d), lambda i: (0, 0))


def _mod_spec(tiles_per_seq, d):
    return pl.BlockSpec((None, N_MOD, d), lambda i: (i // tiles_per_seq, 0, 0))


def _seq_acc_spec(tiles_per_seq, d):
    return pl.BlockSpec((None, 1, d), lambda i: (i // tiles_per_seq, 0, 0))


def _acc(ref, val, first):
    @pl.when(first)
    def _():
        ref[...] = val

    @pl.when(jnp.logical_not(first))
    def _():
        ref[...] += val


def _colsum(v):
    return jnp.sum(v, axis=0, keepdims=True)


def _pre_mix(x2, g_pre, mod, seq):
    t, d = x2.shape
    tm = EW_TILE

    def body(x_ref, g_ref, mod_ref, h_ref):
        xv = x_ref[...]
        n = xv * _rms(xv) * g_ref[...]
        h_ref[...] = (n * (1.0 + mod_ref[1:2, :]) + mod_ref[0:1, :]).astype(BF16)

    return pl.pallas_call(
        body, name="pre_mix", out_shape=jax.ShapeDtypeStruct((t, d), BF16), grid=(t // tm,),
        in_specs=[_tok_spec(tm, d), _vec_spec(d), _mod_spec(seq // tm, d)],
        out_specs=_tok_spec(tm, d), compiler_params=_params(),
    )(x2, g_pre, mod)


def _mid(mix, x2, g_post, g_pre, mod, seq):
    t, d = x2.shape
    tm = EW_TILE

    def body(mix_ref, x_ref, gpost_ref, gpre_ref, mod_ref, x1_ref, h2_ref):
        mv = mix_ref[...]
        x1 = x_ref[...] + mod_ref[2:3, :] * (mv * _rms(mv) * gpost_ref[...])
        x1_ref[...] = x1
        n = x1 * _rms(x1) * gpre_ref[...]
        h2_ref[...] = (n * (1.0 + mod_ref[4:5, :]) + mod_ref[3:4, :]).astype(BF16)

    return pl.pallas_call(
        body, name="mid", grid=(t // tm,),
        out_shape=(jax.ShapeDtypeStruct((t, d), F32), jax.ShapeDtypeStruct((t, d), BF16)),
        in_specs=[_tok_spec(tm, d), _tok_spec(tm, d), _vec_spec(d), _vec_spec(d), _mod_spec(seq // tm, d)],
        out_specs=(_tok_spec(tm, d), _tok_spec(tm, d)), compiler_params=_params(),
    )(mix, x2, g_post, g_pre, mod)


def _post(f, x1, target, g_post, mod, seq):
    t, d = x1.shape
    tm = EW_TILE
    tps = seq // tm

    def body(f_ref, x1_ref, tgt_ref, g_ref, mod_ref, loss_ref, dy_ref, df_ref, dgate_ref, gg_ref):
        i = pl.program_id(0)
        fv = f_ref[...]
        r = _rms(fv)
        fh = fv * r
        nf = fh * g_ref[...]
        gate = mod_ref[5:6, :]
        err = x1_ref[...] + gate * nf - tgt_ref[...]
        _acc(loss_ref, jnp.sum(_colsum(err * err), axis=1, keepdims=True) * jnp.ones((1, LANES), F32), i == 0)
        dy = err * (1.0 / d)
        dy_ref[...] = dy
        _acc(dgate_ref, _colsum(dy * nf), i % tps == 0)
        dn = dy * gate
        _acc(gg_ref, _colsum(dn * fh), i == 0)
        df_ref[...] = _rms_bwd(dn * g_ref[...], fh, r).astype(BF16)

    n_seq = t // seq
    return pl.pallas_call(
        body, name="post", grid=(t // tm,),
        out_shape=(jax.ShapeDtypeStruct((1, LANES), F32), jax.ShapeDtypeStruct((t, d), F32),
                   jax.ShapeDtypeStruct((t, d), BF16), jax.ShapeDtypeStruct((n_seq, 1, d), F32),
                   jax.ShapeDtypeStruct((1, d), F32)),
        in_specs=[_tok_spec(tm, d), _tok_spec(tm, d), _tok_spec(tm, d), _vec_spec(d), _mod_spec(tps, d)],
        out_specs=(pl.BlockSpec((1, LANES), lambda i: (0, 0)), _tok_spec(tm, d), _tok_spec(tm, d),
                   _seq_acc_spec(tps, d), _vec_spec(d)),
        compiler_params=_params(),
    )(f, x1, target, g_post, mod)


def _bwd_mid(dh2, dy, x1, mix, g_pre, g_post, mod, seq):
    t, d = x1.shape
    tm = EW_TILE
    tps = seq // tm

    def body(dh2_ref, dy_ref, x1_ref, mix_ref, gpre_ref, gpost_ref, mod_ref,
             dx1_ref, dmix_ref, dshift_ref, dscale_ref, dgate_ref, ggpre_ref, ggpost_ref):
        i = pl.program_id(0)
        seq_first = i % tps == 0
        dh = dh2_ref[...]
        x1 = x1_ref[...]
        r = _rms(x1)
        xh = x1 * r
        gpre = gpre_ref[...]
        _acc(dshift_ref, _colsum(dh), seq_first)
        _acc(dscale_ref, _colsum(dh * xh * gpre), seq_first)
        dn = dh * (1.0 + mod_ref[4:5, :])
        _acc(ggpre_ref, _colsum(dn * xh), i == 0)
        dx1 = dy_ref[...] + _rms_bwd(dn * gpre, xh, r)
        dx1_ref[...] = dx1
        mv = mix_ref[...]
        rm = _rms(mv)
        mh = mv * rm
        gpost = gpost_ref[...]
        _acc(dgate_ref, _colsum(dx1 * mh * gpost), seq_first)
        dnm = dx1 * mod_ref[2:3, :]
        _acc(ggpost_ref, _colsum(dnm * mh), i == 0)
        dmix_ref[...] = _rms_bwd(dnm * gpost, mh, rm).astype(BF16)

    n_seq = t // seq
    seq_sds = jax.ShapeDtypeStruct((n_seq, 1, d), F32)
    vec_sds = jax.ShapeDtypeStruct((1, d), F32)
    return pl.pallas_call(
        body, name="bwd_mid", grid=(t // tm,),
        out_shape=(jax.ShapeDtypeStruct((t, d), F32), jax.ShapeDtypeStruct((t, d), BF16),
                   seq_sds, seq_sds, seq_sds, vec_sds, vec_sds),
        in_specs=[_tok_spec(tm, d)] * 4 + [_vec_spec(d), _vec_spec(d), _mod_spec(tps, d)],
        out_specs=(_tok_spec(tm, d), _tok_spec(tm, d), _seq_acc_spec(tps, d), _seq_acc_spec(tps, d),
                   _seq_acc_spec(tps, d), _vec_spec(d), _vec_spec(d)),
        compiler_params=_params(),
    )(dh2, dy, x1, mix, g_pre, g_post, mod)


def _bwd_pre(dh1, dx1, x2, g_pre, mod, seq):
    t, d = x2.shape
    tm = EW_TILE
    tps = seq // tm

    def body(dh_ref, dx1_ref, x_ref, g_ref, mod_ref, gx_ref, dshift_ref, dscale_ref, gg_ref):
        i = pl.program_id(0)
        seq_first = i % tps == 0
        dh = dh_ref[...]
        xv = x_ref[...]
        r = _rms(xv)
        xh = xv * r
        g = g_ref[...]
        _acc(dshift_ref, _colsum(dh), seq_first)
        _acc(dscale_ref, _colsum(dh * xh * g), seq_first)
        dn = dh * (1.0 + mod_ref[1:2, :])
        _acc(gg_ref, _colsum(dn * xh), i == 0)
        gx_ref[...] = dx1_ref[...] + _rms_bwd(dn * g, xh, r)

    n_seq = t // seq
    seq_sds = jax.ShapeDtypeStruct((n_seq, 1, d), F32)
    return pl.pallas_call(
        body, name="bwd_pre", grid=(t // tm,),
        out_shape=(jax.ShapeDtypeStruct((t, d), F32), seq_sds, seq_sds, jax.ShapeDtypeStruct((1, d), F32)),
        in_specs=[_tok_spec(tm, d)] * 3 + [_vec_spec(d), _mod_spec(tps, d)],
        out_specs=(_tok_spec(tm, d), _seq_acc_spec(tps, d), _seq_acc_spec(tps, d), _vec_spec(d)),
        compiler_params=_params(),
    )(dh1, dx1, x2, g_pre, mod)


def _ffn_up(h2, wgu, tm, tn):
    t, d = h2.shape
    f = wgu.shape[1]

    def body(h_ref, w_ref, gu_ref, act_ref):
        h = h_ref[...]
        g = _dot_nt(h, w_ref[0])
        u = _dot_nt(h, w_ref[1])
        gu_ref[0] = g.astype(BF16)
        gu_ref[1] = u.astype(BF16)
        act_ref[...] = (g * jax.nn.sigmoid(g) * u).astype(BF16)

    return pl.pallas_call(
        body, name="ffn_up", grid=(t // tm, f // tn),
        out_shape=(jax.ShapeDtypeStruct((2, t, f), BF16), jax.ShapeDtypeStruct((t, f), BF16)),
        in_specs=[pl.BlockSpec((tm, d), lambda i, j: (i, 0)), pl.BlockSpec((2, tn, d), lambda i, j: (0, j, 0))],
        out_specs=(pl.BlockSpec((2, tm, tn), lambda i, j: (0, i, j)), pl.BlockSpec((tm, tn), lambda i, j: (i, j))),
        compiler_params=_params(),
    )(h2, wgu)


def _ffn_act_bwd(df, wd, gu, tm, tn):
    t, d = df.shape
    f = wd.shape[0]

    def body(df_ref, w_ref, gu_ref, dgu_ref):
        da = _dot_nt(df_ref[...], w_ref[...])
        g = gu_ref[0].astype(F32)
        u = gu_ref[1].astype(F32)
        s = jax.nn.sigmoid(g)
        silu = g * s
        dgu_ref[0] = (da * u * (s + silu * (1.0 - s))).astype(BF16)
        dgu_ref[1] = (da * silu).astype(BF16)

    return pl.pallas_call(
        body, name="ffn_act_bwd", grid=(t // tm, f // tn),
        out_shape=jax.ShapeDtypeStruct((2, t, f), BF16),
        in_specs=[pl.BlockSpec((tm, d), lambda i, j: (i, 0)), pl.BlockSpec((tn, d), lambda i, j: (j, 0)),
                  pl.BlockSpec((2, tm, tn), lambda i, j: (0, i, j))],
        out_specs=pl.BlockSpec((2, tm, tn), lambda i, j: (0, i, j)),
        compiler_params=_params(),
    )(df, wd, gu)


SIGN_BIT = 0x80000000
Q_SCALE = 1.0 / math.sqrt(HEAD_DIM)


def _split_dot(v, tri2):
    hi = v.astype(BF16)
    lo = (v - hi.astype(F32)).astype(BF16)
    return _dot_nn(jnp.concatenate([hi, lo], axis=1), tri2)


def _sb_tile(qs, k2, mask, ntri2, cur):
    z = _dot_nt(qs, k2)
    neg_abs = lax.bitcast_convert_type(lax.bitcast_convert_type(z, jnp.uint32) | jnp.uint32(SIGN_BIT), F32)
    sp = jnp.maximum(z, 0.0) + jnp.log(1.0 + jnp.exp(neg_abs))
    if mask is not None:
        sp = jnp.where(mask, sp, 0.0)
    w = jnp.exp(z + _split_dot(sp, ntri2) + cur)
    if mask is not None:
        w = jnp.where(mask, w, 0.0)
    return z, sp, w


def _stack_heads(v, lane, scale=None):
    if scale is not None:
        v = v * jnp.asarray(scale, v.dtype)
    zero = jnp.zeros_like(v)
    return jnp.concatenate([jnp.where(lane < HEAD_DIM, v, zero), jnp.where(lane >= HEAD_DIM, v, zero)], axis=0)


def _diag_mask(tq):
    row = lax.broadcasted_iota(jnp.int32, (2 * tq, tq), 0)
    col = lax.broadcasted_iota(jnp.int32, (2 * tq, tq), 1)
    return col < jnp.where(row >= tq, row - tq, row)


def _attn_fwd(proj, tri_after, n_seq, seq, ag_srcs, ag_out_shapes, ag_dests):
    t = proj.shape[0]
    tq = ATT_TILE
    n_pair = (proj.shape[1] // 4) // LANES
    n_ag, n_ag_out = len(ag_srcs), len(ag_out_shapes)
    n_steps = n_seq * n_pair

    def body(q_ref, k_ref, v_ref, tri_ref, *rest):
        ag_src, rest = rest[:n_ag], rest[n_ag:]
        o_ref, cs_ref = rest[:2]
        ag_out, rest = rest[2:2 + n_ag_out], rest[2 + n_ag_out:]
        oacc, cmat, carry = rest[:3]
        ag_start, ag_forward, ag_finish = _ag_phases(ag_dests, ag_src, ag_out, *rest[3:])
        step = pl.program_id(0) * n_pair + pl.program_id(1)
        pl.when(step == 0)(ag_start)
        pl.when(step == (5 * n_steps) // 8)(ag_forward)
        lane = lax.broadcasted_iota(jnp.int32, (1, LANES), 1)
        ntri2 = tri_ref[...]
        diag = _diag_mask(tq)

        def q_tile(qi, _):
            r0 = pl.multiple_of(qi * tq, tq)
            qs = _stack_heads(q_ref[pl.ds(r0, tq), :], lane, Q_SCALE)
            carry[...] = jnp.zeros_like(carry)
            cmat[...] = jnp.zeros_like(cmat)
            oacc[...] = jnp.zeros_like(oacc)

            def run_tiles(tiles):
                cur = carry[...]
                cm = cmat[...]
                pv = None
                for kb, mask in tiles:
                    c0 = pl.multiple_of(kb * tq, tq)
                    _, sp, w = _sb_tile(qs, k_ref[pl.ds(c0, tq), :], mask, ntri2, cur)
                    p = _dot_nn(w.astype(BF16), v_ref[pl.ds(c0, tq), :])
                    pv = p if pv is None else pv + p
                    cm = jnp.where(lane == kb, cur, cm)
                    cur = cur - jnp.sum(sp, axis=1, keepdims=True)
                oacc[...] += pv
                cmat[...] = cm
                carry[...] = cur

            odd = qi % 2

            @pl.when(odd == 0)
            def _():
                run_tiles([(qi, diag)])

            @pl.when(odd == 1)
            def _():
                run_tiles([(qi, diag), (qi - 1, None)])

            def pair(j, _):
                kb = qi - 1 - odd - 2 * j
                run_tiles([(kb, None), (kb - 1, None)])
                return 0

            lax.fori_loop(0, qi // 2, pair, 0)
            cs_ref[pl.ds(r0, tq), 0:LANES] = cmat[0:tq, :]
            cs_ref[pl.ds(r0, tq), LANES:2 * LANES] = cmat[tq:2 * tq, :]
            o_ref[pl.ds(r0, tq), :] = jnp.where(lane < HEAD_DIM, oacc[0:tq, :], oacc[tq:2 * tq, :]).astype(BF16)
            return 0

        lax.fori_loop(0, seq // tq, q_tile, 0)
        pl.when(step == n_steps - 1)(ag_finish)

    blk = lambda off: pl.BlockSpec((seq, LANES), lambda b, p: (b, off + p))
    any_spec = pl.BlockSpec(memory_space=pl.ANY)
    return pl.pallas_call(
        body, name="attn_fwd", grid=(n_seq, n_pair),
        out_shape=(jax.ShapeDtypeStruct((t, n_pair * LANES), BF16),
                   jax.ShapeDtypeStruct((t, n_pair * 2 * LANES), F32), *ag_out_shapes),
        in_specs=[blk(0), blk(n_pair), blk(2 * n_pair), pl.BlockSpec((2 * tq, tq), lambda b, p: (0, 0))]
        + [any_spec] * n_ag,
        out_specs=(pl.BlockSpec((seq, LANES), lambda b, p: (b, p)),
                   pl.BlockSpec((seq, 2 * LANES), lambda b, p: (b, p)), *([any_spec] * n_ag_out)),
        scratch_shapes=[pltpu.VMEM((2 * tq, LANES), F32), pltpu.VMEM((2 * tq, LANES), F32),
                        pltpu.VMEM((2 * tq, 1), F32)] + _ag_scratch(n_ag),
        compiler_params=_params(),
    )(proj, proj, proj, tri_after, *ag_srcs)


def _attn_bwd(proj, dcat, cstats, tri_after, tri_incl, n_seq, seq, rs_sends):
    t = proj.shape[0]
    tq = ATT_TILE
    width = proj.shape[1] // 4
    n_pair = width // LANES
    n_rs = len(rs_sends)
    rs_shapes = [r.shape for r in rs_sends]
    n_steps = n_seq * n_pair

    def body(q_ref, k_ref, v_ref, do_ref, cs_ref, tria_ref, trii_ref, *rest):
        rs_src, rest = rest[:n_rs], rest[n_rs:]
        out_ref = rest[0]
        rs_dst, rest = rest[1:1 + n_rs], rest[1 + n_rs:]
        dq_acc, dk_acc, dv_acc, ecarry = rest[:4]
        rs_start, rs_finish = _rs_chip_phases(rs_shapes, rs_src, rs_dst, *rest[4:])
        step = pl.program_id(0) * n_pair + pl.program_id(1)
        pl.when(step == 0)(rs_start)
        lane = lax.broadcasted_iota(jnp.int32, (1, LANES), 1)
        ntri2 = tria_ref[...]
        tri_i2 = trii_ref[...]
        diag = _diag_mask(tq)
        dk_acc[...] = jnp.zeros_like(dk_acc)
        dv_acc[...] = jnp.zeros_like(dv_acc)

        def q_tile(qi, _):
            r0 = pl.multiple_of(qi * tq, tq)
            qs = _stack_heads(q_ref[pl.ds(r0, tq), :], lane, Q_SCALE)
            dos = _stack_heads(do_ref[pl.ds(r0, tq), :], lane)
            cs = jnp.concatenate([cs_ref[pl.ds(r0, tq), 0:LANES], cs_ref[pl.ds(r0, tq), LANES:2 * LANES]], axis=0)
            ecarry[...] = jnp.zeros_like(ecarry)
            dq_acc[...] = jnp.zeros_like(dq_acc)

            def run_tiles(tiles):
                ec = ecarry[...]
                dq = None
                for kb, mask in tiles:
                    c0 = pl.multiple_of(kb * tq, tq)
                    k2 = k_ref[pl.ds(c0, tq), :]
                    v2 = v_ref[pl.ds(c0, tq), :]
                    cur = jnp.sum(jnp.where(lane == kb, cs, 0.0), axis=1, keepdims=True)
                    z, sp, w = _sb_tile(qs, k2, mask, ntri2, cur)
                    ee = w * _dot_nt(dos, v2)
                    einc = _split_dot(ee, tri_i2) + ec
                    dz = ee - jnp.exp(z - sp) * einc
                    if mask is not None:
                        dz = jnp.where(mask, dz, 0.0)
                    dzb = dz.astype(BF16)
                    p = _dot_nn(dzb, k2)
                    dq = p if dq is None else dq + p
                    dk_acc[pl.ds(c0, tq), :] += _dot_tn(dzb, qs)
                    dv_acc[pl.ds(c0, tq), :] += _dot_tn(w.astype(BF16), dos)
                    ec = ec + jnp.sum(ee, axis=1, keepdims=True)
                dq_acc[...] += dq
                ecarry[...] = ec

            def pair(j, _):
                run_tiles([(2 * j, None), (2 * j + 1, None)])
                return 0

            lax.fori_loop(0, qi // 2, pair, 0)
            odd = qi % 2

            @pl.when(odd == 0)
            def _():
                run_tiles([(qi, diag)])

            @pl.when(odd == 1)
            def _():
                run_tiles([(qi - 1, None), (qi, diag)])

            dq = jnp.where(lane < HEAD_DIM, dq_acc[0:tq, :], dq_acc[tq:2 * tq, :])
            out_ref[0, pl.ds(r0, tq), :] = (dq * Q_SCALE).astype(BF16)
            return 0

        lax.fori_loop(0, seq // tq, q_tile, 0)
        out_ref[1] = dk_acc[...].astype(BF16)
        out_ref[2] = dv_acc[...].astype(BF16)
        pl.when(step == n_steps - 1)(rs_finish)

    blk = lambda off: pl.BlockSpec((seq, LANES), lambda b, p: (b, off + p))
    tri_spec = pl.BlockSpec((2 * tq, tq), lambda b, p: (0, 0))
    any_spec = pl.BlockSpec(memory_space=pl.ANY)
    return pl.pallas_call(
        body, name="attn_bwd", grid=(n_seq, n_pair),
        out_shape=(jax.ShapeDtypeStruct((4, t, width), BF16), *_rs_chip_out(rs_sends)),
        in_specs=[blk(0), blk(n_pair), blk(2 * n_pair), pl.BlockSpec((seq, LANES), lambda b, p: (b, p)),
                  pl.BlockSpec((seq, 2 * LANES), lambda b, p: (b, p)), tri_spec, tri_spec] + [any_spec] * n_rs,
        out_specs=(pl.BlockSpec((3, seq, LANES), lambda b, p: (0, b, p)), *([any_spec] * n_rs)),
        scratch_shapes=[pltpu.VMEM((2 * tq, LANES), F32), pltpu.VMEM((seq, LANES), F32),
                        pltpu.VMEM((seq, LANES), F32), pltpu.VMEM((2 * tq, 1), F32)] + _rs_chip_scratch(rs_sends),
        compiler_params=_params(),
    )(proj, proj, proj, dcat, cstats, tri_after, tri_incl, *rs_sends)


def _window_terms(g, rows):
    win = jnp.where(g == 0, POOL_WINDOWS[0], jnp.where(g == 1, POOL_WINDOWS[1],
                    jnp.where(g == 2, POOL_WINDOWS[2], POOL_WINDOWS[3])))
    cnt = jnp.minimum(rows + 1, win).astype(F32)
    return win, cnt


def _window_sum(v, g, rows, forward):
    s_len = v.shape[0]
    sums = []
    s = v
    for step in range(len(POOL_WINDOWS)):
        sh = 1 << step
        if forward:
            shifted = jnp.where(rows < s_len - sh, pltpu.roll(s, s_len - sh, axis=0), 0.0)
        else:
            shifted = jnp.where(rows >= sh, pltpu.roll(s, sh, axis=0), 0.0)
        s = s + shifted
        sums.append(s)
    return jnp.where(g == 0, sums[0], jnp.where(g == 1, sums[1], jnp.where(g == 2, sums[2], sums[3])))


def _pooled(u, g, rows):
    _, cnt = _window_terms(g, rows)
    return _window_sum(u, g, rows, forward=False) / cnt - u


def _pool_fwd(proj, w_pool, pool_scale, n_seq, seq):
    t = proj.shape[0]
    n_grp = len(POOL_WINDOWS)
    u_off = 3 * (proj.shape[1] // 4) // LANES

    def body(u_ref, w_ref, s_ref, o_ref):
        g = pl.program_id(1)
        rows = lax.broadcasted_iota(jnp.int32, (seq, 1), 0)
        pooled = _pooled(u_ref[...].astype(F32), g, rows)
        y = _dot_nn(pooled.astype(BF16), w_ref[...].astype(BF16))
        o_ref[...] = (y * s_ref[...]).astype(BF16)

    return pl.pallas_call(
        body, name="pool_fwd", grid=(n_seq, n_grp),
        out_shape=jax.ShapeDtypeStruct((t, n_grp * POOL_GROUP_DIM), BF16),
        in_specs=[pl.BlockSpec((seq, LANES), lambda b, g: (b, u_off + g)),
                  pl.BlockSpec((None, POOL_GROUP_DIM, POOL_GROUP_DIM), lambda b, g: (g, 0, 0)),
                  pl.BlockSpec((1, POOL_GROUP_DIM), lambda b, g: (0, g))],
        out_specs=pl.BlockSpec((seq, LANES), lambda b, g: (b, g)),
        compiler_params=_params(),
    )(proj, w_pool, pool_scale)


def _pool_bwd(proj, dcat, w_pool, pool_scale, dqkv, n_seq, seq):
    n_grp = len(POOL_WINDOWS)
    width = proj.shape[1] // 4
    u_off = 3 * width // LANES
    dp_off = width // LANES

    def body(u_ref, dp_ref, w_ref, s_ref, alias_ref, du_ref, gw_ref, gs_ref):
        del alias_ref
        g = pl.program_id(0)
        b = pl.program_id(1)
        rows = lax.broadcasted_iota(jnp.int32, (seq, 1), 0)
        pooled = _pooled(u_ref[...].astype(F32), g, rows)
        pb = pooled.astype(BF16)
        wb = w_ref[...].astype(BF16)
        z = _dot_nn(pb, wb)
        dp = dp_ref[...].astype(F32)
        _acc(gs_ref, _colsum(dp * z), b == 0)
        dys = (dp * s_ref[...]).astype(BF16)
        _acc(gw_ref, _dot_tn(pb, dys), b == 0)
        dpooled = _dot_nt(dys, wb)
        _, cnt = _window_terms(g, rows)
        du = _window_sum(dpooled / cnt, g, rows, forward=True) - dpooled
        du_ref[...] = du.astype(BF16)

    t = proj.shape[0]
    return pl.pallas_call(
        body, name="pool_bwd", grid=(n_grp, n_seq),
        out_shape=(jax.ShapeDtypeStruct(dqkv.shape, BF16),
                   jax.ShapeDtypeStruct((n_grp, POOL_GROUP_DIM, POOL_GROUP_DIM), F32),
                   jax.ShapeDtypeStruct((1, n_grp * POOL_GROUP_DIM), F32)),
        in_specs=[pl.BlockSpec((seq, LANES), lambda g, b: (b, u_off + g)),
                  pl.BlockSpec((seq, LANES), lambda g, b: (b, dp_off + g)),
                  pl.BlockSpec((None, POOL_GROUP_DIM, POOL_GROUP_DIM), lambda g, b: (g, 0, 0)),
                  pl.BlockSpec((1, POOL_GROUP_DIM), lambda g, b: (0, g)),
                  pl.BlockSpec(memory_space=pl.ANY)],
        out_specs=(pl.BlockSpec((None, seq, LANES), lambda g, b: (3, b, g)),
                   pl.BlockSpec((None, POOL_GROUP_DIM, POOL_GROUP_DIM), lambda g, b: (g, 0, 0)),
                   pl.BlockSpec((1, POOL_GROUP_DIM), lambda g, b: (0, g))),
        input_output_aliases={4: 0},
        compiler_params=_params(),
    )(proj, dcat, w_pool, pool_scale, dqkv)


def _cond_fwd(c_all, w_cond, b_cols):
    n, _ = c_all.shape
    cols = w_cond.shape[1]

    def body(c_ref, w_ref, b_ref, o_ref):
        cv = c_ref[...]
        a = cv * jax.nn.sigmoid(cv)
        o_ref[...] = jnp.dot(a, w_ref[...], preferred_element_type=F32,
                             precision=lax.Precision.HIGHEST) + b_ref[...]

    return pl.pallas_call(
        body, name="cond_fwd", out_shape=jax.ShapeDtypeStruct((n, cols), F32),
        compiler_params=_params(),
    )(c_all, w_cond, b_cols)


def _cond_bwd(c_all, dmod_all, dmod_cols):
    n, d = c_all.shape
    cols = dmod_cols.shape[1]

    def body(c_ref, dm_ref, dmc_ref, gw_ref, gb_ref):
        cv = c_ref[...]
        a = cv * jax.nn.sigmoid(cv)
        gw_ref[...] = lax.dot_general(a, dmc_ref[...], (((0,), (0,)), ((), ())),
                                      preferred_element_type=F32, precision=lax.Precision.HIGHEST)
        gb_ref[...] = _colsum(dm_ref[...])

    return pl.pallas_call(
        body, name="cond_bwd",
        out_shape=(jax.ShapeDtypeStruct((d, cols), F32), jax.ShapeDtypeStruct((1, dmod_all.shape[1]), F32)),
        compiler_params=_params(),
    )(c_all, dmod_all, dmod_cols)


def _adamw_math(w, g, m, v):
    m = ADAM_B1 * m + (1.0 - ADAM_B1) * g
    v = ADAM_B2 * v + (1.0 - ADAM_B2) * (g * g)
    m_hat = m / (1.0 - ADAM_B1 ** ADAM_STEP)
    v_hat = v / (1.0 - ADAM_B2 ** ADAM_STEP)
    delta = -ADAM_LR * (m_hat / (jnp.sqrt(v_hat) + ADAM_EPS) + ADAM_WD * w)
    return delta, m, v


def _adamw(w, g, m, v, rows, name):
    r, cdim = w.shape

    def body(w_ref, g_ref, m_ref, v_ref, d_ref, nm_ref, nv_ref):
        d_ref[...], nm_ref[...], nv_ref[...] = _adamw_math(w_ref[...], g_ref[...], m_ref[...], v_ref[...])

    spec = pl.BlockSpec((rows, cdim), lambda i: (i, 0))
    sds = jax.ShapeDtypeStruct((r, cdim), F32)
    return pl.pallas_call(
        body, name=name, grid=(r // rows,), out_shape=(sds, sds, sds),
        in_specs=[spec] * 4, out_specs=(spec, spec, spec), compiler_params=_params(),
    )(w, g, m, v)


def _adamw_small(ws, gparts, ms, vs, name):
    n = len(ws)

    def body(*refs):
        w_r, g_r, m_r, v_r = refs[:n], refs[n:2 * n], refs[2 * n:3 * n], refs[3 * n:4 * n]
        outs = refs[4 * n:]
        for i in range(n):
            g = g_r[i][0]
            for dev in range(1, g_r[i].shape[0]):
                g = g + g_r[i][dev]
            delta, m, v = _adamw_math(w_r[i][...], g, m_r[i][...], v_r[i][...])
            outs[i][...] = g
            outs[n + i][...] = delta
            outs[2 * n + i][...] = m
            outs[3 * n + i][...] = v

    sds = [jax.ShapeDtypeStruct(w.shape, F32) for w in ws]
    return pl.pallas_call(
        body, name=name, out_shape=tuple(sds * 4), compiler_params=_params(),
    )(*ws, *gparts, *ms, *vs)


def kernel(x, c, w_cond, b_cond, g_mix_pre, g_mix_post, w_in, w_pool, pool_scale, w_out, g_ffn_pre, g_ffn_post, w_gate, w_up, w_down, loss_target, m_w_cond, m_b_cond, m_g_mix_pre, m_g_mix_post, m_w_in, m_w_pool, m_pool_scale, m_w_out, m_g_ffn_pre, m_g_ffn_post, m_w_gate, m_w_up, m_w_down, v_w_cond, v_b_cond, v_g_mix_pre, v_g_mix_post, v_w_in, v_w_pool, v_pool_scale, v_w_out, v_g_ffn_pre, v_g_ffn_post, v_w_gate, v_w_up, v_w_down):
    n_seq, seq, d = x.shape
    t = n_seq * seq
    xi, yi, ci = _mesh_pos()
    me = 4 * xi + 2 * yi + ci
    x2 = x.reshape(t, d)
    tgt2 = loss_target.reshape(t, d)
    in_rows = w_in.shape[2]
    out_rows = w_out.shape[1]
    ff_rows = w_gate.shape[2]
    ff = N_DEV * ff_rows
    cond_cols = w_cond.shape[2]

    win_t = w_in[0].T.astype(BF16)
    wout_s = w_out[0].astype(BF16)
    wg_t = w_gate[0].T.astype(BF16)
    wu_t = w_up[0].T.astype(BF16)
    wd_s = w_down[0].astype(BF16)
    c_all, win_g = _all_gather(
        [c, win_t],
        [jax.ShapeDtypeStruct((N_DEV, n_seq, d), F32), jax.ShapeDtypeStruct((N_DEV, in_rows, d), BF16)],
        [(0, ()), (1, ())], "ag_c_win")
    c_all = c_all.reshape(N_DEV * n_seq, d)
    win_full = win_g.reshape(N_DEV * in_rows, d)

    b_cols = lax.dynamic_slice_in_dim(b_cond, me * cond_cols, cond_cols, axis=1)
    mod_cols = _cond_fwd(c_all, w_cond[0], b_cols)
    (mod_g,) = _all_gather([mod_cols], [jax.ShapeDtypeStruct((N_DEV,) + mod_cols.shape, F32)], [(0, ())], "ag_mod")
    mod_mine = lax.dynamic_slice_in_dim(mod_g, me * n_seq, n_seq, axis=1)
    mod = jnp.transpose(mod_mine, (1, 0, 2)).reshape(n_seq, N_MOD, d)

    h1 = _pre_mix(x2, g_mix_pre, mod, seq)
    proj = _matmul(h1, win_full, "nt", BF16, 1024, 512, d, "proj")
    tq = ATT_TILE
    ids = jnp.arange(tq)
    tri_after = jnp.tile(-(ids[:, None] >= ids[None, :]).astype(BF16), (2, 1))
    tri_incl = jnp.tile((ids[:, None] <= ids[None, :]).astype(BF16), (2, 1))
    attn, cstats, wout_g, wgu_g, wd_g = _attn_fwd(
        proj, tri_after, n_seq, seq, [wout_s, wg_t, wu_t, wd_s],
        [jax.ShapeDtypeStruct((N_DEV, out_rows, d), BF16), jax.ShapeDtypeStruct((2, N_DEV, ff_rows, d), BF16),
         jax.ShapeDtypeStruct((N_DEV, ff_rows, d), BF16)],
        [(0, ()), (1, (0,)), (1, (1,)), (2, ())])
    wout_full = wout_g.reshape(N_DEV * out_rows, d)
    wgu_full = wgu_g.reshape(2, ff, d)
    wd_full = wd_g.reshape(ff, d)
    pool = _pool_fwd(proj, w_pool[0], pool_scale, n_seq, seq)
    cat = jnp.stack([attn, pool])
    mix = _matmul(cat, wout_full.reshape(2, d // 2, d), "nn", F32, 512, d, d // 2, "mix")
    x1, h2 = _mid(mix, x2, g_mix_post, g_ffn_pre, mod, seq)
    gu, act = _ffn_up(h2, wgu_full, 512, ff // 2)
    f = _matmul(act, wd_full, "nn", F32, 512, d, ff, "ffn_down")
    loss_sum, dy, df, dgate_f, gg_ffn_post = _post(f, x1, tgt2, g_ffn_post, mod, seq)
    loss = lax.psum(loss_sum[0, 0] * (0.5 / d), ("x", "y", "c"))

    dgu = _ffn_act_bwd(df, wd_full, gu, 512, ff // 2)
    gwd = _matmul(act, df, "tn", F32, ff // 2, d, 1024, "grad_w_down")
    gwgu = _matmul(dgu, h2, "tn", F32, ff // 2, d, 1024, "grad_w_gate_up")
    dh2 = _matmul(dgu, wgu_full, "nn", F32, 512, d, ff, "dh2")
    dx1, dmix, dshift_f, dscale_f, dgate_m, gg_ffn_pre, gg_mix_post = _bwd_mid(
        dh2, dy, x1, mix, g_ffn_pre, g_mix_post, mod, seq)
    dcat = _matmul(dmix, wout_full, "nt", BF16, 1024, 512, d, "dcat")
    gwout = _matmul(cat, dmix, "tn", F32, d // 2, d, 1024, "grad_w_out")
    ffn_f32, ffn_bf16 = _rs_core_stage(
        [gwgu.reshape(2, N_DEV, ff_rows, d), gwd.reshape(1, N_DEV, ff_rows, d),
         gwout.reshape(1, N_DEV, out_rows, d)], "ffn")
    dqkv, rv_wgu, rv_wd, rv_wout = _attn_bwd(proj, dcat, cstats, tri_after, tri_incl, n_seq, seq, ffn_bf16)
    dproj, gw_pool, gs_pool = _pool_bwd(proj, dcat, w_pool[0], pool_scale, dqkv, n_seq, seq)
    gwin = _matmul(dproj, h1, "tn", F32, d // 2, d, 1024, "grad_w_in")
    in_f32, in_bf16 = _rs_core_stage([gwin.reshape(1, N_DEV, in_rows, d)], "in")
    (rv_win,) = _rs_chip_exchange(in_bf16, "rs_in_xchg_chip")
    dh1 = _matmul(dproj, win_full.reshape(4, d // 2, d), "nn", F32, 512, d, d // 2, "dh1")
    grad_x, dshift_m, dscale_m, gg_mix_pre = _bwd_pre(dh1, dx1, x2, g_mix_pre, mod, seq)

    r_wgu = _rs_final(ffn_f32[0], rv_wgu, "rs_final_gate_up")
    r_wd = _rs_final(ffn_f32[1], rv_wd, "rs_final_down")
    r_wout = _rs_final(ffn_f32[2], rv_wout, "rs_final_out")
    r_win = _rs_final(in_f32[0], rv_win, "rs_final_in")
    grad_w_in = r_win[0].T
    grad_w_out = r_wout[0]
    grad_w_gate = r_wgu[0].T
    grad_w_up = r_wgu[1].T
    grad_w_down = r_wd[0]

    dmod = jnp.concatenate([dshift_m, dscale_m, dgate_m, dshift_f, dscale_f, dgate_f], axis=1)
    small = jnp.concatenate([gg_mix_pre, gg_mix_post, gg_ffn_pre, gg_ffn_post,
                             jnp.pad(gs_pool, ((0, 0), (0, d - gs_pool.shape[1]))),
                             jnp.zeros((3, d), F32),
                             gw_pool.reshape(-1, d), dmod.reshape(n_seq * N_MOD, d)], axis=0)
    n_gw = gw_pool.size // d
    (small_g,) = _all_gather([small], [jax.ShapeDtypeStruct((N_DEV,) + small.shape, F32)], [(0, ())], "ag_small")
    dmod_all = small_g[:, 8 + n_gw:, :].reshape(N_DEV * n_seq, N_MOD * d)
    dmod_cols = lax.dynamic_slice_in_dim(dmod_all, me * cond_cols, cond_cols, axis=1)
    grad_w_cond, grad_b_cond = _cond_bwd(c_all, dmod_all, dmod_cols)

    small_ws = [g_mix_pre, g_mix_post, g_ffn_pre, g_ffn_post, pool_scale, w_pool.reshape(-1, POOL_GROUP_DIM)]
    small_ms = [m_g_mix_pre, m_g_mix_post, m_g_ffn_pre, m_g_ffn_post, m_pool_scale, m_w_pool.reshape(-1, POOL_GROUP_DIM)]
    small_vs = [v_g_mix_pre, v_g_mix_post, v_g_ffn_pre, v_g_ffn_post, v_pool_scale, v_w_pool.reshape(-1, POOL_GROUP_DIM)]
    small_gparts = [small_g[:, 0:1, :], small_g[:, 1:2, :], small_g[:, 2:3, :], small_g[:, 3:4, :],
                    small_g[:, 4:5, :pool_scale.shape[1]],
                    small_g[:, 8:8 + n_gw, :].reshape(N_DEV, -1, POOL_GROUP_DIM)]
    so = _adamw_small(small_ws, small_gparts, small_ms, small_vs, "adamw_small")
    ns = len(small_ws)
    sg, sdl, sm, sv = so[:ns], so[ns:2 * ns], so[2 * ns:3 * ns], so[3 * ns:]
    pool_shape = w_pool.shape
    fix = lambda lst: [lst[0], lst[1], lst[2], lst[3], lst[4], lst[5].reshape(pool_shape)]
    sg, sdl, sm, sv = fix(sg), fix(sdl), fix(sm), fix(sv)

    def big(w, g, m, v, rows, name):
        dl, nm, nv = _adamw(w[0], g, m[0], v[0], rows, name)
        return g[None], dl[None], nm[None], nv[None]

    o_cond = big(w_cond, grad_w_cond, m_w_cond, v_w_cond, 256, "adamw_w_cond")
    o_bcond = _adamw(b_cond, grad_b_cond, m_b_cond, v_b_cond, 1, "adamw_b_cond")
    o_bcond = (grad_b_cond,) + tuple(o_bcond)
    o_in = big(w_in, grad_w_in, m_w_in, v_w_in, 256, "adamw_w_in")
    o_out = big(w_out, grad_w_out, m_w_out, v_w_out, out_rows, "adamw_w_out")
    o_gate = big(w_gate, grad_w_gate, m_w_gate, v_w_gate, 256, "adamw_w_gate")
    o_up = big(w_up, grad_w_up, m_w_up, v_w_up, 256, "adamw_w_up")
    o_down = big(w_down, grad_w_down, m_w_down, v_w_down, ff_rows, "adamw_w_down")

    def pick(k):
        small_k = [sg, sdl, sm, sv][k]
        return [o_cond[k], o_bcond[k], small_k[0], small_k[1], o_in[k], small_k[5], small_k[4], o_out[k],
                small_k[2], small_k[3], o_gate[k], o_up[k], o_down[k]]

    return (loss, grad_x.reshape(n_seq, seq, d), *pick(0), *pick(1), *pick(2), *pick(3))
```

```python
import functools
import math

import jax
import jax.numpy as jnp
from jax import lax
from jax.experimental import pallas as pl
from jax.experimental.pallas import tpu as pltpu

F32 = jnp.float32
BF16 = jnp.bfloat16
MESH = pl.DeviceIdType.MESH

N_DEV = 8
HEAD_DIM = 64
LANES = 128
POOL_WINDOWS = (2, 4, 8, 16)
POOL_GROUP_DIM = 128
N_MOD = 6
EPS = 1e-6
ATT_TILE = 256
VMEM_LIMIT = 56 * 1024 * 1024

ADAM_LR = 0.001
ADAM_B1 = 0.9
ADAM_B2 = 0.999
ADAM_EPS = 1e-08
ADAM_WD = 0.01
ADAM_STEP = 10


def _params(**kw):
    return pltpu.CompilerParams(vmem_limit_bytes=VMEM_LIMIT, **kw)


def _dot_nn(a, b):
    return jnp.dot(a, b, preferred_element_type=F32)


def _dot_nt(a, b):
    return lax.dot_general(a, b, (((1,), (1,)), ((), ())), preferred_element_type=F32)


def _dot_tn(a, b):
    return lax.dot_general(a, b, (((0,), (0,)), ((), ())), preferred_element_type=F32)


def _mesh_pos():
    return lax.axis_index("x"), lax.axis_index("y"), lax.axis_index("c")


def _ag_phases(dests, src, outs, send_sems, recv_sems, local_sems):
    n = len(src)
    x, y, c = _mesh_pos()
    me, sibling = (x, y, c), (x, y, 1 - c)
    chips = [(1 - x, y), (x, 1 - y), (1 - x, 1 - y)]

    def slot(i, dev):
        oi, prefix = dests[i]
        px, py, pc = dev
        return outs[oi].at[prefix + (4 * px + 2 * py + pc,)]

    def copy(i, k, block, to, from_src=False):
        return pltpu.make_async_remote_copy(
            src_ref=src[i] if from_src else slot(i, block), dst_ref=slot(i, block),
            send_sem=send_sems.at[i, k], recv_sem=recv_sems.at[i, k],
            device_id=to, device_id_type=MESH)

    def mine(i):
        return pltpu.make_async_copy(src[i], slot(i, me), local_sems.at[i])

    def first(i):
        return [copy(i, 0, me, sibling, from_src=True)] + [
            copy(i, 1 + j, me, (*chip, c), from_src=True) for j, chip in enumerate(chips)]

    def passed(i, j):
        return copy(i, 4 + j, (*chips[j], c), sibling)

    def start():
        for i in range(n):
            mine(i).start()
        for i in range(n):
            for cp in first(i):
                cp.start()

    def forward():
        for j, chip in enumerate(chips):
            for i in range(n):
                copy(i, 1 + j, (*chip, c), me).wait_recv()
                passed(i, j).start()

    def finish():
        for i in range(n):
            copy(i, 0, sibling, me).wait_recv()
            for j, chip in enumerate(chips):
                copy(i, 4 + j, (*chip, 1 - c), me).wait_recv()
        for i in range(n):
            for cp in first(i) + [passed(i, j) for j in range(3)]:
                cp.wait_send()
            mine(i).wait()

    return start, forward, finish


def _ag_scratch(n):
    return [pltpu.SemaphoreType.DMA((n, 7)), pltpu.SemaphoreType.DMA((n, 7)), pltpu.SemaphoreType.DMA((n,))]


def _all_gather(srcs, out_shapes, dests, name):
    n = len(srcs)

    def body(*refs):
        src = refs[:n]
        outs = refs[n:n + len(out_shapes)]
        start, forward, finish = _ag_phases(dests, src, outs, *refs[n + len(out_shapes):])
        start()
        forward()
        finish()

    any_spec = pl.BlockSpec(memory_space=pl.ANY)
    return pl.pallas_call(
        body, name=name,
        out_shape=tuple(out_shapes),
        in_specs=[any_spec] * n,
        out_specs=tuple([any_spec] * len(out_shapes)),
        scratch_shapes=_ag_scratch(n),
    )(*srcs)


def _rs_phases(shapes, src, dst, send_sems, recv_sems):
    x, y, c = _mesh_pos()

    def copies():
        out = []
        n = 0
        for i, shp in enumerate(shapes):
            for m in range(shp[0]):
                for k in range(1, N_DEV):
                    px, py, pc = x ^ (k >> 2), y ^ ((k >> 1) & 1), c ^ (k & 1)
                    out.append(pltpu.make_async_remote_copy(
                        src_ref=src[i].at[m, 4 * px + 2 * py + pc], dst_ref=dst[i].at[m, k - 1],
                        send_sem=send_sems.at[n], recv_sem=recv_sems.at[n],
                        device_id=(px, py, pc), device_id_type=MESH))
                    n += 1
        return out

    def start():
        for cp in copies():
            cp.start()

    def finish():
        for cp in copies():
            cp.wait_send()
        for cp in copies():
            cp.wait_recv()

    return start, finish


def _rs_out(sends):
    return [jax.ShapeDtypeStruct((s.shape[0], N_DEV - 1) + s.shape[2:], s.dtype) for s in sends]


def _rs_scratch(sends):
    total = sum((N_DEV - 1) * s.shape[0] for s in sends)
    return [pltpu.SemaphoreType.DMA((total,)), pltpu.SemaphoreType.DMA((total,))]


def _rs_final(mine, recv, name):
    m_n, _, r, cdim = mine.shape
    x, y, c = _mesh_pos()
    me = jnp.reshape(4 * x + 2 * y + c, (1,)).astype(jnp.int32)

    def body(me_ref, p_ref, r_ref, o_ref):
        del me_ref
        s = p_ref[...]
        for k in range(N_DEV - 1):
            s = s + r_ref[k].astype(F32)
        o_ref[...] = s

    return pl.pallas_call(
        body, name=name, out_shape=jax.ShapeDtypeStruct((m_n, r, cdim), F32),
        grid_spec=pltpu.PrefetchScalarGridSpec(
            num_scalar_prefetch=1, grid=(m_n,),
            in_specs=[pl.BlockSpec((None, None, r, cdim), lambda m, s: (m, s[0], 0, 0)),
                      pl.BlockSpec((None, N_DEV - 1, r, cdim), lambda m, s: (m, 0, 0, 0))],
            out_specs=pl.BlockSpec((None, r, cdim), lambda m, s: (m, 0, 0))),
        compiler_params=_params(),
    )(me, mine, recv)


def _matmul(a, b, mode, out_dtype, tm, tn, tk, name, bf16_copy=False, rs_sends=()):
    ga = a.shape[0] if a.ndim == 3 else None
    gb = b.shape[0] if b.ndim == 3 else None
    a2, b2 = a.shape[-2:], b.shape[-2:]
    if mode == "nn":
        (m, k), n = a2, b2[1]
    elif mode == "nt":
        (m, k), n = a2, b2[0]
    else:
        (k, m), n = a2, b2[1]
    assert m % tm == 0 and n % tn == 0 and k % tk == 0, (name, m, n, k)
    nk = k // tk
    g_n = ga or 1
    batch_out = mode == "tn" and ga is not None
    n_red = nk if batch_out else nk * g_n
    dot = {"nn": _dot_nn, "nt": _dot_nt, "tn": _dot_tn}[mode]
    acc_in_out = out_dtype == F32

    n_rs = len(rs_sends)
    rs_shapes = [r.shape for r in rs_sends]
    n_out = 2 if bf16_copy else 1
    assert not bf16_copy or acc_in_out

    def body(a_ref, b_ref, *rest):
        rs_src, rest = rest[:n_rs], rest[n_rs:]
        o_ref = rest[0]
        copy_ref = rest[1] if bf16_copy else None
        rs_dst, scratch = rest[n_out:n_out + n_rs], rest[n_out + n_rs:]
        if n_rs:
            rs_start, rs_finish = _rs_phases(rs_shapes, rs_src, rs_dst, *scratch[-2:])
            first = functools.reduce(jnp.logical_and, [pl.program_id(ax) == 0 for ax in range(4)])
            last = functools.reduce(jnp.logical_and, [pl.program_id(ax) == grid[ax] - 1 for ax in range(4)])
            pl.when(first)(rs_start)
        p = dot(a_ref[...], b_ref[...])
        kk = pl.program_id(3) if batch_out else pl.program_id(2) * nk + pl.program_id(3)
        if n_red == 1:
            o_ref[...] = p.astype(out_dtype)
            if bf16_copy:
                copy_ref[...] = p.astype(BF16)
        else:
            acc = o_ref if acc_in_out else scratch[0]

            @pl.when(kk == 0)
            def _():
                acc[...] = p

            @pl.when(kk > 0)
            def _():
                acc[...] += p

            @pl.when(kk == n_red - 1)
            def _():
                if not acc_in_out:
                    o_ref[...] = acc[...].astype(out_dtype)
                if bf16_copy:
                    copy_ref[...] = acc[...].astype(BF16)

        if n_rs:
            pl.when(last)(rs_finish)

    def order(ids):
        return ids if batch_out else (ids[2], ids[0], ids[1], ids[3])

    def a_idx(*ids):
        g, i, j, kq = order(ids)
        blk = {"nn": (i, kq), "nt": (i, kq), "tn": (kq, i)}[mode]
        return (g,) + blk if ga is not None else blk

    def b_idx(*ids):
        g, i, j, kq = order(ids)
        blk = {"nn": (kq, j), "nt": (j, kq), "tn": (kq, j)}[mode]
        return (g,) + blk if gb is not None else blk

    def o_idx(*ids):
        g, i, j, kq = order(ids)
        return (g, i, j) if batch_out else (i, j)

    a_blk = {"nn": (tm, tk), "nt": (tm, tk), "tn": (tk, tm)}[mode]
    b_blk = {"nn": (tk, tn), "nt": (tn, tk), "tn": (tk, tn)}[mode]
    if ga is not None:
        a_blk = (None,) + a_blk
    if gb is not None:
        b_blk = (None,) + b_blk
    if batch_out:
        out_shape = jax.ShapeDtypeStruct((g_n, m, n), out_dtype)
        o_blk = (None, tm, tn)
        grid = (g_n, m // tm, n // tn, nk)
    else:
        out_shape = jax.ShapeDtypeStruct((m, n), out_dtype)
        o_blk = (tm, tn)
        grid = (m // tm, n // tn, g_n, nk)
    scratch = [] if (acc_in_out or n_red == 1) else [pltpu.VMEM((tm, tn), F32)]
    any_spec = pl.BlockSpec(memory_space=pl.ANY)
    out_shapes = [out_shape] + ([jax.ShapeDtypeStruct(out_shape.shape, BF16)] if bf16_copy else [])
    res = pl.pallas_call(
        body, name=name, out_shape=tuple(out_shapes + _rs_out(rs_sends)), grid=grid,
        in_specs=[pl.BlockSpec(a_blk, a_idx), pl.BlockSpec(b_blk, b_idx)] + [any_spec] * n_rs,
        out_specs=tuple([pl.BlockSpec(o_blk, o_idx)] * n_out + [any_spec] * n_rs),
        scratch_shapes=scratch + (_rs_scratch(rs_sends) if n_rs else []), compiler_params=_params(),
    )(a, b, *rs_sends)
    return res if len(res) > 1 else res[0]


EW_TILE = 256


def _rms(v):
    return lax.rsqrt(jnp.mean(v * v, axis=-1, keepdims=True) + EPS)


def _rms_bwd(dhat, vh, r):
    return r * (dhat - vh * jnp.mean(dhat * vh, axis=-1, keepdims=True))


def _tok_spec(tm, d):
    return pl.BlockSpec((tm, d), lambda i: (i, 0))


def _vec_spec(d):
    return pl.BlockSpec((1, d), lambda i: (0, 0))


def _mod_spec(tiles_per_seq, d):
    return pl.BlockSpec((None, N_MOD, d), lambda i: (i // tiles_per_seq, 0, 0))


def _seq_acc_spec(tiles_per_seq, d):
    return pl.BlockSpec((None, 1, d), lambda i: (i // tiles_per_seq, 0, 0))


def _acc(ref, val, first):
    @pl.when(first)
    def _():
        ref[...] = val

    @pl.when(jnp.logical_not(first))
    def _():
        ref[...] += val


def _colsum(v):
    return jnp.sum(v, axis=0, keepdims=True)


def _pre_mix(x2, g_pre, mod, seq):
    t, d = x2.shape
    tm = EW_TILE

    def body(x_ref, g_ref, mod_ref, h_ref):
        xv = x_ref[...]
        n = xv * _rms(xv) * g_ref[...]
        h_ref[...] = (n * (1.0 + mod_ref[1:2, :]) + mod_ref[0:1, :]).astype(BF16)

    return pl.pallas_call(
        body, name="pre_mix", out_shape=jax.ShapeDtypeStruct((t, d), BF16), grid=(t // tm,),
        in_specs=[_tok_spec(tm, d), _vec_spec(d), _mod_spec(seq // tm, d)],
        out_specs=_tok_spec(tm, d), compiler_params=_params(),
    )(x2, g_pre, mod)


def _mid(mix, x2, g_post, g_pre, mod, seq):
    t, d = x2.shape
    tm = EW_TILE

    def body(mix_ref, x_ref, gpost_ref, gpre_ref, mod_ref, x1_ref, h2_ref):
        mv = mix_ref[...]
        x1 = x_ref[...] + mod_ref[2:3, :] * (mv * _rms(mv) * gpost_ref[...])
        x1_ref[...] = x1
        n = x1 * _rms(x1) * gpre_ref[...]
        h2_ref[...] = (n * (1.0 + mod_ref[4:5, :]) + mod_ref[3:4, :]).astype(BF16)

    return pl.pallas_call(
        body, name="mid", grid=(t // tm,),
        out_shape=(jax.ShapeDtypeStruct((t, d), F32), jax.ShapeDtypeStruct((t, d), BF16)),
        in_specs=[_tok_spec(tm, d), _tok_spec(tm, d), _vec_spec(d), _vec_spec(d), _mod_spec(seq // tm, d)],
        out_specs=(_tok_spec(tm, d), _tok_spec(tm, d)), compiler_params=_params(),
    )(mix, x2, g_post, g_pre, mod)


def _post(f, x1, target, g_post, mod, seq):
    t, d = x1.shape
    tm = EW_TILE
    tps = seq // tm

    def body(f_ref, x1_ref, tgt_ref, g_ref, mod_ref, loss_ref, dy_ref, df_ref, dgate_ref, gg_ref):
        i = pl.program_id(0)
        fv = f_ref[...]
        r = _rms(fv)
        fh = fv * r
        nf = fh * g_ref[...]
        gate = mod_ref[5:6, :]
        err = x1_ref[...] + gate * nf - tgt_ref[...]
        _acc(loss_ref, jnp.sum(_colsum(err * err), axis=1, keepdims=True) * jnp.ones((1, LANES), F32), i == 0)
        dy = err * (1.0 / d)
        dy_ref[...] = dy
        _acc(dgate_ref, _colsum(dy * nf), i % tps == 0)
        dn = dy * gate
        _acc(gg_ref, _colsum(dn * fh), i == 0)
        df_ref[...] = _rms_bwd(dn * g_ref[...], fh, r).astype(BF16)

    n_seq = t // seq
    return pl.pallas_call(
        body, name="post", grid=(t // tm,),
        out_shape=(jax.ShapeDtypeStruct((1, LANES), F32), jax.ShapeDtypeStruct((t, d), F32),
                   jax.ShapeDtypeStruct((t, d), BF16), jax.ShapeDtypeStruct((n_seq, 1, d), F32),
                   jax.ShapeDtypeStruct((1, d), F32)),
        in_specs=[_tok_spec(tm, d), _tok_spec(tm, d), _tok_spec(tm, d), _vec_spec(d), _mod_spec(tps, d)],
        out_specs=(pl.BlockSpec((1, LANES), lambda i: (0, 0)), _tok_spec(tm, d), _tok_spec(tm, d),
                   _seq_acc_spec(tps, d), _vec_spec(d)),
        compiler_params=_params(),
    )(f, x1, target, g_post, mod)


def _bwd_mid(dh2, dy, x1, mix, g_pre, g_post, mod, seq):
    t, d = x1.shape
    tm = EW_TILE
    tps = seq // tm

    def body(dh2_ref, dy_ref, x1_ref, mix_ref, gpre_ref, gpost_ref, mod_ref,
             dx1_ref, dmix_ref, dshift_ref, dscale_ref, dgate_ref, ggpre_ref, ggpost_ref):
        i = pl.program_id(0)
        seq_first = i % tps == 0
        dh = dh2_ref[...]
        x1 = x1_ref[...]
        r = _rms(x1)
        xh = x1 * r
        gpre = gpre_ref[...]
        _acc(dshift_ref, _colsum(dh), seq_first)
        _acc(dscale_ref, _colsum(dh * xh * gpre), seq_first)
        dn = dh * (1.0 + mod_ref[4:5, :])
        _acc(ggpre_ref, _colsum(dn * xh), i == 0)
        dx1 = dy_ref[...] + _rms_bwd(dn * gpre, xh, r)
        dx1_ref[...] = dx1
        mv = mix_ref[...]
        rm = _rms(mv)
        mh = mv * rm
        gpost = gpost_ref[...]
        _acc(dgate_ref, _colsum(dx1 * mh * gpost), seq_first)
        dnm = dx1 * mod_ref[2:3, :]
        _acc(ggpost_ref, _colsum(dnm * mh), i == 0)
        dmix_ref[...] = _rms_bwd(dnm * gpost, mh, rm).astype(BF16)

    n_seq = t // seq
    seq_sds = jax.ShapeDtypeStruct((n_seq, 1, d), F32)
    vec_sds = jax.ShapeDtypeStruct((1, d), F32)
    return pl.pallas_call(
        body, name="bwd_mid", grid=(t // tm,),
        out_shape=(jax.ShapeDtypeStruct((t, d), F32), jax.ShapeDtypeStruct((t, d), BF16),
                   seq_sds, seq_sds, seq_sds, vec_sds, vec_sds),
        in_specs=[_tok_spec(tm, d)] * 4 + [_vec_spec(d), _vec_spec(d), _mod_spec(tps, d)],
        out_specs=(_tok_spec(tm, d), _tok_spec(tm, d), _seq_acc_spec(tps, d), _seq_acc_spec(tps, d),
                   _seq_acc_spec(tps, d), _vec_spec(d), _vec_spec(d)),
        compiler_params=_params(),
    )(dh2, dy, x1, mix, g_pre, g_post, mod)


def _bwd_pre(dh1, dx1, x2, g_pre, mod, seq):
    t, d = x2.shape
    tm = EW_TILE
    tps = seq // tm

    def body(dh_ref, dx1_ref, x_ref, g_ref, mod_ref, gx_ref, dshift_ref, dscale_ref, gg_ref):
        i = pl.program_id(0)
        seq_first = i % tps == 0
        dh = dh_ref[...]
        xv = x_ref[...]
        r = _rms(xv)
        xh = xv * r
        g = g_ref[...]
        _acc(dshift_ref, _colsum(dh), seq_first)
        _acc(dscale_ref, _colsum(dh * xh * g), seq_first)
        dn = dh * (1.0 + mod_ref[1:2, :])
        _acc(gg_ref, _colsum(dn * xh), i == 0)
        gx_ref[...] = dx1_ref[...] + _rms_bwd(dn * g, xh, r)

    n_seq = t // seq
    seq_sds = jax.ShapeDtypeStruct((n_seq, 1, d), F32)
    return pl.pallas_call(
        body, name="bwd_pre", grid=(t // tm,),
        out_shape=(jax.ShapeDtypeStruct((t, d), F32), seq_sds, seq_sds, jax.ShapeDtypeStruct((1, d), F32)),
        in_specs=[_tok_spec(tm, d)] * 3 + [_vec_spec(d), _mod_spec(tps, d)],
        out_specs=(_tok_spec(tm, d), _seq_acc_spec(tps, d), _seq_acc_spec(tps, d), _vec_spec(d)),
        compiler_params=_params(),
    )(dh1, dx1, x2, g_pre, mod)


def _ffn_up(h2, wgu, tm, tn):
    t, d = h2.shape
    f = wgu.shape[1]

    def body(h_ref, w_ref, gu_ref, act_ref):
        h = h_ref[...]
        g = _dot_nt(h, w_ref[0])
        u = _dot_nt(h, w_ref[1])
        gu_ref[0] = g.astype(BF16)
        gu_ref[1] = u.astype(BF16)
        act_ref[...] = (g * jax.nn.sigmoid(g) * u).astype(BF16)

    return pl.pallas_call(
        body, name="ffn_up", grid=(t // tm, f // tn),
        out_shape=(jax.ShapeDtypeStruct((2, t, f), BF16), jax.ShapeDtypeStruct((t, f), BF16)),
        in_specs=[pl.BlockSpec((tm, d), lambda i, j: (i, 0)), pl.BlockSpec((2, tn, d), lambda i, j: (0, j, 0))],
        out_specs=(pl.BlockSpec((2, tm, tn), lambda i, j: (0, i, j)), pl.BlockSpec((tm, tn), lambda i, j: (i, j))),
        compiler_params=_params(),
    )(h2, wgu)


def _ffn_act_bwd(df, wd, gu, tm, tn):
    t, d = df.shape
    f = wd.shape[0]

    def body(df_ref, w_ref, gu_ref, dgu_ref):
        da = _dot_nt(df_ref[...], w_ref[...])
        g = gu_ref[0].astype(F32)
        u = gu_ref[1].astype(F32)
        s = jax.nn.sigmoid(g)
        silu = g * s
        dgu_ref[0] = (da * u * (s + silu * (1.0 - s))).astype(BF16)
        dgu_ref[1] = (da * silu).astype(BF16)

    return pl.pallas_call(
        body, name="ffn_act_bwd", grid=(t // tm, f // tn),
        out_shape=jax.ShapeDtypeStruct((2, t, f), BF16),
        in_specs=[pl.BlockSpec((tm, d), lambda i, j: (i, 0)), pl.BlockSpec((tn, d), lambda i, j: (j, 0)),
                  pl.BlockSpec((2, tm, tn), lambda i, j: (0, i, j))],
        out_specs=pl.BlockSpec((2, tm, tn), lambda i, j: (0, i, j)),
        compiler_params=_params(),
    )(df, wd, gu)


SIGN_BIT = 0x80000000
Q_SCALE = 1.0 / math.sqrt(HEAD_DIM)


def _split_dot(v, tri2):
    hi = v.astype(BF16)
    lo = (v - hi.astype(F32)).astype(BF16)
    return _dot_nn(jnp.concatenate([hi, lo], axis=1), tri2)


def _sb_tile(qs, k2, mask, ntri2, cur):
    z = _dot_nt(qs, k2)
    neg_abs = lax.bitcast_convert_type(lax.bitcast_convert_type(z, jnp.uint32) | jnp.uint32(SIGN_BIT), F32)
    sp = jnp.maximum(z, 0.0) + jnp.log(1.0 + jnp.exp(neg_abs))
    if mask is not None:
        sp = jnp.where(mask, sp, 0.0)
    w = jnp.exp(z + _split_dot(sp, ntri2) + cur)
    if mask is not None:
        w = jnp.where(mask, w, 0.0)
    return z, sp, w


def _stack_heads(v, lane, scale=None):
    if scale is not None:
        v = v * jnp.asarray(scale, v.dtype)
    zero = jnp.zeros_like(v)
    return jnp.concatenate([jnp.where(lane < HEAD_DIM, v, zero), jnp.where(lane >= HEAD_DIM, v, zero)], axis=0)


def _diag_mask(tq):
    row = lax.broadcasted_iota(jnp.int32, (2 * tq, tq), 0)
    col = lax.broadcasted_iota(jnp.int32, (2 * tq, tq), 1)
    return col < jnp.where(row >= tq, row - tq, row)


def _attn_fwd(proj, tri_after, n_seq, seq, ag_srcs, ag_out_shapes, ag_dests):
    t = proj.shape[0]
    tq = ATT_TILE
    n_pair = (proj.shape[1] // 4) // LANES
    n_ag, n_ag_out = len(ag_srcs), len(ag_out_shapes)
    n_steps = n_seq * n_pair

    def body(q_ref, k_ref, v_ref, tri_ref, *rest):
        ag_src, rest = rest[:n_ag], rest[n_ag:]
        o_ref, cs_ref = rest[:2]
        ag_out, rest = rest[2:2 + n_ag_out], rest[2 + n_ag_out:]
        oacc, cmat, carry = rest[:3]
        ag_start, ag_forward, ag_finish = _ag_phases(ag_dests, ag_src, ag_out, *rest[3:])
        step = pl.program_id(0) * n_pair + pl.program_id(1)
        pl.when(step == 0)(ag_start)
        pl.when(step == (5 * n_steps) // 8)(ag_forward)
        lane = lax.broadcasted_iota(jnp.int32, (1, LANES), 1)
        ntri2 = tri_ref[...]
        diag = _diag_mask(tq)

        def q_tile(qi, _):
            r0 = pl.multiple_of(qi * tq, tq)
            qs = _stack_heads(q_ref[pl.ds(r0, tq), :], lane, Q_SCALE)
            carry[...] = jnp.zeros_like(carry)
            cmat[...] = jnp.zeros_like(cmat)
            oacc[...] = jnp.zeros_like(oacc)

            def run_tiles(tiles):
                cur = carry[...]
                cm = cmat[...]
                pv = None
                for kb, mask in tiles:
                    c0 = pl.multiple_of(kb * tq, tq)
                    _, sp, w = _sb_tile(qs, k_ref[pl.ds(c0, tq), :], mask, ntri2, cur)
                    p = _dot_nn(w.astype(BF16), v_ref[pl.ds(c0, tq), :])
                    pv = p if pv is None else pv + p
                    cm = jnp.where(lane == kb, cur, cm)
                    cur = cur - jnp.sum(sp, axis=1, keepdims=True)
                oacc[...] += pv
                cmat[...] = cm
                carry[...] = cur

            odd = qi % 2

            @pl.when(odd == 0)
            def _():
                run_tiles([(qi, diag)])

            @pl.when(odd == 1)
            def _():
                run_tiles([(qi, diag), (qi - 1, None)])

            def pair(j, _):
                kb = qi - 1 - odd - 2 * j
                run_tiles([(kb, None), (kb - 1, None)])
                return 0

            lax.fori_loop(0, qi // 2, pair, 0)
            cs_ref[pl.ds(r0, tq), 0:LANES] = cmat[0:tq, :]
            cs_ref[pl.ds(r0, tq), LANES:2 * LANES] = cmat[tq:2 * tq, :]
            o_ref[pl.ds(r0, tq), :] = jnp.where(lane < HEAD_DIM, oacc[0:tq, :], oacc[tq:2 * tq, :]).astype(BF16)
            return 0

        lax.fori_loop(0, seq // tq, q_tile, 0)
        pl.when(step == n_steps - 1)(ag_finish)

    blk = lambda off: pl.BlockSpec((seq, LANES), lambda b, p: (b, off + p))
    any_spec = pl.BlockSpec(memory_space=pl.ANY)
    return pl.pallas_call(
        body, name="attn_fwd", grid=(n_seq, n_pair),
        out_shape=(jax.ShapeDtypeStruct((t, n_pair * LANES), BF16),
                   jax.ShapeDtypeStruct((t, n_pair * 2 * LANES), F32), *ag_out_shapes),
        in_specs=[blk(0), blk(n_pair), blk(2 * n_pair), pl.BlockSpec((2 * tq, tq), lambda b, p: (0, 0))]
        + [any_spec] * n_ag,
        out_specs=(pl.BlockSpec((seq, LANES), lambda b, p: (b, p)),
                   pl.BlockSpec((seq, 2 * LANES), lambda b, p: (b, p)), *([any_spec] * n_ag_out)),
        scratch_shapes=[pltpu.VMEM((2 * tq, LANES), F32), pltpu.VMEM((2 * tq, LANES), F32),
                        pltpu.VMEM((2 * tq, 1), F32)] + _ag_scratch(n_ag),
        compiler_params=_params(),
    )(proj, proj, proj, tri_after, *ag_srcs)


def _attn_bwd(proj, dcat, cstats, tri_after, tri_incl, n_seq, seq, rs_sends):
    t = proj.shape[0]
    tq = ATT_TILE
    width = proj.shape[1] // 4
    n_pair = width // LANES
    n_rs = len(rs_sends)
    rs_shapes = [r.shape for r in rs_sends]
    n_steps = n_seq * n_pair

    def body(q_ref, k_ref, v_ref, do_ref, cs_ref, tria_ref, trii_ref, *rest):
        rs_src, rest = rest[:n_rs], rest[n_rs:]
        out_ref = rest[0]
        rs_dst, rest = rest[1:1 + n_rs], rest[1 + n_rs:]
        dq_acc, dk_acc, dv_acc, ecarry = rest[:4]
        rs_start, rs_finish = _rs_phases(rs_shapes, rs_src, rs_dst, *rest[4:])
        step = pl.program_id(0) * n_pair + pl.program_id(1)
        pl.when(step == 0)(rs_start)
        lane = lax.broadcasted_iota(jnp.int32, (1, LANES), 1)
        ntri2 = tria_ref[...]
        tri_i2 = trii_ref[...]
        diag = _diag_mask(tq)
        dk_acc[...] = jnp.zeros_like(dk_acc)
        dv_acc[...] = jnp.zeros_like(dv_acc)

        def q_tile(qi, _):
            r0 = pl.multiple_of(qi * tq, tq)
            qs = _stack_heads(q_ref[pl.ds(r0, tq), :], lane, Q_SCALE)
            dos = _stack_heads(do_ref[pl.ds(r0, tq), :], lane)
            cs = jnp.concatenate([cs_ref[pl.ds(r0, tq), 0:LANES], cs_ref[pl.ds(r0, tq), LANES:2 * LANES]], axis=0)
            ecarry[...] = jnp.zeros_like(ecarry)
            dq_acc[...] = jnp.zeros_like(dq_acc)

            def run_tiles(tiles):
                ec = ecarry[...]
                dq = None
                for kb, mask in tiles:
                    c0 = pl.multiple_of(kb * tq, tq)
                    k2 = k_ref[pl.ds(c0, tq), :]
                    v2 = v_ref[pl.ds(c0, tq), :]
                    cur = jnp.sum(jnp.where(lane == kb, cs, 0.0), axis=1, keepdims=True)
                    z, sp, w = _sb_tile(qs, k2, mask, ntri2, cur)
                    ee = w * _dot_nt(dos, v2)
                    einc = _split_dot(ee, tri_i2) + ec
                    dz = ee - jnp.exp(z - sp) * einc
                    if mask is not None:
                        dz = jnp.where(mask, dz, 0.0)
                    dzb = dz.astype(BF16)
                    p = _dot_nn(dzb, k2)
                    dq = p if dq is None else dq + p
                    dk_acc[pl.ds(c0, tq), :] += _dot_tn(dzb, qs)
                    dv_acc[pl.ds(c0, tq), :] += _dot_tn(w.astype(BF16), dos)
                    ec = ec + jnp.sum(ee, axis=1, keepdims=True)
                dq_acc[...] += dq
                ecarry[...] = ec

            def pair(j, _):
                run_tiles([(2 * j, None), (2 * j + 1, None)])
                return 0

            lax.fori_loop(0, qi // 2, pair, 0)
            odd = qi % 2

            @pl.when(odd == 0)
            def _():
                run_tiles([(qi, diag)])

            @pl.when(odd == 1)
            def _():
                run_tiles([(qi - 1, None), (qi, diag)])

            dq = jnp.where(lane < HEAD_DIM, dq_acc[0:tq, :], dq_acc[tq:2 * tq, :])
            out_ref[0, pl.ds(r0, tq), :] = (dq * Q_SCALE).astype(BF16)
            return 0

        lax.fori_loop(0, seq // tq, q_tile, 0)
        out_ref[1] = dk_acc[...].astype(BF16)
        out_ref[2] = dv_acc[...].astype(BF16)
        pl.when(step == n_steps - 1)(rs_finish)

    blk = lambda off: pl.BlockSpec((seq, LANES), lambda b, p: (b, off + p))
    tri_spec = pl.BlockSpec((2 * tq, tq), lambda b, p: (0, 0))
    any_spec = pl.BlockSpec(memory_space=pl.ANY)
    return pl.pallas_call(
        body, name="attn_bwd", grid=(n_seq, n_pair),
        out_shape=(jax.ShapeDtypeStruct((4, t, width), BF16), *_rs_out(rs_sends)),
        in_specs=[blk(0), blk(n_pair), blk(2 * n_pair), pl.BlockSpec((seq, LANES), lambda b, p: (b, p)),
                  pl.BlockSpec((seq, 2 * LANES), lambda b, p: (b, p)), tri_spec, tri_spec] + [any_spec] * n_rs,
        out_specs=(pl.BlockSpec((3, seq, LANES), lambda b, p: (0, b, p)), *([any_spec] * n_rs)),
        scratch_shapes=[pltpu.VMEM((2 * tq, LANES), F32), pltpu.VMEM((seq, LANES), F32),
                        pltpu.VMEM((seq, LANES), F32), pltpu.VMEM((2 * tq, 1), F32)] + _rs_scratch(rs_sends),
        compiler_params=_params(),
    )(proj, proj, proj, dcat, cstats, tri_after, tri_incl, *rs_sends)


def _window_terms(g, rows):
    win = jnp.where(g == 0, POOL_WINDOWS[0], jnp.where(g == 1, POOL_WINDOWS[1],
                    jnp.where(g == 2, POOL_WINDOWS[2], POOL_WINDOWS[3])))
    cnt = jnp.minimum(rows + 1, win).astype(F32)
    return win, cnt


def _window_sum(v, g, rows, forward):
    s_len = v.shape[0]
    sums = []
    s = v
    for step in range(len(POOL_WINDOWS)):
        sh = 1 << step
        if forward:
            shifted = jnp.where(rows < s_len - sh, pltpu.roll(s, s_len - sh, axis=0), 0.0)
        else:
            shifted = jnp.where(rows >= sh, pltpu.roll(s, sh, axis=0), 0.0)
        s = s + shifted
        sums.append(s)
    return jnp.where(g == 0, sums[0], jnp.where(g == 1, sums[1], jnp.where(g == 2, sums[2], sums[3])))


def _pooled(u, g, rows):
    _, cnt = _window_terms(g, rows)
    return _window_sum(u, g, rows, forward=False) / cnt - u


def _pool_fwd(proj, w_pool, pool_scale, n_seq, seq):
    t = proj.shape[0]
    n_grp = len(POOL_WINDOWS)
    u_off = 3 * (proj.shape[1] // 4) // LANES

    def body(u_ref, w_ref, s_ref, o_ref):
        g = pl.program_id(1)
        rows = lax.broadcasted_iota(jnp.int32, (seq, 1), 0)
        pooled = _pooled(u_ref[...].astype(F32), g, rows)
        y = _dot_nn(pooled.astype(BF16), w_ref[...].astype(BF16))
        o_ref[...] = (y * s_ref[...]).astype(BF16)

    return pl.pallas_call(
        body, name="pool_fwd", grid=(n_seq, n_grp),
        out_shape=jax.ShapeDtypeStruct((t, n_grp * POOL_GROUP_DIM), BF16),
        in_specs=[pl.BlockSpec((seq, LANES), lambda b, g: (b, u_off + g)),
                  pl.BlockSpec((None, POOL_GROUP_DIM, POOL_GROUP_DIM), lambda b, g: (g, 0, 0)),
                  pl.BlockSpec((1, POOL_GROUP_DIM), lambda b, g: (0, g))],
        out_specs=pl.BlockSpec((seq, LANES), lambda b, g: (b, g)),
        compiler_params=_params(),
    )(proj, w_pool, pool_scale)


def _pool_bwd(proj, dcat, w_pool, pool_scale, dqkv, n_seq, seq):
    n_grp = len(POOL_WINDOWS)
    width = proj.shape[1] // 4
    u_off = 3 * width // LANES
    dp_off = width // LANES

    def body(u_ref, dp_ref, w_ref, s_ref, alias_ref, du_ref, gw_ref, gs_ref):
        del alias_ref
        g = pl.program_id(0)
        b = pl.program_id(1)
        rows = lax.broadcasted_iota(jnp.int32, (seq, 1), 0)
        pooled = _pooled(u_ref[...].astype(F32), g, rows)
        pb = pooled.astype(BF16)
        wb = w_ref[...].astype(BF16)
        z = _dot_nn(pb, wb)
        dp = dp_ref[...].astype(F32)
        _acc(gs_ref, _colsum(dp * z), b == 0)
        dys = (dp * s_ref[...]).astype(BF16)
        _acc(gw_ref, _dot_tn(pb, dys), b == 0)
        dpooled = _dot_nt(dys, wb)
        _, cnt = _window_terms(g, rows)
        du = _window_sum(dpooled / cnt, g, rows, forward=True) - dpooled
        du_ref[...] = du.astype(BF16)

    t = proj.shape[0]
    return pl.pallas_call(
        body, name="pool_bwd", grid=(n_grp, n_seq),
        out_shape=(jax.ShapeDtypeStruct(dqkv.shape, BF16),
                   jax.ShapeDtypeStruct((n_grp, POOL_GROUP_DIM, POOL_GROUP_DIM), F32),
                   jax.ShapeDtypeStruct((1, n_grp * POOL_GROUP_DIM), F32)),
        in_specs=[pl.BlockSpec((seq, LANES), lambda g, b: (b, u_off + g)),
                  pl.BlockSpec((seq, LANES), lambda g, b: (b, dp_off + g)),
                  pl.BlockSpec((None, POOL_GROUP_DIM, POOL_GROUP_DIM), lambda g, b: (g, 0, 0)),
                  pl.BlockSpec((1, POOL_GROUP_DIM), lambda g, b: (0, g)),
                  pl.BlockSpec(memory_space=pl.ANY)],
        out_specs=(pl.BlockSpec((None, seq, LANES), lambda g, b: (3, b, g)),
                   pl.BlockSpec((None, POOL_GROUP_DIM, POOL_GROUP_DIM), lambda g, b: (g, 0, 0)),
                   pl.BlockSpec((1, POOL_GROUP_DIM), lambda g, b: (0, g))),
        input_output_aliases={4: 0},
        compiler_params=_params(),
    )(proj, dcat, w_pool, pool_scale, dqkv)


def _cond_fwd(c_all, w_cond, b_cols):
    n, _ = c_all.shape
    cols = w_cond.shape[1]

    def body(c_ref, w_ref, b_ref, o_ref):
        cv = c_ref[...]
        a = cv * jax.nn.sigmoid(cv)
        o_ref[...] = jnp.dot(a, w_ref[...], preferred_element_type=F32,
                             precision=lax.Precision.HIGHEST) + b_ref[...]

    return pl.pallas_call(
        body, name="cond_fwd", out_shape=jax.ShapeDtypeStruct((n, cols), F32),
        compiler_params=_params(),
    )(c_all, w_cond, b_cols)


def _cond_bwd(c_all, dmod_all, dmod_cols):
    n, d = c_all.shape
    cols = dmod_cols.shape[1]

    def body(c_ref, dm_ref, dmc_ref, gw_ref, gb_ref):
        cv = c_ref[...]
        a = cv * jax.nn.sigmoid(cv)
        gw_ref[...] = lax.dot_general(a, dmc_ref[...], (((0,), (0,)), ((), ())),
                                      preferred_element_type=F32, precision=lax.Precision.HIGHEST)
        gb_ref[...] = _colsum(dm_ref[...])

    return pl.pallas_call(
        body, name="cond_bwd",
        out_shape=(jax.ShapeDtypeStruct((d, cols), F32), jax.ShapeDtypeStruct((1, dmod_all.shape[1]), F32)),
        compiler_params=_params(),
    )(c_all, dmod_all, dmod_cols)


def _adamw_math(w, g, m, v):
    m = ADAM_B1 * m + (1.0 - ADAM_B1) * g
    v = ADAM_B2 * v + (1.0 - ADAM_B2) * (g * g)
    m_hat = m / (1.0 - ADAM_B1 ** ADAM_STEP)
    v_hat = v / (1.0 - ADAM_B2 ** ADAM_STEP)
    delta = -ADAM_LR * (m_hat / (jnp.sqrt(v_hat) + ADAM_EPS) + ADAM_WD * w)
    return delta, m, v


def _adamw(w, g, m, v, rows, name):
    r, cdim = w.shape

    def body(w_ref, g_ref, m_ref, v_ref, d_ref, nm_ref, nv_ref):
        d_ref[...], nm_ref[...], nv_ref[...] = _adamw_math(w_ref[...], g_ref[...], m_ref[...], v_ref[...])

    spec = pl.BlockSpec((rows, cdim), lambda i: (i, 0))
    sds = jax.ShapeDtypeStruct((r, cdim), F32)
    return pl.pallas_call(
        body, name=name, grid=(r // rows,), out_shape=(sds, sds, sds),
        in_specs=[spec] * 4, out_specs=(spec, spec, spec), compiler_params=_params(),
    )(w, g, m, v)


def _adamw_small(ws, gparts, ms, vs, name):
    n = len(ws)

    def body(*refs):
        w_r, g_r, m_r, v_r = refs[:n], refs[n:2 * n], refs[2 * n:3 * n], refs[3 * n:4 * n]
        outs = refs[4 * n:]
        for i in range(n):
            g = g_r[i][0]
            for dev in range(1, g_r[i].shape[0]):
                g = g + g_r[i][dev]
            delta, m, v = _adamw_math(w_r[i][...], g, m_r[i][...], v_r[i][...])
            outs[i][...] = g
            outs[n + i][...] = delta
            outs[2 * n + i][...] = m
            outs[3 * n + i][...] = v

    sds = [jax.ShapeDtypeStruct(w.shape, F32) for w in ws]
    return pl.pallas_call(
        body, name=name, out_shape=tuple(sds * 4), compiler_params=_params(),
    )(*ws, *gparts, *ms, *vs)


def kernel(x, c, w_cond, b_cond, g_mix_pre, g_mix_post, w_in, w_pool, pool_scale, w_out, g_ffn_pre, g_ffn_post, w_gate, w_up, w_down, loss_target, m_w_cond, m_b_cond, m_g_mix_pre, m_g_mix_post, m_w_in, m_w_pool, m_pool_scale, m_w_out, m_g_ffn_pre, m_g_ffn_post, m_w_gate, m_w_up, m_w_down, v_w_cond, v_b_cond, v_g_mix_pre, v_g_mix_post, v_w_in, v_w_pool, v_pool_scale, v_w_out, v_g_ffn_pre, v_g_ffn_post, v_w_gate, v_w_up, v_w_down):
    n_seq, seq, d = x.shape
    t = n_seq * seq
    xi, yi, ci = _mesh_pos()
    me = 4 * xi + 2 * yi + ci
    x2 = x.reshape(t, d)
    tgt2 = loss_target.reshape(t, d)
    in_rows = w_in.shape[2]
    out_rows = w_out.shape[1]
    ff_rows = w_gate.shape[2]
    ff = N_DEV * ff_rows
    cond_cols = w_cond.shape[2]

    win_t = w_in[0].T.astype(BF16)
    wout_s = w_out[0].astype(BF16)
    wg_t = w_gate[0].T.astype(BF16)
    wu_t = w_up[0].T.astype(BF16)
    wd_s = w_down[0].astype(BF16)
    c_all, win_g = _all_gather(
        [c, win_t],
        [jax.ShapeDtypeStruct((N_DEV, n_seq, d), F32), jax.ShapeDtypeStruct((N_DEV, in_rows, d), BF16)],
        [(0, ()), (1, ())], "ag_c_win")
    c_all = c_all.reshape(N_DEV * n_seq, d)
    win_full = win_g.reshape(N_DEV * in_rows, d)

    b_cols = lax.dynamic_slice_in_dim(b_cond, me * cond_cols, cond_cols, axis=1)
    mod_cols = _cond_fwd(c_all, w_cond[0], b_cols)
    (mod_g,) = _all_gather([mod_cols], [jax.ShapeDtypeStruct((N_DEV,) + mod_cols.shape, F32)], [(0, ())], "ag_mod")
    mod_mine = lax.dynamic_slice_in_dim(mod_g, me * n_seq, n_seq, axis=1)
    mod = jnp.transpose(mod_mine, (1, 0, 2)).reshape(n_seq, N_MOD, d)

    h1 = _pre_mix(x2, g_mix_pre, mod, seq)
    proj = _matmul(h1, win_full, "nt", BF16, 1024, 512, d, "proj")
    tq = ATT_TILE
    ids = jnp.arange(tq)
    tri_after = jnp.tile(-(ids[:, None] >= ids[None, :]).astype(BF16), (2, 1))
    tri_incl = jnp.tile((ids[:, None] <= ids[None, :]).astype(BF16), (2, 1))
    attn, cstats, wout_g, wgu_g, wd_g = _attn_fwd(
        proj, tri_after, n_seq, seq, [wout_s, wg_t, wu_t, wd_s],
        [jax.ShapeDtypeStruct((N_DEV, out_rows, d), BF16), jax.ShapeDtypeStruct((2, N_DEV, ff_rows, d), BF16),
         jax.ShapeDtypeStruct((N_DEV, ff_rows, d), BF16)],
        [(0, ()), (1, (0,)), (1, (1,)), (2, ())])
    wout_full = wout_g.reshape(N_DEV * out_rows, d)
    wgu_full = wgu_g.reshape(2, ff, d)
    wd_full = wd_g.reshape(ff, d)
    pool = _pool_fwd(proj, w_pool[0], pool_scale, n_seq, seq)
    cat = jnp.stack([attn, pool])
    mix = _matmul(cat, wout_full.reshape(2, d // 2, d), "nn", F32, 512, d, d // 2, "mix")
    x1, h2 = _mid(mix, x2, g_mix_post, g_ffn_pre, mod, seq)
    gu, act = _ffn_up(h2, wgu_full, 512, ff // 2)
    f = _matmul(act, wd_full, "nn", F32, 512, d, ff, "ffn_down")
    loss_sum, dy, df, dgate_f, gg_ffn_post = _post(f, x1, tgt2, g_ffn_post, mod, seq)

    dgu = _ffn_act_bwd(df, wd_full, gu, 512, ff // 2)
    gwd, gwd_b = _matmul(act, df, "tn", F32, ff // 2, d, 1024, "grad_w_down", bf16_copy=True)
    gwgu, gwgu_b = _matmul(dgu, h2, "tn", F32, ff // 2, d, 1024, "grad_w_gate_up", bf16_copy=True)
    dh2 = _matmul(dgu, wgu_full, "nn", F32, 512, d, ff, "dh2")
    dx1, dmix, dshift_f, dscale_f, dgate_m, gg_ffn_pre, gg_mix_post = _bwd_mid(
        dh2, dy, x1, mix, g_ffn_pre, g_mix_post, mod, seq)
    dcat = _matmul(dmix, wout_full, "nt", BF16, 1024, 512, d, "dcat")
    gwout, gwout_b = _matmul(cat, dmix, "tn", F32, d // 2, d, 1024, "grad_w_out", bf16_copy=True)
    dqkv, rv_wgu, rv_wd, rv_wout = _attn_bwd(
        proj, dcat, cstats, tri_after, tri_incl, n_seq, seq,
        [gwgu_b.reshape(2, N_DEV, ff_rows, d), gwd_b.reshape(1, N_DEV, ff_rows, d),
         gwout_b.reshape(1, N_DEV, out_rows, d)])
    dproj, gw_pool, gs_pool = _pool_bwd(proj, dcat, w_pool[0], pool_scale, dqkv, n_seq, seq)
    gwin, gwin_b = _matmul(dproj, h1, "tn", F32, d // 2, d, 1024, "grad_w_in", bf16_copy=True)
    dh1, rv_win = _matmul(dproj, win_full.reshape(4, d // 2, d), "nn", F32, 512, d, d // 2, "dh1",
                          rs_sends=[gwin_b.reshape(1, N_DEV, in_rows, d)])
    grad_x, dshift_m, dscale_m, gg_mix_pre = _bwd_pre(dh1, dx1, x2, g_mix_pre, mod, seq)

    r_wgu = _rs_final(gwgu.reshape(2, N_DEV, ff_rows, d), rv_wgu, "rs_final_gate_up")
    r_wd = _rs_final(gwd.reshape(1, N_DEV, ff_rows, d), rv_wd, "rs_final_down")
    r_wout = _rs_final(gwout.reshape(1, N_DEV, out_rows, d), rv_wout, "rs_final_out")
    r_win = _rs_final(gwin.reshape(1, N_DEV, in_rows, d), rv_win, "rs_final_in")
    grad_w_in = r_win[0].T
    grad_w_out = r_wout[0]
    grad_w_gate = r_wgu[0].T
    grad_w_up = r_wgu[1].T
    grad_w_down = r_wd[0]

    dmod = jnp.concatenate([dshift_m, dscale_m, dgate_m, dshift_f, dscale_f, dgate_f], axis=1)
    small = jnp.concatenate([gg_mix_pre, gg_mix_post, gg_ffn_pre, gg_ffn_post,
                             jnp.pad(gs_pool, ((0, 0), (0, d - gs_pool.shape[1]))),
                             jnp.pad(loss_sum, ((0, 0), (0, d - loss_sum.shape[1]))), jnp.zeros((2, d), F32),
                             gw_pool.reshape(-1, d), dmod.reshape(n_seq * N_MOD, d)], axis=0)
    n_gw = gw_pool.size // d
    (small_g,) = _all_gather([small], [jax.ShapeDtypeStruct((N_DEV,) + small.shape, F32)], [(0, ())], "ag_small")
    loss = jnp.sum(small_g[:, 5, 0]) * (0.5 / d)
    dmod_all = small_g[:, 8 + n_gw:, :].reshape(N_DEV * n_seq, N_MOD * d)
    dmod_cols = lax.dynamic_slice_in_dim(dmod_all, me * cond_cols, cond_cols, axis=1)
    grad_w_cond, grad_b_cond = _cond_bwd(c_all, dmod_all, dmod_cols)

    small_ws = [g_mix_pre, g_mix_post, g_ffn_pre, g_ffn_post, pool_scale, w_pool.reshape(-1, POOL_GROUP_DIM)]
    small_ms = [m_g_mix_pre, m_g_mix_post, m_g_ffn_pre, m_g_ffn_post, m_pool_scale, m_w_pool.reshape(-1, POOL_GROUP_DIM)]
    small_vs = [v_g_mix_pre, v_g_mix_post, v_g_ffn_pre, v_g_ffn_post, v_pool_scale, v_w_pool.reshape(-1, POOL_GROUP_DIM)]
    small_gparts = [small_g[:, 0:1, :], small_g[:, 1:2, :], small_g[:, 2:3, :], small_g[:, 3:4, :],
                    small_g[:, 4:5, :pool_scale.shape[1]],
                    small_g[:, 8:8 + n_gw, :].reshape(N_DEV, -1, POOL_GROUP_DIM)]
    so = _adamw_small(small_ws, small_gparts, small_ms, small_vs, "adamw_small")
    ns = len(small_ws)
    sg, sdl, sm, sv = so[:ns], so[ns:2 * ns], so[2 * ns:3 * ns], so[3 * ns:]
    pool_shape = w_pool.shape
    fix = lambda lst: [lst[0], lst[1], lst[2], lst[3], lst[4], lst[5].reshape(pool_shape)]
    sg, sdl, sm, sv = fix(sg), fix(sdl), fix(sm), fix(sv)

    def big(w, g, m, v, rows, name):
        dl, nm, nv = _adamw(w[0], g, m[0], v[0], rows, name)
        return g[None], dl[None], nm[None], nv[None]

    o_cond = big(w_cond, grad_w_cond, m_w_cond, v_w_cond, 256, "adamw_w_cond")
    o_bcond = _adamw(b_cond, grad_b_cond, m_b_cond, v_b_cond, 1, "adamw_b_cond")
    o_bcond = (grad_b_cond,) + tuple(o_bcond)
    o_in = big(w_in, grad_w_in, m_w_in, v_w_in, 256, "adamw_w_in")
    o_out = big(w_out, grad_w_out, m_w_out, v_w_out, out_rows, "adamw_w_out")
    o_gate = big(w_gate, grad_w_gate, m_w_gate, v_w_gate, 256, "adamw_w_gate")
    o_up = big(w_up, grad_w_up, m_w_up, v_w_up, 256, "adamw_w_up")
    o_down = big(w_down, grad_w_down, m_w_down, v_w_down, ff_rows, "adamw_w_down")

    def pick(k):
        small_k = [sg, sdl, sm, sv][k]
        return [o_cond[k], o_bcond[k], small_k[0], small_k[1], o_in[k], small_k[5], small_k[4], o_out[k],
                small_k[2], small_k[3], o_gate[k], o_up[k], o_down[k]]

    return (loss, grad_x.reshape(n_seq, seq, d), *pick(0), *pick(1), *pick(2), *pick(3))
```

```python
import functools
import math

import jax
import jax.numpy as jnp
from jax import lax
from jax.experimental import pallas as pl
from jax.experimental.pallas import tpu as pltpu

F32 = jnp.float32
BF16 = jnp.bfloat16
MESH = pl.DeviceIdType.MESH

N_DEV = 8
HEAD_DIM = 64
LANES = 128
POOL_WINDOWS = (2, 4, 8, 16)
POOL_GROUP_DIM = 128
N_MOD = 6
EPS = 1e-6
ATT_TILE = 256
VMEM_LIMIT = 56 * 1024 * 1024

ADAM_LR = 0.001
ADAM_B1 = 0.9
ADAM_B2 = 0.999
ADAM_EPS = 1e-08
ADAM_WD = 0.01
ADAM_STEP = 10


def _params(**kw):
    return pltpu.CompilerParams(vmem_limit_bytes=VMEM_LIMIT, **kw)


def _dot_nn(a, b):
    return jnp.dot(a, b, preferred_element_type=F32)


def _dot_nt(a, b):
    return lax.dot_general(a, b, (((1,), (1,)), ((), ())), preferred_element_type=F32)


def _dot_tn(a, b):
    return lax.dot_general(a, b, (((0,), (0,)), ((), ())), preferred_element_type=F32)


def _mesh_pos():
    return lax.axis_index("x"), lax.axis_index("y"), lax.axis_index("c")


def _ag_phases(dests, src, outs, send_sems, recv_sems, local_sems):
    n = len(src)
    x, y, c = _mesh_pos()
    me, sibling = (x, y, c), (x, y, 1 - c)
    chips = [(1 - x, y), (x, 1 - y), (1 - x, 1 - y)]

    def slot(i, dev):
        oi, prefix = dests[i]
        px, py, pc = dev
        return outs[oi].at[prefix + (4 * px + 2 * py + pc,)]

    def copy(i, k, block, to, from_src=False):
        return pltpu.make_async_remote_copy(
            src_ref=src[i] if from_src else slot(i, block), dst_ref=slot(i, block),
            send_sem=send_sems.at[i, k], recv_sem=recv_sems.at[i, k],
            device_id=to, device_id_type=MESH)

    def mine(i):
        return pltpu.make_async_copy(src[i], slot(i, me), local_sems.at[i])

    def first(i):
        return [copy(i, 0, me, sibling, from_src=True)] + [
            copy(i, 1 + j, me, (*chip, c), from_src=True) for j, chip in enumerate(chips)]

    def passed(i, j):
        return copy(i, 4 + j, (*chips[j], c), sibling)

    def start():
        for i in range(n):
            mine(i).start()
        for i in range(n):
            for cp in first(i):
                cp.start()

    def forward():
        for j, chip in enumerate(chips):
            for i in range(n):
                copy(i, 1 + j, (*chip, c), me).wait_recv()
                passed(i, j).start()

    def finish():
        for i in range(n):
            copy(i, 0, sibling, me).wait_recv()
            for j, chip in enumerate(chips):
                copy(i, 4 + j, (*chip, 1 - c), me).wait_recv()
        for i in range(n):
            for cp in first(i) + [passed(i, j) for j in range(3)]:
                cp.wait_send()
            mine(i).wait()

    return start, forward, finish


def _ag_scratch(n):
    return [pltpu.SemaphoreType.DMA((n, 7)), pltpu.SemaphoreType.DMA((n, 7)), pltpu.SemaphoreType.DMA((n,))]


def _all_gather(srcs, out_shapes, dests, name):
    n = len(srcs)

    def body(*refs):
        src = refs[:n]
        outs = refs[n:n + len(out_shapes)]
        start, forward, finish = _ag_phases(dests, src, outs, *refs[n + len(out_shapes):])
        start()
        forward()
        finish()

    any_spec = pl.BlockSpec(memory_space=pl.ANY)
    return pl.pallas_call(
        body, name=name,
        out_shape=tuple(out_shapes),
        in_specs=[any_spec] * n,
        out_specs=tuple([any_spec] * len(out_shapes)),
        scratch_shapes=_ag_scratch(n),
    )(*srcs)


def _rs_phases(shapes, src, dst, send_sems, recv_sems):
    x, y, c = _mesh_pos()

    def copies():
        out = []
        n = 0
        for i, shp in enumerate(shapes):
            for m in range(shp[0]):
                for k in range(1, N_DEV):
                    px, py, pc = x ^ (k >> 2), y ^ ((k >> 1) & 1), c ^ (k & 1)
                    out.append(pltpu.make_async_remote_copy(
                        src_ref=src[i].at[m, 4 * px + 2 * py + pc], dst_ref=dst[i].at[m, k - 1],
                        send_sem=send_sems.at[n], recv_sem=recv_sems.at[n],
                        device_id=(px, py, pc), device_id_type=MESH))
                    n += 1
        return out

    def start():
        for cp in copies():
            cp.start()

    def finish():
        for cp in copies():
            cp.wait_send()
        for cp in copies():
            cp.wait_recv()

    return start, finish


def _rs_out(sends):
    return [jax.ShapeDtypeStruct((s.shape[0], N_DEV - 1) + s.shape[2:], s.dtype) for s in sends]


def _rs_scratch(sends):
    total = sum((N_DEV - 1) * s.shape[0] for s in sends)
    return [pltpu.SemaphoreType.DMA((total,)), pltpu.SemaphoreType.DMA((total,))]


def _rs_final(mine, recv, name):
    m_n, _, r, cdim = mine.shape
    x, y, c = _mesh_pos()
    me = jnp.reshape(4 * x + 2 * y + c, (1,)).astype(jnp.int32)

    def body(me_ref, p_ref, r_ref, o_ref):
        del me_ref
        s = p_ref[...]
        for k in range(N_DEV - 1):
            s = s + r_ref[k].astype(F32)
        o_ref[...] = s

    return pl.pallas_call(
        body, name=name, out_shape=jax.ShapeDtypeStruct((m_n, r, cdim), F32),
        grid_spec=pltpu.PrefetchScalarGridSpec(
            num_scalar_prefetch=1, grid=(m_n,),
            in_specs=[pl.BlockSpec((None, None, r, cdim), lambda m, s: (m, s[0], 0, 0)),
                      pl.BlockSpec((None, N_DEV - 1, r, cdim), lambda m, s: (m, 0, 0, 0))],
            out_specs=pl.BlockSpec((None, r, cdim), lambda m, s: (m, 0, 0))),
        compiler_params=_params(),
    )(me, mine, recv)


def _matmul(a, b, mode, out_dtype, tm, tn, tk, name, bf16_copy=False, rs_sends=()):
    ga = a.shape[0] if a.ndim == 3 else None
    gb = b.shape[0] if b.ndim == 3 else None
    a2, b2 = a.shape[-2:], b.shape[-2:]
    if mode == "nn":
        (m, k), n = a2, b2[1]
    elif mode == "nt":
        (m, k), n = a2, b2[0]
    else:
        (k, m), n = a2, b2[1]
    assert m % tm == 0 and n % tn == 0 and k % tk == 0, (name, m, n, k)
    nk = k // tk
    g_n = ga or 1
    batch_out = mode == "tn" and ga is not None
    n_red = nk if batch_out else nk * g_n
    dot = {"nn": _dot_nn, "nt": _dot_nt, "tn": _dot_tn}[mode]
    acc_in_out = out_dtype == F32

    n_rs = len(rs_sends)
    rs_shapes = [r.shape for r in rs_sends]
    n_out = 2 if bf16_copy else 1
    assert not bf16_copy or acc_in_out

    def body(a_ref, b_ref, *rest):
        rs_src, rest = rest[:n_rs], rest[n_rs:]
        o_ref = rest[0]
        copy_ref = rest[1] if bf16_copy else None
        rs_dst, scratch = rest[n_out:n_out + n_rs], rest[n_out + n_rs:]
        if n_rs:
            rs_start, rs_finish = _rs_phases(rs_shapes, rs_src, rs_dst, *scratch[-2:])
            first = functools.reduce(jnp.logical_and, [pl.program_id(ax) == 0 for ax in range(4)])
            last = functools.reduce(jnp.logical_and, [pl.program_id(ax) == grid[ax] - 1 for ax in range(4)])
            pl.when(first)(rs_start)
        p = dot(a_ref[...], b_ref[...])
        kk = pl.program_id(3) if batch_out else pl.program_id(2) * nk + pl.program_id(3)
        if n_red == 1:
            o_ref[...] = p.astype(out_dtype)
            if bf16_copy:
                copy_ref[...] = p.astype(BF16)
        else:
            acc = o_ref if acc_in_out else scratch[0]

            @pl.when(kk == 0)
            def _():
                acc[...] = p

            @pl.when(kk > 0)
            def _():
                acc[...] += p

            @pl.when(kk == n_red - 1)
            def _():
                if not acc_in_out:
                    o_ref[...] = acc[...].astype(out_dtype)
                if bf16_copy:
                    copy_ref[...] = acc[...].astype(BF16)

        if n_rs:
            pl.when(last)(rs_finish)

    def order(ids):
        return ids if batch_out else (ids[2], ids[0], ids[1], ids[3])

    def a_idx(*ids):
        g, i, j, kq = order(ids)
        blk = {"nn": (i, kq), "nt": (i, kq), "tn": (kq, i)}[mode]
        return (g,) + blk if ga is not None else blk

    def b_idx(*ids):
        g, i, j, kq = order(ids)
        blk = {"nn": (kq, j), "nt": (j, kq), "tn": (kq, j)}[mode]
        return (g,) + blk if gb is not None else blk

    def o_idx(*ids):
        g, i, j, kq = order(ids)
        return (g, i, j) if batch_out else (i, j)

    a_blk = {"nn": (tm, tk), "nt": (tm, tk), "tn": (tk, tm)}[mode]
    b_blk = {"nn": (tk, tn), "nt": (tn, tk), "tn": (tk, tn)}[mode]
    if ga is not None:
        a_blk = (None,) + a_blk
    if gb is not None:
        b_blk = (None,) + b_blk
    if batch_out:
        out_shape = jax.ShapeDtypeStruct((g_n, m, n), out_dtype)
        o_blk = (None, tm, tn)
        grid = (g_n, m // tm, n // tn, nk)
    else:
        out_shape = jax.ShapeDtypeStruct((m, n), out_dtype)
        o_blk = (tm, tn)
        grid = (m // tm, n // tn, g_n, nk)
    scratch = [] if (acc_in_out or n_red == 1) else [pltpu.VMEM((tm, tn), F32)]
    any_spec = pl.BlockSpec(memory_space=pl.ANY)
    out_shapes = [out_shape] + ([jax.ShapeDtypeStruct(out_shape.shape, BF16)] if bf16_copy else [])
    res = pl.pallas_call(
        body, name=name, out_shape=tuple(out_shapes + _rs_out(rs_sends)), grid=grid,
        in_specs=[pl.BlockSpec(a_blk, a_idx), pl.BlockSpec(b_blk, b_idx)] + [any_spec] * n_rs,
        out_specs=tuple([pl.BlockSpec(o_blk, o_idx)] * n_out + [any_spec] * n_rs),
        scratch_shapes=scratch + (_rs_scratch(rs_sends) if n_rs else []), compiler_params=_params(),
    )(a, b, *rs_sends)
    return res if len(res) > 1 else res[0]


EW_TILE = 256
ROW_TILE = 512


def _rms(v):
    return lax.rsqrt(jnp.mean(v * v, axis=-1, keepdims=True) + EPS)


def _rms_bwd(dhat, vh, r):
    return r * (dhat - vh * jnp.mean(dhat * vh, axis=-1, keepdims=True))


def _tok_spec(tm, d):
    return pl.BlockSpec((tm, d), lambda i: (i, 0))


def _vec_spec(d):
    return pl.BlockSpec((1, d), lambda i: (0, 0))


def _mod_spec(tiles_per_seq, d):
    return pl.BlockSpec((None, N_MOD, d), lambda i: (i // tiles_per_seq, 0, 0))


def _seq_acc_spec(tiles_per_seq, d):
    return pl.BlockSpec((None, 1, d), lambda i: (i // tiles_per_seq, 0, 0))


def _acc(ref, val, first):
    @pl.when(first)
    def _():
        ref[...] = val

    @pl.when(jnp.logical_not(first))
    def _():
        ref[...] += val


def _colsum(v):
    return jnp.sum(v, axis=0, keepdims=True)


def _pre_mix(x2, g_pre, mod, seq):
    t, d = x2.shape
    tm = EW_TILE

    def body(x_ref, g_ref, mod_ref, h_ref):
        xv = x_ref[...]
        n = xv * _rms(xv) * g_ref[...]
        h_ref[...] = (n * (1.0 + mod_ref[1:2, :]) + mod_ref[0:1, :]).astype(BF16)

    return pl.pallas_call(
        body, name="pre_mix", out_shape=jax.ShapeDtypeStruct((t, d), BF16), grid=(t // tm,),
        in_specs=[_tok_spec(tm, d), _vec_spec(d), _mod_spec(seq // tm, d)],
        out_specs=_tok_spec(tm, d), compiler_params=_params(),
    )(x2, g_pre, mod)


def _matmul_rows(a, b, tm, tk, seq, name, epilogue, ep_in, ep_in_kinds, ep_out, ep_out_kinds, rs_sends=()):
    ga = a.shape[0] if a.ndim == 3 else None
    (m, k), n = a.shape[-2:], b.shape[-1]
    g_n = ga or 1
    nk = k // tk
    n_red = g_n * nk
    tps = seq // tm
    grid = (m // tm, g_n, nk)
    n_rs = len(rs_sends)
    rs_shapes = [r.shape for r in rs_sends]
    n_in, n_out = len(ep_in), len(ep_out)

    def spec(kind):
        return {"tok": pl.BlockSpec((tm, n), lambda i, g, kq: (i, 0)),
                "vec": pl.BlockSpec((1, n), lambda i, g, kq: (0, 0)),
                "mod": pl.BlockSpec((None, N_MOD, n), lambda i, g, kq: (i // tps, 0, 0)),
                "seq": pl.BlockSpec((None, 1, n), lambda i, g, kq: (i // tps, 0, 0)),
                "loss": pl.BlockSpec((1, LANES), lambda i, g, kq: (0, 0))}[kind]

    def body(a_ref, b_ref, *rest):
        in_refs, rest = rest[:n_in], rest[n_in:]
        rs_src, rest = rest[:n_rs], rest[n_rs:]
        out_refs, rest = rest[:n_out], rest[n_out:]
        rs_dst, rest = rest[:n_rs], rest[n_rs:]
        acc = rest[0]
        i, kk = pl.program_id(0), pl.program_id(1) * nk + pl.program_id(2)
        if n_rs:
            rs_start, rs_finish = _rs_phases(rs_shapes, rs_src, rs_dst, *rest[1:])
            pl.when(jnp.logical_and(i == 0, kk == 0))(rs_start)
        p = _dot_nn(a_ref[...], b_ref[...])
        if n_red == 1:
            epilogue(p, i, tps, in_refs, out_refs)
        else:
            @pl.when(kk == 0)
            def _():
                acc[...] = p

            @pl.when(jnp.logical_and(kk > 0, kk < n_red - 1))
            def _():
                acc[...] += p

            @pl.when(kk == n_red - 1)
            def _():
                epilogue(acc[...] + p, i, tps, in_refs, out_refs)

        if n_rs:
            pl.when(jnp.logical_and(i == grid[0] - 1, kk == n_red - 1))(rs_finish)

    a_blk = (tm, tk) if ga is None else (None, tm, tk)
    b_blk = (tk, n) if ga is None else (None, tk, n)
    a_idx = (lambda i, g, kq: (i, kq)) if ga is None else (lambda i, g, kq: (g, i, kq))
    b_idx = (lambda i, g, kq: (kq, 0)) if ga is None else (lambda i, g, kq: (g, kq, 0))
    any_spec = pl.BlockSpec(memory_space=pl.ANY)
    res = pl.pallas_call(
        body, name=name, grid=grid, out_shape=tuple(list(ep_out) + _rs_out(rs_sends)),
        in_specs=[pl.BlockSpec(a_blk, a_idx), pl.BlockSpec(b_blk, b_idx)] + [spec(kd) for kd in ep_in_kinds]
        + [any_spec] * n_rs,
        out_specs=tuple([spec(kd) for kd in ep_out_kinds] + [any_spec] * n_rs),
        scratch_shapes=[pltpu.VMEM((tm, n), F32)] + (_rs_scratch(rs_sends) if n_rs else []),
        compiler_params=_params(),
    )(a, b, *ep_in, *rs_sends)
    return res


def _mid_epilogue(mv, i, tps, in_refs, out_refs):
    x_ref, gpost_ref, gpre_ref, mod_ref = in_refs
    mix_ref, x1_ref, h2_ref = out_refs
    mix_ref[...] = mv
    x1 = x_ref[...] + mod_ref[2:3, :] * (mv * _rms(mv) * gpost_ref[...])
    x1_ref[...] = x1
    n = x1 * _rms(x1) * gpre_ref[...]
    h2_ref[...] = (n * (1.0 + mod_ref[4:5, :]) + mod_ref[3:4, :]).astype(BF16)


def _post_epilogue(fv, i, tps, in_refs, out_refs):
    x1_ref, tgt_ref, g_ref, mod_ref = in_refs
    loss_ref, dy_ref, df_ref, dgate_ref, gg_ref = out_refs
    d = fv.shape[1]
    r = _rms(fv)
    fh = fv * r
    nf = fh * g_ref[...]
    gate = mod_ref[5:6, :]
    err = x1_ref[...] + gate * nf - tgt_ref[...]
    _acc(loss_ref, jnp.sum(_colsum(err * err), axis=1, keepdims=True) * jnp.ones((1, LANES), F32), i == 0)
    dy = err * (1.0 / d)
    dy_ref[...] = dy
    _acc(dgate_ref, _colsum(dy * nf), i % tps == 0)
    dn = dy * gate
    _acc(gg_ref, _colsum(dn * fh), i == 0)
    df_ref[...] = _rms_bwd(dn * g_ref[...], fh, r).astype(BF16)


def _bwd_mid_epilogue(dh, i, tps, in_refs, out_refs):
    dy_ref, x1_ref, mix_ref, gpre_ref, gpost_ref, mod_ref = in_refs
    dx1_ref, dmix_ref, dshift_ref, dscale_ref, dgate_ref, ggpre_ref, ggpost_ref = out_refs
    seq_first = i % tps == 0
    x1 = x1_ref[...]
    r = _rms(x1)
    xh = x1 * r
    gpre = gpre_ref[...]
    _acc(dshift_ref, _colsum(dh), seq_first)
    _acc(dscale_ref, _colsum(dh * xh * gpre), seq_first)
    dn = dh * (1.0 + mod_ref[4:5, :])
    _acc(ggpre_ref, _colsum(dn * xh), i == 0)
    dx1 = dy_ref[...] + _rms_bwd(dn * gpre, xh, r)
    dx1_ref[...] = dx1
    mv = mix_ref[...]
    rm = _rms(mv)
    mh = mv * rm
    gpost = gpost_ref[...]
    _acc(dgate_ref, _colsum(dx1 * mh * gpost), seq_first)
    dnm = dx1 * mod_ref[2:3, :]
    _acc(ggpost_ref, _colsum(dnm * mh), i == 0)
    dmix_ref[...] = _rms_bwd(dnm * gpost, mh, rm).astype(BF16)


def _bwd_pre_epilogue(dh, i, tps, in_refs, out_refs):
    dx1_ref, x_ref, g_ref, mod_ref = in_refs
    gx_ref, dshift_ref, dscale_ref, gg_ref = out_refs
    seq_first = i % tps == 0
    xv = x_ref[...]
    r = _rms(xv)
    xh = xv * r
    g = g_ref[...]
    _acc(dshift_ref, _colsum(dh), seq_first)
    _acc(dscale_ref, _colsum(dh * xh * g), seq_first)
    dn = dh * (1.0 + mod_ref[1:2, :])
    _acc(gg_ref, _colsum(dn * xh), i == 0)
    gx_ref[...] = dx1_ref[...] + _rms_bwd(dn * g, xh, r)


def _ffn_up(h2, wgu, tm, tn):
    t, d = h2.shape
    f = wgu.shape[1]

    def body(h_ref, w_ref, gu_ref, act_ref):
        h = h_ref[...]
        g = _dot_nt(h, w_ref[0])
        u = _dot_nt(h, w_ref[1])
        gu_ref[0] = g.astype(BF16)
        gu_ref[1] = u.astype(BF16)
        act_ref[...] = (g * jax.nn.sigmoid(g) * u).astype(BF16)

    return pl.pallas_call(
        body, name="ffn_up", grid=(t // tm, f // tn),
        out_shape=(jax.ShapeDtypeStruct((2, t, f), BF16), jax.ShapeDtypeStruct((t, f), BF16)),
        in_specs=[pl.BlockSpec((tm, d), lambda i, j: (i, 0)), pl.BlockSpec((2, tn, d), lambda i, j: (0, j, 0))],
        out_specs=(pl.BlockSpec((2, tm, tn), lambda i, j: (0, i, j)), pl.BlockSpec((tm, tn), lambda i, j: (i, j))),
        compiler_params=_params(),
    )(h2, wgu)


def _ffn_act_bwd(df, wd, gu, tm, tn):
    t, d = df.shape
    f = wd.shape[0]

    def body(df_ref, w_ref, gu_ref, dgu_ref):
        da = _dot_nt(df_ref[...], w_ref[...])
        g = gu_ref[0].astype(F32)
        u = gu_ref[1].astype(F32)
        s = jax.nn.sigmoid(g)
        silu = g * s
        dgu_ref[0] = (da * u * (s + silu * (1.0 - s))).astype(BF16)
        dgu_ref[1] = (da * silu).astype(BF16)

    return pl.pallas_call(
        body, name="ffn_act_bwd", grid=(t // tm, f // tn),
        out_shape=jax.ShapeDtypeStruct((2, t, f), BF16),
        in_specs=[pl.BlockSpec((tm, d), lambda i, j: (i, 0)), pl.BlockSpec((tn, d), lambda i, j: (j, 0)),
                  pl.BlockSpec((2, tm, tn), lambda i, j: (0, i, j))],
        out_specs=pl.BlockSpec((2, tm, tn), lambda i, j: (0, i, j)),
        compiler_params=_params(),
    )(df, wd, gu)


SIGN_BIT = 0x80000000
Q_SCALE = 1.0 / math.sqrt(HEAD_DIM)


def _split_dot(v, tri2):
    hi = v.astype(BF16)
    lo = (v - hi.astype(F32)).astype(BF16)
    return _dot_nn(jnp.concatenate([hi, lo], axis=1), tri2)


def _sb_tile(qs, k2, mask, ntri2, cur):
    z = _dot_nt(qs, k2)
    neg_abs = lax.bitcast_convert_type(lax.bitcast_convert_type(z, jnp.uint32) | jnp.uint32(SIGN_BIT), F32)
    sp = jnp.maximum(z, 0.0) + jnp.log(1.0 + jnp.exp(neg_abs))
    if mask is not None:
        sp = jnp.where(mask, sp, 0.0)
    w = jnp.exp(z + _split_dot(sp, ntri2) + cur)
    if mask is not None:
        w = jnp.where(mask, w, 0.0)
    return z, sp, w


def _stack_heads(v, lane, scale=None):
    if scale is not None:
        v = v * jnp.asarray(scale, v.dtype)
    zero = jnp.zeros_like(v)
    return jnp.concatenate([jnp.where(lane < HEAD_DIM, v, zero), jnp.where(lane >= HEAD_DIM, v, zero)], axis=0)


def _diag_mask(tq):
    row = lax.broadcasted_iota(jnp.int32, (2 * tq, tq), 0)
    col = lax.broadcasted_iota(jnp.int32, (2 * tq, tq), 1)
    return col < jnp.where(row >= tq, row - tq, row)


def _attn_fwd(proj, tri_after, n_seq, seq, ag_srcs, ag_out_shapes, ag_dests):
    t = proj.shape[0]
    tq = ATT_TILE
    n_pair = (proj.shape[1] // 4) // LANES
    n_ag, n_ag_out = len(ag_srcs), len(ag_out_shapes)
    n_steps = n_seq * n_pair

    def body(q_ref, k_ref, v_ref, tri_ref, *rest):
        ag_src, rest = rest[:n_ag], rest[n_ag:]
        o_ref, cs_ref = rest[:2]
        ag_out, rest = rest[2:2 + n_ag_out], rest[2 + n_ag_out:]
        oacc, cmat, carry = rest[:3]
        ag_start, ag_forward, ag_finish = _ag_phases(ag_dests, ag_src, ag_out, *rest[3:])
        step = pl.program_id(0) * n_pair + pl.program_id(1)
        pl.when(step == 0)(ag_start)
        pl.when(step == (5 * n_steps) // 8)(ag_forward)
        lane = lax.broadcasted_iota(jnp.int32, (1, LANES), 1)
        ntri2 = tri_ref[...]
        diag = _diag_mask(tq)

        def q_tile(qi, _):
            r0 = pl.multiple_of(qi * tq, tq)
            qs = _stack_heads(q_ref[pl.ds(r0, tq), :], lane, Q_SCALE)
            carry[...] = jnp.zeros_like(carry)
            cmat[...] = jnp.zeros_like(cmat)
            oacc[...] = jnp.zeros_like(oacc)

            def run_tiles(tiles):
                cur = carry[...]
                cm = cmat[...]
                pv = None
                for kb, mask in tiles:
                    c0 = pl.multiple_of(kb * tq, tq)
                    _, sp, w = _sb_tile(qs, k_ref[pl.ds(c0, tq), :], mask, ntri2, cur)
                    p = _dot_nn(w.astype(BF16), v_ref[pl.ds(c0, tq), :])
                    pv = p if pv is None else pv + p
                    cm = jnp.where(lane == kb, cur, cm)
                    cur = cur - jnp.sum(sp, axis=1, keepdims=True)
                oacc[...] += pv
                cmat[...] = cm
                carry[...] = cur

            odd = qi % 2

            @pl.when(odd == 0)
            def _():
                run_tiles([(qi, diag)])

            @pl.when(odd == 1)
            def _():
                run_tiles([(qi, diag), (qi - 1, None)])

            def pair(j, _):
                kb = qi - 1 - odd - 2 * j
                run_tiles([(kb, None), (kb - 1, None)])
                return 0

            lax.fori_loop(0, qi // 2, pair, 0)
            cs_ref[pl.ds(r0, tq), 0:LANES] = cmat[0:tq, :]
            cs_ref[pl.ds(r0, tq), LANES:2 * LANES] = cmat[tq:2 * tq, :]
            o_ref[pl.ds(r0, tq), :] = jnp.where(lane < HEAD_DIM, oacc[0:tq, :], oacc[tq:2 * tq, :]).astype(BF16)
            return 0

        lax.fori_loop(0, seq // tq, q_tile, 0)
        pl.when(step == n_steps - 1)(ag_finish)

    blk = lambda off: pl.BlockSpec((seq, LANES), lambda b, p: (b, off + p))
    any_spec = pl.BlockSpec(memory_space=pl.ANY)
    return pl.pallas_call(
        body, name="attn_fwd", grid=(n_seq, n_pair),
        out_shape=(jax.ShapeDtypeStruct((t, n_pair * LANES), BF16),
                   jax.ShapeDtypeStruct((t, n_pair * 2 * LANES), F32), *ag_out_shapes),
        in_specs=[blk(0), blk(n_pair), blk(2 * n_pair), pl.BlockSpec((2 * tq, tq), lambda b, p: (0, 0))]
        + [any_spec] * n_ag,
        out_specs=(pl.BlockSpec((seq, LANES), lambda b, p: (b, p)),
                   pl.BlockSpec((seq, 2 * LANES), lambda b, p: (b, p)), *([any_spec] * n_ag_out)),
        scratch_shapes=[pltpu.VMEM((2 * tq, LANES), F32), pltpu.VMEM((2 * tq, LANES), F32),
                        pltpu.VMEM((2 * tq, 1), F32)] + _ag_scratch(n_ag),
        compiler_params=_params(),
    )(proj, proj, proj, tri_after, *ag_srcs)


def _attn_bwd(proj, dcat, cstats, tri_after, tri_incl, n_seq, seq, rs_sends):
    t = proj.shape[0]
    tq = ATT_TILE
    width = proj.shape[1] // 4
    n_pair = width // LANES
    n_rs = len(rs_sends)
    rs_shapes = [r.shape for r in rs_sends]
    n_steps = n_seq * n_pair

    def body(q_ref, k_ref, v_ref, do_ref, cs_ref, tria_ref, trii_ref, *rest):
        rs_src, rest = rest[:n_rs], rest[n_rs:]
        out_ref = rest[0]
        rs_dst, rest = rest[1:1 + n_rs], rest[1 + n_rs:]
        dq_acc, dk_acc, dv_acc, ecarry = rest[:4]
        rs_start, rs_finish = _rs_phases(rs_shapes, rs_src, rs_dst, *rest[4:])
        step = pl.program_id(0) * n_pair + pl.program_id(1)
        pl.when(step == 0)(rs_start)
        lane = lax.broadcasted_iota(jnp.int32, (1, LANES), 1)
        ntri2 = tria_ref[...]
        tri_i2 = trii_ref[...]
        diag = _diag_mask(tq)
        dk_acc[...] = jnp.zeros_like(dk_acc)
        dv_acc[...] = jnp.zeros_like(dv_acc)

        def q_tile(qi, _):
            r0 = pl.multiple_of(qi * tq, tq)
            qs = _stack_heads(q_ref[pl.ds(r0, tq), :], lane, Q_SCALE)
            dos = _stack_heads(do_ref[pl.ds(r0, tq), :], lane)
            cs = jnp.concatenate([cs_ref[pl.ds(r0, tq), 0:LANES], cs_ref[pl.ds(r0, tq), LANES:2 * LANES]], axis=0)
            ecarry[...] = jnp.zeros_like(ecarry)
            dq_acc[...] = jnp.zeros_like(dq_acc)

            def run_tiles(tiles):
                ec = ecarry[...]
                dq = None
                for kb, mask in tiles:
                    c0 = pl.multiple_of(kb * tq, tq)
                    k2 = k_ref[pl.ds(c0, tq), :]
                    v2 = v_ref[pl.ds(c0, tq), :]
                    cur = jnp.sum(jnp.where(lane == kb, cs, 0.0), axis=1, keepdims=True)
                    z, sp, w = _sb_tile(qs, k2, mask, ntri2, cur)
                    ee = w * _dot_nt(dos, v2)
                    einc = _split_dot(ee, tri_i2) + ec
                    dz = ee - jnp.exp(z - sp) * einc
                    if mask is not None:
                        dz = jnp.where(mask, dz, 0.0)
                    dzb = dz.astype(BF16)
                    p = _dot_nn(dzb, k2)
                    dq = p if dq is None else dq + p
                    dk_acc[pl.ds(c0, tq), :] += _dot_tn(dzb, qs)
                    dv_acc[pl.ds(c0, tq), :] += _dot_tn(w.astype(BF16), dos)
                    ec = ec + jnp.sum(ee, axis=1, keepdims=True)
                dq_acc[...] += dq
                ecarry[...] = ec

            def pair(j, _):
                run_tiles([(2 * j, None), (2 * j + 1, None)])
                return 0

            lax.fori_loop(0, qi // 2, pair, 0)
            odd = qi % 2

            @pl.when(odd == 0)
            def _():
                run_tiles([(qi, diag)])

            @pl.when(odd == 1)
            def _():
                run_tiles([(qi - 1, None), (qi, diag)])

            dq = jnp.where(lane < HEAD_DIM, dq_acc[0:tq, :], dq_acc[tq:2 * tq, :])
            out_ref[0, pl.ds(r0, tq), :] = (dq * Q_SCALE).astype(BF16)
            return 0

        lax.fori_loop(0, seq // tq, q_tile, 0)
        out_ref[1] = dk_acc[...].astype(BF16)
        out_ref[2] = dv_acc[...].astype(BF16)
        pl.when(step == n_steps - 1)(rs_finish)

    blk = lambda off: pl.BlockSpec((seq, LANES), lambda b, p: (b, off + p))
    tri_spec = pl.BlockSpec((2 * tq, tq), lambda b, p: (0, 0))
    any_spec = pl.BlockSpec(memory_space=pl.ANY)
    return pl.pallas_call(
        body, name="attn_bwd", grid=(n_seq, n_pair),
        out_shape=(jax.ShapeDtypeStruct((4, t, width), BF16), *_rs_out(rs_sends)),
        in_specs=[blk(0), blk(n_pair), blk(2 * n_pair), pl.BlockSpec((seq, LANES), lambda b, p: (b, p)),
                  pl.BlockSpec((seq, 2 * LANES), lambda b, p: (b, p)), tri_spec, tri_spec] + [any_spec] * n_rs,
        out_specs=(pl.BlockSpec((3, seq, LANES), lambda b, p: (0, b, p)), *([any_spec] * n_rs)),
        scratch_shapes=[pltpu.VMEM((2 * tq, LANES), F32), pltpu.VMEM((seq, LANES), F32),
                        pltpu.VMEM((seq, LANES), F32), pltpu.VMEM((2 * tq, 1), F32)] + _rs_scratch(rs_sends),
        compiler_params=_params(),
    )(proj, proj, proj, dcat, cstats, tri_after, tri_incl, *rs_sends)


def _window_terms(g, rows):
    win = jnp.where(g == 0, POOL_WINDOWS[0], jnp.where(g == 1, POOL_WINDOWS[1],
                    jnp.where(g == 2, POOL_WINDOWS[2], POOL_WINDOWS[3])))
    cnt = jnp.minimum(rows + 1, win).astype(F32)
    return win, cnt


def _window_sum(v, g, rows, forward):
    s_len = v.shape[0]
    sums = []
    s = v
    for step in range(len(POOL_WINDOWS)):
        sh = 1 << step
        if forward:
            shifted = jnp.where(rows < s_len - sh, pltpu.roll(s, s_len - sh, axis=0), 0.0)
        else:
            shifted = jnp.where(rows >= sh, pltpu.roll(s, sh, axis=0), 0.0)
        s = s + shifted
        sums.append(s)
    return jnp.where(g == 0, sums[0], jnp.where(g == 1, sums[1], jnp.where(g == 2, sums[2], sums[3])))


def _pooled(u, g, rows):
    _, cnt = _window_terms(g, rows)
    return _window_sum(u, g, rows, forward=False) / cnt - u


def _pool_fwd(proj, w_pool, pool_scale, n_seq, seq):
    t = proj.shape[0]
    n_grp = len(POOL_WINDOWS)
    u_off = 3 * (proj.shape[1] // 4) // LANES

    def body(u_ref, w_ref, s_ref, o_ref):
        g = pl.program_id(1)
        rows = lax.broadcasted_iota(jnp.int32, (seq, 1), 0)
        pooled = _pooled(u_ref[...].astype(F32), g, rows)
        y = _dot_nn(pooled.astype(BF16), w_ref[...].astype(BF16))
        o_ref[...] = (y * s_ref[...]).astype(BF16)

    return pl.pallas_call(
        body, name="pool_fwd", grid=(n_seq, n_grp),
        out_shape=jax.ShapeDtypeStruct((t, n_grp * POOL_GROUP_DIM), BF16),
        in_specs=[pl.BlockSpec((seq, LANES), lambda b, g: (b, u_off + g)),
                  pl.BlockSpec((None, POOL_GROUP_DIM, POOL_GROUP_DIM), lambda b, g: (g, 0, 0)),
                  pl.BlockSpec((1, POOL_GROUP_DIM), lambda b, g: (0, g))],
        out_specs=pl.BlockSpec((seq, LANES), lambda b, g: (b, g)),
        compiler_params=_params(),
    )(proj, w_pool, pool_scale)


def _pool_bwd(proj, dcat, w_pool, pool_scale, dqkv, n_seq, seq):
    n_grp = len(POOL_WINDOWS)
    width = proj.shape[1] // 4
    u_off = 3 * width // LANES
    dp_off = width // LANES

    def body(u_ref, dp_ref, w_ref, s_ref, alias_ref, du_ref, gw_ref, gs_ref):
        del alias_ref
        g = pl.program_id(0)
        b = pl.program_id(1)
        rows = lax.broadcasted_iota(jnp.int32, (seq, 1), 0)
        pooled = _pooled(u_ref[...].astype(F32), g, rows)
        pb = pooled.astype(BF16)
        wb = w_ref[...].astype(BF16)
        z = _dot_nn(pb, wb)
        dp = dp_ref[...].astype(F32)
        _acc(gs_ref, _colsum(dp * z), b == 0)
        dys = (dp * s_ref[...]).astype(BF16)
        _acc(gw_ref, _dot_tn(pb, dys), b == 0)
        dpooled = _dot_nt(dys, wb)
        _, cnt = _window_terms(g, rows)
        du = _window_sum(dpooled / cnt, g, rows, forward=True) - dpooled
        du_ref[...] = du.astype(BF16)

    t = proj.shape[0]
    return pl.pallas_call(
        body, name="pool_bwd", grid=(n_grp, n_seq),
        out_shape=(jax.ShapeDtypeStruct(dqkv.shape, BF16),
                   jax.ShapeDtypeStruct((n_grp, POOL_GROUP_DIM, POOL_GROUP_DIM), F32),
                   jax.ShapeDtypeStruct((1, n_grp * POOL_GROUP_DIM), F32)),
        in_specs=[pl.BlockSpec((seq, LANES), lambda g, b: (b, u_off + g)),
                  pl.BlockSpec((seq, LANES), lambda g, b: (b, dp_off + g)),
                  pl.BlockSpec((None, POOL_GROUP_DIM, POOL_GROUP_DIM), lambda g, b: (g, 0, 0)),
                  pl.BlockSpec((1, POOL_GROUP_DIM), lambda g, b: (0, g)),
                  pl.BlockSpec(memory_space=pl.ANY)],
        out_specs=(pl.BlockSpec((None, seq, LANES), lambda g, b: (3, b, g)),
                   pl.BlockSpec((None, POOL_GROUP_DIM, POOL_GROUP_DIM), lambda g, b: (g, 0, 0)),
                   pl.BlockSpec((1, POOL_GROUP_DIM), lambda g, b: (0, g))),
        input_output_aliases={4: 0},
        compiler_params=_params(),
    )(proj, dcat, w_pool, pool_scale, dqkv)


def _cond_fwd(c_all, w_cond, b_cols):
    n, _ = c_all.shape
    cols = w_cond.shape[1]

    def body(c_ref, w_ref, b_ref, o_ref):
        cv = c_ref[...]
        a = cv * jax.nn.sigmoid(cv)
        o_ref[...] = jnp.dot(a, w_ref[...], preferred_element_type=F32,
                             precision=lax.Precision.HIGHEST) + b_ref[...]

    return pl.pallas_call(
        body, name="cond_fwd", out_shape=jax.ShapeDtypeStruct((n, cols), F32),
        compiler_params=_params(),
    )(c_all, w_cond, b_cols)


def _cond_bwd(c_all, dmod_all, dmod_cols):
    n, d = c_all.shape
    cols = dmod_cols.shape[1]

    def body(c_ref, dm_ref, dmc_ref, gw_ref, gb_ref):
        cv = c_ref[...]
        a = cv * jax.nn.sigmoid(cv)
        gw_ref[...] = lax.dot_general(a, dmc_ref[...], (((0,), (0,)), ((), ())),
                                      preferred_element_type=F32, precision=lax.Precision.HIGHEST)
        gb_ref[...] = _colsum(dm_ref[...])

    return pl.pallas_call(
        body, name="cond_bwd",
        out_shape=(jax.ShapeDtypeStruct((d, cols), F32), jax.ShapeDtypeStruct((1, dmod_all.shape[1]), F32)),
        compiler_params=_params(),
    )(c_all, dmod_all, dmod_cols)


def _adamw_math(w, g, m, v):
    m = ADAM_B1 * m + (1.0 - ADAM_B1) * g
    v = ADAM_B2 * v + (1.0 - ADAM_B2) * (g * g)
    m_hat = m / (1.0 - ADAM_B1 ** ADAM_STEP)
    v_hat = v / (1.0 - ADAM_B2 ** ADAM_STEP)
    delta = -ADAM_LR * (m_hat / (jnp.sqrt(v_hat) + ADAM_EPS) + ADAM_WD * w)
    return delta, m, v


def _adamw(w, g, m, v, rows, name):
    r, cdim = w.shape

    def body(w_ref, g_ref, m_ref, v_ref, d_ref, nm_ref, nv_ref):
        d_ref[...], nm_ref[...], nv_ref[...] = _adamw_math(w_ref[...], g_ref[...], m_ref[...], v_ref[...])

    spec = pl.BlockSpec((rows, cdim), lambda i: (i, 0))
    sds = jax.ShapeDtypeStruct((r, cdim), F32)
    return pl.pallas_call(
        body, name=name, grid=(r // rows,), out_shape=(sds, sds, sds),
        in_specs=[spec] * 4, out_specs=(spec, spec, spec), compiler_params=_params(),
    )(w, g, m, v)


def _adamw_small(ws, gparts, ms, vs, name):
    n = len(ws)

    def body(*refs):
        w_r, g_r, m_r, v_r = refs[:n], refs[n:2 * n], refs[2 * n:3 * n], refs[3 * n:4 * n]
        outs = refs[4 * n:]
        for i in range(n):
            g = g_r[i][0]
            for dev in range(1, g_r[i].shape[0]):
                g = g + g_r[i][dev]
            delta, m, v = _adamw_math(w_r[i][...], g, m_r[i][...], v_r[i][...])
            outs[i][...] = g
            outs[n + i][...] = delta
            outs[2 * n + i][...] = m
            outs[3 * n + i][...] = v

    sds = [jax.ShapeDtypeStruct(w.shape, F32) for w in ws]
    return pl.pallas_call(
        body, name=name, out_shape=tuple(sds * 4), compiler_params=_params(),
    )(*ws, *gparts, *ms, *vs)


def kernel(x, c, w_cond, b_cond, g_mix_pre, g_mix_post, w_in, w_pool, pool_scale, w_out, g_ffn_pre, g_ffn_post, w_gate, w_up, w_down, loss_target, m_w_cond, m_b_cond, m_g_mix_pre, m_g_mix_post, m_w_in, m_w_pool, m_pool_scale, m_w_out, m_g_ffn_pre, m_g_ffn_post, m_w_gate, m_w_up, m_w_down, v_w_cond, v_b_cond, v_g_mix_pre, v_g_mix_post, v_w_in, v_w_pool, v_pool_scale, v_w_out, v_g_ffn_pre, v_g_ffn_post, v_w_gate, v_w_up, v_w_down):
    n_seq, seq, d = x.shape
    t = n_seq * seq
    xi, yi, ci = _mesh_pos()
    me = 4 * xi + 2 * yi + ci
    x2 = x.reshape(t, d)
    tgt2 = loss_target.reshape(t, d)
    in_rows = w_in.shape[2]
    out_rows = w_out.shape[1]
    ff_rows = w_gate.shape[2]
    ff = N_DEV * ff_rows
    cond_cols = w_cond.shape[2]

    win_t = w_in[0].T.astype(BF16)
    wout_s = w_out[0].astype(BF16)
    wg_t = w_gate[0].T.astype(BF16)
    wu_t = w_up[0].T.astype(BF16)
    wd_s = w_down[0].astype(BF16)
    c_all, win_g = _all_gather(
        [c, win_t],
        [jax.ShapeDtypeStruct((N_DEV, n_seq, d), F32), jax.ShapeDtypeStruct((N_DEV, in_rows, d), BF16)],
        [(0, ()), (1, ())], "ag_c_win")
    c_all = c_all.reshape(N_DEV * n_seq, d)
    win_full = win_g.reshape(N_DEV * in_rows, d)

    b_cols = lax.dynamic_slice_in_dim(b_cond, me * cond_cols, cond_cols, axis=1)
    mod_cols = _cond_fwd(c_all, w_cond[0], b_cols)
    (mod_g,) = _all_gather([mod_cols], [jax.ShapeDtypeStruct((N_DEV,) + mod_cols.shape, F32)], [(0, ())], "ag_mod")
    mod_mine = lax.dynamic_slice_in_dim(mod_g, me * n_seq, n_seq, axis=1)
    mod = jnp.transpose(mod_mine, (1, 0, 2)).reshape(n_seq, N_MOD, d)

    h1 = _pre_mix(x2, g_mix_pre, mod, seq)
    proj = _matmul(h1, win_full, "nt", BF16, 1024, 512, d, "proj")
    tq = ATT_TILE
    ids = jnp.arange(tq)
    tri_after = jnp.tile(-(ids[:, None] >= ids[None, :]).astype(BF16), (2, 1))
    tri_incl = jnp.tile((ids[:, None] <= ids[None, :]).astype(BF16), (2, 1))
    attn, cstats, wout_g, wgu_g, wd_g = _attn_fwd(
        proj, tri_after, n_seq, seq, [wout_s, wg_t, wu_t, wd_s],
        [jax.ShapeDtypeStruct((N_DEV, out_rows, d), BF16), jax.ShapeDtypeStruct((2, N_DEV, ff_rows, d), BF16),
         jax.ShapeDtypeStruct((N_DEV, ff_rows, d), BF16)],
        [(0, ()), (1, (0,)), (1, (1,)), (2, ())])
    wout_full = wout_g.reshape(N_DEV * out_rows, d)
    wgu_full = wgu_g.reshape(2, ff, d)
    wd_full = wd_g.reshape(ff, d)
    pool = _pool_fwd(proj, w_pool[0], pool_scale, n_seq, seq)
    cat = jnp.stack([attn, pool])
    tok_f32, tok_bf16 = jax.ShapeDtypeStruct((t, d), F32), jax.ShapeDtypeStruct((t, d), BF16)
    seq_sds, vec_sds = jax.ShapeDtypeStruct((n_seq, 1, d), F32), jax.ShapeDtypeStruct((1, d), F32)
    mix, x1, h2 = _matmul_rows(
        cat, wout_full.reshape(2, d // 2, d), ROW_TILE, d // 2, seq, "mix_mid", _mid_epilogue,
        [x2, g_mix_post, g_ffn_pre, mod], ["tok", "vec", "vec", "mod"],
        [tok_f32, tok_f32, tok_bf16], ["tok", "tok", "tok"])
    gu, act = _ffn_up(h2, wgu_full, 512, ff // 2)
    loss_sum, dy, df, dgate_f, gg_ffn_post = _matmul_rows(
        act, wd_full, ROW_TILE, ff, seq, "ffn_down_post", _post_epilogue,
        [x1, tgt2, g_ffn_post, mod], ["tok", "tok", "vec", "mod"],
        [jax.ShapeDtypeStruct((1, LANES), F32), tok_f32, tok_bf16, seq_sds, vec_sds],
        ["loss", "tok", "tok", "seq", "vec"])

    dgu = _ffn_act_bwd(df, wd_full, gu, 512, ff // 2)
    gwd, gwd_b = _matmul(act, df, "tn", F32, ff // 2, d, 1024, "grad_w_down", bf16_copy=True)
    gwgu, gwgu_b = _matmul(dgu, h2, "tn", F32, ff // 2, d, 1024, "grad_w_gate_up", bf16_copy=True)
    dx1, dmix, dshift_f, dscale_f, dgate_m, gg_ffn_pre, gg_mix_post = _matmul_rows(
        dgu, wgu_full, ROW_TILE, ff, seq, "dh2_bwd_mid", _bwd_mid_epilogue,
        [dy, x1, mix, g_ffn_pre, g_mix_post, mod], ["tok", "tok", "tok", "vec", "vec", "mod"],
        [tok_f32, tok_bf16, seq_sds, seq_sds, seq_sds, vec_sds, vec_sds],
        ["tok", "tok", "seq", "seq", "seq", "vec", "vec"])
    dcat = _matmul(dmix, wout_full, "nt", BF16, 1024, 512, d, "dcat")
    gwout, gwout_b = _matmul(cat, dmix, "tn", F32, d // 2, d, 1024, "grad_w_out", bf16_copy=True)
    dqkv, rv_wgu, rv_wd, rv_wout = _attn_bwd(
        proj, dcat, cstats, tri_after, tri_incl, n_seq, seq,
        [gwgu_b.reshape(2, N_DEV, ff_rows, d), gwd_b.reshape(1, N_DEV, ff_rows, d),
         gwout_b.reshape(1, N_DEV, out_rows, d)])
    dproj, gw_pool, gs_pool = _pool_bwd(proj, dcat, w_pool[0], pool_scale, dqkv, n_seq, seq)
    gwin, gwin_b = _matmul(dproj, h1, "tn", F32, d // 2, d, 1024, "grad_w_in", bf16_copy=True)
    grad_x, dshift_m, dscale_m, gg_mix_pre, rv_win = _matmul_rows(
        dproj, win_full.reshape(4, d // 2, d), ROW_TILE, d // 2, seq, "dh1_bwd_pre", _bwd_pre_epilogue,
        [dx1, x2, g_mix_pre, mod], ["tok", "tok", "vec", "mod"],
        [tok_f32, seq_sds, seq_sds, vec_sds], ["tok", "seq", "seq", "vec"],
        rs_sends=[gwin_b.reshape(1, N_DEV, in_rows, d)])

    r_wgu = _rs_final(gwgu.reshape(2, N_DEV, ff_rows, d), rv_wgu, "rs_final_gate_up")
    r_wd = _rs_final(gwd.reshape(1, N_DEV, ff_rows, d), rv_wd, "rs_final_down")
    r_wout = _rs_final(gwout.reshape(1, N_DEV, out_rows, d), rv_wout, "rs_final_out")
    r_win = _rs_final(gwin.reshape(1, N_DEV, in_rows, d), rv_win, "rs_final_in")
    grad_w_in = r_win[0].T
    grad_w_out = r_wout[0]
    grad_w_gate = r_wgu[0].T
    grad_w_up = r_wgu[1].T
    grad_w_down = r_wd[0]

    dmod = jnp.concatenate([dshift_m, dscale_m, dgate_m, dshift_f, dscale_f, dgate_f], axis=1)
    small = jnp.concatenate([gg_mix_pre, gg_mix_post, gg_ffn_pre, gg_ffn_post,
                             jnp.pad(gs_pool, ((0, 0), (0, d - gs_pool.shape[1]))),
                             jnp.pad(loss_sum, ((0, 0), (0, d - loss_sum.shape[1]))), jnp.zeros((2, d), F32),
                             gw_pool.reshape(-1, d), dmod.reshape(n_seq * N_MOD, d)], axis=0)
    n_gw = gw_pool.size // d
    (small_g,) = _all_gather([small], [jax.ShapeDtypeStruct((N_DEV,) + small.shape, F32)], [(0, ())], "ag_small")
    loss = jnp.sum(small_g[:, 5, 0]) * (0.5 / d)
    dmod_all = small_g[:, 8 + n_gw:, :].reshape(N_DEV * n_seq, N_MOD * d)
    dmod_cols = lax.dynamic_slice_in_dim(dmod_all, me * cond_cols, cond_cols, axis=1)
    grad_w_cond, grad_b_cond = _cond_bwd(c_all, dmod_all, dmod_cols)

    small_ws = [g_mix_pre, g_mix_post, g_ffn_pre, g_ffn_post, pool_scale, w_pool.reshape(-1, POOL_GROUP_DIM)]
    small_ms = [m_g_mix_pre, m_g_mix_post, m_g_ffn_pre, m_g_ffn_post, m_pool_scale, m_w_pool.reshape(-1, POOL_GROUP_DIM)]
    small_vs = [v_g_mix_pre, v_g_mix_post, v_g_ffn_pre, v_g_ffn_post, v_pool_scale, v_w_pool.reshape(-1, POOL_GROUP_DIM)]
    small_gparts = [small_g[:, 0:1, :], small_g[:, 1:2, :], small_g[:, 2:3, :], small_g[:, 3:4, :],
                    small_g[:, 4:5, :pool_scale.shape[1]],
                    small_g[:, 8:8 + n_gw, :].reshape(N_DEV, -1, POOL_GROUP_DIM)]
    so = _adamw_small(small_ws, small_gparts, small_ms, small_vs, "adamw_small")
    ns = len(small_ws)
    sg, sdl, sm, sv = so[:ns], so[ns:2 * ns], so[2 * ns:3 * ns], so[3 * ns:]
    pool_shape = w_pool.shape
    fix = lambda lst: [lst[0], lst[1], lst[2], lst[3], lst[4], lst[5].reshape(pool_shape)]
    sg, sdl, sm, sv = fix(sg), fix(sdl), fix(sm), fix(sv)

    def big(w, g, m, v, rows, name):
        dl, nm, nv = _adamw(w[0], g, m[0], v[0], rows, name)
        return g[None], dl[None], nm[None], nv[None]

    o_cond = big(w_cond, grad_w_cond, m_w_cond, v_w_cond, 256, "adamw_w_cond")
    o_bcond = _adamw(b_cond, grad_b_cond, m_b_cond, v_b_cond, 1, "adamw_b_cond")
    o_bcond = (grad_b_cond,) + tuple(o_bcond)
    o_in = big(w_in, grad_w_in, m_w_in, v_w_in, 256, "adamw_w_in")
    o_out = big(w_out, grad_w_out, m_w_out, v_w_out, out_rows, "adamw_w_out")
    o_gate = big(w_gate, grad_w_gate, m_w_gate, v_w_gate, 256, "adamw_w_gate")
    o_up = big(w_up, grad_w_up, m_w_up, v_w_up, 256, "adamw_w_up")
    o_down = big(w_down, grad_w_down, m_w_down, v_w_down, ff_rows, "adamw_w_down")

    def pick(k):
        small_k = [sg, sdl, sm, sv][k]
        return [o_cond[k], o_bcond[k], small_k[0], small_k[1], o_in[k], small_k[5], small_k[4], o_out[k],
                small_k[2], small_k[3], o_gate[k], o_up[k], o_down[k]]

    return (loss, grad_x.reshape(n_seq, seq, d), *pick(0), *pick(1), *pick(2), *pick(3))
```

```python
import functools
import math

import jax
import jax.numpy as jnp
from jax import lax
from jax.experimental import pallas as pl
from jax.experimental.pallas import tpu as pltpu

F32 = jnp.float32
BF16 = jnp.bfloat16
MESH = pl.DeviceIdType.MESH

N_DEV = 8
HEAD_DIM = 64
LANES = 128
POOL_WINDOWS = (2, 4, 8, 16)
POOL_GROUP_DIM = 128
N_MOD = 6
EPS = 1e-6
ATT_TILE = 256
ATT_PAIRS = 2
VMEM_LIMIT = 56 * 1024 * 1024

ADAM_LR = 0.001
ADAM_B1 = 0.9
ADAM_B2 = 0.999
ADAM_EPS = 1e-08
ADAM_WD = 0.01
ADAM_STEP = 10


def _params(**kw):
    return pltpu.CompilerParams(vmem_limit_bytes=VMEM_LIMIT, **kw)


def _dot_nn(a, b):
    return jnp.dot(a, b, preferred_element_type=F32)


def _dot_nt(a, b):
    return lax.dot_general(a, b, (((1,), (1,)), ((), ())), preferred_element_type=F32)


def _dot_tn(a, b):
    return lax.dot_general(a, b, (((0,), (0,)), ((), ())), preferred_element_type=F32)


def _mesh_pos():
    return lax.axis_index("x"), lax.axis_index("y"), lax.axis_index("c")


def _ag_phases(dests, src, outs, send_sems, recv_sems, local_sems):
    n = len(src)
    x, y, c = _mesh_pos()
    me, sibling = (x, y, c), (x, y, 1 - c)
    chips = [(1 - x, y), (x, 1 - y), (1 - x, 1 - y)]

    def slot(i, dev):
        oi, prefix = dests[i]
        px, py, pc = dev
        return outs[oi].at[prefix + (4 * px + 2 * py + pc,)]

    def copy(i, k, block, to, from_src=False):
        return pltpu.make_async_remote_copy(
            src_ref=src[i] if from_src else slot(i, block), dst_ref=slot(i, block),
            send_sem=send_sems.at[i, k], recv_sem=recv_sems.at[i, k],
            device_id=to, device_id_type=MESH)

    def mine(i):
        return pltpu.make_async_copy(src[i], slot(i, me), local_sems.at[i])

    def first(i):
        return [copy(i, 0, me, sibling, from_src=True)] + [
            copy(i, 1 + j, me, (*chip, c), from_src=True) for j, chip in enumerate(chips)]

    def passed(i, j):
        return copy(i, 4 + j, (*chips[j], c), sibling)

    def start():
        for i in range(n):
            mine(i).start()
        for i in range(n):
            for cp in first(i):
                cp.start()

    def forward():
        for j, chip in enumerate(chips):
            for i in range(n):
                copy(i, 1 + j, (*chip, c), me).wait_recv()
                passed(i, j).start()

    def finish():
        for i in range(n):
            copy(i, 0, sibling, me).wait_recv()
            for j, chip in enumerate(chips):
                copy(i, 4 + j, (*chip, 1 - c), me).wait_recv()
        for i in range(n):
            for cp in first(i) + [passed(i, j) for j in range(3)]:
                cp.wait_send()
            mine(i).wait()

    return start, forward, finish


def _ag_scratch(n):
    return [pltpu.SemaphoreType.DMA((n, 7)), pltpu.SemaphoreType.DMA((n, 7)), pltpu.SemaphoreType.DMA((n,))]


def _all_gather(srcs, out_shapes, dests, name):
    n = len(srcs)

    def body(*refs):
        src = refs[:n]
        outs = refs[n:n + len(out_shapes)]
        start, forward, finish = _ag_phases(dests, src, outs, *refs[n + len(out_shapes):])
        start()
        forward()
        finish()

    any_spec = pl.BlockSpec(memory_space=pl.ANY)
    return pl.pallas_call(
        body, name=name,
        out_shape=tuple(out_shapes),
        in_specs=[any_spec] * n,
        out_specs=tuple([any_spec] * len(out_shapes)),
        scratch_shapes=_ag_scratch(n),
    )(*srcs)


def _rs_phases(shapes, src, dst, send_sems, recv_sems):
    x, y, c = _mesh_pos()

    def copies():
        out = []
        n = 0
        for i, shp in enumerate(shapes):
            for m in range(shp[0]):
                for k in range(1, N_DEV):
                    px, py, pc = x ^ (k >> 2), y ^ ((k >> 1) & 1), c ^ (k & 1)
                    out.append(pltpu.make_async_remote_copy(
                        src_ref=src[i].at[m, 4 * px + 2 * py + pc], dst_ref=dst[i].at[m, k - 1],
                        send_sem=send_sems.at[n], recv_sem=recv_sems.at[n],
                        device_id=(px, py, pc), device_id_type=MESH))
                    n += 1
        return out

    def start():
        for cp in copies():
            cp.start()

    def finish():
        for cp in copies():
            cp.wait_send()
        for cp in copies():
            cp.wait_recv()

    return start, finish


def _rs_out(sends):
    return [jax.ShapeDtypeStruct((s.shape[0], N_DEV - 1) + s.shape[2:], s.dtype) for s in sends]


def _rs_scratch(sends):
    total = sum((N_DEV - 1) * s.shape[0] for s in sends)
    return [pltpu.SemaphoreType.DMA((total,)), pltpu.SemaphoreType.DMA((total,))]


def _rs_final(mine, recv, name):
    m_n, _, r, cdim = mine.shape
    x, y, c = _mesh_pos()
    me = jnp.reshape(4 * x + 2 * y + c, (1,)).astype(jnp.int32)

    def body(me_ref, p_ref, r_ref, o_ref):
        del me_ref
        s = p_ref[...]
        for k in range(N_DEV - 1):
            s = s + r_ref[k].astype(F32)
        o_ref[...] = s

    return pl.pallas_call(
        body, name=name, out_shape=jax.ShapeDtypeStruct((m_n, r, cdim), F32),
        grid_spec=pltpu.PrefetchScalarGridSpec(
            num_scalar_prefetch=1, grid=(m_n,),
            in_specs=[pl.BlockSpec((None, None, r, cdim), lambda m, s: (m, s[0], 0, 0)),
                      pl.BlockSpec((None, N_DEV - 1, r, cdim), lambda m, s: (m, 0, 0, 0))],
            out_specs=pl.BlockSpec((None, r, cdim), lambda m, s: (m, 0, 0))),
        compiler_params=_params(),
    )(me, mine, recv)


def _matmul(a, b, mode, out_dtype, tm, tn, tk, name, bf16_copy=False, rs_sends=()):
    ga = a.shape[0] if a.ndim == 3 else None
    gb = b.shape[0] if b.ndim == 3 else None
    a2, b2 = a.shape[-2:], b.shape[-2:]
    if mode == "nn":
        (m, k), n = a2, b2[1]
    elif mode == "nt":
        (m, k), n = a2, b2[0]
    else:
        (k, m), n = a2, b2[1]
    assert m % tm == 0 and n % tn == 0 and k % tk == 0, (name, m, n, k)
    nk = k // tk
    g_n = ga or 1
    batch_out = mode == "tn" and ga is not None
    n_red = nk if batch_out else nk * g_n
    dot = {"nn": _dot_nn, "nt": _dot_nt, "tn": _dot_tn}[mode]
    acc_in_out = out_dtype == F32

    n_rs = len(rs_sends)
    rs_shapes = [r.shape for r in rs_sends]
    n_out = 2 if bf16_copy else 1
    assert not bf16_copy or acc_in_out

    def body(a_ref, b_ref, *rest):
        rs_src, rest = rest[:n_rs], rest[n_rs:]
        o_ref = rest[0]
        copy_ref = rest[1] if bf16_copy else None
        rs_dst, scratch = rest[n_out:n_out + n_rs], rest[n_out + n_rs:]
        if n_rs:
            rs_start, rs_finish = _rs_phases(rs_shapes, rs_src, rs_dst, *scratch[-2:])
            first = functools.reduce(jnp.logical_and, [pl.program_id(ax) == 0 for ax in range(4)])
            last = functools.reduce(jnp.logical_and, [pl.program_id(ax) == grid[ax] - 1 for ax in range(4)])
            pl.when(first)(rs_start)
        p = dot(a_ref[...], b_ref[...])
        kk = pl.program_id(3) if batch_out else pl.program_id(2) * nk + pl.program_id(3)
        if n_red == 1:
            o_ref[...] = p.astype(out_dtype)
            if bf16_copy:
                copy_ref[...] = p.astype(BF16)
        else:
            acc = o_ref if acc_in_out else scratch[0]

            @pl.when(kk == 0)
            def _():
                acc[...] = p

            @pl.when(kk > 0)
            def _():
                acc[...] += p

            @pl.when(kk == n_red - 1)
            def _():
                if not acc_in_out:
                    o_ref[...] = acc[...].astype(out_dtype)
                if bf16_copy:
                    copy_ref[...] = acc[...].astype(BF16)

        if n_rs:
            pl.when(last)(rs_finish)

    def order(ids):
        return ids if batch_out else (ids[2], ids[0], ids[1], ids[3])

    def a_idx(*ids):
        g, i, j, kq = order(ids)
        blk = {"nn": (i, kq), "nt": (i, kq), "tn": (kq, i)}[mode]
        return (g,) + blk if ga is not None else blk

    def b_idx(*ids):
        g, i, j, kq = order(ids)
        blk = {"nn": (kq, j), "nt": (j, kq), "tn": (kq, j)}[mode]
        return (g,) + blk if gb is not None else blk

    def o_idx(*ids):
        g, i, j, kq = order(ids)
        return (g, i, j) if batch_out else (i, j)

    a_blk = {"nn": (tm, tk), "nt": (tm, tk), "tn": (tk, tm)}[mode]
    b_blk = {"nn": (tk, tn), "nt": (tn, tk), "tn": (tk, tn)}[mode]
    if ga is not None:
        a_blk = (None,) + a_blk
    if gb is not None:
        b_blk = (None,) + b_blk
    if batch_out:
        out_shape = jax.ShapeDtypeStruct((g_n, m, n), out_dtype)
        o_blk = (None, tm, tn)
        grid = (g_n, m // tm, n // tn, nk)
    else:
        out_shape = jax.ShapeDtypeStruct((m, n), out_dtype)
        o_blk = (tm, tn)
        grid = (m // tm, n // tn, g_n, nk)
    scratch = [] if (acc_in_out or n_red == 1) else [pltpu.VMEM((tm, tn), F32)]
    any_spec = pl.BlockSpec(memory_space=pl.ANY)
    out_shapes = [out_shape] + ([jax.ShapeDtypeStruct(out_shape.shape, BF16)] if bf16_copy else [])
    res = pl.pallas_call(
        body, name=name, out_shape=tuple(out_shapes + _rs_out(rs_sends)), grid=grid,
        in_specs=[pl.BlockSpec(a_blk, a_idx), pl.BlockSpec(b_blk, b_idx)] + [any_spec] * n_rs,
        out_specs=tuple([pl.BlockSpec(o_blk, o_idx)] * n_out + [any_spec] * n_rs),
        scratch_shapes=scratch + (_rs_scratch(rs_sends) if n_rs else []), compiler_params=_params(),
    )(a, b, *rs_sends)
    return res if len(res) > 1 else res[0]


EW_TILE = 256
ROW_TILE = 512


def _rms(v):
    return lax.rsqrt(jnp.mean(v * v, axis=-1, keepdims=True) + EPS)


def _rms_bwd(dhat, vh, r):
    return r * (dhat - vh * jnp.mean(dhat * vh, axis=-1, keepdims=True))


def _tok_spec(tm, d):
    return pl.BlockSpec((tm, d), lambda i: (i, 0))


def _vec_spec(d):
    return pl.BlockSpec((1, d), lambda i: (0, 0))


def _mod_spec(tiles_per_seq, d):
    return pl.BlockSpec((None, N_MOD, d), lambda i: (i // tiles_per_seq, 0, 0))


def _seq_acc_spec(tiles_per_seq, d):
    return pl.BlockSpec((None, 1, d), lambda i: (i // tiles_per_seq, 0, 0))


def _acc(ref, val, first):
    @pl.when(first)
    def _():
        ref[...] = val

    @pl.when(jnp.logical_not(first))
    def _():
        ref[...] += val


def _colsum(v):
    return jnp.sum(v, axis=0, keepdims=True)


def _pre_mix(x2, g_pre, mod, seq):
    t, d = x2.shape
    tm = EW_TILE

    def body(x_ref, g_ref, mod_ref, h_ref):
        xv = x_ref[...]
        n = xv * _rms(xv) * g_ref[...]
        h_ref[...] = (n * (1.0 + mod_ref[1:2, :]) + mod_ref[0:1, :]).astype(BF16)

    return pl.pallas_call(
        body, name="pre_mix", out_shape=jax.ShapeDtypeStruct((t, d), BF16), grid=(t // tm,),
        in_specs=[_tok_spec(tm, d), _vec_spec(d), _mod_spec(seq // tm, d)],
        out_specs=_tok_spec(tm, d), compiler_params=_params(),
    )(x2, g_pre, mod)


def _matmul_rows(a, b, tm, tk, seq, name, epilogue, ep_in, ep_in_kinds, ep_out, ep_out_kinds, rs_sends=()):
    ga = a.shape[0] if a.ndim == 3 else None
    (m, k), n = a.shape[-2:], b.shape[-1]
    g_n = ga or 1
    nk = k // tk
    n_red = g_n * nk
    tps = seq // tm
    grid = (m // tm, g_n, nk)
    n_rs = len(rs_sends)
    rs_shapes = [r.shape for r in rs_sends]
    n_in, n_out = len(ep_in), len(ep_out)

    def spec(kind):
        return {"tok": pl.BlockSpec((tm, n), lambda i, g, kq: (i, 0)),
                "vec": pl.BlockSpec((1, n), lambda i, g, kq: (0, 0)),
                "mod": pl.BlockSpec((None, N_MOD, n), lambda i, g, kq: (i // tps, 0, 0)),
                "seq": pl.BlockSpec((None, 1, n), lambda i, g, kq: (i // tps, 0, 0)),
                "loss": pl.BlockSpec((1, LANES), lambda i, g, kq: (0, 0))}[kind]

    def body(a_ref, b_ref, *rest):
        in_refs, rest = rest[:n_in], rest[n_in:]
        rs_src, rest = rest[:n_rs], rest[n_rs:]
        out_refs, rest = rest[:n_out], rest[n_out:]
        rs_dst, rest = rest[:n_rs], rest[n_rs:]
        acc = rest[0]
        i, kk = pl.program_id(0), pl.program_id(1) * nk + pl.program_id(2)
        if n_rs:
            rs_start, rs_finish = _rs_phases(rs_shapes, rs_src, rs_dst, *rest[1:])
            pl.when(jnp.logical_and(i == 0, kk == 0))(rs_start)
        p = _dot_nn(a_ref[...], b_ref[...])
        if n_red == 1:
            epilogue(p, i, tps, in_refs, out_refs)
        else:
            @pl.when(kk == 0)
            def _():
                acc[...] = p

            @pl.when(jnp.logical_and(kk > 0, kk < n_red - 1))
            def _():
                acc[...] += p

            @pl.when(kk == n_red - 1)
            def _():
                epilogue(acc[...] + p, i, tps, in_refs, out_refs)

        if n_rs:
            pl.when(jnp.logical_and(i == grid[0] - 1, kk == n_red - 1))(rs_finish)

    a_blk = (tm, tk) if ga is None else (None, tm, tk)
    b_blk = (tk, n) if ga is None else (None, tk, n)
    a_idx = (lambda i, g, kq: (i, kq)) if ga is None else (lambda i, g, kq: (g, i, kq))
    b_idx = (lambda i, g, kq: (kq, 0)) if ga is None else (lambda i, g, kq: (g, kq, 0))
    any_spec = pl.BlockSpec(memory_space=pl.ANY)
    res = pl.pallas_call(
        body, name=name, grid=grid, out_shape=tuple(list(ep_out) + _rs_out(rs_sends)),
        in_specs=[pl.BlockSpec(a_blk, a_idx), pl.BlockSpec(b_blk, b_idx)] + [spec(kd) for kd in ep_in_kinds]
        + [any_spec] * n_rs,
        out_specs=tuple([spec(kd) for kd in ep_out_kinds] + [any_spec] * n_rs),
        scratch_shapes=[pltpu.VMEM((tm, n), F32)] + (_rs_scratch(rs_sends) if n_rs else []),
        compiler_params=_params(),
    )(a, b, *ep_in, *rs_sends)
    return res


def _mid_epilogue(mv, i, tps, in_refs, out_refs):
    x_ref, gpost_ref, gpre_ref, mod_ref = in_refs
    mix_ref, x1_ref, h2_ref = out_refs
    mix_ref[...] = mv
    x1 = x_ref[...] + mod_ref[2:3, :] * (mv * _rms(mv) * gpost_ref[...])
    x1_ref[...] = x1
    n = x1 * _rms(x1) * gpre_ref[...]
    h2_ref[...] = (n * (1.0 + mod_ref[4:5, :]) + mod_ref[3:4, :]).astype(BF16)


def _post_epilogue(fv, i, tps, in_refs, out_refs):
    x1_ref, tgt_ref, g_ref, mod_ref = in_refs
    loss_ref, dy_ref, df_ref, dgate_ref, gg_ref = out_refs
    d = fv.shape[1]
    r = _rms(fv)
    fh = fv * r
    nf = fh * g_ref[...]
    gate = mod_ref[5:6, :]
    err = x1_ref[...] + gate * nf - tgt_ref[...]
    _acc(loss_ref, jnp.sum(_colsum(err * err), axis=1, keepdims=True) * jnp.ones((1, LANES), F32), i == 0)
    dy = err * (1.0 / d)
    dy_ref[...] = dy
    _acc(dgate_ref, _colsum(dy * nf), i % tps == 0)
    dn = dy * gate
    _acc(gg_ref, _colsum(dn * fh), i == 0)
    df_ref[...] = _rms_bwd(dn * g_ref[...], fh, r).astype(BF16)


def _bwd_mid_epilogue(dh, i, tps, in_refs, out_refs):
    dy_ref, x1_ref, mix_ref, gpre_ref, gpost_ref, mod_ref = in_refs
    dx1_ref, dmix_ref, dshift_ref, dscale_ref, dgate_ref, ggpre_ref, ggpost_ref = out_refs
    seq_first = i % tps == 0
    x1 = x1_ref[...]
    r = _rms(x1)
    xh = x1 * r
    gpre = gpre_ref[...]
    _acc(dshift_ref, _colsum(dh), seq_first)
    _acc(dscale_ref, _colsum(dh * xh * gpre), seq_first)
    dn = dh * (1.0 + mod_ref[4:5, :])
    _acc(ggpre_ref, _colsum(dn * xh), i == 0)
    dx1 = dy_ref[...] + _rms_bwd(dn * gpre, xh, r)
    dx1_ref[...] = dx1
    mv = mix_ref[...]
    rm = _rms(mv)
    mh = mv * rm
    gpost = gpost_ref[...]
    _acc(dgate_ref, _colsum(dx1 * mh * gpost), seq_first)
    dnm = dx1 * mod_ref[2:3, :]
    _acc(ggpost_ref, _colsum(dnm * mh), i == 0)
    dmix_ref[...] = _rms_bwd(dnm * gpost, mh, rm).astype(BF16)


def _bwd_pre_epilogue(dh, i, tps, in_refs, out_refs):
    dx1_ref, x_ref, g_ref, mod_ref = in_refs
    gx_ref, dshift_ref, dscale_ref, gg_ref = out_refs
    seq_first = i % tps == 0
    xv = x_ref[...]
    r = _rms(xv)
    xh = xv * r
    g = g_ref[...]
    _acc(dshift_ref, _colsum(dh), seq_first)
    _acc(dscale_ref, _colsum(dh * xh * g), seq_first)
    dn = dh * (1.0 + mod_ref[1:2, :])
    _acc(gg_ref, _colsum(dn * xh), i == 0)
    gx_ref[...] = dx1_ref[...] + _rms_bwd(dn * g, xh, r)


def _ffn_up(h2, wgu, tm, tn):
    t, d = h2.shape
    f = wgu.shape[1]

    def body(h_ref, w_ref, gu_ref, act_ref):
        h = h_ref[...]
        g = _dot_nt(h, w_ref[0])
        u = _dot_nt(h, w_ref[1])
        gu_ref[0] = g.astype(BF16)
        gu_ref[1] = u.astype(BF16)
        act_ref[...] = (g * jax.nn.sigmoid(g) * u).astype(BF16)

    return pl.pallas_call(
        body, name="ffn_up", grid=(t // tm, f // tn),
        out_shape=(jax.ShapeDtypeStruct((2, t, f), BF16), jax.ShapeDtypeStruct((t, f), BF16)),
        in_specs=[pl.BlockSpec((tm, d), lambda i, j: (i, 0)), pl.BlockSpec((2, tn, d), lambda i, j: (0, j, 0))],
        out_specs=(pl.BlockSpec((2, tm, tn), lambda i, j: (0, i, j)), pl.BlockSpec((tm, tn), lambda i, j: (i, j))),
        compiler_params=_params(),
    )(h2, wgu)


def _ffn_act_bwd(df, wd, gu, tm, tn):
    t, d = df.shape
    f = wd.shape[0]

    def body(df_ref, w_ref, gu_ref, dgu_ref):
        da = _dot_nt(df_ref[...], w_ref[...])
        g = gu_ref[0].astype(F32)
        u = gu_ref[1].astype(F32)
        s = jax.nn.sigmoid(g)
        silu = g * s
        dgu_ref[0] = (da * u * (s + silu * (1.0 - s))).astype(BF16)
        dgu_ref[1] = (da * silu).astype(BF16)

    return pl.pallas_call(
        body, name="ffn_act_bwd", grid=(t // tm, f // tn),
        out_shape=jax.ShapeDtypeStruct((2, t, f), BF16),
        in_specs=[pl.BlockSpec((tm, d), lambda i, j: (i, 0)), pl.BlockSpec((tn, d), lambda i, j: (j, 0)),
                  pl.BlockSpec((2, tm, tn), lambda i, j: (0, i, j))],
        out_specs=pl.BlockSpec((2, tm, tn), lambda i, j: (0, i, j)),
        compiler_params=_params(),
    )(df, wd, gu)


SIGN_BIT = 0x80000000
Q_SCALE = 1.0 / math.sqrt(HEAD_DIM)


def _split_dot(v, tri2):
    hi = v.astype(BF16)
    lo = (v - hi.astype(F32)).astype(BF16)
    return _dot_nn(jnp.concatenate([hi, lo], axis=1), tri2)


def _sb_tile(qs, k2, mask, ntri2, cur):
    z = _dot_nt(qs, k2)
    neg_abs = lax.bitcast_convert_type(lax.bitcast_convert_type(z, jnp.uint32) | jnp.uint32(SIGN_BIT), F32)
    sp = jnp.maximum(z, 0.0) + jnp.log(1.0 + jnp.exp(neg_abs))
    if mask is not None:
        sp = jnp.where(mask, sp, 0.0)
    w = jnp.exp(z + _split_dot(sp, ntri2) + cur)
    if mask is not None:
        w = jnp.where(mask, w, 0.0)
    return z, sp, w


def _stack_heads(v, lane, scale=None):
    if scale is not None:
        v = v * jnp.asarray(scale, v.dtype)
    zero = jnp.zeros_like(v)
    return jnp.concatenate([jnp.where(lane < HEAD_DIM, v, zero), jnp.where(lane >= HEAD_DIM, v, zero)], axis=0)


def _diag_mask(tq):
    row = lax.broadcasted_iota(jnp.int32, (2 * tq, tq), 0)
    col = lax.broadcasted_iota(jnp.int32, (2 * tq, tq), 1)
    return col < jnp.where(row >= tq, row - tq, row)


def _attn_fwd(proj, tri_after, n_seq, seq, ag_srcs, ag_out_shapes, ag_dests):
    t = proj.shape[0]
    tq = ATT_TILE
    npp = ATT_PAIRS
    n_blk = (proj.shape[1] // 4) // (npp * LANES)
    n_ag, n_ag_out = len(ag_srcs), len(ag_out_shapes)
    n_steps = n_seq * n_blk

    def body(q_ref, k_ref, v_ref, tri_ref, *rest):
        ag_src, rest = rest[:n_ag], rest[n_ag:]
        o_ref, cs_ref = rest[:2]
        ag_out, rest = rest[2:2 + n_ag_out], rest[2 + n_ag_out:]
        oacc, cmat, carry = rest[:3]
        ag_start, ag_forward, ag_finish = _ag_phases(ag_dests, ag_src, ag_out, *rest[3:])
        step = pl.program_id(0) * n_blk + pl.program_id(1)
        pl.when(step == 0)(ag_start)
        pl.when(step == (3 * n_steps) // 4)(ag_forward)
        lane = lax.broadcasted_iota(jnp.int32, (1, LANES), 1)
        ntri2 = tri_ref[...]
        diag = _diag_mask(tq)

        def q_tile(qi, _):
            r0 = pl.multiple_of(qi * tq, tq)
            qs = [_stack_heads(q_ref[pl.ds(r0, tq), pp * LANES:(pp + 1) * LANES], lane, Q_SCALE)
                  for pp in range(npp)]
            carry[...] = jnp.zeros_like(carry)
            cmat[...] = jnp.zeros_like(cmat)
            oacc[...] = jnp.zeros_like(oacc)

            def run_tiles(tiles):
                for pp in range(npp):
                    cols = slice(pp * LANES, (pp + 1) * LANES)
                    cur = carry[pp]
                    cm = cmat[pp]
                    pv = None
                    for kb, mask in tiles:
                        c0 = pl.multiple_of(kb * tq, tq)
                        _, sp, w = _sb_tile(qs[pp], k_ref[pl.ds(c0, tq), cols], mask, ntri2, cur)
                        p = _dot_nn(w.astype(BF16), v_ref[pl.ds(c0, tq), cols])
                        pv = p if pv is None else pv + p
                        cm = jnp.where(lane == kb, cur, cm)
                        cur = cur - jnp.sum(sp, axis=1, keepdims=True)
                    oacc[pp] += pv
                    cmat[pp] = cm
                    carry[pp] = cur

            odd = qi % 2

            @pl.when(odd == 0)
            def _():
                run_tiles([(qi, diag)])

            @pl.when(odd == 1)
            def _():
                run_tiles([(qi, diag), (qi - 1, None)])

            def pair(j, _):
                kb = qi - 1 - odd - 2 * j
                run_tiles([(kb, None), (kb - 1, None)])
                return 0

            lax.fori_loop(0, qi // 2, pair, 0)
            for pp in range(npp):
                c_off = 2 * pp * LANES
                cs_ref[pl.ds(r0, tq), c_off:c_off + LANES] = cmat[pp, 0:tq, :]
                cs_ref[pl.ds(r0, tq), c_off + LANES:c_off + 2 * LANES] = cmat[pp, tq:2 * tq, :]
                o_ref[pl.ds(r0, tq), pp * LANES:(pp + 1) * LANES] = jnp.where(
                    lane < HEAD_DIM, oacc[pp, 0:tq, :], oacc[pp, tq:2 * tq, :]).astype(BF16)
            return 0

        lax.fori_loop(0, seq // tq, q_tile, 0)
        pl.when(step == n_steps - 1)(ag_finish)

    wid = npp * LANES
    blk = lambda off: pl.BlockSpec((seq, wid), lambda b, p: (b, off + p))
    any_spec = pl.BlockSpec(memory_space=pl.ANY)
    return pl.pallas_call(
        body, name="attn_fwd", grid=(n_seq, n_blk),
        out_shape=(jax.ShapeDtypeStruct((t, n_blk * wid), BF16),
                   jax.ShapeDtypeStruct((t, n_blk * 2 * wid), F32), *ag_out_shapes),
        in_specs=[blk(0), blk(n_blk), blk(2 * n_blk), pl.BlockSpec((2 * tq, tq), lambda b, p: (0, 0))]
        + [any_spec] * n_ag,
        out_specs=(pl.BlockSpec((seq, wid), lambda b, p: (b, p)),
                   pl.BlockSpec((seq, 2 * wid), lambda b, p: (b, p)), *([any_spec] * n_ag_out)),
        scratch_shapes=[pltpu.VMEM((npp, 2 * tq, LANES), F32), pltpu.VMEM((npp, 2 * tq, LANES), F32),
                        pltpu.VMEM((npp, 2 * tq, 1), F32)] + _ag_scratch(n_ag),
        compiler_params=_params(),
    )(proj, proj, proj, tri_after, *ag_srcs)


def _attn_bwd(proj, dcat, cstats, tri_after, tri_incl, n_seq, seq, rs_sends):
    t = proj.shape[0]
    tq = ATT_TILE
    npp = ATT_PAIRS
    width = proj.shape[1] // 4
    n_blk = width // (npp * LANES)
    n_rs = len(rs_sends)
    rs_shapes = [r.shape for r in rs_sends]
    n_steps = n_seq * n_blk

    def body(q_ref, k_ref, v_ref, do_ref, cs_ref, tria_ref, trii_ref, *rest):
        rs_src, rest = rest[:n_rs], rest[n_rs:]
        out_ref = rest[0]
        rs_dst, rest = rest[1:1 + n_rs], rest[1 + n_rs:]
        dq_acc, dk_acc, dv_acc, ecarry = rest[:4]
        rs_start, rs_finish = _rs_phases(rs_shapes, rs_src, rs_dst, *rest[4:])
        step = pl.program_id(0) * n_blk + pl.program_id(1)
        pl.when(step == 0)(rs_start)
        lane = lax.broadcasted_iota(jnp.int32, (1, LANES), 1)
        ntri2 = tria_ref[...]
        tri_i2 = trii_ref[...]
        diag = _diag_mask(tq)
        dk_acc[...] = jnp.zeros_like(dk_acc)
        dv_acc[...] = jnp.zeros_like(dv_acc)

        def q_tile(qi, _):
            r0 = pl.multiple_of(qi * tq, tq)
            qs, dos, cs = [], [], []
            for pp in range(npp):
                cols = slice(pp * LANES, (pp + 1) * LANES)
                qs.append(_stack_heads(q_ref[pl.ds(r0, tq), cols], lane, Q_SCALE))
                dos.append(_stack_heads(do_ref[pl.ds(r0, tq), cols], lane))
                c_off = 2 * pp * LANES
                cs.append(jnp.concatenate([cs_ref[pl.ds(r0, tq), c_off:c_off + LANES],
                                           cs_ref[pl.ds(r0, tq), c_off + LANES:c_off + 2 * LANES]], axis=0))
            ecarry[...] = jnp.zeros_like(ecarry)
            dq_acc[...] = jnp.zeros_like(dq_acc)

            def run_tiles(tiles):
                for pp in range(npp):
                    cols = slice(pp * LANES, (pp + 1) * LANES)
                    ec = ecarry[pp]
                    dq = None
                    for kb, mask in tiles:
                        c0 = pl.multiple_of(kb * tq, tq)
                        k2 = k_ref[pl.ds(c0, tq), cols]
                        v2 = v_ref[pl.ds(c0, tq), cols]
                        cur = jnp.sum(jnp.where(lane == kb, cs[pp], 0.0), axis=1, keepdims=True)
                        z, sp, w = _sb_tile(qs[pp], k2, mask, ntri2, cur)
                        ee = w * _dot_nt(dos[pp], v2)
                        einc = _split_dot(ee, tri_i2) + ec
                        dz = ee - jnp.exp(z - sp) * einc
                        if mask is not None:
                            dz = jnp.where(mask, dz, 0.0)
                        dzb = dz.astype(BF16)
                        p = _dot_nn(dzb, k2)
                        dq = p if dq is None else dq + p
                        dk_acc[pp, pl.ds(c0, tq), :] += _dot_tn(dzb, qs[pp])
                        dv_acc[pp, pl.ds(c0, tq), :] += _dot_tn(w.astype(BF16), dos[pp])
                        ec = ec + jnp.sum(ee, axis=1, keepdims=True)
                    dq_acc[pp] += dq
                    ecarry[pp] = ec

            def pair(j, _):
                run_tiles([(2 * j, None), (2 * j + 1, None)])
                return 0

            lax.fori_loop(0, qi // 2, pair, 0)
            odd = qi % 2

            @pl.when(odd == 0)
            def _():
                run_tiles([(qi, diag)])

            @pl.when(odd == 1)
            def _():
                run_tiles([(qi - 1, None), (qi, diag)])

            for pp in range(npp):
                dq = jnp.where(lane < HEAD_DIM, dq_acc[pp, 0:tq, :], dq_acc[pp, tq:2 * tq, :])
                out_ref[0, pl.ds(r0, tq), pp * LANES:(pp + 1) * LANES] = (dq * Q_SCALE).astype(BF16)
            return 0

        lax.fori_loop(0, seq // tq, q_tile, 0)
        for pp in range(npp):
            cols = slice(pp * LANES, (pp + 1) * LANES)
            out_ref[1, :, cols] = dk_acc[pp].astype(BF16)
            out_ref[2, :, cols] = dv_acc[pp].astype(BF16)
        pl.when(step == n_steps - 1)(rs_finish)

    wid = npp * LANES
    blk = lambda off: pl.BlockSpec((seq, wid), lambda b, p: (b, off + p))
    tri_spec = pl.BlockSpec((2 * tq, tq), lambda b, p: (0, 0))
    any_spec = pl.BlockSpec(memory_space=pl.ANY)
    return pl.pallas_call(
        body, name="attn_bwd", grid=(n_seq, n_blk),
        out_shape=(jax.ShapeDtypeStruct((4, t, width), BF16), *_rs_out(rs_sends)),
        in_specs=[blk(0), blk(n_blk), blk(2 * n_blk), pl.BlockSpec((seq, wid), lambda b, p: (b, p)),
                  pl.BlockSpec((seq, 2 * wid), lambda b, p: (b, p)), tri_spec, tri_spec] + [any_spec] * n_rs,
        out_specs=(pl.BlockSpec((3, seq, wid), lambda b, p: (0, b, p)), *([any_spec] * n_rs)),
        scratch_shapes=[pltpu.VMEM((npp, 2 * tq, LANES), F32), pltpu.VMEM((npp, seq, LANES), F32),
                        pltpu.VMEM((npp, seq, LANES), F32), pltpu.VMEM((npp, 2 * tq, 1), F32)]
        + _rs_scratch(rs_sends),
        compiler_params=_params(),
    )(proj, proj, proj, dcat, cstats, tri_after, tri_incl, *rs_sends)


def _window_terms(g, rows):
    win = jnp.where(g == 0, POOL_WINDOWS[0], jnp.where(g == 1, POOL_WINDOWS[1],
                    jnp.where(g == 2, POOL_WINDOWS[2], POOL_WINDOWS[3])))
    cnt = jnp.minimum(rows + 1, win).astype(F32)
    return win, cnt


def _window_sum(v, g, rows, forward):
    s_len = v.shape[0]
    sums = []
    s = v
    for step in range(len(POOL_WINDOWS)):
        sh = 1 << step
        if forward:
            shifted = jnp.where(rows < s_len - sh, pltpu.roll(s, s_len - sh, axis=0), 0.0)
        else:
            shifted = jnp.where(rows >= sh, pltpu.roll(s, sh, axis=0), 0.0)
        s = s + shifted
        sums.append(s)
    return jnp.where(g == 0, sums[0], jnp.where(g == 1, sums[1], jnp.where(g == 2, sums[2], sums[3])))


def _pooled(u, g, rows):
    _, cnt = _window_terms(g, rows)
    return _window_sum(u, g, rows, forward=False) / cnt - u


def _pool_fwd(proj, w_pool, pool_scale, n_seq, seq):
    t = proj.shape[0]
    n_grp = len(POOL_WINDOWS)
    u_off = 3 * (proj.shape[1] // 4) // LANES

    def body(u_ref, w_ref, s_ref, o_ref):
        g = pl.program_id(1)
        rows = lax.broadcasted_iota(jnp.int32, (seq, 1), 0)
        pooled = _pooled(u_ref[...].astype(F32), g, rows)
        y = _dot_nn(pooled.astype(BF16), w_ref[...].astype(BF16))
        o_ref[...] = (y * s_ref[...]).astype(BF16)

    return pl.pallas_call(
        body, name="pool_fwd", grid=(n_seq, n_grp),
        out_shape=jax.ShapeDtypeStruct((t, n_grp * POOL_GROUP_DIM), BF16),
        in_specs=[pl.BlockSpec((seq, LANES), lambda b, g: (b, u_off + g)),
                  pl.BlockSpec((None, POOL_GROUP_DIM, POOL_GROUP_DIM), lambda b, g: (g, 0, 0)),
                  pl.BlockSpec((1, POOL_GROUP_DIM), lambda b, g: (0, g))],
        out_specs=pl.BlockSpec((seq, LANES), lambda b, g: (b, g)),
        compiler_params=_params(),
    )(proj, w_pool, pool_scale)


def _pool_bwd(proj, dcat, w_pool, pool_scale, dqkv, n_seq, seq):
    n_grp = len(POOL_WINDOWS)
    width = proj.shape[1] // 4
    u_off = 3 * width // LANES
    dp_off = width // LANES

    def body(u_ref, dp_ref, w_ref, s_ref, alias_ref, du_ref, gw_ref, gs_ref):
        del alias_ref
        g = pl.program_id(0)
        b = pl.program_id(1)
        rows = lax.broadcasted_iota(jnp.int32, (seq, 1), 0)
        pooled = _pooled(u_ref[...].astype(F32), g, rows)
        pb = pooled.astype(BF16)
        wb = w_ref[...].astype(BF16)
        z = _dot_nn(pb, wb)
        dp = dp_ref[...].astype(F32)
        _acc(gs_ref, _colsum(dp * z), b == 0)
        dys = (dp * s_ref[...]).astype(BF16)
        _acc(gw_ref, _dot_tn(pb, dys), b == 0)
        dpooled = _dot_nt(dys, wb)
        _, cnt = _window_terms(g, rows)
        du = _window_sum(dpooled / cnt, g, rows, forward=True) - dpooled
        du_ref[...] = du.astype(BF16)

    t = proj.shape[0]
    return pl.pallas_call(
        body, name="pool_bwd", grid=(n_grp, n_seq),
        out_shape=(jax.ShapeDtypeStruct(dqkv.shape, BF16),
                   jax.ShapeDtypeStruct((n_grp, POOL_GROUP_DIM, POOL_GROUP_DIM), F32),
                   jax.ShapeDtypeStruct((1, n_grp * POOL_GROUP_DIM), F32)),
        in_specs=[pl.BlockSpec((seq, LANES), lambda g, b: (b, u_off + g)),
                  pl.BlockSpec((seq, LANES), lambda g, b: (b, dp_off + g)),
                  pl.BlockSpec((None, POOL_GROUP_DIM, POOL_GROUP_DIM), lambda g, b: (g, 0, 0)),
                  pl.BlockSpec((1, POOL_GROUP_DIM), lambda g, b: (0, g)),
                  pl.BlockSpec(memory_space=pl.ANY)],
        out_specs=(pl.BlockSpec((None, seq, LANES), lambda g, b: (3, b, g)),
                   pl.BlockSpec((None, POOL_GROUP_DIM, POOL_GROUP_DIM), lambda g, b: (g, 0, 0)),
                   pl.BlockSpec((1, POOL_GROUP_DIM), lambda g, b: (0, g))),
        input_output_aliases={4: 0},
        compiler_params=_params(),
    )(proj, dcat, w_pool, pool_scale, dqkv)


def _cond_fwd(c_all, w_cond, b_cols):
    n, _ = c_all.shape
    cols = w_cond.shape[1]

    def body(c_ref, w_ref, b_ref, o_ref):
        cv = c_ref[...]
        a = cv * jax.nn.sigmoid(cv)
        o_ref[...] = jnp.dot(a, w_ref[...], preferred_element_type=F32,
                             precision=lax.Precision.HIGHEST) + b_ref[...]

    return pl.pallas_call(
        body, name="cond_fwd", out_shape=jax.ShapeDtypeStruct((n, cols), F32),
        compiler_params=_params(),
    )(c_all, w_cond, b_cols)


def _cond_bwd(c_all, dmod_all, dmod_cols):
    n, d = c_all.shape
    cols = dmod_cols.shape[1]

    def body(c_ref, dm_ref, dmc_ref, gw_ref, gb_ref):
        cv = c_ref[...]
        a = cv * jax.nn.sigmoid(cv)
        gw_ref[...] = lax.dot_general(a, dmc_ref[...], (((0,), (0,)), ((), ())),
                                      preferred_element_type=F32, precision=lax.Precision.HIGHEST)
        gb_ref[...] = _colsum(dm_ref[...])

    return pl.pallas_call(
        body, name="cond_bwd",
        out_shape=(jax.ShapeDtypeStruct((d, cols), F32), jax.ShapeDtypeStruct((1, dmod_all.shape[1]), F32)),
        compiler_params=_params(),
    )(c_all, dmod_all, dmod_cols)


def _adamw_math(w, g, m, v):
    m = ADAM_B1 * m + (1.0 - ADAM_B1) * g
    v = ADAM_B2 * v + (1.0 - ADAM_B2) * (g * g)
    m_hat = m / (1.0 - ADAM_B1 ** ADAM_STEP)
    v_hat = v / (1.0 - ADAM_B2 ** ADAM_STEP)
    delta = -ADAM_LR * (m_hat / (jnp.sqrt(v_hat) + ADAM_EPS) + ADAM_WD * w)
    return delta, m, v


def _adamw(w, g, m, v, rows, name):
    r, cdim = w.shape

    def body(w_ref, g_ref, m_ref, v_ref, d_ref, nm_ref, nv_ref):
        d_ref[...], nm_ref[...], nv_ref[...] = _adamw_math(w_ref[...], g_ref[...], m_ref[...], v_ref[...])

    spec = pl.BlockSpec((rows, cdim), lambda i: (i, 0))
    sds = jax.ShapeDtypeStruct((r, cdim), F32)
    return pl.pallas_call(
        body, name=name, grid=(r // rows,), out_shape=(sds, sds, sds),
        in_specs=[spec] * 4, out_specs=(spec, spec, spec), compiler_params=_params(),
    )(w, g, m, v)


def _adamw_small(ws, gparts, ms, vs, name):
    n = len(ws)

    def body(*refs):
        w_r, g_r, m_r, v_r = refs[:n], refs[n:2 * n], refs[2 * n:3 * n], refs[3 * n:4 * n]
        outs = refs[4 * n:]
        for i in range(n):
            g = g_r[i][0]
            for dev in range(1, g_r[i].shape[0]):
                g = g + g_r[i][dev]
            delta, m, v = _adamw_math(w_r[i][...], g, m_r[i][...], v_r[i][...])
            outs[i][...] = g
            outs[n + i][...] = delta
            outs[2 * n + i][...] = m
            outs[3 * n + i][...] = v

    sds = [jax.ShapeDtypeStruct(w.shape, F32) for w in ws]
    return pl.pallas_call(
        body, name=name, out_shape=tuple(sds * 4), compiler_params=_params(),
    )(*ws, *gparts, *ms, *vs)


def kernel(x, c, w_cond, b_cond, g_mix_pre, g_mix_post, w_in, w_pool, pool_scale, w_out, g_ffn_pre, g_ffn_post, w_gate, w_up, w_down, loss_target, m_w_cond, m_b_cond, m_g_mix_pre, m_g_mix_post, m_w_in, m_w_pool, m_pool_scale, m_w_out, m_g_ffn_pre, m_g_ffn_post, m_w_gate, m_w_up, m_w_down, v_w_cond, v_b_cond, v_g_mix_pre, v_g_mix_post, v_w_in, v_w_pool, v_pool_scale, v_w_out, v_g_ffn_pre, v_g_ffn_post, v_w_gate, v_w_up, v_w_down):
    n_seq, seq, d = x.shape
    t = n_seq * seq
    xi, yi, ci = _mesh_pos()
    me = 4 * xi + 2 * yi + ci
    x2 = x.reshape(t, d)
    tgt2 = loss_target.reshape(t, d)
    in_rows = w_in.shape[2]
    out_rows = w_out.shape[1]
    ff_rows = w_gate.shape[2]
    ff = N_DEV * ff_rows
    cond_cols = w_cond.shape[2]

    win_t = w_in[0].T.astype(BF16)
    wout_s = w_out[0].astype(BF16)
    wg_t = w_gate[0].T.astype(BF16)
    wu_t = w_up[0].T.astype(BF16)
    wd_s = w_down[0].astype(BF16)
    c_all, win_g = _all_gather(
        [c, win_t],
        [jax.ShapeDtypeStruct((N_DEV, n_seq, d), F32), jax.ShapeDtypeStruct((N_DEV, in_rows, d), BF16)],
        [(0, ()), (1, ())], "ag_c_win")
    c_all = c_all.reshape(N_DEV * n_seq, d)
    win_full = win_g.reshape(N_DEV * in_rows, d)

    b_cols = lax.dynamic_slice_in_dim(b_cond, me * cond_cols, cond_cols, axis=1)
    mod_cols = _cond_fwd(c_all, w_cond[0], b_cols)
    (mod_g,) = _all_gather([mod_cols], [jax.ShapeDtypeStruct((N_DEV,) + mod_cols.shape, F32)], [(0, ())], "ag_mod")
    mod_mine = lax.dynamic_slice_in_dim(mod_g, me * n_seq, n_seq, axis=1)
    mod = jnp.transpose(mod_mine, (1, 0, 2)).reshape(n_seq, N_MOD, d)

    h1 = _pre_mix(x2, g_mix_pre, mod, seq)
    proj = _matmul(h1, win_full, "nt", BF16, 1024, 512, d, "proj")
    tq = ATT_TILE
    ids = jnp.arange(tq)
    tri_after = jnp.tile(-(ids[:, None] >= ids[None, :]).astype(BF16), (2, 1))
    tri_incl = jnp.tile((ids[:, None] <= ids[None, :]).astype(BF16), (2, 1))
    attn, cstats, wout_g, wgu_g, wd_g = _attn_fwd(
        proj, tri_after, n_seq, seq, [wout_s, wg_t, wu_t, wd_s],
        [jax.ShapeDtypeStruct((N_DEV, out_rows, d), BF16), jax.ShapeDtypeStruct((2, N_DEV, ff_rows, d), BF16),
         jax.ShapeDtypeStruct((N_DEV, ff_rows, d), BF16)],
        [(0, ()), (1, (0,)), (1, (1,)), (2, ())])
    wout_full = wout_g.reshape(N_DEV * out_rows, d)
    wgu_full = wgu_g.reshape(2, ff, d)
    wd_full = wd_g.reshape(ff, d)
    pool = _pool_fwd(proj, w_pool[0], pool_scale, n_seq, seq)
    cat = jnp.stack([attn, pool])
    tok_f32, tok_bf16 = jax.ShapeDtypeStruct((t, d), F32), jax.ShapeDtypeStruct((t, d), BF16)
    seq_sds, vec_sds = jax.ShapeDtypeStruct((n_seq, 1, d), F32), jax.ShapeDtypeStruct((1, d), F32)
    mix, x1, h2 = _matmul_rows(
        cat, wout_full.reshape(2, d // 2, d), ROW_TILE, d // 2, seq, "mix_mid", _mid_epilogue,
        [x2, g_mix_post, g_ffn_pre, mod], ["tok", "vec", "vec", "mod"],
        [tok_f32, tok_f32, tok_bf16], ["tok", "tok", "tok"])
    gu, act = _ffn_up(h2, wgu_full, 512, ff // 2)
    loss_sum, dy, df, dgate_f, gg_ffn_post = _matmul_rows(
        act, wd_full, ROW_TILE, ff, seq, "ffn_down_post", _post_epilogue,
        [x1, tgt2, g_ffn_post, mod], ["tok", "tok", "vec", "mod"],
        [jax.ShapeDtypeStruct((1, LANES), F32), tok_f32, tok_bf16, seq_sds, vec_sds],
        ["loss", "tok", "tok", "seq", "vec"])

    dgu = _ffn_act_bwd(df, wd_full, gu, 512, ff // 2)
    gwd, gwd_b = _matmul(act, df, "tn", F32, ff // 2, d, 1024, "grad_w_down", bf16_copy=True)
    gwgu, gwgu_b = _matmul(dgu, h2, "tn", F32, ff // 2, d, 1024, "grad_w_gate_up", bf16_copy=True)
    dx1, dmix, dshift_f, dscale_f, dgate_m, gg_ffn_pre, gg_mix_post = _matmul_rows(
        dgu, wgu_full, ROW_TILE, ff, seq, "dh2_bwd_mid", _bwd_mid_epilogue,
        [dy, x1, mix, g_ffn_pre, g_mix_post, mod], ["tok", "tok", "tok", "vec", "vec", "mod"],
        [tok_f32, tok_bf16, seq_sds, seq_sds, seq_sds, vec_sds, vec_sds],
        ["tok", "tok", "seq", "seq", "seq", "vec", "vec"])
    dcat = _matmul(dmix, wout_full, "nt", BF16, 1024, 512, d, "dcat")
    gwout, gwout_b = _matmul(cat, dmix, "tn", F32, d // 2, d, 1024, "grad_w_out", bf16_copy=True)
    dqkv, rv_wgu, rv_wd, rv_wout = _attn_bwd(
        proj, dcat, cstats, tri_after, tri_incl, n_seq, seq,
        [gwgu_b.reshape(2, N_DEV, ff_rows, d), gwd_b.reshape(1, N_DEV, ff_rows, d),
         gwout_b.reshape(1, N_DEV, out_rows, d)])
    dproj, gw_pool, gs_pool = _pool_bwd(proj, dcat, w_pool[0], pool_scale, dqkv, n_seq, seq)
    gwin, gwin_b = _matmul(dproj, h1, "tn", F32, d // 2, d, 1024, "grad_w_in", bf16_copy=True)
    grad_x, dshift_m, dscale_m, gg_mix_pre, rv_win = _matmul_rows(
        dproj, win_full.reshape(4, d // 2, d), ROW_TILE, d // 2, seq, "dh1_bwd_pre", _bwd_pre_epilogue,
        [dx1, x2, g_mix_pre, mod], ["tok", "tok", "vec", "mod"],
        [tok_f32, seq_sds, seq_sds, vec_sds], ["tok", "seq", "seq", "vec"],
        rs_sends=[gwin_b.reshape(1, N_DEV, in_rows, d)])

    r_wgu = _rs_final(gwgu.reshape(2, N_DEV, ff_rows, d), rv_wgu, "rs_final_gate_up")
    r_wd = _rs_final(gwd.reshape(1, N_DEV, ff_rows, d), rv_wd, "rs_final_down")
    r_wout = _rs_final(gwout.reshape(1, N_DEV, out_rows, d), rv_wout, "rs_final_out")
    r_win = _rs_final(gwin.reshape(1, N_DEV, in_rows, d), rv_win, "rs_final_in")
    grad_w_in = r_win[0].T
    grad_w_out = r_wout[0]
    grad_w_gate = r_wgu[0].T
    grad_w_up = r_wgu[1].T
    grad_w_down = r_wd[0]

    dmod = jnp.concatenate([dshift_m, dscale_m, dgate_m, dshift_f, dscale_f, dgate_f], axis=1)
    small = jnp.concatenate([gg_mix_pre, gg_mix_post, gg_ffn_pre, gg_ffn_post,
                             jnp.pad(gs_pool, ((0, 0), (0, d - gs_pool.shape[1]))),
                             jnp.pad(loss_sum, ((0, 0), (0, d - loss_sum.shape[1]))), jnp.zeros((2, d), F32),
                             gw_pool.reshape(-1, d), dmod.reshape(n_seq * N_MOD, d)], axis=0)
    n_gw = gw_pool.size // d
    (small_g,) = _all_gather([small], [jax.ShapeDtypeStruct((N_DEV,) + small.shape, F32)], [(0, ())], "ag_small")
    loss = jnp.sum(small_g[:, 5, 0]) * (0.5 / d)
    dmod_all = small_g[:, 8 + n_gw:, :].reshape(N_DEV * n_seq, N_MOD * d)
    dmod_cols = lax.dynamic_slice_in_dim(dmod_all, me * cond_cols, cond_cols, axis=1)
    grad_w_cond, grad_b_cond = _cond_bwd(c_all, dmod_all, dmod_cols)

    small_ws = [g_mix_pre, g_mix_post, g_ffn_pre, g_ffn_post, pool_scale, w_pool.reshape(-1, POOL_GROUP_DIM)]
    small_ms = [m_g_mix_pre, m_g_mix_post, m_g_ffn_pre, m_g_ffn_post, m_pool_scale, m_w_pool.reshape(-1, POOL_GROUP_DIM)]
    small_vs = [v_g_mix_pre, v_g_mix_post, v_g_ffn_pre, v_g_ffn_post, v_pool_scale, v_w_pool.reshape(-1, POOL_GROUP_DIM)]
    small_gparts = [small_g[:, 0:1, :], small_g[:, 1:2, :], small_g[:, 2:3, :], small_g[:, 3:4, :],
                    small_g[:, 4:5, :pool_scale.shape[1]],
                    small_g[:, 8:8 + n_gw, :].reshape(N_DEV, -1, POOL_GROUP_DIM)]
    so = _adamw_small(small_ws, small_gparts, small_ms, small_vs, "adamw_small")
    ns = len(small_ws)
    sg, sdl, sm, sv = so[:ns], so[ns:2 * ns], so[2 * ns:3 * ns], so[3 * ns:]
    pool_shape = w_pool.shape
    fix = lambda lst: [lst[0], lst[1], lst[2], lst[3], lst[4], lst[5].reshape(pool_shape)]
    sg, sdl, sm, sv = fix(sg), fix(sdl), fix(sm), fix(sv)

    def big(w, g, m, v, rows, name):
        dl, nm, nv = _adamw(w[0], g, m[0], v[0], rows, name)
        return g[None], dl[None], nm[None], nv[None]

    o_cond = big(w_cond, grad_w_cond, m_w_cond, v_w_cond, 256, "adamw_w_cond")
    o_bcond = _adamw(b_cond, grad_b_cond, m_b_cond, v_b_cond, 1, "adamw_b_cond")
    o_bcond = (grad_b_cond,) + tuple(o_bcond)
    o_in = big(w_in, grad_w_in, m_w_in, v_w_in, 256, "adamw_w_in")
    o_out = big(w_out, grad_w_out, m_w_out, v_w_out, out_rows, "adamw_w_out")
    o_gate = big(w_gate, grad_w_gate, m_w_gate, v_w_gate, 256, "adamw_w_gate")
    o_up = big(w_up, grad_w_up, m_w_up, v_w_up, 256, "adamw_w_up")
    o_down = big(w_down, grad_w_down, m_w_down, v_w_down, ff_rows, "adamw_w_down")

    def pick(k):
        small_k = [sg, sdl, sm, sv][k]
        return [o_cond[k], o_bcond[k], small_k[0], small_k[1], o_in[k], small_k[5], small_k[4], o_out[k],
                small_k[2], small_k[3], o_gate[k], o_up[k], o_down[k]]

    return (loss, grad_x.reshape(n_seq, seq, d), *pick(0), *pick(1), *pick(2), *pick(3))
```

```python
import functools
import math

import jax
import jax.numpy as jnp
from jax import lax
from jax.experimental import pallas as pl
from jax.experimental.pallas import tpu as pltpu

F32 = jnp.float32
BF16 = jnp.bfloat16
MESH = pl.DeviceIdType.MESH

N_DEV = 8
HEAD_DIM = 64
LANES = 128
POOL_WINDOWS = (2, 4, 8, 16)
POOL_GROUP_DIM = 128
N_MOD = 6
EPS = 1e-6
ATT_TILE = 256
ATT_PAIRS = 2
VMEM_LIMIT = 56 * 1024 * 1024

ADAM_LR = 0.001
ADAM_B1 = 0.9
ADAM_B2 = 0.999
ADAM_EPS = 1e-08
ADAM_WD = 0.01
ADAM_STEP = 10


def _params(**kw):
    return pltpu.CompilerParams(vmem_limit_bytes=VMEM_LIMIT, **kw)


def _dot_nn(a, b):
    return jnp.dot(a, b, preferred_element_type=F32)


def _dot_nt(a, b):
    return lax.dot_general(a, b, (((1,), (1,)), ((), ())), preferred_element_type=F32)


def _dot_tn(a, b):
    return lax.dot_general(a, b, (((0,), (0,)), ((), ())), preferred_element_type=F32)


def _mesh_pos():
    return lax.axis_index("x"), lax.axis_index("y"), lax.axis_index("c")


def _ag_phases(dests, src, outs, send_sems, recv_sems, local_sems):
    n = len(src)
    x, y, c = _mesh_pos()
    me, sibling = (x, y, c), (x, y, 1 - c)
    chips = [(1 - x, y), (x, 1 - y), (1 - x, 1 - y)]

    def slot(i, dev):
        oi, prefix = dests[i]
        px, py, pc = dev
        return outs[oi].at[prefix + (4 * px + 2 * py + pc,)]

    def copy(i, k, block, to, from_src=False):
        return pltpu.make_async_remote_copy(
            src_ref=src[i] if from_src else slot(i, block), dst_ref=slot(i, block),
            send_sem=send_sems.at[i, k], recv_sem=recv_sems.at[i, k],
            device_id=to, device_id_type=MESH)

    def mine(i):
        return pltpu.make_async_copy(src[i], slot(i, me), local_sems.at[i])

    def first(i):
        return [copy(i, 0, me, sibling, from_src=True)] + [
            copy(i, 1 + j, me, (*chip, c), from_src=True) for j, chip in enumerate(chips)]

    def passed(i, j):
        return copy(i, 4 + j, (*chips[j], c), sibling)

    def start():
        for i in range(n):
            mine(i).start()
        for i in range(n):
            for cp in first(i):
                cp.start()

    def forward():
        for j, chip in enumerate(chips):
            for i in range(n):
                copy(i, 1 + j, (*chip, c), me).wait_recv()
                passed(i, j).start()

    def finish():
        for i in range(n):
            copy(i, 0, sibling, me).wait_recv()
            for j, chip in enumerate(chips):
                copy(i, 4 + j, (*chip, 1 - c), me).wait_recv()
        for i in range(n):
            for cp in first(i) + [passed(i, j) for j in range(3)]:
                cp.wait_send()
            mine(i).wait()

    return start, forward, finish


def _ag_scratch(n):
    return [pltpu.SemaphoreType.DMA((n, 7)), pltpu.SemaphoreType.DMA((n, 7)), pltpu.SemaphoreType.DMA((n,))]


def _all_gather(srcs, out_shapes, dests, name):
    n = len(srcs)

    def body(*refs):
        src = refs[:n]
        outs = refs[n:n + len(out_shapes)]
        start, forward, finish = _ag_phases(dests, src, outs, *refs[n + len(out_shapes):])
        start()
        forward()
        finish()

    any_spec = pl.BlockSpec(memory_space=pl.ANY)
    return pl.pallas_call(
        body, name=name,
        out_shape=tuple(out_shapes),
        in_specs=[any_spec] * n,
        out_specs=tuple([any_spec] * len(out_shapes)),
        scratch_shapes=_ag_scratch(n),
    )(*srcs)


def _rs_phases(shapes, src, dst, send_sems, recv_sems):
    x, y, c = _mesh_pos()

    def copies():
        out = []
        n = 0
        for i, shp in enumerate(shapes):
            for m in range(shp[0]):
                for k in range(1, N_DEV):
                    px, py, pc = x ^ (k >> 2), y ^ ((k >> 1) & 1), c ^ (k & 1)
                    out.append(pltpu.make_async_remote_copy(
                        src_ref=src[i].at[m, 4 * px + 2 * py + pc], dst_ref=dst[i].at[m, k - 1],
                        send_sem=send_sems.at[n], recv_sem=recv_sems.at[n],
                        device_id=(px, py, pc), device_id_type=MESH))
                    n += 1
        return out

    def start():
        for cp in copies():
            cp.start()

    def finish():
        for cp in copies():
            cp.wait_send()
        for cp in copies():
            cp.wait_recv()

    return start, finish


def _rs_out(sends):
    return [jax.ShapeDtypeStruct((s.shape[0], N_DEV - 1) + s.shape[2:], s.dtype) for s in sends]


def _rs_scratch(sends):
    total = sum((N_DEV - 1) * s.shape[0] for s in sends)
    return [pltpu.SemaphoreType.DMA((total,)), pltpu.SemaphoreType.DMA((total,))]


def _rs_final(mine, recv, name):
    m_n, _, r, cdim = mine.shape
    x, y, c = _mesh_pos()
    me = jnp.reshape(4 * x + 2 * y + c, (1,)).astype(jnp.int32)

    def body(me_ref, p_ref, r_ref, o_ref):
        del me_ref
        s = p_ref[...]
        for k in range(N_DEV - 1):
            s = s + r_ref[k].astype(F32)
        o_ref[...] = s

    return pl.pallas_call(
        body, name=name, out_shape=jax.ShapeDtypeStruct((m_n, r, cdim), F32),
        grid_spec=pltpu.PrefetchScalarGridSpec(
            num_scalar_prefetch=1, grid=(m_n,),
            in_specs=[pl.BlockSpec((None, None, r, cdim), lambda m, s: (m, s[0], 0, 0)),
                      pl.BlockSpec((None, N_DEV - 1, r, cdim), lambda m, s: (m, 0, 0, 0))],
            out_specs=pl.BlockSpec((None, r, cdim), lambda m, s: (m, 0, 0))),
        compiler_params=_params(),
    )(me, mine, recv)


def _matmul(a, b, mode, out_dtype, tm, tn, tk, name, bf16_copy=False, rs_sends=()):
    ga = a.shape[0] if a.ndim == 3 else None
    gb = b.shape[0] if b.ndim == 3 else None
    a2, b2 = a.shape[-2:], b.shape[-2:]
    if mode == "nn":
        (m, k), n = a2, b2[1]
    elif mode == "nt":
        (m, k), n = a2, b2[0]
    else:
        (k, m), n = a2, b2[1]
    assert m % tm == 0 and n % tn == 0 and k % tk == 0, (name, m, n, k)
    nk = k // tk
    g_n = ga or 1
    batch_out = mode == "tn" and ga is not None
    n_red = nk if batch_out else nk * g_n
    dot = {"nn": _dot_nn, "nt": _dot_nt, "tn": _dot_tn}[mode]
    acc_in_out = out_dtype == F32

    n_rs = len(rs_sends)
    rs_shapes = [r.shape for r in rs_sends]
    n_out = 2 if bf16_copy else 1
    assert not bf16_copy or acc_in_out

    def body(a_ref, b_ref, *rest):
        rs_src, rest = rest[:n_rs], rest[n_rs:]
        o_ref = rest[0]
        copy_ref = rest[1] if bf16_copy else None
        rs_dst, scratch = rest[n_out:n_out + n_rs], rest[n_out + n_rs:]
        if n_rs:
            rs_start, rs_finish = _rs_phases(rs_shapes, rs_src, rs_dst, *scratch[-2:])
            first = functools.reduce(jnp.logical_and, [pl.program_id(ax) == 0 for ax in range(4)])
            last = functools.reduce(jnp.logical_and, [pl.program_id(ax) == grid[ax] - 1 for ax in range(4)])
            pl.when(first)(rs_start)
        p = dot(a_ref[...], b_ref[...])
        kk = pl.program_id(3) if batch_out else pl.program_id(2) * nk + pl.program_id(3)
        if n_red == 1:
            o_ref[...] = p.astype(out_dtype)
            if bf16_copy:
                copy_ref[...] = p.astype(BF16)
        else:
            acc = o_ref if acc_in_out else scratch[0]

            @pl.when(kk == 0)
            def _():
                acc[...] = p

            @pl.when(kk > 0)
            def _():
                acc[...] += p

            @pl.when(kk == n_red - 1)
            def _():
                if not acc_in_out:
                    o_ref[...] = acc[...].astype(out_dtype)
                if bf16_copy:
                    copy_ref[...] = acc[...].astype(BF16)

        if n_rs:
            pl.when(last)(rs_finish)

    def order(ids):
        return ids if batch_out else (ids[2], ids[0], ids[1], ids[3])

    def a_idx(*ids):
        g, i, j, kq = order(ids)
        blk = {"nn": (i, kq), "nt": (i, kq), "tn": (kq, i)}[mode]
        return (g,) + blk if ga is not None else blk

    def b_idx(*ids):
        g, i, j, kq = order(ids)
        blk = {"nn": (kq, j), "nt": (j, kq), "tn": (kq, j)}[mode]
        return (g,) + blk if gb is not None else blk

    def o_idx(*ids):
        g, i, j, kq = order(ids)
        return (g, i, j) if batch_out else (i, j)

    a_blk = {"nn": (tm, tk), "nt": (tm, tk), "tn": (tk, tm)}[mode]
    b_blk = {"nn": (tk, tn), "nt": (tn, tk), "tn": (tk, tn)}[mode]
    if ga is not None:
        a_blk = (None,) + a_blk
    if gb is not None:
        b_blk = (None,) + b_blk
    if batch_out:
        out_shape = jax.ShapeDtypeStruct((g_n, m, n), out_dtype)
        o_blk = (None, tm, tn)
        grid = (g_n, m // tm, n // tn, nk)
    else:
        out_shape = jax.ShapeDtypeStruct((m, n), out_dtype)
        o_blk = (tm, tn)
        grid = (m // tm, n // tn, g_n, nk)
    scratch = [] if (acc_in_out or n_red == 1) else [pltpu.VMEM((tm, tn), F32)]
    any_spec = pl.BlockSpec(memory_space=pl.ANY)
    out_shapes = [out_shape] + ([jax.ShapeDtypeStruct(out_shape.shape, BF16)] if bf16_copy else [])
    res = pl.pallas_call(
        body, name=name, out_shape=tuple(out_shapes + _rs_out(rs_sends)), grid=grid,
        in_specs=[pl.BlockSpec(a_blk, a_idx), pl.BlockSpec(b_blk, b_idx)] + [any_spec] * n_rs,
        out_specs=tuple([pl.BlockSpec(o_blk, o_idx)] * n_out + [any_spec] * n_rs),
        scratch_shapes=scratch + (_rs_scratch(rs_sends) if n_rs else []), compiler_params=_params(),
    )(a, b, *rs_sends)
    return res if len(res) > 1 else res[0]


EW_TILE = 256
ROW_TILE = 512


def _rms(v):
    return lax.rsqrt(jnp.mean(v * v, axis=-1, keepdims=True) + EPS)


def _rms_bwd(dhat, vh, r):
    return r * (dhat - vh * jnp.mean(dhat * vh, axis=-1, keepdims=True))


def _tok_spec(tm, d):
    return pl.BlockSpec((tm, d), lambda i: (i, 0))


def _vec_spec(d):
    return pl.BlockSpec((1, d), lambda i: (0, 0))


def _mod_spec(tiles_per_seq, d):
    return pl.BlockSpec((None, N_MOD, d), lambda i: (i // tiles_per_seq, 0, 0))


def _seq_acc_spec(tiles_per_seq, d):
    return pl.BlockSpec((None, 1, d), lambda i: (i // tiles_per_seq, 0, 0))


def _acc(ref, val, first):
    @pl.when(first)
    def _():
        ref[...] = val

    @pl.when(jnp.logical_not(first))
    def _():
        ref[...] += val


def _colsum(v):
    return jnp.sum(v, axis=0, keepdims=True)


def _pre_mix(x2, g_pre, mod, seq):
    t, d = x2.shape
    tm = EW_TILE

    def body(x_ref, g_ref, mod_ref, h_ref):
        xv = x_ref[...]
        n = xv * _rms(xv) * g_ref[...]
        h_ref[...] = (n * (1.0 + mod_ref[1:2, :]) + mod_ref[0:1, :]).astype(BF16)

    return pl.pallas_call(
        body, name="pre_mix", out_shape=jax.ShapeDtypeStruct((t, d), BF16), grid=(t // tm,),
        in_specs=[_tok_spec(tm, d), _vec_spec(d), _mod_spec(seq // tm, d)],
        out_specs=_tok_spec(tm, d), compiler_params=_params(),
    )(x2, g_pre, mod)


def _matmul_rows(a, b, tm, tk, seq, name, epilogue, ep_in, ep_in_kinds, ep_out, ep_out_kinds, rs_sends=()):
    ga = a.shape[0] if a.ndim == 3 else None
    (m, k), n = a.shape[-2:], b.shape[-1]
    g_n = ga or 1
    nk = k // tk
    n_red = g_n * nk
    tps = seq // tm
    grid = (m // tm, g_n, nk)
    n_rs = len(rs_sends)
    rs_shapes = [r.shape for r in rs_sends]
    n_in, n_out = len(ep_in), len(ep_out)

    def spec(kind):
        return {"tok": pl.BlockSpec((tm, n), lambda i, g, kq: (i, 0)),
                "vec": pl.BlockSpec((1, n), lambda i, g, kq: (0, 0)),
                "mod": pl.BlockSpec((None, N_MOD, n), lambda i, g, kq: (i // tps, 0, 0)),
                "seq": pl.BlockSpec((None, 1, n), lambda i, g, kq: (i // tps, 0, 0)),
                "loss": pl.BlockSpec((1, LANES), lambda i, g, kq: (0, 0))}[kind]

    def body(a_ref, b_ref, *rest):
        in_refs, rest = rest[:n_in], rest[n_in:]
        rs_src, rest = rest[:n_rs], rest[n_rs:]
        out_refs, rest = rest[:n_out], rest[n_out:]
        rs_dst, rest = rest[:n_rs], rest[n_rs:]
        acc = rest[0]
        i, kk = pl.program_id(0), pl.program_id(1) * nk + pl.program_id(2)
        if n_rs:
            rs_start, rs_finish = _rs_phases(rs_shapes, rs_src, rs_dst, *rest[1:])
            pl.when(jnp.logical_and(i == 0, kk == 0))(rs_start)
        p = _dot_nn(a_ref[...], b_ref[...])
        if n_red == 1:
            epilogue(p, i, tps, in_refs, out_refs)
        else:
            @pl.when(kk == 0)
            def _():
                acc[...] = p

            @pl.when(jnp.logical_and(kk > 0, kk < n_red - 1))
            def _():
                acc[...] += p

            @pl.when(kk == n_red - 1)
            def _():
                epilogue(acc[...] + p, i, tps, in_refs, out_refs)

        if n_rs:
            pl.when(jnp.logical_and(i == grid[0] - 1, kk == n_red - 1))(rs_finish)

    a_blk = (tm, tk) if ga is None else (None, tm, tk)
    b_blk = (tk, n) if ga is None else (None, tk, n)
    a_idx = (lambda i, g, kq: (i, kq)) if ga is None else (lambda i, g, kq: (g, i, kq))
    b_idx = (lambda i, g, kq: (kq, 0)) if ga is None else (lambda i, g, kq: (g, kq, 0))
    any_spec = pl.BlockSpec(memory_space=pl.ANY)
    res = pl.pallas_call(
        body, name=name, grid=grid, out_shape=tuple(list(ep_out) + _rs_out(rs_sends)),
        in_specs=[pl.BlockSpec(a_blk, a_idx), pl.BlockSpec(b_blk, b_idx)] + [spec(kd) for kd in ep_in_kinds]
        + [any_spec] * n_rs,
        out_specs=tuple([spec(kd) for kd in ep_out_kinds] + [any_spec] * n_rs),
        scratch_shapes=[pltpu.VMEM((tm, n), F32)] + (_rs_scratch(rs_sends) if n_rs else []),
        compiler_params=_params(),
    )(a, b, *ep_in, *rs_sends)
    return res


def _mid_epilogue(mv, i, tps, in_refs, out_refs):
    x_ref, gpost_ref, gpre_ref, mod_ref = in_refs
    mix_ref, x1_ref, h2_ref = out_refs
    mix_ref[...] = mv
    x1 = x_ref[...] + mod_ref[2:3, :] * (mv * _rms(mv) * gpost_ref[...])
    x1_ref[...] = x1
    n = x1 * _rms(x1) * gpre_ref[...]
    h2_ref[...] = (n * (1.0 + mod_ref[4:5, :]) + mod_ref[3:4, :]).astype(BF16)


def _post_epilogue(fv, i, tps, in_refs, out_refs):
    x1_ref, tgt_ref, g_ref, mod_ref = in_refs
    loss_ref, dy_ref, df_ref, dgate_ref, gg_ref = out_refs
    d = fv.shape[1]
    r = _rms(fv)
    fh = fv * r
    nf = fh * g_ref[...]
    gate = mod_ref[5:6, :]
    err = x1_ref[...] + gate * nf - tgt_ref[...]
    _acc(loss_ref, jnp.sum(_colsum(err * err), axis=1, keepdims=True) * jnp.ones((1, LANES), F32), i == 0)
    dy = err * (1.0 / d)
    dy_ref[...] = dy
    _acc(dgate_ref, _colsum(dy * nf), i % tps == 0)
    dn = dy * gate
    _acc(gg_ref, _colsum(dn * fh), i == 0)
    df_ref[...] = _rms_bwd(dn * g_ref[...], fh, r).astype(BF16)


def _bwd_mid_epilogue(dh, i, tps, in_refs, out_refs):
    dy_ref, x1_ref, mix_ref, gpre_ref, gpost_ref, mod_ref = in_refs
    dx1_ref, dmix_ref, dshift_ref, dscale_ref, dgate_ref, ggpre_ref, ggpost_ref = out_refs
    seq_first = i % tps == 0
    x1 = x1_ref[...]
    r = _rms(x1)
    xh = x1 * r
    gpre = gpre_ref[...]
    _acc(dshift_ref, _colsum(dh), seq_first)
    _acc(dscale_ref, _colsum(dh * xh * gpre), seq_first)
    dn = dh * (1.0 + mod_ref[4:5, :])
    _acc(ggpre_ref, _colsum(dn * xh), i == 0)
    dx1 = dy_ref[...] + _rms_bwd(dn * gpre, xh, r)
    dx1_ref[...] = dx1
    mv = mix_ref[...]
    rm = _rms(mv)
    mh = mv * rm
    gpost = gpost_ref[...]
    _acc(dgate_ref, _colsum(dx1 * mh * gpost), seq_first)
    dnm = dx1 * mod_ref[2:3, :]
    _acc(ggpost_ref, _colsum(dnm * mh), i == 0)
    dmix_ref[...] = _rms_bwd(dnm * gpost, mh, rm).astype(BF16)


def _bwd_pre_epilogue(dh, i, tps, in_refs, out_refs):
    dx1_ref, x_ref, g_ref, mod_ref = in_refs
    gx_ref, dshift_ref, dscale_ref, gg_ref = out_refs
    seq_first = i % tps == 0
    xv = x_ref[...]
    r = _rms(xv)
    xh = xv * r
    g = g_ref[...]
    _acc(dshift_ref, _colsum(dh), seq_first)
    _acc(dscale_ref, _colsum(dh * xh * g), seq_first)
    dn = dh * (1.0 + mod_ref[1:2, :])
    _acc(gg_ref, _colsum(dn * xh), i == 0)
    gx_ref[...] = dx1_ref[...] + _rms_bwd(dn * g, xh, r)


def _ffn_up(h2, wgu, tm, tn):
    t, d = h2.shape
    f = wgu.shape[1]

    def body(h_ref, w_ref, gu_ref, act_ref):
        h = h_ref[...]
        g = _dot_nt(h, w_ref[0])
        u = _dot_nt(h, w_ref[1])
        gu_ref[0] = g.astype(BF16)
        gu_ref[1] = u.astype(BF16)
        act_ref[...] = (g * jax.nn.sigmoid(g) * u).astype(BF16)

    return pl.pallas_call(
        body, name="ffn_up", grid=(t // tm, f // tn),
        out_shape=(jax.ShapeDtypeStruct((2, t, f), BF16), jax.ShapeDtypeStruct((t, f), BF16)),
        in_specs=[pl.BlockSpec((tm, d), lambda i, j: (i, 0)), pl.BlockSpec((2, tn, d), lambda i, j: (0, j, 0))],
        out_specs=(pl.BlockSpec((2, tm, tn), lambda i, j: (0, i, j)), pl.BlockSpec((tm, tn), lambda i, j: (i, j))),
        compiler_params=_params(),
    )(h2, wgu)


def _ffn_act_bwd(df, wd, gu, tm, tn):
    t, d = df.shape
    f = wd.shape[0]

    def body(df_ref, w_ref, gu_ref, dgu_ref):
        da = _dot_nt(df_ref[...], w_ref[...])
        g = gu_ref[0].astype(F32)
        u = gu_ref[1].astype(F32)
        s = jax.nn.sigmoid(g)
        silu = g * s
        dgu_ref[0] = (da * u * (s + silu * (1.0 - s))).astype(BF16)
        dgu_ref[1] = (da * silu).astype(BF16)

    return pl.pallas_call(
        body, name="ffn_act_bwd", grid=(t // tm, f // tn),
        out_shape=jax.ShapeDtypeStruct((2, t, f), BF16),
        in_specs=[pl.BlockSpec((tm, d), lambda i, j: (i, 0)), pl.BlockSpec((tn, d), lambda i, j: (j, 0)),
                  pl.BlockSpec((2, tm, tn), lambda i, j: (0, i, j))],
        out_specs=pl.BlockSpec((2, tm, tn), lambda i, j: (0, i, j)),
        compiler_params=_params(),
    )(df, wd, gu)


SIGN_BIT = 0x80000000
Q_SCALE = 1.0 / math.sqrt(HEAD_DIM)


def _split_dot(v, tri2):
    hi = v.astype(BF16)
    lo = (v - hi.astype(F32)).astype(BF16)
    return _dot_nn(jnp.concatenate([hi, lo], axis=1), tri2)


def _sb_tile(qs, k2, mask, ntri2, cur):
    z = _dot_nt(qs, k2)
    neg_abs = lax.bitcast_convert_type(lax.bitcast_convert_type(z, jnp.uint32) | jnp.uint32(SIGN_BIT), F32)
    sp = jnp.maximum(z, 0.0) + jnp.log(1.0 + jnp.exp(neg_abs))
    if mask is not None:
        sp = jnp.where(mask, sp, 0.0)
    w = jnp.exp(z + _split_dot(sp, ntri2) + cur)
    if mask is not None:
        w = jnp.where(mask, w, 0.0)
    return z, sp, w


def _stack_heads(v, lane, scale=None):
    if scale is not None:
        v = v * jnp.asarray(scale, v.dtype)
    zero = jnp.zeros_like(v)
    return jnp.concatenate([jnp.where(lane < HEAD_DIM, v, zero), jnp.where(lane >= HEAD_DIM, v, zero)], axis=0)


def _diag_mask(tq):
    row = lax.broadcasted_iota(jnp.int32, (2 * tq, tq), 0)
    col = lax.broadcasted_iota(jnp.int32, (2 * tq, tq), 1)
    return col < jnp.where(row >= tq, row - tq, row)


def _attn_fwd(proj, tri_after, n_seq, seq, ag_srcs, ag_out_shapes, ag_dests):
    t = proj.shape[0]
    tq = ATT_TILE
    npp = ATT_PAIRS
    n_blk = (proj.shape[1] // 4) // (npp * LANES)
    n_ag, n_ag_out = len(ag_srcs), len(ag_out_shapes)
    n_steps = n_seq * n_blk

    def body(q_ref, k_ref, v_ref, tri_ref, *rest):
        ag_src, rest = rest[:n_ag], rest[n_ag:]
        o_ref, cs_ref = rest[:2]
        ag_out, rest = rest[2:2 + n_ag_out], rest[2 + n_ag_out:]
        oacc, cmat, carry = rest[:3]
        ag_start, ag_forward, ag_finish = _ag_phases(ag_dests, ag_src, ag_out, *rest[3:])
        step = pl.program_id(0) * n_blk + pl.program_id(1)
        pl.when(step == 0)(ag_start)
        pl.when(step == (3 * n_steps) // 4)(ag_forward)
        lane = lax.broadcasted_iota(jnp.int32, (1, LANES), 1)
        ntri2 = tri_ref[...]
        diag = _diag_mask(tq)

        def q_tile(qi, _):
            r0 = pl.multiple_of(qi * tq, tq)
            qs = [_stack_heads(q_ref[pl.ds(r0, tq), pp * LANES:(pp + 1) * LANES], lane, Q_SCALE)
                  for pp in range(npp)]
            carry[...] = jnp.zeros_like(carry)
            cmat[...] = jnp.zeros_like(cmat)
            oacc[...] = jnp.zeros_like(oacc)

            def run_tiles(tiles):
                for pp in range(npp):
                    cols = slice(pp * LANES, (pp + 1) * LANES)
                    cur = carry[pp]
                    cm = cmat[pp]
                    pv = None
                    for kb, mask in tiles:
                        c0 = pl.multiple_of(kb * tq, tq)
                        _, sp, w = _sb_tile(qs[pp], k_ref[pl.ds(c0, tq), cols], mask, ntri2, cur)
                        p = _dot_nn(w.astype(BF16), v_ref[pl.ds(c0, tq), cols])
                        pv = p if pv is None else pv + p
                        cm = jnp.where(lane == kb, cur, cm)
                        cur = cur - jnp.sum(sp, axis=1, keepdims=True)
                    oacc[pp] += pv
                    cmat[pp] = cm
                    carry[pp] = cur

            odd = qi % 2

            @pl.when(odd == 0)
            def _():
                run_tiles([(qi, diag)])

            @pl.when(odd == 1)
            def _():
                run_tiles([(qi, diag), (qi - 1, None)])

            def pair(j, _):
                kb = qi - 1 - odd - 2 * j
                run_tiles([(kb, None), (kb - 1, None)])
                return 0

            lax.fori_loop(0, qi // 2, pair, 0)
            for pp in range(npp):
                c_off = 2 * pp * LANES
                cs_ref[pl.ds(r0, tq), c_off:c_off + LANES] = cmat[pp, 0:tq, :]
                cs_ref[pl.ds(r0, tq), c_off + LANES:c_off + 2 * LANES] = cmat[pp, tq:2 * tq, :]
                o_ref[pl.ds(r0, tq), pp * LANES:(pp + 1) * LANES] = jnp.where(
                    lane < HEAD_DIM, oacc[pp, 0:tq, :], oacc[pp, tq:2 * tq, :]).astype(BF16)
            return 0

        lax.fori_loop(0, seq // tq, q_tile, 0)
        pl.when(step == n_steps - 1)(ag_finish)

    wid = npp * LANES
    blk = lambda off: pl.BlockSpec((seq, wid), lambda b, p: (b, off + p))
    any_spec = pl.BlockSpec(memory_space=pl.ANY)
    return pl.pallas_call(
        body, name="attn_fwd", grid=(n_seq, n_blk),
        out_shape=(jax.ShapeDtypeStruct((2, t, n_blk * wid), BF16),
                   jax.ShapeDtypeStruct((t, n_blk * 2 * wid), F32), *ag_out_shapes),
        in_specs=[blk(0), blk(n_blk), blk(2 * n_blk), pl.BlockSpec((2 * tq, tq), lambda b, p: (0, 0))]
        + [any_spec] * n_ag,
        out_specs=(pl.BlockSpec((None, seq, wid), lambda b, p: (0, b, p)),
                   pl.BlockSpec((seq, 2 * wid), lambda b, p: (b, p)), *([any_spec] * n_ag_out)),
        scratch_shapes=[pltpu.VMEM((npp, 2 * tq, LANES), F32), pltpu.VMEM((npp, 2 * tq, LANES), F32),
                        pltpu.VMEM((npp, 2 * tq, 1), F32)] + _ag_scratch(n_ag),
        compiler_params=_params(),
    )(proj, proj, proj, tri_after, *ag_srcs)


def _attn_bwd(proj, dcat, cstats, tri_after, tri_incl, n_seq, seq, rs_sends):
    t = proj.shape[0]
    tq = ATT_TILE
    npp = ATT_PAIRS
    width = proj.shape[1] // 4
    n_blk = width // (npp * LANES)
    n_rs = len(rs_sends)
    rs_shapes = [r.shape for r in rs_sends]
    n_steps = n_seq * n_blk

    def body(q_ref, k_ref, v_ref, do_ref, cs_ref, tria_ref, trii_ref, *rest):
        rs_src, rest = rest[:n_rs], rest[n_rs:]
        out_ref = rest[0]
        rs_dst, rest = rest[1:1 + n_rs], rest[1 + n_rs:]
        dq_acc, dk_acc, dv_acc, ecarry = rest[:4]
        rs_start, rs_finish = _rs_phases(rs_shapes, rs_src, rs_dst, *rest[4:])
        step = pl.program_id(0) * n_blk + pl.program_id(1)
        pl.when(step == 0)(rs_start)
        lane = lax.broadcasted_iota(jnp.int32, (1, LANES), 1)
        ntri2 = tria_ref[...]
        tri_i = trii_ref[...]
        diag = _diag_mask(tq)
        dk_acc[...] = jnp.zeros_like(dk_acc)
        dv_acc[...] = jnp.zeros_like(dv_acc)

        def q_tile(qi, _):
            r0 = pl.multiple_of(qi * tq, tq)
            qs, dos, cs = [], [], []
            for pp in range(npp):
                cols = slice(pp * LANES, (pp + 1) * LANES)
                qs.append(_stack_heads(q_ref[pl.ds(r0, tq), cols], lane, Q_SCALE))
                dos.append(_stack_heads(do_ref[pl.ds(r0, tq), cols], lane))
                c_off = 2 * pp * LANES
                cs.append(jnp.concatenate([cs_ref[pl.ds(r0, tq), c_off:c_off + LANES],
                                           cs_ref[pl.ds(r0, tq), c_off + LANES:c_off + 2 * LANES]], axis=0))
            ecarry[...] = jnp.zeros_like(ecarry)
            dq_acc[...] = jnp.zeros_like(dq_acc)

            def run_tiles(tiles):
                for pp in range(npp):
                    cols = slice(pp * LANES, (pp + 1) * LANES)
                    ec = ecarry[pp]
                    dq = None
                    for kb, mask in tiles:
                        c0 = pl.multiple_of(kb * tq, tq)
                        k2 = k_ref[pl.ds(c0, tq), cols]
                        v2 = v_ref[pl.ds(c0, tq), cols]
                        cur = jnp.sum(jnp.where(lane == kb, cs[pp], 0.0), axis=1, keepdims=True)
                        z, sp, w = _sb_tile(qs[pp], k2, mask, ntri2, cur)
                        ee = w * _dot_nt(dos[pp], v2)
                        einc = _dot_nn(ee.astype(BF16), tri_i) + ec
                        dz = ee - jnp.exp(z - sp) * einc
                        if mask is not None:
                            dz = jnp.where(mask, dz, 0.0)
                        dzb = dz.astype(BF16)
                        p = _dot_nn(dzb, k2)
                        dq = p if dq is None else dq + p
                        dk_acc[pp, pl.ds(c0, tq), :] += _dot_tn(dzb, qs[pp])
                        dv_acc[pp, pl.ds(c0, tq), :] += _dot_tn(w.astype(BF16), dos[pp])
                        ec = ec + jnp.sum(ee, axis=1, keepdims=True)
                    dq_acc[pp] += dq
                    ecarry[pp] = ec

            def pair(j, _):
                run_tiles([(2 * j, None), (2 * j + 1, None)])
                return 0

            lax.fori_loop(0, qi // 2, pair, 0)
            odd = qi % 2

            @pl.when(odd == 0)
            def _():
                run_tiles([(qi, diag)])

            @pl.when(odd == 1)
            def _():
                run_tiles([(qi - 1, None), (qi, diag)])

            for pp in range(npp):
                dq = jnp.where(lane < HEAD_DIM, dq_acc[pp, 0:tq, :], dq_acc[pp, tq:2 * tq, :])
                out_ref[0, pl.ds(r0, tq), pp * LANES:(pp + 1) * LANES] = (dq * Q_SCALE).astype(BF16)
            return 0

        lax.fori_loop(0, seq // tq, q_tile, 0)
        for pp in range(npp):
            cols = slice(pp * LANES, (pp + 1) * LANES)
            out_ref[1, :, cols] = dk_acc[pp].astype(BF16)
            out_ref[2, :, cols] = dv_acc[pp].astype(BF16)
        pl.when(step == n_steps - 1)(rs_finish)

    wid = npp * LANES
    blk = lambda off: pl.BlockSpec((seq, wid), lambda b, p: (b, off + p))
    tri_spec = pl.BlockSpec((2 * tq, tq), lambda b, p: (0, 0))
    any_spec = pl.BlockSpec(memory_space=pl.ANY)
    return pl.pallas_call(
        body, name="attn_bwd", grid=(n_seq, n_blk),
        out_shape=(jax.ShapeDtypeStruct((4, t, width), BF16), *_rs_out(rs_sends)),
        in_specs=[blk(0), blk(n_blk), blk(2 * n_blk), pl.BlockSpec((seq, wid), lambda b, p: (b, p)),
                  pl.BlockSpec((seq, 2 * wid), lambda b, p: (b, p)), tri_spec,
                  pl.BlockSpec((tq, tq), lambda b, p: (0, 0))] + [any_spec] * n_rs,
        out_specs=(pl.BlockSpec((3, seq, wid), lambda b, p: (0, b, p)), *([any_spec] * n_rs)),
        scratch_shapes=[pltpu.VMEM((npp, 2 * tq, LANES), F32), pltpu.VMEM((npp, seq, LANES), F32),
                        pltpu.VMEM((npp, seq, LANES), F32), pltpu.VMEM((npp, 2 * tq, 1), F32)]
        + _rs_scratch(rs_sends),
        compiler_params=_params(),
    )(proj, proj, proj, dcat, cstats, tri_after, tri_incl, *rs_sends)


def _window_terms(g, rows):
    win = jnp.where(g == 0, POOL_WINDOWS[0], jnp.where(g == 1, POOL_WINDOWS[1],
                    jnp.where(g == 2, POOL_WINDOWS[2], POOL_WINDOWS[3])))
    cnt = jnp.minimum(rows + 1, win).astype(F32)
    return win, cnt


def _window_sum(v, g, rows, forward):
    s_len = v.shape[0]
    sums = []
    s = v
    for step in range(len(POOL_WINDOWS)):
        sh = 1 << step
        if forward:
            shifted = jnp.where(rows < s_len - sh, pltpu.roll(s, s_len - sh, axis=0), 0.0)
        else:
            shifted = jnp.where(rows >= sh, pltpu.roll(s, sh, axis=0), 0.0)
        s = s + shifted
        sums.append(s)
    return jnp.where(g == 0, sums[0], jnp.where(g == 1, sums[1], jnp.where(g == 2, sums[2], sums[3])))


def _pooled(u, g, rows):
    _, cnt = _window_terms(g, rows)
    return _window_sum(u, g, rows, forward=False) / cnt - u


def _pool_fwd(proj, w_pool, pool_scale, cat, n_seq, seq):
    n_grp = len(POOL_WINDOWS)
    u_off = 3 * (proj.shape[1] // 4) // LANES

    def body(u_ref, w_ref, s_ref, alias_ref, o_ref):
        del alias_ref
        g = pl.program_id(1)
        rows = lax.broadcasted_iota(jnp.int32, (seq, 1), 0)
        pooled = _pooled(u_ref[...].astype(F32), g, rows)
        y = _dot_nn(pooled.astype(BF16), w_ref[...].astype(BF16))
        o_ref[...] = (y * s_ref[...]).astype(BF16)

    return pl.pallas_call(
        body, name="pool_fwd", grid=(n_seq, n_grp),
        out_shape=jax.ShapeDtypeStruct(cat.shape, BF16),
        in_specs=[pl.BlockSpec((seq, LANES), lambda b, g: (b, u_off + g)),
                  pl.BlockSpec((None, POOL_GROUP_DIM, POOL_GROUP_DIM), lambda b, g: (g, 0, 0)),
                  pl.BlockSpec((1, POOL_GROUP_DIM), lambda b, g: (0, g)),
                  pl.BlockSpec(memory_space=pl.ANY)],
        out_specs=pl.BlockSpec((None, seq, LANES), lambda b, g: (1, b, g)),
        input_output_aliases={3: 0},
        compiler_params=_params(),
    )(proj, w_pool, pool_scale, cat)


def _pool_bwd(proj, dcat, w_pool, pool_scale, dqkv, n_seq, seq):
    n_grp = len(POOL_WINDOWS)
    width = proj.shape[1] // 4
    u_off = 3 * width // LANES
    dp_off = width // LANES

    def body(u_ref, dp_ref, w_ref, s_ref, alias_ref, du_ref, gw_ref, gs_ref):
        del alias_ref
        g = pl.program_id(0)
        b = pl.program_id(1)
        rows = lax.broadcasted_iota(jnp.int32, (seq, 1), 0)
        pooled = _pooled(u_ref[...].astype(F32), g, rows)
        pb = pooled.astype(BF16)
        wb = w_ref[...].astype(BF16)
        z = _dot_nn(pb, wb)
        dp = dp_ref[...].astype(F32)
        _acc(gs_ref, _colsum(dp * z), b == 0)
        dys = (dp * s_ref[...]).astype(BF16)
        _acc(gw_ref, _dot_tn(pb, dys), b == 0)
        dpooled = _dot_nt(dys, wb)
        _, cnt = _window_terms(g, rows)
        du = _window_sum(dpooled / cnt, g, rows, forward=True) - dpooled
        du_ref[...] = du.astype(BF16)

    t = proj.shape[0]
    return pl.pallas_call(
        body, name="pool_bwd", grid=(n_grp, n_seq),
        out_shape=(jax.ShapeDtypeStruct(dqkv.shape, BF16),
                   jax.ShapeDtypeStruct((n_grp, POOL_GROUP_DIM, POOL_GROUP_DIM), F32),
                   jax.ShapeDtypeStruct((1, n_grp * POOL_GROUP_DIM), F32)),
        in_specs=[pl.BlockSpec((seq, LANES), lambda g, b: (b, u_off + g)),
                  pl.BlockSpec((seq, LANES), lambda g, b: (b, dp_off + g)),
                  pl.BlockSpec((None, POOL_GROUP_DIM, POOL_GROUP_DIM), lambda g, b: (g, 0, 0)),
                  pl.BlockSpec((1, POOL_GROUP_DIM), lambda g, b: (0, g)),
                  pl.BlockSpec(memory_space=pl.ANY)],
        out_specs=(pl.BlockSpec((None, seq, LANES), lambda g, b: (3, b, g)),
                   pl.BlockSpec((None, POOL_GROUP_DIM, POOL_GROUP_DIM), lambda g, b: (g, 0, 0)),
                   pl.BlockSpec((1, POOL_GROUP_DIM), lambda g, b: (0, g))),
        input_output_aliases={4: 0},
        compiler_params=_params(),
    )(proj, dcat, w_pool, pool_scale, dqkv)


def _cond_fwd(c_all, w_cond, b_cols):
    n, _ = c_all.shape
    cols = w_cond.shape[1]

    def body(c_ref, w_ref, b_ref, o_ref):
        cv = c_ref[...]
        a = cv * jax.nn.sigmoid(cv)
        o_ref[...] = jnp.dot(a, w_ref[...], preferred_element_type=F32,
                             precision=lax.Precision.HIGHEST) + b_ref[...]

    return pl.pallas_call(
        body, name="cond_fwd", out_shape=jax.ShapeDtypeStruct((n, cols), F32),
        compiler_params=_params(),
    )(c_all, w_cond, b_cols)


def _cond_bwd(c_all, dmod_all, dmod_cols):
    n, d = c_all.shape
    cols = dmod_cols.shape[1]

    def body(c_ref, dm_ref, dmc_ref, gw_ref, gb_ref):
        cv = c_ref[...]
        a = cv * jax.nn.sigmoid(cv)
        gw_ref[...] = lax.dot_general(a, dmc_ref[...], (((0,), (0,)), ((), ())),
                                      preferred_element_type=F32, precision=lax.Precision.HIGHEST)
        gb_ref[...] = _colsum(dm_ref[...])

    return pl.pallas_call(
        body, name="cond_bwd",
        out_shape=(jax.ShapeDtypeStruct((d, cols), F32), jax.ShapeDtypeStruct((1, dmod_all.shape[1]), F32)),
        compiler_params=_params(),
    )(c_all, dmod_all, dmod_cols)


def _adamw_math(w, g, m, v):
    m = ADAM_B1 * m + (1.0 - ADAM_B1) * g
    v = ADAM_B2 * v + (1.0 - ADAM_B2) * (g * g)
    m_hat = m / (1.0 - ADAM_B1 ** ADAM_STEP)
    v_hat = v / (1.0 - ADAM_B2 ** ADAM_STEP)
    delta = -ADAM_LR * (m_hat / (jnp.sqrt(v_hat) + ADAM_EPS) + ADAM_WD * w)
    return delta, m, v


def _adamw(w, g, m, v, rows, name):
    r, cdim = w.shape

    def body(w_ref, g_ref, m_ref, v_ref, d_ref, nm_ref, nv_ref):
        d_ref[...], nm_ref[...], nv_ref[...] = _adamw_math(w_ref[...], g_ref[...], m_ref[...], v_ref[...])

    spec = pl.BlockSpec((rows, cdim), lambda i: (i, 0))
    sds = jax.ShapeDtypeStruct((r, cdim), F32)
    return pl.pallas_call(
        body, name=name, grid=(r // rows,), out_shape=(sds, sds, sds),
        in_specs=[spec] * 4, out_specs=(spec, spec, spec), compiler_params=_params(),
    )(w, g, m, v)


def _adamw_small(ws, gparts, ms, vs, name):
    n = len(ws)

    def body(*refs):
        w_r, g_r, m_r, v_r = refs[:n], refs[n:2 * n], refs[2 * n:3 * n], refs[3 * n:4 * n]
        outs = refs[4 * n:]
        for i in range(n):
            g = g_r[i][0]
            for dev in range(1, g_r[i].shape[0]):
                g = g + g_r[i][dev]
            delta, m, v = _adamw_math(w_r[i][...], g, m_r[i][...], v_r[i][...])
            outs[i][...] = g
            outs[n + i][...] = delta
            outs[2 * n + i][...] = m
            outs[3 * n + i][...] = v

    sds = [jax.ShapeDtypeStruct(w.shape, F32) for w in ws]
    return pl.pallas_call(
        body, name=name, out_shape=tuple(sds * 4), compiler_params=_params(),
    )(*ws, *gparts, *ms, *vs)


def kernel(x, c, w_cond, b_cond, g_mix_pre, g_mix_post, w_in, w_pool, pool_scale, w_out, g_ffn_pre, g_ffn_post, w_gate, w_up, w_down, loss_target, m_w_cond, m_b_cond, m_g_mix_pre, m_g_mix_post, m_w_in, m_w_pool, m_pool_scale, m_w_out, m_g_ffn_pre, m_g_ffn_post, m_w_gate, m_w_up, m_w_down, v_w_cond, v_b_cond, v_g_mix_pre, v_g_mix_post, v_w_in, v_w_pool, v_pool_scale, v_w_out, v_g_ffn_pre, v_g_ffn_post, v_w_gate, v_w_up, v_w_down):
    n_seq, seq, d = x.shape
    t = n_seq * seq
    xi, yi, ci = _mesh_pos()
    me = 4 * xi + 2 * yi + ci
    x2 = x.reshape(t, d)
    tgt2 = loss_target.reshape(t, d)
    in_rows = w_in.shape[2]
    out_rows = w_out.shape[1]
    ff_rows = w_gate.shape[2]
    ff = N_DEV * ff_rows
    cond_cols = w_cond.shape[2]

    win_t = w_in[0].T.astype(BF16)
    wout_s = w_out[0].astype(BF16)
    wg_t = w_gate[0].T.astype(BF16)
    wu_t = w_up[0].T.astype(BF16)
    wd_s = w_down[0].astype(BF16)
    c_all, win_g = _all_gather(
        [c, win_t],
        [jax.ShapeDtypeStruct((N_DEV, n_seq, d), F32), jax.ShapeDtypeStruct((N_DEV, in_rows, d), BF16)],
        [(0, ()), (1, ())], "ag_c_win")
    c_all = c_all.reshape(N_DEV * n_seq, d)
    win_full = win_g.reshape(N_DEV * in_rows, d)

    b_cols = lax.dynamic_slice_in_dim(b_cond, me * cond_cols, cond_cols, axis=1)
    mod_cols = _cond_fwd(c_all, w_cond[0], b_cols)
    (mod_g,) = _all_gather([mod_cols], [jax.ShapeDtypeStruct((N_DEV,) + mod_cols.shape, F32)], [(0, ())], "ag_mod")
    mod_mine = lax.dynamic_slice_in_dim(mod_g, me * n_seq, n_seq, axis=1)
    mod = jnp.transpose(mod_mine, (1, 0, 2)).reshape(n_seq, N_MOD, d)

    h1 = _pre_mix(x2, g_mix_pre, mod, seq)
    proj = _matmul(h1, win_full, "nt", BF16, 1024, 512, d, "proj")
    tq = ATT_TILE
    ids = jnp.arange(tq)
    tri_after = jnp.tile(-(ids[:, None] >= ids[None, :]).astype(BF16), (2, 1))
    tri_incl = (ids[:, None] <= ids[None, :]).astype(BF16)
    attn, cstats, wout_g, wgu_g, wd_g = _attn_fwd(
        proj, tri_after, n_seq, seq, [wout_s, wg_t, wu_t, wd_s],
        [jax.ShapeDtypeStruct((N_DEV, out_rows, d), BF16), jax.ShapeDtypeStruct((2, N_DEV, ff_rows, d), BF16),
         jax.ShapeDtypeStruct((N_DEV, ff_rows, d), BF16)],
        [(0, ()), (1, (0,)), (1, (1,)), (2, ())])
    wout_full = wout_g.reshape(N_DEV * out_rows, d)
    wgu_full = wgu_g.reshape(2, ff, d)
    wd_full = wd_g.reshape(ff, d)
    cat = _pool_fwd(proj, w_pool[0], pool_scale, attn, n_seq, seq)
    tok_f32, tok_bf16 = jax.ShapeDtypeStruct((t, d), F32), jax.ShapeDtypeStruct((t, d), BF16)
    seq_sds, vec_sds = jax.ShapeDtypeStruct((n_seq, 1, d), F32), jax.ShapeDtypeStruct((1, d), F32)
    mix, x1, h2 = _matmul_rows(
        cat, wout_full.reshape(2, d // 2, d), ROW_TILE, d // 2, seq, "mix_mid", _mid_epilogue,
        [x2, g_mix_post, g_ffn_pre, mod], ["tok", "vec", "vec", "mod"],
        [tok_f32, tok_f32, tok_bf16], ["tok", "tok", "tok"])
    gu, act = _ffn_up(h2, wgu_full, 512, ff // 2)
    loss_sum, dy, df, dgate_f, gg_ffn_post = _matmul_rows(
        act, wd_full, ROW_TILE, ff, seq, "ffn_down_post", _post_epilogue,
        [x1, tgt2, g_ffn_post, mod], ["tok", "tok", "vec", "mod"],
        [jax.ShapeDtypeStruct((1, LANES), F32), tok_f32, tok_bf16, seq_sds, vec_sds],
        ["loss", "tok", "tok", "seq", "vec"])

    dgu = _ffn_act_bwd(df, wd_full, gu, 512, ff // 2)
    gwd, gwd_b = _matmul(act, df, "tn", F32, ff // 2, d, 1024, "grad_w_down", bf16_copy=True)
    gwgu, gwgu_b = _matmul(dgu, h2, "tn", F32, ff // 2, d, 1024, "grad_w_gate_up", bf16_copy=True)
    dx1, dmix, dshift_f, dscale_f, dgate_m, gg_ffn_pre, gg_mix_post = _matmul_rows(
        dgu, wgu_full, ROW_TILE, ff, seq, "dh2_bwd_mid", _bwd_mid_epilogue,
        [dy, x1, mix, g_ffn_pre, g_mix_post, mod], ["tok", "tok", "tok", "vec", "vec", "mod"],
        [tok_f32, tok_bf16, seq_sds, seq_sds, seq_sds, vec_sds, vec_sds],
        ["tok", "tok", "seq", "seq", "seq", "vec", "vec"])
    dcat = _matmul(dmix, wout_full, "nt", BF16, 1024, 512, d, "dcat")
    gwout, gwout_b = _matmul(cat, dmix, "tn", F32, d // 2, d, 1024, "grad_w_out", bf16_copy=True)
    dqkv, rv_wgu, rv_wd, rv_wout = _attn_bwd(
        proj, dcat, cstats, tri_after, tri_incl, n_seq, seq,
        [gwgu_b.reshape(2, N_DEV, ff_rows, d), gwd_b.reshape(1, N_DEV, ff_rows, d),
         gwout_b.reshape(1, N_DEV, out_rows, d)])
    dproj, gw_pool, gs_pool = _pool_bwd(proj, dcat, w_pool[0], pool_scale, dqkv, n_seq, seq)
    gwin, gwin_b = _matmul(dproj, h1, "tn", F32, d // 2, d, 1024, "grad_w_in", bf16_copy=True)
    grad_x, dshift_m, dscale_m, gg_mix_pre, rv_win = _matmul_rows(
        dproj, win_full.reshape(4, d // 2, d), ROW_TILE, d // 2, seq, "dh1_bwd_pre", _bwd_pre_epilogue,
        [dx1, x2, g_mix_pre, mod], ["tok", "tok", "vec", "mod"],
        [tok_f32, seq_sds, seq_sds, vec_sds], ["tok", "seq", "seq", "vec"],
        rs_sends=[gwin_b.reshape(1, N_DEV, in_rows, d)])

    r_wgu = _rs_final(gwgu.reshape(2, N_DEV, ff_rows, d), rv_wgu, "rs_final_gate_up")
    r_wd = _rs_final(gwd.reshape(1, N_DEV, ff_rows, d), rv_wd, "rs_final_down")
    r_wout = _rs_final(gwout.reshape(1, N_DEV, out_rows, d), rv_wout, "rs_final_out")
    r_win = _rs_final(gwin.reshape(1, N_DEV, in_rows, d), rv_win, "rs_final_in")
    grad_w_in = r_win[0].T
    grad_w_out = r_wout[0]
    grad_w_gate = r_wgu[0].T
    grad_w_up = r_wgu[1].T
    grad_w_down = r_wd[0]

    dmod = jnp.concatenate([dshift_m, dscale_m, dgate_m, dshift_f, dscale_f, dgate_f], axis=1)
    small = jnp.concatenate([gg_mix_pre, gg_mix_post, gg_ffn_pre, gg_ffn_post,
                             jnp.pad(gs_pool, ((0, 0), (0, d - gs_pool.shape[1]))),
                             jnp.pad(loss_sum, ((0, 0), (0, d - loss_sum.shape[1]))), jnp.zeros((2, d), F32),
                             gw_pool.reshape(-1, d), dmod.reshape(n_seq * N_MOD, d)], axis=0)
    n_gw = gw_pool.size // d
    (small_g,) = _all_gather([small], [jax.ShapeDtypeStruct((N_DEV,) + small.shape, F32)], [(0, ())], "ag_small")
    loss = jnp.sum(small_g[:, 5, 0]) * (0.5 / d)
    dmod_all = small_g[:, 8 + n_gw:, :].reshape(N_DEV * n_seq, N_MOD * d)
    dmod_cols = lax.dynamic_slice_in_dim(dmod_all, me * cond_cols, cond_cols, axis=1)
    grad_w_cond, grad_b_cond = _cond_bwd(c_all, dmod_all, dmod_cols)

    small_ws = [g_mix_pre, g_mix_post, g_ffn_pre, g_ffn_post, pool_scale, w_pool.reshape(-1, POOL_GROUP_DIM)]
    small_ms = [m_g_mix_pre, m_g_mix_post, m_g_ffn_pre, m_g_ffn_post, m_pool_scale, m_w_pool.reshape(-1, POOL_GROUP_DIM)]
    small_vs = [v_g_mix_pre, v_g_mix_post, v_g_ffn_pre, v_g_ffn_post, v_pool_scale, v_w_pool.reshape(-1, POOL_GROUP_DIM)]
    small_gparts = [small_g[:, 0:1, :], small_g[:, 1:2, :], small_g[:, 2:3, :], small_g[:, 3:4, :],
                    small_g[:, 4:5, :pool_scale.shape[1]],
                    small_g[:, 8:8 + n_gw, :].reshape(N_DEV, -1, POOL_GROUP_DIM)]
    so = _adamw_small(small_ws, small_gparts, small_ms, small_vs, "adamw_small")
    ns = len(small_ws)
    sg, sdl, sm, sv = so[:ns], so[ns:2 * ns], so[2 * ns:3 * ns], so[3 * ns:]
    pool_shape = w_pool.shape
    fix = lambda lst: [lst[0], lst[1], lst[2], lst[3], lst[4], lst[5].reshape(pool_shape)]
    sg, sdl, sm, sv = fix(sg), fix(sdl), fix(sm), fix(sv)

    def big(w, g, m, v, rows, name):
        dl, nm, nv = _adamw(w[0], g, m[0], v[0], rows, name)
        return g[None], dl[None], nm[None], nv[None]

    o_cond = big(w_cond, grad_w_cond, m_w_cond, v_w_cond, 256, "adamw_w_cond")
    o_bcond = _adamw(b_cond, grad_b_cond, m_b_cond, v_b_cond, 1, "adamw_b_cond")
    o_bcond = (grad_b_cond,) + tuple(o_bcond)
    o_in = big(w_in, grad_w_in, m_w_in, v_w_in, 256, "adamw_w_in")
    o_out = big(w_out, grad_w_out, m_w_out, v_w_out, out_rows, "adamw_w_out")
    o_gate = big(w_gate, grad_w_gate, m_w_gate, v_w_gate, 256, "adamw_w_gate")
    o_up = big(w_up, grad_w_up, m_w_up, v_w_up, 256, "adamw_w_up")
    o_down = big(w_down, grad_w_down, m_w_down, v_w_down, ff_rows, "adamw_w_down")

    def pick(k):
        small_k = [sg, sdl, sm, sv][k]
        return [o_cond[k], o_bcond[k], small_k[0], small_k[1], o_in[k], small_k[5], small_k[4], o_out[k],
                small_k[2], small_k[3], o_gate[k], o_up[k], o_down[k]]

    return (loss, grad_x.reshape(n_seq, seq, d), *pick(0), *pick(1), *pick(2), *pick(3))
```

```python
import functools
import math

import jax
import jax.numpy as jnp
from jax import lax
from jax.experimental import pallas as pl
from jax.experimental.pallas import tpu as pltpu

F32 = jnp.float32
BF16 = jnp.bfloat16
MESH = pl.DeviceIdType.MESH

N_DEV = 8
HEAD_DIM = 64
LANES = 128
POOL_WINDOWS = (2, 4, 8, 16)
POOL_GROUP_DIM = 128
N_MOD = 6
EPS = 1e-6
ATT_TILE = 256
ATT_PAIRS = 2
VMEM_LIMIT = 56 * 1024 * 1024

ADAM_LR = 0.001
ADAM_B1 = 0.9
ADAM_B2 = 0.999
ADAM_EPS = 1e-08
ADAM_WD = 0.01
ADAM_STEP = 10


def _params(**kw):
    return pltpu.CompilerParams(vmem_limit_bytes=VMEM_LIMIT, **kw)


def _dot_nn(a, b):
    return jnp.dot(a, b, preferred_element_type=F32)


def _dot_nt(a, b):
    return lax.dot_general(a, b, (((1,), (1,)), ((), ())), preferred_element_type=F32)


def _dot_tn(a, b):
    return lax.dot_general(a, b, (((0,), (0,)), ((), ())), preferred_element_type=F32)


def _mesh_pos():
    return lax.axis_index("x"), lax.axis_index("y"), lax.axis_index("c")


def _ag_phases(dests, src, outs, send_sems, recv_sems, local_sems):
    n = len(src)
    x, y, c = _mesh_pos()
    me, sibling = (x, y, c), (x, y, 1 - c)
    chips = [(1 - x, y), (x, 1 - y), (1 - x, 1 - y)]

    def slot(i, dev):
        oi, prefix = dests[i]
        px, py, pc = dev
        return outs[oi].at[prefix + (4 * px + 2 * py + pc,)]

    def copy(i, k, block, to, from_src=False):
        return pltpu.make_async_remote_copy(
            src_ref=src[i] if from_src else slot(i, block), dst_ref=slot(i, block),
            send_sem=send_sems.at[i, k], recv_sem=recv_sems.at[i, k],
            device_id=to, device_id_type=MESH)

    def mine(i):
        return pltpu.make_async_copy(src[i], slot(i, me), local_sems.at[i])

    def first(i):
        return [copy(i, 0, me, sibling, from_src=True)] + [
            copy(i, 1 + j, me, (*chip, c), from_src=True) for j, chip in enumerate(chips)]

    def passed(i, j):
        return copy(i, 4 + j, (*chips[j], c), sibling)

    def start():
        for i in range(n):
            mine(i).start()
        for i in range(n):
            for cp in first(i):
                cp.start()

    def forward():
        for j, chip in enumerate(chips):
            for i in range(n):
                copy(i, 1 + j, (*chip, c), me).wait_recv()
                passed(i, j).start()

    def finish():
        for i in range(n):
            copy(i, 0, sibling, me).wait_recv()
            for j, chip in enumerate(chips):
                copy(i, 4 + j, (*chip, 1 - c), me).wait_recv()
        for i in range(n):
            for cp in first(i) + [passed(i, j) for j in range(3)]:
                cp.wait_send()
            mine(i).wait()

    return start, forward, finish


def _ag_scratch(n):
    return [pltpu.SemaphoreType.DMA((n, 7)), pltpu.SemaphoreType.DMA((n, 7)), pltpu.SemaphoreType.DMA((n,))]


def _all_gather(srcs, out_shapes, dests, name):
    n = len(srcs)

    def body(*refs):
        src = refs[:n]
        outs = refs[n:n + len(out_shapes)]
        start, forward, finish = _ag_phases(dests, src, outs, *refs[n + len(out_shapes):])
        start()
        forward()
        finish()

    any_spec = pl.BlockSpec(memory_space=pl.ANY)
    return pl.pallas_call(
        body, name=name,
        out_shape=tuple(out_shapes),
        in_specs=[any_spec] * n,
        out_specs=tuple([any_spec] * len(out_shapes)),
        scratch_shapes=_ag_scratch(n),
    )(*srcs)


def _rs_phases(shapes, src, dst, send_sems, recv_sems):
    x, y, c = _mesh_pos()

    def copies():
        out = []
        n = 0
        for i, shp in enumerate(shapes):
            for m in range(shp[0]):
                for k in range(1, N_DEV):
                    px, py, pc = x ^ (k >> 2), y ^ ((k >> 1) & 1), c ^ (k & 1)
                    out.append(pltpu.make_async_remote_copy(
                        src_ref=src[i].at[m, 4 * px + 2 * py + pc], dst_ref=dst[i].at[m, k - 1],
                        send_sem=send_sems.at[n], recv_sem=recv_sems.at[n],
                        device_id=(px, py, pc), device_id_type=MESH))
                    n += 1
        return out

    def start():
        for cp in copies():
            cp.start()

    def finish():
        for cp in copies():
            cp.wait_send()
        for cp in copies():
            cp.wait_recv()

    return start, finish


def _rs_out(sends):
    return [jax.ShapeDtypeStruct((s.shape[0], N_DEV - 1) + s.shape[2:], s.dtype) for s in sends]


def _rs_scratch(sends):
    total = sum((N_DEV - 1) * s.shape[0] for s in sends)
    return [pltpu.SemaphoreType.DMA((total,)), pltpu.SemaphoreType.DMA((total,))]


def _rs_final(mine, recv, name):
    m_n, _, r, cdim = mine.shape
    x, y, c = _mesh_pos()
    me = jnp.reshape(4 * x + 2 * y + c, (1,)).astype(jnp.int32)

    def body(me_ref, p_ref, r_ref, o_ref):
        del me_ref
        s = p_ref[...]
        for k in range(N_DEV - 1):
            s = s + r_ref[k].astype(F32)
        o_ref[...] = s

    return pl.pallas_call(
        body, name=name, out_shape=jax.ShapeDtypeStruct((m_n, r, cdim), F32),
        grid_spec=pltpu.PrefetchScalarGridSpec(
            num_scalar_prefetch=1, grid=(m_n,),
            in_specs=[pl.BlockSpec((None, None, r, cdim), lambda m, s: (m, s[0], 0, 0)),
                      pl.BlockSpec((None, N_DEV - 1, r, cdim), lambda m, s: (m, 0, 0, 0))],
            out_specs=pl.BlockSpec((None, r, cdim), lambda m, s: (m, 0, 0))),
        compiler_params=_params(),
    )(me, mine, recv)


def _matmul(a, b, mode, out_dtype, tm, tn, tk, name, bf16_copy=False, rs_sends=()):
    ga = a.shape[0] if a.ndim == 3 else None
    gb = b.shape[0] if b.ndim == 3 else None
    a2, b2 = a.shape[-2:], b.shape[-2:]
    if mode == "nn":
        (m, k), n = a2, b2[1]
    elif mode == "nt":
        (m, k), n = a2, b2[0]
    else:
        (k, m), n = a2, b2[1]
    assert m % tm == 0 and n % tn == 0 and k % tk == 0, (name, m, n, k)
    nk = k // tk
    g_n = ga or 1
    batch_out = mode == "tn" and ga is not None
    n_red = nk if batch_out else nk * g_n
    dot = {"nn": _dot_nn, "nt": _dot_nt, "tn": _dot_tn}[mode]
    acc_in_out = out_dtype == F32

    n_rs = len(rs_sends)
    rs_shapes = [r.shape for r in rs_sends]
    n_out = 2 if bf16_copy else 1
    assert not bf16_copy or acc_in_out

    def body(a_ref, b_ref, *rest):
        rs_src, rest = rest[:n_rs], rest[n_rs:]
        o_ref = rest[0]
        copy_ref = rest[1] if bf16_copy else None
        rs_dst, scratch = rest[n_out:n_out + n_rs], rest[n_out + n_rs:]
        if n_rs:
            rs_start, rs_finish = _rs_phases(rs_shapes, rs_src, rs_dst, *scratch[-2:])
            first = functools.reduce(jnp.logical_and, [pl.program_id(ax) == 0 for ax in range(4)])
            last = functools.reduce(jnp.logical_and, [pl.program_id(ax) == grid[ax] - 1 for ax in range(4)])
            pl.when(first)(rs_start)
        p = dot(a_ref[...], b_ref[...])
        kk = pl.program_id(3) if batch_out else pl.program_id(2) * nk + pl.program_id(3)
        if n_red == 1:
            o_ref[...] = p.astype(out_dtype)
            if bf16_copy:
                copy_ref[...] = p.astype(BF16)
        else:
            acc = o_ref if acc_in_out else scratch[0]

            @pl.when(kk == 0)
            def _():
                acc[...] = p

            @pl.when(kk > 0)
            def _():
                acc[...] += p

            @pl.when(kk == n_red - 1)
            def _():
                if not acc_in_out:
                    o_ref[...] = acc[...].astype(out_dtype)
                if bf16_copy:
                    copy_ref[...] = acc[...].astype(BF16)

        if n_rs:
            pl.when(last)(rs_finish)

    def order(ids):
        return ids if batch_out else (ids[2], ids[0], ids[1], ids[3])

    def a_idx(*ids):
        g, i, j, kq = order(ids)
        blk = {"nn": (i, kq), "nt": (i, kq), "tn": (kq, i)}[mode]
        return (g,) + blk if ga is not None else blk

    def b_idx(*ids):
        g, i, j, kq = order(ids)
        blk = {"nn": (kq, j), "nt": (j, kq), "tn": (kq, j)}[mode]
        return (g,) + blk if gb is not None else blk

    def o_idx(*ids):
        g, i, j, kq = order(ids)
        return (g, i, j) if batch_out else (i, j)

    a_blk = {"nn": (tm, tk), "nt": (tm, tk), "tn": (tk, tm)}[mode]
    b_blk = {"nn": (tk, tn), "nt": (tn, tk), "tn": (tk, tn)}[mode]
    if ga is not None:
        a_blk = (None,) + a_blk
    if gb is not None:
        b_blk = (None,) + b_blk
    if batch_out:
        out_shape = jax.ShapeDtypeStruct((g_n, m, n), out_dtype)
        o_blk = (None, tm, tn)
        grid = (g_n, m // tm, n // tn, nk)
    else:
        out_shape = jax.ShapeDtypeStruct((m, n), out_dtype)
        o_blk = (tm, tn)
        grid = (m // tm, n // tn, g_n, nk)
    scratch = [] if (acc_in_out or n_red == 1) else [pltpu.VMEM((tm, tn), F32)]
    any_spec = pl.BlockSpec(memory_space=pl.ANY)
    out_shapes = [out_shape] + ([jax.ShapeDtypeStruct(out_shape.shape, BF16)] if bf16_copy else [])
    res = pl.pallas_call(
        body, name=name, out_shape=tuple(out_shapes + _rs_out(rs_sends)), grid=grid,
        in_specs=[pl.BlockSpec(a_blk, a_idx), pl.BlockSpec(b_blk, b_idx)] + [any_spec] * n_rs,
        out_specs=tuple([pl.BlockSpec(o_blk, o_idx)] * n_out + [any_spec] * n_rs),
        scratch_shapes=scratch + (_rs_scratch(rs_sends) if n_rs else []), compiler_params=_params(),
    )(a, b, *rs_sends)
    return res if len(res) > 1 else res[0]


EW_TILE = 256
ROW_TILE = 512


def _rms(v):
    return lax.rsqrt(jnp.mean(v * v, axis=-1, keepdims=True) + EPS)


def _rms_bwd(dhat, vh, r):
    return r * (dhat - vh * jnp.mean(dhat * vh, axis=-1, keepdims=True))


def _tok_spec(tm, d):
    return pl.BlockSpec((tm, d), lambda i: (i, 0))


def _vec_spec(d):
    return pl.BlockSpec((1, d), lambda i: (0, 0))


def _mod_spec(tiles_per_seq, d):
    return pl.BlockSpec((None, N_MOD, d), lambda i: (i // tiles_per_seq, 0, 0))


def _seq_acc_spec(tiles_per_seq, d):
    return pl.BlockSpec((None, 1, d), lambda i: (i // tiles_per_seq, 0, 0))


def _acc(ref, val, first):
    @pl.when(first)
    def _():
        ref[...] = val

    @pl.when(jnp.logical_not(first))
    def _():
        ref[...] += val


def _colsum(v):
    return jnp.sum(v, axis=0, keepdims=True)


def _pre_mix(x2, g_pre, mod, seq):
    t, d = x2.shape
    tm = EW_TILE

    def body(x_ref, g_ref, mod_ref, h_ref):
        xv = x_ref[...]
        n = xv * _rms(xv) * g_ref[...]
        h_ref[...] = (n * (1.0 + mod_ref[1:2, :]) + mod_ref[0:1, :]).astype(BF16)

    return pl.pallas_call(
        body, name="pre_mix", out_shape=jax.ShapeDtypeStruct((t, d), BF16), grid=(t // tm,),
        in_specs=[_tok_spec(tm, d), _vec_spec(d), _mod_spec(seq // tm, d)],
        out_specs=_tok_spec(tm, d), compiler_params=_params(),
    )(x2, g_pre, mod)


def _matmul_rows(a, b, tm, tk, seq, name, epilogue, ep_in, ep_in_kinds, ep_out, ep_out_kinds, rs_sends=()):
    ga = a.shape[0] if a.ndim == 3 else None
    (m, k), n = a.shape[-2:], b.shape[-1]
    g_n = ga or 1
    nk = k // tk
    n_red = g_n * nk
    tps = seq // tm
    grid = (m // tm, g_n, nk)
    n_rs = len(rs_sends)
    rs_shapes = [r.shape for r in rs_sends]
    n_in, n_out = len(ep_in), len(ep_out)

    def spec(kind):
        return {"tok": pl.BlockSpec((tm, n), lambda i, g, kq: (i, 0)),
                "vec": pl.BlockSpec((1, n), lambda i, g, kq: (0, 0)),
                "mod": pl.BlockSpec((None, N_MOD, n), lambda i, g, kq: (i // tps, 0, 0)),
                "seq": pl.BlockSpec((None, 1, n), lambda i, g, kq: (i // tps, 0, 0)),
                "loss": pl.BlockSpec((1, LANES), lambda i, g, kq: (0, 0))}[kind]

    def body(a_ref, b_ref, *rest):
        in_refs, rest = rest[:n_in], rest[n_in:]
        rs_src, rest = rest[:n_rs], rest[n_rs:]
        out_refs, rest = rest[:n_out], rest[n_out:]
        rs_dst, rest = rest[:n_rs], rest[n_rs:]
        acc = rest[0]
        i, kk = pl.program_id(0), pl.program_id(1) * nk + pl.program_id(2)
        if n_rs:
            rs_start, rs_finish = _rs_phases(rs_shapes, rs_src, rs_dst, *rest[1:])
            pl.when(jnp.logical_and(i == 0, kk == 0))(rs_start)
        p = _dot_nn(a_ref[...], b_ref[...])
        if n_red == 1:
            epilogue(p, i, tps, in_refs, out_refs)
        else:
            @pl.when(kk == 0)
            def _():
                acc[...] = p

            @pl.when(jnp.logical_and(kk > 0, kk < n_red - 1))
            def _():
                acc[...] += p

            @pl.when(kk == n_red - 1)
            def _():
                epilogue(acc[...] + p, i, tps, in_refs, out_refs)

        if n_rs:
            pl.when(jnp.logical_and(i == grid[0] - 1, kk == n_red - 1))(rs_finish)

    a_blk = (tm, tk) if ga is None else (None, tm, tk)
    b_blk = (tk, n) if ga is None else (None, tk, n)
    a_idx = (lambda i, g, kq: (i, kq)) if ga is None else (lambda i, g, kq: (g, i, kq))
    b_idx = (lambda i, g, kq: (kq, 0)) if ga is None else (lambda i, g, kq: (g, kq, 0))
    any_spec = pl.BlockSpec(memory_space=pl.ANY)
    res = pl.pallas_call(
        body, name=name, grid=grid, out_shape=tuple(list(ep_out) + _rs_out(rs_sends)),
        in_specs=[pl.BlockSpec(a_blk, a_idx), pl.BlockSpec(b_blk, b_idx)] + [spec(kd) for kd in ep_in_kinds]
        + [any_spec] * n_rs,
        out_specs=tuple([spec(kd) for kd in ep_out_kinds] + [any_spec] * n_rs),
        scratch_shapes=[pltpu.VMEM((tm, n), F32)] + (_rs_scratch(rs_sends) if n_rs else []),
        compiler_params=_params(),
    )(a, b, *ep_in, *rs_sends)
    return res


def _mid_epilogue(mv, i, tps, in_refs, out_refs):
    x_ref, gpost_ref, gpre_ref, mod_ref = in_refs
    mix_ref, x1_ref, h2_ref = out_refs
    mix_ref[...] = mv
    x1 = x_ref[...] + mod_ref[2:3, :] * (mv * _rms(mv) * gpost_ref[...])
    x1_ref[...] = x1
    n = x1 * _rms(x1) * gpre_ref[...]
    h2_ref[...] = (n * (1.0 + mod_ref[4:5, :]) + mod_ref[3:4, :]).astype(BF16)


def _post_epilogue(fv, i, tps, in_refs, out_refs):
    x1_ref, tgt_ref, g_ref, mod_ref = in_refs
    loss_ref, dy_ref, df_ref, dgate_ref, gg_ref = out_refs
    d = fv.shape[1]
    r = _rms(fv)
    fh = fv * r
    nf = fh * g_ref[...]
    gate = mod_ref[5:6, :]
    err = x1_ref[...] + gate * nf - tgt_ref[...]
    _acc(loss_ref, jnp.sum(_colsum(err * err), axis=1, keepdims=True) * jnp.ones((1, LANES), F32), i == 0)
    dy = err * (1.0 / d)
    dy_ref[...] = dy
    _acc(dgate_ref, _colsum(dy * nf), i % tps == 0)
    dn = dy * gate
    _acc(gg_ref, _colsum(dn * fh), i == 0)
    df_ref[...] = _rms_bwd(dn * g_ref[...], fh, r).astype(BF16)


def _bwd_mid_epilogue(dh, i, tps, in_refs, out_refs):
    dy_ref, x1_ref, mix_ref, gpre_ref, gpost_ref, mod_ref = in_refs
    dx1_ref, dmix_ref, dshift_ref, dscale_ref, dgate_ref, ggpre_ref, ggpost_ref = out_refs
    seq_first = i % tps == 0
    x1 = x1_ref[...]
    r = _rms(x1)
    xh = x1 * r
    gpre = gpre_ref[...]
    _acc(dshift_ref, _colsum(dh), seq_first)
    _acc(dscale_ref, _colsum(dh * xh * gpre), seq_first)
    dn = dh * (1.0 + mod_ref[4:5, :])
    _acc(ggpre_ref, _colsum(dn * xh), i == 0)
    dx1 = dy_ref[...] + _rms_bwd(dn * gpre, xh, r)
    dx1_ref[...] = dx1
    mv = mix_ref[...]
    rm = _rms(mv)
    mh = mv * rm
    gpost = gpost_ref[...]
    _acc(dgate_ref, _colsum(dx1 * mh * gpost), seq_first)
    dnm = dx1 * mod_ref[2:3, :]
    _acc(ggpost_ref, _colsum(dnm * mh), i == 0)
    dmix_ref[...] = _rms_bwd(dnm * gpost, mh, rm).astype(BF16)


def _bwd_pre_epilogue(dh, i, tps, in_refs, out_refs):
    dx1_ref, x_ref, g_ref, mod_ref = in_refs
    gx_ref, dshift_ref, dscale_ref, gg_ref = out_refs
    seq_first = i % tps == 0
    xv = x_ref[...]
    r = _rms(xv)
    xh = xv * r
    g = g_ref[...]
    _acc(dshift_ref, _colsum(dh), seq_first)
    _acc(dscale_ref, _colsum(dh * xh * g), seq_first)
    dn = dh * (1.0 + mod_ref[1:2, :])
    _acc(gg_ref, _colsum(dn * xh), i == 0)
    gx_ref[...] = dx1_ref[...] + _rms_bwd(dn * g, xh, r)


def _ffn_up(h2, wgu, tm, tn):
    t, d = h2.shape
    f = wgu.shape[1]

    def body(h_ref, w_ref, gu_ref, act_ref):
        h = h_ref[...]
        g = _dot_nt(h, w_ref[0])
        u = _dot_nt(h, w_ref[1])
        gu_ref[0] = g.astype(BF16)
        gu_ref[1] = u.astype(BF16)
        act_ref[...] = (g * jax.nn.sigmoid(g) * u).astype(BF16)

    return pl.pallas_call(
        body, name="ffn_up", grid=(t // tm, f // tn),
        out_shape=(jax.ShapeDtypeStruct((2, t, f), BF16), jax.ShapeDtypeStruct((t, f), BF16)),
        in_specs=[pl.BlockSpec((tm, d), lambda i, j: (i, 0)), pl.BlockSpec((2, tn, d), lambda i, j: (0, j, 0))],
        out_specs=(pl.BlockSpec((2, tm, tn), lambda i, j: (0, i, j)), pl.BlockSpec((tm, tn), lambda i, j: (i, j))),
        compiler_params=_params(),
    )(h2, wgu)


def _ffn_act_bwd(df, wd, gu, tm, tn):
    t, d = df.shape
    f = wd.shape[0]

    def body(df_ref, w_ref, gu_ref, dgu_ref):
        da = _dot_nt(df_ref[...], w_ref[...])
        g = gu_ref[0].astype(F32)
        u = gu_ref[1].astype(F32)
        s = jax.nn.sigmoid(g)
        silu = g * s
        dgu_ref[0] = (da * u * (s + silu * (1.0 - s))).astype(BF16)
        dgu_ref[1] = (da * silu).astype(BF16)

    return pl.pallas_call(
        body, name="ffn_act_bwd", grid=(t // tm, f // tn),
        out_shape=jax.ShapeDtypeStruct((2, t, f), BF16),
        in_specs=[pl.BlockSpec((tm, d), lambda i, j: (i, 0)), pl.BlockSpec((tn, d), lambda i, j: (j, 0)),
                  pl.BlockSpec((2, tm, tn), lambda i, j: (0, i, j))],
        out_specs=pl.BlockSpec((2, tm, tn), lambda i, j: (0, i, j)),
        compiler_params=_params(),
    )(df, wd, gu)


SIGN_BIT = 0x80000000
Q_SCALE = 1.0 / math.sqrt(HEAD_DIM)


def _split_dot(v, tri2):
    hi = v.astype(BF16)
    lo = (v - hi.astype(F32)).astype(BF16)
    return _dot_nn(jnp.concatenate([hi, lo], axis=1), tri2)


def _sb_tile(qs, k2, mask, ntri2, cur):
    z = _dot_nt(qs, k2)
    neg_abs = lax.bitcast_convert_type(lax.bitcast_convert_type(z, jnp.uint32) | jnp.uint32(SIGN_BIT), F32)
    sp = jnp.maximum(z, 0.0) + jnp.log(1.0 + jnp.exp(neg_abs))
    if mask is not None:
        sp = jnp.where(mask, sp, 0.0)
    w = jnp.exp(z + _split_dot(sp, ntri2) + cur)
    if mask is not None:
        w = jnp.where(mask, w, 0.0)
    return z, sp, w


def _softplus(z):
    neg_abs = lax.bitcast_convert_type(lax.bitcast_convert_type(z, jnp.uint32) | jnp.uint32(SIGN_BIT), F32)
    return jnp.maximum(z, 0.0) + jnp.log(1.0 + jnp.exp(neg_abs))


def _hi_lo(v):
    hi = v.astype(BF16)
    return jnp.concatenate([hi, (v - hi.astype(F32)).astype(BF16)], axis=1)


def _emit_skewed(chains):
    for t in range(max(len(ch) for ch in chains) + len(chains) - 1):
        for c, ch in enumerate(chains):
            if 0 <= t - c < len(ch):
                ch[t - c]()


def _fwd_chain(blk, qs, k_ref, v_ref, c0, kb, cols, mask, ntri2, lane, tq):
    st = {}

    def scores():
        st["z"] = _dot_nt(qs, k_ref[pl.ds(c0, tq), cols])

    def soft():
        sp = _softplus(st["z"])
        if mask is not None:
            sp = jnp.where(mask, sp, 0.0)
        st["parts"] = _hi_lo(sp)
        st["cur"] = blk["cur"]
        blk["cm"] = jnp.where(lane == kb, blk["cur"], blk["cm"])
        blk["cur"] = blk["cur"] - jnp.sum(sp, axis=1, keepdims=True)

    def sums():
        st["s"] = _dot_nn(st["parts"], ntri2)

    def weights():
        w = jnp.exp(st["z"] + st["s"] + st["cur"])
        if mask is not None:
            w = jnp.where(mask, w, 0.0)
        st["w"] = w.astype(BF16)

    def out():
        p = _dot_nn(st["w"], v_ref[pl.ds(c0, tq), cols])
        blk["pv"] = p if blk["pv"] is None else blk["pv"] + p

    return [scores, soft, sums, weights, out]


def _bwd_chain(blk, qs, dos, cs, k_ref, v_ref, dk_ref, dv_ref, c0, kb, cols, mask, ntri2, tri_i, lane, tq):
    st = {}

    def scores():
        st["z"] = _dot_nt(qs, k_ref[pl.ds(c0, tq), cols])
        st["dw"] = _dot_nt(dos, v_ref[pl.ds(c0, tq), cols])

    def soft():
        sp = _softplus(st["z"])
        if mask is not None:
            sp = jnp.where(mask, sp, 0.0)
        st["sp"] = sp
        st["parts"] = _hi_lo(sp)
        st["cur"] = jnp.sum(jnp.where(lane == kb, cs, 0.0), axis=1, keepdims=True)

    def sums():
        st["s"] = _dot_nn(st["parts"], ntri2)

    def weights():
        w = jnp.exp(st["z"] + st["s"] + st["cur"])
        if mask is not None:
            w = jnp.where(mask, w, 0.0)
        ee = w * st["dw"]
        st["w"], st["ee"], st["ec"] = w.astype(BF16), ee, blk["ec"]
        blk["ec"] = blk["ec"] + jnp.sum(ee, axis=1, keepdims=True)

    def prefix():
        st["einc"] = _dot_nn(st["ee"].astype(BF16), tri_i)

    def dz():
        v = st["ee"] - jnp.exp(st["z"] - st["sp"]) * (st["einc"] + st["ec"])
        if mask is not None:
            v = jnp.where(mask, v, 0.0)
        st["dz"] = v.astype(BF16)

    def grads():
        p = _dot_nn(st["dz"], k_ref[pl.ds(c0, tq), cols])
        blk["dq"] = p if blk["dq"] is None else blk["dq"] + p
        dk_ref[pl.ds(c0, tq), :] += _dot_tn(st["dz"], qs)
        dv_ref[pl.ds(c0, tq), :] += _dot_tn(st["w"], dos)

    return [scores, soft, sums, weights, prefix, dz, grads]


def _stack_heads(v, lane, scale=None):
    if scale is not None:
        v = v * jnp.asarray(scale, v.dtype)
    zero = jnp.zeros_like(v)
    return jnp.concatenate([jnp.where(lane < HEAD_DIM, v, zero), jnp.where(lane >= HEAD_DIM, v, zero)], axis=0)


def _diag_mask(tq):
    row = lax.broadcasted_iota(jnp.int32, (2 * tq, tq), 0)
    col = lax.broadcasted_iota(jnp.int32, (2 * tq, tq), 1)
    return col < jnp.where(row >= tq, row - tq, row)


def _attn_fwd(proj, tri_after, n_seq, seq, ag_srcs, ag_out_shapes, ag_dests):
    t = proj.shape[0]
    tq = ATT_TILE
    npp = ATT_PAIRS
    n_blk = (proj.shape[1] // 4) // (npp * LANES)
    n_ag, n_ag_out = len(ag_srcs), len(ag_out_shapes)
    n_steps = n_seq * n_blk

    def body(q_ref, k_ref, v_ref, tri_ref, *rest):
        ag_src, rest = rest[:n_ag], rest[n_ag:]
        o_ref, cs_ref = rest[:2]
        ag_out, rest = rest[2:2 + n_ag_out], rest[2 + n_ag_out:]
        oacc, cmat, carry = rest[:3]
        ag_start, ag_forward, ag_finish = _ag_phases(ag_dests, ag_src, ag_out, *rest[3:])
        step = pl.program_id(0) * n_blk + pl.program_id(1)
        pl.when(step == 0)(ag_start)
        pl.when(step == (3 * n_steps) // 4)(ag_forward)
        lane = lax.broadcasted_iota(jnp.int32, (1, LANES), 1)
        ntri2 = tri_ref[...]
        diag = _diag_mask(tq)

        def q_tile(qi, _):
            r0 = pl.multiple_of(qi * tq, tq)
            qs = [_stack_heads(q_ref[pl.ds(r0, tq), pp * LANES:(pp + 1) * LANES], lane, Q_SCALE)
                  for pp in range(npp)]
            carry[...] = jnp.zeros_like(carry)
            cmat[...] = jnp.zeros_like(cmat)
            oacc[...] = jnp.zeros_like(oacc)

            def run_tiles(tiles):
                blocks = [dict(cur=carry[pp], cm=cmat[pp], pv=None) for pp in range(npp)]
                chains = []
                for kb, mask in tiles:
                    c0 = pl.multiple_of(kb * tq, tq)
                    for pp in range(npp):
                        chains.append(_fwd_chain(blocks[pp], qs[pp], k_ref, v_ref, c0, kb,
                                                 slice(pp * LANES, (pp + 1) * LANES), mask, ntri2, lane, tq))
                _emit_skewed(chains)
                for pp in range(npp):
                    oacc[pp] += blocks[pp]["pv"]
                    cmat[pp] = blocks[pp]["cm"]
                    carry[pp] = blocks[pp]["cur"]

            odd = qi % 2

            @pl.when(odd == 0)
            def _():
                run_tiles([(qi, diag)])

            @pl.when(odd == 1)
            def _():
                run_tiles([(qi, diag), (qi - 1, None)])

            def pair(j, _):
                kb = qi - 1 - odd - 2 * j
                run_tiles([(kb, None), (kb - 1, None)])
                return 0

            lax.fori_loop(0, qi // 2, pair, 0)
            for pp in range(npp):
                c_off = 2 * pp * LANES
                cs_ref[pl.ds(r0, tq), c_off:c_off + LANES] = cmat[pp, 0:tq, :]
                cs_ref[pl.ds(r0, tq), c_off + LANES:c_off + 2 * LANES] = cmat[pp, tq:2 * tq, :]
                o_ref[pl.ds(r0, tq), pp * LANES:(pp + 1) * LANES] = jnp.where(
                    lane < HEAD_DIM, oacc[pp, 0:tq, :], oacc[pp, tq:2 * tq, :]).astype(BF16)
            return 0

        lax.fori_loop(0, seq // tq, q_tile, 0)
        pl.when(step == n_steps - 1)(ag_finish)

    wid = npp * LANES
    blk = lambda off: pl.BlockSpec((seq, wid), lambda b, p: (b, off + p))
    any_spec = pl.BlockSpec(memory_space=pl.ANY)
    return pl.pallas_call(
        body, name="attn_fwd", grid=(n_seq, n_blk),
        out_shape=(jax.ShapeDtypeStruct((2, t, n_blk * wid), BF16),
                   jax.ShapeDtypeStruct((t, n_blk * 2 * wid), F32), *ag_out_shapes),
        in_specs=[blk(0), blk(n_blk), blk(2 * n_blk), pl.BlockSpec((2 * tq, tq), lambda b, p: (0, 0))]
        + [any_spec] * n_ag,
        out_specs=(pl.BlockSpec((None, seq, wid), lambda b, p: (0, b, p)),
                   pl.BlockSpec((seq, 2 * wid), lambda b, p: (b, p)), *([any_spec] * n_ag_out)),
        scratch_shapes=[pltpu.VMEM((npp, 2 * tq, LANES), F32), pltpu.VMEM((npp, 2 * tq, LANES), F32),
                        pltpu.VMEM((npp, 2 * tq, 1), F32)] + _ag_scratch(n_ag),
        compiler_params=_params(),
    )(proj, proj, proj, tri_after, *ag_srcs)


def _attn_bwd(proj, dcat, cstats, tri_after, tri_incl, n_seq, seq, rs_sends):
    t = proj.shape[0]
    tq = ATT_TILE
    npp = ATT_PAIRS
    width = proj.shape[1] // 4
    n_blk = width // (npp * LANES)
    n_rs = len(rs_sends)
    rs_shapes = [r.shape for r in rs_sends]
    n_steps = n_seq * n_blk

    def body(q_ref, k_ref, v_ref, do_ref, cs_ref, tria_ref, trii_ref, *rest):
        rs_src, rest = rest[:n_rs], rest[n_rs:]
        out_ref = rest[0]
        rs_dst, rest = rest[1:1 + n_rs], rest[1 + n_rs:]
        dq_acc, dk_acc, dv_acc, ecarry = rest[:4]
        rs_start, rs_finish = _rs_phases(rs_shapes, rs_src, rs_dst, *rest[4:])
        step = pl.program_id(0) * n_blk + pl.program_id(1)
        pl.when(step == 0)(rs_start)
        lane = lax.broadcasted_iota(jnp.int32, (1, LANES), 1)
        ntri2 = tria_ref[...]
        tri_i = trii_ref[...]
        diag = _diag_mask(tq)
        dk_acc[...] = jnp.zeros_like(dk_acc)
        dv_acc[...] = jnp.zeros_like(dv_acc)

        def q_tile(qi, _):
            r0 = pl.multiple_of(qi * tq, tq)
            qs, dos, cs = [], [], []
            for pp in range(npp):
                cols = slice(pp * LANES, (pp + 1) * LANES)
                qs.append(_stack_heads(q_ref[pl.ds(r0, tq), cols], lane, Q_SCALE))
                dos.append(_stack_heads(do_ref[pl.ds(r0, tq), cols], lane))
                c_off = 2 * pp * LANES
                cs.append(jnp.concatenate([cs_ref[pl.ds(r0, tq), c_off:c_off + LANES],
                                           cs_ref[pl.ds(r0, tq), c_off + LANES:c_off + 2 * LANES]], axis=0))
            ecarry[...] = jnp.zeros_like(ecarry)
            dq_acc[...] = jnp.zeros_like(dq_acc)

            def run_tiles(tiles):
                blocks = [dict(ec=ecarry[pp], dq=None) for pp in range(npp)]
                chains = []
                for kb, mask in tiles:
                    c0 = pl.multiple_of(kb * tq, tq)
                    for pp in range(npp):
                        chains.append(_bwd_chain(
                            blocks[pp], qs[pp], dos[pp], cs[pp], k_ref, v_ref, dk_acc.at[pp], dv_acc.at[pp],
                            c0, kb, slice(pp * LANES, (pp + 1) * LANES), mask, ntri2, tri_i, lane, tq))
                _emit_skewed(chains)
                for pp in range(npp):
                    dq_acc[pp] += blocks[pp]["dq"]
                    ecarry[pp] = blocks[pp]["ec"]

            def pair(j, _):
                run_tiles([(2 * j, None), (2 * j + 1, None)])
                return 0

            lax.fori_loop(0, qi // 2, pair, 0)
            odd = qi % 2

            @pl.when(odd == 0)
            def _():
                run_tiles([(qi, diag)])

            @pl.when(odd == 1)
            def _():
                run_tiles([(qi - 1, None), (qi, diag)])

            for pp in range(npp):
                dq = jnp.where(lane < HEAD_DIM, dq_acc[pp, 0:tq, :], dq_acc[pp, tq:2 * tq, :])
                out_ref[0, pl.ds(r0, tq), pp * LANES:(pp + 1) * LANES] = (dq * Q_SCALE).astype(BF16)
            return 0

        lax.fori_loop(0, seq // tq, q_tile, 0)
        for pp in range(npp):
            cols = slice(pp * LANES, (pp + 1) * LANES)
            out_ref[1, :, cols] = dk_acc[pp].astype(BF16)
            out_ref[2, :, cols] = dv_acc[pp].astype(BF16)
        pl.when(step == n_steps - 1)(rs_finish)

    wid = npp * LANES
    blk = lambda off: pl.BlockSpec((seq, wid), lambda b, p: (b, off + p))
    tri_spec = pl.BlockSpec((2 * tq, tq), lambda b, p: (0, 0))
    any_spec = pl.BlockSpec(memory_space=pl.ANY)
    return pl.pallas_call(
        body, name="attn_bwd", grid=(n_seq, n_blk),
        out_shape=(jax.ShapeDtypeStruct((4, t, width), BF16), *_rs_out(rs_sends)),
        in_specs=[blk(0), blk(n_blk), blk(2 * n_blk), pl.BlockSpec((seq, wid), lambda b, p: (b, p)),
                  pl.BlockSpec((seq, 2 * wid), lambda b, p: (b, p)), tri_spec,
                  pl.BlockSpec((tq, tq), lambda b, p: (0, 0))] + [any_spec] * n_rs,
        out_specs=(pl.BlockSpec((3, seq, wid), lambda b, p: (0, b, p)), *([any_spec] * n_rs)),
        scratch_shapes=[pltpu.VMEM((npp, 2 * tq, LANES), F32), pltpu.VMEM((npp, seq, LANES), F32),
                        pltpu.VMEM((npp, seq, LANES), F32), pltpu.VMEM((npp, 2 * tq, 1), F32)]
        + _rs_scratch(rs_sends),
        compiler_params=_params(),
    )(proj, proj, proj, dcat, cstats, tri_after, tri_incl, *rs_sends)


def _window_terms(g, rows):
    win = jnp.where(g == 0, POOL_WINDOWS[0], jnp.where(g == 1, POOL_WINDOWS[1],
                    jnp.where(g == 2, POOL_WINDOWS[2], POOL_WINDOWS[3])))
    cnt = jnp.minimum(rows + 1, win).astype(F32)
    return win, cnt


def _window_sum(v, g, rows, forward):
    s_len = v.shape[0]
    sums = []
    s = v
    for step in range(len(POOL_WINDOWS)):
        sh = 1 << step
        if forward:
            shifted = jnp.where(rows < s_len - sh, pltpu.roll(s, s_len - sh, axis=0), 0.0)
        else:
            shifted = jnp.where(rows >= sh, pltpu.roll(s, sh, axis=0), 0.0)
        s = s + shifted
        sums.append(s)
    return jnp.where(g == 0, sums[0], jnp.where(g == 1, sums[1], jnp.where(g == 2, sums[2], sums[3])))


def _pooled(u, g, rows):
    _, cnt = _window_terms(g, rows)
    return _window_sum(u, g, rows, forward=False) / cnt - u


def _pool_fwd(proj, w_pool, pool_scale, cat, n_seq, seq):
    n_grp = len(POOL_WINDOWS)
    u_off = 3 * (proj.shape[1] // 4) // LANES

    def body(u_ref, w_ref, s_ref, alias_ref, o_ref):
        del alias_ref
        g = pl.program_id(1)
        rows = lax.broadcasted_iota(jnp.int32, (seq, 1), 0)
        pooled = _pooled(u_ref[...].astype(F32), g, rows)
        y = _dot_nn(pooled.astype(BF16), w_ref[...].astype(BF16))
        o_ref[...] = (y * s_ref[...]).astype(BF16)

    return pl.pallas_call(
        body, name="pool_fwd", grid=(n_seq, n_grp),
        out_shape=jax.ShapeDtypeStruct(cat.shape, BF16),
        in_specs=[pl.BlockSpec((seq, LANES), lambda b, g: (b, u_off + g)),
                  pl.BlockSpec((None, POOL_GROUP_DIM, POOL_GROUP_DIM), lambda b, g: (g, 0, 0)),
                  pl.BlockSpec((1, POOL_GROUP_DIM), lambda b, g: (0, g)),
                  pl.BlockSpec(memory_space=pl.ANY)],
        out_specs=pl.BlockSpec((None, seq, LANES), lambda b, g: (1, b, g)),
        input_output_aliases={3: 0},
        compiler_params=_params(),
    )(proj, w_pool, pool_scale, cat)


def _pool_bwd(proj, dcat, w_pool, pool_scale, dqkv, n_seq, seq):
    n_grp = len(POOL_WINDOWS)
    width = proj.shape[1] // 4
    u_off = 3 * width // LANES
    dp_off = width // LANES

    def body(u_ref, dp_ref, w_ref, s_ref, alias_ref, du_ref, gw_ref, gs_ref):
        del alias_ref
        g = pl.program_id(0)
        b = pl.program_id(1)
        rows = lax.broadcasted_iota(jnp.int32, (seq, 1), 0)
        pooled = _pooled(u_ref[...].astype(F32), g, rows)
        pb = pooled.astype(BF16)
        wb = w_ref[...].astype(BF16)
        z = _dot_nn(pb, wb)
        dp = dp_ref[...].astype(F32)
        _acc(gs_ref, _colsum(dp * z), b == 0)
        dys = (dp * s_ref[...]).astype(BF16)
        _acc(gw_ref, _dot_tn(pb, dys), b == 0)
        dpooled = _dot_nt(dys, wb)
        _, cnt = _window_terms(g, rows)
        du = _window_sum(dpooled / cnt, g, rows, forward=True) - dpooled
        du_ref[...] = du.astype(BF16)

    t = proj.shape[0]
    return pl.pallas_call(
        body, name="pool_bwd", grid=(n_grp, n_seq),
        out_shape=(jax.ShapeDtypeStruct(dqkv.shape, BF16),
                   jax.ShapeDtypeStruct((n_grp, POOL_GROUP_DIM, POOL_GROUP_DIM), F32),
                   jax.ShapeDtypeStruct((1, n_grp * POOL_GROUP_DIM), F32)),
        in_specs=[pl.BlockSpec((seq, LANES), lambda g, b: (b, u_off + g)),
                  pl.BlockSpec((seq, LANES), lambda g, b: (b, dp_off + g)),
                  pl.BlockSpec((None, POOL_GROUP_DIM, POOL_GROUP_DIM), lambda g, b: (g, 0, 0)),
                  pl.BlockSpec((1, POOL_GROUP_DIM), lambda g, b: (0, g)),
                  pl.BlockSpec(memory_space=pl.ANY)],
        out_specs=(pl.BlockSpec((None, seq, LANES), lambda g, b: (3, b, g)),
                   pl.BlockSpec((None, POOL_GROUP_DIM, POOL_GROUP_DIM), lambda g, b: (g, 0, 0)),
                   pl.BlockSpec((1, POOL_GROUP_DIM), lambda g, b: (0, g))),
        input_output_aliases={4: 0},
        compiler_params=_params(),
    )(proj, dcat, w_pool, pool_scale, dqkv)


def _cond_fwd(c_all, w_cond, b_cols):
    n, _ = c_all.shape
    cols = w_cond.shape[1]

    def body(c_ref, w_ref, b_ref, o_ref):
        cv = c_ref[...]
        a = cv * jax.nn.sigmoid(cv)
        o_ref[...] = jnp.dot(a, w_ref[...], preferred_element_type=F32,
                             precision=lax.Precision.HIGHEST) + b_ref[...]

    return pl.pallas_call(
        body, name="cond_fwd", out_shape=jax.ShapeDtypeStruct((n, cols), F32),
        compiler_params=_params(),
    )(c_all, w_cond, b_cols)


def _cond_bwd(c_all, dmod_all, dmod_cols):
    n, d = c_all.shape
    cols = dmod_cols.shape[1]

    def body(c_ref, dm_ref, dmc_ref, gw_ref, gb_ref):
        cv = c_ref[...]
        a = cv * jax.nn.sigmoid(cv)
        gw_ref[...] = lax.dot_general(a, dmc_ref[...], (((0,), (0,)), ((), ())),
                                      preferred_element_type=F32, precision=lax.Precision.HIGHEST)
        gb_ref[...] = _colsum(dm_ref[...])

    return pl.pallas_call(
        body, name="cond_bwd",
        out_shape=(jax.ShapeDtypeStruct((d, cols), F32), jax.ShapeDtypeStruct((1, dmod_all.shape[1]), F32)),
        compiler_params=_params(),
    )(c_all, dmod_all, dmod_cols)


def _adamw_math(w, g, m, v):
    m = ADAM_B1 * m + (1.0 - ADAM_B1) * g
    v = ADAM_B2 * v + (1.0 - ADAM_B2) * (g * g)
    m_hat = m / (1.0 - ADAM_B1 ** ADAM_STEP)
    v_hat = v / (1.0 - ADAM_B2 ** ADAM_STEP)
    delta = -ADAM_LR * (m_hat / (jnp.sqrt(v_hat) + ADAM_EPS) + ADAM_WD * w)
    return delta, m, v


def _adamw(w, g, m, v, rows, name):
    r, cdim = w.shape

    def body(w_ref, g_ref, m_ref, v_ref, d_ref, nm_ref, nv_ref):
        d_ref[...], nm_ref[...], nv_ref[...] = _adamw_math(w_ref[...], g_ref[...], m_ref[...], v_ref[...])

    spec = pl.BlockSpec((rows, cdim), lambda i: (i, 0))
    sds = jax.ShapeDtypeStruct((r, cdim), F32)
    return pl.pallas_call(
        body, name=name, grid=(r // rows,), out_shape=(sds, sds, sds),
        in_specs=[spec] * 4, out_specs=(spec, spec, spec), compiler_params=_params(),
    )(w, g, m, v)


def _adamw_small(ws, gparts, ms, vs, name):
    n = len(ws)

    def body(*refs):
        w_r, g_r, m_r, v_r = refs[:n], refs[n:2 * n], refs[2 * n:3 * n], refs[3 * n:4 * n]
        outs = refs[4 * n:]
        for i in range(n):
            g = g_r[i][0]
            for dev in range(1, g_r[i].shape[0]):
                g = g + g_r[i][dev]
            delta, m, v = _adamw_math(w_r[i][...], g, m_r[i][...], v_r[i][...])
            outs[i][...] = g
            outs[n + i][...] = delta
            outs[2 * n + i][...] = m
            outs[3 * n + i][...] = v

    sds = [jax.ShapeDtypeStruct(w.shape, F32) for w in ws]
    return pl.pallas_call(
        body, name=name, out_shape=tuple(sds * 4), compiler_params=_params(),
    )(*ws, *gparts, *ms, *vs)


def kernel(x, c, w_cond, b_cond, g_mix_pre, g_mix_post, w_in, w_pool, pool_scale, w_out, g_ffn_pre, g_ffn_post, w_gate, w_up, w_down, loss_target, m_w_cond, m_b_cond, m_g_mix_pre, m_g_mix_post, m_w_in, m_w_pool, m_pool_scale, m_w_out, m_g_ffn_pre, m_g_ffn_post, m_w_gate, m_w_up, m_w_down, v_w_cond, v_b_cond, v_g_mix_pre, v_g_mix_post, v_w_in, v_w_pool, v_pool_scale, v_w_out, v_g_ffn_pre, v_g_ffn_post, v_w_gate, v_w_up, v_w_down):
    n_seq, seq, d = x.shape
    t = n_seq * seq
    xi, yi, ci = _mesh_pos()
    me = 4 * xi + 2 * yi + ci
    x2 = x.reshape(t, d)
    tgt2 = loss_target.reshape(t, d)
    in_rows = w_in.shape[2]
    out_rows = w_out.shape[1]
    ff_rows = w_gate.shape[2]
    ff = N_DEV * ff_rows
    cond_cols = w_cond.shape[2]

    win_t = w_in[0].T.astype(BF16)
    wout_s = w_out[0].astype(BF16)
    wg_t = w_gate[0].T.astype(BF16)
    wu_t = w_up[0].T.astype(BF16)
    wd_s = w_down[0].astype(BF16)
    c_all, win_g = _all_gather(
        [c, win_t],
        [jax.ShapeDtypeStruct((N_DEV, n_seq, d), F32), jax.ShapeDtypeStruct((N_DEV, in_rows, d), BF16)],
        [(0, ()), (1, ())], "ag_c_win")
    c_all = c_all.reshape(N_DEV * n_seq, d)
    win_full = win_g.reshape(N_DEV * in_rows, d)

    b_cols = lax.dynamic_slice_in_dim(b_cond, me * cond_cols, cond_cols, axis=1)
    mod_cols = _cond_fwd(c_all, w_cond[0], b_cols)
    (mod_g,) = _all_gather([mod_cols], [jax.ShapeDtypeStruct((N_DEV,) + mod_cols.shape, F32)], [(0, ())], "ag_mod")
    mod_mine = lax.dynamic_slice_in_dim(mod_g, me * n_seq, n_seq, axis=1)
    mod = jnp.transpose(mod_mine, (1, 0, 2)).reshape(n_seq, N_MOD, d)

    h1 = _pre_mix(x2, g_mix_pre, mod, seq)
    proj = _matmul(h1, win_full, "nt", BF16, 1024, 512, d, "proj")
    tq = ATT_TILE
    ids = jnp.arange(tq)
    tri_after = jnp.tile(-(ids[:, None] >= ids[None, :]).astype(BF16), (2, 1))
    tri_incl = (ids[:, None] <= ids[None, :]).astype(BF16)
    attn, cstats, wout_g, wgu_g, wd_g = _attn_fwd(
        proj, tri_after, n_seq, seq, [wout_s, wg_t, wu_t, wd_s],
        [jax.ShapeDtypeStruct((N_DEV, out_rows, d), BF16), jax.ShapeDtypeStruct((2, N_DEV, ff_rows, d), BF16),
         jax.ShapeDtypeStruct((N_DEV, ff_rows, d), BF16)],
        [(0, ()), (1, (0,)), (1, (1,)), (2, ())])
    wout_full = wout_g.reshape(N_DEV * out_rows, d)
    wgu_full = wgu_g.reshape(2, ff, d)
    wd_full = wd_g.reshape(ff, d)
    cat = _pool_fwd(proj, w_pool[0], pool_scale, attn, n_seq, seq)
    tok_f32, tok_bf16 = jax.ShapeDtypeStruct((t, d), F32), jax.ShapeDtypeStruct((t, d), BF16)
    seq_sds, vec_sds = jax.ShapeDtypeStruct((n_seq, 1, d), F32), jax.ShapeDtypeStruct((1, d), F32)
    mix, x1, h2 = _matmul_rows(
        cat, wout_full.reshape(2, d // 2, d), ROW_TILE, d // 2, seq, "mix_mid", _mid_epilogue,
        [x2, g_mix_post, g_ffn_pre, mod], ["tok", "vec", "vec", "mod"],
        [tok_f32, tok_f32, tok_bf16], ["tok", "tok", "tok"])
    gu, act = _ffn_up(h2, wgu_full, 512, ff // 2)
    loss_sum, dy, df, dgate_f, gg_ffn_post = _matmul_rows(
        act, wd_full, ROW_TILE, ff, seq, "ffn_down_post", _post_epilogue,
        [x1, tgt2, g_ffn_post, mod], ["tok", "tok", "vec", "mod"],
        [jax.ShapeDtypeStruct((1, LANES), F32), tok_f32, tok_bf16, seq_sds, vec_sds],
        ["loss", "tok", "tok", "seq", "vec"])

    dgu = _ffn_act_bwd(df, wd_full, gu, 512, ff // 2)
    gwd, gwd_b = _matmul(act, df, "tn", F32, ff // 2, d, 1024, "grad_w_down", bf16_copy=True)
    gwgu, gwgu_b = _matmul(dgu, h2, "tn", F32, ff // 2, d, 1024, "grad_w_gate_up", bf16_copy=True)
    dx1, dmix, dshift_f, dscale_f, dgate_m, gg_ffn_pre, gg_mix_post = _matmul_rows(
        dgu, wgu_full, ROW_TILE, ff, seq, "dh2_bwd_mid", _bwd_mid_epilogue,
        [dy, x1, mix, g_ffn_pre, g_mix_post, mod], ["tok", "tok", "tok", "vec", "vec", "mod"],
        [tok_f32, tok_bf16, seq_sds, seq_sds, seq_sds, vec_sds, vec_sds],
        ["tok", "tok", "seq", "seq", "seq", "vec", "vec"])
    dcat = _matmul(dmix, wout_full, "nt", BF16, 1024, 512, d, "dcat")
    gwout, gwout_b = _matmul(cat, dmix, "tn", F32, d // 2, d, 1024, "grad_w_out", bf16_copy=True)
    dqkv, rv_wgu, rv_wd, rv_wout = _attn_bwd(
        proj, dcat, cstats, tri_after, tri_incl, n_seq, seq,
        [gwgu_b.reshape(2, N_DEV, ff_rows, d), gwd_b.reshape(1, N_DEV, ff_rows, d),
         gwout_b.reshape(1, N_DEV, out_rows, d)])
    dproj, gw_pool, gs_pool = _pool_bwd(proj, dcat, w_pool[0], pool_scale, dqkv, n_seq, seq)
    gwin, gwin_b = _matmul(dproj, h1, "tn", F32, d // 2, d, 1024, "grad_w_in", bf16_copy=True)
    grad_x, dshift_m, dscale_m, gg_mix_pre, rv_win = _matmul_rows(
        dproj, win_full.reshape(4, d // 2, d), ROW_TILE, d // 2, seq, "dh1_bwd_pre", _bwd_pre_epilogue,
        [dx1, x2, g_mix_pre, mod], ["tok", "tok", "vec", "mod"],
        [tok_f32, seq_sds, seq_sds, vec_sds], ["tok", "seq", "seq", "vec"],
        rs_sends=[gwin_b.reshape(1, N_DEV, in_rows, d)])

    r_wgu = _rs_final(gwgu.reshape(2, N_DEV, ff_rows, d), rv_wgu, "rs_final_gate_up")
    r_wd = _rs_final(gwd.reshape(1, N_DEV, ff_rows, d), rv_wd, "rs_final_down")
    r_wout = _rs_final(gwout.reshape(1, N_DEV, out_rows, d), rv_wout, "rs_final_out")
    r_win = _rs_final(gwin.reshape(1, N_DEV, in_rows, d), rv_win, "rs_final_in")
    grad_w_in = r_win[0].T
    grad_w_out = r_wout[0]
    grad_w_gate = r_wgu[0].T
    grad_w_up = r_wgu[1].T
    grad_w_down = r_wd[0]

    dmod = jnp.concatenate([dshift_m, dscale_m, dgate_m, dshift_f, dscale_f, dgate_f], axis=1)
    small = jnp.concatenate([gg_mix_pre, gg_mix_post, gg_ffn_pre, gg_ffn_post,
                             jnp.pad(gs_pool, ((0, 0), (0, d - gs_pool.shape[1]))),
                             jnp.pad(loss_sum, ((0, 0), (0, d - loss_sum.shape[1]))), jnp.zeros((2, d), F32),
                             gw_pool.reshape(-1, d), dmod.reshape(n_seq * N_MOD, d)], axis=0)
    n_gw = gw_pool.size // d
    (small_g,) = _all_gather([small], [jax.ShapeDtypeStruct((N_DEV,) + small.shape, F32)], [(0, ())], "ag_small")
    loss = jnp.sum(small_g[:, 5, 0]) * (0.5 / d)
    dmod_all = small_g[:, 8 + n_gw:, :].reshape(N_DEV * n_seq, N_MOD * d)
    dmod_cols = lax.dynamic_slice_in_dim(dmod_all, me * cond_cols, cond_cols, axis=1)
    grad_w_cond, grad_b_cond = _cond_bwd(c_all, dmod_all, dmod_cols)

    small_ws = [g_mix_pre, g_mix_post, g_ffn_pre, g_ffn_post, pool_scale, w_pool.reshape(-1, POOL_GROUP_DIM)]
    small_ms = [m_g_mix_pre, m_g_mix_post, m_g_ffn_pre, m_g_ffn_post, m_pool_scale, m_w_pool.reshape(-1, POOL_GROUP_DIM)]
    small_vs = [v_g_mix_pre, v_g_mix_post, v_g_ffn_pre, v_g_ffn_post, v_pool_scale, v_w_pool.reshape(-1, POOL_GROUP_DIM)]
    small_gparts = [small_g[:, 0:1, :], small_g[:, 1:2, :], small_g[:, 2:3, :], small_g[:, 3:4, :],
                    small_g[:, 4:5, :pool_scale.shape[1]],
                    small_g[:, 8:8 + n_gw, :].reshape(N_DEV, -1, POOL_GROUP_DIM)]
    so = _adamw_small(small_ws, small_gparts, small_ms, small_vs, "adamw_small")
    ns = len(small_ws)
    sg, sdl, sm, sv = so[:ns], so[ns:2 * ns], so[2 * ns:3 * ns], so[3 * ns:]
    pool_shape = w_pool.shape
    fix = lambda lst: [lst[0], lst[1], lst[2], lst[3], lst[4], lst[5].reshape(pool_shape)]
    sg, sdl, sm, sv = fix(sg), fix(sdl), fix(sm), fix(sv)

    def big(w, g, m, v, rows, name):
        dl, nm, nv = _adamw(w[0], g, m[0], v[0], rows, name)
        return g[None], dl[None], nm[None], nv[None]

    o_cond = big(w_cond, grad_w_cond, m_w_cond, v_w_cond, 256, "adamw_w_cond")
    o_bcond = _adamw(b_cond, grad_b_cond, m_b_cond, v_b_cond, 1, "adamw_b_cond")
    o_bcond = (grad_b_cond,) + tuple(o_bcond)
    o_in = big(w_in, grad_w_in, m_w_in, v_w_in, 256, "adamw_w_in")
    o_out = big(w_out, grad_w_out, m_w_out, v_w_out, out_rows, "adamw_w_out")
    o_gate = big(w_gate, grad_w_gate, m_w_gate, v_w_gate, 256, "adamw_w_gate")
    o_up = big(w_up, grad_w_up, m_w_up, v_w_up, 256, "adamw_w_up")
    o_down = big(w_down, grad_w_down, m_w_down, v_w_down, ff_rows, "adamw_w_down")

    def pick(k):
        small_k = [sg, sdl, sm, sv][k]
        return [o_cond[k], o_bcond[k], small_k[0], small_k[1], o_in[k], small_k[5], small_k[4], o_out[k],
                small_k[2], small_k[3], o_gate[k], o_up[k], o_down[k]]

    return (loss, grad_x.reshape(n_seq, seq, d), *pick(0), *pick(1), *pick(2), *pick(3))
```

```python
import functools
import math

import jax
import jax.numpy as jnp
from jax import lax
from jax.experimental import pallas as pl
from jax.experimental.pallas import tpu as pltpu

F32 = jnp.float32
BF16 = jnp.bfloat16
MESH = pl.DeviceIdType.MESH

N_DEV = 8
HEAD_DIM = 64
LANES = 128
POOL_WINDOWS = (2, 4, 8, 16)
POOL_GROUP_DIM = 128
N_MOD = 6
EPS = 1e-6
ATT_TILE = 256
ATT_PAIRS = 2
VMEM_LIMIT = 56 * 1024 * 1024

ADAM_LR = 0.001
ADAM_B1 = 0.9
ADAM_B2 = 0.999
ADAM_EPS = 1e-08
ADAM_WD = 0.01
ADAM_STEP = 10


def _params(**kw):
    return pltpu.CompilerParams(vmem_limit_bytes=VMEM_LIMIT, **kw)


def _dot_nn(a, b):
    return jnp.dot(a, b, preferred_element_type=F32)


def _dot_nt(a, b):
    return lax.dot_general(a, b, (((1,), (1,)), ((), ())), preferred_element_type=F32)


def _dot_tn(a, b):
    return lax.dot_general(a, b, (((0,), (0,)), ((), ())), preferred_element_type=F32)


def _mesh_pos():
    return lax.axis_index("x"), lax.axis_index("y"), lax.axis_index("c")


def _ag_phases(dests, src, outs, send_sems, recv_sems, local_sems):
    n = len(src)
    x, y, c = _mesh_pos()
    me, sibling = (x, y, c), (x, y, 1 - c)
    chips = [(1 - x, y), (x, 1 - y), (1 - x, 1 - y)]

    def slot(i, dev):
        oi, prefix = dests[i]
        px, py, pc = dev
        return outs[oi].at[prefix + (4 * px + 2 * py + pc,)]

    def copy(i, k, block, to, from_src=False):
        return pltpu.make_async_remote_copy(
            src_ref=src[i] if from_src else slot(i, block), dst_ref=slot(i, block),
            send_sem=send_sems.at[i, k], recv_sem=recv_sems.at[i, k],
            device_id=to, device_id_type=MESH)

    def mine(i):
        return pltpu.make_async_copy(src[i], slot(i, me), local_sems.at[i])

    def first(i):
        return [copy(i, 0, me, sibling, from_src=True)] + [
            copy(i, 1 + j, me, (*chip, c), from_src=True) for j, chip in enumerate(chips)]

    def passed(i, j):
        return copy(i, 4 + j, (*chips[j], c), sibling)

    def start():
        for i in range(n):
            mine(i).start()
        for i in range(n):
            for cp in first(i):
                cp.start()

    def forward():
        for j, chip in enumerate(chips):
            for i in range(n):
                copy(i, 1 + j, (*chip, c), me).wait_recv()
                passed(i, j).start()

    def finish():
        for i in range(n):
            copy(i, 0, sibling, me).wait_recv()
            for j, chip in enumerate(chips):
                copy(i, 4 + j, (*chip, 1 - c), me).wait_recv()
        for i in range(n):
            for cp in first(i) + [passed(i, j) for j in range(3)]:
                cp.wait_send()
            mine(i).wait()

    return start, forward, finish


def _ag_scratch(n):
    return [pltpu.SemaphoreType.DMA((n, 7)), pltpu.SemaphoreType.DMA((n, 7)), pltpu.SemaphoreType.DMA((n,))]


def _all_gather(srcs, out_shapes, dests, name):
    n = len(srcs)

    def body(*refs):
        src = refs[:n]
        outs = refs[n:n + len(out_shapes)]
        start, forward, finish = _ag_phases(dests, src, outs, *refs[n + len(out_shapes):])
        start()
        forward()
        finish()

    any_spec = pl.BlockSpec(memory_space=pl.ANY)
    return pl.pallas_call(
        body, name=name,
        out_shape=tuple(out_shapes),
        in_specs=[any_spec] * n,
        out_specs=tuple([any_spec] * len(out_shapes)),
        scratch_shapes=_ag_scratch(n),
    )(*srcs)


def _rs_phases(shapes, src, dst, send_sems, recv_sems):
    x, y, c = _mesh_pos()

    def copies():
        out = []
        n = 0
        for i, shp in enumerate(shapes):
            for m in range(shp[0]):
                for k in range(1, N_DEV):
                    px, py, pc = x ^ (k >> 2), y ^ ((k >> 1) & 1), c ^ (k & 1)
                    out.append(pltpu.make_async_remote_copy(
                        src_ref=src[i].at[m, 4 * px + 2 * py + pc], dst_ref=dst[i].at[m, k - 1],
                        send_sem=send_sems.at[n], recv_sem=recv_sems.at[n],
                        device_id=(px, py, pc), device_id_type=MESH))
                    n += 1
        return out

    def start():
        for cp in copies():
            cp.start()

    def finish():
        for cp in copies():
            cp.wait_send()
        for cp in copies():
            cp.wait_recv()

    return start, finish


def _rs_out(sends):
    return [jax.ShapeDtypeStruct((s.shape[0], N_DEV - 1) + s.shape[2:], s.dtype) for s in sends]


def _rs_scratch(sends):
    total = sum((N_DEV - 1) * s.shape[0] for s in sends)
    return [pltpu.SemaphoreType.DMA((total,)), pltpu.SemaphoreType.DMA((total,))]


def _rs_final(mine, recv, name):
    m_n, _, r, cdim = mine.shape
    x, y, c = _mesh_pos()
    me = jnp.reshape(4 * x + 2 * y + c, (1,)).astype(jnp.int32)

    def body(me_ref, p_ref, r_ref, o_ref):
        del me_ref
        s = p_ref[...]
        for k in range(N_DEV - 1):
            s = s + r_ref[k].astype(F32)
        o_ref[...] = s

    return pl.pallas_call(
        body, name=name, out_shape=jax.ShapeDtypeStruct((m_n, r, cdim), F32),
        grid_spec=pltpu.PrefetchScalarGridSpec(
            num_scalar_prefetch=1, grid=(m_n,),
            in_specs=[pl.BlockSpec((None, None, r, cdim), lambda m, s: (m, s[0], 0, 0)),
                      pl.BlockSpec((None, N_DEV - 1, r, cdim), lambda m, s: (m, 0, 0, 0))],
            out_specs=pl.BlockSpec((None, r, cdim), lambda m, s: (m, 0, 0))),
        compiler_params=_params(),
    )(me, mine, recv)


def _matmul(a, b, mode, out_dtype, tm, tn, tk, name, bf16_copy=False, rs_sends=()):
    ga = a.shape[0] if a.ndim == 3 else None
    gb = b.shape[0] if b.ndim == 3 else None
    a2, b2 = a.shape[-2:], b.shape[-2:]
    if mode == "nn":
        (m, k), n = a2, b2[1]
    elif mode == "nt":
        (m, k), n = a2, b2[0]
    else:
        (k, m), n = a2, b2[1]
    assert m % tm == 0 and n % tn == 0 and k % tk == 0, (name, m, n, k)
    nk = k // tk
    g_n = ga or 1
    batch_out = mode == "tn" and ga is not None
    n_red = nk if batch_out else nk * g_n
    dot = {"nn": _dot_nn, "nt": _dot_nt, "tn": _dot_tn}[mode]
    acc_in_out = out_dtype == F32

    n_rs = len(rs_sends)
    rs_shapes = [r.shape for r in rs_sends]
    n_out = 2 if bf16_copy else 1
    assert not bf16_copy or acc_in_out

    def body(a_ref, b_ref, *rest):
        rs_src, rest = rest[:n_rs], rest[n_rs:]
        o_ref = rest[0]
        copy_ref = rest[1] if bf16_copy else None
        rs_dst, scratch = rest[n_out:n_out + n_rs], rest[n_out + n_rs:]
        if n_rs:
            rs_start, rs_finish = _rs_phases(rs_shapes, rs_src, rs_dst, *scratch[-2:])
            first = functools.reduce(jnp.logical_and, [pl.program_id(ax) == 0 for ax in range(4)])
            last = functools.reduce(jnp.logical_and, [pl.program_id(ax) == grid[ax] - 1 for ax in range(4)])
            pl.when(first)(rs_start)
        p = dot(a_ref[...], b_ref[...])
        kk = pl.program_id(3) if batch_out else pl.program_id(2) * nk + pl.program_id(3)
        if n_red == 1:
            o_ref[...] = p.astype(out_dtype)
            if bf16_copy:
                copy_ref[...] = p.astype(BF16)
        else:
            acc = o_ref if acc_in_out else scratch[0]

            @pl.when(kk == 0)
            def _():
                acc[...] = p

            @pl.when(kk > 0)
            def _():
                acc[...] += p

            @pl.when(kk == n_red - 1)
            def _():
                if not acc_in_out:
                    o_ref[...] = acc[...].astype(out_dtype)
                if bf16_copy:
                    copy_ref[...] = acc[...].astype(BF16)

        if n_rs:
            pl.when(last)(rs_finish)

    def order(ids):
        return ids if batch_out else (ids[2], ids[0], ids[1], ids[3])

    def a_idx(*ids):
        g, i, j, kq = order(ids)
        blk = {"nn": (i, kq), "nt": (i, kq), "tn": (kq, i)}[mode]
        return (g,) + blk if ga is not None else blk

    def b_idx(*ids):
        g, i, j, kq = order(ids)
        blk = {"nn": (kq, j), "nt": (j, kq), "tn": (kq, j)}[mode]
        return (g,) + blk if gb is not None else blk

    def o_idx(*ids):
        g, i, j, kq = order(ids)
        return (g, i, j) if batch_out else (i, j)

    a_blk = {"nn": (tm, tk), "nt": (tm, tk), "tn": (tk, tm)}[mode]
    b_blk = {"nn": (tk, tn), "nt": (tn, tk), "tn": (tk, tn)}[mode]
    if ga is not None:
        a_blk = (None,) + a_blk
    if gb is not None:
        b_blk = (None,) + b_blk
    if batch_out:
        out_shape = jax.ShapeDtypeStruct((g_n, m, n), out_dtype)
        o_blk = (None, tm, tn)
        grid = (g_n, m // tm, n // tn, nk)
    else:
        out_shape = jax.ShapeDtypeStruct((m, n), out_dtype)
        o_blk = (tm, tn)
        grid = (m // tm, n // tn, g_n, nk)
    scratch = [] if (acc_in_out or n_red == 1) else [pltpu.VMEM((tm, tn), F32)]
    any_spec = pl.BlockSpec(memory_space=pl.ANY)
    out_shapes = [out_shape] + ([jax.ShapeDtypeStruct(out_shape.shape, BF16)] if bf16_copy else [])
    res = pl.pallas_call(
        body, name=name, out_shape=tuple(out_shapes + _rs_out(rs_sends)), grid=grid,
        in_specs=[pl.BlockSpec(a_blk, a_idx), pl.BlockSpec(b_blk, b_idx)] + [any_spec] * n_rs,
        out_specs=tuple([pl.BlockSpec(o_blk, o_idx)] * n_out + [any_spec] * n_rs),
        scratch_shapes=scratch + (_rs_scratch(rs_sends) if n_rs else []), compiler_params=_params(),
    )(a, b, *rs_sends)
    return res if len(res) > 1 else res[0]


EW_TILE = 256
ROW_TILE = 512
EPILOGUE_CHUNKS = 8
MXU_WIDTH = 256


def _rms(v):
    return lax.rsqrt(jnp.mean(v * v, axis=-1, keepdims=True) + EPS)


def _rms_bwd(dhat, vh, r):
    return r * (dhat - vh * jnp.mean(dhat * vh, axis=-1, keepdims=True))


def _tok_spec(tm, d):
    return pl.BlockSpec((tm, d), lambda i: (i, 0))


def _vec_spec(d):
    return pl.BlockSpec((1, d), lambda i: (0, 0))


def _mod_spec(tiles_per_seq, d):
    return pl.BlockSpec((None, N_MOD, d), lambda i: (i // tiles_per_seq, 0, 0))


def _seq_acc_spec(tiles_per_seq, d):
    return pl.BlockSpec((None, 1, d), lambda i: (i // tiles_per_seq, 0, 0))


def _acc(ref, val, first):
    if first is False:
        ref[...] += val
        return

    @pl.when(first)
    def _():
        ref[...] = val

    @pl.when(jnp.logical_not(first))
    def _():
        ref[...] += val


def _colsum(v):
    return jnp.sum(v, axis=0, keepdims=True)


def _pre_mix(x2, g_pre, mod, seq):
    t, d = x2.shape
    tm = EW_TILE

    def body(x_ref, g_ref, mod_ref, h_ref):
        xv = x_ref[...]
        n = xv * _rms(xv) * g_ref[...]
        h_ref[...] = (n * (1.0 + mod_ref[1:2, :]) + mod_ref[0:1, :]).astype(BF16)

    return pl.pallas_call(
        body, name="pre_mix", out_shape=jax.ShapeDtypeStruct((t, d), BF16), grid=(t // tm,),
        in_specs=[_tok_spec(tm, d), _vec_spec(d), _mod_spec(seq // tm, d)],
        out_specs=_tok_spec(tm, d), compiler_params=_params(),
    )(x2, g_pre, mod)


def _matmul_rows(a, b, tm, tk, seq, name, epilogue, ep_in, ep_in_kinds, ep_out, ep_out_kinds, rs_sends=()):
    ga = a.shape[0] if a.ndim == 3 else None
    (m, k), n = a.shape[-2:], b.shape[-1]
    g_n = ga or 1
    nk = k // tk
    n_red = g_n * nk
    tps = seq // tm
    n_i = m // tm
    grid = (n_i + 1, g_n, nk)
    n_rs = len(rs_sends)
    rs_shapes = [r.shape for r in rs_sends]
    n_in, n_out = len(ep_in), len(ep_out)
    per_step = -(-EPILOGUE_CHUNKS // n_red)
    n_chunks = per_step * n_red
    n_cols = min(per_step, n // MXU_WIDTH)
    rc, cw = tm // n_chunks, n // n_cols

    def prev(i):
        return jnp.maximum(i - 1, 0)

    def spec(kind):
        return {"tok": pl.BlockSpec((tm, n), lambda i, g, kq: (prev(i), 0)),
                "vec": pl.BlockSpec((1, n), lambda i, g, kq: (0, 0)),
                "mod": pl.BlockSpec((None, N_MOD, n), lambda i, g, kq: (prev(i) // tps, 0, 0)),
                "seq": pl.BlockSpec((None, 1, n), lambda i, g, kq: (prev(i) // tps, 0, 0)),
                "loss": pl.BlockSpec((1, LANES), lambda i, g, kq: (0, 0))}[kind]

    def body(a_ref, b_ref, *rest):
        in_refs, rest = rest[:n_in], rest[n_in:]
        rs_src, rest = rest[:n_rs], rest[n_rs:]
        out_refs, rest = rest[:n_out], rest[n_out:]
        rs_dst, rest = rest[:n_rs], rest[n_rs:]
        acc, fin = rest[:2]
        i, kk = pl.program_id(0), pl.program_id(1) * nk + pl.program_id(2)
        if n_rs:
            rs_start, rs_finish = _rs_phases(rs_shapes, rs_src, rs_dst, *rest[2:])
            pl.when(jnp.logical_and(i == 0, kk == 0))(rs_start)

        def step(s, with_epilogue, with_matmul):
            last = s == n_red - 1
            parts = []
            cols_done = 0
            for c in range(per_step):
                if with_epilogue:
                    chunk = s * per_step + c
                    rows = pl.ds(chunk * rc, rc)
                    epilogue(fin[rows, :], i - 1, tps, in_refs, out_refs, rows, chunk)
                while with_matmul and cols_done < (c + 1) * n_cols // per_step:
                    cols = slice(cols_done * cw, (cols_done + 1) * cw)
                    cols_done += 1
                    p = _dot_nn(a_ref[...], b_ref[:, cols])
                    if s == 0 and not last:
                        acc[:, cols] = p
                    elif not last:
                        acc[:, cols] += p
                    else:
                        parts.append((cols, p if s == 0 else acc[:, cols] + p))
            for cols, v in parts:
                fin[:, cols] = v

        for s in range(n_red):
            at = kk == s
            pl.when(jnp.logical_and(at, i == 0))(functools.partial(step, s, False, True))
            pl.when(jnp.logical_and(at, jnp.logical_and(i > 0, i < n_i)))(functools.partial(step, s, True, True))
            pl.when(jnp.logical_and(at, i == n_i))(functools.partial(step, s, True, False))

        if n_rs:
            pl.when(jnp.logical_and(i == n_i, kk == n_red - 1))(rs_finish)

    def row(i):
        return jnp.minimum(i, n_i - 1)

    a_blk = (tm, tk) if ga is None else (None, tm, tk)
    b_blk = (tk, n) if ga is None else (None, tk, n)
    a_idx = (lambda i, g, kq: (row(i), kq)) if ga is None else (lambda i, g, kq: (g, row(i), kq))
    b_idx = (lambda i, g, kq: (kq, 0)) if ga is None else (lambda i, g, kq: (g, kq, 0))
    any_spec = pl.BlockSpec(memory_space=pl.ANY)
    res = pl.pallas_call(
        body, name=name, grid=grid, out_shape=tuple(list(ep_out) + _rs_out(rs_sends)),
        in_specs=[pl.BlockSpec(a_blk, a_idx), pl.BlockSpec(b_blk, b_idx)] + [spec(kd) for kd in ep_in_kinds]
        + [any_spec] * n_rs,
        out_specs=tuple([spec(kd) for kd in ep_out_kinds] + [any_spec] * n_rs),
        scratch_shapes=[pltpu.VMEM((tm, n), F32)] * 2 + (_rs_scratch(rs_sends) if n_rs else []),
        compiler_params=_params(),
    )(a, b, *ep_in, *rs_sends)
    return res


def _first(cond, chunk):
    return cond if chunk == 0 else False


def _mid_epilogue(mv, i, tps, in_refs, out_refs, rows, chunk):
    x_ref, gpost_ref, gpre_ref, mod_ref = in_refs
    mix_ref, x1_ref, h2_ref = out_refs
    mix_ref[rows, :] = mv
    x1 = x_ref[rows, :] + mod_ref[2:3, :] * (mv * _rms(mv) * gpost_ref[...])
    x1_ref[rows, :] = x1
    n = x1 * _rms(x1) * gpre_ref[...]
    h2_ref[rows, :] = (n * (1.0 + mod_ref[4:5, :]) + mod_ref[3:4, :]).astype(BF16)


def _post_epilogue(fv, i, tps, in_refs, out_refs, rows, chunk):
    x1_ref, tgt_ref, g_ref, mod_ref = in_refs
    loss_ref, dy_ref, df_ref, dgate_ref, gg_ref = out_refs
    d = fv.shape[1]
    r = _rms(fv)
    fh = fv * r
    nf = fh * g_ref[...]
    gate = mod_ref[5:6, :]
    err = x1_ref[rows, :] + gate * nf - tgt_ref[rows, :]
    _acc(loss_ref, jnp.sum(_colsum(err * err), axis=1, keepdims=True) * jnp.ones((1, LANES), F32),
         _first(i == 0, chunk))
    dy = err * (1.0 / d)
    dy_ref[rows, :] = dy
    _acc(dgate_ref, _colsum(dy * nf), _first(i % tps == 0, chunk))
    dn = dy * gate
    _acc(gg_ref, _colsum(dn * fh), _first(i == 0, chunk))
    df_ref[rows, :] = _rms_bwd(dn * g_ref[...], fh, r).astype(BF16)


def _bwd_mid_epilogue(dh, i, tps, in_refs, out_refs, rows, chunk):
    dy_ref, x1_ref, mix_ref, gpre_ref, gpost_ref, mod_ref = in_refs
    dx1_ref, dmix_ref, dshift_ref, dscale_ref, dgate_ref, ggpre_ref, ggpost_ref = out_refs
    seq_first, first = _first(i % tps == 0, chunk), _first(i == 0, chunk)
    x1 = x1_ref[rows, :]
    r = _rms(x1)
    xh = x1 * r
    gpre = gpre_ref[...]
    _acc(dshift_ref, _colsum(dh), seq_first)
    _acc(dscale_ref, _colsum(dh * xh * gpre), seq_first)
    dn = dh * (1.0 + mod_ref[4:5, :])
    _acc(ggpre_ref, _colsum(dn * xh), first)
    dx1 = dy_ref[rows, :] + _rms_bwd(dn * gpre, xh, r)
    dx1_ref[rows, :] = dx1
    mv = mix_ref[rows, :]
    rm = _rms(mv)
    mh = mv * rm
    gpost = gpost_ref[...]
    _acc(dgate_ref, _colsum(dx1 * mh * gpost), seq_first)
    dnm = dx1 * mod_ref[2:3, :]
    _acc(ggpost_ref, _colsum(dnm * mh), first)
    dmix_ref[rows, :] = _rms_bwd(dnm * gpost, mh, rm).astype(BF16)


def _bwd_pre_epilogue(dh, i, tps, in_refs, out_refs, rows, chunk):
    dx1_ref, x_ref, g_ref, mod_ref = in_refs
    gx_ref, dshift_ref, dscale_ref, gg_ref = out_refs
    seq_first = _first(i % tps == 0, chunk)
    xv = x_ref[rows, :]
    r = _rms(xv)
    xh = xv * r
    g = g_ref[...]
    _acc(dshift_ref, _colsum(dh), seq_first)
    _acc(dscale_ref, _colsum(dh * xh * g), seq_first)
    dn = dh * (1.0 + mod_ref[1:2, :])
    _acc(gg_ref, _colsum(dn * xh), _first(i == 0, chunk))
    gx_ref[rows, :] = dx1_ref[rows, :] + _rms_bwd(dn * g, xh, r)


def _ffn_up(h2, wgu, tm, tn):
    t, d = h2.shape
    f = wgu.shape[1]

    def body(h_ref, w_ref, gu_ref, act_ref):
        h = h_ref[...]
        g = _dot_nt(h, w_ref[0])
        u = _dot_nt(h, w_ref[1])
        gu_ref[0] = g.astype(BF16)
        gu_ref[1] = u.astype(BF16)
        act_ref[...] = (g * jax.nn.sigmoid(g) * u).astype(BF16)

    return pl.pallas_call(
        body, name="ffn_up", grid=(t // tm, f // tn),
        out_shape=(jax.ShapeDtypeStruct((2, t, f), BF16), jax.ShapeDtypeStruct((t, f), BF16)),
        in_specs=[pl.BlockSpec((tm, d), lambda i, j: (i, 0)), pl.BlockSpec((2, tn, d), lambda i, j: (0, j, 0))],
        out_specs=(pl.BlockSpec((2, tm, tn), lambda i, j: (0, i, j)), pl.BlockSpec((tm, tn), lambda i, j: (i, j))),
        compiler_params=_params(),
    )(h2, wgu)


def _ffn_act_bwd(df, wd, gu, tm, tn):
    t, d = df.shape
    f = wd.shape[0]

    def body(df_ref, w_ref, gu_ref, dgu_ref):
        da = _dot_nt(df_ref[...], w_ref[...])
        g = gu_ref[0].astype(F32)
        u = gu_ref[1].astype(F32)
        s = jax.nn.sigmoid(g)
        silu = g * s
        dgu_ref[0] = (da * u * (s + silu * (1.0 - s))).astype(BF16)
        dgu_ref[1] = (da * silu).astype(BF16)

    return pl.pallas_call(
        body, name="ffn_act_bwd", grid=(t // tm, f // tn),
        out_shape=jax.ShapeDtypeStruct((2, t, f), BF16),
        in_specs=[pl.BlockSpec((tm, d), lambda i, j: (i, 0)), pl.BlockSpec((tn, d), lambda i, j: (j, 0)),
                  pl.BlockSpec((2, tm, tn), lambda i, j: (0, i, j))],
        out_specs=pl.BlockSpec((2, tm, tn), lambda i, j: (0, i, j)),
        compiler_params=_params(),
    )(df, wd, gu)


SIGN_BIT = 0x80000000
Q_SCALE = 1.0 / math.sqrt(HEAD_DIM)


def _split_dot(v, tri2):
    hi = v.astype(BF16)
    lo = (v - hi.astype(F32)).astype(BF16)
    return _dot_nn(jnp.concatenate([hi, lo], axis=1), tri2)


def _sb_tile(qs, k2, mask, ntri2, cur):
    z = _dot_nt(qs, k2)
    neg_abs = lax.bitcast_convert_type(lax.bitcast_convert_type(z, jnp.uint32) | jnp.uint32(SIGN_BIT), F32)
    sp = jnp.maximum(z, 0.0) + jnp.log(1.0 + jnp.exp(neg_abs))
    if mask is not None:
        sp = jnp.where(mask, sp, 0.0)
    w = jnp.exp(z + _split_dot(sp, ntri2) + cur)
    if mask is not None:
        w = jnp.where(mask, w, 0.0)
    return z, sp, w


def _softplus(z):
    neg_abs = lax.bitcast_convert_type(lax.bitcast_convert_type(z, jnp.uint32) | jnp.uint32(SIGN_BIT), F32)
    return jnp.maximum(z, 0.0) + jnp.log(1.0 + jnp.exp(neg_abs))


def _hi_lo(v):
    hi = v.astype(BF16)
    return jnp.concatenate([hi, (v - hi.astype(F32)).astype(BF16)], axis=1)


def _emit_skewed(chains, lag=1):
    for t in range(max(len(ch) for ch in chains) + lag * (len(chains) - 1)):
        for c, ch in enumerate(chains):
            if 0 <= t - lag * c < len(ch):
                ch[t - lag * c]()


def _fwd_chain(blk, qs, k_ref, v_ref, c0, kb, cols, mask, ntri2, lane, tq):
    st = {}

    def scores():
        st["z"] = _dot_nt(qs, k_ref[pl.ds(c0, tq), cols])

    def soft():
        sp = _softplus(st["z"])
        if mask is not None:
            sp = jnp.where(mask, sp, 0.0)
        st["parts"] = _hi_lo(sp)
        st["cur"] = blk["cur"]
        blk["cm"] = jnp.where(lane == kb, blk["cur"], blk["cm"])
        blk["cur"] = blk["cur"] - jnp.sum(sp, axis=1, keepdims=True)

    def sums():
        st["s"] = _dot_nn(st["parts"], ntri2)

    def weights():
        w = jnp.exp(st["z"] + st["s"] + st["cur"])
        if mask is not None:
            w = jnp.where(mask, w, 0.0)
        st["w"] = w.astype(BF16)

    def out():
        p = _dot_nn(st["w"], v_ref[pl.ds(c0, tq), cols])
        blk["pv"] = p if blk["pv"] is None else blk["pv"] + p

    return [scores, soft, sums, weights, out]


def _bwd_chain(blk, qs, dos, cs, k_ref, v_ref, dk_ref, dv_ref, c0, kb, cols, mask, ntri2, tri_i, lane, tq):
    st = {}

    def scores():
        st["z"] = _dot_nt(qs, k_ref[pl.ds(c0, tq), cols])
        st["dw"] = _dot_nt(dos, v_ref[pl.ds(c0, tq), cols])

    def soft():
        sp = _softplus(st["z"])
        if mask is not None:
            sp = jnp.where(mask, sp, 0.0)
        st["sp"] = sp
        st["parts"] = _hi_lo(sp)
        st["cur"] = jnp.sum(jnp.where(lane == kb, cs, 0.0), axis=1, keepdims=True)

    def sums():
        st["s"] = _dot_nn(st["parts"], ntri2)

    def weights():
        w = jnp.exp(st["z"] + st["s"] + st["cur"])
        if mask is not None:
            w = jnp.where(mask, w, 0.0)
        ee = w * st["dw"]
        st["w"], st["ee"], st["ec"] = w.astype(BF16), ee, blk["ec"]
        blk["ec"] = blk["ec"] + jnp.sum(ee, axis=1, keepdims=True)

    def prefix():
        st["einc"] = _dot_nn(st["ee"].astype(BF16), tri_i)

    def dz():
        v = st["ee"] - jnp.exp(st["z"] - st["sp"]) * (st["einc"] + st["ec"])
        if mask is not None:
            v = jnp.where(mask, v, 0.0)
        st["dz"] = v.astype(BF16)

    def grads():
        p = _dot_nn(st["dz"], k_ref[pl.ds(c0, tq), cols])
        blk["dq"] = p if blk["dq"] is None else blk["dq"] + p
        dk_ref[pl.ds(c0, tq), :] += _dot_tn(st["dz"], qs)
        dv_ref[pl.ds(c0, tq), :] += _dot_tn(st["w"], dos)

    return [scores, soft, sums, weights, prefix, dz, grads]


def _stack_heads(v, lane, scale=None):
    if scale is not None:
        v = v * jnp.asarray(scale, v.dtype)
    zero = jnp.zeros_like(v)
    return jnp.concatenate([jnp.where(lane < HEAD_DIM, v, zero), jnp.where(lane >= HEAD_DIM, v, zero)], axis=0)


def _diag_mask(tq):
    row = lax.broadcasted_iota(jnp.int32, (2 * tq, tq), 0)
    col = lax.broadcasted_iota(jnp.int32, (2 * tq, tq), 1)
    return col < jnp.where(row >= tq, row - tq, row)


def _attn_fwd(proj, tri_after, n_seq, seq, ag_srcs, ag_out_shapes, ag_dests):
    t = proj.shape[0]
    tq = ATT_TILE
    npp = ATT_PAIRS
    n_blk = (proj.shape[1] // 4) // (npp * LANES)
    n_ag, n_ag_out = len(ag_srcs), len(ag_out_shapes)
    n_steps = n_seq * n_blk

    def body(q_ref, k_ref, v_ref, tri_ref, *rest):
        ag_src, rest = rest[:n_ag], rest[n_ag:]
        o_ref, cs_ref = rest[:2]
        ag_out, rest = rest[2:2 + n_ag_out], rest[2 + n_ag_out:]
        oacc, cmat, carry = rest[:3]
        ag_start, ag_forward, ag_finish = _ag_phases(ag_dests, ag_src, ag_out, *rest[3:])
        step = pl.program_id(0) * n_blk + pl.program_id(1)
        pl.when(step == 0)(ag_start)
        pl.when(step == (3 * n_steps) // 4)(ag_forward)
        lane = lax.broadcasted_iota(jnp.int32, (1, LANES), 1)
        ntri2 = tri_ref[...]
        diag = _diag_mask(tq)

        def q_tile(qi, _):
            r0 = pl.multiple_of(qi * tq, tq)
            qs = [_stack_heads(q_ref[pl.ds(r0, tq), pp * LANES:(pp + 1) * LANES], lane, Q_SCALE)
                  for pp in range(npp)]
            carry[...] = jnp.zeros_like(carry)
            cmat[...] = jnp.zeros_like(cmat)
            oacc[...] = jnp.zeros_like(oacc)

            def run_tiles(tiles):
                blocks = [dict(cur=carry[pp], cm=cmat[pp], pv=None) for pp in range(npp)]
                chains = []
                for kb, mask in tiles:
                    c0 = pl.multiple_of(kb * tq, tq)
                    for pp in range(npp):
                        chains.append(_fwd_chain(blocks[pp], qs[pp], k_ref, v_ref, c0, kb,
                                                 slice(pp * LANES, (pp + 1) * LANES), mask, ntri2, lane, tq))
                _emit_skewed(chains)
                for pp in range(npp):
                    oacc[pp] += blocks[pp]["pv"]
                    cmat[pp] = blocks[pp]["cm"]
                    carry[pp] = blocks[pp]["cur"]

            odd = qi % 2

            @pl.when(odd == 0)
            def _():
                run_tiles([(qi, diag)])

            @pl.when(odd == 1)
            def _():
                run_tiles([(qi, diag), (qi - 1, None)])

            def pair(j, _):
                kb = qi - 1 - odd - 2 * j
                run_tiles([(kb, None), (kb - 1, None)])
                return 0

            lax.fori_loop(0, qi // 2, pair, 0)
            for pp in range(npp):
                c_off = 2 * pp * LANES
                cs_ref[pl.ds(r0, tq), c_off:c_off + LANES] = cmat[pp, 0:tq, :]
                cs_ref[pl.ds(r0, tq), c_off + LANES:c_off + 2 * LANES] = cmat[pp, tq:2 * tq, :]
                o_ref[pl.ds(r0, tq), pp * LANES:(pp + 1) * LANES] = jnp.where(
                    lane < HEAD_DIM, oacc[pp, 0:tq, :], oacc[pp, tq:2 * tq, :]).astype(BF16)
            return 0

        lax.fori_loop(0, seq // tq, q_tile, 0)
        pl.when(step == n_steps - 1)(ag_finish)

    wid = npp * LANES
    blk = lambda off: pl.BlockSpec((seq, wid), lambda b, p: (b, off + p))
    any_spec = pl.BlockSpec(memory_space=pl.ANY)
    return pl.pallas_call(
        body, name="attn_fwd", grid=(n_seq, n_blk),
        out_shape=(jax.ShapeDtypeStruct((2, t, n_blk * wid), BF16),
                   jax.ShapeDtypeStruct((t, n_blk * 2 * wid), F32), *ag_out_shapes),
        in_specs=[blk(0), blk(n_blk), blk(2 * n_blk), pl.BlockSpec((2 * tq, tq), lambda b, p: (0, 0))]
        + [any_spec] * n_ag,
        out_specs=(pl.BlockSpec((None, seq, wid), lambda b, p: (0, b, p)),
                   pl.BlockSpec((seq, 2 * wid), lambda b, p: (b, p)), *([any_spec] * n_ag_out)),
        scratch_shapes=[pltpu.VMEM((npp, 2 * tq, LANES), F32), pltpu.VMEM((npp, 2 * tq, LANES), F32),
                        pltpu.VMEM((npp, 2 * tq, 1), F32)] + _ag_scratch(n_ag),
        compiler_params=_params(),
    )(proj, proj, proj, tri_after, *ag_srcs)


def _attn_bwd(proj, dcat, cstats, tri_after, tri_incl, n_seq, seq, rs_sends):
    t = proj.shape[0]
    tq = ATT_TILE
    npp = ATT_PAIRS
    width = proj.shape[1] // 4
    n_blk = width // (npp * LANES)
    n_rs = len(rs_sends)
    rs_shapes = [r.shape for r in rs_sends]
    n_steps = n_seq * n_blk

    def body(q_ref, k_ref, v_ref, do_ref, cs_ref, tria_ref, trii_ref, *rest):
        rs_src, rest = rest[:n_rs], rest[n_rs:]
        out_ref = rest[0]
        rs_dst, rest = rest[1:1 + n_rs], rest[1 + n_rs:]
        dq_acc, dk_acc, dv_acc, ecarry = rest[:4]
        rs_start, rs_finish = _rs_phases(rs_shapes, rs_src, rs_dst, *rest[4:])
        step = pl.program_id(0) * n_blk + pl.program_id(1)
        pl.when(step == 0)(rs_start)
        lane = lax.broadcasted_iota(jnp.int32, (1, LANES), 1)
        ntri2 = tria_ref[...]
        tri_i = trii_ref[...]
        diag = _diag_mask(tq)
        dk_acc[...] = jnp.zeros_like(dk_acc)
        dv_acc[...] = jnp.zeros_like(dv_acc)

        def q_tile(qi, _):
            r0 = pl.multiple_of(qi * tq, tq)
            qs, dos, cs = [], [], []
            for pp in range(npp):
                cols = slice(pp * LANES, (pp + 1) * LANES)
                qs.append(_stack_heads(q_ref[pl.ds(r0, tq), cols], lane, Q_SCALE))
                dos.append(_stack_heads(do_ref[pl.ds(r0, tq), cols], lane))
                c_off = 2 * pp * LANES
                cs.append(jnp.concatenate([cs_ref[pl.ds(r0, tq), c_off:c_off + LANES],
                                           cs_ref[pl.ds(r0, tq), c_off + LANES:c_off + 2 * LANES]], axis=0))
            ecarry[...] = jnp.zeros_like(ecarry)
            dq_acc[...] = jnp.zeros_like(dq_acc)

            def run_tiles(tiles):
                blocks = [dict(ec=ecarry[pp], dq=None) for pp in range(npp)]
                chains = []
                for kb, mask in tiles:
                    c0 = pl.multiple_of(kb * tq, tq)
                    for pp in range(npp):
                        chains.append(_bwd_chain(
                            blocks[pp], qs[pp], dos[pp], cs[pp], k_ref, v_ref, dk_acc.at[pp], dv_acc.at[pp],
                            c0, kb, slice(pp * LANES, (pp + 1) * LANES), mask, ntri2, tri_i, lane, tq))
                _emit_skewed(chains)
                for pp in range(npp):
                    dq_acc[pp] += blocks[pp]["dq"]
                    ecarry[pp] = blocks[pp]["ec"]

            def pair(j, _):
                run_tiles([(2 * j, None), (2 * j + 1, None)])
                return 0

            lax.fori_loop(0, qi // 2, pair, 0)
            odd = qi % 2

            @pl.when(odd == 0)
            def _():
                run_tiles([(qi, diag)])

            @pl.when(odd == 1)
            def _():
                run_tiles([(qi - 1, None), (qi, diag)])

            for pp in range(npp):
                dq = jnp.where(lane < HEAD_DIM, dq_acc[pp, 0:tq, :], dq_acc[pp, tq:2 * tq, :])
                out_ref[0, pl.ds(r0, tq), pp * LANES:(pp + 1) * LANES] = (dq * Q_SCALE).astype(BF16)
            return 0

        lax.fori_loop(0, seq // tq, q_tile, 0)
        for pp in range(npp):
            cols = slice(pp * LANES, (pp + 1) * LANES)
            out_ref[1, :, cols] = dk_acc[pp].astype(BF16)
            out_ref[2, :, cols] = dv_acc[pp].astype(BF16)
        pl.when(step == n_steps - 1)(rs_finish)

    wid = npp * LANES
    blk = lambda off: pl.BlockSpec((seq, wid), lambda b, p: (b, off + p))
    tri_spec = pl.BlockSpec((2 * tq, tq), lambda b, p: (0, 0))
    any_spec = pl.BlockSpec(memory_space=pl.ANY)
    return pl.pallas_call(
        body, name="attn_bwd", grid=(n_seq, n_blk),
        out_shape=(jax.ShapeDtypeStruct((4, t, width), BF16), *_rs_out(rs_sends)),
        in_specs=[blk(0), blk(n_blk), blk(2 * n_blk), pl.BlockSpec((seq, wid), lambda b, p: (b, p)),
                  pl.BlockSpec((seq, 2 * wid), lambda b, p: (b, p)), tri_spec,
                  pl.BlockSpec((tq, tq), lambda b, p: (0, 0))] + [any_spec] * n_rs,
        out_specs=(pl.BlockSpec((3, seq, wid), lambda b, p: (0, b, p)), *([any_spec] * n_rs)),
        scratch_shapes=[pltpu.VMEM((npp, 2 * tq, LANES), F32), pltpu.VMEM((npp, seq, LANES), F32),
                        pltpu.VMEM((npp, seq, LANES), F32), pltpu.VMEM((npp, 2 * tq, 1), F32)]
        + _rs_scratch(rs_sends),
        compiler_params=_params(),
    )(proj, proj, proj, dcat, cstats, tri_after, tri_incl, *rs_sends)


def _window_terms(g, rows):
    win = jnp.where(g == 0, POOL_WINDOWS[0], jnp.where(g == 1, POOL_WINDOWS[1],
                    jnp.where(g == 2, POOL_WINDOWS[2], POOL_WINDOWS[3])))
    cnt = jnp.minimum(rows + 1, win).astype(F32)
    return win, cnt


def _window_sum(v, g, rows, forward):
    s_len = v.shape[0]
    sums = []
    s = v
    for step in range(len(POOL_WINDOWS)):
        sh = 1 << step
        if forward:
            shifted = jnp.where(rows < s_len - sh, pltpu.roll(s, s_len - sh, axis=0), 0.0)
        else:
            shifted = jnp.where(rows >= sh, pltpu.roll(s, sh, axis=0), 0.0)
        s = s + shifted
        sums.append(s)
    return jnp.where(g == 0, sums[0], jnp.where(g == 1, sums[1], jnp.where(g == 2, sums[2], sums[3])))


def _pooled(u, g, rows):
    _, cnt = _window_terms(g, rows)
    return _window_sum(u, g, rows, forward=False) / cnt - u


def _pool_fwd(proj, w_pool, pool_scale, cat, n_seq, seq):
    n_grp = len(POOL_WINDOWS)
    u_off = 3 * (proj.shape[1] // 4) // LANES

    def body(u_ref, w_ref, s_ref, alias_ref, o_ref):
        del alias_ref
        g = pl.program_id(1)
        rows = lax.broadcasted_iota(jnp.int32, (seq, 1), 0)
        pooled = _pooled(u_ref[...].astype(F32), g, rows)
        y = _dot_nn(pooled.astype(BF16), w_ref[...].astype(BF16))
        o_ref[...] = (y * s_ref[...]).astype(BF16)

    return pl.pallas_call(
        body, name="pool_fwd", grid=(n_seq, n_grp),
        out_shape=jax.ShapeDtypeStruct(cat.shape, BF16),
        in_specs=[pl.BlockSpec((seq, LANES), lambda b, g: (b, u_off + g)),
                  pl.BlockSpec((None, POOL_GROUP_DIM, POOL_GROUP_DIM), lambda b, g: (g, 0, 0)),
                  pl.BlockSpec((1, POOL_GROUP_DIM), lambda b, g: (0, g)),
                  pl.BlockSpec(memory_space=pl.ANY)],
        out_specs=pl.BlockSpec((None, seq, LANES), lambda b, g: (1, b, g)),
        input_output_aliases={3: 0},
        compiler_params=_params(),
    )(proj, w_pool, pool_scale, cat)


def _pool_bwd(proj, dcat, w_pool, pool_scale, dqkv, n_seq, seq):
    n_grp = len(POOL_WINDOWS)
    width = proj.shape[1] // 4
    u_off = 3 * width // LANES
    dp_off = width // LANES

    def body(u_ref, dp_ref, w_ref, s_ref, alias_ref, du_ref, gw_ref, gs_ref):
        del alias_ref
        g = pl.program_id(0)
        b = pl.program_id(1)
        rows = lax.broadcasted_iota(jnp.int32, (seq, 1), 0)
        pooled = _pooled(u_ref[...].astype(F32), g, rows)
        pb = pooled.astype(BF16)
        wb = w_ref[...].astype(BF16)
        z = _dot_nn(pb, wb)
        dp = dp_ref[...].astype(F32)
        _acc(gs_ref, _colsum(dp * z), b == 0)
        dys = (dp * s_ref[...]).astype(BF16)
        _acc(gw_ref, _dot_tn(pb, dys), b == 0)
        dpooled = _dot_nt(dys, wb)
        _, cnt = _window_terms(g, rows)
        du = _window_sum(dpooled / cnt, g, rows, forward=True) - dpooled
        du_ref[...] = du.astype(BF16)

    t = proj.shape[0]
    return pl.pallas_call(
        body, name="pool_bwd", grid=(n_grp, n_seq),
        out_shape=(jax.ShapeDtypeStruct(dqkv.shape, BF16),
                   jax.ShapeDtypeStruct((n_grp, POOL_GROUP_DIM, POOL_GROUP_DIM), F32),
                   jax.ShapeDtypeStruct((1, n_grp * POOL_GROUP_DIM), F32)),
        in_specs=[pl.BlockSpec((seq, LANES), lambda g, b: (b, u_off + g)),
                  pl.BlockSpec((seq, LANES), lambda g, b: (b, dp_off + g)),
                  pl.BlockSpec((None, POOL_GROUP_DIM, POOL_GROUP_DIM), lambda g, b: (g, 0, 0)),
                  pl.BlockSpec((1, POOL_GROUP_DIM), lambda g, b: (0, g)),
                  pl.BlockSpec(memory_space=pl.ANY)],
        out_specs=(pl.BlockSpec((None, seq, LANES), lambda g, b: (3, b, g)),
                   pl.BlockSpec((None, POOL_GROUP_DIM, POOL_GROUP_DIM), lambda g, b: (g, 0, 0)),
                   pl.BlockSpec((1, POOL_GROUP_DIM), lambda g, b: (0, g))),
        input_output_aliases={4: 0},
        compiler_params=_params(),
    )(proj, dcat, w_pool, pool_scale, dqkv)


def _cond_fwd(c_all, w_cond, b_cols):
    n, _ = c_all.shape
    cols = w_cond.shape[1]

    def body(c_ref, w_ref, b_ref, o_ref):
        cv = c_ref[...]
        a = cv * jax.nn.sigmoid(cv)
        o_ref[...] = jnp.dot(a, w_ref[...], preferred_element_type=F32,
                             precision=lax.Precision.HIGHEST) + b_ref[...]

    return pl.pallas_call(
        body, name="cond_fwd", out_shape=jax.ShapeDtypeStruct((n, cols), F32),
        compiler_params=_params(),
    )(c_all, w_cond, b_cols)


def _cond_bwd(c_all, dmod_all, dmod_cols):
    n, d = c_all.shape
    cols = dmod_cols.shape[1]

    def body(c_ref, dm_ref, dmc_ref, gw_ref, gb_ref):
        cv = c_ref[...]
        a = cv * jax.nn.sigmoid(cv)
        gw_ref[...] = lax.dot_general(a, dmc_ref[...], (((0,), (0,)), ((), ())),
                                      preferred_element_type=F32, precision=lax.Precision.HIGHEST)
        gb_ref[...] = _colsum(dm_ref[...])

    return pl.pallas_call(
        body, name="cond_bwd",
        out_shape=(jax.ShapeDtypeStruct((d, cols), F32), jax.ShapeDtypeStruct((1, dmod_all.shape[1]), F32)),
        compiler_params=_params(),
    )(c_all, dmod_all, dmod_cols)


def _adamw_math(w, g, m, v):
    m = ADAM_B1 * m + (1.0 - ADAM_B1) * g
    v = ADAM_B2 * v + (1.0 - ADAM_B2) * (g * g)
    m_hat = m / (1.0 - ADAM_B1 ** ADAM_STEP)
    v_hat = v / (1.0 - ADAM_B2 ** ADAM_STEP)
    delta = -ADAM_LR * (m_hat / (jnp.sqrt(v_hat) + ADAM_EPS) + ADAM_WD * w)
    return delta, m, v


def _adamw(w, g, m, v, rows, name):
    r, cdim = w.shape

    def body(w_ref, g_ref, m_ref, v_ref, d_ref, nm_ref, nv_ref):
        d_ref[...], nm_ref[...], nv_ref[...] = _adamw_math(w_ref[...], g_ref[...], m_ref[...], v_ref[...])

    spec = pl.BlockSpec((rows, cdim), lambda i: (i, 0))
    sds = jax.ShapeDtypeStruct((r, cdim), F32)
    return pl.pallas_call(
        body, name=name, grid=(r // rows,), out_shape=(sds, sds, sds),
        in_specs=[spec] * 4, out_specs=(spec, spec, spec), compiler_params=_params(),
    )(w, g, m, v)


def _adamw_small(ws, gparts, ms, vs, name):
    n = len(ws)

    def body(*refs):
        w_r, g_r, m_r, v_r = refs[:n], refs[n:2 * n], refs[2 * n:3 * n], refs[3 * n:4 * n]
        outs = refs[4 * n:]
        for i in range(n):
            g = g_r[i][0]
            for dev in range(1, g_r[i].shape[0]):
                g = g + g_r[i][dev]
            delta, m, v = _adamw_math(w_r[i][...], g, m_r[i][...], v_r[i][...])
            outs[i][...] = g
            outs[n + i][...] = delta
            outs[2 * n + i][...] = m
            outs[3 * n + i][...] = v

    sds = [jax.ShapeDtypeStruct(w.shape, F32) for w in ws]
    return pl.pallas_call(
        body, name=name, out_shape=tuple(sds * 4), compiler_params=_params(),
    )(*ws, *gparts, *ms, *vs)


def kernel(x, c, w_cond, b_cond, g_mix_pre, g_mix_post, w_in, w_pool, pool_scale, w_out, g_ffn_pre, g_ffn_post, w_gate, w_up, w_down, loss_target, m_w_cond, m_b_cond, m_g_mix_pre, m_g_mix_post, m_w_in, m_w_pool, m_pool_scale, m_w_out, m_g_ffn_pre, m_g_ffn_post, m_w_gate, m_w_up, m_w_down, v_w_cond, v_b_cond, v_g_mix_pre, v_g_mix_post, v_w_in, v_w_pool, v_pool_scale, v_w_out, v_g_ffn_pre, v_g_ffn_post, v_w_gate, v_w_up, v_w_down):
    n_seq, seq, d = x.shape
    t = n_seq * seq
    xi, yi, ci = _mesh_pos()
    me = 4 * xi + 2 * yi + ci
    x2 = x.reshape(t, d)
    tgt2 = loss_target.reshape(t, d)
    in_rows = w_in.shape[2]
    out_rows = w_out.shape[1]
    ff_rows = w_gate.shape[2]
    ff = N_DEV * ff_rows
    cond_cols = w_cond.shape[2]

    win_t = w_in[0].T.astype(BF16)
    wout_s = w_out[0].astype(BF16)
    wg_t = w_gate[0].T.astype(BF16)
    wu_t = w_up[0].T.astype(BF16)
    wd_s = w_down[0].astype(BF16)
    c_all, win_g = _all_gather(
        [c, win_t],
        [jax.ShapeDtypeStruct((N_DEV, n_seq, d), F32), jax.ShapeDtypeStruct((N_DEV, in_rows, d), BF16)],
        [(0, ()), (1, ())], "ag_c_win")
    c_all = c_all.reshape(N_DEV * n_seq, d)
    win_full = win_g.reshape(N_DEV * in_rows, d)

    b_cols = lax.dynamic_slice_in_dim(b_cond, me * cond_cols, cond_cols, axis=1)
    mod_cols = _cond_fwd(c_all, w_cond[0], b_cols)
    (mod_g,) = _all_gather([mod_cols], [jax.ShapeDtypeStruct((N_DEV,) + mod_cols.shape, F32)], [(0, ())], "ag_mod")
    mod_mine = lax.dynamic_slice_in_dim(mod_g, me * n_seq, n_seq, axis=1)
    mod = jnp.transpose(mod_mine, (1, 0, 2)).reshape(n_seq, N_MOD, d)

    h1 = _pre_mix(x2, g_mix_pre, mod, seq)
    proj = _matmul(h1, win_full, "nt", BF16, 1024, 512, d, "proj")
    tq = ATT_TILE
    ids = jnp.arange(tq)
    tri_after = jnp.tile(-(ids[:, None] >= ids[None, :]).astype(BF16), (2, 1))
    tri_incl = (ids[:, None] <= ids[None, :]).astype(BF16)
    attn, cstats, wout_g, wgu_g, wd_g = _attn_fwd(
        proj, tri_after, n_seq, seq, [wout_s, wg_t, wu_t, wd_s],
        [jax.ShapeDtypeStruct((N_DEV, out_rows, d), BF16), jax.ShapeDtypeStruct((2, N_DEV, ff_rows, d), BF16),
         jax.ShapeDtypeStruct((N_DEV, ff_rows, d), BF16)],
        [(0, ()), (1, (0,)), (1, (1,)), (2, ())])
    wout_full = wout_g.reshape(N_DEV * out_rows, d)
    wgu_full = wgu_g.reshape(2, ff, d)
    wd_full = wd_g.reshape(ff, d)
    cat = _pool_fwd(proj, w_pool[0], pool_scale, attn, n_seq, seq)
    tok_f32, tok_bf16 = jax.ShapeDtypeStruct((t, d), F32), jax.ShapeDtypeStruct((t, d), BF16)
    seq_sds, vec_sds = jax.ShapeDtypeStruct((n_seq, 1, d), F32), jax.ShapeDtypeStruct((1, d), F32)
    mix, x1, h2 = _matmul_rows(
        cat, wout_full.reshape(2, d // 2, d), ROW_TILE, d // 2, seq, "mix_mid", _mid_epilogue,
        [x2, g_mix_post, g_ffn_pre, mod], ["tok", "vec", "vec", "mod"],
        [tok_f32, tok_f32, tok_bf16], ["tok", "tok", "tok"])
    gu, act = _ffn_up(h2, wgu_full, 512, ff // 2)
    loss_sum, dy, df, dgate_f, gg_ffn_post = _matmul_rows(
        act, wd_full, ROW_TILE, ff, seq, "ffn_down_post", _post_epilogue,
        [x1, tgt2, g_ffn_post, mod], ["tok", "tok", "vec", "mod"],
        [jax.ShapeDtypeStruct((1, LANES), F32), tok_f32, tok_bf16, seq_sds, vec_sds],
        ["loss", "tok", "tok", "seq", "vec"])

    dgu = _ffn_act_bwd(df, wd_full, gu, 512, ff // 2)
    gwd, gwd_b = _matmul(act, df, "tn", F32, ff // 2, d, 1024, "grad_w_down", bf16_copy=True)
    gwgu, gwgu_b = _matmul(dgu, h2, "tn", F32, ff // 2, d, 1024, "grad_w_gate_up", bf16_copy=True)
    dx1, dmix, dshift_f, dscale_f, dgate_m, gg_ffn_pre, gg_mix_post = _matmul_rows(
        dgu, wgu_full, ROW_TILE, ff, seq, "dh2_bwd_mid", _bwd_mid_epilogue,
        [dy, x1, mix, g_ffn_pre, g_mix_post, mod], ["tok", "tok", "tok", "vec", "vec", "mod"],
        [tok_f32, tok_bf16, seq_sds, seq_sds, seq_sds, vec_sds, vec_sds],
        ["tok", "tok", "seq", "seq", "seq", "vec", "vec"])
    dcat = _matmul(dmix, wout_full, "nt", BF16, 1024, 512, d, "dcat")
    gwout, gwout_b = _matmul(cat, dmix, "tn", F32, d // 2, d, 1024, "grad_w_out", bf16_copy=True)
    dqkv, rv_wgu, rv_wd, rv_wout = _attn_bwd(
        proj, dcat, cstats, tri_after, tri_incl, n_seq, seq,
        [gwgu_b.reshape(2, N_DEV, ff_rows, d), gwd_b.reshape(1, N_DEV, ff_rows, d),
         gwout_b.reshape(1, N_DEV, out_rows, d)])
    dproj, gw_pool, gs_pool = _pool_bwd(proj, dcat, w_pool[0], pool_scale, dqkv, n_seq, seq)
    gwin, gwin_b = _matmul(dproj, h1, "tn", F32, d // 2, d, 1024, "grad_w_in", bf16_copy=True)
    grad_x, dshift_m, dscale_m, gg_mix_pre, rv_win = _matmul_rows(
        dproj, win_full.reshape(4, d // 2, d), ROW_TILE, d // 2, seq, "dh1_bwd_pre", _bwd_pre_epilogue,
        [dx1, x2, g_mix_pre, mod], ["tok", "tok", "vec", "mod"],
        [tok_f32, seq_sds, seq_sds, vec_sds], ["tok", "seq", "seq", "vec"],
        rs_sends=[gwin_b.reshape(1, N_DEV, in_rows, d)])

    r_wgu = _rs_final(gwgu.reshape(2, N_DEV, ff_rows, d), rv_wgu, "rs_final_gate_up")
    r_wd = _rs_final(gwd.reshape(1, N_DEV, ff_rows, d), rv_wd, "rs_final_down")
    r_wout = _rs_final(gwout.reshape(1, N_DEV, out_rows, d), rv_wout, "rs_final_out")
    r_win = _rs_final(gwin.reshape(1, N_DEV, in_rows, d), rv_win, "rs_final_in")
    grad_w_in = r_win[0].T
    grad_w_out = r_wout[0]
    grad_w_down = r_wd[0]

    dmod = jnp.concatenate([dshift_m, dscale_m, dgate_m, dshift_f, dscale_f, dgate_f], axis=1)
    small = jnp.concatenate([gg_mix_pre, gg_mix_post, gg_ffn_pre, gg_ffn_post,
                             jnp.pad(gs_pool, ((0, 0), (0, d - gs_pool.shape[1]))),
                             jnp.pad(loss_sum, ((0, 0), (0, d - loss_sum.shape[1]))), jnp.zeros((2, d), F32),
                             gw_pool.reshape(-1, d), dmod.reshape(n_seq * N_MOD, d)], axis=0)
    n_gw = gw_pool.size // d
    (small_g,) = _all_gather([small], [jax.ShapeDtypeStruct((N_DEV,) + small.shape, F32)], [(0, ())], "ag_small")
    loss = jnp.sum(small_g[:, 5, 0]) * (0.5 / d)
    dmod_all = small_g[:, 8 + n_gw:, :].reshape(N_DEV * n_seq, N_MOD * d)
    dmod_cols = lax.dynamic_slice_in_dim(dmod_all, me * cond_cols, cond_cols, axis=1)
    grad_w_cond, grad_b_cond = _cond_bwd(c_all, dmod_all, dmod_cols)

    small_ws = [g_mix_pre, g_mix_post, g_ffn_pre, g_ffn_post, pool_scale, w_pool.reshape(-1, POOL_GROUP_DIM)]
    small_ms = [m_g_mix_pre, m_g_mix_post, m_g_ffn_pre, m_g_ffn_post, m_pool_scale, m_w_pool.reshape(-1, POOL_GROUP_DIM)]
    small_vs = [v_g_mix_pre, v_g_mix_post, v_g_ffn_pre, v_g_ffn_post, v_pool_scale, v_w_pool.reshape(-1, POOL_GROUP_DIM)]
    small_gparts = [small_g[:, 0:1, :], small_g[:, 1:2, :], small_g[:, 2:3, :], small_g[:, 3:4, :],
                    small_g[:, 4:5, :pool_scale.shape[1]],
                    small_g[:, 8:8 + n_gw, :].reshape(N_DEV, -1, POOL_GROUP_DIM)]
    so = _adamw_small(small_ws, small_gparts, small_ms, small_vs, "adamw_small")
    ns = len(small_ws)
    sg, sdl, sm, sv = so[:ns], so[ns:2 * ns], so[2 * ns:3 * ns], so[3 * ns:]
    pool_shape = w_pool.shape
    fix = lambda lst: [lst[0], lst[1], lst[2], lst[3], lst[4], lst[5].reshape(pool_shape)]
    sg, sdl, sm, sv = fix(sg), fix(sdl), fix(sm), fix(sv)

    def big(w, g, m, v, rows, name):
        dl, nm, nv = _adamw(w[0], g, m[0], v[0], rows, name)
        return g[None], dl[None], nm[None], nv[None]

    o_cond = big(w_cond, grad_w_cond, m_w_cond, v_w_cond, 256, "adamw_w_cond")
    o_bcond = _adamw(b_cond, grad_b_cond, m_b_cond, v_b_cond, 1, "adamw_b_cond")
    o_bcond = (grad_b_cond,) + tuple(o_bcond)
    o_in = big(w_in, grad_w_in, m_w_in, v_w_in, 256, "adamw_w_in")
    o_out = big(w_out, grad_w_out, m_w_out, v_w_out, out_rows, "adamw_w_out")
    def big_t(w, g_t, m, v, name):
        outs = _adamw(w[0].T, g_t, m[0].T, v[0].T, g_t.shape[0], name)
        return tuple(o.T[None] for o in (g_t,) + tuple(outs))

    o_gate = big_t(w_gate, r_wgu[0], m_w_gate, v_w_gate, "adamw_w_gate")
    o_up = big_t(w_up, r_wgu[1], m_w_up, v_w_up, "adamw_w_up")
    o_down = big(w_down, grad_w_down, m_w_down, v_w_down, ff_rows, "adamw_w_down")

    def pick(k):
        small_k = [sg, sdl, sm, sv][k]
        return [o_cond[k], o_bcond[k], small_k[0], small_k[1], o_in[k], small_k[5], small_k[4], o_out[k],
                small_k[2], small_k[3], o_gate[k], o_up[k], o_down[k]]

    return (loss, grad_x.reshape(n_seq, seq, d), *pick(0), *pick(1), *pick(2), *pick(3))
```

```python
import functools
import math

import jax
import jax.numpy as jnp
from jax import lax
from jax.experimental import pallas as pl
from jax.experimental.pallas import tpu as pltpu

F32 = jnp.float32
BF16 = jnp.bfloat16
MESH = pl.DeviceIdType.MESH

N_DEV = 8
HEAD_DIM = 64
LANES = 128
POOL_WINDOWS = (2, 4, 8, 16)
POOL_GROUP_DIM = 128
N_MOD = 6
EPS = 1e-6
ATT_TILE = 256
ATT_PAIRS = 2
VMEM_LIMIT = 56 * 1024 * 1024

ADAM_LR = 0.001
ADAM_B1 = 0.9
ADAM_B2 = 0.999
ADAM_EPS = 1e-08
ADAM_WD = 0.01
ADAM_STEP = 10


def _params(**kw):
    return pltpu.CompilerParams(vmem_limit_bytes=VMEM_LIMIT, **kw)


def _dot_nn(a, b):
    return jnp.dot(a, b, preferred_element_type=F32)


def _dot_nt(a, b):
    return lax.dot_general(a, b, (((1,), (1,)), ((), ())), preferred_element_type=F32)


def _dot_tn(a, b):
    return lax.dot_general(a, b, (((0,), (0,)), ((), ())), preferred_element_type=F32)


def _mesh_pos():
    return lax.axis_index("x"), lax.axis_index("y"), lax.axis_index("c")


def _ag_phases(dests, src, outs, send_sems, recv_sems, local_sems):
    n = len(src)
    x, y, c = _mesh_pos()
    me, sibling = (x, y, c), (x, y, 1 - c)
    chips = [(1 - x, y), (x, 1 - y), (1 - x, 1 - y)]

    def slot(i, dev):
        oi, prefix = dests[i]
        px, py, pc = dev
        return outs[oi].at[prefix + (4 * px + 2 * py + pc,)]

    def copy(i, k, block, to, from_src=False):
        return pltpu.make_async_remote_copy(
            src_ref=src[i] if from_src else slot(i, block), dst_ref=slot(i, block),
            send_sem=send_sems.at[i, k], recv_sem=recv_sems.at[i, k],
            device_id=to, device_id_type=MESH)

    def mine(i):
        return pltpu.make_async_copy(src[i], slot(i, me), local_sems.at[i])

    def first(i):
        return [copy(i, 0, me, sibling, from_src=True)] + [
            copy(i, 1 + j, me, (*chip, c), from_src=True) for j, chip in enumerate(chips)]

    def passed(i, j):
        return copy(i, 4 + j, (*chips[j], c), sibling)

    def start():
        for i in range(n):
            mine(i).start()
        for i in range(n):
            for cp in first(i):
                cp.start()

    def forward():
        for j, chip in enumerate(chips):
            for i in range(n):
                copy(i, 1 + j, (*chip, c), me).wait_recv()
                passed(i, j).start()

    def finish():
        for i in range(n):
            copy(i, 0, sibling, me).wait_recv()
            for j, chip in enumerate(chips):
                copy(i, 4 + j, (*chip, 1 - c), me).wait_recv()
        for i in range(n):
            for cp in first(i) + [passed(i, j) for j in range(3)]:
                cp.wait_send()
            mine(i).wait()

    return start, forward, finish


def _ag_scratch(n):
    return [pltpu.SemaphoreType.DMA((n, 7)), pltpu.SemaphoreType.DMA((n, 7)), pltpu.SemaphoreType.DMA((n,))]


def _all_gather(srcs, out_shapes, dests, name):
    n = len(srcs)

    def body(*refs):
        src = refs[:n]
        outs = refs[n:n + len(out_shapes)]
        start, forward, finish = _ag_phases(dests, src, outs, *refs[n + len(out_shapes):])
        start()
        forward()
        finish()

    any_spec = pl.BlockSpec(memory_space=pl.ANY)
    return pl.pallas_call(
        body, name=name,
        out_shape=tuple(out_shapes),
        in_specs=[any_spec] * n,
        out_specs=tuple([any_spec] * len(out_shapes)),
        scratch_shapes=_ag_scratch(n),
    )(*srcs)


def _rs_phases(shapes, src, dst, send_sems, recv_sems):
    x, y, c = _mesh_pos()

    def copies():
        out = []
        n = 0
        for i, shp in enumerate(shapes):
            for m in range(shp[0]):
                for k in range(1, N_DEV):
                    px, py, pc = x ^ (k >> 2), y ^ ((k >> 1) & 1), c ^ (k & 1)
                    out.append(pltpu.make_async_remote_copy(
                        src_ref=src[i].at[m, 4 * px + 2 * py + pc], dst_ref=dst[i].at[m, k - 1],
                        send_sem=send_sems.at[n], recv_sem=recv_sems.at[n],
                        device_id=(px, py, pc), device_id_type=MESH))
                    n += 1
        return out

    def start():
        for cp in copies():
            cp.start()

    def finish():
        for cp in copies():
            cp.wait_send()
        for cp in copies():
            cp.wait_recv()

    return start, finish


def _rs_out(sends):
    return [jax.ShapeDtypeStruct((s.shape[0], N_DEV - 1) + s.shape[2:], s.dtype) for s in sends]


def _rs_scratch(sends):
    total = sum((N_DEV - 1) * s.shape[0] for s in sends)
    return [pltpu.SemaphoreType.DMA((total,)), pltpu.SemaphoreType.DMA((total,))]


def _rs_final(mine, recv, name):
    m_n, _, r, cdim = mine.shape
    x, y, c = _mesh_pos()
    me = jnp.reshape(4 * x + 2 * y + c, (1,)).astype(jnp.int32)

    def body(me_ref, p_ref, r_ref, o_ref):
        del me_ref
        s = p_ref[...]
        for k in range(N_DEV - 1):
            s = s + r_ref[k].astype(F32)
        o_ref[...] = s

    return pl.pallas_call(
        body, name=name, out_shape=jax.ShapeDtypeStruct((m_n, r, cdim), F32),
        grid_spec=pltpu.PrefetchScalarGridSpec(
            num_scalar_prefetch=1, grid=(m_n,),
            in_specs=[pl.BlockSpec((None, None, r, cdim), lambda m, s: (m, s[0], 0, 0)),
                      pl.BlockSpec((None, N_DEV - 1, r, cdim), lambda m, s: (m, 0, 0, 0))],
            out_specs=pl.BlockSpec((None, r, cdim), lambda m, s: (m, 0, 0))),
        compiler_params=_params(),
    )(me, mine, recv)


def _matmul(a, b, mode, out_dtype, tm, tn, tk, name, bf16_copy=False, rs_sends=()):
    ga = a.shape[0] if a.ndim == 3 else None
    gb = b.shape[0] if b.ndim == 3 else None
    a2, b2 = a.shape[-2:], b.shape[-2:]
    if mode == "nn":
        (m, k), n = a2, b2[1]
    elif mode == "nt":
        (m, k), n = a2, b2[0]
    else:
        (k, m), n = a2, b2[1]
    assert m % tm == 0 and n % tn == 0 and k % tk == 0, (name, m, n, k)
    nk = k // tk
    g_n = ga or 1
    batch_out = mode == "tn" and ga is not None
    n_red = nk if batch_out else nk * g_n
    dot = {"nn": _dot_nn, "nt": _dot_nt, "tn": _dot_tn}[mode]
    acc_in_out = out_dtype == F32

    n_rs = len(rs_sends)
    rs_shapes = [r.shape for r in rs_sends]
    n_out = 2 if bf16_copy else 1
    assert not bf16_copy or acc_in_out

    def body(a_ref, b_ref, *rest):
        rs_src, rest = rest[:n_rs], rest[n_rs:]
        o_ref = rest[0]
        copy_ref = rest[1] if bf16_copy else None
        rs_dst, scratch = rest[n_out:n_out + n_rs], rest[n_out + n_rs:]
        if n_rs:
            rs_start, rs_finish = _rs_phases(rs_shapes, rs_src, rs_dst, *scratch[-2:])
            first = functools.reduce(jnp.logical_and, [pl.program_id(ax) == 0 for ax in range(4)])
            last = functools.reduce(jnp.logical_and, [pl.program_id(ax) == grid[ax] - 1 for ax in range(4)])
            pl.when(first)(rs_start)
        p = dot(a_ref[...], b_ref[...])
        kk = pl.program_id(3) if batch_out else pl.program_id(2) * nk + pl.program_id(3)
        if n_red == 1:
            o_ref[...] = p.astype(out_dtype)
            if bf16_copy:
                copy_ref[...] = p.astype(BF16)
        else:
            acc = o_ref if acc_in_out else scratch[0]

            @pl.when(kk == 0)
            def _():
                acc[...] = p

            @pl.when(kk > 0)
            def _():
                acc[...] += p

            @pl.when(kk == n_red - 1)
            def _():
                if not acc_in_out:
                    o_ref[...] = acc[...].astype(out_dtype)
                if bf16_copy:
                    copy_ref[...] = acc[...].astype(BF16)

        if n_rs:
            pl.when(last)(rs_finish)

    def order(ids):
        return ids if batch_out else (ids[2], ids[0], ids[1], ids[3])

    def a_idx(*ids):
        g, i, j, kq = order(ids)
        blk = {"nn": (i, kq), "nt": (i, kq), "tn": (kq, i)}[mode]
        return (g,) + blk if ga is not None else blk

    def b_idx(*ids):
        g, i, j, kq = order(ids)
        blk = {"nn": (kq, j), "nt": (j, kq), "tn": (kq, j)}[mode]
        return (g,) + blk if gb is not None else blk

    def o_idx(*ids):
        g, i, j, kq = order(ids)
        return (g, i, j) if batch_out else (i, j)

    a_blk = {"nn": (tm, tk), "nt": (tm, tk), "tn": (tk, tm)}[mode]
    b_blk = {"nn": (tk, tn), "nt": (tn, tk), "tn": (tk, tn)}[mode]
    if ga is not None:
        a_blk = (None,) + a_blk
    if gb is not None:
        b_blk = (None,) + b_blk
    if batch_out:
        out_shape = jax.ShapeDtypeStruct((g_n, m, n), out_dtype)
        o_blk = (None, tm, tn)
        grid = (g_n, m // tm, n // tn, nk)
    else:
        out_shape = jax.ShapeDtypeStruct((m, n), out_dtype)
        o_blk = (tm, tn)
        grid = (m // tm, n // tn, g_n, nk)
    scratch = [] if (acc_in_out or n_red == 1) else [pltpu.VMEM((tm, tn), F32)]
    any_spec = pl.BlockSpec(memory_space=pl.ANY)
    out_shapes = [out_shape] + ([jax.ShapeDtypeStruct(out_shape.shape, BF16)] if bf16_copy else [])
    res = pl.pallas_call(
        body, name=name, out_shape=tuple(out_shapes + _rs_out(rs_sends)), grid=grid,
        in_specs=[pl.BlockSpec(a_blk, a_idx), pl.BlockSpec(b_blk, b_idx)] + [any_spec] * n_rs,
        out_specs=tuple([pl.BlockSpec(o_blk, o_idx)] * n_out + [any_spec] * n_rs),
        scratch_shapes=scratch + (_rs_scratch(rs_sends) if n_rs else []), compiler_params=_params(),
    )(a, b, *rs_sends)
    return res if len(res) > 1 else res[0]


EW_TILE = 256
ROW_TILE = 512
EPILOGUE_CHUNKS = 8
MXU_WIDTH = 256


def _rms(v):
    return lax.rsqrt(jnp.mean(v * v, axis=-1, keepdims=True) + EPS)


def _rms_bwd(dhat, vh, r):
    return r * (dhat - vh * jnp.mean(dhat * vh, axis=-1, keepdims=True))


def _tok_spec(tm, d):
    return pl.BlockSpec((tm, d), lambda i: (i, 0))


def _vec_spec(d):
    return pl.BlockSpec((1, d), lambda i: (0, 0))


def _mod_spec(tiles_per_seq, d):
    return pl.BlockSpec((None, N_MOD, d), lambda i: (i // tiles_per_seq, 0, 0))


def _seq_acc_spec(tiles_per_seq, d):
    return pl.BlockSpec((None, 1, d), lambda i: (i // tiles_per_seq, 0, 0))


def _acc(ref, val, first):
    if first is False:
        ref[...] += val
        return

    @pl.when(first)
    def _():
        ref[...] = val

    @pl.when(jnp.logical_not(first))
    def _():
        ref[...] += val


def _colsum(v):
    return jnp.sum(v, axis=0, keepdims=True)


def _pre_mix(x2, g_pre, mod, seq):
    t, d = x2.shape
    tm = EW_TILE

    def body(x_ref, g_ref, mod_ref, h_ref):
        xv = x_ref[...]
        n = xv * _rms(xv) * g_ref[...]
        h_ref[...] = (n * (1.0 + mod_ref[1:2, :]) + mod_ref[0:1, :]).astype(BF16)

    return pl.pallas_call(
        body, name="pre_mix", out_shape=jax.ShapeDtypeStruct((t, d), BF16), grid=(t // tm,),
        in_specs=[_tok_spec(tm, d), _vec_spec(d), _mod_spec(seq // tm, d)],
        out_specs=_tok_spec(tm, d), compiler_params=_params(),
    )(x2, g_pre, mod)


def _matmul_rows(a, b, tm, tk, seq, name, epilogue, ep_in, ep_in_kinds, ep_out, ep_out_kinds, rs_sends=()):
    ga = a.shape[0] if a.ndim == 3 else None
    (m, k), n = a.shape[-2:], b.shape[-1]
    g_n = ga or 1
    nk = k // tk
    n_red = g_n * nk
    tps = seq // tm
    n_i = m // tm
    grid = (n_i + 1, g_n, nk)
    n_rs = len(rs_sends)
    rs_shapes = [r.shape for r in rs_sends]
    n_in, n_out = len(ep_in), len(ep_out)
    per_step = -(-EPILOGUE_CHUNKS // n_red)
    n_chunks = per_step * n_red
    n_cols = min(per_step, n // MXU_WIDTH)
    rc, cw = tm // n_chunks, n // n_cols

    def prev(i):
        return jnp.maximum(i - 1, 0)

    def spec(kind):
        return {"tok": pl.BlockSpec((tm, n), lambda i, g, kq: (prev(i), 0)),
                "vec": pl.BlockSpec((1, n), lambda i, g, kq: (0, 0)),
                "mod": pl.BlockSpec((None, N_MOD, n), lambda i, g, kq: (prev(i) // tps, 0, 0)),
                "seq": pl.BlockSpec((None, 1, n), lambda i, g, kq: (prev(i) // tps, 0, 0)),
                "loss": pl.BlockSpec((1, LANES), lambda i, g, kq: (0, 0))}[kind]

    def body(a_ref, b_ref, *rest):
        in_refs, rest = rest[:n_in], rest[n_in:]
        rs_src, rest = rest[:n_rs], rest[n_rs:]
        out_refs, rest = rest[:n_out], rest[n_out:]
        rs_dst, rest = rest[:n_rs], rest[n_rs:]
        acc, fin = rest[:2]
        i, kk = pl.program_id(0), pl.program_id(1) * nk + pl.program_id(2)
        if n_rs:
            rs_start, rs_finish = _rs_phases(rs_shapes, rs_src, rs_dst, *rest[2:])
            pl.when(jnp.logical_and(i == 0, kk == 0))(rs_start)

        def step(s, with_epilogue, with_matmul):
            last = s == n_red - 1
            parts = []
            cols_done = 0
            for c in range(per_step):
                if with_epilogue:
                    chunk = s * per_step + c
                    rows = pl.ds(chunk * rc, rc)
                    epilogue(fin[rows, :], i - 1, tps, in_refs, out_refs, rows, chunk)
                while with_matmul and cols_done < (c + 1) * n_cols // per_step:
                    cols = slice(cols_done * cw, (cols_done + 1) * cw)
                    cols_done += 1
                    p = _dot_nn(a_ref[...], b_ref[:, cols])
                    if s == 0 and not last:
                        acc[:, cols] = p
                    elif not last:
                        acc[:, cols] += p
                    else:
                        parts.append((cols, p if s == 0 else acc[:, cols] + p))
            for cols, v in parts:
                fin[:, cols] = v

        for s in range(n_red):
            at = kk == s
            pl.when(jnp.logical_and(at, i == 0))(functools.partial(step, s, False, True))
            pl.when(jnp.logical_and(at, jnp.logical_and(i > 0, i < n_i)))(functools.partial(step, s, True, True))
            pl.when(jnp.logical_and(at, i == n_i))(functools.partial(step, s, True, False))

        if n_rs:
            pl.when(jnp.logical_and(i == n_i, kk == n_red - 1))(rs_finish)

    def row(i):
        return jnp.minimum(i, n_i - 1)

    a_blk = (tm, tk) if ga is None else (None, tm, tk)
    b_blk = (tk, n) if ga is None else (None, tk, n)
    a_idx = (lambda i, g, kq: (row(i), kq)) if ga is None else (lambda i, g, kq: (g, row(i), kq))
    b_idx = (lambda i, g, kq: (kq, 0)) if ga is None else (lambda i, g, kq: (g, kq, 0))
    any_spec = pl.BlockSpec(memory_space=pl.ANY)
    res = pl.pallas_call(
        body, name=name, grid=grid, out_shape=tuple(list(ep_out) + _rs_out(rs_sends)),
        in_specs=[pl.BlockSpec(a_blk, a_idx), pl.BlockSpec(b_blk, b_idx)] + [spec(kd) for kd in ep_in_kinds]
        + [any_spec] * n_rs,
        out_specs=tuple([spec(kd) for kd in ep_out_kinds] + [any_spec] * n_rs),
        scratch_shapes=[pltpu.VMEM((tm, n), F32)] * 2 + (_rs_scratch(rs_sends) if n_rs else []),
        compiler_params=_params(),
    )(a, b, *ep_in, *rs_sends)
    return res


def _first(cond, chunk):
    return cond if chunk == 0 else False


def _mid_epilogue(mv, i, tps, in_refs, out_refs, rows, chunk):
    x_ref, gpost_ref, gpre_ref, mod_ref = in_refs
    mix_ref, x1_ref, h2_ref = out_refs
    mix_ref[rows, :] = mv
    x1 = x_ref[rows, :] + mod_ref[2:3, :] * (mv * _rms(mv) * gpost_ref[...])
    x1_ref[rows, :] = x1
    n = x1 * _rms(x1) * gpre_ref[...]
    h2_ref[rows, :] = (n * (1.0 + mod_ref[4:5, :]) + mod_ref[3:4, :]).astype(BF16)


def _post_epilogue(fv, i, tps, in_refs, out_refs, rows, chunk):
    x1_ref, tgt_ref, g_ref, mod_ref = in_refs
    loss_ref, dy_ref, df_ref, dgate_ref, gg_ref = out_refs
    d = fv.shape[1]
    r = _rms(fv)
    fh = fv * r
    nf = fh * g_ref[...]
    gate = mod_ref[5:6, :]
    err = x1_ref[rows, :] + gate * nf - tgt_ref[rows, :]
    _acc(loss_ref, jnp.sum(_colsum(err * err), axis=1, keepdims=True) * jnp.ones((1, LANES), F32),
         _first(i == 0, chunk))
    dy = err * (1.0 / d)
    dy_ref[rows, :] = dy
    _acc(dgate_ref, _colsum(dy * nf), _first(i % tps == 0, chunk))
    dn = dy * gate
    _acc(gg_ref, _colsum(dn * fh), _first(i == 0, chunk))
    df_ref[rows, :] = _rms_bwd(dn * g_ref[...], fh, r).astype(BF16)


def _bwd_mid_epilogue(dh, i, tps, in_refs, out_refs, rows, chunk):
    dy_ref, x1_ref, mix_ref, gpre_ref, gpost_ref, mod_ref = in_refs
    dx1_ref, dmix_ref, dshift_ref, dscale_ref, dgate_ref, ggpre_ref, ggpost_ref = out_refs
    seq_first, first = _first(i % tps == 0, chunk), _first(i == 0, chunk)
    x1 = x1_ref[rows, :]
    r = _rms(x1)
    xh = x1 * r
    gpre = gpre_ref[...]
    _acc(dshift_ref, _colsum(dh), seq_first)
    _acc(dscale_ref, _colsum(dh * xh * gpre), seq_first)
    dn = dh * (1.0 + mod_ref[4:5, :])
    _acc(ggpre_ref, _colsum(dn * xh), first)
    dx1 = dy_ref[rows, :] + _rms_bwd(dn * gpre, xh, r)
    dx1_ref[rows, :] = dx1
    mv = mix_ref[rows, :]
    rm = _rms(mv)
    mh = mv * rm
    gpost = gpost_ref[...]
    _acc(dgate_ref, _colsum(dx1 * mh * gpost), seq_first)
    dnm = dx1 * mod_ref[2:3, :]
    _acc(ggpost_ref, _colsum(dnm * mh), first)
    dmix_ref[rows, :] = _rms_bwd(dnm * gpost, mh, rm).astype(BF16)


def _bwd_pre_epilogue(dh, i, tps, in_refs, out_refs, rows, chunk):
    dx1_ref, x_ref, g_ref, mod_ref = in_refs
    gx_ref, dshift_ref, dscale_ref, gg_ref = out_refs
    seq_first = _first(i % tps == 0, chunk)
    xv = x_ref[rows, :]
    r = _rms(xv)
    xh = xv * r
    g = g_ref[...]
    _acc(dshift_ref, _colsum(dh), seq_first)
    _acc(dscale_ref, _colsum(dh * xh * g), seq_first)
    dn = dh * (1.0 + mod_ref[1:2, :])
    _acc(gg_ref, _colsum(dn * xh), _first(i == 0, chunk))
    gx_ref[rows, :] = dx1_ref[rows, :] + _rms_bwd(dn * g, xh, r)


def _ffn_up(h2, wgu, tm, tn):
    t, d = h2.shape
    f = wgu.shape[1]

    def body(h_ref, w_ref, gu_ref, act_ref):
        h = h_ref[...]
        g = _dot_nt(h, w_ref[0])
        u = _dot_nt(h, w_ref[1])
        gu_ref[0] = g.astype(BF16)
        gu_ref[1] = u.astype(BF16)
        act_ref[...] = (g * jax.nn.sigmoid(g) * u).astype(BF16)

    return pl.pallas_call(
        body, name="ffn_up", grid=(t // tm, f // tn),
        out_shape=(jax.ShapeDtypeStruct((2, t, f), BF16), jax.ShapeDtypeStruct((t, f), BF16)),
        in_specs=[pl.BlockSpec((tm, d), lambda i, j: (i, 0)), pl.BlockSpec((2, tn, d), lambda i, j: (0, j, 0))],
        out_specs=(pl.BlockSpec((2, tm, tn), lambda i, j: (0, i, j)), pl.BlockSpec((tm, tn), lambda i, j: (i, j))),
        compiler_params=_params(),
    )(h2, wgu)


def _ffn_act_bwd(df, wd, gu, tm, tn):
    t, d = df.shape
    f = wd.shape[0]

    def body(df_ref, w_ref, gu_ref, dgu_ref):
        da = _dot_nt(df_ref[...], w_ref[...])
        g = gu_ref[0].astype(F32)
        u = gu_ref[1].astype(F32)
        s = jax.nn.sigmoid(g)
        silu = g * s
        dgu_ref[0] = (da * u * (s + silu * (1.0 - s))).astype(BF16)
        dgu_ref[1] = (da * silu).astype(BF16)

    return pl.pallas_call(
        body, name="ffn_act_bwd", grid=(t // tm, f // tn),
        out_shape=jax.ShapeDtypeStruct((2, t, f), BF16),
        in_specs=[pl.BlockSpec((tm, d), lambda i, j: (i, 0)), pl.BlockSpec((tn, d), lambda i, j: (j, 0)),
                  pl.BlockSpec((2, tm, tn), lambda i, j: (0, i, j))],
        out_specs=pl.BlockSpec((2, tm, tn), lambda i, j: (0, i, j)),
        compiler_params=_params(),
    )(df, wd, gu)


SIGN_BIT = 0x80000000
Q_SCALE = 1.0 / math.sqrt(HEAD_DIM)


def _split_dot(v, tri2):
    hi = v.astype(BF16)
    lo = (v - hi.astype(F32)).astype(BF16)
    return _dot_nn(jnp.concatenate([hi, lo], axis=1), tri2)


def _sb_tile(qs, k2, mask, ntri2, cur):
    z = _dot_nt(qs, k2)
    neg_abs = lax.bitcast_convert_type(lax.bitcast_convert_type(z, jnp.uint32) | jnp.uint32(SIGN_BIT), F32)
    sp = jnp.maximum(z, 0.0) + jnp.log(1.0 + jnp.exp(neg_abs))
    if mask is not None:
        sp = jnp.where(mask, sp, 0.0)
    w = jnp.exp(z + _split_dot(sp, ntri2) + cur)
    if mask is not None:
        w = jnp.where(mask, w, 0.0)
    return z, sp, w


def _softplus(z):
    neg_abs = lax.bitcast_convert_type(lax.bitcast_convert_type(z, jnp.uint32) | jnp.uint32(SIGN_BIT), F32)
    return jnp.maximum(z, 0.0) + jnp.log(1.0 + jnp.exp(neg_abs))


def _hi_lo(v):
    hi = v.astype(BF16)
    return jnp.concatenate([hi, (v - hi.astype(F32)).astype(BF16)], axis=1)


def _emit_skewed(chains, lag=1):
    for t in range(max(len(ch) for ch in chains) + lag * (len(chains) - 1)):
        for c, ch in enumerate(chains):
            if 0 <= t - lag * c < len(ch):
                ch[t - lag * c]()


def _fwd_chain(blk, qs, k_ref, v_ref, c0, kb, cols, mask, ntri2, lane, tq):
    st = {}

    def scores():
        st["z"] = _dot_nt(qs, k_ref[pl.ds(c0, tq), cols])

    def soft():
        sp = _softplus(st["z"])
        if mask is not None:
            sp = jnp.where(mask, sp, 0.0)
        st["parts"] = _hi_lo(sp)
        st["cur"] = blk["cur"]
        blk["cm"] = jnp.where(lane == kb, blk["cur"], blk["cm"])
        blk["cur"] = blk["cur"] - jnp.sum(sp, axis=1, keepdims=True)

    def sums():
        st["s"] = _dot_nn(st["parts"], ntri2)

    def weights():
        w = jnp.exp(st["z"] + st["s"] + st["cur"])
        if mask is not None:
            w = jnp.where(mask, w, 0.0)
        st["w"] = w.astype(BF16)

    def out():
        p = _dot_nn(st["w"], v_ref[pl.ds(c0, tq), cols])
        blk["pv"] = p if blk["pv"] is None else blk["pv"] + p

    return [scores, soft, sums, weights, out]


def _bwd_chain(blk, qs, dos, cs, k_ref, v_ref, dk_ref, dv_ref, c0, kb, cols, mask, ntri2, tri_i, lane, tq):
    st = {}

    def scores():
        st["z"] = _dot_nt(qs, k_ref[pl.ds(c0, tq), cols])
        st["dw"] = _dot_nt(dos, v_ref[pl.ds(c0, tq), cols])

    def soft():
        sp = _softplus(st["z"])
        if mask is not None:
            sp = jnp.where(mask, sp, 0.0)
        st["sp"] = sp
        st["parts"] = _hi_lo(sp)
        st["cur"] = jnp.sum(jnp.where(lane == kb, cs, 0.0), axis=1, keepdims=True)

    def sums():
        st["s"] = _dot_nn(st["parts"], ntri2)

    def weights():
        w = jnp.exp(st["z"] + st["s"] + st["cur"])
        if mask is not None:
            w = jnp.where(mask, w, 0.0)
        ee = w * st["dw"]
        st["w"], st["ee"], st["ec"] = w.astype(BF16), ee, blk["ec"]
        blk["ec"] = blk["ec"] + jnp.sum(ee, axis=1, keepdims=True)

    def prefix():
        st["einc"] = _dot_nn(st["ee"].astype(BF16), tri_i)

    def dz():
        v = st["ee"] - jnp.exp(st["z"] - st["sp"]) * (st["einc"] + st["ec"])
        if mask is not None:
            v = jnp.where(mask, v, 0.0)
        st["dz"] = v.astype(BF16)

    def grads():
        p = _dot_nn(st["dz"], k_ref[pl.ds(c0, tq), cols])
        blk["dq"] = p if blk["dq"] is None else blk["dq"] + p
        dk_ref[pl.ds(c0, tq), :] += _dot_tn(st["dz"], qs)
        dv_ref[pl.ds(c0, tq), :] += _dot_tn(st["w"], dos)

    return [scores, soft, sums, weights, prefix, dz, grads]


def _stack_heads(v, lane, scale=None):
    if scale is not None:
        v = v * jnp.asarray(scale, v.dtype)
    zero = jnp.zeros_like(v)
    return jnp.concatenate([jnp.where(lane < HEAD_DIM, v, zero), jnp.where(lane >= HEAD_DIM, v, zero)], axis=0)


def _diag_mask(tq):
    row = lax.broadcasted_iota(jnp.int32, (2 * tq, tq), 0)
    col = lax.broadcasted_iota(jnp.int32, (2 * tq, tq), 1)
    return col < jnp.where(row >= tq, row - tq, row)


def _attn_fwd(proj, tri_after, n_seq, seq, ag_srcs, ag_out_shapes, ag_dests):
    t = proj.shape[0]
    tq = ATT_TILE
    npp = ATT_PAIRS
    n_blk = (proj.shape[1] // 4) // (npp * LANES)
    n_ag, n_ag_out = len(ag_srcs), len(ag_out_shapes)
    n_steps = n_seq * n_blk

    def body(q_ref, k_ref, v_ref, tri_ref, *rest):
        ag_src, rest = rest[:n_ag], rest[n_ag:]
        o_ref, cs_ref = rest[:2]
        ag_out, rest = rest[2:2 + n_ag_out], rest[2 + n_ag_out:]
        oacc, cmat, carry = rest[:3]
        ag_start, ag_forward, ag_finish = _ag_phases(ag_dests, ag_src, ag_out, *rest[3:])
        step = pl.program_id(0) * n_blk + pl.program_id(1)
        pl.when(step == 0)(ag_start)
        pl.when(step == (3 * n_steps) // 4)(ag_forward)
        lane = lax.broadcasted_iota(jnp.int32, (1, LANES), 1)
        ntri2 = tri_ref[...]
        diag = _diag_mask(tq)

        def q_tile(qi, _):
            r0 = pl.multiple_of(qi * tq, tq)
            qs = [_stack_heads(q_ref[pl.ds(r0, tq), pp * LANES:(pp + 1) * LANES], lane, Q_SCALE)
                  for pp in range(npp)]
            carry[...] = jnp.zeros_like(carry)
            cmat[...] = jnp.zeros_like(cmat)
            oacc[...] = jnp.zeros_like(oacc)

            def run_tiles(tiles):
                blocks = [dict(cur=carry[pp], cm=cmat[pp], pv=None) for pp in range(npp)]
                chains = []
                for kb, mask in tiles:
                    c0 = pl.multiple_of(kb * tq, tq)
                    for pp in range(npp):
                        chains.append(_fwd_chain(blocks[pp], qs[pp], k_ref, v_ref, c0, kb,
                                                 slice(pp * LANES, (pp + 1) * LANES), mask, ntri2, lane, tq))
                _emit_skewed(chains)
                for pp in range(npp):
                    oacc[pp] += blocks[pp]["pv"]
                    cmat[pp] = blocks[pp]["cm"]
                    carry[pp] = blocks[pp]["cur"]

            odd = qi % 2

            @pl.when(odd == 0)
            def _():
                run_tiles([(qi, diag)])

            @pl.when(odd == 1)
            def _():
                run_tiles([(qi, diag), (qi - 1, None)])

            def pair(j, _):
                kb = qi - 1 - odd - 2 * j
                run_tiles([(kb, None), (kb - 1, None)])
                return 0

            lax.fori_loop(0, qi // 2, pair, 0)
            for pp in range(npp):
                c_off = 2 * pp * LANES
                cs_ref[pl.ds(r0, tq), c_off:c_off + LANES] = cmat[pp, 0:tq, :]
                cs_ref[pl.ds(r0, tq), c_off + LANES:c_off + 2 * LANES] = cmat[pp, tq:2 * tq, :]
                o_ref[pl.ds(r0, tq), pp * LANES:(pp + 1) * LANES] = jnp.where(
                    lane < HEAD_DIM, oacc[pp, 0:tq, :], oacc[pp, tq:2 * tq, :]).astype(BF16)
            return 0

        lax.fori_loop(0, seq // tq, q_tile, 0)
        pl.when(step == n_steps - 1)(ag_finish)

    wid = npp * LANES
    blk = lambda off: pl.BlockSpec((seq, wid), lambda b, p: (b, off + p))
    any_spec = pl.BlockSpec(memory_space=pl.ANY)
    return pl.pallas_call(
        body, name="attn_fwd", grid=(n_seq, n_blk),
        out_shape=(jax.ShapeDtypeStruct((2, t, n_blk * wid), BF16),
                   jax.ShapeDtypeStruct((t, n_blk * 2 * wid), F32), *ag_out_shapes),
        in_specs=[blk(0), blk(n_blk), blk(2 * n_blk), pl.BlockSpec((2 * tq, tq), lambda b, p: (0, 0))]
        + [any_spec] * n_ag,
        out_specs=(pl.BlockSpec((None, seq, wid), lambda b, p: (0, b, p)),
                   pl.BlockSpec((seq, 2 * wid), lambda b, p: (b, p)), *([any_spec] * n_ag_out)),
        scratch_shapes=[pltpu.VMEM((npp, 2 * tq, LANES), F32), pltpu.VMEM((npp, 2 * tq, LANES), F32),
                        pltpu.VMEM((npp, 2 * tq, 1), F32)] + _ag_scratch(n_ag),
        compiler_params=_params(),
    )(proj, proj, proj, tri_after, *ag_srcs)


def _attn_bwd(proj, dcat, cstats, tri_after, tri_incl, n_seq, seq, rs_sends):
    t = proj.shape[0]
    tq = ATT_TILE
    npp = ATT_PAIRS
    width = proj.shape[1] // 4
    n_blk = width // (npp * LANES)
    n_rs = len(rs_sends)
    rs_shapes = [r.shape for r in rs_sends]
    n_steps = n_seq * n_blk

    def body(q_ref, k_ref, v_ref, do_ref, cs_ref, tria_ref, trii_ref, *rest):
        rs_src, rest = rest[:n_rs], rest[n_rs:]
        out_ref = rest[0]
        rs_dst, rest = rest[1:1 + n_rs], rest[1 + n_rs:]
        dq_acc, dk_acc, dv_acc, ecarry = rest[:4]
        rs_start, rs_finish = _rs_phases(rs_shapes, rs_src, rs_dst, *rest[4:])
        step = pl.program_id(0) * n_blk + pl.program_id(1)
        pl.when(step == 0)(rs_start)
        lane = lax.broadcasted_iota(jnp.int32, (1, LANES), 1)
        ntri2 = tria_ref[...]
        tri_i = trii_ref[...]
        diag = _diag_mask(tq)
        dk_acc[...] = jnp.zeros_like(dk_acc)
        dv_acc[...] = jnp.zeros_like(dv_acc)

        def q_tile(qi, _):
            r0 = pl.multiple_of(qi * tq, tq)
            qs, dos, cs = [], [], []
            for pp in range(npp):
                cols = slice(pp * LANES, (pp + 1) * LANES)
                qs.append(_stack_heads(q_ref[pl.ds(r0, tq), cols], lane, Q_SCALE))
                dos.append(_stack_heads(do_ref[pl.ds(r0, tq), cols], lane))
                c_off = 2 * pp * LANES
                cs.append(jnp.concatenate([cs_ref[pl.ds(r0, tq), c_off:c_off + LANES],
                                           cs_ref[pl.ds(r0, tq), c_off + LANES:c_off + 2 * LANES]], axis=0))
            ecarry[...] = jnp.zeros_like(ecarry)
            dq_acc[...] = jnp.zeros_like(dq_acc)

            def run_tiles(tiles):
                blocks = [dict(ec=ecarry[pp], dq=None) for pp in range(npp)]
                chains = []
                for kb, mask in tiles:
                    c0 = pl.multiple_of(kb * tq, tq)
                    for pp in range(npp):
                        chains.append(_bwd_chain(
                            blocks[pp], qs[pp], dos[pp], cs[pp], k_ref, v_ref, dk_acc.at[pp], dv_acc.at[pp],
                            c0, kb, slice(pp * LANES, (pp + 1) * LANES), mask, ntri2, tri_i, lane, tq))
                _emit_skewed(chains)
                for pp in range(npp):
                    dq_acc[pp] += blocks[pp]["dq"]
                    ecarry[pp] = blocks[pp]["ec"]

            def pair(j, _):
                run_tiles([(2 * j, None), (2 * j + 1, None)])
                return 0

            lax.fori_loop(0, qi // 2, pair, 0)
            odd = qi % 2

            @pl.when(odd == 0)
            def _():
                run_tiles([(qi, diag)])

            @pl.when(odd == 1)
            def _():
                run_tiles([(qi - 1, None), (qi, diag)])

            for pp in range(npp):
                dq = jnp.where(lane < HEAD_DIM, dq_acc[pp, 0:tq, :], dq_acc[pp, tq:2 * tq, :])
                out_ref[0, pl.ds(r0, tq), pp * LANES:(pp + 1) * LANES] = (dq * Q_SCALE).astype(BF16)
            return 0

        lax.fori_loop(0, seq // tq, q_tile, 0)
        for pp in range(npp):
            cols = slice(pp * LANES, (pp + 1) * LANES)
            out_ref[1, :, cols] = dk_acc[pp].astype(BF16)
            out_ref[2, :, cols] = dv_acc[pp].astype(BF16)
        pl.when(step == n_steps - 1)(rs_finish)

    wid = npp * LANES
    blk = lambda off: pl.BlockSpec((seq, wid), lambda b, p: (b, off + p))
    tri_spec = pl.BlockSpec((2 * tq, tq), lambda b, p: (0, 0))
    any_spec = pl.BlockSpec(memory_space=pl.ANY)
    return pl.pallas_call(
        body, name="attn_bwd", grid=(n_seq, n_blk),
        out_shape=(jax.ShapeDtypeStruct((4, t, width), BF16), *_rs_out(rs_sends)),
        in_specs=[blk(0), blk(n_blk), blk(2 * n_blk), pl.BlockSpec((seq, wid), lambda b, p: (b, p)),
                  pl.BlockSpec((seq, 2 * wid), lambda b, p: (b, p)), tri_spec,
                  pl.BlockSpec((tq, tq), lambda b, p: (0, 0))] + [any_spec] * n_rs,
        out_specs=(pl.BlockSpec((3, seq, wid), lambda b, p: (0, b, p)), *([any_spec] * n_rs)),
        scratch_shapes=[pltpu.VMEM((npp, 2 * tq, LANES), F32), pltpu.VMEM((npp, seq, LANES), F32),
                        pltpu.VMEM((npp, seq, LANES), F32), pltpu.VMEM((npp, 2 * tq, 1), F32)]
        + _rs_scratch(rs_sends),
        compiler_params=_params(),
    )(proj, proj, proj, dcat, cstats, tri_after, tri_incl, *rs_sends)


def _window_terms(g, rows):
    win = jnp.where(g == 0, POOL_WINDOWS[0], jnp.where(g == 1, POOL_WINDOWS[1],
                    jnp.where(g == 2, POOL_WINDOWS[2], POOL_WINDOWS[3])))
    cnt = jnp.minimum(rows + 1, win).astype(F32)
    return win, cnt


def _window_sum(v, g, rows, forward):
    s_len = v.shape[0]
    sums = []
    s = v
    for step in range(len(POOL_WINDOWS)):
        sh = 1 << step
        if forward:
            shifted = jnp.where(rows < s_len - sh, pltpu.roll(s, s_len - sh, axis=0), 0.0)
        else:
            shifted = jnp.where(rows >= sh, pltpu.roll(s, sh, axis=0), 0.0)
        s = s + shifted
        sums.append(s)
    return jnp.where(g == 0, sums[0], jnp.where(g == 1, sums[1], jnp.where(g == 2, sums[2], sums[3])))


def _pooled(u, g, rows):
    _, cnt = _window_terms(g, rows)
    return _window_sum(u, g, rows, forward=False) / cnt - u


def _pool_fwd(proj, w_pool, pool_scale, cat, n_seq, seq):
    n_grp = len(POOL_WINDOWS)
    u_off = 3 * (proj.shape[1] // 4) // LANES

    def body(u_ref, w_ref, s_ref, alias_ref, o_ref):
        del alias_ref
        g = pl.program_id(1)
        rows = lax.broadcasted_iota(jnp.int32, (seq, 1), 0)
        pooled = _pooled(u_ref[...].astype(F32), g, rows)
        y = _dot_nn(pooled.astype(BF16), w_ref[...].astype(BF16))
        o_ref[...] = (y * s_ref[...]).astype(BF16)

    return pl.pallas_call(
        body, name="pool_fwd", grid=(n_seq, n_grp),
        out_shape=jax.ShapeDtypeStruct(cat.shape, BF16),
        in_specs=[pl.BlockSpec((seq, LANES), lambda b, g: (b, u_off + g)),
                  pl.BlockSpec((None, POOL_GROUP_DIM, POOL_GROUP_DIM), lambda b, g: (g, 0, 0)),
                  pl.BlockSpec((1, POOL_GROUP_DIM), lambda b, g: (0, g)),
                  pl.BlockSpec(memory_space=pl.ANY)],
        out_specs=pl.BlockSpec((None, seq, LANES), lambda b, g: (1, b, g)),
        input_output_aliases={3: 0},
        compiler_params=_params(),
    )(proj, w_pool, pool_scale, cat)


def _pool_bwd(proj, dcat, w_pool, pool_scale, dqkv, n_seq, seq):
    n_grp = len(POOL_WINDOWS)
    width = proj.shape[1] // 4
    u_off = 3 * width // LANES
    dp_off = width // LANES

    def body(u_ref, dp_ref, w_ref, s_ref, alias_ref, du_ref, gw_ref, gs_ref):
        del alias_ref
        g = pl.program_id(0)
        b = pl.program_id(1)
        rows = lax.broadcasted_iota(jnp.int32, (seq, 1), 0)
        pooled = _pooled(u_ref[...].astype(F32), g, rows)
        pb = pooled.astype(BF16)
        wb = w_ref[...].astype(BF16)
        z = _dot_nn(pb, wb)
        dp = dp_ref[...].astype(F32)
        _acc(gs_ref, _colsum(dp * z), b == 0)
        dys = (dp * s_ref[...]).astype(BF16)
        _acc(gw_ref, _dot_tn(pb, dys), b == 0)
        dpooled = _dot_nt(dys, wb)
        _, cnt = _window_terms(g, rows)
        du = _window_sum(dpooled / cnt, g, rows, forward=True) - dpooled
        du_ref[...] = du.astype(BF16)

    t = proj.shape[0]
    return pl.pallas_call(
        body, name="pool_bwd", grid=(n_grp, n_seq),
        out_shape=(jax.ShapeDtypeStruct(dqkv.shape, BF16),
                   jax.ShapeDtypeStruct((n_grp, POOL_GROUP_DIM, POOL_GROUP_DIM), F32),
                   jax.ShapeDtypeStruct((1, n_grp * POOL_GROUP_DIM), F32)),
        in_specs=[pl.BlockSpec((seq, LANES), lambda g, b: (b, u_off + g)),
                  pl.BlockSpec((seq, LANES), lambda g, b: (b, dp_off + g)),
                  pl.BlockSpec((None, POOL_GROUP_DIM, POOL_GROUP_DIM), lambda g, b: (g, 0, 0)),
                  pl.BlockSpec((1, POOL_GROUP_DIM), lambda g, b: (0, g)),
                  pl.BlockSpec(memory_space=pl.ANY)],
        out_specs=(pl.BlockSpec((None, seq, LANES), lambda g, b: (3, b, g)),
                   pl.BlockSpec((None, POOL_GROUP_DIM, POOL_GROUP_DIM), lambda g, b: (g, 0, 0)),
                   pl.BlockSpec((1, POOL_GROUP_DIM), lambda g, b: (0, g))),
        input_output_aliases={4: 0},
        compiler_params=_params(),
    )(proj, dcat, w_pool, pool_scale, dqkv)


def _cond_fwd(c_all, w_cond, b_cols):
    n, _ = c_all.shape
    cols = w_cond.shape[1]

    def body(c_ref, w_ref, b_ref, o_ref):
        cv = c_ref[...]
        a = cv * jax.nn.sigmoid(cv)
        o_ref[...] = jnp.dot(a, w_ref[...], preferred_element_type=F32,
                             precision=lax.Precision.HIGHEST) + b_ref[...]

    return pl.pallas_call(
        body, name="cond_fwd", out_shape=jax.ShapeDtypeStruct((n, cols), F32),
        compiler_params=_params(),
    )(c_all, w_cond, b_cols)


def _cond_bwd(c_all, dmod_all, dmod_cols):
    n, d = c_all.shape
    cols = dmod_cols.shape[1]

    def body(c_ref, dm_ref, dmc_ref, gw_ref, gb_ref):
        cv = c_ref[...]
        a = cv * jax.nn.sigmoid(cv)
        gw_ref[...] = lax.dot_general(a, dmc_ref[...], (((0,), (0,)), ((), ())),
                                      preferred_element_type=F32, precision=lax.Precision.HIGHEST)
        gb_ref[...] = _colsum(dm_ref[...])

    return pl.pallas_call(
        body, name="cond_bwd",
        out_shape=(jax.ShapeDtypeStruct((d, cols), F32), jax.ShapeDtypeStruct((1, dmod_all.shape[1]), F32)),
        compiler_params=_params(),
    )(c_all, dmod_all, dmod_cols)


def _adamw_math(w, g, m, v):
    m = ADAM_B1 * m + (1.0 - ADAM_B1) * g
    v = ADAM_B2 * v + (1.0 - ADAM_B2) * (g * g)
    m_hat = m / (1.0 - ADAM_B1 ** ADAM_STEP)
    v_hat = v / (1.0 - ADAM_B2 ** ADAM_STEP)
    delta = -ADAM_LR * (m_hat / (jnp.sqrt(v_hat) + ADAM_EPS) + ADAM_WD * w)
    return delta, m, v


def _adamw(w, g, m, v, rows, name):
    r, cdim = w.shape

    def body(w_ref, g_ref, m_ref, v_ref, d_ref, nm_ref, nv_ref):
        d_ref[...], nm_ref[...], nv_ref[...] = _adamw_math(w_ref[...], g_ref[...], m_ref[...], v_ref[...])

    spec = pl.BlockSpec((rows, cdim), lambda i: (i, 0))
    sds = jax.ShapeDtypeStruct((r, cdim), F32)
    return pl.pallas_call(
        body, name=name, grid=(r // rows,), out_shape=(sds, sds, sds),
        in_specs=[spec] * 4, out_specs=(spec, spec, spec), compiler_params=_params(),
    )(w, g, m, v)


def _adamw_small(ws, gparts, ms, vs, name):
    n = len(ws)

    def body(*refs):
        w_r, g_r, m_r, v_r = refs[:n], refs[n:2 * n], refs[2 * n:3 * n], refs[3 * n:4 * n]
        outs = refs[4 * n:]
        for i in range(n):
            g = g_r[i][0]
            for dev in range(1, g_r[i].shape[0]):
                g = g + g_r[i][dev]
            delta, m, v = _adamw_math(w_r[i][...], g, m_r[i][...], v_r[i][...])
            outs[i][...] = g
            outs[n + i][...] = delta
            outs[2 * n + i][...] = m
            outs[3 * n + i][...] = v

    sds = [jax.ShapeDtypeStruct(w.shape, F32) for w in ws]
    return pl.pallas_call(
        body, name=name, out_shape=tuple(sds * 4), compiler_params=_params(),
    )(*ws, *gparts, *ms, *vs)


def kernel(x, c, w_cond, b_cond, g_mix_pre, g_mix_post, w_in, w_pool, pool_scale, w_out, g_ffn_pre, g_ffn_post, w_gate, w_up, w_down, loss_target, m_w_cond, m_b_cond, m_g_mix_pre, m_g_mix_post, m_w_in, m_w_pool, m_pool_scale, m_w_out, m_g_ffn_pre, m_g_ffn_post, m_w_gate, m_w_up, m_w_down, v_w_cond, v_b_cond, v_g_mix_pre, v_g_mix_post, v_w_in, v_w_pool, v_pool_scale, v_w_out, v_g_ffn_pre, v_g_ffn_post, v_w_gate, v_w_up, v_w_down):
    n_seq, seq, d = x.shape
    t = n_seq * seq
    xi, yi, ci = _mesh_pos()
    me = 4 * xi + 2 * yi + ci
    x2 = x.reshape(t, d)
    tgt2 = loss_target.reshape(t, d)
    in_rows = w_in.shape[2]
    out_rows = w_out.shape[1]
    ff_rows = w_gate.shape[2]
    ff = N_DEV * ff_rows
    cond_cols = w_cond.shape[2]

    win_t = w_in[0].T.astype(BF16)
    wout_s = w_out[0].astype(BF16)
    wg_t = w_gate[0].T.astype(BF16)
    wu_t = w_up[0].T.astype(BF16)
    wd_s = w_down[0].astype(BF16)
    c_all, win_g = _all_gather(
        [c, win_t],
        [jax.ShapeDtypeStruct((N_DEV, n_seq, d), F32), jax.ShapeDtypeStruct((N_DEV, in_rows, d), BF16)],
        [(0, ()), (1, ())], "ag_c_win")
    c_all = c_all.reshape(N_DEV * n_seq, d)
    win_full = win_g.reshape(N_DEV * in_rows, d)

    b_cols = lax.dynamic_slice_in_dim(b_cond, me * cond_cols, cond_cols, axis=1)
    mod_cols = _cond_fwd(c_all, w_cond[0], b_cols)
    (mod_g,) = _all_gather([mod_cols], [jax.ShapeDtypeStruct((N_DEV,) + mod_cols.shape, F32)], [(0, ())], "ag_mod")
    mod_mine = lax.dynamic_slice_in_dim(mod_g, me * n_seq, n_seq, axis=1)
    mod = jnp.transpose(mod_mine, (1, 0, 2)).reshape(n_seq, N_MOD, d)

    h1 = _pre_mix(x2, g_mix_pre, mod, seq)
    proj = _matmul(h1, win_full, "nt", BF16, 1024, 512, d, "proj")
    tq = ATT_TILE
    ids = jnp.arange(tq)
    tri_after = jnp.tile(-(ids[:, None] >= ids[None, :]).astype(BF16), (2, 1))
    tri_incl = (ids[:, None] <= ids[None, :]).astype(BF16)
    attn, cstats, wout_g, wgu_g, wd_g = _attn_fwd(
        proj, tri_after, n_seq, seq, [wout_s, wg_t, wu_t, wd_s],
        [jax.ShapeDtypeStruct((N_DEV, out_rows, d), BF16), jax.ShapeDtypeStruct((2, N_DEV, ff_rows, d), BF16),
         jax.ShapeDtypeStruct((N_DEV, ff_rows, d), BF16)],
        [(0, ()), (1, (0,)), (1, (1,)), (2, ())])
    wout_full = wout_g.reshape(N_DEV * out_rows, d)
    wgu_full = wgu_g.reshape(2, ff, d)
    wd_full = wd_g.reshape(ff, d)
    cat = _pool_fwd(proj, w_pool[0], pool_scale, attn, n_seq, seq)
    tok_f32, tok_bf16 = jax.ShapeDtypeStruct((t, d), F32), jax.ShapeDtypeStruct((t, d), BF16)
    seq_sds, vec_sds = jax.ShapeDtypeStruct((n_seq, 1, d), F32), jax.ShapeDtypeStruct((1, d), F32)
    mix, x1, h2 = _matmul_rows(
        cat, wout_full.reshape(2, d // 2, d), ROW_TILE, d // 2, seq, "mix_mid", _mid_epilogue,
        [x2, g_mix_post, g_ffn_pre, mod], ["tok", "vec", "vec", "mod"],
        [tok_f32, tok_f32, tok_bf16], ["tok", "tok", "tok"])
    gu, act = _ffn_up(h2, wgu_full, 512, ff // 2)
    loss_sum, dy, df, dgate_f, gg_ffn_post = _matmul_rows(
        act, wd_full, ROW_TILE, ff, seq, "ffn_down_post", _post_epilogue,
        [x1, tgt2, g_ffn_post, mod], ["tok", "tok", "vec", "mod"],
        [jax.ShapeDtypeStruct((1, LANES), F32), tok_f32, tok_bf16, seq_sds, vec_sds],
        ["loss", "tok", "tok", "seq", "vec"])

    dgu = _ffn_act_bwd(df, wd_full, gu, 512, ff // 2)
    gwd, gwd_b = _matmul(act, df, "tn", F32, ff // 2, d // 2, t, "grad_w_down", bf16_copy=True)
    gwgu, gwgu_b = _matmul(dgu, h2, "tn", F32, ff // 2, d // 2, t, "grad_w_gate_up", bf16_copy=True)
    dx1, dmix, dshift_f, dscale_f, dgate_m, gg_ffn_pre, gg_mix_post = _matmul_rows(
        dgu, wgu_full, ROW_TILE, ff, seq, "dh2_bwd_mid", _bwd_mid_epilogue,
        [dy, x1, mix, g_ffn_pre, g_mix_post, mod], ["tok", "tok", "tok", "vec", "vec", "mod"],
        [tok_f32, tok_bf16, seq_sds, seq_sds, seq_sds, vec_sds, vec_sds],
        ["tok", "tok", "seq", "seq", "seq", "vec", "vec"])
    dcat = _matmul(dmix, wout_full, "nt", BF16, 1024, 512, d, "dcat")
    gwout, gwout_b = _matmul(cat, dmix, "tn", F32, d // 2, d // 2, t, "grad_w_out", bf16_copy=True)
    dqkv, rv_wgu, rv_wd, rv_wout = _attn_bwd(
        proj, dcat, cstats, tri_after, tri_incl, n_seq, seq,
        [gwgu_b.reshape(2, N_DEV, ff_rows, d), gwd_b.reshape(1, N_DEV, ff_rows, d),
         gwout_b.reshape(1, N_DEV, out_rows, d)])
    dproj, gw_pool, gs_pool = _pool_bwd(proj, dcat, w_pool[0], pool_scale, dqkv, n_seq, seq)
    gwin, gwin_b = _matmul(dproj, h1, "tn", F32, d // 2, d // 2, t, "grad_w_in", bf16_copy=True)
    grad_x, dshift_m, dscale_m, gg_mix_pre, rv_win = _matmul_rows(
        dproj, win_full.reshape(4, d // 2, d), ROW_TILE, d // 2, seq, "dh1_bwd_pre", _bwd_pre_epilogue,
        [dx1, x2, g_mix_pre, mod], ["tok", "tok", "vec", "mod"],
        [tok_f32, seq_sds, seq_sds, vec_sds], ["tok", "seq", "seq", "vec"],
        rs_sends=[gwin_b.reshape(1, N_DEV, in_rows, d)])

    r_wgu = _rs_final(gwgu.reshape(2, N_DEV, ff_rows, d), rv_wgu, "rs_final_gate_up")
    r_wd = _rs_final(gwd.reshape(1, N_DEV, ff_rows, d), rv_wd, "rs_final_down")
    r_wout = _rs_final(gwout.reshape(1, N_DEV, out_rows, d), rv_wout, "rs_final_out")
    r_win = _rs_final(gwin.reshape(1, N_DEV, in_rows, d), rv_win, "rs_final_in")
    grad_w_in = r_win[0].T
    grad_w_out = r_wout[0]
    grad_w_down = r_wd[0]

    dmod = jnp.concatenate([dshift_m, dscale_m, dgate_m, dshift_f, dscale_f, dgate_f], axis=1)
    small = jnp.concatenate([gg_mix_pre, gg_mix_post, gg_ffn_pre, gg_ffn_post,
                             jnp.pad(gs_pool, ((0, 0), (0, d - gs_pool.shape[1]))),
                             jnp.pad(loss_sum, ((0, 0), (0, d - loss_sum.shape[1]))), jnp.zeros((2, d), F32),
                             gw_pool.reshape(-1, d), dmod.reshape(n_seq * N_MOD, d)], axis=0)
    n_gw = gw_pool.size // d
    (small_g,) = _all_gather([small], [jax.ShapeDtypeStruct((N_DEV,) + small.shape, F32)], [(0, ())], "ag_small")
    loss = jnp.sum(small_g[:, 5, 0]) * (0.5 / d)
    dmod_all = small_g[:, 8 + n_gw:, :].reshape(N_DEV * n_seq, N_MOD * d)
    dmod_cols = lax.dynamic_slice_in_dim(dmod_all, me * cond_cols, cond_cols, axis=1)
    grad_w_cond, grad_b_cond = _cond_bwd(c_all, dmod_all, dmod_cols)

    small_ws = [g_mix_pre, g_mix_post, g_ffn_pre, g_ffn_post, pool_scale, w_pool.reshape(-1, POOL_GROUP_DIM)]
    small_ms = [m_g_mix_pre, m_g_mix_post, m_g_ffn_pre, m_g_ffn_post, m_pool_scale, m_w_pool.reshape(-1, POOL_GROUP_DIM)]
    small_vs = [v_g_mix_pre, v_g_mix_post, v_g_ffn_pre, v_g_ffn_post, v_pool_scale, v_w_pool.reshape(-1, POOL_GROUP_DIM)]
    small_gparts = [small_g[:, 0:1, :], small_g[:, 1:2, :], small_g[:, 2:3, :], small_g[:, 3:4, :],
                    small_g[:, 4:5, :pool_scale.shape[1]],
                    small_g[:, 8:8 + n_gw, :].reshape(N_DEV, -1, POOL_GROUP_DIM)]
    so = _adamw_small(small_ws, small_gparts, small_ms, small_vs, "adamw_small")
    ns = len(small_ws)
    sg, sdl, sm, sv = so[:ns], so[ns:2 * ns], so[2 * ns:3 * ns], so[3 * ns:]
    pool_shape = w_pool.shape
    fix = lambda lst: [lst[0], lst[1], lst[2], lst[3], lst[4], lst[5].reshape(pool_shape)]
    sg, sdl, sm, sv = fix(sg), fix(sdl), fix(sm), fix(sv)

    def big(w, g, m, v, rows, name):
        dl, nm, nv = _adamw(w[0], g, m[0], v[0], rows, name)
        return g[None], dl[None], nm[None], nv[None]

    o_cond = big(w_cond, grad_w_cond, m_w_cond, v_w_cond, 256, "adamw_w_cond")
    o_bcond = _adamw(b_cond, grad_b_cond, m_b_cond, v_b_cond, 1, "adamw_b_cond")
    o_bcond = (grad_b_cond,) + tuple(o_bcond)
    o_in = big(w_in, grad_w_in, m_w_in, v_w_in, 256, "adamw_w_in")
    o_out = big(w_out, grad_w_out, m_w_out, v_w_out, out_rows, "adamw_w_out")
    def big_t(w, g_t, m, v, name):
        outs = _adamw(w[0].T, g_t, m[0].T, v[0].T, g_t.shape[0], name)
        return tuple(o.T[None] for o in (g_t,) + tuple(outs))

    o_gate = big_t(w_gate, r_wgu[0], m_w_gate, v_w_gate, "adamw_w_gate")
    o_up = big_t(w_up, r_wgu[1], m_w_up, v_w_up, "adamw_w_up")
    o_down = big(w_down, grad_w_down, m_w_down, v_w_down, ff_rows, "adamw_w_down")

    def pick(k):
        small_k = [sg, sdl, sm, sv][k]
        return [o_cond[k], o_bcond[k], small_k[0], small_k[1], o_in[k], small_k[5], small_k[4], o_out[k],
                small_k[2], small_k[3], o_gate[k], o_up[k], o_down[k]]

    return (loss, grad_x.reshape(n_seq, seq, d), *pick(0), *pick(1), *pick(2), *pick(3))
```

```python
import functools
import math

import jax
import jax.numpy as jnp
from jax import lax
from jax.experimental import pallas as pl
from jax.experimental.pallas import tpu as pltpu

F32 = jnp.float32
BF16 = jnp.bfloat16
MESH = pl.DeviceIdType.MESH

N_DEV = 8
HEAD_DIM = 64
LANES = 128
POOL_WINDOWS = (2, 4, 8, 16)
POOL_GROUP_DIM = 128
N_MOD = 6
EPS = 1e-6
ATT_TILE = 256
ATT_PAIRS = 2
VMEM_LIMIT = 56 * 1024 * 1024

ADAM_LR = 0.001
ADAM_B1 = 0.9
ADAM_B2 = 0.999
ADAM_EPS = 1e-08
ADAM_WD = 0.01
ADAM_STEP = 10


def _params(**kw):
    return pltpu.CompilerParams(vmem_limit_bytes=VMEM_LIMIT, **kw)


def _dot_nn(a, b):
    return jnp.dot(a, b, preferred_element_type=F32)


def _dot_nt(a, b):
    return lax.dot_general(a, b, (((1,), (1,)), ((), ())), preferred_element_type=F32)


def _dot_tn(a, b):
    return lax.dot_general(a, b, (((0,), (0,)), ((), ())), preferred_element_type=F32)


def _mesh_pos():
    return lax.axis_index("x"), lax.axis_index("y"), lax.axis_index("c")


def _ag_phases(dests, src, outs, send_sems, recv_sems, local_sems):
    n = len(src)
    x, y, c = _mesh_pos()
    me, sibling = (x, y, c), (x, y, 1 - c)
    chips = [(1 - x, y), (x, 1 - y), (1 - x, 1 - y)]

    def slot(i, dev):
        oi, prefix = dests[i]
        px, py, pc = dev
        return outs[oi].at[prefix + (4 * px + 2 * py + pc,)]

    def copy(i, k, block, to, from_src=False):
        return pltpu.make_async_remote_copy(
            src_ref=src[i] if from_src else slot(i, block), dst_ref=slot(i, block),
            send_sem=send_sems.at[i, k], recv_sem=recv_sems.at[i, k],
            device_id=to, device_id_type=MESH)

    def mine(i):
        return pltpu.make_async_copy(src[i], slot(i, me), local_sems.at[i])

    def first(i):
        return [copy(i, 0, me, sibling, from_src=True)] + [
            copy(i, 1 + j, me, (*chip, c), from_src=True) for j, chip in enumerate(chips)]

    def passed(i, j):
        return copy(i, 4 + j, (*chips[j], c), sibling)

    def start():
        for i in range(n):
            mine(i).start()
        for i in range(n):
            for cp in first(i):
                cp.start()

    def forward():
        for j, chip in enumerate(chips):
            for i in range(n):
                copy(i, 1 + j, (*chip, c), me).wait_recv()
                passed(i, j).start()

    def finish():
        for i in range(n):
            copy(i, 0, sibling, me).wait_recv()
            for j, chip in enumerate(chips):
                copy(i, 4 + j, (*chip, 1 - c), me).wait_recv()
        for i in range(n):
            for cp in first(i) + [passed(i, j) for j in range(3)]:
                cp.wait_send()
            mine(i).wait()

    return start, forward, finish


def _ag_scratch(n):
    return [pltpu.SemaphoreType.DMA((n, 7)), pltpu.SemaphoreType.DMA((n, 7)), pltpu.SemaphoreType.DMA((n,))]


def _all_gather(srcs, out_shapes, dests, name):
    n = len(srcs)

    def body(*refs):
        src = refs[:n]
        outs = refs[n:n + len(out_shapes)]
        start, forward, finish = _ag_phases(dests, src, outs, *refs[n + len(out_shapes):])
        start()
        forward()
        finish()

    any_spec = pl.BlockSpec(memory_space=pl.ANY)
    return pl.pallas_call(
        body, name=name,
        out_shape=tuple(out_shapes),
        in_specs=[any_spec] * n,
        out_specs=tuple([any_spec] * len(out_shapes)),
        scratch_shapes=_ag_scratch(n),
    )(*srcs)


def _rs_phases(shapes, src, dst, send_sems, recv_sems):
    x, y, c = _mesh_pos()

    def copies():
        out = []
        n = 0
        for i, shp in enumerate(shapes):
            for m in range(shp[0]):
                for k in range(1, N_DEV):
                    px, py, pc = x ^ (k >> 2), y ^ ((k >> 1) & 1), c ^ (k & 1)
                    out.append(pltpu.make_async_remote_copy(
                        src_ref=src[i].at[m, 4 * px + 2 * py + pc], dst_ref=dst[i].at[m, k - 1],
                        send_sem=send_sems.at[n], recv_sem=recv_sems.at[n],
                        device_id=(px, py, pc), device_id_type=MESH))
                    n += 1
        return out

    def start():
        for cp in copies():
            cp.start()

    def finish():
        for cp in copies():
            cp.wait_send()
        for cp in copies():
            cp.wait_recv()

    return start, finish


def _rs_out(sends):
    return [jax.ShapeDtypeStruct((s.shape[0], N_DEV - 1) + s.shape[2:], s.dtype) for s in sends]


def _rs_scratch(sends):
    total = sum((N_DEV - 1) * s.shape[0] for s in sends)
    return [pltpu.SemaphoreType.DMA((total,)), pltpu.SemaphoreType.DMA((total,))]


def _rs_final(mine, recv, name):
    m_n, _, r, cdim = mine.shape
    x, y, c = _mesh_pos()
    me = jnp.reshape(4 * x + 2 * y + c, (1,)).astype(jnp.int32)

    def body(me_ref, p_ref, r_ref, o_ref):
        del me_ref
        s = p_ref[...]
        for k in range(N_DEV - 1):
            s = s + r_ref[k].astype(F32)
        o_ref[...] = s

    return pl.pallas_call(
        body, name=name, out_shape=jax.ShapeDtypeStruct((m_n, r, cdim), F32),
        grid_spec=pltpu.PrefetchScalarGridSpec(
            num_scalar_prefetch=1, grid=(m_n,),
            in_specs=[pl.BlockSpec((None, None, r, cdim), lambda m, s: (m, s[0], 0, 0)),
                      pl.BlockSpec((None, N_DEV - 1, r, cdim), lambda m, s: (m, 0, 0, 0))],
            out_specs=pl.BlockSpec((None, r, cdim), lambda m, s: (m, 0, 0))),
        compiler_params=_params(),
    )(me, mine, recv)


def _matmul(a, b, mode, out_dtype, tm, tn, tk, name, bf16_copy=False, rs_sends=(), ag=None):
    ga = a.shape[0] if a.ndim == 3 else None
    gb = b.shape[0] if b.ndim == 3 else None
    a2, b2 = a.shape[-2:], b.shape[-2:]
    if mode == "nn":
        (m, k), n = a2, b2[1]
    elif mode == "nt":
        (m, k), n = a2, b2[0]
    else:
        (k, m), n = a2, b2[1]
    assert m % tm == 0 and n % tn == 0 and k % tk == 0, (name, m, n, k)
    nk = k // tk
    g_n = ga or 1
    batch_out = mode == "tn" and ga is not None
    n_red = nk if batch_out else nk * g_n
    dot = {"nn": _dot_nn, "nt": _dot_nt, "tn": _dot_tn}[mode]
    acc_in_out = out_dtype == F32

    n_rs = len(rs_sends)
    rs_shapes = [r.shape for r in rs_sends]
    ag_srcs, ag_out_shapes, ag_dests = ag if ag is not None else ((), (), ())
    n_ag, n_ag_out = len(ag_srcs), len(ag_out_shapes)
    n_out = 2 if bf16_copy else 1
    assert not bf16_copy or acc_in_out
    assert not (n_rs and n_ag)

    def body(a_ref, b_ref, *rest):
        rs_src, rest = rest[:n_rs], rest[n_rs:]
        ag_src, rest = rest[:n_ag], rest[n_ag:]
        o_ref = rest[0]
        copy_ref = rest[1] if bf16_copy else None
        rs_dst, rest = rest[n_out:n_out + n_rs], rest[n_out + n_rs:]
        ag_out, scratch = rest[:n_ag_out], rest[n_ag_out:]
        first = functools.reduce(jnp.logical_and, [pl.program_id(ax) == 0 for ax in range(4)])
        last = functools.reduce(jnp.logical_and, [pl.program_id(ax) == grid[ax] - 1 for ax in range(4)])
        if n_rs:
            rs_start, rs_finish = _rs_phases(rs_shapes, rs_src, rs_dst, *scratch[-2:])
            pl.when(first)(rs_start)
        if n_ag:
            ag_start, ag_forward, ag_finish = _ag_phases(ag_dests, ag_src, ag_out, *scratch[-3:])
            pl.when(first)(ag_start)
        p = dot(a_ref[...], b_ref[...])
        kk = pl.program_id(3) if batch_out else pl.program_id(2) * nk + pl.program_id(3)
        if n_red == 1:
            o_ref[...] = p.astype(out_dtype)
            if bf16_copy:
                copy_ref[...] = p.astype(BF16)
        else:
            acc = o_ref if acc_in_out else scratch[0]

            @pl.when(kk == 0)
            def _():
                acc[...] = p

            @pl.when(kk > 0)
            def _():
                acc[...] += p

            @pl.when(kk == n_red - 1)
            def _():
                if not acc_in_out:
                    o_ref[...] = acc[...].astype(out_dtype)
                if bf16_copy:
                    copy_ref[...] = acc[...].astype(BF16)

        if n_rs:
            pl.when(last)(rs_finish)
        if n_ag:
            @pl.when(last)
            def _():
                ag_forward()
                ag_finish()

    def order(ids):
        return ids if batch_out else (ids[2], ids[0], ids[1], ids[3])

    def a_idx(*ids):
        g, i, j, kq = order(ids)
        blk = {"nn": (i, kq), "nt": (i, kq), "tn": (kq, i)}[mode]
        return (g,) + blk if ga is not None else blk

    def b_idx(*ids):
        g, i, j, kq = order(ids)
        blk = {"nn": (kq, j), "nt": (j, kq), "tn": (kq, j)}[mode]
        return (g,) + blk if gb is not None else blk

    def o_idx(*ids):
        g, i, j, kq = order(ids)
        return (g, i, j) if batch_out else (i, j)

    a_blk = {"nn": (tm, tk), "nt": (tm, tk), "tn": (tk, tm)}[mode]
    b_blk = {"nn": (tk, tn), "nt": (tn, tk), "tn": (tk, tn)}[mode]
    if ga is not None:
        a_blk = (None,) + a_blk
    if gb is not None:
        b_blk = (None,) + b_blk
    if batch_out:
        out_shape = jax.ShapeDtypeStruct((g_n, m, n), out_dtype)
        o_blk = (None, tm, tn)
        grid = (g_n, m // tm, n // tn, nk)
    else:
        out_shape = jax.ShapeDtypeStruct((m, n), out_dtype)
        o_blk = (tm, tn)
        grid = (m // tm, n // tn, g_n, nk)
    scratch = [] if (acc_in_out or n_red == 1) else [pltpu.VMEM((tm, tn), F32)]
    any_spec = pl.BlockSpec(memory_space=pl.ANY)
    out_shapes = [out_shape] + ([jax.ShapeDtypeStruct(out_shape.shape, BF16)] if bf16_copy else [])
    res = pl.pallas_call(
        body, name=name, out_shape=tuple(out_shapes + _rs_out(rs_sends) + list(ag_out_shapes)), grid=grid,
        in_specs=[pl.BlockSpec(a_blk, a_idx), pl.BlockSpec(b_blk, b_idx)] + [any_spec] * (n_rs + n_ag),
        out_specs=tuple([pl.BlockSpec(o_blk, o_idx)] * n_out + [any_spec] * (n_rs + n_ag_out)),
        scratch_shapes=scratch + (_rs_scratch(rs_sends) if n_rs else []) + (_ag_scratch(n_ag) if n_ag else []),
        compiler_params=_params(),
    )(a, b, *rs_sends, *ag_srcs)
    return res if len(res) > 1 else res[0]


EW_TILE = 256
ROW_TILE = 512
EPILOGUE_CHUNKS = 8
MXU_WIDTH = 256


def _rms(v):
    return lax.rsqrt(jnp.mean(v * v, axis=-1, keepdims=True) + EPS)


def _rms_bwd(dhat, vh, r):
    return r * (dhat - vh * jnp.mean(dhat * vh, axis=-1, keepdims=True))


def _tok_spec(tm, d):
    return pl.BlockSpec((tm, d), lambda i: (i, 0))


def _vec_spec(d):
    return pl.BlockSpec((1, d), lambda i: (0, 0))


def _mod_spec(tiles_per_seq, d):
    return pl.BlockSpec((None, N_MOD, d), lambda i: (i // tiles_per_seq, 0, 0))


def _seq_acc_spec(tiles_per_seq, d):
    return pl.BlockSpec((None, 1, d), lambda i: (i // tiles_per_seq, 0, 0))


def _acc(ref, val, first):
    if first is False:
        ref[...] += val
        return

    @pl.when(first)
    def _():
        ref[...] = val

    @pl.when(jnp.logical_not(first))
    def _():
        ref[...] += val


def _colsum(v):
    return jnp.sum(v, axis=0, keepdims=True)


def _pre_mix(x2, g_pre, mod, seq, ag_srcs, ag_out_shapes, ag_dests):
    t, d = x2.shape
    tm = EW_TILE
    n_steps = t // tm
    n_ag, n_ag_out = len(ag_srcs), len(ag_out_shapes)

    def body(x_ref, g_ref, mod_ref, *rest):
        ag_src, h_ref = rest[:n_ag], rest[n_ag]
        ag_out, sems = rest[n_ag + 1:n_ag + 1 + n_ag_out], rest[n_ag + 1 + n_ag_out:]
        ag_start, ag_forward, ag_finish = _ag_phases(ag_dests, ag_src, ag_out, *sems)
        step = pl.program_id(0)
        pl.when(step == 0)(ag_start)
        xv = x_ref[...]
        n = xv * _rms(xv) * g_ref[...]
        h_ref[...] = (n * (1.0 + mod_ref[1:2, :]) + mod_ref[0:1, :]).astype(BF16)

        @pl.when(step == n_steps - 1)
        def _():
            ag_forward()
            ag_finish()

    any_spec = pl.BlockSpec(memory_space=pl.ANY)
    return pl.pallas_call(
        body, name="pre_mix", out_shape=(jax.ShapeDtypeStruct((t, d), BF16), *ag_out_shapes), grid=(n_steps,),
        in_specs=[_tok_spec(tm, d), _vec_spec(d), _mod_spec(seq // tm, d)] + [any_spec] * n_ag,
        out_specs=(_tok_spec(tm, d), *([any_spec] * n_ag_out)),
        scratch_shapes=_ag_scratch(n_ag), compiler_params=_params(),
    )(x2, g_pre, mod, *ag_srcs)


def _matmul_rows(a, b, tm, tk, seq, name, epilogue, ep_in, ep_in_kinds, ep_out, ep_out_kinds, rs_sends=()):
    ga = a.shape[0] if a.ndim == 3 else None
    (m, k), n = a.shape[-2:], b.shape[-1]
    g_n = ga or 1
    nk = k // tk
    n_red = g_n * nk
    tps = seq // tm
    n_i = m // tm
    grid = (n_i + 1, g_n, nk)
    n_rs = len(rs_sends)
    rs_shapes = [r.shape for r in rs_sends]
    n_in, n_out = len(ep_in), len(ep_out)
    per_step = -(-EPILOGUE_CHUNKS // n_red)
    n_chunks = per_step * n_red
    n_cols = min(per_step, n // MXU_WIDTH)
    rc, cw = tm // n_chunks, n // n_cols

    def prev(i):
        return jnp.maximum(i - 1, 0)

    def spec(kind):
        return {"tok": pl.BlockSpec((tm, n), lambda i, g, kq: (prev(i), 0)),
                "vec": pl.BlockSpec((1, n), lambda i, g, kq: (0, 0)),
                "mod": pl.BlockSpec((None, N_MOD, n), lambda i, g, kq: (prev(i) // tps, 0, 0)),
                "seq": pl.BlockSpec((None, 1, n), lambda i, g, kq: (prev(i) // tps, 0, 0)),
                "loss": pl.BlockSpec((1, LANES), lambda i, g, kq: (0, 0))}[kind]

    def body(a_ref, b_ref, *rest):
        in_refs, rest = rest[:n_in], rest[n_in:]
        rs_src, rest = rest[:n_rs], rest[n_rs:]
        out_refs, rest = rest[:n_out], rest[n_out:]
        rs_dst, rest = rest[:n_rs], rest[n_rs:]
        acc, fin = rest[:2]
        i, kk = pl.program_id(0), pl.program_id(1) * nk + pl.program_id(2)
        if n_rs:
            rs_start, rs_finish = _rs_phases(rs_shapes, rs_src, rs_dst, *rest[2:])
            pl.when(jnp.logical_and(i == 0, kk == 0))(rs_start)

        def step(s, with_epilogue, with_matmul):
            last = s == n_red - 1
            parts = []
            cols_done = 0
            for c in range(per_step):
                if with_epilogue:
                    chunk = s * per_step + c
                    rows = pl.ds(chunk * rc, rc)
                    epilogue(fin[rows, :], i - 1, tps, in_refs, out_refs, rows, chunk)
                while with_matmul and cols_done < (c + 1) * n_cols // per_step:
                    cols = slice(cols_done * cw, (cols_done + 1) * cw)
                    cols_done += 1
                    p = _dot_nn(a_ref[...], b_ref[:, cols])
                    if s == 0 and not last:
                        acc[:, cols] = p
                    elif not last:
                        acc[:, cols] += p
                    else:
                        parts.append((cols, p if s == 0 else acc[:, cols] + p))
            for cols, v in parts:
                fin[:, cols] = v

        for s in range(n_red):
            at = kk == s
            pl.when(jnp.logical_and(at, i == 0))(functools.partial(step, s, False, True))
            pl.when(jnp.logical_and(at, jnp.logical_and(i > 0, i < n_i)))(functools.partial(step, s, True, True))
            pl.when(jnp.logical_and(at, i == n_i))(functools.partial(step, s, True, False))

        if n_rs:
            pl.when(jnp.logical_and(i == n_i, kk == n_red - 1))(rs_finish)

    def row(i):
        return jnp.minimum(i, n_i - 1)

    a_blk = (tm, tk) if ga is None else (None, tm, tk)
    b_blk = (tk, n) if ga is None else (None, tk, n)
    a_idx = (lambda i, g, kq: (row(i), kq)) if ga is None else (lambda i, g, kq: (g, row(i), kq))
    b_idx = (lambda i, g, kq: (kq, 0)) if ga is None else (lambda i, g, kq: (g, kq, 0))
    any_spec = pl.BlockSpec(memory_space=pl.ANY)
    res = pl.pallas_call(
        body, name=name, grid=grid, out_shape=tuple(list(ep_out) + _rs_out(rs_sends)),
        in_specs=[pl.BlockSpec(a_blk, a_idx), pl.BlockSpec(b_blk, b_idx)] + [spec(kd) for kd in ep_in_kinds]
        + [any_spec] * n_rs,
        out_specs=tuple([spec(kd) for kd in ep_out_kinds] + [any_spec] * n_rs),
        scratch_shapes=[pltpu.VMEM((tm, n), F32)] * 2 + (_rs_scratch(rs_sends) if n_rs else []),
        compiler_params=_params(),
    )(a, b, *ep_in, *rs_sends)
    return res


def _first(cond, chunk):
    return cond if chunk == 0 else False


def _mid_epilogue(mv, i, tps, in_refs, out_refs, rows, chunk):
    x_ref, gpost_ref, gpre_ref, mod_ref = in_refs
    mix_ref, x1_ref, h2_ref = out_refs
    mix_ref[rows, :] = mv
    x1 = x_ref[rows, :] + mod_ref[2:3, :] * (mv * _rms(mv) * gpost_ref[...])
    x1_ref[rows, :] = x1
    n = x1 * _rms(x1) * gpre_ref[...]
    h2_ref[rows, :] = (n * (1.0 + mod_ref[4:5, :]) + mod_ref[3:4, :]).astype(BF16)


def _post_epilogue(fv, i, tps, in_refs, out_refs, rows, chunk):
    x1_ref, tgt_ref, g_ref, mod_ref = in_refs
    loss_ref, dy_ref, df_ref, dgate_ref, gg_ref = out_refs
    d = fv.shape[1]
    r = _rms(fv)
    fh = fv * r
    nf = fh * g_ref[...]
    gate = mod_ref[5:6, :]
    err = x1_ref[rows, :] + gate * nf - tgt_ref[rows, :]
    _acc(loss_ref, jnp.sum(_colsum(err * err), axis=1, keepdims=True) * jnp.ones((1, LANES), F32),
         _first(i == 0, chunk))
    dy = err * (1.0 / d)
    dy_ref[rows, :] = dy
    _acc(dgate_ref, _colsum(dy * nf), _first(i % tps == 0, chunk))
    dn = dy * gate
    _acc(gg_ref, _colsum(dn * fh), _first(i == 0, chunk))
    df_ref[rows, :] = _rms_bwd(dn * g_ref[...], fh, r).astype(BF16)


def _bwd_mid_epilogue(dh, i, tps, in_refs, out_refs, rows, chunk):
    dy_ref, x1_ref, mix_ref, gpre_ref, gpost_ref, mod_ref = in_refs
    dx1_ref, dmix_ref, dshift_ref, dscale_ref, dgate_ref, ggpre_ref, ggpost_ref = out_refs
    seq_first, first = _first(i % tps == 0, chunk), _first(i == 0, chunk)
    x1 = x1_ref[rows, :]
    r = _rms(x1)
    xh = x1 * r
    gpre = gpre_ref[...]
    _acc(dshift_ref, _colsum(dh), seq_first)
    _acc(dscale_ref, _colsum(dh * xh * gpre), seq_first)
    dn = dh * (1.0 + mod_ref[4:5, :])
    _acc(ggpre_ref, _colsum(dn * xh), first)
    dx1 = dy_ref[rows, :] + _rms_bwd(dn * gpre, xh, r)
    dx1_ref[rows, :] = dx1
    mv = mix_ref[rows, :]
    rm = _rms(mv)
    mh = mv * rm
    gpost = gpost_ref[...]
    _acc(dgate_ref, _colsum(dx1 * mh * gpost), seq_first)
    dnm = dx1 * mod_ref[2:3, :]
    _acc(ggpost_ref, _colsum(dnm * mh), first)
    dmix_ref[rows, :] = _rms_bwd(dnm * gpost, mh, rm).astype(BF16)


def _bwd_pre_epilogue(dh, i, tps, in_refs, out_refs, rows, chunk):
    dx1_ref, x_ref, g_ref, mod_ref = in_refs
    gx_ref, dshift_ref, dscale_ref, gg_ref = out_refs
    seq_first = _first(i % tps == 0, chunk)
    xv = x_ref[rows, :]
    r = _rms(xv)
    xh = xv * r
    g = g_ref[...]
    _acc(dshift_ref, _colsum(dh), seq_first)
    _acc(dscale_ref, _colsum(dh * xh * g), seq_first)
    dn = dh * (1.0 + mod_ref[1:2, :])
    _acc(gg_ref, _colsum(dn * xh), _first(i == 0, chunk))
    gx_ref[rows, :] = dx1_ref[rows, :] + _rms_bwd(dn * g, xh, r)


def _ffn_up(h2, wgu, tm, tn):
    t, d = h2.shape
    f = wgu.shape[1]

    def body(h_ref, w_ref, gu_ref, act_ref):
        h = h_ref[...]
        g = _dot_nt(h, w_ref[0])
        u = _dot_nt(h, w_ref[1])
        gu_ref[0] = g.astype(BF16)
        gu_ref[1] = u.astype(BF16)
        act_ref[...] = (g * jax.nn.sigmoid(g) * u).astype(BF16)

    return pl.pallas_call(
        body, name="ffn_up", grid=(t // tm, f // tn),
        out_shape=(jax.ShapeDtypeStruct((2, t, f), BF16), jax.ShapeDtypeStruct((t, f), BF16)),
        in_specs=[pl.BlockSpec((tm, d), lambda i, j: (i, 0)), pl.BlockSpec((2, tn, d), lambda i, j: (0, j, 0))],
        out_specs=(pl.BlockSpec((2, tm, tn), lambda i, j: (0, i, j)), pl.BlockSpec((tm, tn), lambda i, j: (i, j))),
        compiler_params=_params(),
    )(h2, wgu)


def _ffn_act_bwd(df, wd, gu, tm, tn):
    t, d = df.shape
    f = wd.shape[0]

    def body(df_ref, w_ref, gu_ref, dgu_ref):
        da = _dot_nt(df_ref[...], w_ref[...])
        g = gu_ref[0].astype(F32)
        u = gu_ref[1].astype(F32)
        s = jax.nn.sigmoid(g)
        silu = g * s
        dgu_ref[0] = (da * u * (s + silu * (1.0 - s))).astype(BF16)
        dgu_ref[1] = (da * silu).astype(BF16)

    return pl.pallas_call(
        body, name="ffn_act_bwd", grid=(t // tm, f // tn),
        out_shape=jax.ShapeDtypeStruct((2, t, f), BF16),
        in_specs=[pl.BlockSpec((tm, d), lambda i, j: (i, 0)), pl.BlockSpec((tn, d), lambda i, j: (j, 0)),
                  pl.BlockSpec((2, tm, tn), lambda i, j: (0, i, j))],
        out_specs=pl.BlockSpec((2, tm, tn), lambda i, j: (0, i, j)),
        compiler_params=_params(),
    )(df, wd, gu)


SIGN_BIT = 0x80000000
Q_SCALE = 1.0 / math.sqrt(HEAD_DIM)


def _softplus(z):
    neg_abs = lax.bitcast_convert_type(lax.bitcast_convert_type(z, jnp.uint32) | jnp.uint32(SIGN_BIT), F32)
    return jnp.maximum(z, 0.0) + jnp.log(1.0 + jnp.exp(neg_abs))


def _hi_lo(v):
    hi = v.astype(BF16)
    return jnp.concatenate([hi, (v - hi.astype(F32)).astype(BF16)], axis=1)


def _emit_skewed(chains, lag=1):
    for t in range(max(len(ch) for ch in chains) + lag * (len(chains) - 1)):
        for c, ch in enumerate(chains):
            if 0 <= t - lag * c < len(ch):
                ch[t - lag * c]()


def _fwd_chain(blk, qs, k_ref, v_ref, c0, kb, cols, mask, ntri, lane, tq):
    st = {}

    def scores():
        st["z"] = _dot_nt(qs, k_ref[pl.ds(c0, tq), cols])

    def soft():
        sp = _softplus(st["z"])
        if mask is not None:
            sp = jnp.where(mask, sp, 0.0)
        st["parts"] = _hi_lo(sp)
        st["cur"] = blk["cur"]
        blk["cm"] = jnp.where(lane == kb, blk["cur"], blk["cm"])
        blk["cur"] = blk["cur"] - jnp.sum(sp, axis=1, keepdims=True)

    def sums():
        st["s"] = _dot_nn(st["parts"], ntri)

    def weights():
        w = jnp.exp(st["z"] + st["s"] + st["cur"])
        if mask is not None:
            w = jnp.where(mask, w, 0.0)
        st["w"] = w.astype(BF16)

    def out():
        p = _dot_nn(st["w"], v_ref[pl.ds(c0, tq), cols])
        blk["pv"] = p if blk["pv"] is None else blk["pv"] + p

    return [scores, soft, sums, weights, out]


def _bwd_chain(blk, qs, dos, cs, k_ref, v_ref, dk_ref, dv_ref, c0, kb, cols, mask, ntri, tri_i, lane, tq):
    st = {}

    def scores():
        st["z"] = _dot_nt(qs, k_ref[pl.ds(c0, tq), cols])
        st["dw"] = _dot_nt(dos, v_ref[pl.ds(c0, tq), cols])

    def soft():
        sp = _softplus(st["z"])
        if mask is not None:
            sp = jnp.where(mask, sp, 0.0)
        st["sp"] = sp
        st["parts"] = _hi_lo(sp)
        st["cur"] = jnp.sum(jnp.where(lane == kb, cs, 0.0), axis=1, keepdims=True)

    def sums():
        st["s"] = _dot_nn(st["parts"], ntri)

    def weights():
        w = jnp.exp(st["z"] + st["s"] + st["cur"])
        if mask is not None:
            w = jnp.where(mask, w, 0.0)
        ee = w * st["dw"]
        st["w"], st["ee"], st["ec"] = w.astype(BF16), ee, blk["ec"]
        blk["ec"] = blk["ec"] + jnp.sum(ee, axis=1, keepdims=True)

    def prefix():
        st["einc"] = _dot_nn(st["ee"].astype(BF16), tri_i)

    def dz():
        v = st["ee"] - jnp.exp(st["z"] - st["sp"]) * (st["einc"] + st["ec"])
        if mask is not None:
            v = jnp.where(mask, v, 0.0)
        st["dz"] = v.astype(BF16)

    def grads():
        p = _dot_nn(st["dz"], k_ref[pl.ds(c0, tq), cols])
        blk["dq"] = p if blk["dq"] is None else blk["dq"] + p
        dk_ref[pl.ds(c0, tq), :] += _dot_tn(st["dz"], qs)
        dv_ref[pl.ds(c0, tq), :] += _dot_tn(st["w"], dos)

    return [scores, soft, sums, weights, prefix, dz, grads]


def _stack_heads(v, lane, scale=None):
    if scale is not None:
        v = v * jnp.asarray(scale, v.dtype)
    zero = jnp.zeros_like(v)
    return jnp.concatenate([jnp.where(lane < HEAD_DIM, v, zero), jnp.where(lane >= HEAD_DIM, v, zero)], axis=0)


def _diag_mask(tq):
    row = lax.broadcasted_iota(jnp.int32, (2 * tq, tq), 0)
    col = lax.broadcasted_iota(jnp.int32, (2 * tq, tq), 1)
    return col < jnp.where(row >= tq, row - tq, row)


def _attn_fwd(proj, tri_after, n_seq, seq, ag_srcs, ag_out_shapes, ag_dests):
    t = proj.shape[0]
    tq = ATT_TILE
    npp = ATT_PAIRS
    n_blk = (proj.shape[1] // 4) // (npp * LANES)
    n_ag, n_ag_out = len(ag_srcs), len(ag_out_shapes)
    n_steps = n_seq * n_blk

    def body(q_ref, k_ref, v_ref, tri_ref, *rest):
        ag_src, rest = rest[:n_ag], rest[n_ag:]
        o_ref, cs_ref = rest[:2]
        ag_out, rest = rest[2:2 + n_ag_out], rest[2 + n_ag_out:]
        oacc, cmat, carry = rest[:3]
        ag_start, ag_forward, ag_finish = _ag_phases(ag_dests, ag_src, ag_out, *rest[3:])
        step = pl.program_id(0) * n_blk + pl.program_id(1)
        pl.when(step == 0)(ag_start)
        pl.when(step == (3 * n_steps) // 4)(ag_forward)
        lane = lax.broadcasted_iota(jnp.int32, (1, LANES), 1)
        ntri = tri_ref[...]
        diag = _diag_mask(tq)

        def q_tile(qi, _):
            r0 = pl.multiple_of(qi * tq, tq)
            qs = [_stack_heads(q_ref[pl.ds(r0, tq), pp * LANES:(pp + 1) * LANES], lane, Q_SCALE)
                  for pp in range(npp)]
            carry[...] = jnp.zeros_like(carry)
            cmat[...] = jnp.zeros_like(cmat)
            oacc[...] = jnp.zeros_like(oacc)

            def run_tiles(tiles):
                blocks = [dict(cur=carry[pp], cm=cmat[pp], pv=None) for pp in range(npp)]
                chains = []
                for kb, mask in tiles:
                    c0 = pl.multiple_of(kb * tq, tq)
                    for pp in range(npp):
                        chains.append(_fwd_chain(blocks[pp], qs[pp], k_ref, v_ref, c0, kb,
                                                 slice(pp * LANES, (pp + 1) * LANES), mask, ntri, lane, tq))
                _emit_skewed(chains)
                for pp in range(npp):
                    oacc[pp] += blocks[pp]["pv"]
                    cmat[pp] = blocks[pp]["cm"]
                    carry[pp] = blocks[pp]["cur"]

            odd = qi % 2

            @pl.when(odd == 0)
            def _():
                run_tiles([(qi, diag)])

            @pl.when(odd == 1)
            def _():
                run_tiles([(qi, diag), (qi - 1, None)])

            def pair(j, _):
                kb = qi - 1 - odd - 2 * j
                run_tiles([(kb, None), (kb - 1, None)])
                return 0

            lax.fori_loop(0, qi // 2, pair, 0)
            for pp in range(npp):
                c_off = 2 * pp * LANES
                cs_ref[pl.ds(r0, tq), c_off:c_off + LANES] = cmat[pp, 0:tq, :]
                cs_ref[pl.ds(r0, tq), c_off + LANES:c_off + 2 * LANES] = cmat[pp, tq:2 * tq, :]
                o_ref[pl.ds(r0, tq), pp * LANES:(pp + 1) * LANES] = jnp.where(
                    lane < HEAD_DIM, oacc[pp, 0:tq, :], oacc[pp, tq:2 * tq, :]).astype(BF16)
            return 0

        lax.fori_loop(0, seq // tq, q_tile, 0)
        pl.when(step == n_steps - 1)(ag_finish)

    wid = npp * LANES
    blk = lambda off: pl.BlockSpec((seq, wid), lambda b, p: (b, off + p))
    any_spec = pl.BlockSpec(memory_space=pl.ANY)
    return pl.pallas_call(
        body, name="attn_fwd", grid=(n_seq, n_blk),
        out_shape=(jax.ShapeDtypeStruct((2, t, n_blk * wid), BF16),
                   jax.ShapeDtypeStruct((t, n_blk * 2 * wid), F32), *ag_out_shapes),
        in_specs=[blk(0), blk(n_blk), blk(2 * n_blk), pl.BlockSpec((2 * tq, tq), lambda b, p: (0, 0))]
        + [any_spec] * n_ag,
        out_specs=(pl.BlockSpec((None, seq, wid), lambda b, p: (0, b, p)),
                   pl.BlockSpec((seq, 2 * wid), lambda b, p: (b, p)), *([any_spec] * n_ag_out)),
        scratch_shapes=[pltpu.VMEM((npp, 2 * tq, LANES), F32), pltpu.VMEM((npp, 2 * tq, LANES), F32),
                        pltpu.VMEM((npp, 2 * tq, 1), F32)] + _ag_scratch(n_ag),
        compiler_params=_params(),
    )(proj, proj, proj, tri_after, *ag_srcs)


def _attn_bwd(proj, dcat, cstats, tri_after, tri_incl, n_seq, seq, rs_sends):
    t = proj.shape[0]
    tq = ATT_TILE
    npp = ATT_PAIRS
    width = proj.shape[1] // 4
    n_blk = width // (npp * LANES)
    n_rs = len(rs_sends)
    rs_shapes = [r.shape for r in rs_sends]
    n_steps = n_seq * n_blk

    def body(q_ref, k_ref, v_ref, do_ref, cs_ref, tria_ref, trii_ref, *rest):
        rs_src, rest = rest[:n_rs], rest[n_rs:]
        out_ref = rest[0]
        rs_dst, rest = rest[1:1 + n_rs], rest[1 + n_rs:]
        dq_acc, dk_acc, dv_acc, ecarry = rest[:4]
        rs_start, rs_finish = _rs_phases(rs_shapes, rs_src, rs_dst, *rest[4:])
        step = pl.program_id(0) * n_blk + pl.program_id(1)
        pl.when(step == 0)(rs_start)
        lane = lax.broadcasted_iota(jnp.int32, (1, LANES), 1)
        ntri = tria_ref[...]
        tri_i = trii_ref[...]
        diag = _diag_mask(tq)
        dk_acc[...] = jnp.zeros_like(dk_acc)
        dv_acc[...] = jnp.zeros_like(dv_acc)

        def q_tile(qi, _):
            r0 = pl.multiple_of(qi * tq, tq)
            qs, dos, cs = [], [], []
            for pp in range(npp):
                cols = slice(pp * LANES, (pp + 1) * LANES)
                qs.append(_stack_heads(q_ref[pl.ds(r0, tq), cols], lane, Q_SCALE))
                dos.append(_stack_heads(do_ref[pl.ds(r0, tq), cols], lane))
                c_off = 2 * pp * LANES
                cs.append(jnp.concatenate([cs_ref[pl.ds(r0, tq), c_off:c_off + LANES],
                                           cs_ref[pl.ds(r0, tq), c_off + LANES:c_off + 2 * LANES]], axis=0))
            ecarry[...] = jnp.zeros_like(ecarry)
            dq_acc[...] = jnp.zeros_like(dq_acc)

            def run_tiles(tiles):
                blocks = [dict(ec=ecarry[pp], dq=None) for pp in range(npp)]
                chains = []
                for kb, mask in tiles:
                    c0 = pl.multiple_of(kb * tq, tq)
                    for pp in range(npp):
                        chains.append(_bwd_chain(
                            blocks[pp], qs[pp], dos[pp], cs[pp], k_ref, v_ref, dk_acc.at[pp], dv_acc.at[pp],
                            c0, kb, slice(pp * LANES, (pp + 1) * LANES), mask, ntri, tri_i, lane, tq))
                _emit_skewed(chains)
                for pp in range(npp):
                    dq_acc[pp] += blocks[pp]["dq"]
                    ecarry[pp] = blocks[pp]["ec"]

            def pair(j, _):
                run_tiles([(2 * j, None), (2 * j + 1, None)])
                return 0

            lax.fori_loop(0, qi // 2, pair, 0)
            odd = qi % 2

            @pl.when(odd == 0)
            def _():
                run_tiles([(qi, diag)])

            @pl.when(odd == 1)
            def _():
                run_tiles([(qi - 1, None), (qi, diag)])

            for pp in range(npp):
                dq = jnp.where(lane < HEAD_DIM, dq_acc[pp, 0:tq, :], dq_acc[pp, tq:2 * tq, :])
                out_ref[0, pl.ds(r0, tq), pp * LANES:(pp + 1) * LANES] = (dq * Q_SCALE).astype(BF16)
            return 0

        lax.fori_loop(0, seq // tq, q_tile, 0)
        for pp in range(npp):
            cols = slice(pp * LANES, (pp + 1) * LANES)
            out_ref[1, :, cols] = dk_acc[pp].astype(BF16)
            out_ref[2, :, cols] = dv_acc[pp].astype(BF16)
        pl.when(step == n_steps - 1)(rs_finish)

    wid = npp * LANES
    blk = lambda off: pl.BlockSpec((seq, wid), lambda b, p: (b, off + p))
    tri_spec = pl.BlockSpec((2 * tq, tq), lambda b, p: (0, 0))
    any_spec = pl.BlockSpec(memory_space=pl.ANY)
    return pl.pallas_call(
        body, name="attn_bwd", grid=(n_seq, n_blk),
        out_shape=(jax.ShapeDtypeStruct((4, t, width), BF16), *_rs_out(rs_sends)),
        in_specs=[blk(0), blk(n_blk), blk(2 * n_blk), pl.BlockSpec((seq, wid), lambda b, p: (b, p)),
                  pl.BlockSpec((seq, 2 * wid), lambda b, p: (b, p)), tri_spec,
                  pl.BlockSpec((tq, tq), lambda b, p: (0, 0))] + [any_spec] * n_rs,
        out_specs=(pl.BlockSpec((3, seq, wid), lambda b, p: (0, b, p)), *([any_spec] * n_rs)),
        scratch_shapes=[pltpu.VMEM((npp, 2 * tq, LANES), F32), pltpu.VMEM((npp, seq, LANES), F32),
                        pltpu.VMEM((npp, seq, LANES), F32), pltpu.VMEM((npp, 2 * tq, 1), F32)]
        + _rs_scratch(rs_sends),
        compiler_params=_params(),
    )(proj, proj, proj, dcat, cstats, tri_after, tri_incl, *rs_sends)


def _window_terms(g, rows):
    win = jnp.where(g == 0, POOL_WINDOWS[0], jnp.where(g == 1, POOL_WINDOWS[1],
                    jnp.where(g == 2, POOL_WINDOWS[2], POOL_WINDOWS[3])))
    cnt = jnp.minimum(rows + 1, win).astype(F32)
    return win, cnt


def _window_sum(v, g, rows, forward):
    s_len = v.shape[0]
    sums = []
    s = v
    for step in range(len(POOL_WINDOWS)):
        sh = 1 << step
        if forward:
            shifted = jnp.where(rows < s_len - sh, pltpu.roll(s, s_len - sh, axis=0), 0.0)
        else:
            shifted = jnp.where(rows >= sh, pltpu.roll(s, sh, axis=0), 0.0)
        s = s + shifted
        sums.append(s)
    return jnp.where(g == 0, sums[0], jnp.where(g == 1, sums[1], jnp.where(g == 2, sums[2], sums[3])))


def _pooled(u, g, rows):
    _, cnt = _window_terms(g, rows)
    return _window_sum(u, g, rows, forward=False) / cnt - u


def _pool_fwd(proj, w_pool, pool_scale, cat, n_seq, seq):
    n_grp = len(POOL_WINDOWS)
    u_off = 3 * (proj.shape[1] // 4) // LANES

    def body(u_ref, w_ref, s_ref, alias_ref, o_ref):
        del alias_ref
        g = pl.program_id(1)
        rows = lax.broadcasted_iota(jnp.int32, (seq, 1), 0)
        pooled = _pooled(u_ref[...].astype(F32), g, rows)
        y = _dot_nn(pooled.astype(BF16), w_ref[...].astype(BF16))
        o_ref[...] = (y * s_ref[...]).astype(BF16)

    return pl.pallas_call(
        body, name="pool_fwd", grid=(n_seq, n_grp),
        out_shape=jax.ShapeDtypeStruct(cat.shape, BF16),
        in_specs=[pl.BlockSpec((seq, LANES), lambda b, g: (b, u_off + g)),
                  pl.BlockSpec((None, POOL_GROUP_DIM, POOL_GROUP_DIM), lambda b, g: (g, 0, 0)),
                  pl.BlockSpec((1, POOL_GROUP_DIM), lambda b, g: (0, g)),
                  pl.BlockSpec(memory_space=pl.ANY)],
        out_specs=pl.BlockSpec((None, seq, LANES), lambda b, g: (1, b, g)),
        input_output_aliases={3: 0},
        compiler_params=_params(),
    )(proj, w_pool, pool_scale, cat)


def _pool_bwd(proj, dcat, w_pool, pool_scale, dqkv, n_seq, seq):
    n_grp = len(POOL_WINDOWS)
    width = proj.shape[1] // 4
    u_off = 3 * width // LANES
    dp_off = width // LANES

    def body(u_ref, dp_ref, w_ref, s_ref, alias_ref, du_ref, gw_ref, gs_ref):
        del alias_ref
        g = pl.program_id(0)
        b = pl.program_id(1)
        rows = lax.broadcasted_iota(jnp.int32, (seq, 1), 0)
        pooled = _pooled(u_ref[...].astype(F32), g, rows)
        pb = pooled.astype(BF16)
        wb = w_ref[...].astype(BF16)
        z = _dot_nn(pb, wb)
        dp = dp_ref[...].astype(F32)
        _acc(gs_ref, _colsum(dp * z), b == 0)
        dys = (dp * s_ref[...]).astype(BF16)
        _acc(gw_ref, _dot_tn(pb, dys), b == 0)
        dpooled = _dot_nt(dys, wb)
        _, cnt = _window_terms(g, rows)
        du = _window_sum(dpooled / cnt, g, rows, forward=True) - dpooled
        du_ref[...] = du.astype(BF16)

    t = proj.shape[0]
    return pl.pallas_call(
        body, name="pool_bwd", grid=(n_grp, n_seq),
        out_shape=(jax.ShapeDtypeStruct(dqkv.shape, BF16),
                   jax.ShapeDtypeStruct((n_grp, POOL_GROUP_DIM, POOL_GROUP_DIM), F32),
                   jax.ShapeDtypeStruct((1, n_grp * POOL_GROUP_DIM), F32)),
        in_specs=[pl.BlockSpec((seq, LANES), lambda g, b: (b, u_off + g)),
                  pl.BlockSpec((seq, LANES), lambda g, b: (b, dp_off + g)),
                  pl.BlockSpec((None, POOL_GROUP_DIM, POOL_GROUP_DIM), lambda g, b: (g, 0, 0)),
                  pl.BlockSpec((1, POOL_GROUP_DIM), lambda g, b: (0, g)),
                  pl.BlockSpec(memory_space=pl.ANY)],
        out_specs=(pl.BlockSpec((None, seq, LANES), lambda g, b: (3, b, g)),
                   pl.BlockSpec((None, POOL_GROUP_DIM, POOL_GROUP_DIM), lambda g, b: (g, 0, 0)),
                   pl.BlockSpec((1, POOL_GROUP_DIM), lambda g, b: (0, g))),
        input_output_aliases={4: 0},
        compiler_params=_params(),
    )(proj, dcat, w_pool, pool_scale, dqkv)


def _cond_fwd(c_all, w_cond, b_cols):
    n, _ = c_all.shape
    cols = w_cond.shape[1]

    def body(c_ref, w_ref, b_ref, o_ref):
        cv = c_ref[...]
        a = cv * jax.nn.sigmoid(cv)
        o_ref[...] = jnp.dot(a, w_ref[...], preferred_element_type=F32,
                             precision=lax.Precision.HIGHEST) + b_ref[...]

    return pl.pallas_call(
        body, name="cond_fwd", out_shape=jax.ShapeDtypeStruct((n, cols), F32),
        compiler_params=_params(),
    )(c_all, w_cond, b_cols)


def _cond_bwd(c_all, dmod_all, dmod_cols):
    n, d = c_all.shape
    cols = dmod_cols.shape[1]

    def body(c_ref, dm_ref, dmc_ref, gw_ref, gb_ref):
        cv = c_ref[...]
        a = cv * jax.nn.sigmoid(cv)
        gw_ref[...] = lax.dot_general(a, dmc_ref[...], (((0,), (0,)), ((), ())),
                                      preferred_element_type=F32, precision=lax.Precision.HIGHEST)
        gb_ref[...] = _colsum(dm_ref[...])

    return pl.pallas_call(
        body, name="cond_bwd",
        out_shape=(jax.ShapeDtypeStruct((d, cols), F32), jax.ShapeDtypeStruct((1, dmod_all.shape[1]), F32)),
        compiler_params=_params(),
    )(c_all, dmod_all, dmod_cols)


def _adamw_math(w, g, m, v):
    m = ADAM_B1 * m + (1.0 - ADAM_B1) * g
    v = ADAM_B2 * v + (1.0 - ADAM_B2) * (g * g)
    m_hat = m / (1.0 - ADAM_B1 ** ADAM_STEP)
    v_hat = v / (1.0 - ADAM_B2 ** ADAM_STEP)
    delta = -ADAM_LR * (m_hat / (jnp.sqrt(v_hat) + ADAM_EPS) + ADAM_WD * w)
    return delta, m, v


def _adamw(w, g, m, v, rows, name):
    r, cdim = w.shape

    def body(w_ref, g_ref, m_ref, v_ref, d_ref, nm_ref, nv_ref):
        d_ref[...], nm_ref[...], nv_ref[...] = _adamw_math(w_ref[...], g_ref[...], m_ref[...], v_ref[...])

    spec = pl.BlockSpec((rows, cdim), lambda i: (i, 0))
    sds = jax.ShapeDtypeStruct((r, cdim), F32)
    return pl.pallas_call(
        body, name=name, grid=(r // rows,), out_shape=(sds, sds, sds),
        in_specs=[spec] * 4, out_specs=(spec, spec, spec), compiler_params=_params(),
    )(w, g, m, v)


def _adamw_small(ws, gparts, ms, vs, name):
    n = len(ws)

    def body(*refs):
        w_r, g_r, m_r, v_r = refs[:n], refs[n:2 * n], refs[2 * n:3 * n], refs[3 * n:4 * n]
        outs = refs[4 * n:]
        for i in range(n):
            g = g_r[i][0]
            for dev in range(1, g_r[i].shape[0]):
                g = g + g_r[i][dev]
            delta, m, v = _adamw_math(w_r[i][...], g, m_r[i][...], v_r[i][...])
            outs[i][...] = g
            outs[n + i][...] = delta
            outs[2 * n + i][...] = m
            outs[3 * n + i][...] = v

    sds = [jax.ShapeDtypeStruct(w.shape, F32) for w in ws]
    return pl.pallas_call(
        body, name=name, out_shape=tuple(sds * 4), compiler_params=_params(),
    )(*ws, *gparts, *ms, *vs)


def kernel(x, c, w_cond, b_cond, g_mix_pre, g_mix_post, w_in, w_pool, pool_scale, w_out, g_ffn_pre, g_ffn_post, w_gate, w_up, w_down, loss_target, m_w_cond, m_b_cond, m_g_mix_pre, m_g_mix_post, m_w_in, m_w_pool, m_pool_scale, m_w_out, m_g_ffn_pre, m_g_ffn_post, m_w_gate, m_w_up, m_w_down, v_w_cond, v_b_cond, v_g_mix_pre, v_g_mix_post, v_w_in, v_w_pool, v_pool_scale, v_w_out, v_g_ffn_pre, v_g_ffn_post, v_w_gate, v_w_up, v_w_down):
    n_seq, seq, d = x.shape
    t = n_seq * seq
    xi, yi, ci = _mesh_pos()
    me = 4 * xi + 2 * yi + ci
    x2 = x.reshape(t, d)
    tgt2 = loss_target.reshape(t, d)
    in_rows = w_in.shape[2]
    out_rows = w_out.shape[1]
    ff_rows = w_gate.shape[2]
    ff = N_DEV * ff_rows
    cond_cols = w_cond.shape[2]

    win_t = w_in[0].T.astype(BF16)
    wout_s = w_out[0].astype(BF16)
    wg_t = w_gate[0].T.astype(BF16)
    wu_t = w_up[0].T.astype(BF16)
    wd_s = w_down[0].astype(BF16)
    (c_all,) = _all_gather([c], [jax.ShapeDtypeStruct((N_DEV, n_seq, d), F32)], [(0, ())], "ag_c")
    c_all = c_all.reshape(N_DEV * n_seq, d)

    b_cols = lax.dynamic_slice_in_dim(b_cond, me * cond_cols, cond_cols, axis=1)
    mod_cols = _cond_fwd(c_all, w_cond[0], b_cols)
    (mod_g,) = _all_gather([mod_cols], [jax.ShapeDtypeStruct((N_DEV,) + mod_cols.shape, F32)], [(0, ())], "ag_mod")
    mod_mine = lax.dynamic_slice_in_dim(mod_g, me * n_seq, n_seq, axis=1)
    mod = jnp.transpose(mod_mine, (1, 0, 2)).reshape(n_seq, N_MOD, d)

    h1, win_g = _pre_mix(x2, g_mix_pre, mod, seq, [win_t],
                         [jax.ShapeDtypeStruct((N_DEV, in_rows, d), BF16)], [(0, ())])
    win_full = win_g.reshape(N_DEV * in_rows, d)
    proj = _matmul(h1, win_full, "nt", BF16, 1024, 512, d, "proj")
    tq = ATT_TILE
    ids = jnp.arange(tq)
    tri_after = jnp.tile(-(ids[:, None] >= ids[None, :]).astype(BF16), (2, 1))
    tri_incl = (ids[:, None] <= ids[None, :]).astype(BF16)
    attn, cstats, wout_g, wgu_g, wd_g = _attn_fwd(
        proj, tri_after, n_seq, seq, [wout_s, wg_t, wu_t, wd_s],
        [jax.ShapeDtypeStruct((N_DEV, out_rows, d), BF16), jax.ShapeDtypeStruct((2, N_DEV, ff_rows, d), BF16),
         jax.ShapeDtypeStruct((N_DEV, ff_rows, d), BF16)],
        [(0, ()), (1, (0,)), (1, (1,)), (2, ())])
    wout_full = wout_g.reshape(N_DEV * out_rows, d)
    wgu_full = wgu_g.reshape(2, ff, d)
    wd_full = wd_g.reshape(ff, d)
    cat = _pool_fwd(proj, w_pool[0], pool_scale, attn, n_seq, seq)
    tok_f32, tok_bf16 = jax.ShapeDtypeStruct((t, d), F32), jax.ShapeDtypeStruct((t, d), BF16)
    seq_sds, vec_sds = jax.ShapeDtypeStruct((n_seq, 1, d), F32), jax.ShapeDtypeStruct((1, d), F32)
    mix, x1, h2 = _matmul_rows(
        cat, wout_full.reshape(2, d // 2, d), ROW_TILE, d // 2, seq, "mix_mid", _mid_epilogue,
        [x2, g_mix_post, g_ffn_pre, mod], ["tok", "vec", "vec", "mod"],
        [tok_f32, tok_f32, tok_bf16], ["tok", "tok", "tok"])
    gu, act = _ffn_up(h2, wgu_full, 512, ff // 2)
    loss_sum, dy, df, dgate_f, gg_ffn_post = _matmul_rows(
        act, wd_full, ROW_TILE, ff, seq, "ffn_down_post", _post_epilogue,
        [x1, tgt2, g_ffn_post, mod], ["tok", "tok", "vec", "mod"],
        [jax.ShapeDtypeStruct((1, LANES), F32), tok_f32, tok_bf16, seq_sds, vec_sds],
        ["loss", "tok", "tok", "seq", "vec"])

    dgu = _ffn_act_bwd(df, wd_full, gu, 512, ff // 2)
    gwd, gwd_b = _matmul(act, df, "tn", F32, ff // 2, d // 2, t, "grad_w_down", bf16_copy=True)
    gwgu, gwgu_b = _matmul(dgu, h2, "tn", F32, ff // 2, d // 2, t, "grad_w_gate_up", bf16_copy=True)
    dx1, dmix, dshift_f, dscale_f, dgate_m, gg_ffn_pre, gg_mix_post = _matmul_rows(
        dgu, wgu_full, ROW_TILE, ff, seq, "dh2_bwd_mid", _bwd_mid_epilogue,
        [dy, x1, mix, g_ffn_pre, g_mix_post, mod], ["tok", "tok", "tok", "vec", "vec", "mod"],
        [tok_f32, tok_bf16, seq_sds, seq_sds, seq_sds, vec_sds, vec_sds],
        ["tok", "tok", "seq", "seq", "seq", "vec", "vec"])
    dcat = _matmul(dmix, wout_full, "nt", BF16, 1024, 512, d, "dcat")
    gwout, gwout_b = _matmul(cat, dmix, "tn", F32, d // 2, d // 2, t, "grad_w_out", bf16_copy=True)
    dqkv, rv_wgu, rv_wd, rv_wout = _attn_bwd(
        proj, dcat, cstats, tri_after, tri_incl, n_seq, seq,
        [gwgu_b.reshape(2, N_DEV, ff_rows, d), gwd_b.reshape(1, N_DEV, ff_rows, d),
         gwout_b.reshape(1, N_DEV, out_rows, d)])
    dproj, gw_pool, gs_pool = _pool_bwd(proj, dcat, w_pool[0], pool_scale, dqkv, n_seq, seq)
    pad_d = lambda v: jnp.pad(v, ((0, 0), (0, d - v.shape[1])))
    n_gw = gw_pool.size // d
    early = jnp.concatenate(
        [gg_mix_post, gg_ffn_pre, gg_ffn_post, pad_d(gs_pool), pad_d(loss_sum), jnp.zeros((3, d), F32),
         gw_pool.reshape(n_gw, d),
         jnp.concatenate([dgate_m, dshift_f, dscale_f, dgate_f], axis=1).reshape(n_seq * 4, d)], axis=0)
    gwin, gwin_b, early_g = _matmul(
        dproj, h1, "tn", F32, d // 2, d // 2, t, "grad_w_in", bf16_copy=True,
        ag=([early], [jax.ShapeDtypeStruct((N_DEV,) + early.shape, F32)], [(0, ())]))
    grad_x, dshift_m, dscale_m, gg_mix_pre, rv_win = _matmul_rows(
        dproj, win_full.reshape(4, d // 2, d), ROW_TILE, d // 2, seq, "dh1_bwd_pre", _bwd_pre_epilogue,
        [dx1, x2, g_mix_pre, mod], ["tok", "tok", "vec", "mod"],
        [tok_f32, seq_sds, seq_sds, vec_sds], ["tok", "seq", "seq", "vec"],
        rs_sends=[gwin_b.reshape(1, N_DEV, in_rows, d)])

    r_wgu = _rs_final(gwgu.reshape(2, N_DEV, ff_rows, d), rv_wgu, "rs_final_gate_up")
    r_wd = _rs_final(gwd.reshape(1, N_DEV, ff_rows, d), rv_wd, "rs_final_down")
    r_wout = _rs_final(gwout.reshape(1, N_DEV, out_rows, d), rv_wout, "rs_final_out")
    r_win = _rs_final(gwin.reshape(1, N_DEV, in_rows, d), rv_win, "rs_final_in")
    grad_w_in = r_win[0].T
    grad_w_out = r_wout[0]
    grad_w_down = r_wd[0]

    late = jnp.concatenate([gg_mix_pre, dshift_m.reshape(n_seq, d), dscale_m.reshape(n_seq, d),
                            jnp.zeros((8 - 1 - 2 * n_seq, d), F32)], axis=0)
    (late_g,) = _all_gather([late], [jax.ShapeDtypeStruct((N_DEV,) + late.shape, F32)], [(0, ())], "ag_late")
    loss = jnp.sum(early_g[:, 4, 0]) * (0.5 / d)
    dmod_all = jnp.concatenate(
        [late_g[:, 1:1 + n_seq, None, :], late_g[:, 1 + n_seq:1 + 2 * n_seq, None, :],
         early_g[:, 8 + n_gw:, :].reshape(N_DEV, n_seq, 4, d)], axis=2).reshape(N_DEV * n_seq, N_MOD * d)
    dmod_cols = lax.dynamic_slice_in_dim(dmod_all, me * cond_cols, cond_cols, axis=1)
    grad_w_cond, grad_b_cond = _cond_bwd(c_all, dmod_all, dmod_cols)

    small_ws = [g_mix_pre, g_mix_post, g_ffn_pre, g_ffn_post, pool_scale, w_pool.reshape(-1, POOL_GROUP_DIM)]
    small_ms = [m_g_mix_pre, m_g_mix_post, m_g_ffn_pre, m_g_ffn_post, m_pool_scale, m_w_pool.reshape(-1, POOL_GROUP_DIM)]
    small_vs = [v_g_mix_pre, v_g_mix_post, v_g_ffn_pre, v_g_ffn_post, v_pool_scale, v_w_pool.reshape(-1, POOL_GROUP_DIM)]
    small_gparts = [late_g[:, 0:1, :], early_g[:, 0:1, :], early_g[:, 1:2, :], early_g[:, 2:3, :],
                    early_g[:, 3:4, :pool_scale.shape[1]],
                    early_g[:, 8:8 + n_gw, :].reshape(N_DEV, -1, POOL_GROUP_DIM)]
    so = _adamw_small(small_ws, small_gparts, small_ms, small_vs, "adamw_small")
    ns = len(small_ws)
    sg, sdl, sm, sv = so[:ns], so[ns:2 * ns], so[2 * ns:3 * ns], so[3 * ns:]
    pool_shape = w_pool.shape
    fix = lambda lst: [lst[0], lst[1], lst[2], lst[3], lst[4], lst[5].reshape(pool_shape)]
    sg, sdl, sm, sv = fix(sg), fix(sdl), fix(sm), fix(sv)

    def big(w, g, m, v, rows, name):
        dl, nm, nv = _adamw(w[0], g, m[0], v[0], rows, name)
        return g[None], dl[None], nm[None], nv[None]

    o_cond = big(w_cond, grad_w_cond, m_w_cond, v_w_cond, 256, "adamw_w_cond")
    o_bcond = _adamw(b_cond, grad_b_cond, m_b_cond, v_b_cond, 1, "adamw_b_cond")
    o_bcond = (grad_b_cond,) + tuple(o_bcond)
    o_in = big(w_in, grad_w_in, m_w_in, v_w_in, 256, "adamw_w_in")
    o_out = big(w_out, grad_w_out, m_w_out, v_w_out, out_rows, "adamw_w_out")
    def big_t(w, g_t, m, v, name):
        outs = _adamw(w[0].T, g_t, m[0].T, v[0].T, g_t.shape[0], name)
        return tuple(o.T[None] for o in (g_t,) + tuple(outs))

    o_gate = big_t(w_gate, r_wgu[0], m_w_gate, v_w_gate, "adamw_w_gate")
    o_up = big_t(w_up, r_wgu[1], m_w_up, v_w_up, "adamw_w_up")
    o_down = big(w_down, grad_w_down, m_w_down, v_w_down, ff_rows, "adamw_w_down")

    def pick(k):
        small_k = [sg, sdl, sm, sv][k]
        return [o_cond[k], o_bcond[k], small_k[0], small_k[1], o_in[k], small_k[5], small_k[4], o_out[k],
                small_k[2], small_k[3], o_gate[k], o_up[k], o_down[k]]

    return (loss, grad_x.reshape(n_seq, seq, d), *pick(0), *pick(1), *pick(2), *pick(3))
```

```python
import functools
import math

import jax
import jax.numpy as jnp
from jax import lax
from jax.experimental import pallas as pl
from jax.experimental.pallas import tpu as pltpu

F32 = jnp.float32
BF16 = jnp.bfloat16
MESH = pl.DeviceIdType.MESH

N_DEV = 8
HEAD_DIM = 64
LANES = 128
POOL_WINDOWS = (2, 4, 8, 16)
POOL_GROUP_DIM = 128
N_MOD = 6
EPS = 1e-6
ATT_TILE = 256
ATT_PAIRS = 2
VMEM_LIMIT = 56 * 1024 * 1024

ADAM_LR = 0.001
ADAM_B1 = 0.9
ADAM_B2 = 0.999
ADAM_EPS = 1e-08
ADAM_WD = 0.01
ADAM_STEP = 10


def _params(**kw):
    return pltpu.CompilerParams(vmem_limit_bytes=VMEM_LIMIT, **kw)


def _dot_nn(a, b):
    return jnp.dot(a, b, preferred_element_type=F32)


def _dot_nt(a, b):
    return lax.dot_general(a, b, (((1,), (1,)), ((), ())), preferred_element_type=F32)


def _dot_tn(a, b):
    return lax.dot_general(a, b, (((0,), (0,)), ((), ())), preferred_element_type=F32)


def _mesh_pos():
    return lax.axis_index("x"), lax.axis_index("y"), lax.axis_index("c")


def _ag_phases(dests, src, outs, send_sems, recv_sems, local_sems):
    n = len(src)
    x, y, c = _mesh_pos()
    me, sibling = (x, y, c), (x, y, 1 - c)
    chips = [(1 - x, y), (x, 1 - y), (1 - x, 1 - y)]

    def slot(i, dev):
        oi, prefix = dests[i]
        px, py, pc = dev
        return outs[oi].at[prefix + (4 * px + 2 * py + pc,)]

    def copy(i, k, block, to, from_src=False):
        return pltpu.make_async_remote_copy(
            src_ref=src[i] if from_src else slot(i, block), dst_ref=slot(i, block),
            send_sem=send_sems.at[i, k], recv_sem=recv_sems.at[i, k],
            device_id=to, device_id_type=MESH)

    def mine(i):
        return pltpu.make_async_copy(src[i], slot(i, me), local_sems.at[i])

    def first(i):
        return [copy(i, 0, me, sibling, from_src=True)] + [
            copy(i, 1 + j, me, (*chip, c), from_src=True) for j, chip in enumerate(chips)]

    def passed(i, j):
        return copy(i, 4 + j, (*chips[j], c), sibling)

    def start():
        for i in range(n):
            mine(i).start()
        for i in range(n):
            for cp in first(i):
                cp.start()

    def forward():
        for j, chip in enumerate(chips):
            for i in range(n):
                copy(i, 1 + j, (*chip, c), me).wait_recv()
                passed(i, j).start()

    def finish():
        for i in range(n):
            copy(i, 0, sibling, me).wait_recv()
            for j, chip in enumerate(chips):
                copy(i, 4 + j, (*chip, 1 - c), me).wait_recv()
        for i in range(n):
            for cp in first(i) + [passed(i, j) for j in range(3)]:
                cp.wait_send()
            mine(i).wait()

    return start, forward, finish


def _ag_scratch(n):
    return [pltpu.SemaphoreType.DMA((n, 7)), pltpu.SemaphoreType.DMA((n, 7)), pltpu.SemaphoreType.DMA((n,))]


def _all_gather(srcs, out_shapes, dests, name):
    n = len(srcs)

    def body(*refs):
        src = refs[:n]
        outs = refs[n:n + len(out_shapes)]
        start, forward, finish = _ag_phases(dests, src, outs, *refs[n + len(out_shapes):])
        start()
        forward()
        finish()

    any_spec = pl.BlockSpec(memory_space=pl.ANY)
    return pl.pallas_call(
        body, name=name,
        out_shape=tuple(out_shapes),
        in_specs=[any_spec] * n,
        out_specs=tuple([any_spec] * len(out_shapes)),
        scratch_shapes=_ag_scratch(n),
    )(*srcs)


def _rs_phases(shapes, src, dst, send_sems, recv_sems):
    x, y, c = _mesh_pos()

    def copies():
        out = []
        n = 0
        for i, shp in enumerate(shapes):
            for m in range(shp[0]):
                for k in range(1, N_DEV):
                    px, py, pc = x ^ (k >> 2), y ^ ((k >> 1) & 1), c ^ (k & 1)
                    out.append(pltpu.make_async_remote_copy(
                        src_ref=src[i].at[m, 4 * px + 2 * py + pc], dst_ref=dst[i].at[m, k - 1],
                        send_sem=send_sems.at[n], recv_sem=recv_sems.at[n],
                        device_id=(px, py, pc), device_id_type=MESH))
                    n += 1
        return out

    def start():
        for cp in copies():
            cp.start()

    def finish():
        for cp in copies():
            cp.wait_send()
        for cp in copies():
            cp.wait_recv()

    return start, finish


def _rs_out(sends):
    return [jax.ShapeDtypeStruct((s.shape[0], N_DEV - 1) + s.shape[2:], s.dtype) for s in sends]


def _rs_scratch(sends):
    total = sum((N_DEV - 1) * s.shape[0] for s in sends)
    return [pltpu.SemaphoreType.DMA((total,)), pltpu.SemaphoreType.DMA((total,))]


def _rs_final(mine, recv, name):
    m_n, _, r, cdim = mine.shape
    x, y, c = _mesh_pos()
    me = jnp.reshape(4 * x + 2 * y + c, (1,)).astype(jnp.int32)

    def body(me_ref, p_ref, r_ref, o_ref):
        del me_ref
        s = p_ref[...]
        for k in range(N_DEV - 1):
            s = s + r_ref[k].astype(F32)
        o_ref[...] = s

    return pl.pallas_call(
        body, name=name, out_shape=jax.ShapeDtypeStruct((m_n, r, cdim), F32),
        grid_spec=pltpu.PrefetchScalarGridSpec(
            num_scalar_prefetch=1, grid=(m_n,),
            in_specs=[pl.BlockSpec((None, None, r, cdim), lambda m, s: (m, s[0], 0, 0)),
                      pl.BlockSpec((None, N_DEV - 1, r, cdim), lambda m, s: (m, 0, 0, 0))],
            out_specs=pl.BlockSpec((None, r, cdim), lambda m, s: (m, 0, 0))),
        compiler_params=_params(),
    )(me, mine, recv)


def _matmul(a, b, mode, out_dtype, tm, tn, tk, name, bf16_copy=False, rs_sends=(), ag=None):
    ga = a.shape[0] if a.ndim == 3 else None
    gb = b.shape[0] if b.ndim == 3 else None
    a2, b2 = a.shape[-2:], b.shape[-2:]
    if mode == "nn":
        (m, k), n = a2, b2[1]
    elif mode == "nt":
        (m, k), n = a2, b2[0]
    else:
        (k, m), n = a2, b2[1]
    assert m % tm == 0 and n % tn == 0 and k % tk == 0, (name, m, n, k)
    nk = k // tk
    g_n = ga or 1
    batch_out = mode == "tn" and ga is not None
    n_red = nk if batch_out else nk * g_n
    dot = {"nn": _dot_nn, "nt": _dot_nt, "tn": _dot_tn}[mode]
    acc_in_out = out_dtype == F32

    n_rs = len(rs_sends)
    rs_shapes = [r.shape for r in rs_sends]
    ag_srcs, ag_out_shapes, ag_dests = ag if ag is not None else ((), (), ())
    n_ag, n_ag_out = len(ag_srcs), len(ag_out_shapes)
    n_out = 2 if bf16_copy else 1
    assert not bf16_copy or acc_in_out
    assert not (n_rs and n_ag)

    def body(a_ref, b_ref, *rest):
        rs_src, rest = rest[:n_rs], rest[n_rs:]
        ag_src, rest = rest[:n_ag], rest[n_ag:]
        o_ref = rest[0]
        copy_ref = rest[1] if bf16_copy else None
        rs_dst, rest = rest[n_out:n_out + n_rs], rest[n_out + n_rs:]
        ag_out, scratch = rest[:n_ag_out], rest[n_ag_out:]
        first = functools.reduce(jnp.logical_and, [pl.program_id(ax) == 0 for ax in range(4)])
        last = functools.reduce(jnp.logical_and, [pl.program_id(ax) == grid[ax] - 1 for ax in range(4)])
        if n_rs:
            rs_start, rs_finish = _rs_phases(rs_shapes, rs_src, rs_dst, *scratch[-2:])
            pl.when(first)(rs_start)
        if n_ag:
            ag_start, ag_forward, ag_finish = _ag_phases(ag_dests, ag_src, ag_out, *scratch[-3:])
            pl.when(first)(ag_start)
        p = dot(a_ref[...], b_ref[...])
        kk = pl.program_id(3) if batch_out else pl.program_id(2) * nk + pl.program_id(3)
        if n_red == 1:
            o_ref[...] = p.astype(out_dtype)
            if bf16_copy:
                copy_ref[...] = p.astype(BF16)
        else:
            acc = o_ref if acc_in_out else scratch[0]

            @pl.when(kk == 0)
            def _():
                acc[...] = p

            @pl.when(kk > 0)
            def _():
                acc[...] += p

            @pl.when(kk == n_red - 1)
            def _():
                if not acc_in_out:
                    o_ref[...] = acc[...].astype(out_dtype)
                if bf16_copy:
                    copy_ref[...] = acc[...].astype(BF16)

        if n_rs:
            pl.when(last)(rs_finish)
        if n_ag:
            @pl.when(last)
            def _():
                ag_forward()
                ag_finish()

    def order(ids):
        return ids if batch_out else (ids[2], ids[0], ids[1], ids[3])

    def a_idx(*ids):
        g, i, j, kq = order(ids)
        blk = {"nn": (i, kq), "nt": (i, kq), "tn": (kq, i)}[mode]
        return (g,) + blk if ga is not None else blk

    def b_idx(*ids):
        g, i, j, kq = order(ids)
        blk = {"nn": (kq, j), "nt": (j, kq), "tn": (kq, j)}[mode]
        return (g,) + blk if gb is not None else blk

    def o_idx(*ids):
        g, i, j, kq = order(ids)
        return (g, i, j) if batch_out else (i, j)

    a_blk = {"nn": (tm, tk), "nt": (tm, tk), "tn": (tk, tm)}[mode]
    b_blk = {"nn": (tk, tn), "nt": (tn, tk), "tn": (tk, tn)}[mode]
    if ga is not None:
        a_blk = (None,) + a_blk
    if gb is not None:
        b_blk = (None,) + b_blk
    if batch_out:
        out_shape = jax.ShapeDtypeStruct((g_n, m, n), out_dtype)
        o_blk = (None, tm, tn)
        grid = (g_n, m // tm, n // tn, nk)
    else:
        out_shape = jax.ShapeDtypeStruct((m, n), out_dtype)
        o_blk = (tm, tn)
        grid = (m // tm, n // tn, g_n, nk)
    scratch = [] if (acc_in_out or n_red == 1) else [pltpu.VMEM((tm, tn), F32)]
    any_spec = pl.BlockSpec(memory_space=pl.ANY)
    out_shapes = [out_shape] + ([jax.ShapeDtypeStruct(out_shape.shape, BF16)] if bf16_copy else [])
    res = pl.pallas_call(
        body, name=name, out_shape=tuple(out_shapes + _rs_out(rs_sends) + list(ag_out_shapes)), grid=grid,
        in_specs=[pl.BlockSpec(a_blk, a_idx), pl.BlockSpec(b_blk, b_idx)] + [any_spec] * (n_rs + n_ag),
        out_specs=tuple([pl.BlockSpec(o_blk, o_idx)] * n_out + [any_spec] * (n_rs + n_ag_out)),
        scratch_shapes=scratch + (_rs_scratch(rs_sends) if n_rs else []) + (_ag_scratch(n_ag) if n_ag else []),
        compiler_params=_params(),
    )(a, b, *rs_sends, *ag_srcs)
    return res if len(res) > 1 else res[0]


EW_TILE = 256
ROW_TILE = 512
EPILOGUE_CHUNKS = 8
MXU_WIDTH = 256


def _rms(v):
    return lax.rsqrt(jnp.mean(v * v, axis=-1, keepdims=True) + EPS)


def _rms_bwd(dhat, vh, r):
    return r * (dhat - vh * jnp.mean(dhat * vh, axis=-1, keepdims=True))


def _tok_spec(tm, d):
    return pl.BlockSpec((tm, d), lambda i: (i, 0))


def _vec_spec(d):
    return pl.BlockSpec((1, d), lambda i: (0, 0))


def _mod_spec(tiles_per_seq, d):
    return pl.BlockSpec((None, N_MOD, d), lambda i: (i // tiles_per_seq, 0, 0))


def _seq_acc_spec(tiles_per_seq, d):
    return pl.BlockSpec((None, 1, d), lambda i: (i // tiles_per_seq, 0, 0))


def _acc(ref, val, first):
    if first is False:
        ref[...] += val
        return

    @pl.when(first)
    def _():
        ref[...] = val

    @pl.when(jnp.logical_not(first))
    def _():
        ref[...] += val


def _colsum(v):
    return jnp.sum(v, axis=0, keepdims=True)


def _pre_mix(x2, g_pre, mod, seq, ag_srcs, ag_out_shapes, ag_dests):
    t, d = x2.shape
    tm = EW_TILE
    n_steps = t // tm
    n_ag, n_ag_out = len(ag_srcs), len(ag_out_shapes)

    def body(x_ref, g_ref, mod_ref, *rest):
        ag_src, h_ref = rest[:n_ag], rest[n_ag]
        ag_out, sems = rest[n_ag + 1:n_ag + 1 + n_ag_out], rest[n_ag + 1 + n_ag_out:]
        ag_start, ag_forward, ag_finish = _ag_phases(ag_dests, ag_src, ag_out, *sems)
        step = pl.program_id(0)
        pl.when(step == 0)(ag_start)
        xv = x_ref[...]
        n = xv * _rms(xv) * g_ref[...]
        h_ref[...] = (n * (1.0 + mod_ref[1:2, :]) + mod_ref[0:1, :]).astype(BF16)

        @pl.when(step == n_steps - 1)
        def _():
            ag_forward()
            ag_finish()

    any_spec = pl.BlockSpec(memory_space=pl.ANY)
    return pl.pallas_call(
        body, name="pre_mix", out_shape=(jax.ShapeDtypeStruct((t, d), BF16), *ag_out_shapes), grid=(n_steps,),
        in_specs=[_tok_spec(tm, d), _vec_spec(d), _mod_spec(seq // tm, d)] + [any_spec] * n_ag,
        out_specs=(_tok_spec(tm, d), *([any_spec] * n_ag_out)),
        scratch_shapes=_ag_scratch(n_ag), compiler_params=_params(),
    )(x2, g_pre, mod, *ag_srcs)


def _matmul_rows(a, b, tm, seq, name, epilogue, ep_in, ep_in_kinds, ep_out, ep_out_kinds, rs_sends=()):
    g_n = a.shape[0] if a.ndim == 3 else None
    (m, k), n = a.shape[-2:], b.shape[-1]
    tps = seq // tm
    n_i = m // tm
    n_rs = len(rs_sends)
    rs_shapes = [r.shape for r in rs_sends]
    n_in, n_out = len(ep_in), len(ep_out)
    n_cols = n // MXU_WIDTH
    rc, cw = tm // EPILOGUE_CHUNKS, n // n_cols

    def prev(i):
        return jnp.maximum(i - 1, 0)

    def spec(kind):
        return {"tok": pl.BlockSpec((tm, n), lambda i: (prev(i), 0)),
                "vec": pl.BlockSpec((1, n), lambda i: (0, 0)),
                "mod": pl.BlockSpec((None, N_MOD, n), lambda i: (prev(i) // tps, 0, 0)),
                "seq": pl.BlockSpec((None, 1, n), lambda i: (prev(i) // tps, 0, 0)),
                "loss": pl.BlockSpec((1, LANES), lambda i: (0, 0))}[kind]

    def body(a_ref, b_ref, *rest):
        in_refs, rest = rest[:n_in], rest[n_in:]
        rs_src, rest = rest[:n_rs], rest[n_rs:]
        out_refs, rest = rest[:n_out], rest[n_out:]
        rs_dst, rest = rest[:n_rs], rest[n_rs:]
        fin = rest[0]
        i = pl.program_id(0)
        if n_rs:
            rs_start, rs_finish = _rs_phases(rs_shapes, rs_src, rs_dst, *rest[1:])
            pl.when(i == 0)(rs_start)

        def product(cols):
            if g_n is None:
                return _dot_nn(a_ref[...], b_ref[:, cols])
            p = _dot_nn(a_ref[0], b_ref[0, :, cols])
            for g in range(1, g_n):
                p = p + _dot_nn(a_ref[g], b_ref[g, :, cols])
            return p

        def step(with_epilogue, with_matmul):
            parts = []
            for c in range(EPILOGUE_CHUNKS):
                if with_epilogue:
                    rows = pl.ds(c * rc, rc)
                    epilogue(fin[rows, :], i - 1, tps, in_refs, out_refs, rows, c)
                while with_matmul and len(parts) < (c + 1) * n_cols // EPILOGUE_CHUNKS:
                    cols = slice(len(parts) * cw, (len(parts) + 1) * cw)
                    parts.append((cols, product(cols)))
            for cols, v in parts:
                fin[:, cols] = v

        pl.when(i == 0)(functools.partial(step, False, True))
        pl.when(jnp.logical_and(i > 0, i < n_i))(functools.partial(step, True, True))
        pl.when(i == n_i)(functools.partial(step, True, False))

        if n_rs:
            pl.when(i == n_i)(rs_finish)

    def row(i):
        return jnp.minimum(i, n_i - 1)

    if g_n is None:
        a_spec = pl.BlockSpec((tm, k), lambda i: (row(i), 0))
        b_spec = pl.BlockSpec(b.shape, lambda i: (0, 0), pipeline_mode=pl.Buffered(1))
    else:
        a_spec = pl.BlockSpec((g_n, tm, k), lambda i: (0, row(i), 0))
        b_spec = pl.BlockSpec(b.shape, lambda i: (0, 0, 0), pipeline_mode=pl.Buffered(1))
    any_spec = pl.BlockSpec(memory_space=pl.ANY)
    res = pl.pallas_call(
        body, name=name, grid=(n_i + 1,), out_shape=tuple(list(ep_out) + _rs_out(rs_sends)),
        in_specs=[a_spec, b_spec] + [spec(kd) for kd in ep_in_kinds] + [any_spec] * n_rs,
        out_specs=tuple([spec(kd) for kd in ep_out_kinds] + [any_spec] * n_rs),
        scratch_shapes=[pltpu.VMEM((tm, n), F32)] + (_rs_scratch(rs_sends) if n_rs else []),
        compiler_params=_params(),
    )(a, b, *ep_in, *rs_sends)
    return res


def _first(cond, chunk):
    return cond if chunk == 0 else False


def _mid_epilogue(mv, i, tps, in_refs, out_refs, rows, chunk):
    x_ref, gpost_ref, gpre_ref, mod_ref = in_refs
    mix_ref, x1_ref, h2_ref = out_refs
    mix_ref[rows, :] = mv
    x1 = x_ref[rows, :] + mod_ref[2:3, :] * (mv * _rms(mv) * gpost_ref[...])
    x1_ref[rows, :] = x1
    n = x1 * _rms(x1) * gpre_ref[...]
    h2_ref[rows, :] = (n * (1.0 + mod_ref[4:5, :]) + mod_ref[3:4, :]).astype(BF16)


def _post_epilogue(fv, i, tps, in_refs, out_refs, rows, chunk):
    x1_ref, tgt_ref, g_ref, mod_ref = in_refs
    loss_ref, dy_ref, df_ref, dgate_ref, gg_ref = out_refs
    d = fv.shape[1]
    r = _rms(fv)
    fh = fv * r
    nf = fh * g_ref[...]
    gate = mod_ref[5:6, :]
    err = x1_ref[rows, :] + gate * nf - tgt_ref[rows, :]
    _acc(loss_ref, jnp.sum(_colsum(err * err), axis=1, keepdims=True) * jnp.ones((1, LANES), F32),
         _first(i == 0, chunk))
    dy = err * (1.0 / d)
    dy_ref[rows, :] = dy
    _acc(dgate_ref, _colsum(dy * nf), _first(i % tps == 0, chunk))
    dn = dy * gate
    _acc(gg_ref, _colsum(dn * fh), _first(i == 0, chunk))
    df_ref[rows, :] = _rms_bwd(dn * g_ref[...], fh, r).astype(BF16)


def _bwd_mid_epilogue(dh, i, tps, in_refs, out_refs, rows, chunk):
    dy_ref, x1_ref, mix_ref, gpre_ref, gpost_ref, mod_ref = in_refs
    dx1_ref, dmix_ref, dshift_ref, dscale_ref, dgate_ref, ggpre_ref, ggpost_ref = out_refs
    seq_first, first = _first(i % tps == 0, chunk), _first(i == 0, chunk)
    x1 = x1_ref[rows, :]
    r = _rms(x1)
    xh = x1 * r
    gpre = gpre_ref[...]
    _acc(dshift_ref, _colsum(dh), seq_first)
    _acc(dscale_ref, _colsum(dh * xh * gpre), seq_first)
    dn = dh * (1.0 + mod_ref[4:5, :])
    _acc(ggpre_ref, _colsum(dn * xh), first)
    dx1 = dy_ref[rows, :] + _rms_bwd(dn * gpre, xh, r)
    dx1_ref[rows, :] = dx1
    mv = mix_ref[rows, :]
    rm = _rms(mv)
    mh = mv * rm
    gpost = gpost_ref[...]
    _acc(dgate_ref, _colsum(dx1 * mh * gpost), seq_first)
    dnm = dx1 * mod_ref[2:3, :]
    _acc(ggpost_ref, _colsum(dnm * mh), first)
    dmix_ref[rows, :] = _rms_bwd(dnm * gpost, mh, rm).astype(BF16)


def _bwd_pre_epilogue(dh, i, tps, in_refs, out_refs, rows, chunk):
    dx1_ref, x_ref, g_ref, mod_ref = in_refs
    gx_ref, dshift_ref, dscale_ref, gg_ref = out_refs
    seq_first = _first(i % tps == 0, chunk)
    xv = x_ref[rows, :]
    r = _rms(xv)
    xh = xv * r
    g = g_ref[...]
    _acc(dshift_ref, _colsum(dh), seq_first)
    _acc(dscale_ref, _colsum(dh * xh * g), seq_first)
    dn = dh * (1.0 + mod_ref[1:2, :])
    _acc(gg_ref, _colsum(dn * xh), _first(i == 0, chunk))
    gx_ref[rows, :] = dx1_ref[rows, :] + _rms_bwd(dn * g, xh, r)


def _ffn_up(h2, wgu, tm, tn):
    t, d = h2.shape
    f = wgu.shape[1]

    def body(h_ref, w_ref, gu_ref, act_ref):
        h = h_ref[...]
        g = _dot_nt(h, w_ref[0])
        u = _dot_nt(h, w_ref[1])
        gu_ref[0] = g.astype(BF16)
        gu_ref[1] = u.astype(BF16)
        act_ref[...] = (g * jax.nn.sigmoid(g) * u).astype(BF16)

    return pl.pallas_call(
        body, name="ffn_up", grid=(f // tn, t // tm),
        out_shape=(jax.ShapeDtypeStruct((2, t, f), BF16), jax.ShapeDtypeStruct((t, f), BF16)),
        in_specs=[pl.BlockSpec((tm, d), lambda j, i: (i, 0)), pl.BlockSpec((2, tn, d), lambda j, i: (0, j, 0))],
        out_specs=(pl.BlockSpec((2, tm, tn), lambda j, i: (0, i, j)), pl.BlockSpec((tm, tn), lambda j, i: (i, j))),
        compiler_params=_params(),
    )(h2, wgu)


def _ffn_act_bwd(df, wd, gu, tm, tn):
    t, d = df.shape
    f = wd.shape[0]

    def body(df_ref, w_ref, gu_ref, dgu_ref):
        da = _dot_nt(df_ref[...], w_ref[...])
        g = gu_ref[0].astype(F32)
        u = gu_ref[1].astype(F32)
        s = jax.nn.sigmoid(g)
        silu = g * s
        dgu_ref[0] = (da * u * (s + silu * (1.0 - s))).astype(BF16)
        dgu_ref[1] = (da * silu).astype(BF16)

    return pl.pallas_call(
        body, name="ffn_act_bwd", grid=(f // tn, t // tm),
        out_shape=jax.ShapeDtypeStruct((2, t, f), BF16),
        in_specs=[pl.BlockSpec((tm, d), lambda j, i: (i, 0)), pl.BlockSpec((tn, d), lambda j, i: (j, 0)),
                  pl.BlockSpec((2, tm, tn), lambda j, i: (0, i, j))],
        out_specs=pl.BlockSpec((2, tm, tn), lambda j, i: (0, i, j)),
        compiler_params=_params(),
    )(df, wd, gu)


SIGN_BIT = 0x80000000
Q_SCALE = 1.0 / math.sqrt(HEAD_DIM)


def _softplus(z):
    neg_abs = lax.bitcast_convert_type(lax.bitcast_convert_type(z, jnp.uint32) | jnp.uint32(SIGN_BIT), F32)
    return jnp.maximum(z, 0.0) + jnp.log(1.0 + jnp.exp(neg_abs))


def _hi_lo(v):
    hi = v.astype(BF16)
    return jnp.concatenate([hi, (v - hi.astype(F32)).astype(BF16)], axis=1)


def _emit_skewed(chains, lag=1):
    for t in range(max(len(ch) for ch in chains) + lag * (len(chains) - 1)):
        for c, ch in enumerate(chains):
            if 0 <= t - lag * c < len(ch):
                ch[t - lag * c]()


def _fwd_chain(blk, qs, k_ref, v_ref, c0, kb, cols, mask, ntri, lane, tq):
    st = {}

    def scores():
        st["z"] = _dot_nt(qs, k_ref[pl.ds(c0, tq), cols])

    def soft():
        sp = _softplus(st["z"])
        if mask is not None:
            sp = jnp.where(mask, sp, 0.0)
        st["parts"] = _hi_lo(sp)
        st["cur"] = blk["cur"]
        blk["cm"] = jnp.where(lane == kb, blk["cur"], blk["cm"])
        blk["cur"] = blk["cur"] - jnp.sum(sp, axis=1, keepdims=True)

    def sums():
        st["s"] = _dot_nn(st["parts"], ntri)

    def weights():
        w = jnp.exp(st["z"] + st["s"] + st["cur"])
        if mask is not None:
            w = jnp.where(mask, w, 0.0)
        st["w"] = w.astype(BF16)

    def out():
        p = _dot_nn(st["w"], v_ref[pl.ds(c0, tq), cols])
        blk["pv"] = p if blk["pv"] is None else blk["pv"] + p

    return [scores, soft, sums, weights, out]


def _bwd_chain(blk, qs, dos, cs, k_ref, v_ref, dk_ref, dv_ref, c0, kb, cols, mask, ntri, tri_i, lane, tq):
    st = {}

    def scores():
        st["z"] = _dot_nt(qs, k_ref[pl.ds(c0, tq), cols])
        st["dw"] = _dot_nt(dos, v_ref[pl.ds(c0, tq), cols])

    def soft():
        sp = _softplus(st["z"])
        if mask is not None:
            sp = jnp.where(mask, sp, 0.0)
        st["sp"] = sp
        st["parts"] = _hi_lo(sp)
        st["cur"] = jnp.sum(jnp.where(lane == kb, cs, 0.0), axis=1, keepdims=True)

    def sums():
        st["s"] = _dot_nn(st["parts"], ntri)

    def weights():
        w = jnp.exp(st["z"] + st["s"] + st["cur"])
        if mask is not None:
            w = jnp.where(mask, w, 0.0)
        ee = w * st["dw"]
        st["w"], st["ee"], st["ec"] = w.astype(BF16), ee, blk["ec"]
        blk["ec"] = blk["ec"] + jnp.sum(ee, axis=1, keepdims=True)

    def prefix():
        st["einc"] = _dot_nn(st["ee"].astype(BF16), tri_i)

    def dz():
        v = st["ee"] - jnp.exp(st["z"] - st["sp"]) * (st["einc"] + st["ec"])
        if mask is not None:
            v = jnp.where(mask, v, 0.0)
        st["dz"] = v.astype(BF16)

    def grads():
        p = _dot_nn(st["dz"], k_ref[pl.ds(c0, tq), cols])
        blk["dq"] = p if blk["dq"] is None else blk["dq"] + p
        dk_ref[pl.ds(c0, tq), :] += _dot_tn(st["dz"], qs)
        dv_ref[pl.ds(c0, tq), :] += _dot_tn(st["w"], dos)

    return [scores, soft, sums, weights, prefix, dz, grads]


def _stack_heads(v, lane, scale=None):
    if scale is not None:
        v = v * jnp.asarray(scale, v.dtype)
    zero = jnp.zeros_like(v)
    return jnp.concatenate([jnp.where(lane < HEAD_DIM, v, zero), jnp.where(lane >= HEAD_DIM, v, zero)], axis=0)


def _diag_mask(tq):
    row = lax.broadcasted_iota(jnp.int32, (2 * tq, tq), 0)
    col = lax.broadcasted_iota(jnp.int32, (2 * tq, tq), 1)
    return col < jnp.where(row >= tq, row - tq, row)


def _attn_fwd(proj, tri_after, n_seq, seq, ag_srcs, ag_out_shapes, ag_dests):
    t = proj.shape[0]
    tq = ATT_TILE
    npp = ATT_PAIRS
    n_blk = (proj.shape[1] // 4) // (npp * LANES)
    n_ag, n_ag_out = len(ag_srcs), len(ag_out_shapes)
    n_steps = n_seq * n_blk

    def body(q_ref, k_ref, v_ref, tri_ref, *rest):
        ag_src, rest = rest[:n_ag], rest[n_ag:]
        o_ref, cs_ref = rest[:2]
        ag_out, rest = rest[2:2 + n_ag_out], rest[2 + n_ag_out:]
        oacc, cmat, carry = rest[:3]
        ag_start, ag_forward, ag_finish = _ag_phases(ag_dests, ag_src, ag_out, *rest[3:])
        step = pl.program_id(0) * n_blk + pl.program_id(1)
        pl.when(step == 0)(ag_start)
        pl.when(step == (3 * n_steps) // 4)(ag_forward)
        lane = lax.broadcasted_iota(jnp.int32, (1, LANES), 1)
        ntri = tri_ref[...]
        diag = _diag_mask(tq)

        def q_tile(qi, _):
            r0 = pl.multiple_of(qi * tq, tq)
            qs = [_stack_heads(q_ref[pl.ds(r0, tq), pp * LANES:(pp + 1) * LANES], lane, Q_SCALE)
                  for pp in range(npp)]
            carry[...] = jnp.zeros_like(carry)
            cmat[...] = jnp.zeros_like(cmat)
            oacc[...] = jnp.zeros_like(oacc)

            def run_tiles(tiles):
                blocks = [dict(cur=carry[pp], cm=cmat[pp], pv=None) for pp in range(npp)]
                chains = []
                for kb, mask in tiles:
                    c0 = pl.multiple_of(kb * tq, tq)
                    for pp in range(npp):
                        chains.append(_fwd_chain(blocks[pp], qs[pp], k_ref, v_ref, c0, kb,
                                                 slice(pp * LANES, (pp + 1) * LANES), mask, ntri, lane, tq))
                _emit_skewed(chains)
                for pp in range(npp):
                    oacc[pp] += blocks[pp]["pv"]
                    cmat[pp] = blocks[pp]["cm"]
                    carry[pp] = blocks[pp]["cur"]

            odd = qi % 2

            @pl.when(odd == 0)
            def _():
                run_tiles([(qi, diag)])

            @pl.when(odd == 1)
            def _():
                run_tiles([(qi, diag), (qi - 1, None)])

            def pair(j, _):
                kb = qi - 1 - odd - 2 * j
                run_tiles([(kb, None), (kb - 1, None)])
                return 0

            lax.fori_loop(0, qi // 2, pair, 0)
            for pp in range(npp):
                c_off = 2 * pp * LANES
                cs_ref[pl.ds(r0, tq), c_off:c_off + LANES] = cmat[pp, 0:tq, :]
                cs_ref[pl.ds(r0, tq), c_off + LANES:c_off + 2 * LANES] = cmat[pp, tq:2 * tq, :]
                o_ref[pl.ds(r0, tq), pp * LANES:(pp + 1) * LANES] = jnp.where(
                    lane < HEAD_DIM, oacc[pp, 0:tq, :], oacc[pp, tq:2 * tq, :]).astype(BF16)
            return 0

        lax.fori_loop(0, seq // tq, q_tile, 0)
        pl.when(step == n_steps - 1)(ag_finish)

    wid = npp * LANES
    blk = lambda off: pl.BlockSpec((seq, wid), lambda b, p: (b, off + p))
    any_spec = pl.BlockSpec(memory_space=pl.ANY)
    return pl.pallas_call(
        body, name="attn_fwd", grid=(n_seq, n_blk),
        out_shape=(jax.ShapeDtypeStruct((2, t, n_blk * wid), BF16),
                   jax.ShapeDtypeStruct((t, n_blk * 2 * wid), F32), *ag_out_shapes),
        in_specs=[blk(0), blk(n_blk), blk(2 * n_blk), pl.BlockSpec((2 * tq, tq), lambda b, p: (0, 0))]
        + [any_spec] * n_ag,
        out_specs=(pl.BlockSpec((None, seq, wid), lambda b, p: (0, b, p)),
                   pl.BlockSpec((seq, 2 * wid), lambda b, p: (b, p)), *([any_spec] * n_ag_out)),
        scratch_shapes=[pltpu.VMEM((npp, 2 * tq, LANES), F32), pltpu.VMEM((npp, 2 * tq, LANES), F32),
                        pltpu.VMEM((npp, 2 * tq, 1), F32)] + _ag_scratch(n_ag),
        compiler_params=_params(),
    )(proj, proj, proj, tri_after, *ag_srcs)


def _attn_bwd(proj, dcat, cstats, tri_after, tri_incl, n_seq, seq, rs_sends):
    t = proj.shape[0]
    tq = ATT_TILE
    npp = ATT_PAIRS
    width = proj.shape[1] // 4
    n_blk = width // (npp * LANES)
    n_rs = len(rs_sends)
    rs_shapes = [r.shape for r in rs_sends]
    n_steps = n_seq * n_blk

    def body(q_ref, k_ref, v_ref, do_ref, cs_ref, tria_ref, trii_ref, *rest):
        rs_src, rest = rest[:n_rs], rest[n_rs:]
        out_ref = rest[0]
        rs_dst, rest = rest[1:1 + n_rs], rest[1 + n_rs:]
        dq_acc, dk_acc, dv_acc, ecarry = rest[:4]
        rs_start, rs_finish = _rs_phases(rs_shapes, rs_src, rs_dst, *rest[4:])
        step = pl.program_id(0) * n_blk + pl.program_id(1)
        pl.when(step == 0)(rs_start)
        lane = lax.broadcasted_iota(jnp.int32, (1, LANES), 1)
        ntri = tria_ref[...]
        tri_i = trii_ref[...]
        diag = _diag_mask(tq)
        dk_acc[...] = jnp.zeros_like(dk_acc)
        dv_acc[...] = jnp.zeros_like(dv_acc)

        def q_tile(qi, _):
            r0 = pl.multiple_of(qi * tq, tq)
            qs, dos, cs = [], [], []
            for pp in range(npp):
                cols = slice(pp * LANES, (pp + 1) * LANES)
                qs.append(_stack_heads(q_ref[pl.ds(r0, tq), cols], lane, Q_SCALE))
                dos.append(_stack_heads(do_ref[pl.ds(r0, tq), cols], lane))
                c_off = 2 * pp * LANES
                cs.append(jnp.concatenate([cs_ref[pl.ds(r0, tq), c_off:c_off + LANES],
                                           cs_ref[pl.ds(r0, tq), c_off + LANES:c_off + 2 * LANES]], axis=0))
            ecarry[...] = jnp.zeros_like(ecarry)
            dq_acc[...] = jnp.zeros_like(dq_acc)

            def run_tiles(tiles):
                blocks = [dict(ec=ecarry[pp], dq=None) for pp in range(npp)]
                chains = []
                for kb, mask in tiles:
                    c0 = pl.multiple_of(kb * tq, tq)
                    for pp in range(npp):
                        chains.append(_bwd_chain(
                            blocks[pp], qs[pp], dos[pp], cs[pp], k_ref, v_ref, dk_acc.at[pp], dv_acc.at[pp],
                            c0, kb, slice(pp * LANES, (pp + 1) * LANES), mask, ntri, tri_i, lane, tq))
                _emit_skewed(chains)
                for pp in range(npp):
                    dq_acc[pp] += blocks[pp]["dq"]
                    ecarry[pp] = blocks[pp]["ec"]

            def pair(j, _):
                run_tiles([(2 * j, None), (2 * j + 1, None)])
                return 0

            lax.fori_loop(0, qi // 2, pair, 0)
            odd = qi % 2

            @pl.when(odd == 0)
            def _():
                run_tiles([(qi, diag)])

            @pl.when(odd == 1)
            def _():
                run_tiles([(qi - 1, None), (qi, diag)])

            for pp in range(npp):
                dq = jnp.where(lane < HEAD_DIM, dq_acc[pp, 0:tq, :], dq_acc[pp, tq:2 * tq, :])
                out_ref[0, pl.ds(r0, tq), pp * LANES:(pp + 1) * LANES] = (dq * Q_SCALE).astype(BF16)
            return 0

        lax.fori_loop(0, seq // tq, q_tile, 0)
        for pp in range(npp):
            cols = slice(pp * LANES, (pp + 1) * LANES)
            out_ref[1, :, cols] = dk_acc[pp].astype(BF16)
            out_ref[2, :, cols] = dv_acc[pp].astype(BF16)
        pl.when(step == n_steps - 1)(rs_finish)

    wid = npp * LANES
    blk = lambda off: pl.BlockSpec((seq, wid), lambda b, p: (b, off + p))
    tri_spec = pl.BlockSpec((2 * tq, tq), lambda b, p: (0, 0))
    any_spec = pl.BlockSpec(memory_space=pl.ANY)
    return pl.pallas_call(
        body, name="attn_bwd", grid=(n_seq, n_blk),
        out_shape=(jax.ShapeDtypeStruct((4, t, width), BF16), *_rs_out(rs_sends)),
        in_specs=[blk(0), blk(n_blk), blk(2 * n_blk), pl.BlockSpec((seq, wid), lambda b, p: (b, p)),
                  pl.BlockSpec((seq, 2 * wid), lambda b, p: (b, p)), tri_spec,
                  pl.BlockSpec((tq, tq), lambda b, p: (0, 0))] + [any_spec] * n_rs,
        out_specs=(pl.BlockSpec((3, seq, wid), lambda b, p: (0, b, p)), *([any_spec] * n_rs)),
        scratch_shapes=[pltpu.VMEM((npp, 2 * tq, LANES), F32), pltpu.VMEM((npp, seq, LANES), F32),
                        pltpu.VMEM((npp, seq, LANES), F32), pltpu.VMEM((npp, 2 * tq, 1), F32)]
        + _rs_scratch(rs_sends),
        compiler_params=_params(),
    )(proj, proj, proj, dcat, cstats, tri_after, tri_incl, *rs_sends)


def _window_terms(g, rows):
    win = jnp.where(g == 0, POOL_WINDOWS[0], jnp.where(g == 1, POOL_WINDOWS[1],
                    jnp.where(g == 2, POOL_WINDOWS[2], POOL_WINDOWS[3])))
    cnt = jnp.minimum(rows + 1, win).astype(F32)
    return win, cnt


def _window_sum(v, g, rows, forward):
    s_len = v.shape[0]
    sums = []
    s = v
    for step in range(len(POOL_WINDOWS)):
        sh = 1 << step
        if forward:
            shifted = jnp.where(rows < s_len - sh, pltpu.roll(s, s_len - sh, axis=0), 0.0)
        else:
            shifted = jnp.where(rows >= sh, pltpu.roll(s, sh, axis=0), 0.0)
        s = s + shifted
        sums.append(s)
    return jnp.where(g == 0, sums[0], jnp.where(g == 1, sums[1], jnp.where(g == 2, sums[2], sums[3])))


def _pooled(u, g, rows):
    _, cnt = _window_terms(g, rows)
    return _window_sum(u, g, rows, forward=False) / cnt - u


def _pool_fwd(proj, w_pool, pool_scale, cat, n_seq, seq):
    n_grp = len(POOL_WINDOWS)
    u_off = 3 * (proj.shape[1] // 4) // LANES

    def body(u_ref, w_ref, s_ref, alias_ref, o_ref):
        del alias_ref
        g = pl.program_id(1)
        rows = lax.broadcasted_iota(jnp.int32, (seq, 1), 0)
        pooled = _pooled(u_ref[...].astype(F32), g, rows)
        y = _dot_nn(pooled.astype(BF16), w_ref[...].astype(BF16))
        o_ref[...] = (y * s_ref[...]).astype(BF16)

    return pl.pallas_call(
        body, name="pool_fwd", grid=(n_seq, n_grp),
        out_shape=jax.ShapeDtypeStruct(cat.shape, BF16),
        in_specs=[pl.BlockSpec((seq, LANES), lambda b, g: (b, u_off + g)),
                  pl.BlockSpec((None, POOL_GROUP_DIM, POOL_GROUP_DIM), lambda b, g: (g, 0, 0)),
                  pl.BlockSpec((1, POOL_GROUP_DIM), lambda b, g: (0, g)),
                  pl.BlockSpec(memory_space=pl.ANY)],
        out_specs=pl.BlockSpec((None, seq, LANES), lambda b, g: (1, b, g)),
        input_output_aliases={3: 0},
        compiler_params=_params(),
    )(proj, w_pool, pool_scale, cat)


def _pool_bwd(proj, dcat, w_pool, pool_scale, dqkv, n_seq, seq):
    n_grp = len(POOL_WINDOWS)
    width = proj.shape[1] // 4
    u_off = 3 * width // LANES
    dp_off = width // LANES

    def body(u_ref, dp_ref, w_ref, s_ref, alias_ref, du_ref, gw_ref, gs_ref):
        del alias_ref
        g = pl.program_id(0)
        b = pl.program_id(1)
        rows = lax.broadcasted_iota(jnp.int32, (seq, 1), 0)
        pooled = _pooled(u_ref[...].astype(F32), g, rows)
        pb = pooled.astype(BF16)
        wb = w_ref[...].astype(BF16)
        z = _dot_nn(pb, wb)
        dp = dp_ref[...].astype(F32)
        _acc(gs_ref, _colsum(dp * z), b == 0)
        dys = (dp * s_ref[...]).astype(BF16)
        _acc(gw_ref, _dot_tn(pb, dys), b == 0)
        dpooled = _dot_nt(dys, wb)
        _, cnt = _window_terms(g, rows)
        du = _window_sum(dpooled / cnt, g, rows, forward=True) - dpooled
        du_ref[...] = du.astype(BF16)

    t = proj.shape[0]
    return pl.pallas_call(
        body, name="pool_bwd", grid=(n_grp, n_seq),
        out_shape=(jax.ShapeDtypeStruct(dqkv.shape, BF16),
                   jax.ShapeDtypeStruct((n_grp, POOL_GROUP_DIM, POOL_GROUP_DIM), F32),
                   jax.ShapeDtypeStruct((1, n_grp * POOL_GROUP_DIM), F32)),
        in_specs=[pl.BlockSpec((seq, LANES), lambda g, b: (b, u_off + g)),
                  pl.BlockSpec((seq, LANES), lambda g, b: (b, dp_off + g)),
                  pl.BlockSpec((None, POOL_GROUP_DIM, POOL_GROUP_DIM), lambda g, b: (g, 0, 0)),
                  pl.BlockSpec((1, POOL_GROUP_DIM), lambda g, b: (0, g)),
                  pl.BlockSpec(memory_space=pl.ANY)],
        out_specs=(pl.BlockSpec((None, seq, LANES), lambda g, b: (3, b, g)),
                   pl.BlockSpec((None, POOL_GROUP_DIM, POOL_GROUP_DIM), lambda g, b: (g, 0, 0)),
                   pl.BlockSpec((1, POOL_GROUP_DIM), lambda g, b: (0, g))),
        input_output_aliases={4: 0},
        compiler_params=_params(),
    )(proj, dcat, w_pool, pool_scale, dqkv)


def _cond_fwd(c_all, w_cond, b_cols):
    n, _ = c_all.shape
    cols = w_cond.shape[1]

    def body(c_ref, w_ref, b_ref, o_ref):
        cv = c_ref[...]
        a = cv * jax.nn.sigmoid(cv)
        o_ref[...] = jnp.dot(a, w_ref[...], preferred_element_type=F32,
                             precision=lax.Precision.HIGHEST) + b_ref[...]

    return pl.pallas_call(
        body, name="cond_fwd", out_shape=jax.ShapeDtypeStruct((n, cols), F32),
        compiler_params=_params(),
    )(c_all, w_cond, b_cols)


def _cond_bwd(c_all, dmod_all, dmod_cols):
    n, d = c_all.shape
    cols = dmod_cols.shape[1]

    def body(c_ref, dm_ref, dmc_ref, gw_ref, gb_ref):
        cv = c_ref[...]
        a = cv * jax.nn.sigmoid(cv)
        gw_ref[...] = lax.dot_general(a, dmc_ref[...], (((0,), (0,)), ((), ())),
                                      preferred_element_type=F32, precision=lax.Precision.HIGHEST)
        gb_ref[...] = _colsum(dm_ref[...])

    return pl.pallas_call(
        body, name="cond_bwd",
        out_shape=(jax.ShapeDtypeStruct((d, cols), F32), jax.ShapeDtypeStruct((1, dmod_all.shape[1]), F32)),
        compiler_params=_params(),
    )(c_all, dmod_all, dmod_cols)


def _adamw_math(w, g, m, v):
    m = ADAM_B1 * m + (1.0 - ADAM_B1) * g
    v = ADAM_B2 * v + (1.0 - ADAM_B2) * (g * g)
    m_hat = m / (1.0 - ADAM_B1 ** ADAM_STEP)
    v_hat = v / (1.0 - ADAM_B2 ** ADAM_STEP)
    delta = -ADAM_LR * (m_hat / (jnp.sqrt(v_hat) + ADAM_EPS) + ADAM_WD * w)
    return delta, m, v


def _adamw(w, g, m, v, rows, name):
    r, cdim = w.shape

    def body(w_ref, g_ref, m_ref, v_ref, d_ref, nm_ref, nv_ref):
        d_ref[...], nm_ref[...], nv_ref[...] = _adamw_math(w_ref[...], g_ref[...], m_ref[...], v_ref[...])

    spec = pl.BlockSpec((rows, cdim), lambda i: (i, 0))
    sds = jax.ShapeDtypeStruct((r, cdim), F32)
    return pl.pallas_call(
        body, name=name, grid=(r // rows,), out_shape=(sds, sds, sds),
        in_specs=[spec] * 4, out_specs=(spec, spec, spec), compiler_params=_params(),
    )(w, g, m, v)


def _adamw_small(ws, gparts, ms, vs, name):
    n = len(ws)

    def body(*refs):
        w_r, g_r, m_r, v_r = refs[:n], refs[n:2 * n], refs[2 * n:3 * n], refs[3 * n:4 * n]
        outs = refs[4 * n:]
        for i in range(n):
            g = g_r[i][0]
            for dev in range(1, g_r[i].shape[0]):
                g = g + g_r[i][dev]
            delta, m, v = _adamw_math(w_r[i][...], g, m_r[i][...], v_r[i][...])
            outs[i][...] = g
            outs[n + i][...] = delta
            outs[2 * n + i][...] = m
            outs[3 * n + i][...] = v

    sds = [jax.ShapeDtypeStruct(w.shape, F32) for w in ws]
    return pl.pallas_call(
        body, name=name, out_shape=tuple(sds * 4), compiler_params=_params(),
    )(*ws, *gparts, *ms, *vs)


def kernel(x, c, w_cond, b_cond, g_mix_pre, g_mix_post, w_in, w_pool, pool_scale, w_out, g_ffn_pre, g_ffn_post, w_gate, w_up, w_down, loss_target, m_w_cond, m_b_cond, m_g_mix_pre, m_g_mix_post, m_w_in, m_w_pool, m_pool_scale, m_w_out, m_g_ffn_pre, m_g_ffn_post, m_w_gate, m_w_up, m_w_down, v_w_cond, v_b_cond, v_g_mix_pre, v_g_mix_post, v_w_in, v_w_pool, v_pool_scale, v_w_out, v_g_ffn_pre, v_g_ffn_post, v_w_gate, v_w_up, v_w_down):
    n_seq, seq, d = x.shape
    t = n_seq * seq
    xi, yi, ci = _mesh_pos()
    me = 4 * xi + 2 * yi + ci
    x2 = x.reshape(t, d)
    tgt2 = loss_target.reshape(t, d)
    in_rows = w_in.shape[2]
    out_rows = w_out.shape[1]
    ff_rows = w_gate.shape[2]
    ff = N_DEV * ff_rows
    cond_cols = w_cond.shape[2]

    win_t = w_in[0].T.astype(BF16)
    wout_s = w_out[0].astype(BF16)
    wg_t = w_gate[0].T.astype(BF16)
    wu_t = w_up[0].T.astype(BF16)
    wd_s = w_down[0].astype(BF16)
    (c_all,) = _all_gather([c], [jax.ShapeDtypeStruct((N_DEV, n_seq, d), F32)], [(0, ())], "ag_c")
    c_all = c_all.reshape(N_DEV * n_seq, d)

    b_cols = lax.dynamic_slice_in_dim(b_cond, me * cond_cols, cond_cols, axis=1)
    mod_cols = _cond_fwd(c_all, w_cond[0], b_cols)
    (mod_g,) = _all_gather([mod_cols], [jax.ShapeDtypeStruct((N_DEV,) + mod_cols.shape, F32)], [(0, ())], "ag_mod")
    mod_mine = lax.dynamic_slice_in_dim(mod_g, me * n_seq, n_seq, axis=1)
    mod = jnp.transpose(mod_mine, (1, 0, 2)).reshape(n_seq, N_MOD, d)

    h1, win_g = _pre_mix(x2, g_mix_pre, mod, seq, [win_t],
                         [jax.ShapeDtypeStruct((N_DEV, in_rows, d), BF16)], [(0, ())])
    win_full = win_g.reshape(N_DEV * in_rows, d)
    proj = _matmul(h1, win_full, "nt", BF16, 1024, 512, d, "proj")
    tq = ATT_TILE
    ids = jnp.arange(tq)
    tri_after = jnp.tile(-(ids[:, None] >= ids[None, :]).astype(BF16), (2, 1))
    tri_incl = (ids[:, None] <= ids[None, :]).astype(BF16)
    attn, cstats, wout_g, wgu_g, wd_g = _attn_fwd(
        proj, tri_after, n_seq, seq, [wout_s, wg_t, wu_t, wd_s],
        [jax.ShapeDtypeStruct((N_DEV, out_rows, d), BF16), jax.ShapeDtypeStruct((2, N_DEV, ff_rows, d), BF16),
         jax.ShapeDtypeStruct((N_DEV, ff_rows, d), BF16)],
        [(0, ()), (1, (0,)), (1, (1,)), (2, ())])
    wout_full = wout_g.reshape(N_DEV * out_rows, d)
    wgu_full = wgu_g.reshape(2, ff, d)
    wd_full = wd_g.reshape(ff, d)
    cat = _pool_fwd(proj, w_pool[0], pool_scale, attn, n_seq, seq)
    tok_f32, tok_bf16 = jax.ShapeDtypeStruct((t, d), F32), jax.ShapeDtypeStruct((t, d), BF16)
    seq_sds, vec_sds = jax.ShapeDtypeStruct((n_seq, 1, d), F32), jax.ShapeDtypeStruct((1, d), F32)
    mix, x1, h2 = _matmul_rows(
        cat, wout_full.reshape(2, d // 2, d), ROW_TILE, seq, "mix_mid", _mid_epilogue,
        [x2, g_mix_post, g_ffn_pre, mod], ["tok", "vec", "vec", "mod"],
        [tok_f32, tok_f32, tok_bf16], ["tok", "tok", "tok"])
    gu, act = _ffn_up(h2, wgu_full, 512, ff // 2)
    loss_sum, dy, df, dgate_f, gg_ffn_post = _matmul_rows(
        act, wd_full, ROW_TILE, seq, "ffn_down_post", _post_epilogue,
        [x1, tgt2, g_ffn_post, mod], ["tok", "tok", "vec", "mod"],
        [jax.ShapeDtypeStruct((1, LANES), F32), tok_f32, tok_bf16, seq_sds, vec_sds],
        ["loss", "tok", "tok", "seq", "vec"])

    dgu = _ffn_act_bwd(df, wd_full, gu, 512, ff // 2)
    gwd, gwd_b = _matmul(act, df, "tn", F32, ff // 2, d // 2, t, "grad_w_down", bf16_copy=True)
    gwgu, gwgu_b = _matmul(dgu, h2, "tn", F32, ff // 2, d // 2, t, "grad_w_gate_up", bf16_copy=True)
    dx1, dmix, dshift_f, dscale_f, dgate_m, gg_ffn_pre, gg_mix_post = _matmul_rows(
        dgu, wgu_full, ROW_TILE, seq, "dh2_bwd_mid", _bwd_mid_epilogue,
        [dy, x1, mix, g_ffn_pre, g_mix_post, mod], ["tok", "tok", "tok", "vec", "vec", "mod"],
        [tok_f32, tok_bf16, seq_sds, seq_sds, seq_sds, vec_sds, vec_sds],
        ["tok", "tok", "seq", "seq", "seq", "vec", "vec"])
    dcat = _matmul(dmix, wout_full, "nt", BF16, 1024, 512, d, "dcat")
    gwout, gwout_b = _matmul(cat, dmix, "tn", F32, d // 2, d // 2, t, "grad_w_out", bf16_copy=True)
    dqkv, rv_wgu, rv_wd, rv_wout = _attn_bwd(
        proj, dcat, cstats, tri_after, tri_incl, n_seq, seq,
        [gwgu_b.reshape(2, N_DEV, ff_rows, d), gwd_b.reshape(1, N_DEV, ff_rows, d),
         gwout_b.reshape(1, N_DEV, out_rows, d)])
    dproj, gw_pool, gs_pool = _pool_bwd(proj, dcat, w_pool[0], pool_scale, dqkv, n_seq, seq)
    pad_d = lambda v: jnp.pad(v, ((0, 0), (0, d - v.shape[1])))
    n_gw = gw_pool.size // d
    early = jnp.concatenate(
        [gg_mix_post, gg_ffn_pre, gg_ffn_post, pad_d(gs_pool), pad_d(loss_sum), jnp.zeros((3, d), F32),
         gw_pool.reshape(n_gw, d),
         jnp.concatenate([dgate_m, dshift_f, dscale_f, dgate_f], axis=1).reshape(n_seq * 4, d)], axis=0)
    gwin, gwin_b, early_g = _matmul(
        dproj, h1, "tn", F32, d // 2, d // 2, t, "grad_w_in", bf16_copy=True,
        ag=([early], [jax.ShapeDtypeStruct((N_DEV,) + early.shape, F32)], [(0, ())]))
    grad_x, dshift_m, dscale_m, gg_mix_pre, rv_win = _matmul_rows(
        dproj, win_full.reshape(4, d // 2, d), ROW_TILE, seq, "dh1_bwd_pre", _bwd_pre_epilogue,
        [dx1, x2, g_mix_pre, mod], ["tok", "tok", "vec", "mod"],
        [tok_f32, seq_sds, seq_sds, vec_sds], ["tok", "seq", "seq", "vec"],
        rs_sends=[gwin_b.reshape(1, N_DEV, in_rows, d)])

    r_wgu = _rs_final(gwgu.reshape(2, N_DEV, ff_rows, d), rv_wgu, "rs_final_gate_up")
    r_wd = _rs_final(gwd.reshape(1, N_DEV, ff_rows, d), rv_wd, "rs_final_down")
    r_wout = _rs_final(gwout.reshape(1, N_DEV, out_rows, d), rv_wout, "rs_final_out")
    r_win = _rs_final(gwin.reshape(1, N_DEV, in_rows, d), rv_win, "rs_final_in")
    grad_w_in = r_win[0].T
    grad_w_out = r_wout[0]
    grad_w_down = r_wd[0]

    late = jnp.concatenate([gg_mix_pre, dshift_m.reshape(n_seq, d), dscale_m.reshape(n_seq, d),
                            jnp.zeros((8 - 1 - 2 * n_seq, d), F32)], axis=0)
    (late_g,) = _all_gather([late], [jax.ShapeDtypeStruct((N_DEV,) + late.shape, F32)], [(0, ())], "ag_late")
    loss = jnp.sum(early_g[:, 4, 0]) * (0.5 / d)
    dmod_all = jnp.concatenate(
        [late_g[:, 1:1 + n_seq, None, :], late_g[:, 1 + n_seq:1 + 2 * n_seq, None, :],
         early_g[:, 8 + n_gw:, :].reshape(N_DEV, n_seq, 4, d)], axis=2).reshape(N_DEV * n_seq, N_MOD * d)
    dmod_cols = lax.dynamic_slice_in_dim(dmod_all, me * cond_cols, cond_cols, axis=1)
    grad_w_cond, grad_b_cond = _cond_bwd(c_all, dmod_all, dmod_cols)

    small_ws = [g_mix_pre, g_mix_post, g_ffn_pre, g_ffn_post, pool_scale, w_pool.reshape(-1, POOL_GROUP_DIM)]
    small_ms = [m_g_mix_pre, m_g_mix_post, m_g_ffn_pre, m_g_ffn_post, m_pool_scale, m_w_pool.reshape(-1, POOL_GROUP_DIM)]
    small_vs = [v_g_mix_pre, v_g_mix_post, v_g_ffn_pre, v_g_ffn_post, v_pool_scale, v_w_pool.reshape(-1, POOL_GROUP_DIM)]
    small_gparts = [late_g[:, 0:1, :], early_g[:, 0:1, :], early_g[:, 1:2, :], early_g[:, 2:3, :],
                    early_g[:, 3:4, :pool_scale.shape[1]],
                    early_g[:, 8:8 + n_gw, :].reshape(N_DEV, -1, POOL_GROUP_DIM)]
    so = _adamw_small(small_ws, small_gparts, small_ms, small_vs, "adamw_small")
    ns = len(small_ws)
    sg, sdl, sm, sv = so[:ns], so[ns:2 * ns], so[2 * ns:3 * ns], so[3 * ns:]
    pool_shape = w_pool.shape
    fix = lambda lst: [lst[0], lst[1], lst[2], lst[3], lst[4], lst[5].reshape(pool_shape)]
    sg, sdl, sm, sv = fix(sg), fix(sdl), fix(sm), fix(sv)

    def big(w, g, m, v, rows, name):
        dl, nm, nv = _adamw(w[0], g, m[0], v[0], rows, name)
        return g[None], dl[None], nm[None], nv[None]

    o_cond = big(w_cond, grad_w_cond, m_w_cond, v_w_cond, 256, "adamw_w_cond")
    o_bcond = _adamw(b_cond, grad_b_cond, m_b_cond, v_b_cond, 1, "adamw_b_cond")
    o_bcond = (grad_b_cond,) + tuple(o_bcond)
    o_in = big(w_in, grad_w_in, m_w_in, v_w_in, 256, "adamw_w_in")
    o_out = big(w_out, grad_w_out, m_w_out, v_w_out, out_rows, "adamw_w_out")
    def big_t(w, g_t, m, v, name):
        outs = _adamw(w[0].T, g_t, m[0].T, v[0].T, g_t.shape[0], name)
        return tuple(o.T[None] for o in (g_t,) + tuple(outs))

    o_gate = big_t(w_gate, r_wgu[0], m_w_gate, v_w_gate, "adamw_w_gate")
    o_up = big_t(w_up, r_wgu[1], m_w_up, v_w_up, "adamw_w_up")
    o_down = big(w_down, grad_w_down, m_w_down, v_w_down, ff_rows, "adamw_w_down")

    def pick(k):
        small_k = [sg, sdl, sm, sv][k]
        return [o_cond[k], o_bcond[k], small_k[0], small_k[1], o_in[k], small_k[5], small_k[4], o_out[k],
                small_k[2], small_k[3], o_gate[k], o_up[k], o_down[k]]

    return (loss, grad_x.reshape(n_seq, seq, d), *pick(0), *pick(1), *pick(2), *pick(3))
```

```python
import functools
import math

import jax
import jax.numpy as jnp
from jax import lax
from jax.experimental import pallas as pl
from jax.experimental.pallas import tpu as pltpu

F32 = jnp.float32
BF16 = jnp.bfloat16
MESH = pl.DeviceIdType.MESH

N_DEV = 8
HEAD_DIM = 64
LANES = 128
POOL_WINDOWS = (2, 4, 8, 16)
POOL_GROUP_DIM = 128
N_MOD = 6
EPS = 1e-6
ATT_TILE = 256
ATT_PAIRS = 2
VMEM_LIMIT = 56 * 1024 * 1024

ADAM_LR = 0.001
ADAM_B1 = 0.9
ADAM_B2 = 0.999
ADAM_EPS = 1e-08
ADAM_WD = 0.01
ADAM_STEP = 10


def _params(**kw):
    return pltpu.CompilerParams(vmem_limit_bytes=VMEM_LIMIT, **kw)


def _dot_nn(a, b):
    return jnp.dot(a, b, preferred_element_type=F32)


def _dot_nt(a, b):
    return lax.dot_general(a, b, (((1,), (1,)), ((), ())), preferred_element_type=F32)


def _dot_tn(a, b):
    return lax.dot_general(a, b, (((0,), (0,)), ((), ())), preferred_element_type=F32)


def _mesh_pos():
    return lax.axis_index("x"), lax.axis_index("y"), lax.axis_index("c")


def _ag_phases(dests, src, outs, send_sems, recv_sems, local_sems):
    n = len(src)
    x, y, c = _mesh_pos()
    me, sibling = (x, y, c), (x, y, 1 - c)
    chips = [(1 - x, y), (x, 1 - y), (1 - x, 1 - y)]

    def slot(i, dev):
        oi, prefix = dests[i]
        px, py, pc = dev
        return outs[oi].at[prefix + (4 * px + 2 * py + pc,)]

    def copy(i, k, block, to, from_src=False):
        return pltpu.make_async_remote_copy(
            src_ref=src[i] if from_src else slot(i, block), dst_ref=slot(i, block),
            send_sem=send_sems.at[i, k], recv_sem=recv_sems.at[i, k],
            device_id=to, device_id_type=MESH)

    def mine(i):
        return pltpu.make_async_copy(src[i], slot(i, me), local_sems.at[i])

    def first(i):
        return [copy(i, 0, me, sibling, from_src=True)] + [
            copy(i, 1 + j, me, (*chip, c), from_src=True) for j, chip in enumerate(chips)]

    def passed(i, j):
        return copy(i, 4 + j, (*chips[j], c), sibling)

    def start():
        for i in range(n):
            mine(i).start()
        for i in range(n):
            for cp in first(i):
                cp.start()

    def forward():
        for j, chip in enumerate(chips):
            for i in range(n):
                copy(i, 1 + j, (*chip, c), me).wait_recv()
                passed(i, j).start()

    def finish():
        for i in range(n):
            copy(i, 0, sibling, me).wait_recv()
            for j, chip in enumerate(chips):
                copy(i, 4 + j, (*chip, 1 - c), me).wait_recv()
        for i in range(n):
            for cp in first(i) + [passed(i, j) for j in range(3)]:
                cp.wait_send()
            mine(i).wait()

    return start, forward, finish


def _ag_scratch(n):
    return [pltpu.SemaphoreType.DMA((n, 7)), pltpu.SemaphoreType.DMA((n, 7)), pltpu.SemaphoreType.DMA((n,))]


def _all_gather(srcs, out_shapes, dests, name):
    n = len(srcs)

    def body(*refs):
        src = refs[:n]
        outs = refs[n:n + len(out_shapes)]
        start, forward, finish = _ag_phases(dests, src, outs, *refs[n + len(out_shapes):])
        start()
        forward()
        finish()

    any_spec = pl.BlockSpec(memory_space=pl.ANY)
    return pl.pallas_call(
        body, name=name,
        out_shape=tuple(out_shapes),
        in_specs=[any_spec] * n,
        out_specs=tuple([any_spec] * len(out_shapes)),
        scratch_shapes=_ag_scratch(n),
    )(*srcs)


def _rs_phases(shapes, src, dst, send_sems, recv_sems):
    x, y, c = _mesh_pos()

    def copies():
        out = []
        n = 0
        for i, shp in enumerate(shapes):
            for m in range(shp[0]):
                for k in range(1, N_DEV):
                    px, py, pc = x ^ (k >> 2), y ^ ((k >> 1) & 1), c ^ (k & 1)
                    out.append(pltpu.make_async_remote_copy(
                        src_ref=src[i].at[m, 4 * px + 2 * py + pc], dst_ref=dst[i].at[m, k - 1],
                        send_sem=send_sems.at[n], recv_sem=recv_sems.at[n],
                        device_id=(px, py, pc), device_id_type=MESH))
                    n += 1
        return out

    def start():
        for cp in copies():
            cp.start()

    def finish():
        for cp in copies():
            cp.wait_send()
        for cp in copies():
            cp.wait_recv()

    return start, finish


def _rs_out(sends):
    return [jax.ShapeDtypeStruct((s.shape[0], N_DEV - 1) + s.shape[2:], s.dtype) for s in sends]


def _rs_scratch(sends):
    total = sum((N_DEV - 1) * s.shape[0] for s in sends)
    return [pltpu.SemaphoreType.DMA((total,)), pltpu.SemaphoreType.DMA((total,))]


def _rs_final(mine, recv, name):
    m_n, _, r, cdim = mine.shape
    x, y, c = _mesh_pos()
    me = jnp.reshape(4 * x + 2 * y + c, (1,)).astype(jnp.int32)

    def body(me_ref, p_ref, r_ref, o_ref):
        del me_ref
        s = p_ref[...]
        for k in range(N_DEV - 1):
            s = s + r_ref[k].astype(F32)
        o_ref[...] = s

    return pl.pallas_call(
        body, name=name, out_shape=jax.ShapeDtypeStruct((m_n, r, cdim), F32),
        grid_spec=pltpu.PrefetchScalarGridSpec(
            num_scalar_prefetch=1, grid=(m_n,),
            in_specs=[pl.BlockSpec((None, None, r, cdim), lambda m, s: (m, s[0], 0, 0)),
                      pl.BlockSpec((None, N_DEV - 1, r, cdim), lambda m, s: (m, 0, 0, 0))],
            out_specs=pl.BlockSpec((None, r, cdim), lambda m, s: (m, 0, 0))),
        compiler_params=_params(),
    )(me, mine, recv)


def _matmul(a, b, mode, out_dtype, tm, tn, tk, name, bf16_copy=False, rs_sends=(), ag=None):
    ga = a.shape[0] if a.ndim == 3 else None
    gb = b.shape[0] if b.ndim == 3 else None
    a2, b2 = a.shape[-2:], b.shape[-2:]
    if mode == "nn":
        (m, k), n = a2, b2[1]
    elif mode == "nt":
        (m, k), n = a2, b2[0]
    else:
        (k, m), n = a2, b2[1]
    assert m % tm == 0 and n % tn == 0 and k % tk == 0, (name, m, n, k)
    nk = k // tk
    g_n = ga or 1
    batch_out = mode == "tn" and ga is not None
    n_red = nk if batch_out else nk * g_n
    dot = {"nn": _dot_nn, "nt": _dot_nt, "tn": _dot_tn}[mode]
    acc_in_out = out_dtype == F32

    n_rs = len(rs_sends)
    rs_shapes = [r.shape for r in rs_sends]
    ag_srcs, ag_out_shapes, ag_dests = ag if ag is not None else ((), (), ())
    n_ag, n_ag_out = len(ag_srcs), len(ag_out_shapes)
    n_out = 2 if bf16_copy else 1
    assert not bf16_copy or acc_in_out
    assert not (n_rs and n_ag)

    def body(a_ref, b_ref, *rest):
        rs_src, rest = rest[:n_rs], rest[n_rs:]
        ag_src, rest = rest[:n_ag], rest[n_ag:]
        o_ref = rest[0]
        copy_ref = rest[1] if bf16_copy else None
        rs_dst, rest = rest[n_out:n_out + n_rs], rest[n_out + n_rs:]
        ag_out, scratch = rest[:n_ag_out], rest[n_ag_out:]
        first = functools.reduce(jnp.logical_and, [pl.program_id(ax) == 0 for ax in range(4)])
        last = functools.reduce(jnp.logical_and, [pl.program_id(ax) == grid[ax] - 1 for ax in range(4)])
        if n_rs:
            rs_start, rs_finish = _rs_phases(rs_shapes, rs_src, rs_dst, *scratch[-2:])
            pl.when(first)(rs_start)
        if n_ag:
            ag_start, ag_forward, ag_finish = _ag_phases(ag_dests, ag_src, ag_out, *scratch[-3:])
            pl.when(first)(ag_start)
        p = dot(a_ref[...], b_ref[...])
        kk = pl.program_id(3) if batch_out else pl.program_id(2) * nk + pl.program_id(3)
        if n_red == 1:
            o_ref[...] = p.astype(out_dtype)
            if bf16_copy:
                copy_ref[...] = p.astype(BF16)
        else:
            acc = o_ref if acc_in_out else scratch[0]

            @pl.when(kk == 0)
            def _():
                acc[...] = p

            @pl.when(kk > 0)
            def _():
                acc[...] += p

            @pl.when(kk == n_red - 1)
            def _():
                if not acc_in_out:
                    o_ref[...] = acc[...].astype(out_dtype)
                if bf16_copy:
                    copy_ref[...] = acc[...].astype(BF16)

        if n_rs:
            pl.when(last)(rs_finish)
        if n_ag:
            @pl.when(last)
            def _():
                ag_forward()
                ag_finish()

    def order(ids):
        return ids if batch_out else (ids[2], ids[0], ids[1], ids[3])

    def a_idx(*ids):
        g, i, j, kq = order(ids)
        blk = {"nn": (i, kq), "nt": (i, kq), "tn": (kq, i)}[mode]
        return (g,) + blk if ga is not None else blk

    def b_idx(*ids):
        g, i, j, kq = order(ids)
        blk = {"nn": (kq, j), "nt": (j, kq), "tn": (kq, j)}[mode]
        return (g,) + blk if gb is not None else blk

    def o_idx(*ids):
        g, i, j, kq = order(ids)
        return (g, i, j) if batch_out else (i, j)

    a_blk = {"nn": (tm, tk), "nt": (tm, tk), "tn": (tk, tm)}[mode]
    b_blk = {"nn": (tk, tn), "nt": (tn, tk), "tn": (tk, tn)}[mode]
    if ga is not None:
        a_blk = (None,) + a_blk
    if gb is not None:
        b_blk = (None,) + b_blk
    if batch_out:
        out_shape = jax.ShapeDtypeStruct((g_n, m, n), out_dtype)
        o_blk = (None, tm, tn)
        grid = (g_n, m // tm, n // tn, nk)
    else:
        out_shape = jax.ShapeDtypeStruct((m, n), out_dtype)
        o_blk = (tm, tn)
        grid = (m // tm, n // tn, g_n, nk)
    scratch = [] if (acc_in_out or n_red == 1) else [pltpu.VMEM((tm, tn), F32)]
    any_spec = pl.BlockSpec(memory_space=pl.ANY)
    out_shapes = [out_shape] + ([jax.ShapeDtypeStruct(out_shape.shape, BF16)] if bf16_copy else [])
    res = pl.pallas_call(
        body, name=name, out_shape=tuple(out_shapes + _rs_out(rs_sends) + list(ag_out_shapes)), grid=grid,
        in_specs=[pl.BlockSpec(a_blk, a_idx), pl.BlockSpec(b_blk, b_idx)] + [any_spec] * (n_rs + n_ag),
        out_specs=tuple([pl.BlockSpec(o_blk, o_idx)] * n_out + [any_spec] * (n_rs + n_ag_out)),
        scratch_shapes=scratch + (_rs_scratch(rs_sends) if n_rs else []) + (_ag_scratch(n_ag) if n_ag else []),
        compiler_params=_params(),
    )(a, b, *rs_sends, *ag_srcs)
    return res if len(res) > 1 else res[0]


EW_TILE = 256
ROW_TILE = 512
EPILOGUE_CHUNKS = 8
MXU_WIDTH = 256


def _rms(v):
    return lax.rsqrt(jnp.mean(v * v, axis=-1, keepdims=True) + EPS)


def _rms_bwd(dhat, vh, r):
    return r * (dhat - vh * jnp.mean(dhat * vh, axis=-1, keepdims=True))


def _tok_spec(tm, d):
    return pl.BlockSpec((tm, d), lambda i: (i, 0))


def _vec_spec(d):
    return pl.BlockSpec((1, d), lambda i: (0, 0))


def _mod_spec(tiles_per_seq, d):
    return pl.BlockSpec((None, N_MOD, d), lambda i: (i // tiles_per_seq, 0, 0))


def _seq_acc_spec(tiles_per_seq, d):
    return pl.BlockSpec((None, 1, d), lambda i: (i // tiles_per_seq, 0, 0))


def _acc(ref, val, first):
    if first is False:
        ref[...] += val
        return

    @pl.when(first)
    def _():
        ref[...] = val

    @pl.when(jnp.logical_not(first))
    def _():
        ref[...] += val


def _colsum(v):
    return jnp.sum(v, axis=0, keepdims=True)


def _pre_mix(x2, g_pre, mod, seq, ag_srcs, ag_out_shapes, ag_dests):
    t, d = x2.shape
    tm = EW_TILE
    n_steps = t // tm
    n_ag, n_ag_out = len(ag_srcs), len(ag_out_shapes)

    def body(x_ref, g_ref, mod_ref, *rest):
        ag_src, h_ref = rest[:n_ag], rest[n_ag]
        ag_out, sems = rest[n_ag + 1:n_ag + 1 + n_ag_out], rest[n_ag + 1 + n_ag_out:]
        ag_start, ag_forward, ag_finish = _ag_phases(ag_dests, ag_src, ag_out, *sems)
        step = pl.program_id(0)
        pl.when(step == 0)(ag_start)
        xv = x_ref[...]
        n = xv * _rms(xv) * g_ref[...]
        h_ref[...] = (n * (1.0 + mod_ref[1:2, :]) + mod_ref[0:1, :]).astype(BF16)

        @pl.when(step == n_steps - 1)
        def _():
            ag_forward()
            ag_finish()

    any_spec = pl.BlockSpec(memory_space=pl.ANY)
    return pl.pallas_call(
        body, name="pre_mix", out_shape=(jax.ShapeDtypeStruct((t, d), BF16), *ag_out_shapes), grid=(n_steps,),
        in_specs=[_tok_spec(tm, d), _vec_spec(d), _mod_spec(seq // tm, d)] + [any_spec] * n_ag,
        out_specs=(_tok_spec(tm, d), *([any_spec] * n_ag_out)),
        scratch_shapes=_ag_scratch(n_ag), compiler_params=_params(),
    )(x2, g_pre, mod, *ag_srcs)


def _matmul_rows(a, b, tm, seq, name, epilogue, ep_in, ep_in_kinds, ep_out, ep_out_kinds, rs_sends=()):
    g_n = a.shape[0] if a.ndim == 3 else None
    (m, k), n = a.shape[-2:], b.shape[-1]
    tps = seq // tm
    n_i = m // tm
    n_rs = len(rs_sends)
    rs_shapes = [r.shape for r in rs_sends]
    n_in, n_out = len(ep_in), len(ep_out)
    n_cols = n // MXU_WIDTH
    rc, cw = tm // EPILOGUE_CHUNKS, n // n_cols

    def prev(i):
        return jnp.maximum(i - 1, 0)

    def spec(kind):
        return {"tok": pl.BlockSpec((tm, n), lambda i: (prev(i), 0)),
                "vec": pl.BlockSpec((1, n), lambda i: (0, 0)),
                "mod": pl.BlockSpec((None, N_MOD, n), lambda i: (prev(i) // tps, 0, 0)),
                "seq": pl.BlockSpec((None, 1, n), lambda i: (prev(i) // tps, 0, 0)),
                "loss": pl.BlockSpec((1, LANES), lambda i: (0, 0))}[kind]

    def body(a_ref, b_ref, *rest):
        in_refs, rest = rest[:n_in], rest[n_in:]
        rs_src, rest = rest[:n_rs], rest[n_rs:]
        out_refs, rest = rest[:n_out], rest[n_out:]
        rs_dst, rest = rest[:n_rs], rest[n_rs:]
        fin = rest[0]
        i = pl.program_id(0)
        if n_rs:
            rs_start, rs_finish = _rs_phases(rs_shapes, rs_src, rs_dst, *rest[1:])
            pl.when(i == 0)(rs_start)

        def product(cols):
            if g_n is None:
                return _dot_nn(a_ref[...], b_ref[:, cols])
            p = _dot_nn(a_ref[0], b_ref[0, :, cols])
            for g in range(1, g_n):
                p = p + _dot_nn(a_ref[g], b_ref[g, :, cols])
            return p

        def step(with_epilogue, with_matmul):
            parts = []
            for c in range(EPILOGUE_CHUNKS):
                if with_epilogue:
                    rows = pl.ds(c * rc, rc)
                    epilogue(fin[rows, :], i - 1, tps, in_refs, out_refs, rows, c)
                while with_matmul and len(parts) < (c + 1) * n_cols // EPILOGUE_CHUNKS:
                    cols = slice(len(parts) * cw, (len(parts) + 1) * cw)
                    parts.append((cols, product(cols)))
            for cols, v in parts:
                fin[:, cols] = v

        pl.when(i == 0)(functools.partial(step, False, True))
        pl.when(jnp.logical_and(i > 0, i < n_i))(functools.partial(step, True, True))
        pl.when(i == n_i)(functools.partial(step, True, False))

        if n_rs:
            pl.when(i == n_i)(rs_finish)

    def row(i):
        return jnp.minimum(i, n_i - 1)

    if g_n is None:
        a_spec = pl.BlockSpec((tm, k), lambda i: (row(i), 0))
        b_spec = pl.BlockSpec(b.shape, lambda i: (0, 0), pipeline_mode=pl.Buffered(1))
    else:
        a_spec = pl.BlockSpec((g_n, tm, k), lambda i: (0, row(i), 0))
        b_spec = pl.BlockSpec(b.shape, lambda i: (0, 0, 0), pipeline_mode=pl.Buffered(1))
    any_spec = pl.BlockSpec(memory_space=pl.ANY)
    res = pl.pallas_call(
        body, name=name, grid=(n_i + 1,), out_shape=tuple(list(ep_out) + _rs_out(rs_sends)),
        in_specs=[a_spec, b_spec] + [spec(kd) for kd in ep_in_kinds] + [any_spec] * n_rs,
        out_specs=tuple([spec(kd) for kd in ep_out_kinds] + [any_spec] * n_rs),
        scratch_shapes=[pltpu.VMEM((tm, n), F32)] + (_rs_scratch(rs_sends) if n_rs else []),
        compiler_params=_params(),
    )(a, b, *ep_in, *rs_sends)
    return res


def _first(cond, chunk):
    return cond if chunk == 0 else False


def _mid_epilogue(mv, i, tps, in_refs, out_refs, rows, chunk):
    x_ref, gpost_ref, gpre_ref, mod_ref = in_refs
    mix_ref, x1_ref, h2_ref = out_refs
    mix_ref[rows, :] = mv
    x1 = x_ref[rows, :] + mod_ref[2:3, :] * (mv * _rms(mv) * gpost_ref[...])
    x1_ref[rows, :] = x1
    n = x1 * _rms(x1) * gpre_ref[...]
    h2_ref[rows, :] = (n * (1.0 + mod_ref[4:5, :]) + mod_ref[3:4, :]).astype(BF16)


def _post_epilogue(fv, i, tps, in_refs, out_refs, rows, chunk):
    x1_ref, tgt_ref, g_ref, mod_ref = in_refs
    loss_ref, dy_ref, df_ref, dgate_ref, gg_ref = out_refs
    d = fv.shape[1]
    r = _rms(fv)
    fh = fv * r
    nf = fh * g_ref[...]
    gate = mod_ref[5:6, :]
    err = x1_ref[rows, :] + gate * nf - tgt_ref[rows, :]
    _acc(loss_ref, jnp.sum(_colsum(err * err), axis=1, keepdims=True) * jnp.ones((1, LANES), F32),
         _first(i == 0, chunk))
    dy = err * (1.0 / d)
    dy_ref[rows, :] = dy
    _acc(dgate_ref, _colsum(dy * nf), _first(i % tps == 0, chunk))
    dn = dy * gate
    _acc(gg_ref, _colsum(dn * fh), _first(i == 0, chunk))
    df_ref[rows, :] = _rms_bwd(dn * g_ref[...], fh, r).astype(BF16)


def _bwd_mid_epilogue(dh, i, tps, in_refs, out_refs, rows, chunk):
    dy_ref, x1_ref, mix_ref, gpre_ref, gpost_ref, mod_ref = in_refs
    dx1_ref, dmix_ref, dshift_ref, dscale_ref, dgate_ref, ggpre_ref, ggpost_ref = out_refs
    seq_first, first = _first(i % tps == 0, chunk), _first(i == 0, chunk)
    x1 = x1_ref[rows, :]
    r = _rms(x1)
    xh = x1 * r
    gpre = gpre_ref[...]
    _acc(dshift_ref, _colsum(dh), seq_first)
    _acc(dscale_ref, _colsum(dh * xh * gpre), seq_first)
    dn = dh * (1.0 + mod_ref[4:5, :])
    _acc(ggpre_ref, _colsum(dn * xh), first)
    dx1 = dy_ref[rows, :] + _rms_bwd(dn * gpre, xh, r)
    dx1_ref[rows, :] = dx1
    mv = mix_ref[rows, :]
    rm = _rms(mv)
    mh = mv * rm
    gpost = gpost_ref[...]
    _acc(dgate_ref, _colsum(dx1 * mh * gpost), seq_first)
    dnm = dx1 * mod_ref[2:3, :]
    _acc(ggpost_ref, _colsum(dnm * mh), first)
    dmix_ref[rows, :] = _rms_bwd(dnm * gpost, mh, rm).astype(BF16)


def _bwd_pre_epilogue(dh, i, tps, in_refs, out_refs, rows, chunk):
    dx1_ref, x_ref, g_ref, mod_ref = in_refs
    gx_ref, dshift_ref, dscale_ref, gg_ref = out_refs
    seq_first = _first(i % tps == 0, chunk)
    xv = x_ref[rows, :]
    r = _rms(xv)
    xh = xv * r
    g = g_ref[...]
    _acc(dshift_ref, _colsum(dh), seq_first)
    _acc(dscale_ref, _colsum(dh * xh * g), seq_first)
    dn = dh * (1.0 + mod_ref[1:2, :])
    _acc(gg_ref, _colsum(dn * xh), _first(i == 0, chunk))
    gx_ref[rows, :] = dx1_ref[rows, :] + _rms_bwd(dn * g, xh, r)


def _ffn_up(h2, wgu, tm, tn):
    t, d = h2.shape
    f = wgu.shape[1]

    def body(h_ref, w_ref, gu_ref, act_ref):
        h = h_ref[...]
        g = _dot_nt(h, w_ref[0])
        u = _dot_nt(h, w_ref[1])
        gu_ref[0] = g.astype(BF16)
        gu_ref[1] = u.astype(BF16)
        act_ref[...] = (g * jax.nn.sigmoid(g) * u).astype(BF16)

    return pl.pallas_call(
        body, name="ffn_up", grid=(f // tn, t // tm),
        out_shape=(jax.ShapeDtypeStruct((2, t, f), BF16), jax.ShapeDtypeStruct((t, f), BF16)),
        in_specs=[pl.BlockSpec((tm, d), lambda j, i: (i, 0)), pl.BlockSpec((2, tn, d), lambda j, i: (0, j, 0))],
        out_specs=(pl.BlockSpec((2, tm, tn), lambda j, i: (0, i, j)), pl.BlockSpec((tm, tn), lambda j, i: (i, j))),
        compiler_params=_params(),
    )(h2, wgu)


def _ffn_act_bwd(df, wd, gu, tm, tn):
    t, d = df.shape
    f = wd.shape[0]

    def body(df_ref, w_ref, gu_ref, dgu_ref):
        da = _dot_nt(df_ref[...], w_ref[...])
        g = gu_ref[0].astype(F32)
        u = gu_ref[1].astype(F32)
        s = jax.nn.sigmoid(g)
        silu = g * s
        dgu_ref[0] = (da * u * (s + silu * (1.0 - s))).astype(BF16)
        dgu_ref[1] = (da * silu).astype(BF16)

    return pl.pallas_call(
        body, name="ffn_act_bwd", grid=(f // tn, t // tm),
        out_shape=jax.ShapeDtypeStruct((2, t, f), BF16),
        in_specs=[pl.BlockSpec((tm, d), lambda j, i: (i, 0)), pl.BlockSpec((tn, d), lambda j, i: (j, 0)),
                  pl.BlockSpec((2, tm, tn), lambda j, i: (0, i, j))],
        out_specs=pl.BlockSpec((2, tm, tn), lambda j, i: (0, i, j)),
        compiler_params=_params(),
    )(df, wd, gu)


SIGN_BIT = 0x80000000
Q_SCALE = 1.0 / math.sqrt(HEAD_DIM)


def _softplus(z):
    neg_abs = lax.bitcast_convert_type(lax.bitcast_convert_type(z, jnp.uint32) | jnp.uint32(SIGN_BIT), F32)
    return jnp.maximum(z, 0.0) + jnp.log(1.0 + jnp.exp(neg_abs))


def _hi_lo(v):
    hi = v.astype(BF16)
    return jnp.concatenate([hi, (v - hi.astype(F32)).astype(BF16)], axis=1)


def _emit_skewed(chains, lag=1):
    for t in range(max(len(ch) for ch in chains) + lag * (len(chains) - 1)):
        for c, ch in enumerate(chains):
            if 0 <= t - lag * c < len(ch):
                ch[t - lag * c]()


def _fwd_chain(blk, qs, k_ref, v_ref, c0, kb, cols, mask, ntri, lane, tq):
    st = {}

    def scores():
        st["z"] = _dot_nt(qs, k_ref[pl.ds(c0, tq), cols])

    def soft():
        sp = _softplus(st["z"])
        if mask is not None:
            sp = jnp.where(mask, sp, 0.0)
        st["parts"] = _hi_lo(sp)
        st["cur"] = blk["cur"]
        blk["cm"] = jnp.where(lane == kb, blk["cur"], blk["cm"])
        blk["cur"] = blk["cur"] - jnp.sum(sp, axis=1, keepdims=True)

    def sums():
        st["s"] = _dot_nn(st["parts"], ntri)

    def weights():
        w = jnp.exp(st["z"] + st["s"] + st["cur"])
        if mask is not None:
            w = jnp.where(mask, w, 0.0)
        st["w"] = w.astype(BF16)

    def out():
        p = _dot_nn(st["w"], v_ref[pl.ds(c0, tq), cols])
        blk["pv"] = p if blk["pv"] is None else blk["pv"] + p

    return [scores, soft, sums, weights, out]


def _bwd_chain(blk, qs, dos, cs, k_ref, v_ref, dk_ref, dv_ref, c0, kb, cols, mask, ntri, tri_i, lane, tq):
    st = {}

    def scores():
        st["z"] = _dot_nt(qs, k_ref[pl.ds(c0, tq), cols])
        st["dw"] = _dot_nt(dos, v_ref[pl.ds(c0, tq), cols])

    def soft():
        sp = _softplus(st["z"])
        if mask is not None:
            sp = jnp.where(mask, sp, 0.0)
        st["sp"] = sp
        st["parts"] = _hi_lo(sp)
        st["cur"] = jnp.sum(jnp.where(lane == kb, cs, 0.0), axis=1, keepdims=True)

    def sums():
        st["s"] = _dot_nn(st["parts"], ntri)

    def weights():
        w = jnp.exp(st["z"] + st["s"] + st["cur"])
        if mask is not None:
            w = jnp.where(mask, w, 0.0)
        ee = w * st["dw"]
        st["w"], st["ee"], st["ec"] = w.astype(BF16), ee, blk["ec"]
        blk["ec"] = blk["ec"] + jnp.sum(ee, axis=1, keepdims=True)

    def prefix():
        st["einc"] = _dot_nn(st["ee"].astype(BF16), tri_i)

    def dz():
        v = st["ee"] - jnp.exp(st["z"] - st["sp"]) * (st["einc"] + st["ec"])
        if mask is not None:
            v = jnp.where(mask, v, 0.0)
        st["dz"] = v.astype(BF16)

    def grads():
        p = _dot_nn(st["dz"], k_ref[pl.ds(c0, tq), cols])
        blk["dq"] = p if blk["dq"] is None else blk["dq"] + p
        dk_ref[pl.ds(c0, tq), :] += _dot_tn(st["dz"], qs)
        dv_ref[pl.ds(c0, tq), :] += _dot_tn(st["w"], dos)

    return [scores, soft, sums, weights, prefix, dz, grads]


def _stack_heads(v, lane, scale=None):
    if scale is not None:
        v = v * jnp.asarray(scale, v.dtype)
    zero = jnp.zeros_like(v)
    return jnp.concatenate([jnp.where(lane < HEAD_DIM, v, zero), jnp.where(lane >= HEAD_DIM, v, zero)], axis=0)


def _diag_mask(tq):
    row = lax.broadcasted_iota(jnp.int32, (2 * tq, tq), 0)
    col = lax.broadcasted_iota(jnp.int32, (2 * tq, tq), 1)
    return col < jnp.where(row >= tq, row - tq, row)


def _attn_fwd(proj, tri_after, n_seq, seq, ag_srcs, ag_out_shapes, ag_dests):
    t = proj.shape[0]
    tq = ATT_TILE
    npp = ATT_PAIRS
    n_blk = (proj.shape[1] // 4) // (npp * LANES)
    n_ag, n_ag_out = len(ag_srcs), len(ag_out_shapes)
    n_steps = n_seq * n_blk

    def body(q_ref, k_ref, v_ref, tri_ref, *rest):
        ag_src, rest = rest[:n_ag], rest[n_ag:]
        o_ref, cs_ref = rest[:2]
        ag_out, rest = rest[2:2 + n_ag_out], rest[2 + n_ag_out:]
        oacc, cmat, carry = rest[:3]
        ag_start, ag_forward, ag_finish = _ag_phases(ag_dests, ag_src, ag_out, *rest[3:])
        step = pl.program_id(0) * n_blk + pl.program_id(1)
        pl.when(step == 0)(ag_start)
        pl.when(step == (3 * n_steps) // 4)(ag_forward)
        lane = lax.broadcasted_iota(jnp.int32, (1, LANES), 1)
        ntri = tri_ref[...]
        diag = _diag_mask(tq)

        def q_tile(qi, _):
            r0 = pl.multiple_of(qi * tq, tq)
            qs = [_stack_heads(q_ref[pl.ds(r0, tq), pp * LANES:(pp + 1) * LANES], lane, Q_SCALE)
                  for pp in range(npp)]
            carry[...] = jnp.zeros_like(carry)
            cmat[...] = jnp.zeros_like(cmat)
            oacc[...] = jnp.zeros_like(oacc)

            def run_tiles(tiles):
                blocks = [dict(cur=carry[pp], cm=cmat[pp], pv=None) for pp in range(npp)]
                chains = []
                for kb, mask in tiles:
                    c0 = pl.multiple_of(kb * tq, tq)
                    for pp in range(npp):
                        chains.append(_fwd_chain(blocks[pp], qs[pp], k_ref, v_ref, c0, kb,
                                                 slice(pp * LANES, (pp + 1) * LANES), mask, ntri, lane, tq))
                _emit_skewed(chains)
                for pp in range(npp):
                    oacc[pp] += blocks[pp]["pv"]
                    cmat[pp] = blocks[pp]["cm"]
                    carry[pp] = blocks[pp]["cur"]

            odd = qi % 2

            @pl.when(odd == 0)
            def _():
                run_tiles([(qi, diag)])

            @pl.when(odd == 1)
            def _():
                run_tiles([(qi, diag), (qi - 1, None)])

            def pair(j, _):
                kb = qi - 1 - odd - 2 * j
                run_tiles([(kb, None), (kb - 1, None)])
                return 0

            lax.fori_loop(0, qi // 2, pair, 0)
            for pp in range(npp):
                c_off = 2 * pp * LANES
                cs_ref[pl.ds(r0, tq), c_off:c_off + LANES] = cmat[pp, 0:tq, :]
                cs_ref[pl.ds(r0, tq), c_off + LANES:c_off + 2 * LANES] = cmat[pp, tq:2 * tq, :]
                o_ref[pl.ds(r0, tq), pp * LANES:(pp + 1) * LANES] = jnp.where(
                    lane < HEAD_DIM, oacc[pp, 0:tq, :], oacc[pp, tq:2 * tq, :]).astype(BF16)
            return 0

        lax.fori_loop(0, seq // tq, q_tile, 0)
        pl.when(step == n_steps - 1)(ag_finish)

    wid = npp * LANES
    blk = lambda off: pl.BlockSpec((seq, wid), lambda b, p: (b, off + p))
    any_spec = pl.BlockSpec(memory_space=pl.ANY)
    return pl.pallas_call(
        body, name="attn_fwd", grid=(n_seq, n_blk),
        out_shape=(jax.ShapeDtypeStruct((2, t, n_blk * wid), BF16),
                   jax.ShapeDtypeStruct((t, n_blk * 2 * wid), F32), *ag_out_shapes),
        in_specs=[blk(0), blk(n_blk), blk(2 * n_blk), pl.BlockSpec((2 * tq, tq), lambda b, p: (0, 0))]
        + [any_spec] * n_ag,
        out_specs=(pl.BlockSpec((None, seq, wid), lambda b, p: (0, b, p)),
                   pl.BlockSpec((seq, 2 * wid), lambda b, p: (b, p)), *([any_spec] * n_ag_out)),
        scratch_shapes=[pltpu.VMEM((npp, 2 * tq, LANES), F32), pltpu.VMEM((npp, 2 * tq, LANES), F32),
                        pltpu.VMEM((npp, 2 * tq, 1), F32)] + _ag_scratch(n_ag),
        compiler_params=_params(),
    )(proj, proj, proj, tri_after, *ag_srcs)


def _attn_bwd(proj, dcat, cstats, tri_after, tri_incl, n_seq, seq, rs_sends):
    t = proj.shape[0]
    tq = ATT_TILE
    npp = ATT_PAIRS
    width = proj.shape[1] // 4
    n_blk = width // (npp * LANES)
    n_rs = len(rs_sends)
    rs_shapes = [r.shape for r in rs_sends]
    n_steps = n_seq * n_blk

    def body(q_ref, k_ref, v_ref, do_ref, cs_ref, tria_ref, trii_ref, *rest):
        rs_src, rest = rest[:n_rs], rest[n_rs:]
        out_ref = rest[0]
        rs_dst, rest = rest[1:1 + n_rs], rest[1 + n_rs:]
        dq_acc, dk_acc, dv_acc, ecarry = rest[:4]
        rs_start, rs_finish = _rs_phases(rs_shapes, rs_src, rs_dst, *rest[4:])
        step = pl.program_id(0) * n_blk + pl.program_id(1)
        pl.when(step == 0)(rs_start)
        lane = lax.broadcasted_iota(jnp.int32, (1, LANES), 1)
        ntri = tria_ref[...]
        tri_i = trii_ref[...]
        diag = _diag_mask(tq)
        dk_acc[...] = jnp.zeros_like(dk_acc)
        dv_acc[...] = jnp.zeros_like(dv_acc)

        def q_tile(qi, _):
            r0 = pl.multiple_of(qi * tq, tq)
            qs, dos, cs = [], [], []
            for pp in range(npp):
                cols = slice(pp * LANES, (pp + 1) * LANES)
                qs.append(_stack_heads(q_ref[pl.ds(r0, tq), cols], lane, Q_SCALE))
                dos.append(_stack_heads(do_ref[pl.ds(r0, tq), cols], lane))
                c_off = 2 * pp * LANES
                cs.append(jnp.concatenate([cs_ref[pl.ds(r0, tq), c_off:c_off + LANES],
                                           cs_ref[pl.ds(r0, tq), c_off + LANES:c_off + 2 * LANES]], axis=0))
            ecarry[...] = jnp.zeros_like(ecarry)
            dq_acc[...] = jnp.zeros_like(dq_acc)

            def run_tiles(tiles):
                blocks = [dict(ec=ecarry[pp], dq=None) for pp in range(npp)]
                chains = []
                for kb, mask in tiles:
                    c0 = pl.multiple_of(kb * tq, tq)
                    for pp in range(npp):
                        chains.append(_bwd_chain(
                            blocks[pp], qs[pp], dos[pp], cs[pp], k_ref, v_ref, dk_acc.at[pp], dv_acc.at[pp],
                            c0, kb, slice(pp * LANES, (pp + 1) * LANES), mask, ntri, tri_i, lane, tq))
                _emit_skewed(chains)
                for pp in range(npp):
                    dq_acc[pp] += blocks[pp]["dq"]
                    ecarry[pp] = blocks[pp]["ec"]

            def pair(j, _):
                run_tiles([(2 * j, None), (2 * j + 1, None)])
                return 0

            lax.fori_loop(0, qi // 2, pair, 0)
            odd = qi % 2

            @pl.when(odd == 0)
            def _():
                run_tiles([(qi, diag)])

            @pl.when(odd == 1)
            def _():
                run_tiles([(qi - 1, None), (qi, diag)])

            for pp in range(npp):
                dq = jnp.where(lane < HEAD_DIM, dq_acc[pp, 0:tq, :], dq_acc[pp, tq:2 * tq, :])
                out_ref[0, pl.ds(r0, tq), pp * LANES:(pp + 1) * LANES] = (dq * Q_SCALE).astype(BF16)
            return 0

        lax.fori_loop(0, seq // tq, q_tile, 0)
        for pp in range(npp):
            cols = slice(pp * LANES, (pp + 1) * LANES)
            out_ref[1, :, cols] = dk_acc[pp].astype(BF16)
            out_ref[2, :, cols] = dv_acc[pp].astype(BF16)
        pl.when(step == n_steps - 1)(rs_finish)

    wid = npp * LANES
    blk = lambda off: pl.BlockSpec((seq, wid), lambda b, p: (b, off + p))
    tri_spec = pl.BlockSpec((2 * tq, tq), lambda b, p: (0, 0))
    any_spec = pl.BlockSpec(memory_space=pl.ANY)
    return pl.pallas_call(
        body, name="attn_bwd", grid=(n_seq, n_blk),
        out_shape=(jax.ShapeDtypeStruct((4, t, width), BF16), *_rs_out(rs_sends)),
        in_specs=[blk(0), blk(n_blk), blk(2 * n_blk), pl.BlockSpec((seq, wid), lambda b, p: (b, p)),
                  pl.BlockSpec((seq, 2 * wid), lambda b, p: (b, p)), tri_spec,
                  pl.BlockSpec((tq, tq), lambda b, p: (0, 0))] + [any_spec] * n_rs,
        out_specs=(pl.BlockSpec((3, seq, wid), lambda b, p: (0, b, p)), *([any_spec] * n_rs)),
        scratch_shapes=[pltpu.VMEM((npp, 2 * tq, LANES), F32), pltpu.VMEM((npp, seq, LANES), F32),
                        pltpu.VMEM((npp, seq, LANES), F32), pltpu.VMEM((npp, 2 * tq, 1), F32)]
        + _rs_scratch(rs_sends),
        compiler_params=_params(),
    )(proj, proj, proj, dcat, cstats, tri_after, tri_incl, *rs_sends)


def _window_terms(g, rows):
    win = jnp.where(g == 0, POOL_WINDOWS[0], jnp.where(g == 1, POOL_WINDOWS[1],
                    jnp.where(g == 2, POOL_WINDOWS[2], POOL_WINDOWS[3])))
    cnt = jnp.minimum(rows + 1, win).astype(F32)
    return win, cnt


def _window_sum(v, g, rows, forward):
    s_len = v.shape[0]
    sums = []
    s = v
    for step in range(len(POOL_WINDOWS)):
        sh = 1 << step
        if forward:
            shifted = jnp.where(rows < s_len - sh, pltpu.roll(s, s_len - sh, axis=0), 0.0)
        else:
            shifted = jnp.where(rows >= sh, pltpu.roll(s, sh, axis=0), 0.0)
        s = s + shifted
        sums.append(s)
    return jnp.where(g == 0, sums[0], jnp.where(g == 1, sums[1], jnp.where(g == 2, sums[2], sums[3])))


def _pooled(u, g, rows):
    _, cnt = _window_terms(g, rows)
    return _window_sum(u, g, rows, forward=False) / cnt - u


def _pool_fwd(proj, w_pool, pool_scale, cat, n_seq, seq):
    n_grp = len(POOL_WINDOWS)
    u_off = 3 * (proj.shape[1] // 4) // LANES

    def body(u_ref, w_ref, s_ref, alias_ref, o_ref):
        del alias_ref
        g = pl.program_id(1)
        rows = lax.broadcasted_iota(jnp.int32, (seq, 1), 0)
        pooled = _pooled(u_ref[...].astype(F32), g, rows)
        y = _dot_nn(pooled.astype(BF16), w_ref[...].astype(BF16))
        o_ref[...] = (y * s_ref[...]).astype(BF16)

    return pl.pallas_call(
        body, name="pool_fwd", grid=(n_seq, n_grp),
        out_shape=jax.ShapeDtypeStruct(cat.shape, BF16),
        in_specs=[pl.BlockSpec((seq, LANES), lambda b, g: (b, u_off + g)),
                  pl.BlockSpec((None, POOL_GROUP_DIM, POOL_GROUP_DIM), lambda b, g: (g, 0, 0)),
                  pl.BlockSpec((1, POOL_GROUP_DIM), lambda b, g: (0, g)),
                  pl.BlockSpec(memory_space=pl.ANY)],
        out_specs=pl.BlockSpec((None, seq, LANES), lambda b, g: (1, b, g)),
        input_output_aliases={3: 0},
        compiler_params=_params(),
    )(proj, w_pool, pool_scale, cat)


def _pool_bwd(proj, dcat, w_pool, pool_scale, dqkv, n_seq, seq):
    n_grp = len(POOL_WINDOWS)
    width = proj.shape[1] // 4
    u_off = 3 * width // LANES
    dp_off = width // LANES

    def body(u_ref, dp_ref, w_ref, s_ref, alias_ref, du_ref, gw_ref, gs_ref):
        del alias_ref
        g = pl.program_id(0)
        b = pl.program_id(1)
        rows = lax.broadcasted_iota(jnp.int32, (seq, 1), 0)
        pooled = _pooled(u_ref[...].astype(F32), g, rows)
        pb = pooled.astype(BF16)
        wb = w_ref[...].astype(BF16)
        z = _dot_nn(pb, wb)
        dp = dp_ref[...].astype(F32)
        _acc(gs_ref, _colsum(dp * z), b == 0)
        dys = (dp * s_ref[...]).astype(BF16)
        _acc(gw_ref, _dot_tn(pb, dys), b == 0)
        dpooled = _dot_nt(dys, wb)
        _, cnt = _window_terms(g, rows)
        du = _window_sum(dpooled / cnt, g, rows, forward=True) - dpooled
        du_ref[...] = du.astype(BF16)

    t = proj.shape[0]
    return pl.pallas_call(
        body, name="pool_bwd", grid=(n_grp, n_seq),
        out_shape=(jax.ShapeDtypeStruct(dqkv.shape, BF16),
                   jax.ShapeDtypeStruct((n_grp, POOL_GROUP_DIM, POOL_GROUP_DIM), F32),
                   jax.ShapeDtypeStruct((1, n_grp * POOL_GROUP_DIM), F32)),
        in_specs=[pl.BlockSpec((seq, LANES), lambda g, b: (b, u_off + g)),
                  pl.BlockSpec((seq, LANES), lambda g, b: (b, dp_off + g)),
                  pl.BlockSpec((None, POOL_GROUP_DIM, POOL_GROUP_DIM), lambda g, b: (g, 0, 0)),
                  pl.BlockSpec((1, POOL_GROUP_DIM), lambda g, b: (0, g)),
                  pl.BlockSpec(memory_space=pl.ANY)],
        out_specs=(pl.BlockSpec((None, seq, LANES), lambda g, b: (3, b, g)),
                   pl.BlockSpec((None, POOL_GROUP_DIM, POOL_GROUP_DIM), lambda g, b: (g, 0, 0)),
                   pl.BlockSpec((1, POOL_GROUP_DIM), lambda g, b: (0, g))),
        input_output_aliases={4: 0},
        compiler_params=_params(),
    )(proj, dcat, w_pool, pool_scale, dqkv)


def _cond_fwd(c_all, w_cond, b_cols):
    n, _ = c_all.shape
    cols = w_cond.shape[1]

    def body(c_ref, w_ref, b_ref, o_ref):
        cv = c_ref[...]
        a = cv * jax.nn.sigmoid(cv)
        o_ref[...] = jnp.dot(a, w_ref[...], preferred_element_type=F32,
                             precision=lax.Precision.HIGHEST) + b_ref[...]

    return pl.pallas_call(
        body, name="cond_fwd", out_shape=jax.ShapeDtypeStruct((n, cols), F32),
        compiler_params=_params(),
    )(c_all, w_cond, b_cols)


def _cond_bwd(c_all, dmod_all, dmod_cols):
    n, d = c_all.shape
    cols = dmod_cols.shape[1]

    def body(c_ref, dm_ref, dmc_ref, gw_ref, gb_ref):
        cv = c_ref[...]
        a = cv * jax.nn.sigmoid(cv)
        gw_ref[...] = lax.dot_general(a, dmc_ref[...], (((0,), (0,)), ((), ())),
                                      preferred_element_type=F32, precision=lax.Precision.HIGHEST)
        gb_ref[...] = _colsum(dm_ref[...])

    return pl.pallas_call(
        body, name="cond_bwd",
        out_shape=(jax.ShapeDtypeStruct((d, cols), F32), jax.ShapeDtypeStruct((1, dmod_all.shape[1]), F32)),
        compiler_params=_params(),
    )(c_all, dmod_all, dmod_cols)


def _adamw_math(w, g, m, v):
    m = ADAM_B1 * m + (1.0 - ADAM_B1) * g
    v = ADAM_B2 * v + (1.0 - ADAM_B2) * (g * g)
    m_hat = m / (1.0 - ADAM_B1 ** ADAM_STEP)
    v_hat = v / (1.0 - ADAM_B2 ** ADAM_STEP)
    delta = -ADAM_LR * (m_hat / (jnp.sqrt(v_hat) + ADAM_EPS) + ADAM_WD * w)
    return delta, m, v


def _adamw(w, g, m, v, rows, name):
    r, cdim = w.shape

    def body(w_ref, g_ref, m_ref, v_ref, d_ref, nm_ref, nv_ref):
        d_ref[...], nm_ref[...], nv_ref[...] = _adamw_math(w_ref[...], g_ref[...], m_ref[...], v_ref[...])

    spec = pl.BlockSpec((rows, cdim), lambda i: (i, 0))
    sds = jax.ShapeDtypeStruct((r, cdim), F32)
    return pl.pallas_call(
        body, name=name, grid=(r // rows,), out_shape=(sds, sds, sds),
        in_specs=[spec] * 4, out_specs=(spec, spec, spec), compiler_params=_params(),
    )(w, g, m, v)


def _adamw_small(ws, gparts, ms, vs, name):
    n = len(ws)

    def body(*refs):
        w_r, g_r, m_r, v_r = refs[:n], refs[n:2 * n], refs[2 * n:3 * n], refs[3 * n:4 * n]
        outs = refs[4 * n:]
        for i in range(n):
            g = g_r[i][0]
            for dev in range(1, g_r[i].shape[0]):
                g = g + g_r[i][dev]
            delta, m, v = _adamw_math(w_r[i][...], g, m_r[i][...], v_r[i][...])
            outs[i][...] = g
            outs[n + i][...] = delta
            outs[2 * n + i][...] = m
            outs[3 * n + i][...] = v

    sds = [jax.ShapeDtypeStruct(w.shape, F32) for w in ws]
    return pl.pallas_call(
        body, name=name, out_shape=tuple(sds * 4), compiler_params=_params(),
    )(*ws, *gparts, *ms, *vs)


def kernel(x, c, w_cond, b_cond, g_mix_pre, g_mix_post, w_in, w_pool, pool_scale, w_out, g_ffn_pre, g_ffn_post, w_gate, w_up, w_down, loss_target, m_w_cond, m_b_cond, m_g_mix_pre, m_g_mix_post, m_w_in, m_w_pool, m_pool_scale, m_w_out, m_g_ffn_pre, m_g_ffn_post, m_w_gate, m_w_up, m_w_down, v_w_cond, v_b_cond, v_g_mix_pre, v_g_mix_post, v_w_in, v_w_pool, v_pool_scale, v_w_out, v_g_ffn_pre, v_g_ffn_post, v_w_gate, v_w_up, v_w_down):
    n_seq, seq, d = x.shape
    t = n_seq * seq
    xi, yi, ci = _mesh_pos()
    me = 4 * xi + 2 * yi + ci
    x2 = x.reshape(t, d)
    tgt2 = loss_target.reshape(t, d)
    in_rows = w_in.shape[2]
    out_rows = w_out.shape[1]
    ff_rows = w_gate.shape[2]
    ff = N_DEV * ff_rows
    cond_cols = w_cond.shape[2]

    win_t = w_in[0].T.astype(BF16)
    wout_s = w_out[0].astype(BF16)
    wg_t = w_gate[0].T.astype(BF16)
    wu_t = w_up[0].T.astype(BF16)
    wd_s = w_down[0].astype(BF16)
    (c_all,) = _all_gather([c], [jax.ShapeDtypeStruct((N_DEV, n_seq, d), F32)], [(0, ())], "ag_c")
    c_all = c_all.reshape(N_DEV * n_seq, d)

    b_cols = lax.dynamic_slice_in_dim(b_cond, me * cond_cols, cond_cols, axis=1)
    mod_cols = _cond_fwd(c_all, w_cond[0], b_cols)
    (mod_g,) = _all_gather([mod_cols], [jax.ShapeDtypeStruct((N_DEV,) + mod_cols.shape, F32)], [(0, ())], "ag_mod")
    mod_mine = lax.dynamic_slice_in_dim(mod_g, me * n_seq, n_seq, axis=1)
    mod = jnp.transpose(mod_mine, (1, 0, 2)).reshape(n_seq, N_MOD, d)

    h1, win_g = _pre_mix(x2, g_mix_pre, mod, seq, [win_t],
                         [jax.ShapeDtypeStruct((N_DEV, in_rows, d), BF16)], [(0, ())])
    win_full = win_g.reshape(N_DEV * in_rows, d)
    proj = _matmul(h1, win_full, "nt", BF16, 512, N_DEV * in_rows, d, "proj")
    tq = ATT_TILE
    ids = jnp.arange(tq)
    tri_after = jnp.tile(-(ids[:, None] >= ids[None, :]).astype(BF16), (2, 1))
    tri_incl = (ids[:, None] <= ids[None, :]).astype(BF16)
    attn, cstats, wout_g, wgu_g, wd_g = _attn_fwd(
        proj, tri_after, n_seq, seq, [wout_s, wg_t, wu_t, wd_s],
        [jax.ShapeDtypeStruct((N_DEV, out_rows, d), BF16), jax.ShapeDtypeStruct((2, N_DEV, ff_rows, d), BF16),
         jax.ShapeDtypeStruct((N_DEV, ff_rows, d), BF16)],
        [(0, ()), (1, (0,)), (1, (1,)), (2, ())])
    wout_full = wout_g.reshape(N_DEV * out_rows, d)
    wgu_full = wgu_g.reshape(2, ff, d)
    wd_full = wd_g.reshape(ff, d)
    cat = _pool_fwd(proj, w_pool[0], pool_scale, attn, n_seq, seq)
    tok_f32, tok_bf16 = jax.ShapeDtypeStruct((t, d), F32), jax.ShapeDtypeStruct((t, d), BF16)
    seq_sds, vec_sds = jax.ShapeDtypeStruct((n_seq, 1, d), F32), jax.ShapeDtypeStruct((1, d), F32)
    mix, x1, h2 = _matmul_rows(
        cat, wout_full.reshape(2, d // 2, d), ROW_TILE, seq, "mix_mid", _mid_epilogue,
        [x2, g_mix_post, g_ffn_pre, mod], ["tok", "vec", "vec", "mod"],
        [tok_f32, tok_f32, tok_bf16], ["tok", "tok", "tok"])
    gu, act = _ffn_up(h2, wgu_full, 512, ff // 2)
    loss_sum, dy, df, dgate_f, gg_ffn_post = _matmul_rows(
        act, wd_full, ROW_TILE, seq, "ffn_down_post", _post_epilogue,
        [x1, tgt2, g_ffn_post, mod], ["tok", "tok", "vec", "mod"],
        [jax.ShapeDtypeStruct((1, LANES), F32), tok_f32, tok_bf16, seq_sds, vec_sds],
        ["loss", "tok", "tok", "seq", "vec"])

    dgu = _ffn_act_bwd(df, wd_full, gu, 512, ff // 2)
    gwd, gwd_b = _matmul(act, df, "tn", F32, ff // 2, d // 2, t, "grad_w_down", bf16_copy=True)
    gwgu, gwgu_b = _matmul(dgu, h2, "tn", F32, ff // 2, d // 2, t, "grad_w_gate_up", bf16_copy=True)
    dx1, dmix, dshift_f, dscale_f, dgate_m, gg_ffn_pre, gg_mix_post = _matmul_rows(
        dgu, wgu_full, ROW_TILE, seq, "dh2_bwd_mid", _bwd_mid_epilogue,
        [dy, x1, mix, g_ffn_pre, g_mix_post, mod], ["tok", "tok", "tok", "vec", "vec", "mod"],
        [tok_f32, tok_bf16, seq_sds, seq_sds, seq_sds, vec_sds, vec_sds],
        ["tok", "tok", "seq", "seq", "seq", "vec", "vec"])
    dcat = _matmul(dmix, wout_full, "nt", BF16, 512, d, d, "dcat")
    gwout, gwout_b = _matmul(cat, dmix, "tn", F32, d // 2, d, t, "grad_w_out", bf16_copy=True)
    dqkv, rv_wgu, rv_wd, rv_wout = _attn_bwd(
        proj, dcat, cstats, tri_after, tri_incl, n_seq, seq,
        [gwgu_b.reshape(2, N_DEV, ff_rows, d), gwd_b.reshape(1, N_DEV, ff_rows, d),
         gwout_b.reshape(1, N_DEV, out_rows, d)])
    dproj, gw_pool, gs_pool = _pool_bwd(proj, dcat, w_pool[0], pool_scale, dqkv, n_seq, seq)
    pad_d = lambda v: jnp.pad(v, ((0, 0), (0, d - v.shape[1])))
    n_gw = gw_pool.size // d
    early = jnp.concatenate(
        [gg_mix_post, gg_ffn_pre, gg_ffn_post, pad_d(gs_pool), pad_d(loss_sum), jnp.zeros((3, d), F32),
         gw_pool.reshape(n_gw, d),
         jnp.concatenate([dgate_m, dshift_f, dscale_f, dgate_f], axis=1).reshape(n_seq * 4, d)], axis=0)
    gwin, gwin_b, early_g = _matmul(
        dproj, h1, "tn", F32, d // 2, d, t, "grad_w_in", bf16_copy=True,
        ag=([early], [jax.ShapeDtypeStruct((N_DEV,) + early.shape, F32)], [(0, ())]))
    grad_x, dshift_m, dscale_m, gg_mix_pre, rv_win = _matmul_rows(
        dproj, win_full.reshape(4, d // 2, d), ROW_TILE, seq, "dh1_bwd_pre", _bwd_pre_epilogue,
        [dx1, x2, g_mix_pre, mod], ["tok", "tok", "vec", "mod"],
        [tok_f32, seq_sds, seq_sds, vec_sds], ["tok", "seq", "seq", "vec"],
        rs_sends=[gwin_b.reshape(1, N_DEV, in_rows, d)])

    r_wgu = _rs_final(gwgu.reshape(2, N_DEV, ff_rows, d), rv_wgu, "rs_final_gate_up")
    r_wd = _rs_final(gwd.reshape(1, N_DEV, ff_rows, d), rv_wd, "rs_final_down")
    r_wout = _rs_final(gwout.reshape(1, N_DEV, out_rows, d), rv_wout, "rs_final_out")
    r_win = _rs_final(gwin.reshape(1, N_DEV, in_rows, d), rv_win, "rs_final_in")
    grad_w_in = r_win[0].T
    grad_w_out = r_wout[0]
    grad_w_down = r_wd[0]

    late = jnp.concatenate([gg_mix_pre, dshift_m.reshape(n_seq, d), dscale_m.reshape(n_seq, d),
                            jnp.zeros((8 - 1 - 2 * n_seq, d), F32)], axis=0)
    (late_g,) = _all_gather([late], [jax.ShapeDtypeStruct((N_DEV,) + late.shape, F32)], [(0, ())], "ag_late")
    loss = jnp.sum(early_g[:, 4, 0]) * (0.5 / d)
    dmod_all = jnp.concatenate(
        [late_g[:, 1:1 + n_seq, None, :], late_g[:, 1 + n_seq:1 + 2 * n_seq, None, :],
         early_g[:, 8 + n_gw:, :].reshape(N_DEV, n_seq, 4, d)], axis=2).reshape(N_DEV * n_seq, N_MOD * d)
    dmod_cols = lax.dynamic_slice_in_dim(dmod_all, me * cond_cols, cond_cols, axis=1)
    grad_w_cond, grad_b_cond = _cond_bwd(c_all, dmod_all, dmod_cols)

    small_ws = [g_mix_pre, g_mix_post, g_ffn_pre, g_ffn_post, pool_scale, w_pool.reshape(-1, POOL_GROUP_DIM)]
    small_ms = [m_g_mix_pre, m_g_mix_post, m_g_ffn_pre, m_g_ffn_post, m_pool_scale, m_w_pool.reshape(-1, POOL_GROUP_DIM)]
    small_vs = [v_g_mix_pre, v_g_mix_post, v_g_ffn_pre, v_g_ffn_post, v_pool_scale, v_w_pool.reshape(-1, POOL_GROUP_DIM)]
    small_gparts = [late_g[:, 0:1, :], early_g[:, 0:1, :], early_g[:, 1:2, :], early_g[:, 2:3, :],
                    early_g[:, 3:4, :pool_scale.shape[1]],
                    early_g[:, 8:8 + n_gw, :].reshape(N_DEV, -1, POOL_GROUP_DIM)]
    so = _adamw_small(small_ws, small_gparts, small_ms, small_vs, "adamw_small")
    ns = len(small_ws)
    sg, sdl, sm, sv = so[:ns], so[ns:2 * ns], so[2 * ns:3 * ns], so[3 * ns:]
    pool_shape = w_pool.shape
    fix = lambda lst: [lst[0], lst[1], lst[2], lst[3], lst[4], lst[5].reshape(pool_shape)]
    sg, sdl, sm, sv = fix(sg), fix(sdl), fix(sm), fix(sv)

    def big(w, g, m, v, rows, name):
        dl, nm, nv = _adamw(w[0], g, m[0], v[0], rows, name)
        return g[None], dl[None], nm[None], nv[None]

    o_cond = big(w_cond, grad_w_cond, m_w_cond, v_w_cond, 256, "adamw_w_cond")
    o_bcond = _adamw(b_cond, grad_b_cond, m_b_cond, v_b_cond, 1, "adamw_b_cond")
    o_bcond = (grad_b_cond,) + tuple(o_bcond)
    o_in = big(w_in, grad_w_in, m_w_in, v_w_in, 256, "adamw_w_in")
    o_out = big(w_out, grad_w_out, m_w_out, v_w_out, out_rows, "adamw_w_out")
    def big_t(w, g_t, m, v, name):
        outs = _adamw(w[0].T, g_t, m[0].T, v[0].T, g_t.shape[0], name)
        return tuple(o.T[None] for o in (g_t,) + tuple(outs))

    o_gate = big_t(w_gate, r_wgu[0], m_w_gate, v_w_gate, "adamw_w_gate")
    o_up = big_t(w_up, r_wgu[1], m_w_up, v_w_up, "adamw_w_up")
    o_down = big(w_down, grad_w_down, m_w_down, v_w_down, ff_rows, "adamw_w_down")

    def pick(k):
        small_k = [sg, sdl, sm, sv][k]
        return [o_cond[k], o_bcond[k], small_k[0], small_k[1], o_in[k], small_k[5], small_k[4], o_out[k],
                small_k[2], small_k[3], o_gate[k], o_up[k], o_down[k]]

    return (loss, grad_x.reshape(n_seq, seq, d), *pick(0), *pick(1), *pick(2), *pick(3))
```

```python
import functools
import math

import jax
import jax.numpy as jnp
from jax import lax
from jax.experimental import pallas as pl
from jax.experimental.pallas import tpu as pltpu

F32 = jnp.float32
BF16 = jnp.bfloat16
MESH = pl.DeviceIdType.MESH

N_DEV = 8
HEAD_DIM = 64
LANES = 128
POOL_WINDOWS = (2, 4, 8, 16)
POOL_GROUP_DIM = 128
N_MOD = 6
EPS = 1e-6
ATT_TILE = 256
ATT_PAIRS = 2
VMEM_LIMIT = 56 * 1024 * 1024

ADAM_LR = 0.001
ADAM_B1 = 0.9
ADAM_B2 = 0.999
ADAM_EPS = 1e-08
ADAM_WD = 0.01
ADAM_STEP = 10


def _params(**kw):
    return pltpu.CompilerParams(vmem_limit_bytes=VMEM_LIMIT, **kw)


def _dot_nn(a, b):
    return jnp.dot(a, b, preferred_element_type=F32)


def _dot_nt(a, b):
    return lax.dot_general(a, b, (((1,), (1,)), ((), ())), preferred_element_type=F32)


def _dot_tn(a, b):
    return lax.dot_general(a, b, (((0,), (0,)), ((), ())), preferred_element_type=F32)


def _mesh_pos():
    return lax.axis_index("x"), lax.axis_index("y"), lax.axis_index("c")


def _ag_phases(dests, src, outs, send_sems, recv_sems, local_sems):
    n = len(src)
    x, y, c = _mesh_pos()
    me, sibling = (x, y, c), (x, y, 1 - c)
    chips = [(1 - x, y), (x, 1 - y), (1 - x, 1 - y)]

    def slot(i, dev):
        oi, prefix = dests[i]
        px, py, pc = dev
        return outs[oi].at[prefix + (4 * px + 2 * py + pc,)]

    def copy(i, k, block, to, from_src=False):
        return pltpu.make_async_remote_copy(
            src_ref=src[i] if from_src else slot(i, block), dst_ref=slot(i, block),
            send_sem=send_sems.at[i, k], recv_sem=recv_sems.at[i, k],
            device_id=to, device_id_type=MESH)

    def mine(i):
        return pltpu.make_async_copy(src[i], slot(i, me), local_sems.at[i])

    def first(i):
        return [copy(i, 0, me, sibling, from_src=True)] + [
            copy(i, 1 + j, me, (*chip, c), from_src=True) for j, chip in enumerate(chips)]

    def passed(i, j):
        return copy(i, 4 + j, (*chips[j], c), sibling)

    def start():
        for i in range(n):
            mine(i).start()
        for i in range(n):
            for cp in first(i):
                cp.start()

    def forward():
        for j, chip in enumerate(chips):
            for i in range(n):
                copy(i, 1 + j, (*chip, c), me).wait_recv()
                passed(i, j).start()

    def finish():
        for i in range(n):
            copy(i, 0, sibling, me).wait_recv()
            for j, chip in enumerate(chips):
                copy(i, 4 + j, (*chip, 1 - c), me).wait_recv()
        for i in range(n):
            for cp in first(i) + [passed(i, j) for j in range(3)]:
                cp.wait_send()
            mine(i).wait()

    return start, forward, finish


def _ag_scratch(n):
    return [pltpu.SemaphoreType.DMA((n, 7)), pltpu.SemaphoreType.DMA((n, 7)), pltpu.SemaphoreType.DMA((n,))]


def _all_gather(srcs, out_shapes, dests, name):
    n = len(srcs)

    def body(*refs):
        src = refs[:n]
        outs = refs[n:n + len(out_shapes)]
        start, forward, finish = _ag_phases(dests, src, outs, *refs[n + len(out_shapes):])
        start()
        forward()
        finish()

    any_spec = pl.BlockSpec(memory_space=pl.ANY)
    return pl.pallas_call(
        body, name=name,
        out_shape=tuple(out_shapes),
        in_specs=[any_spec] * n,
        out_specs=tuple([any_spec] * len(out_shapes)),
        scratch_shapes=_ag_scratch(n),
    )(*srcs)


def _rs_phases(shapes, src, dst, send_sems, recv_sems):
    x, y, c = _mesh_pos()

    def copies():
        out = []
        n = 0
        for i, shp in enumerate(shapes):
            for m in range(shp[0]):
                for k in range(1, N_DEV):
                    px, py, pc = x ^ (k >> 2), y ^ ((k >> 1) & 1), c ^ (k & 1)
                    out.append(pltpu.make_async_remote_copy(
                        src_ref=src[i].at[m, 4 * px + 2 * py + pc], dst_ref=dst[i].at[m, k - 1],
                        send_sem=send_sems.at[n], recv_sem=recv_sems.at[n],
                        device_id=(px, py, pc), device_id_type=MESH))
                    n += 1
        return out

    def start():
        for cp in copies():
            cp.start()

    def finish():
        for cp in copies():
            cp.wait_send()
        for cp in copies():
            cp.wait_recv()

    return start, finish


def _rs_out(sends):
    return [jax.ShapeDtypeStruct((s.shape[0], N_DEV - 1) + s.shape[2:], s.dtype) for s in sends]


def _rs_scratch(sends):
    total = sum((N_DEV - 1) * s.shape[0] for s in sends)
    return [pltpu.SemaphoreType.DMA((total,)), pltpu.SemaphoreType.DMA((total,))]


def _rs_final_adamw(mine, recv, slab, w, m, v, name, transpose=False):
    _, _, r, cdim = mine.shape
    x, y, c = _mesh_pos()
    me = jnp.reshape(4 * x + 2 * y + c, (1,)).astype(jnp.int32)

    def body(me_ref, p_ref, r_ref, w_ref, m_ref, v_ref, g_ref, d_ref, nm_ref, nv_ref):
        del me_ref
        g = p_ref[...]
        for k in range(N_DEV - 1):
            g = g + r_ref[k].astype(F32)
        if transpose:
            g = g.T
        g_ref[...] = g
        d_ref[...], nm_ref[...], nv_ref[...] = _adamw_math(w_ref[...], g, m_ref[...], v_ref[...])

    full = pl.BlockSpec(w.shape, lambda i, s: (0, 0))
    sds = jax.ShapeDtypeStruct(w.shape, F32)
    return pl.pallas_call(
        body, name=name, out_shape=(sds, sds, sds, sds),
        grid_spec=pltpu.PrefetchScalarGridSpec(
            num_scalar_prefetch=1, grid=(1,),
            in_specs=[pl.BlockSpec((None, None, r, cdim), lambda i, s: (slab, s[0], 0, 0)),
                      pl.BlockSpec((None, N_DEV - 1, r, cdim), lambda i, s: (slab, 0, 0, 0)), full, full, full],
            out_specs=(full, full, full, full)),
        compiler_params=_params(),
    )(me, mine, recv, w, m, v)


def _matmul(a, b, mode, out_dtype, tm, tn, tk, name, bf16_copy=False, rs_sends=(), ag=None):
    ga = a.shape[0] if a.ndim == 3 else None
    gb = b.shape[0] if b.ndim == 3 else None
    a2, b2 = a.shape[-2:], b.shape[-2:]
    if mode == "nn":
        (m, k), n = a2, b2[1]
    elif mode == "nt":
        (m, k), n = a2, b2[0]
    else:
        (k, m), n = a2, b2[1]
    assert m % tm == 0 and n % tn == 0 and k % tk == 0, (name, m, n, k)
    nk = k // tk
    g_n = ga or 1
    batch_out = mode == "tn" and ga is not None
    n_red = nk if batch_out else nk * g_n
    dot = {"nn": _dot_nn, "nt": _dot_nt, "tn": _dot_tn}[mode]
    acc_in_out = out_dtype == F32

    n_rs = len(rs_sends)
    rs_shapes = [r.shape for r in rs_sends]
    ag_srcs, ag_out_shapes, ag_dests = ag if ag is not None else ((), (), ())
    n_ag, n_ag_out = len(ag_srcs), len(ag_out_shapes)
    n_out = 2 if bf16_copy else 1
    assert not bf16_copy or acc_in_out
    assert not (n_rs and n_ag)

    def body(a_ref, b_ref, *rest):
        rs_src, rest = rest[:n_rs], rest[n_rs:]
        ag_src, rest = rest[:n_ag], rest[n_ag:]
        o_ref = rest[0]
        copy_ref = rest[1] if bf16_copy else None
        rs_dst, rest = rest[n_out:n_out + n_rs], rest[n_out + n_rs:]
        ag_out, scratch = rest[:n_ag_out], rest[n_ag_out:]
        first = functools.reduce(jnp.logical_and, [pl.program_id(ax) == 0 for ax in range(4)])
        last = functools.reduce(jnp.logical_and, [pl.program_id(ax) == grid[ax] - 1 for ax in range(4)])
        if n_rs:
            rs_start, rs_finish = _rs_phases(rs_shapes, rs_src, rs_dst, *scratch[-2:])
            pl.when(first)(rs_start)
        if n_ag:
            ag_start, ag_forward, ag_finish = _ag_phases(ag_dests, ag_src, ag_out, *scratch[-3:])
            pl.when(first)(ag_start)
        p = dot(a_ref[...], b_ref[...])
        kk = pl.program_id(3) if batch_out else pl.program_id(2) * nk + pl.program_id(3)
        if n_red == 1:
            o_ref[...] = p.astype(out_dtype)
            if bf16_copy:
                copy_ref[...] = p.astype(BF16)
        else:
            acc = o_ref if acc_in_out else scratch[0]

            @pl.when(kk == 0)
            def _():
                acc[...] = p

            @pl.when(kk > 0)
            def _():
                acc[...] += p

            @pl.when(kk == n_red - 1)
            def _():
                if not acc_in_out:
                    o_ref[...] = acc[...].astype(out_dtype)
                if bf16_copy:
                    copy_ref[...] = acc[...].astype(BF16)

        if n_rs:
            pl.when(last)(rs_finish)
        if n_ag:
            @pl.when(last)
            def _():
                ag_forward()
                ag_finish()

    def order(ids):
        return ids if batch_out else (ids[2], ids[0], ids[1], ids[3])

    def a_idx(*ids):
        g, i, j, kq = order(ids)
        blk = {"nn": (i, kq), "nt": (i, kq), "tn": (kq, i)}[mode]
        return (g,) + blk if ga is not None else blk

    def b_idx(*ids):
        g, i, j, kq = order(ids)
        blk = {"nn": (kq, j), "nt": (j, kq), "tn": (kq, j)}[mode]
        return (g,) + blk if gb is not None else blk

    def o_idx(*ids):
        g, i, j, kq = order(ids)
        return (g, i, j) if batch_out else (i, j)

    a_blk = {"nn": (tm, tk), "nt": (tm, tk), "tn": (tk, tm)}[mode]
    b_blk = {"nn": (tk, tn), "nt": (tn, tk), "tn": (tk, tn)}[mode]
    if ga is not None:
        a_blk = (None,) + a_blk
    if gb is not None:
        b_blk = (None,) + b_blk
    if batch_out:
        out_shape = jax.ShapeDtypeStruct((g_n, m, n), out_dtype)
        o_blk = (None, tm, tn)
        grid = (g_n, m // tm, n // tn, nk)
    else:
        out_shape = jax.ShapeDtypeStruct((m, n), out_dtype)
        o_blk = (tm, tn)
        grid = (m // tm, n // tn, g_n, nk)
    scratch = [] if (acc_in_out or n_red == 1) else [pltpu.VMEM((tm, tn), F32)]
    any_spec = pl.BlockSpec(memory_space=pl.ANY)
    out_shapes = [out_shape] + ([jax.ShapeDtypeStruct(out_shape.shape, BF16)] if bf16_copy else [])
    res = pl.pallas_call(
        body, name=name, out_shape=tuple(out_shapes + _rs_out(rs_sends) + list(ag_out_shapes)), grid=grid,
        in_specs=[pl.BlockSpec(a_blk, a_idx), pl.BlockSpec(b_blk, b_idx)] + [any_spec] * (n_rs + n_ag),
        out_specs=tuple([pl.BlockSpec(o_blk, o_idx)] * n_out + [any_spec] * (n_rs + n_ag_out)),
        scratch_shapes=scratch + (_rs_scratch(rs_sends) if n_rs else []) + (_ag_scratch(n_ag) if n_ag else []),
        compiler_params=_params(),
    )(a, b, *rs_sends, *ag_srcs)
    return res if len(res) > 1 else res[0]


EW_TILE = 256
ROW_TILE = 512
EPILOGUE_CHUNKS = 8
MXU_WIDTH = 256


def _rms(v):
    return lax.rsqrt(jnp.mean(v * v, axis=-1, keepdims=True) + EPS)


def _rms_bwd(dhat, vh, r):
    return r * (dhat - vh * jnp.mean(dhat * vh, axis=-1, keepdims=True))


def _tok_spec(tm, d):
    return pl.BlockSpec((tm, d), lambda i: (i, 0))


def _vec_spec(d):
    return pl.BlockSpec((1, d), lambda i: (0, 0))


def _mod_spec(tiles_per_seq, d):
    return pl.BlockSpec((None, N_MOD, d), lambda i: (i // tiles_per_seq, 0, 0))


def _seq_acc_spec(tiles_per_seq, d):
    return pl.BlockSpec((None, 1, d), lambda i: (i // tiles_per_seq, 0, 0))


def _acc(ref, val, first):
    if first is False:
        ref[...] += val
        return

    @pl.when(first)
    def _():
        ref[...] = val

    @pl.when(jnp.logical_not(first))
    def _():
        ref[...] += val


def _colsum(v):
    return jnp.sum(v, axis=0, keepdims=True)


def _pre_mix(x2, g_pre, mod, seq, ag_srcs, ag_out_shapes, ag_dests):
    t, d = x2.shape
    tm = EW_TILE
    n_steps = t // tm
    n_ag, n_ag_out = len(ag_srcs), len(ag_out_shapes)

    def body(x_ref, g_ref, mod_ref, *rest):
        ag_src, h_ref = rest[:n_ag], rest[n_ag]
        ag_out, sems = rest[n_ag + 1:n_ag + 1 + n_ag_out], rest[n_ag + 1 + n_ag_out:]
        ag_start, ag_forward, ag_finish = _ag_phases(ag_dests, ag_src, ag_out, *sems)
        step = pl.program_id(0)
        pl.when(step == 0)(ag_start)
        xv = x_ref[...]
        n = xv * _rms(xv) * g_ref[...]
        h_ref[...] = (n * (1.0 + mod_ref[1:2, :]) + mod_ref[0:1, :]).astype(BF16)

        @pl.when(step == n_steps - 1)
        def _():
            ag_forward()
            ag_finish()

    any_spec = pl.BlockSpec(memory_space=pl.ANY)
    return pl.pallas_call(
        body, name="pre_mix", out_shape=(jax.ShapeDtypeStruct((t, d), BF16), *ag_out_shapes), grid=(n_steps,),
        in_specs=[_tok_spec(tm, d), _vec_spec(d), _mod_spec(seq // tm, d)] + [any_spec] * n_ag,
        out_specs=(_tok_spec(tm, d), *([any_spec] * n_ag_out)),
        scratch_shapes=_ag_scratch(n_ag), compiler_params=_params(),
    )(x2, g_pre, mod, *ag_srcs)


def _matmul_rows(a, b, tm, seq, name, epilogue, ep_in, ep_in_kinds, ep_out, ep_out_kinds, rs_sends=()):
    g_n = a.shape[0] if a.ndim == 3 else None
    (m, k), n = a.shape[-2:], b.shape[-1]
    tps = seq // tm
    n_i = m // tm
    n_rs = len(rs_sends)
    rs_shapes = [r.shape for r in rs_sends]
    n_in, n_out = len(ep_in), len(ep_out)
    n_cols = n // MXU_WIDTH
    rc, cw = tm // EPILOGUE_CHUNKS, n // n_cols

    def prev(i):
        return jnp.maximum(i - 1, 0)

    def spec(kind):
        return {"tok": pl.BlockSpec((tm, n), lambda i: (prev(i), 0)),
                "vec": pl.BlockSpec((1, n), lambda i: (0, 0)),
                "mod": pl.BlockSpec((None, N_MOD, n), lambda i: (prev(i) // tps, 0, 0)),
                "seq": pl.BlockSpec((None, 1, n), lambda i: (prev(i) // tps, 0, 0)),
                "loss": pl.BlockSpec((1, LANES), lambda i: (0, 0))}[kind]

    def body(a_ref, b_ref, *rest):
        in_refs, rest = rest[:n_in], rest[n_in:]
        rs_src, rest = rest[:n_rs], rest[n_rs:]
        out_refs, rest = rest[:n_out], rest[n_out:]
        rs_dst, rest = rest[:n_rs], rest[n_rs:]
        fin = rest[0]
        i = pl.program_id(0)
        if n_rs:
            rs_start, rs_finish = _rs_phases(rs_shapes, rs_src, rs_dst, *rest[1:])
            pl.when(i == 0)(rs_start)

        def product(cols):
            if g_n is None:
                return _dot_nn(a_ref[...], b_ref[:, cols])
            p = _dot_nn(a_ref[0], b_ref[0, :, cols])
            for g in range(1, g_n):
                p = p + _dot_nn(a_ref[g], b_ref[g, :, cols])
            return p

        def step(with_epilogue, with_matmul):
            parts = []
            for c in range(EPILOGUE_CHUNKS):
                if with_epilogue:
                    rows = pl.ds(c * rc, rc)
                    epilogue(fin[rows, :], i - 1, tps, in_refs, out_refs, rows, c)
                while with_matmul and len(parts) < (c + 1) * n_cols // EPILOGUE_CHUNKS:
                    cols = slice(len(parts) * cw, (len(parts) + 1) * cw)
                    parts.append((cols, product(cols)))
            for cols, v in parts:
                fin[:, cols] = v

        pl.when(i == 0)(functools.partial(step, False, True))
        pl.when(jnp.logical_and(i > 0, i < n_i))(functools.partial(step, True, True))
        pl.when(i == n_i)(functools.partial(step, True, False))

        if n_rs:
            pl.when(i == n_i)(rs_finish)

    def row(i):
        return jnp.minimum(i, n_i - 1)

    if g_n is None:
        a_spec = pl.BlockSpec((tm, k), lambda i: (row(i), 0))
        b_spec = pl.BlockSpec(b.shape, lambda i: (0, 0), pipeline_mode=pl.Buffered(1))
    else:
        a_spec = pl.BlockSpec((g_n, tm, k), lambda i: (0, row(i), 0))
        b_spec = pl.BlockSpec(b.shape, lambda i: (0, 0, 0), pipeline_mode=pl.Buffered(1))
    any_spec = pl.BlockSpec(memory_space=pl.ANY)
    res = pl.pallas_call(
        body, name=name, grid=(n_i + 1,), out_shape=tuple(list(ep_out) + _rs_out(rs_sends)),
        in_specs=[a_spec, b_spec] + [spec(kd) for kd in ep_in_kinds] + [any_spec] * n_rs,
        out_specs=tuple([spec(kd) for kd in ep_out_kinds] + [any_spec] * n_rs),
        scratch_shapes=[pltpu.VMEM((tm, n), F32)] + (_rs_scratch(rs_sends) if n_rs else []),
        compiler_params=_params(),
    )(a, b, *ep_in, *rs_sends)
    return res


def _first(cond, chunk):
    return cond if chunk == 0 else False


def _mid_epilogue(mv, i, tps, in_refs, out_refs, rows, chunk):
    x_ref, gpost_ref, gpre_ref, mod_ref = in_refs
    mix_ref, x1_ref, h2_ref = out_refs
    mix_ref[rows, :] = mv
    x1 = x_ref[rows, :] + mod_ref[2:3, :] * (mv * _rms(mv) * gpost_ref[...])
    x1_ref[rows, :] = x1
    n = x1 * _rms(x1) * gpre_ref[...]
    h2_ref[rows, :] = (n * (1.0 + mod_ref[4:5, :]) + mod_ref[3:4, :]).astype(BF16)


def _post_epilogue(fv, i, tps, in_refs, out_refs, rows, chunk):
    x1_ref, tgt_ref, g_ref, mod_ref = in_refs
    loss_ref, dy_ref, df_ref, dgate_ref, gg_ref = out_refs
    d = fv.shape[1]
    r = _rms(fv)
    fh = fv * r
    nf = fh * g_ref[...]
    gate = mod_ref[5:6, :]
    err = x1_ref[rows, :] + gate * nf - tgt_ref[rows, :]
    _acc(loss_ref, jnp.sum(_colsum(err * err), axis=1, keepdims=True) * jnp.ones((1, LANES), F32),
         _first(i == 0, chunk))
    dy = err * (1.0 / d)
    dy_ref[rows, :] = dy
    _acc(dgate_ref, _colsum(dy * nf), _first(i % tps == 0, chunk))
    dn = dy * gate
    _acc(gg_ref, _colsum(dn * fh), _first(i == 0, chunk))
    df_ref[rows, :] = _rms_bwd(dn * g_ref[...], fh, r).astype(BF16)


def _bwd_mid_epilogue(dh, i, tps, in_refs, out_refs, rows, chunk):
    dy_ref, x1_ref, mix_ref, gpre_ref, gpost_ref, mod_ref = in_refs
    dx1_ref, dmix_ref, dshift_ref, dscale_ref, dgate_ref, ggpre_ref, ggpost_ref = out_refs
    seq_first, first = _first(i % tps == 0, chunk), _first(i == 0, chunk)
    x1 = x1_ref[rows, :]
    r = _rms(x1)
    xh = x1 * r
    gpre = gpre_ref[...]
    _acc(dshift_ref, _colsum(dh), seq_first)
    _acc(dscale_ref, _colsum(dh * xh * gpre), seq_first)
    dn = dh * (1.0 + mod_ref[4:5, :])
    _acc(ggpre_ref, _colsum(dn * xh), first)
    dx1 = dy_ref[rows, :] + _rms_bwd(dn * gpre, xh, r)
    dx1_ref[rows, :] = dx1
    mv = mix_ref[rows, :]
    rm = _rms(mv)
    mh = mv * rm
    gpost = gpost_ref[...]
    _acc(dgate_ref, _colsum(dx1 * mh * gpost), seq_first)
    dnm = dx1 * mod_ref[2:3, :]
    _acc(ggpost_ref, _colsum(dnm * mh), first)
    dmix_ref[rows, :] = _rms_bwd(dnm * gpost, mh, rm).astype(BF16)


def _bwd_pre_epilogue(dh, i, tps, in_refs, out_refs, rows, chunk):
    dx1_ref, x_ref, g_ref, mod_ref = in_refs
    gx_ref, dshift_ref, dscale_ref, gg_ref = out_refs
    seq_first = _first(i % tps == 0, chunk)
    xv = x_ref[rows, :]
    r = _rms(xv)
    xh = xv * r
    g = g_ref[...]
    _acc(dshift_ref, _colsum(dh), seq_first)
    _acc(dscale_ref, _colsum(dh * xh * g), seq_first)
    dn = dh * (1.0 + mod_ref[1:2, :])
    _acc(gg_ref, _colsum(dn * xh), _first(i == 0, chunk))
    gx_ref[rows, :] = dx1_ref[rows, :] + _rms_bwd(dn * g, xh, r)


def _ffn_up(h2, wgu, tm, tn):
    t, d = h2.shape
    f = wgu.shape[1]

    def body(h_ref, w_ref, gu_ref, act_ref):
        h = h_ref[...]
        g = _dot_nt(h, w_ref[0])
        u = _dot_nt(h, w_ref[1])
        gu_ref[0] = g.astype(BF16)
        gu_ref[1] = u.astype(BF16)
        act_ref[...] = (g * jax.nn.sigmoid(g) * u).astype(BF16)

    return pl.pallas_call(
        body, name="ffn_up", grid=(f // tn, t // tm),
        out_shape=(jax.ShapeDtypeStruct((2, t, f), BF16), jax.ShapeDtypeStruct((t, f), BF16)),
        in_specs=[pl.BlockSpec((tm, d), lambda j, i: (i, 0)), pl.BlockSpec((2, tn, d), lambda j, i: (0, j, 0))],
        out_specs=(pl.BlockSpec((2, tm, tn), lambda j, i: (0, i, j)), pl.BlockSpec((tm, tn), lambda j, i: (i, j))),
        compiler_params=_params(),
    )(h2, wgu)


def _ffn_act_bwd(df, wd, gu, tm, tn):
    t, d = df.shape
    f = wd.shape[0]

    def body(df_ref, w_ref, gu_ref, dgu_ref):
        da = _dot_nt(df_ref[...], w_ref[...])
        g = gu_ref[0].astype(F32)
        u = gu_ref[1].astype(F32)
        s = jax.nn.sigmoid(g)
        silu = g * s
        dgu_ref[0] = (da * u * (s + silu * (1.0 - s))).astype(BF16)
        dgu_ref[1] = (da * silu).astype(BF16)

    return pl.pallas_call(
        body, name="ffn_act_bwd", grid=(f // tn, t // tm),
        out_shape=jax.ShapeDtypeStruct((2, t, f), BF16),
        in_specs=[pl.BlockSpec((tm, d), lambda j, i: (i, 0)), pl.BlockSpec((tn, d), lambda j, i: (j, 0)),
                  pl.BlockSpec((2, tm, tn), lambda j, i: (0, i, j))],
        out_specs=pl.BlockSpec((2, tm, tn), lambda j, i: (0, i, j)),
        compiler_params=_params(),
    )(df, wd, gu)


SIGN_BIT = 0x80000000
Q_SCALE = 1.0 / math.sqrt(HEAD_DIM)


def _softplus(z):
    neg_abs = lax.bitcast_convert_type(lax.bitcast_convert_type(z, jnp.uint32) | jnp.uint32(SIGN_BIT), F32)
    return jnp.maximum(z, 0.0) + jnp.log(1.0 + jnp.exp(neg_abs))


def _hi_lo(v):
    hi = v.astype(BF16)
    return jnp.concatenate([hi, (v - hi.astype(F32)).astype(BF16)], axis=1)


def _emit_skewed(chains, lag=1):
    for t in range(max(len(ch) for ch in chains) + lag * (len(chains) - 1)):
        for c, ch in enumerate(chains):
            if 0 <= t - lag * c < len(ch):
                ch[t - lag * c]()


def _fwd_chain(blk, qs, k_ref, v_ref, c0, kb, cols, mask, ntri, lane, tq):
    st = {}

    def scores():
        st["z"] = _dot_nt(qs, k_ref[pl.ds(c0, tq), cols])

    def soft():
        sp = _softplus(st["z"])
        if mask is not None:
            sp = jnp.where(mask, sp, 0.0)
        st["parts"] = _hi_lo(sp)
        st["cur"] = blk["cur"]
        blk["cm"] = jnp.where(lane == kb, blk["cur"], blk["cm"])
        blk["cur"] = blk["cur"] - jnp.sum(sp, axis=1, keepdims=True)

    def sums():
        st["s"] = _dot_nn(st["parts"], ntri)

    def weights():
        w = jnp.exp(st["z"] + st["s"] + st["cur"])
        if mask is not None:
            w = jnp.where(mask, w, 0.0)
        st["w"] = w.astype(BF16)

    def out():
        p = _dot_nn(st["w"], v_ref[pl.ds(c0, tq), cols])
        blk["pv"] = p if blk["pv"] is None else blk["pv"] + p

    return [scores, soft, sums, weights, out]


def _bwd_chain(blk, qs, dos, cs, k_ref, v_ref, dk_ref, dv_ref, c0, kb, cols, mask, ntri, tri_i, lane, tq):
    st = {}

    def scores():
        st["z"] = _dot_nt(qs, k_ref[pl.ds(c0, tq), cols])
        st["dw"] = _dot_nt(dos, v_ref[pl.ds(c0, tq), cols])

    def soft():
        sp = _softplus(st["z"])
        if mask is not None:
            sp = jnp.where(mask, sp, 0.0)
        st["sp"] = sp
        st["parts"] = _hi_lo(sp)
        st["cur"] = jnp.sum(jnp.where(lane == kb, cs, 0.0), axis=1, keepdims=True)

    def sums():
        st["s"] = _dot_nn(st["parts"], ntri)

    def weights():
        w = jnp.exp(st["z"] + st["s"] + st["cur"])
        if mask is not None:
            w = jnp.where(mask, w, 0.0)
        ee = w * st["dw"]
        st["w"], st["ee"], st["ec"] = w.astype(BF16), ee, blk["ec"]
        blk["ec"] = blk["ec"] + jnp.sum(ee, axis=1, keepdims=True)

    def prefix():
        st["einc"] = _dot_nn(st["ee"].astype(BF16), tri_i)

    def dz():
        v = st["ee"] - jnp.exp(st["z"] - st["sp"]) * (st["einc"] + st["ec"])
        if mask is not None:
            v = jnp.where(mask, v, 0.0)
        st["dz"] = v.astype(BF16)

    def grads():
        p = _dot_nn(st["dz"], k_ref[pl.ds(c0, tq), cols])
        blk["dq"] = p if blk["dq"] is None else blk["dq"] + p
        dk_ref[pl.ds(c0, tq), :] += _dot_tn(st["dz"], qs)
        dv_ref[pl.ds(c0, tq), :] += _dot_tn(st["w"], dos)

    return [scores, soft, sums, weights, prefix, dz, grads]


def _stack_heads(v, lane, scale=None):
    if scale is not None:
        v = v * jnp.asarray(scale, v.dtype)
    zero = jnp.zeros_like(v)
    return jnp.concatenate([jnp.where(lane < HEAD_DIM, v, zero), jnp.where(lane >= HEAD_DIM, v, zero)], axis=0)


def _diag_mask(tq):
    row = lax.broadcasted_iota(jnp.int32, (2 * tq, tq), 0)
    col = lax.broadcasted_iota(jnp.int32, (2 * tq, tq), 1)
    return col < jnp.where(row >= tq, row - tq, row)


def _attn_fwd(proj, tri_after, n_seq, seq, ag_srcs, ag_out_shapes, ag_dests):
    t = proj.shape[0]
    tq = ATT_TILE
    npp = ATT_PAIRS
    n_blk = (proj.shape[1] // 4) // (npp * LANES)
    n_ag, n_ag_out = len(ag_srcs), len(ag_out_shapes)
    n_steps = n_seq * n_blk

    def body(q_ref, k_ref, v_ref, tri_ref, *rest):
        ag_src, rest = rest[:n_ag], rest[n_ag:]
        o_ref, cs_ref = rest[:2]
        ag_out, rest = rest[2:2 + n_ag_out], rest[2 + n_ag_out:]
        oacc, cmat, carry = rest[:3]
        ag_start, ag_forward, ag_finish = _ag_phases(ag_dests, ag_src, ag_out, *rest[3:])
        step = pl.program_id(0) * n_blk + pl.program_id(1)
        pl.when(step == 0)(ag_start)
        pl.when(step == (3 * n_steps) // 4)(ag_forward)
        lane = lax.broadcasted_iota(jnp.int32, (1, LANES), 1)
        ntri = tri_ref[...]
        diag = _diag_mask(tq)

        def q_tile(qi, _):
            r0 = pl.multiple_of(qi * tq, tq)
            qs = [_stack_heads(q_ref[pl.ds(r0, tq), pp * LANES:(pp + 1) * LANES], lane, Q_SCALE)
                  for pp in range(npp)]
            carry[...] = jnp.zeros_like(carry)
            cmat[...] = jnp.zeros_like(cmat)
            oacc[...] = jnp.zeros_like(oacc)

            def run_tiles(tiles):
                blocks = [dict(cur=carry[pp], cm=cmat[pp], pv=None) for pp in range(npp)]
                chains = []
                for kb, mask in tiles:
                    c0 = pl.multiple_of(kb * tq, tq)
                    for pp in range(npp):
                        chains.append(_fwd_chain(blocks[pp], qs[pp], k_ref, v_ref, c0, kb,
                                                 slice(pp * LANES, (pp + 1) * LANES), mask, ntri, lane, tq))
                _emit_skewed(chains)
                for pp in range(npp):
                    oacc[pp] += blocks[pp]["pv"]
                    cmat[pp] = blocks[pp]["cm"]
                    carry[pp] = blocks[pp]["cur"]

            odd = qi % 2

            @pl.when(odd == 0)
            def _():
                run_tiles([(qi, diag)])

            @pl.when(odd == 1)
            def _():
                run_tiles([(qi, diag), (qi - 1, None)])

            def pair(j, _):
                kb = qi - 1 - odd - 2 * j
                run_tiles([(kb, None), (kb - 1, None)])
                return 0

            lax.fori_loop(0, qi // 2, pair, 0)
            for pp in range(npp):
                c_off = 2 * pp * LANES
                cs_ref[pl.ds(r0, tq), c_off:c_off + LANES] = cmat[pp, 0:tq, :]
                cs_ref[pl.ds(r0, tq), c_off + LANES:c_off + 2 * LANES] = cmat[pp, tq:2 * tq, :]
                o_ref[pl.ds(r0, tq), pp * LANES:(pp + 1) * LANES] = jnp.where(
                    lane < HEAD_DIM, oacc[pp, 0:tq, :], oacc[pp, tq:2 * tq, :]).astype(BF16)
            return 0

        lax.fori_loop(0, seq // tq, q_tile, 0)
        pl.when(step == n_steps - 1)(ag_finish)

    wid = npp * LANES
    blk = lambda off: pl.BlockSpec((seq, wid), lambda b, p: (b, off + p))
    any_spec = pl.BlockSpec(memory_space=pl.ANY)
    return pl.pallas_call(
        body, name="attn_fwd", grid=(n_seq, n_blk),
        out_shape=(jax.ShapeDtypeStruct((2, t, n_blk * wid), BF16),
                   jax.ShapeDtypeStruct((t, n_blk * 2 * wid), F32), *ag_out_shapes),
        in_specs=[blk(0), blk(n_blk), blk(2 * n_blk), pl.BlockSpec((2 * tq, tq), lambda b, p: (0, 0))]
        + [any_spec] * n_ag,
        out_specs=(pl.BlockSpec((None, seq, wid), lambda b, p: (0, b, p)),
                   pl.BlockSpec((seq, 2 * wid), lambda b, p: (b, p)), *([any_spec] * n_ag_out)),
        scratch_shapes=[pltpu.VMEM((npp, 2 * tq, LANES), F32), pltpu.VMEM((npp, 2 * tq, LANES), F32),
                        pltpu.VMEM((npp, 2 * tq, 1), F32)] + _ag_scratch(n_ag),
        compiler_params=_params(),
    )(proj, proj, proj, tri_after, *ag_srcs)


def _attn_bwd(proj, dcat, cstats, tri_after, tri_incl, n_seq, seq, rs_sends):
    t = proj.shape[0]
    tq = ATT_TILE
    npp = ATT_PAIRS
    width = proj.shape[1] // 4
    n_blk = width // (npp * LANES)
    n_rs = len(rs_sends)
    rs_shapes = [r.shape for r in rs_sends]
    n_steps = n_seq * n_blk

    def body(q_ref, k_ref, v_ref, do_ref, cs_ref, tria_ref, trii_ref, *rest):
        rs_src, rest = rest[:n_rs], rest[n_rs:]
        out_ref = rest[0]
        rs_dst, rest = rest[1:1 + n_rs], rest[1 + n_rs:]
        dq_acc, dk_acc, dv_acc, ecarry = rest[:4]
        rs_start, rs_finish = _rs_phases(rs_shapes, rs_src, rs_dst, *rest[4:])
        step = pl.program_id(0) * n_blk + pl.program_id(1)
        pl.when(step == 0)(rs_start)
        lane = lax.broadcasted_iota(jnp.int32, (1, LANES), 1)
        ntri = tria_ref[...]
        tri_i = trii_ref[...]
        diag = _diag_mask(tq)
        dk_acc[...] = jnp.zeros_like(dk_acc)
        dv_acc[...] = jnp.zeros_like(dv_acc)

        def q_tile(qi, _):
            r0 = pl.multiple_of(qi * tq, tq)
            qs, dos, cs = [], [], []
            for pp in range(npp):
                cols = slice(pp * LANES, (pp + 1) * LANES)
                qs.append(_stack_heads(q_ref[pl.ds(r0, tq), cols], lane, Q_SCALE))
                dos.append(_stack_heads(do_ref[pl.ds(r0, tq), cols], lane))
                c_off = 2 * pp * LANES
                cs.append(jnp.concatenate([cs_ref[pl.ds(r0, tq), c_off:c_off + LANES],
                                           cs_ref[pl.ds(r0, tq), c_off + LANES:c_off + 2 * LANES]], axis=0))
            ecarry[...] = jnp.zeros_like(ecarry)
            dq_acc[...] = jnp.zeros_like(dq_acc)

            def run_tiles(tiles):
                blocks = [dict(ec=ecarry[pp], dq=None) for pp in range(npp)]
                chains = []
                for kb, mask in tiles:
                    c0 = pl.multiple_of(kb * tq, tq)
                    for pp in range(npp):
                        chains.append(_bwd_chain(
                            blocks[pp], qs[pp], dos[pp], cs[pp], k_ref, v_ref, dk_acc.at[pp], dv_acc.at[pp],
                            c0, kb, slice(pp * LANES, (pp + 1) * LANES), mask, ntri, tri_i, lane, tq))
                _emit_skewed(chains)
                for pp in range(npp):
                    dq_acc[pp] += blocks[pp]["dq"]
                    ecarry[pp] = blocks[pp]["ec"]

            def pair(j, _):
                run_tiles([(2 * j, None), (2 * j + 1, None)])
                return 0

            lax.fori_loop(0, qi // 2, pair, 0)
            odd = qi % 2

            @pl.when(odd == 0)
            def _():
                run_tiles([(qi, diag)])

            @pl.when(odd == 1)
            def _():
                run_tiles([(qi - 1, None), (qi, diag)])

            for pp in range(npp):
                dq = jnp.where(lane < HEAD_DIM, dq_acc[pp, 0:tq, :], dq_acc[pp, tq:2 * tq, :])
                out_ref[0, pl.ds(r0, tq), pp * LANES:(pp + 1) * LANES] = (dq * Q_SCALE).astype(BF16)
            return 0

        lax.fori_loop(0, seq // tq, q_tile, 0)
        for pp in range(npp):
            cols = slice(pp * LANES, (pp + 1) * LANES)
            out_ref[1, :, cols] = dk_acc[pp].astype(BF16)
            out_ref[2, :, cols] = dv_acc[pp].astype(BF16)
        pl.when(step == n_steps - 1)(rs_finish)

    wid = npp * LANES
    blk = lambda off: pl.BlockSpec((seq, wid), lambda b, p: (b, off + p))
    tri_spec = pl.BlockSpec((2 * tq, tq), lambda b, p: (0, 0))
    any_spec = pl.BlockSpec(memory_space=pl.ANY)
    return pl.pallas_call(
        body, name="attn_bwd", grid=(n_seq, n_blk),
        out_shape=(jax.ShapeDtypeStruct((4, t, width), BF16), *_rs_out(rs_sends)),
        in_specs=[blk(0), blk(n_blk), blk(2 * n_blk), pl.BlockSpec((seq, wid), lambda b, p: (b, p)),
                  pl.BlockSpec((seq, 2 * wid), lambda b, p: (b, p)), tri_spec,
                  pl.BlockSpec((tq, tq), lambda b, p: (0, 0))] + [any_spec] * n_rs,
        out_specs=(pl.BlockSpec((3, seq, wid), lambda b, p: (0, b, p)), *([any_spec] * n_rs)),
        scratch_shapes=[pltpu.VMEM((npp, 2 * tq, LANES), F32), pltpu.VMEM((npp, seq, LANES), F32),
                        pltpu.VMEM((npp, seq, LANES), F32), pltpu.VMEM((npp, 2 * tq, 1), F32)]
        + _rs_scratch(rs_sends),
        compiler_params=_params(),
    )(proj, proj, proj, dcat, cstats, tri_after, tri_incl, *rs_sends)


def _window_terms(g, rows):
    win = jnp.where(g == 0, POOL_WINDOWS[0], jnp.where(g == 1, POOL_WINDOWS[1],
                    jnp.where(g == 2, POOL_WINDOWS[2], POOL_WINDOWS[3])))
    cnt = jnp.minimum(rows + 1, win).astype(F32)
    return win, cnt


def _window_sum(v, g, rows, forward):
    s_len = v.shape[0]
    sums = []
    s = v
    for step in range(len(POOL_WINDOWS)):
        sh = 1 << step
        if forward:
            shifted = jnp.where(rows < s_len - sh, pltpu.roll(s, s_len - sh, axis=0), 0.0)
        else:
            shifted = jnp.where(rows >= sh, pltpu.roll(s, sh, axis=0), 0.0)
        s = s + shifted
        sums.append(s)
    return jnp.where(g == 0, sums[0], jnp.where(g == 1, sums[1], jnp.where(g == 2, sums[2], sums[3])))


def _pooled(u, g, rows):
    _, cnt = _window_terms(g, rows)
    return _window_sum(u, g, rows, forward=False) / cnt - u


def _pool_fwd(proj, w_pool, pool_scale, cat, n_seq, seq):
    n_grp = len(POOL_WINDOWS)
    u_off = 3 * (proj.shape[1] // 4) // LANES

    def body(u_ref, w_ref, s_ref, alias_ref, o_ref):
        del alias_ref
        g = pl.program_id(1)
        rows = lax.broadcasted_iota(jnp.int32, (seq, 1), 0)
        pooled = _pooled(u_ref[...].astype(F32), g, rows)
        y = _dot_nn(pooled.astype(BF16), w_ref[...].astype(BF16))
        o_ref[...] = (y * s_ref[...]).astype(BF16)

    return pl.pallas_call(
        body, name="pool_fwd", grid=(n_seq, n_grp),
        out_shape=jax.ShapeDtypeStruct(cat.shape, BF16),
        in_specs=[pl.BlockSpec((seq, LANES), lambda b, g: (b, u_off + g)),
                  pl.BlockSpec((None, POOL_GROUP_DIM, POOL_GROUP_DIM), lambda b, g: (g, 0, 0)),
                  pl.BlockSpec((1, POOL_GROUP_DIM), lambda b, g: (0, g)),
                  pl.BlockSpec(memory_space=pl.ANY)],
        out_specs=pl.BlockSpec((None, seq, LANES), lambda b, g: (1, b, g)),
        input_output_aliases={3: 0},
        compiler_params=_params(),
    )(proj, w_pool, pool_scale, cat)


def _pool_bwd(proj, dcat, w_pool, pool_scale, dqkv, n_seq, seq):
    n_grp = len(POOL_WINDOWS)
    width = proj.shape[1] // 4
    u_off = 3 * width // LANES
    dp_off = width // LANES

    def body(u_ref, dp_ref, w_ref, s_ref, alias_ref, du_ref, gw_ref, gs_ref):
        del alias_ref
        g = pl.program_id(0)
        b = pl.program_id(1)
        rows = lax.broadcasted_iota(jnp.int32, (seq, 1), 0)
        pooled = _pooled(u_ref[...].astype(F32), g, rows)
        pb = pooled.astype(BF16)
        wb = w_ref[...].astype(BF16)
        z = _dot_nn(pb, wb)
        dp = dp_ref[...].astype(F32)
        _acc(gs_ref, _colsum(dp * z), b == 0)
        dys = (dp * s_ref[...]).astype(BF16)
        _acc(gw_ref, _dot_tn(pb, dys), b == 0)
        dpooled = _dot_nt(dys, wb)
        _, cnt = _window_terms(g, rows)
        du = _window_sum(dpooled / cnt, g, rows, forward=True) - dpooled
        du_ref[...] = du.astype(BF16)

    t = proj.shape[0]
    return pl.pallas_call(
        body, name="pool_bwd", grid=(n_grp, n_seq),
        out_shape=(jax.ShapeDtypeStruct(dqkv.shape, BF16),
                   jax.ShapeDtypeStruct((n_grp, POOL_GROUP_DIM, POOL_GROUP_DIM), F32),
                   jax.ShapeDtypeStruct((1, n_grp * POOL_GROUP_DIM), F32)),
        in_specs=[pl.BlockSpec((seq, LANES), lambda g, b: (b, u_off + g)),
                  pl.BlockSpec((seq, LANES), lambda g, b: (b, dp_off + g)),
                  pl.BlockSpec((None, POOL_GROUP_DIM, POOL_GROUP_DIM), lambda g, b: (g, 0, 0)),
                  pl.BlockSpec((1, POOL_GROUP_DIM), lambda g, b: (0, g)),
                  pl.BlockSpec(memory_space=pl.ANY)],
        out_specs=(pl.BlockSpec((None, seq, LANES), lambda g, b: (3, b, g)),
                   pl.BlockSpec((None, POOL_GROUP_DIM, POOL_GROUP_DIM), lambda g, b: (g, 0, 0)),
                   pl.BlockSpec((1, POOL_GROUP_DIM), lambda g, b: (0, g))),
        input_output_aliases={4: 0},
        compiler_params=_params(),
    )(proj, dcat, w_pool, pool_scale, dqkv)


def _cond_fwd(c_all, w_cond, b_cols):
    n, _ = c_all.shape
    cols = w_cond.shape[1]

    def body(c_ref, w_ref, b_ref, o_ref):
        cv = c_ref[...]
        a = cv * jax.nn.sigmoid(cv)
        o_ref[...] = jnp.dot(a, w_ref[...], preferred_element_type=F32,
                             precision=lax.Precision.HIGHEST) + b_ref[...]

    return pl.pallas_call(
        body, name="cond_fwd", out_shape=jax.ShapeDtypeStruct((n, cols), F32),
        compiler_params=_params(),
    )(c_all, w_cond, b_cols)


def _cond_bwd(c_all, dmod_all, dmod_cols):
    n, d = c_all.shape
    cols = dmod_cols.shape[1]

    def body(c_ref, dm_ref, dmc_ref, gw_ref, gb_ref):
        cv = c_ref[...]
        a = cv * jax.nn.sigmoid(cv)
        gw_ref[...] = lax.dot_general(a, dmc_ref[...], (((0,), (0,)), ((), ())),
                                      preferred_element_type=F32, precision=lax.Precision.HIGHEST)
        gb_ref[...] = _colsum(dm_ref[...])

    return pl.pallas_call(
        body, name="cond_bwd",
        out_shape=(jax.ShapeDtypeStruct((d, cols), F32), jax.ShapeDtypeStruct((1, dmod_all.shape[1]), F32)),
        compiler_params=_params(),
    )(c_all, dmod_all, dmod_cols)


def _adamw_math(w, g, m, v):
    m = ADAM_B1 * m + (1.0 - ADAM_B1) * g
    v = ADAM_B2 * v + (1.0 - ADAM_B2) * (g * g)
    m_hat = m / (1.0 - ADAM_B1 ** ADAM_STEP)
    v_hat = v / (1.0 - ADAM_B2 ** ADAM_STEP)
    delta = -ADAM_LR * (m_hat / (jnp.sqrt(v_hat) + ADAM_EPS) + ADAM_WD * w)
    return delta, m, v


def _adamw(w, g, m, v, rows, name):
    r, cdim = w.shape

    def body(w_ref, g_ref, m_ref, v_ref, d_ref, nm_ref, nv_ref):
        d_ref[...], nm_ref[...], nv_ref[...] = _adamw_math(w_ref[...], g_ref[...], m_ref[...], v_ref[...])

    spec = pl.BlockSpec((rows, cdim), lambda i: (i, 0))
    sds = jax.ShapeDtypeStruct((r, cdim), F32)
    return pl.pallas_call(
        body, name=name, grid=(r // rows,), out_shape=(sds, sds, sds),
        in_specs=[spec] * 4, out_specs=(spec, spec, spec), compiler_params=_params(),
    )(w, g, m, v)


def _adamw_small(ws, gparts, ms, vs, name):
    n = len(ws)

    def body(*refs):
        w_r, g_r, m_r, v_r = refs[:n], refs[n:2 * n], refs[2 * n:3 * n], refs[3 * n:4 * n]
        outs = refs[4 * n:]
        for i in range(n):
            g = g_r[i][0]
            for dev in range(1, g_r[i].shape[0]):
                g = g + g_r[i][dev]
            delta, m, v = _adamw_math(w_r[i][...], g, m_r[i][...], v_r[i][...])
            outs[i][...] = g
            outs[n + i][...] = delta
            outs[2 * n + i][...] = m
            outs[3 * n + i][...] = v

    sds = [jax.ShapeDtypeStruct(w.shape, F32) for w in ws]
    return pl.pallas_call(
        body, name=name, out_shape=tuple(sds * 4), compiler_params=_params(),
    )(*ws, *gparts, *ms, *vs)


def kernel(x, c, w_cond, b_cond, g_mix_pre, g_mix_post, w_in, w_pool, pool_scale, w_out, g_ffn_pre, g_ffn_post, w_gate, w_up, w_down, loss_target, m_w_cond, m_b_cond, m_g_mix_pre, m_g_mix_post, m_w_in, m_w_pool, m_pool_scale, m_w_out, m_g_ffn_pre, m_g_ffn_post, m_w_gate, m_w_up, m_w_down, v_w_cond, v_b_cond, v_g_mix_pre, v_g_mix_post, v_w_in, v_w_pool, v_pool_scale, v_w_out, v_g_ffn_pre, v_g_ffn_post, v_w_gate, v_w_up, v_w_down):
    n_seq, seq, d = x.shape
    t = n_seq * seq
    xi, yi, ci = _mesh_pos()
    me = 4 * xi + 2 * yi + ci
    x2 = x.reshape(t, d)
    tgt2 = loss_target.reshape(t, d)
    in_rows = w_in.shape[2]
    out_rows = w_out.shape[1]
    ff_rows = w_gate.shape[2]
    ff = N_DEV * ff_rows
    cond_cols = w_cond.shape[2]

    win_t = w_in[0].T.astype(BF16)
    wout_s = w_out[0].astype(BF16)
    wg_t = w_gate[0].T.astype(BF16)
    wu_t = w_up[0].T.astype(BF16)
    wd_s = w_down[0].astype(BF16)
    (c_all,) = _all_gather([c], [jax.ShapeDtypeStruct((N_DEV, n_seq, d), F32)], [(0, ())], "ag_c")
    c_all = c_all.reshape(N_DEV * n_seq, d)

    b_cols = lax.dynamic_slice_in_dim(b_cond, me * cond_cols, cond_cols, axis=1)
    mod_cols = _cond_fwd(c_all, w_cond[0], b_cols)
    (mod_g,) = _all_gather([mod_cols], [jax.ShapeDtypeStruct((N_DEV,) + mod_cols.shape, F32)], [(0, ())], "ag_mod")
    mod_mine = lax.dynamic_slice_in_dim(mod_g, me * n_seq, n_seq, axis=1)
    mod = jnp.transpose(mod_mine, (1, 0, 2)).reshape(n_seq, N_MOD, d)

    h1, win_g = _pre_mix(x2, g_mix_pre, mod, seq, [win_t],
                         [jax.ShapeDtypeStruct((N_DEV, in_rows, d), BF16)], [(0, ())])
    win_full = win_g.reshape(N_DEV * in_rows, d)
    proj = _matmul(h1, win_full, "nt", BF16, 512, N_DEV * in_rows, d, "proj")
    tq = ATT_TILE
    ids = jnp.arange(tq)
    tri_after = jnp.tile(-(ids[:, None] >= ids[None, :]).astype(BF16), (2, 1))
    tri_incl = (ids[:, None] <= ids[None, :]).astype(BF16)
    attn, cstats, wout_g, wgu_g, wd_g = _attn_fwd(
        proj, tri_after, n_seq, seq, [wout_s, wg_t, wu_t, wd_s],
        [jax.ShapeDtypeStruct((N_DEV, out_rows, d), BF16), jax.ShapeDtypeStruct((2, N_DEV, ff_rows, d), BF16),
         jax.ShapeDtypeStruct((N_DEV, ff_rows, d), BF16)],
        [(0, ()), (1, (0,)), (1, (1,)), (2, ())])
    wout_full = wout_g.reshape(N_DEV * out_rows, d)
    wgu_full = wgu_g.reshape(2, ff, d)
    wd_full = wd_g.reshape(ff, d)
    cat = _pool_fwd(proj, w_pool[0], pool_scale, attn, n_seq, seq)
    tok_f32, tok_bf16 = jax.ShapeDtypeStruct((t, d), F32), jax.ShapeDtypeStruct((t, d), BF16)
    seq_sds, vec_sds = jax.ShapeDtypeStruct((n_seq, 1, d), F32), jax.ShapeDtypeStruct((1, d), F32)
    mix, x1, h2 = _matmul_rows(
        cat, wout_full.reshape(2, d // 2, d), ROW_TILE, seq, "mix_mid", _mid_epilogue,
        [x2, g_mix_post, g_ffn_pre, mod], ["tok", "vec", "vec", "mod"],
        [tok_f32, tok_f32, tok_bf16], ["tok", "tok", "tok"])
    gu, act = _ffn_up(h2, wgu_full, 512, ff // 2)
    loss_sum, dy, df, dgate_f, gg_ffn_post = _matmul_rows(
        act, wd_full, ROW_TILE, seq, "ffn_down_post", _post_epilogue,
        [x1, tgt2, g_ffn_post, mod], ["tok", "tok", "vec", "mod"],
        [jax.ShapeDtypeStruct((1, LANES), F32), tok_f32, tok_bf16, seq_sds, vec_sds],
        ["loss", "tok", "tok", "seq", "vec"])

    dgu = _ffn_act_bwd(df, wd_full, gu, 512, ff // 2)
    gwd, gwd_b = _matmul(act, df, "tn", F32, ff // 2, d // 2, t, "grad_w_down", bf16_copy=True)
    gwgu, gwgu_b = _matmul(dgu, h2, "tn", F32, ff // 2, d // 2, t, "grad_w_gate_up", bf16_copy=True)
    dx1, dmix, dshift_f, dscale_f, dgate_m, gg_ffn_pre, gg_mix_post = _matmul_rows(
        dgu, wgu_full, ROW_TILE, seq, "dh2_bwd_mid", _bwd_mid_epilogue,
        [dy, x1, mix, g_ffn_pre, g_mix_post, mod], ["tok", "tok", "tok", "vec", "vec", "mod"],
        [tok_f32, tok_bf16, seq_sds, seq_sds, seq_sds, vec_sds, vec_sds],
        ["tok", "tok", "seq", "seq", "seq", "vec", "vec"])
    dcat = _matmul(dmix, wout_full, "nt", BF16, 512, d, d, "dcat")
    gwout, gwout_b = _matmul(cat, dmix, "tn", F32, d // 2, d, t, "grad_w_out", bf16_copy=True)
    dqkv, rv_wgu, rv_wd, rv_wout = _attn_bwd(
        proj, dcat, cstats, tri_after, tri_incl, n_seq, seq,
        [gwgu_b.reshape(2, N_DEV, ff_rows, d), gwd_b.reshape(1, N_DEV, ff_rows, d),
         gwout_b.reshape(1, N_DEV, out_rows, d)])
    dproj, gw_pool, gs_pool = _pool_bwd(proj, dcat, w_pool[0], pool_scale, dqkv, n_seq, seq)
    pad_d = lambda v: jnp.pad(v, ((0, 0), (0, d - v.shape[1])))
    n_gw = gw_pool.size // d
    early = jnp.concatenate(
        [gg_mix_post, gg_ffn_pre, gg_ffn_post, pad_d(gs_pool), pad_d(loss_sum), jnp.zeros((3, d), F32),
         gw_pool.reshape(n_gw, d),
         jnp.concatenate([dgate_m, dshift_f, dscale_f, dgate_f], axis=1).reshape(n_seq * 4, d)], axis=0)
    gwin, gwin_b, early_g = _matmul(
        dproj, h1, "tn", F32, d // 2, d, t, "grad_w_in", bf16_copy=True,
        ag=([early], [jax.ShapeDtypeStruct((N_DEV,) + early.shape, F32)], [(0, ())]))
    grad_x, dshift_m, dscale_m, gg_mix_pre, rv_win = _matmul_rows(
        dproj, win_full.reshape(4, d // 2, d), ROW_TILE, seq, "dh1_bwd_pre", _bwd_pre_epilogue,
        [dx1, x2, g_mix_pre, mod], ["tok", "tok", "vec", "mod"],
        [tok_f32, seq_sds, seq_sds, vec_sds], ["tok", "seq", "seq", "vec"],
        rs_sends=[gwin_b.reshape(1, N_DEV, in_rows, d)])


    late = jnp.concatenate([gg_mix_pre, dshift_m.reshape(n_seq, d), dscale_m.reshape(n_seq, d),
                            jnp.zeros((8 - 1 - 2 * n_seq, d), F32)], axis=0)
    (late_g,) = _all_gather([late], [jax.ShapeDtypeStruct((N_DEV,) + late.shape, F32)], [(0, ())], "ag_late")
    loss = jnp.sum(early_g[:, 4, 0]) * (0.5 / d)
    dmod_all = jnp.concatenate(
        [late_g[:, 1:1 + n_seq, None, :], late_g[:, 1 + n_seq:1 + 2 * n_seq, None, :],
         early_g[:, 8 + n_gw:, :].reshape(N_DEV, n_seq, 4, d)], axis=2).reshape(N_DEV * n_seq, N_MOD * d)
    dmod_cols = lax.dynamic_slice_in_dim(dmod_all, me * cond_cols, cond_cols, axis=1)
    grad_w_cond, grad_b_cond = _cond_bwd(c_all, dmod_all, dmod_cols)

    small_ws = [g_mix_pre, g_mix_post, g_ffn_pre, g_ffn_post, pool_scale, w_pool.reshape(-1, POOL_GROUP_DIM)]
    small_ms = [m_g_mix_pre, m_g_mix_post, m_g_ffn_pre, m_g_ffn_post, m_pool_scale, m_w_pool.reshape(-1, POOL_GROUP_DIM)]
    small_vs = [v_g_mix_pre, v_g_mix_post, v_g_ffn_pre, v_g_ffn_post, v_pool_scale, v_w_pool.reshape(-1, POOL_GROUP_DIM)]
    small_gparts = [late_g[:, 0:1, :], early_g[:, 0:1, :], early_g[:, 1:2, :], early_g[:, 2:3, :],
                    early_g[:, 3:4, :pool_scale.shape[1]],
                    early_g[:, 8:8 + n_gw, :].reshape(N_DEV, -1, POOL_GROUP_DIM)]
    so = _adamw_small(small_ws, small_gparts, small_ms, small_vs, "adamw_small")
    ns = len(small_ws)
    sg, sdl, sm, sv = so[:ns], so[ns:2 * ns], so[2 * ns:3 * ns], so[3 * ns:]
    pool_shape = w_pool.shape
    fix = lambda lst: [lst[0], lst[1], lst[2], lst[3], lst[4], lst[5].reshape(pool_shape)]
    sg, sdl, sm, sv = fix(sg), fix(sdl), fix(sm), fix(sv)

    def big(w, g, m, v, rows, name):
        dl, nm, nv = _adamw(w[0], g, m[0], v[0], rows, name)
        return g[None], dl[None], nm[None], nv[None]

    o_cond = big(w_cond, grad_w_cond, m_w_cond, v_w_cond, 256, "adamw_w_cond")
    o_bcond = _adamw(b_cond, grad_b_cond, m_b_cond, v_b_cond, 1, "adamw_b_cond")
    o_bcond = (grad_b_cond,) + tuple(o_bcond)

    def reduced(mine, recv, slab, w, m, v, name, transposed=False, transpose=False):
        turn = (lambda u: u.T) if transposed else (lambda u: u)
        outs = _rs_final_adamw(mine, recv, slab, turn(w[0]), turn(m[0]), turn(v[0]), name, transpose)
        return tuple(turn(o)[None] for o in outs)

    o_in = reduced(gwin.reshape(1, N_DEV, in_rows, d), rv_win, 0, w_in, m_w_in, v_w_in, "adamw_w_in",
                   transpose=True)
    o_out = reduced(gwout.reshape(1, N_DEV, out_rows, d), rv_wout, 0, w_out, m_w_out, v_w_out, "adamw_w_out")
    gwgu8 = gwgu.reshape(2, N_DEV, ff_rows, d)
    o_gate = reduced(gwgu8, rv_wgu, 0, w_gate, m_w_gate, v_w_gate, "adamw_w_gate", transposed=True)
    o_up = reduced(gwgu8, rv_wgu, 1, w_up, m_w_up, v_w_up, "adamw_w_up", transposed=True)
    o_down = reduced(gwd.reshape(1, N_DEV, ff_rows, d), rv_wd, 0, w_down, m_w_down, v_w_down, "adamw_w_down")

    def pick(k):
        small_k = [sg, sdl, sm, sv][k]
        return [o_cond[k], o_bcond[k], small_k[0], small_k[1], o_in[k], small_k[5], small_k[4], o_out[k],
                small_k[2], small_k[3], o_gate[k], o_up[k], o_down[k]]

    return (loss, grad_x.reshape(n_seq, seq, d), *pick(0), *pick(1), *pick(2), *pick(3))
```

```python
import functools
import math

import jax
import jax.numpy as jnp
from jax import lax
from jax.experimental import pallas as pl
from jax.experimental.pallas import tpu as pltpu

F32 = jnp.float32
BF16 = jnp.bfloat16
MESH = pl.DeviceIdType.MESH

N_DEV = 8
HEAD_DIM = 64
LANES = 128
POOL_WINDOWS = (2, 4, 8, 16)
POOL_GROUP_DIM = 128
N_MOD = 6
EPS = 1e-6
ATT_TILE = 256
ATT_PAIRS = 2
VMEM_LIMIT = 56 * 1024 * 1024

ADAM_LR = 0.001
ADAM_B1 = 0.9
ADAM_B2 = 0.999
ADAM_EPS = 1e-08
ADAM_WD = 0.01
ADAM_STEP = 10


def _params(**kw):
    return pltpu.CompilerParams(vmem_limit_bytes=VMEM_LIMIT, **kw)


def _dot_nn(a, b):
    return jnp.dot(a, b, preferred_element_type=F32)


def _dot_nt(a, b):
    return lax.dot_general(a, b, (((1,), (1,)), ((), ())), preferred_element_type=F32)


def _dot_tn(a, b):
    return lax.dot_general(a, b, (((0,), (0,)), ((), ())), preferred_element_type=F32)


def _mesh_pos():
    return lax.axis_index("x"), lax.axis_index("y"), lax.axis_index("c")


def _ag_phases(dests, src, outs, send_sems, recv_sems, local_sems):
    n = len(src)
    x, y, c = _mesh_pos()
    me, sibling = (x, y, c), (x, y, 1 - c)
    chips = [(1 - x, y), (x, 1 - y), (1 - x, 1 - y)]

    def slot(i, dev):
        oi, prefix = dests[i]
        px, py, pc = dev
        return outs[oi].at[prefix + (4 * px + 2 * py + pc,)]

    def copy(i, k, block, to, from_src=False):
        return pltpu.make_async_remote_copy(
            src_ref=src[i] if from_src else slot(i, block), dst_ref=slot(i, block),
            send_sem=send_sems.at[i, k], recv_sem=recv_sems.at[i, k],
            device_id=to, device_id_type=MESH)

    def mine(i):
        return pltpu.make_async_copy(src[i], slot(i, me), local_sems.at[i])

    def first(i):
        return [copy(i, 0, me, sibling, from_src=True)] + [
            copy(i, 1 + j, me, (*chip, c), from_src=True) for j, chip in enumerate(chips)]

    def passed(i, j):
        return copy(i, 4 + j, (*chips[j], c), sibling)

    def start():
        for i in range(n):
            mine(i).start()
        for i in range(n):
            for cp in first(i):
                cp.start()

    def forward():
        for j, chip in enumerate(chips):
            for i in range(n):
                copy(i, 1 + j, (*chip, c), me).wait_recv()
                passed(i, j).start()

    def finish():
        for i in range(n):
            copy(i, 0, sibling, me).wait_recv()
            for j, chip in enumerate(chips):
                copy(i, 4 + j, (*chip, 1 - c), me).wait_recv()
        for i in range(n):
            for cp in first(i) + [passed(i, j) for j in range(3)]:
                cp.wait_send()
            mine(i).wait()

    return start, forward, finish


def _ag_scratch(n):
    return [pltpu.SemaphoreType.DMA((n, 7)), pltpu.SemaphoreType.DMA((n, 7)), pltpu.SemaphoreType.DMA((n,))]


def _all_gather(srcs, out_shapes, dests, name):
    n = len(srcs)

    def body(*refs):
        src = refs[:n]
        outs = refs[n:n + len(out_shapes)]
        start, forward, finish = _ag_phases(dests, src, outs, *refs[n + len(out_shapes):])
        start()
        forward()
        finish()

    any_spec = pl.BlockSpec(memory_space=pl.ANY)
    return pl.pallas_call(
        body, name=name,
        out_shape=tuple(out_shapes),
        in_specs=[any_spec] * n,
        out_specs=tuple([any_spec] * len(out_shapes)),
        scratch_shapes=_ag_scratch(n),
    )(*srcs)


def _rs_phases(shapes, src, dst, send_sems, recv_sems):
    x, y, c = _mesh_pos()

    def copies():
        out = []
        n = 0
        for i, shp in enumerate(shapes):
            for m in range(shp[0]):
                for k in range(1, N_DEV):
                    px, py, pc = x ^ (k >> 2), y ^ ((k >> 1) & 1), c ^ (k & 1)
                    out.append(pltpu.make_async_remote_copy(
                        src_ref=src[i].at[m, 4 * px + 2 * py + pc], dst_ref=dst[i].at[m, k - 1],
                        send_sem=send_sems.at[n], recv_sem=recv_sems.at[n],
                        device_id=(px, py, pc), device_id_type=MESH))
                    n += 1
        return out

    def start():
        for cp in copies():
            cp.start()

    def finish():
        for cp in copies():
            cp.wait_send()
        for cp in copies():
            cp.wait_recv()

    return start, finish


def _rs_out(sends):
    return [jax.ShapeDtypeStruct((s.shape[0], N_DEV - 1) + s.shape[2:], s.dtype) for s in sends]


def _rs_scratch(sends):
    total = sum((N_DEV - 1) * s.shape[0] for s in sends)
    return [pltpu.SemaphoreType.DMA((total,)), pltpu.SemaphoreType.DMA((total,))]


def _rs_final_adamw(mine, recv, slab, w, m, v, name, transpose=False):
    _, _, r, cdim = mine.shape
    x, y, c = _mesh_pos()
    me = jnp.reshape(4 * x + 2 * y + c, (1,)).astype(jnp.int32)

    def body(me_ref, p_ref, r_ref, w_ref, m_ref, v_ref, g_ref, d_ref, nm_ref, nv_ref):
        del me_ref
        g = p_ref[...]
        for k in range(N_DEV - 1):
            g = g + r_ref[k].astype(F32)
        if transpose:
            g = g.T
        g_ref[...] = g
        d_ref[...], nm_ref[...], nv_ref[...] = _adamw_math(w_ref[...], g, m_ref[...], v_ref[...])

    full = pl.BlockSpec(w.shape, lambda i, s: (0, 0))
    sds = jax.ShapeDtypeStruct(w.shape, F32)
    return pl.pallas_call(
        body, name=name, out_shape=(sds, sds, sds, sds),
        grid_spec=pltpu.PrefetchScalarGridSpec(
            num_scalar_prefetch=1, grid=(1,),
            in_specs=[pl.BlockSpec((None, None, r, cdim), lambda i, s: (slab, s[0], 0, 0)),
                      pl.BlockSpec((None, N_DEV - 1, r, cdim), lambda i, s: (slab, 0, 0, 0)), full, full, full],
            out_specs=(full, full, full, full)),
        compiler_params=_params(),
    )(me, mine, recv, w, m, v)


def _matmul(a, b, mode, out_dtype, tm, tn, tk, name, bf16_copy=False, rs_sends=(), ag=None):
    ga = a.shape[0] if a.ndim == 3 else None
    gb = b.shape[0] if b.ndim == 3 else None
    a2, b2 = a.shape[-2:], b.shape[-2:]
    if mode == "nn":
        (m, k), n = a2, b2[1]
    elif mode == "nt":
        (m, k), n = a2, b2[0]
    else:
        (k, m), n = a2, b2[1]
    assert m % tm == 0 and n % tn == 0 and k % tk == 0, (name, m, n, k)
    nk = k // tk
    g_n = ga or 1
    batch_out = mode == "tn" and ga is not None
    n_red = nk if batch_out else nk * g_n
    dot = {"nn": _dot_nn, "nt": _dot_nt, "tn": _dot_tn}[mode]
    acc_in_out = out_dtype == F32

    n_rs = len(rs_sends)
    rs_shapes = [r.shape for r in rs_sends]
    ag_srcs, ag_out_shapes, ag_dests = ag if ag is not None else ((), (), ())
    n_ag, n_ag_out = len(ag_srcs), len(ag_out_shapes)
    n_out = 2 if bf16_copy else 1
    assert not bf16_copy or acc_in_out
    assert not (n_rs and n_ag)

    def body(a_ref, b_ref, *rest):
        rs_src, rest = rest[:n_rs], rest[n_rs:]
        ag_src, rest = rest[:n_ag], rest[n_ag:]
        o_ref = rest[0]
        copy_ref = rest[1] if bf16_copy else None
        rs_dst, rest = rest[n_out:n_out + n_rs], rest[n_out + n_rs:]
        ag_out, scratch = rest[:n_ag_out], rest[n_ag_out:]
        first = functools.reduce(jnp.logical_and, [pl.program_id(ax) == 0 for ax in range(4)])
        last = functools.reduce(jnp.logical_and, [pl.program_id(ax) == grid[ax] - 1 for ax in range(4)])
        if n_rs:
            rs_start, rs_finish = _rs_phases(rs_shapes, rs_src, rs_dst, *scratch[-2:])
            pl.when(first)(rs_start)
        if n_ag:
            ag_start, ag_forward, ag_finish = _ag_phases(ag_dests, ag_src, ag_out, *scratch[-3:])
            pl.when(first)(ag_start)
        p = dot(a_ref[...], b_ref[...])
        kk = pl.program_id(3) if batch_out else pl.program_id(2) * nk + pl.program_id(3)
        if n_red == 1:
            o_ref[...] = p.astype(out_dtype)
            if bf16_copy:
                copy_ref[...] = p.astype(BF16)
        else:
            acc = o_ref if acc_in_out else scratch[0]

            @pl.when(kk == 0)
            def _():
                acc[...] = p

            @pl.when(kk > 0)
            def _():
                acc[...] += p

            @pl.when(kk == n_red - 1)
            def _():
                if not acc_in_out:
                    o_ref[...] = acc[...].astype(out_dtype)
                if bf16_copy:
                    copy_ref[...] = acc[...].astype(BF16)

        if n_rs:
            pl.when(last)(rs_finish)
        if n_ag:
            @pl.when(last)
            def _():
                ag_forward()
                ag_finish()

    def order(ids):
        return ids if batch_out else (ids[2], ids[0], ids[1], ids[3])

    def a_idx(*ids):
        g, i, j, kq = order(ids)
        blk = {"nn": (i, kq), "nt": (i, kq), "tn": (kq, i)}[mode]
        return (g,) + blk if ga is not None else blk

    def b_idx(*ids):
        g, i, j, kq = order(ids)
        blk = {"nn": (kq, j), "nt": (j, kq), "tn": (kq, j)}[mode]
        return (g,) + blk if gb is not None else blk

    def o_idx(*ids):
        g, i, j, kq = order(ids)
        return (g, i, j) if batch_out else (i, j)

    a_blk = {"nn": (tm, tk), "nt": (tm, tk), "tn": (tk, tm)}[mode]
    b_blk = {"nn": (tk, tn), "nt": (tn, tk), "tn": (tk, tn)}[mode]
    if ga is not None:
        a_blk = (None,) + a_blk
    if gb is not None:
        b_blk = (None,) + b_blk
    if batch_out:
        out_shape = jax.ShapeDtypeStruct((g_n, m, n), out_dtype)
        o_blk = (None, tm, tn)
        grid = (g_n, m // tm, n // tn, nk)
    else:
        out_shape = jax.ShapeDtypeStruct((m, n), out_dtype)
        o_blk = (tm, tn)
        grid = (m // tm, n // tn, g_n, nk)
    scratch = [] if (acc_in_out or n_red == 1) else [pltpu.VMEM((tm, tn), F32)]
    any_spec = pl.BlockSpec(memory_space=pl.ANY)
    out_shapes = [out_shape] + ([jax.ShapeDtypeStruct(out_shape.shape, BF16)] if bf16_copy else [])
    res = pl.pallas_call(
        body, name=name, out_shape=tuple(out_shapes + _rs_out(rs_sends) + list(ag_out_shapes)), grid=grid,
        in_specs=[pl.BlockSpec(a_blk, a_idx), pl.BlockSpec(b_blk, b_idx)] + [any_spec] * (n_rs + n_ag),
        out_specs=tuple([pl.BlockSpec(o_blk, o_idx)] * n_out + [any_spec] * (n_rs + n_ag_out)),
        scratch_shapes=scratch + (_rs_scratch(rs_sends) if n_rs else []) + (_ag_scratch(n_ag) if n_ag else []),
        compiler_params=_params(),
    )(a, b, *rs_sends, *ag_srcs)
    return res if len(res) > 1 else res[0]


EW_TILE = 256
ROW_TILE = 512
EPILOGUE_CHUNKS = 8
MXU_WIDTH = 256


def _rms(v):
    return lax.rsqrt(jnp.mean(v * v, axis=-1, keepdims=True) + EPS)


def _rms_bwd(dhat, vh, r):
    return r * (dhat - vh * jnp.mean(dhat * vh, axis=-1, keepdims=True))


def _tok_spec(tm, d):
    return pl.BlockSpec((tm, d), lambda i: (i, 0))


def _vec_spec(d):
    return pl.BlockSpec((1, d), lambda i: (0, 0))


def _mod_spec(tiles_per_seq, d):
    return pl.BlockSpec((None, N_MOD, d), lambda i: (i // tiles_per_seq, 0, 0))


def _seq_acc_spec(tiles_per_seq, d):
    return pl.BlockSpec((None, 1, d), lambda i: (i // tiles_per_seq, 0, 0))


def _acc(ref, val, first):
    if first is False:
        ref[...] += val
        return

    @pl.when(first)
    def _():
        ref[...] = val

    @pl.when(jnp.logical_not(first))
    def _():
        ref[...] += val


def _colsum(v):
    return jnp.sum(v, axis=0, keepdims=True)


def _pre_mix(x2, g_pre, mod, seq, ag_srcs, ag_out_shapes, ag_dests):
    t, d = x2.shape
    tm = EW_TILE
    n_steps = t // tm
    n_ag, n_ag_out = len(ag_srcs), len(ag_out_shapes)

    def body(x_ref, g_ref, mod_ref, *rest):
        ag_src, h_ref = rest[:n_ag], rest[n_ag]
        ag_out, sems = rest[n_ag + 1:n_ag + 1 + n_ag_out], rest[n_ag + 1 + n_ag_out:]
        ag_start, ag_forward, ag_finish = _ag_phases(ag_dests, ag_src, ag_out, *sems)
        step = pl.program_id(0)
        pl.when(step == 0)(ag_start)
        xv = x_ref[...]
        n = xv * _rms(xv) * g_ref[...]
        h_ref[...] = (n * (1.0 + mod_ref[1:2, :]) + mod_ref[0:1, :]).astype(BF16)

        @pl.when(step == n_steps - 1)
        def _():
            ag_forward()
            ag_finish()

    any_spec = pl.BlockSpec(memory_space=pl.ANY)
    return pl.pallas_call(
        body, name="pre_mix", out_shape=(jax.ShapeDtypeStruct((t, d), BF16), *ag_out_shapes), grid=(n_steps,),
        in_specs=[_tok_spec(tm, d), _vec_spec(d), _mod_spec(seq // tm, d)] + [any_spec] * n_ag,
        out_specs=(_tok_spec(tm, d), *([any_spec] * n_ag_out)),
        scratch_shapes=_ag_scratch(n_ag), compiler_params=_params(),
    )(x2, g_pre, mod, *ag_srcs)


def _matmul_rows(a, b, tm, seq, name, epilogue, ep_in, ep_in_kinds, ep_out, ep_out_kinds, rs_sends=()):
    g_n = a.shape[0] if a.ndim == 3 else None
    (m, k), n = a.shape[-2:], b.shape[-1]
    tps = seq // tm
    n_i = m // tm
    n_rs = len(rs_sends)
    rs_shapes = [r.shape for r in rs_sends]
    n_in, n_out = len(ep_in), len(ep_out)
    n_cols = n // MXU_WIDTH
    rc, cw = tm // EPILOGUE_CHUNKS, n // n_cols

    def prev(i):
        return jnp.maximum(i - 1, 0)

    def spec(kind):
        return {"tok": pl.BlockSpec((tm, n), lambda i: (prev(i), 0)),
                "vec": pl.BlockSpec((1, n), lambda i: (0, 0)),
                "mod": pl.BlockSpec((None, N_MOD, n), lambda i: (prev(i) // tps, 0, 0)),
                "seq": pl.BlockSpec((None, 1, n), lambda i: (prev(i) // tps, 0, 0)),
                "loss": pl.BlockSpec((1, LANES), lambda i: (0, 0))}[kind]

    def body(a_ref, b_ref, *rest):
        in_refs, rest = rest[:n_in], rest[n_in:]
        rs_src, rest = rest[:n_rs], rest[n_rs:]
        out_refs, rest = rest[:n_out], rest[n_out:]
        rs_dst, rest = rest[:n_rs], rest[n_rs:]
        fin = rest[0]
        i = pl.program_id(0)
        if n_rs:
            rs_start, rs_finish = _rs_phases(rs_shapes, rs_src, rs_dst, *rest[1:])
            pl.when(i == 0)(rs_start)

        def product(cols):
            if g_n is None:
                return _dot_nn(a_ref[...], b_ref[:, cols])
            p = _dot_nn(a_ref[0], b_ref[0, :, cols])
            for g in range(1, g_n):
                p = p + _dot_nn(a_ref[g], b_ref[g, :, cols])
            return p

        def step(with_epilogue, with_matmul):
            parts = []
            for c in range(EPILOGUE_CHUNKS):
                if with_epilogue:
                    rows = pl.ds(c * rc, rc)
                    epilogue(fin[rows, :], i - 1, tps, in_refs, out_refs, rows, c)
                while with_matmul and len(parts) < (c + 1) * n_cols // EPILOGUE_CHUNKS:
                    cols = slice(len(parts) * cw, (len(parts) + 1) * cw)
                    parts.append((cols, product(cols)))
            for cols, v in parts:
                fin[:, cols] = v

        pl.when(i == 0)(functools.partial(step, False, True))
        pl.when(jnp.logical_and(i > 0, i < n_i))(functools.partial(step, True, True))
        pl.when(i == n_i)(functools.partial(step, True, False))

        if n_rs:
            pl.when(i == n_i)(rs_finish)

    def row(i):
        return jnp.minimum(i, n_i - 1)

    if g_n is None:
        a_spec = pl.BlockSpec((tm, k), lambda i: (row(i), 0))
        b_spec = pl.BlockSpec(b.shape, lambda i: (0, 0), pipeline_mode=pl.Buffered(1))
    else:
        a_spec = pl.BlockSpec((g_n, tm, k), lambda i: (0, row(i), 0))
        b_spec = pl.BlockSpec(b.shape, lambda i: (0, 0, 0), pipeline_mode=pl.Buffered(1))
    any_spec = pl.BlockSpec(memory_space=pl.ANY)
    res = pl.pallas_call(
        body, name=name, grid=(n_i + 1,), out_shape=tuple(list(ep_out) + _rs_out(rs_sends)),
        in_specs=[a_spec, b_spec] + [spec(kd) for kd in ep_in_kinds] + [any_spec] * n_rs,
        out_specs=tuple([spec(kd) for kd in ep_out_kinds] + [any_spec] * n_rs),
        scratch_shapes=[pltpu.VMEM((tm, n), F32)] + (_rs_scratch(rs_sends) if n_rs else []),
        compiler_params=_params(),
    )(a, b, *ep_in, *rs_sends)
    return res


def _first(cond, chunk):
    return cond if chunk == 0 else False


def _mid_epilogue(mv, i, tps, in_refs, out_refs, rows, chunk):
    x_ref, gpost_ref, gpre_ref, mod_ref = in_refs
    mix_ref, x1_ref, h2_ref = out_refs
    mix_ref[rows, :] = mv
    x1 = x_ref[rows, :] + mod_ref[2:3, :] * (mv * _rms(mv) * gpost_ref[...])
    x1_ref[rows, :] = x1
    n = x1 * _rms(x1) * gpre_ref[...]
    h2_ref[rows, :] = (n * (1.0 + mod_ref[4:5, :]) + mod_ref[3:4, :]).astype(BF16)


def _post_epilogue(fv, i, tps, in_refs, out_refs, rows, chunk):
    x1_ref, tgt_ref, g_ref, mod_ref = in_refs
    loss_ref, dy_ref, df_ref, dgate_ref, gg_ref = out_refs
    d = fv.shape[1]
    r = _rms(fv)
    fh = fv * r
    nf = fh * g_ref[...]
    gate = mod_ref[5:6, :]
    err = x1_ref[rows, :] + gate * nf - tgt_ref[rows, :]
    _acc(loss_ref, jnp.sum(_colsum(err * err), axis=1, keepdims=True) * jnp.ones((1, LANES), F32),
         _first(i == 0, chunk))
    dy = err * (1.0 / d)
    dy_ref[rows, :] = dy
    _acc(dgate_ref, _colsum(dy * nf), _first(i % tps == 0, chunk))
    dn = dy * gate
    _acc(gg_ref, _colsum(dn * fh), _first(i == 0, chunk))
    df_ref[rows, :] = _rms_bwd(dn * g_ref[...], fh, r).astype(BF16)


def _bwd_mid_epilogue(dh, i, tps, in_refs, out_refs, rows, chunk):
    dy_ref, x1_ref, mix_ref, gpre_ref, gpost_ref, mod_ref = in_refs
    dx1_ref, dmix_ref, dshift_ref, dscale_ref, dgate_ref, ggpre_ref, ggpost_ref = out_refs
    seq_first, first = _first(i % tps == 0, chunk), _first(i == 0, chunk)
    x1 = x1_ref[rows, :]
    r = _rms(x1)
    xh = x1 * r
    gpre = gpre_ref[...]
    _acc(dshift_ref, _colsum(dh), seq_first)
    _acc(dscale_ref, _colsum(dh * xh * gpre), seq_first)
    dn = dh * (1.0 + mod_ref[4:5, :])
    _acc(ggpre_ref, _colsum(dn * xh), first)
    dx1 = dy_ref[rows, :] + _rms_bwd(dn * gpre, xh, r)
    dx1_ref[rows, :] = dx1
    mv = mix_ref[rows, :]
    rm = _rms(mv)
    mh = mv * rm
    gpost = gpost_ref[...]
    _acc(dgate_ref, _colsum(dx1 * mh * gpost), seq_first)
    dnm = dx1 * mod_ref[2:3, :]
    _acc(ggpost_ref, _colsum(dnm * mh), first)
    dmix_ref[rows, :] = _rms_bwd(dnm * gpost, mh, rm).astype(BF16)


def _bwd_pre_epilogue(dh, i, tps, in_refs, out_refs, rows, chunk):
    dx1_ref, x_ref, g_ref, mod_ref = in_refs
    gx_ref, dshift_ref, dscale_ref, gg_ref = out_refs
    seq_first = _first(i % tps == 0, chunk)
    xv = x_ref[rows, :]
    r = _rms(xv)
    xh = xv * r
    g = g_ref[...]
    _acc(dshift_ref, _colsum(dh), seq_first)
    _acc(dscale_ref, _colsum(dh * xh * g), seq_first)
    dn = dh * (1.0 + mod_ref[1:2, :])
    _acc(gg_ref, _colsum(dn * xh), _first(i == 0, chunk))
    gx_ref[rows, :] = dx1_ref[rows, :] + _rms_bwd(dn * g, xh, r)


def _ffn_up(h2, wgu, tm, tn):
    t, d = h2.shape
    f = wgu.shape[1]

    def body(h_ref, w_ref, gu_ref, act_ref):
        h = h_ref[...]
        g = _dot_nt(h, w_ref[0])
        u = _dot_nt(h, w_ref[1])
        gu_ref[0] = g.astype(BF16)
        gu_ref[1] = u.astype(BF16)
        act_ref[...] = (g * jax.nn.sigmoid(g) * u).astype(BF16)

    return pl.pallas_call(
        body, name="ffn_up", grid=(f // tn, t // tm),
        out_shape=(jax.ShapeDtypeStruct((2, t, f), BF16), jax.ShapeDtypeStruct((t, f), BF16)),
        in_specs=[pl.BlockSpec((tm, d), lambda j, i: (i, 0)), pl.BlockSpec((2, tn, d), lambda j, i: (0, j, 0))],
        out_specs=(pl.BlockSpec((2, tm, tn), lambda j, i: (0, i, j)), pl.BlockSpec((tm, tn), lambda j, i: (i, j))),
        compiler_params=_params(),
    )(h2, wgu)


def _ffn_act_bwd(df, wd, gu, tm, tn):
    t, d = df.shape
    f = wd.shape[0]

    def body(df_ref, w_ref, gu_ref, dgu_ref):
        da = _dot_nt(df_ref[...], w_ref[...])
        g = gu_ref[0].astype(F32)
        u = gu_ref[1].astype(F32)
        s = jax.nn.sigmoid(g)
        silu = g * s
        dgu_ref[0] = (da * u * (s + silu * (1.0 - s))).astype(BF16)
        dgu_ref[1] = (da * silu).astype(BF16)

    return pl.pallas_call(
        body, name="ffn_act_bwd", grid=(f // tn, t // tm),
        out_shape=jax.ShapeDtypeStruct((2, t, f), BF16),
        in_specs=[pl.BlockSpec((tm, d), lambda j, i: (i, 0)), pl.BlockSpec((tn, d), lambda j, i: (j, 0)),
                  pl.BlockSpec((2, tm, tn), lambda j, i: (0, i, j))],
        out_specs=pl.BlockSpec((2, tm, tn), lambda j, i: (0, i, j)),
        compiler_params=_params(),
    )(df, wd, gu)


SIGN_BIT = 0x80000000
Q_SCALE = 1.0 / math.sqrt(HEAD_DIM)


def _softplus(z):
    neg_abs = lax.bitcast_convert_type(lax.bitcast_convert_type(z, jnp.uint32) | jnp.uint32(SIGN_BIT), F32)
    return jnp.maximum(z, 0.0) + jnp.log(1.0 + jnp.exp(neg_abs))


def _hi_lo(v):
    hi = v.astype(BF16)
    return jnp.concatenate([hi, (v - hi.astype(F32)).astype(BF16)], axis=1)


def _emit_skewed(chains, lag=1):
    for t in range(max(len(ch) for ch in chains) + lag * (len(chains) - 1)):
        for c, ch in enumerate(chains):
            if 0 <= t - lag * c < len(ch):
                ch[t - lag * c]()


def _fwd_chain(blk, qs, k_ref, v_ref, c0, kb, cols, mask, ntri, lane, tq):
    st = {}

    def scores():
        st["z"] = _dot_nt(qs, k_ref[pl.ds(c0, tq), cols])

    def soft():
        sp = _softplus(st["z"])
        if mask is not None:
            sp = jnp.where(mask, sp, 0.0)
        st["parts"] = _hi_lo(sp)
        st["cur"] = blk["cur"]
        blk["cm"] = jnp.where(lane == kb, blk["cur"], blk["cm"])
        blk["cur"] = blk["cur"] - jnp.sum(sp, axis=1, keepdims=True)

    def sums():
        st["s"] = _dot_nn(st["parts"], ntri)

    def weights():
        w = jnp.exp(st["z"] + st["s"] + st["cur"])
        if mask is not None:
            w = jnp.where(mask, w, 0.0)
        st["w"] = w.astype(BF16)

    def out():
        p = _dot_nn(st["w"], v_ref[pl.ds(c0, tq), cols])
        blk["pv"] = p if blk["pv"] is None else blk["pv"] + p

    return [scores, soft, sums, weights, out]


def _bwd_chain(blk, qs, dos, cs, k_ref, v_ref, dk_ref, dv_ref, c0, kb, cols, mask, ntri, tri_i, lane, tq):
    st = {}

    def scores():
        st["z"] = _dot_nt(qs, k_ref[pl.ds(c0, tq), cols])
        st["dw"] = _dot_nt(dos, v_ref[pl.ds(c0, tq), cols])

    def soft():
        sp = _softplus(st["z"])
        if mask is not None:
            sp = jnp.where(mask, sp, 0.0)
        st["sp"] = sp
        st["parts"] = _hi_lo(sp)
        st["cur"] = jnp.sum(jnp.where(lane == kb, cs, 0.0), axis=1, keepdims=True)

    def sums():
        st["s"] = _dot_nn(st["parts"], ntri)

    def weights():
        w = jnp.exp(st["z"] + st["s"] + st["cur"])
        if mask is not None:
            w = jnp.where(mask, w, 0.0)
        ee = w * st["dw"]
        st["w"], st["ee"], st["ec"] = w.astype(BF16), ee, blk["ec"]
        blk["ec"] = blk["ec"] + jnp.sum(ee, axis=1, keepdims=True)

    def prefix():
        st["einc"] = _dot_nn(st["ee"].astype(BF16), tri_i)

    def dz():
        v = st["ee"] - jnp.exp(st["z"] - st["sp"]) * (st["einc"] + st["ec"])
        if mask is not None:
            v = jnp.where(mask, v, 0.0)
        st["dz"] = v.astype(BF16)

    def grads():
        p = _dot_nn(st["dz"], k_ref[pl.ds(c0, tq), cols])
        blk["dq"] = p if blk["dq"] is None else blk["dq"] + p
        dk_ref[pl.ds(c0, tq), :] += _dot_tn(st["dz"], qs)
        dv_ref[pl.ds(c0, tq), :] += _dot_tn(st["w"], dos)

    return [scores, soft, sums, weights, prefix, dz, grads]


def _stack_heads(v, lane, scale=None):
    if scale is not None:
        v = v * jnp.asarray(scale, v.dtype)
    zero = jnp.zeros_like(v)
    return jnp.concatenate([jnp.where(lane < HEAD_DIM, v, zero), jnp.where(lane >= HEAD_DIM, v, zero)], axis=0)


def _diag_mask(tq):
    row = lax.broadcasted_iota(jnp.int32, (2 * tq, tq), 0)
    col = lax.broadcasted_iota(jnp.int32, (2 * tq, tq), 1)
    return col < jnp.where(row >= tq, row - tq, row)


def _attn_fwd(proj, tri_after, n_seq, seq, ag_srcs, ag_out_shapes, ag_dests):
    t = proj.shape[0]
    tq = ATT_TILE
    npp = ATT_PAIRS
    n_blk = (proj.shape[1] // 4) // (npp * LANES)
    n_ag, n_ag_out = len(ag_srcs), len(ag_out_shapes)
    n_steps = n_seq * n_blk

    def body(q_ref, k_ref, v_ref, tri_ref, *rest):
        ag_src, rest = rest[:n_ag], rest[n_ag:]
        o_ref, cs_ref = rest[:2]
        ag_out, rest = rest[2:2 + n_ag_out], rest[2 + n_ag_out:]
        oacc, cmat, carry = rest[:3]
        ag_start, ag_forward, ag_finish = _ag_phases(ag_dests, ag_src, ag_out, *rest[3:])
        step = pl.program_id(0) * n_blk + pl.program_id(1)
        pl.when(step == 0)(ag_start)
        pl.when(step == (3 * n_steps) // 4)(ag_forward)
        lane = lax.broadcasted_iota(jnp.int32, (1, LANES), 1)
        ntri = tri_ref[...]
        diag = _diag_mask(tq)

        def q_tile(qi, _):
            r0 = pl.multiple_of(qi * tq, tq)
            qs = [_stack_heads(q_ref[pl.ds(r0, tq), pp * LANES:(pp + 1) * LANES], lane, Q_SCALE)
                  for pp in range(npp)]
            carry[...] = jnp.zeros_like(carry)
            cmat[...] = jnp.zeros_like(cmat)
            oacc[...] = jnp.zeros_like(oacc)

            def run_tiles(tiles):
                blocks = [dict(cur=carry[pp], cm=cmat[pp], pv=None) for pp in range(npp)]
                chains = []
                for kb, mask in tiles:
                    c0 = pl.multiple_of(kb * tq, tq)
                    for pp in range(npp):
                        chains.append(_fwd_chain(blocks[pp], qs[pp], k_ref, v_ref, c0, kb,
                                                 slice(pp * LANES, (pp + 1) * LANES), mask, ntri, lane, tq))
                _emit_skewed(chains)
                for pp in range(npp):
                    oacc[pp] += blocks[pp]["pv"]
                    cmat[pp] = blocks[pp]["cm"]
                    carry[pp] = blocks[pp]["cur"]

            odd = qi % 2

            @pl.when(odd == 0)
            def _():
                run_tiles([(qi, diag)])

            @pl.when(odd == 1)
            def _():
                run_tiles([(qi, diag), (qi - 1, None)])

            def pair(j, _):
                kb = qi - 1 - odd - 2 * j
                run_tiles([(kb, None), (kb - 1, None)])
                return 0

            lax.fori_loop(0, qi // 2, pair, 0)
            for pp in range(npp):
                c_off = 2 * pp * LANES
                cs_ref[pl.ds(r0, tq), c_off:c_off + LANES] = cmat[pp, 0:tq, :]
                cs_ref[pl.ds(r0, tq), c_off + LANES:c_off + 2 * LANES] = cmat[pp, tq:2 * tq, :]
                o_ref[pl.ds(r0, tq), pp * LANES:(pp + 1) * LANES] = jnp.where(
                    lane < HEAD_DIM, oacc[pp, 0:tq, :], oacc[pp, tq:2 * tq, :]).astype(BF16)
            return 0

        lax.fori_loop(0, seq // tq, q_tile, 0)
        pl.when(step == n_steps - 1)(ag_finish)

    wid = npp * LANES
    blk = lambda off: pl.BlockSpec((seq, wid), lambda b, p: (b, off + p))
    any_spec = pl.BlockSpec(memory_space=pl.ANY)
    return pl.pallas_call(
        body, name="attn_fwd", grid=(n_seq, n_blk),
        out_shape=(jax.ShapeDtypeStruct((2, t, n_blk * wid), BF16),
                   jax.ShapeDtypeStruct((t, n_blk * 2 * wid), F32), *ag_out_shapes),
        in_specs=[blk(0), blk(n_blk), blk(2 * n_blk), pl.BlockSpec((2 * tq, tq), lambda b, p: (0, 0))]
        + [any_spec] * n_ag,
        out_specs=(pl.BlockSpec((None, seq, wid), lambda b, p: (0, b, p)),
                   pl.BlockSpec((seq, 2 * wid), lambda b, p: (b, p)), *([any_spec] * n_ag_out)),
        scratch_shapes=[pltpu.VMEM((npp, 2 * tq, LANES), F32), pltpu.VMEM((npp, 2 * tq, LANES), F32),
                        pltpu.VMEM((npp, 2 * tq, 1), F32)] + _ag_scratch(n_ag),
        compiler_params=_params(),
    )(proj, proj, proj, tri_after, *ag_srcs)


def _attn_bwd(proj, dcat, cstats, tri_after, tri_incl, n_seq, seq, rs_sends):
    t = proj.shape[0]
    tq = ATT_TILE
    npp = ATT_PAIRS
    width = proj.shape[1] // 4
    n_blk = width // (npp * LANES)
    n_rs = len(rs_sends)
    rs_shapes = [r.shape for r in rs_sends]
    n_steps = n_seq * n_blk

    def body(q_ref, k_ref, v_ref, do_ref, cs_ref, tria_ref, trii_ref, *rest):
        rs_src, rest = rest[:n_rs], rest[n_rs:]
        out_ref = rest[0]
        rs_dst, rest = rest[1:1 + n_rs], rest[1 + n_rs:]
        dq_acc, dk_acc, dv_acc, ecarry = rest[:4]
        rs_start, rs_finish = _rs_phases(rs_shapes, rs_src, rs_dst, *rest[4:])
        step = pl.program_id(0) * n_blk + pl.program_id(1)
        pl.when(step == 0)(rs_start)
        lane = lax.broadcasted_iota(jnp.int32, (1, LANES), 1)
        ntri = tria_ref[...]
        tri_i = trii_ref[...]
        diag = _diag_mask(tq)
        dk_acc[...] = jnp.zeros_like(dk_acc)
        dv_acc[...] = jnp.zeros_like(dv_acc)

        def q_tile(qi, _):
            r0 = pl.multiple_of(qi * tq, tq)
            qs, dos, cs = [], [], []
            for pp in range(npp):
                cols = slice(pp * LANES, (pp + 1) * LANES)
                qs.append(_stack_heads(q_ref[pl.ds(r0, tq), cols], lane, Q_SCALE))
                dos.append(_stack_heads(do_ref[pl.ds(r0, tq), cols], lane))
                c_off = 2 * pp * LANES
                cs.append(jnp.concatenate([cs_ref[pl.ds(r0, tq), c_off:c_off + LANES],
                                           cs_ref[pl.ds(r0, tq), c_off + LANES:c_off + 2 * LANES]], axis=0))
            ecarry[...] = jnp.zeros_like(ecarry)
            dq_acc[...] = jnp.zeros_like(dq_acc)

            def run_tiles(tiles):
                blocks = [dict(ec=ecarry[pp], dq=None) for pp in range(npp)]
                chains = []
                for kb, mask in tiles:
                    c0 = pl.multiple_of(kb * tq, tq)
                    for pp in range(npp):
                        chains.append(_bwd_chain(
                            blocks[pp], qs[pp], dos[pp], cs[pp], k_ref, v_ref, dk_acc.at[pp], dv_acc.at[pp],
                            c0, kb, slice(pp * LANES, (pp + 1) * LANES), mask, ntri, tri_i, lane, tq))
                _emit_skewed(chains)
                for pp in range(npp):
                    dq_acc[pp] += blocks[pp]["dq"]
                    ecarry[pp] = blocks[pp]["ec"]

            def pair(j, _):
                run_tiles([(2 * j, None), (2 * j + 1, None)])
                return 0

            lax.fori_loop(0, qi // 2, pair, 0)
            odd = qi % 2

            @pl.when(odd == 0)
            def _():
                run_tiles([(qi, diag)])

            @pl.when(odd == 1)
            def _():
                run_tiles([(qi - 1, None), (qi, diag)])

            for pp in range(npp):
                dq = jnp.where(lane < HEAD_DIM, dq_acc[pp, 0:tq, :], dq_acc[pp, tq:2 * tq, :])
                out_ref[0, pl.ds(r0, tq), pp * LANES:(pp + 1) * LANES] = (dq * Q_SCALE).astype(BF16)
            return 0

        lax.fori_loop(0, seq // tq, q_tile, 0)
        for pp in range(npp):
            cols = slice(pp * LANES, (pp + 1) * LANES)
            out_ref[1, :, cols] = dk_acc[pp].astype(BF16)
            out_ref[2, :, cols] = dv_acc[pp].astype(BF16)
        pl.when(step == n_steps - 1)(rs_finish)

    wid = npp * LANES
    blk = lambda off: pl.BlockSpec((seq, wid), lambda b, p: (b, off + p))
    tri_spec = pl.BlockSpec((2 * tq, tq), lambda b, p: (0, 0))
    any_spec = pl.BlockSpec(memory_space=pl.ANY)
    return pl.pallas_call(
        body, name="attn_bwd", grid=(n_seq, n_blk),
        out_shape=(jax.ShapeDtypeStruct((4, t, width), BF16), *_rs_out(rs_sends)),
        in_specs=[blk(0), blk(n_blk), blk(2 * n_blk), pl.BlockSpec((seq, wid), lambda b, p: (b, p)),
                  pl.BlockSpec((seq, 2 * wid), lambda b, p: (b, p)), tri_spec,
                  pl.BlockSpec((tq, tq), lambda b, p: (0, 0))] + [any_spec] * n_rs,
        out_specs=(pl.BlockSpec((3, seq, wid), lambda b, p: (0, b, p)), *([any_spec] * n_rs)),
        scratch_shapes=[pltpu.VMEM((npp, 2 * tq, LANES), F32), pltpu.VMEM((npp, seq, LANES), F32),
                        pltpu.VMEM((npp, seq, LANES), F32), pltpu.VMEM((npp, 2 * tq, 1), F32)]
        + _rs_scratch(rs_sends),
        compiler_params=_params(),
    )(proj, proj, proj, dcat, cstats, tri_after, tri_incl, *rs_sends)


def _window_sum(v, g, rows, forward):
    s_len = v.shape[0]
    s = v
    for step in range(g + 1):
        sh = 1 << step
        if forward:
            s = s + jnp.where(rows < s_len - sh, pltpu.roll(s, s_len - sh, axis=0), 0.0)
        else:
            s = s + jnp.where(rows >= sh, pltpu.roll(s, sh, axis=0), 0.0)
    return s


def _window_count(g, rows):
    return jnp.minimum(rows + 1, POOL_WINDOWS[g]).astype(F32)


def _pooled(u, g, rows):
    return _window_sum(u, g, rows, forward=False) / _window_count(g, rows) - u


def _group_cols(g):
    return slice(g * POOL_GROUP_DIM, (g + 1) * POOL_GROUP_DIM)


def _pool_fwd(proj, w_pool, pool_scale, cat, n_seq, seq):
    n_grp = len(POOL_WINDOWS)
    width = n_grp * POOL_GROUP_DIM
    assert [1 << (g + 1) for g in range(n_grp)] == list(POOL_WINDOWS)

    def body(u_ref, w_ref, s_ref, alias_ref, o_ref):
        del alias_ref
        rows = lax.broadcasted_iota(jnp.int32, (seq, 1), 0)
        for g in range(n_grp):
            cols = _group_cols(g)
            pooled = _pooled(u_ref[:, cols].astype(F32), g, rows)
            y = _dot_nn(pooled.astype(BF16), w_ref[g].astype(BF16))
            o_ref[:, cols] = (y * s_ref[:, cols]).astype(BF16)

    return pl.pallas_call(
        body, name="pool_fwd", grid=(n_seq,),
        out_shape=jax.ShapeDtypeStruct(cat.shape, BF16),
        in_specs=[pl.BlockSpec((seq, width), lambda b: (b, 3)),
                  pl.BlockSpec((n_grp, POOL_GROUP_DIM, POOL_GROUP_DIM), lambda b: (0, 0, 0)),
                  pl.BlockSpec((1, width), lambda b: (0, 0)),
                  pl.BlockSpec(memory_space=pl.ANY)],
        out_specs=pl.BlockSpec((None, seq, width), lambda b: (1, b, 0)),
        input_output_aliases={3: 0},
        compiler_params=_params(),
    )(proj, w_pool, pool_scale, cat)


def _pool_bwd(proj, dcat, w_pool, pool_scale, dqkv, n_seq, seq):
    n_grp = len(POOL_WINDOWS)
    width = n_grp * POOL_GROUP_DIM

    def body(u_ref, dp_ref, w_ref, s_ref, alias_ref, du_ref, gw_ref, gs_ref):
        del alias_ref
        b = pl.program_id(0)
        rows = lax.broadcasted_iota(jnp.int32, (seq, 1), 0)
        for g in range(n_grp):
            cols = _group_cols(g)
            pb = _pooled(u_ref[:, cols].astype(F32), g, rows).astype(BF16)
            wb = w_ref[g].astype(BF16)
            z = _dot_nn(pb, wb)
            dp = dp_ref[:, cols].astype(F32)
            _acc(gs_ref.at[:, cols], _colsum(dp * z), b == 0)
            dys = (dp * s_ref[:, cols]).astype(BF16)
            _acc(gw_ref.at[g], _dot_tn(pb, dys), b == 0)
            dpooled = _dot_nt(dys, wb)
            du = _window_sum(dpooled / _window_count(g, rows), g, rows, forward=True) - dpooled
            du_ref[:, cols] = du.astype(BF16)

    return pl.pallas_call(
        body, name="pool_bwd", grid=(n_seq,),
        out_shape=(jax.ShapeDtypeStruct(dqkv.shape, BF16),
                   jax.ShapeDtypeStruct((n_grp, POOL_GROUP_DIM, POOL_GROUP_DIM), F32),
                   jax.ShapeDtypeStruct((1, width), F32)),
        in_specs=[pl.BlockSpec((seq, width), lambda b: (b, 3)),
                  pl.BlockSpec((seq, width), lambda b: (b, 1)),
                  pl.BlockSpec((n_grp, POOL_GROUP_DIM, POOL_GROUP_DIM), lambda b: (0, 0, 0)),
                  pl.BlockSpec((1, width), lambda b: (0, 0)),
                  pl.BlockSpec(memory_space=pl.ANY)],
        out_specs=(pl.BlockSpec((None, seq, width), lambda b: (3, b, 0)),
                   pl.BlockSpec((n_grp, POOL_GROUP_DIM, POOL_GROUP_DIM), lambda b: (0, 0, 0)),
                   pl.BlockSpec((1, width), lambda b: (0, 0))),
        input_output_aliases={4: 0},
        compiler_params=_params(),
    )(proj, dcat, w_pool, pool_scale, dqkv)


def _cond_fwd(c_all, w_cond, b_cols):
    n, _ = c_all.shape
    cols = w_cond.shape[1]

    def body(c_ref, w_ref, b_ref, o_ref):
        cv = c_ref[...]
        a = cv * jax.nn.sigmoid(cv)
        o_ref[...] = jnp.dot(a, w_ref[...], preferred_element_type=F32,
                             precision=lax.Precision.HIGHEST) + b_ref[...]

    return pl.pallas_call(
        body, name="cond_fwd", out_shape=jax.ShapeDtypeStruct((n, cols), F32),
        compiler_params=_params(),
    )(c_all, w_cond, b_cols)


def _cond_bwd_adamw(c_all, dmod_all, dmod_cols, w, m_w, v_w, b, m_b, v_b):
    def body(c_ref, dm_ref, dmc_ref, w_ref, mw_ref, vw_ref, b_ref, mb_ref, vb_ref,
             gw_ref, dw_ref, nmw_ref, nvw_ref, gb_ref, db_ref, nmb_ref, nvb_ref):
        cv = c_ref[...]
        a = cv * jax.nn.sigmoid(cv)
        gw = lax.dot_general(a, dmc_ref[...], (((0,), (0,)), ((), ())),
                             preferred_element_type=F32, precision=lax.Precision.HIGHEST)
        gw_ref[...] = gw
        dw_ref[...], nmw_ref[...], nvw_ref[...] = _adamw_math(w_ref[...], gw, mw_ref[...], vw_ref[...])
        gb = _colsum(dm_ref[...])
        gb_ref[...] = gb
        db_ref[...], nmb_ref[...], nvb_ref[...] = _adamw_math(b_ref[...], gb, mb_ref[...], vb_ref[...])

    w_sds, b_sds = jax.ShapeDtypeStruct(w.shape, F32), jax.ShapeDtypeStruct(b.shape, F32)
    outs = pl.pallas_call(
        body, name="cond_bwd_adamw", out_shape=(w_sds,) * 4 + (b_sds,) * 4, compiler_params=_params(),
    )(c_all, dmod_all, dmod_cols, w, m_w, v_w, b, m_b, v_b)
    return outs[:4], outs[4:]


def _adamw_math(w, g, m, v):
    m = ADAM_B1 * m + (1.0 - ADAM_B1) * g
    v = ADAM_B2 * v + (1.0 - ADAM_B2) * (g * g)
    m_hat = m / (1.0 - ADAM_B1 ** ADAM_STEP)
    v_hat = v / (1.0 - ADAM_B2 ** ADAM_STEP)
    delta = -ADAM_LR * (m_hat / (jnp.sqrt(v_hat) + ADAM_EPS) + ADAM_WD * w)
    return delta, m, v


def _adamw_small(ws, gparts, ms, vs, name):
    n = len(ws)

    def body(*refs):
        w_r, g_r, m_r, v_r = refs[:n], refs[n:2 * n], refs[2 * n:3 * n], refs[3 * n:4 * n]
        outs = refs[4 * n:]
        for i in range(n):
            g = g_r[i][0]
            for dev in range(1, g_r[i].shape[0]):
                g = g + g_r[i][dev]
            delta, m, v = _adamw_math(w_r[i][...], g, m_r[i][...], v_r[i][...])
            outs[i][...] = g
            outs[n + i][...] = delta
            outs[2 * n + i][...] = m
            outs[3 * n + i][...] = v

    sds = [jax.ShapeDtypeStruct(w.shape, F32) for w in ws]
    return pl.pallas_call(
        body, name=name, out_shape=tuple(sds * 4), compiler_params=_params(),
    )(*ws, *gparts, *ms, *vs)


def kernel(x, c, w_cond, b_cond, g_mix_pre, g_mix_post, w_in, w_pool, pool_scale, w_out, g_ffn_pre, g_ffn_post, w_gate, w_up, w_down, loss_target, m_w_cond, m_b_cond, m_g_mix_pre, m_g_mix_post, m_w_in, m_w_pool, m_pool_scale, m_w_out, m_g_ffn_pre, m_g_ffn_post, m_w_gate, m_w_up, m_w_down, v_w_cond, v_b_cond, v_g_mix_pre, v_g_mix_post, v_w_in, v_w_pool, v_pool_scale, v_w_out, v_g_ffn_pre, v_g_ffn_post, v_w_gate, v_w_up, v_w_down):
    n_seq, seq, d = x.shape
    t = n_seq * seq
    xi, yi, ci = _mesh_pos()
    me = 4 * xi + 2 * yi + ci
    x2 = x.reshape(t, d)
    tgt2 = loss_target.reshape(t, d)
    in_rows = w_in.shape[2]
    out_rows = w_out.shape[1]
    ff_rows = w_gate.shape[2]
    ff = N_DEV * ff_rows
    cond_cols = w_cond.shape[2]

    win_t = w_in[0].T.astype(BF16)
    wout_s = w_out[0].astype(BF16)
    wg_t = w_gate[0].T.astype(BF16)
    wu_t = w_up[0].T.astype(BF16)
    wd_s = w_down[0].astype(BF16)
    (c_all,) = _all_gather([c], [jax.ShapeDtypeStruct((N_DEV, n_seq, d), F32)], [(0, ())], "ag_c")
    c_all = c_all.reshape(N_DEV * n_seq, d)

    b_cols = lax.dynamic_slice_in_dim(b_cond, me * cond_cols, cond_cols, axis=1)
    mod_cols = _cond_fwd(c_all, w_cond[0], b_cols)
    (mod_g,) = _all_gather([mod_cols], [jax.ShapeDtypeStruct((N_DEV,) + mod_cols.shape, F32)], [(0, ())], "ag_mod")
    mod_mine = lax.dynamic_slice_in_dim(mod_g, me * n_seq, n_seq, axis=1)
    mod = jnp.transpose(mod_mine, (1, 0, 2)).reshape(n_seq, N_MOD, d)

    h1, win_g = _pre_mix(x2, g_mix_pre, mod, seq, [win_t],
                         [jax.ShapeDtypeStruct((N_DEV, in_rows, d), BF16)], [(0, ())])
    win_full = win_g.reshape(N_DEV * in_rows, d)
    proj = _matmul(h1, win_full, "nt", BF16, 512, N_DEV * in_rows, d, "proj")
    tq = ATT_TILE
    ids = jnp.arange(tq)
    tri_after = jnp.tile(-(ids[:, None] >= ids[None, :]).astype(BF16), (2, 1))
    tri_incl = (ids[:, None] <= ids[None, :]).astype(BF16)
    attn, cstats, wout_g, wgu_g, wd_g = _attn_fwd(
        proj, tri_after, n_seq, seq, [wout_s, wg_t, wu_t, wd_s],
        [jax.ShapeDtypeStruct((N_DEV, out_rows, d), BF16), jax.ShapeDtypeStruct((2, N_DEV, ff_rows, d), BF16),
         jax.ShapeDtypeStruct((N_DEV, ff_rows, d), BF16)],
        [(0, ()), (1, (0,)), (1, (1,)), (2, ())])
    wout_full = wout_g.reshape(N_DEV * out_rows, d)
    wgu_full = wgu_g.reshape(2, ff, d)
    wd_full = wd_g.reshape(ff, d)
    cat = _pool_fwd(proj, w_pool[0], pool_scale, attn, n_seq, seq)
    tok_f32, tok_bf16 = jax.ShapeDtypeStruct((t, d), F32), jax.ShapeDtypeStruct((t, d), BF16)
    seq_sds, vec_sds = jax.ShapeDtypeStruct((n_seq, 1, d), F32), jax.ShapeDtypeStruct((1, d), F32)
    mix, x1, h2 = _matmul_rows(
        cat, wout_full.reshape(2, d // 2, d), ROW_TILE, seq, "mix_mid", _mid_epilogue,
        [x2, g_mix_post, g_ffn_pre, mod], ["tok", "vec", "vec", "mod"],
        [tok_f32, tok_f32, tok_bf16], ["tok", "tok", "tok"])
    gu, act = _ffn_up(h2, wgu_full, 512, ff // 2)
    loss_sum, dy, df, dgate_f, gg_ffn_post = _matmul_rows(
        act, wd_full, ROW_TILE, seq, "ffn_down_post", _post_epilogue,
        [x1, tgt2, g_ffn_post, mod], ["tok", "tok", "vec", "mod"],
        [jax.ShapeDtypeStruct((1, LANES), F32), tok_f32, tok_bf16, seq_sds, vec_sds],
        ["loss", "tok", "tok", "seq", "vec"])

    dgu = _ffn_act_bwd(df, wd_full, gu, 512, ff // 2)
    gwd, gwd_b = _matmul(act, df, "tn", F32, ff // 2, d // 2, t, "grad_w_down", bf16_copy=True)
    gwgu, gwgu_b = _matmul(dgu, h2, "tn", F32, ff // 2, d // 2, t, "grad_w_gate_up", bf16_copy=True)
    dx1, dmix, dshift_f, dscale_f, dgate_m, gg_ffn_pre, gg_mix_post = _matmul_rows(
        dgu, wgu_full, ROW_TILE, seq, "dh2_bwd_mid", _bwd_mid_epilogue,
        [dy, x1, mix, g_ffn_pre, g_mix_post, mod], ["tok", "tok", "tok", "vec", "vec", "mod"],
        [tok_f32, tok_bf16, seq_sds, seq_sds, seq_sds, vec_sds, vec_sds],
        ["tok", "tok", "seq", "seq", "seq", "vec", "vec"])
    dcat = _matmul(dmix, wout_full, "nt", BF16, 512, d, d, "dcat")
    gwout, gwout_b = _matmul(cat, dmix, "tn", F32, d // 2, d, t, "grad_w_out", bf16_copy=True)
    dqkv, rv_wgu, rv_wd, rv_wout = _attn_bwd(
        proj, dcat, cstats, tri_after, tri_incl, n_seq, seq,
        [gwgu_b.reshape(2, N_DEV, ff_rows, d), gwd_b.reshape(1, N_DEV, ff_rows, d),
         gwout_b.reshape(1, N_DEV, out_rows, d)])
    dproj, gw_pool, gs_pool = _pool_bwd(proj, dcat, w_pool[0], pool_scale, dqkv, n_seq, seq)
    pad_d = lambda v: jnp.pad(v, ((0, 0), (0, d - v.shape[1])))
    n_gw = gw_pool.size // d
    early = jnp.concatenate(
        [gg_mix_post, gg_ffn_pre, gg_ffn_post, pad_d(gs_pool), pad_d(loss_sum), jnp.zeros((3, d), F32),
         gw_pool.reshape(n_gw, d),
         jnp.concatenate([dgate_m, dshift_f, dscale_f, dgate_f], axis=1).reshape(n_seq * 4, d)], axis=0)
    gwin, gwin_b, early_g = _matmul(
        dproj, h1, "tn", F32, d // 2, d, t, "grad_w_in", bf16_copy=True,
        ag=([early], [jax.ShapeDtypeStruct((N_DEV,) + early.shape, F32)], [(0, ())]))
    grad_x, dshift_m, dscale_m, gg_mix_pre, rv_win = _matmul_rows(
        dproj, win_full.reshape(4, d // 2, d), ROW_TILE, seq, "dh1_bwd_pre", _bwd_pre_epilogue,
        [dx1, x2, g_mix_pre, mod], ["tok", "tok", "vec", "mod"],
        [tok_f32, seq_sds, seq_sds, vec_sds], ["tok", "seq", "seq", "vec"],
        rs_sends=[gwin_b.reshape(1, N_DEV, in_rows, d)])


    late = jnp.concatenate([gg_mix_pre, dshift_m.reshape(n_seq, d), dscale_m.reshape(n_seq, d),
                            jnp.zeros((8 - 1 - 2 * n_seq, d), F32)], axis=0)
    (late_g,) = _all_gather([late], [jax.ShapeDtypeStruct((N_DEV,) + late.shape, F32)], [(0, ())], "ag_late")
    loss = jnp.sum(early_g[:, 4, 0]) * (0.5 / d)
    dmod_all = jnp.concatenate(
        [late_g[:, 1:1 + n_seq, None, :], late_g[:, 1 + n_seq:1 + 2 * n_seq, None, :],
         early_g[:, 8 + n_gw:, :].reshape(N_DEV, n_seq, 4, d)], axis=2).reshape(N_DEV * n_seq, N_MOD * d)
    dmod_cols = lax.dynamic_slice_in_dim(dmod_all, me * cond_cols, cond_cols, axis=1)
    o_cond, o_bcond = _cond_bwd_adamw(c_all, dmod_all, dmod_cols, w_cond[0], m_w_cond[0], v_w_cond[0],
                                      b_cond, m_b_cond, v_b_cond)
    o_cond = tuple(o[None] for o in o_cond)

    small_ws = [g_mix_pre, g_mix_post, g_ffn_pre, g_ffn_post, pool_scale, w_pool.reshape(-1, POOL_GROUP_DIM)]
    small_ms = [m_g_mix_pre, m_g_mix_post, m_g_ffn_pre, m_g_ffn_post, m_pool_scale, m_w_pool.reshape(-1, POOL_GROUP_DIM)]
    small_vs = [v_g_mix_pre, v_g_mix_post, v_g_ffn_pre, v_g_ffn_post, v_pool_scale, v_w_pool.reshape(-1, POOL_GROUP_DIM)]
    small_gparts = [late_g[:, 0:1, :], early_g[:, 0:1, :], early_g[:, 1:2, :], early_g[:, 2:3, :],
                    early_g[:, 3:4, :pool_scale.shape[1]],
                    early_g[:, 8:8 + n_gw, :].reshape(N_DEV, -1, POOL_GROUP_DIM)]
    so = _adamw_small(small_ws, small_gparts, small_ms, small_vs, "adamw_small")
    ns = len(small_ws)
    sg, sdl, sm, sv = so[:ns], so[ns:2 * ns], so[2 * ns:3 * ns], so[3 * ns:]
    pool_shape = w_pool.shape
    fix = lambda lst: [lst[0], lst[1], lst[2], lst[3], lst[4], lst[5].reshape(pool_shape)]
    sg, sdl, sm, sv = fix(sg), fix(sdl), fix(sm), fix(sv)


    def reduced(mine, recv, slab, w, m, v, name, transposed=False, transpose=False):
        turn = (lambda u: u.T) if transposed else (lambda u: u)
        outs = _rs_final_adamw(mine, recv, slab, turn(w[0]), turn(m[0]), turn(v[0]), name, transpose)
        return tuple(turn(o)[None] for o in outs)

    o_in = reduced(gwin.reshape(1, N_DEV, in_rows, d), rv_win, 0, w_in, m_w_in, v_w_in, "adamw_w_in",
                   transpose=True)
    o_out = reduced(gwout.reshape(1, N_DEV, out_rows, d), rv_wout, 0, w_out, m_w_out, v_w_out, "adamw_w_out")
    gwgu8 = gwgu.reshape(2, N_DEV, ff_rows, d)
    o_gate = reduced(gwgu8, rv_wgu, 0, w_gate, m_w_gate, v_w_gate, "adamw_w_gate", transposed=True)
    o_up = reduced(gwgu8, rv_wgu, 1, w_up, m_w_up, v_w_up, "adamw_w_up", transposed=True)
    o_down = reduced(gwd.reshape(1, N_DEV, ff_rows, d), rv_wd, 0, w_down, m_w_down, v_w_down, "adamw_w_down")

    def pick(k):
        small_k = [sg, sdl, sm, sv][k]
        return [o_cond[k], o_bcond[k], small_k[0], small_k[1], o_in[k], small_k[5], small_k[4], o_out[k],
                small_k[2], small_k[3], o_gate[k], o_up[k], o_down[k]]

    return (loss, grad_x.reshape(n_seq, seq, d), *pick(0), *pick(1), *pick(2), *pick(3))
```

```python
import functools
import math

import jax
import jax.numpy as jnp
from jax import lax
from jax.experimental import pallas as pl
from jax.experimental.pallas import tpu as pltpu

F32 = jnp.float32
BF16 = jnp.bfloat16
MESH = pl.DeviceIdType.MESH

N_DEV = 8
HEAD_DIM = 64
LANES = 128
POOL_WINDOWS = (2, 4, 8, 16)
POOL_GROUP_DIM = 128
N_MOD = 6
EPS = 1e-6
ATT_TILE = 256
ATT_PAIRS = 2
VMEM_LIMIT = 56 * 1024 * 1024

ADAM_LR = 0.001
ADAM_B1 = 0.9
ADAM_B2 = 0.999
ADAM_EPS = 1e-08
ADAM_WD = 0.01
ADAM_STEP = 10


def _params(**kw):
    return pltpu.CompilerParams(vmem_limit_bytes=VMEM_LIMIT, **kw)


def _dot_nn(a, b):
    return jnp.dot(a, b, preferred_element_type=F32)


def _dot_nt(a, b):
    return lax.dot_general(a, b, (((1,), (1,)), ((), ())), preferred_element_type=F32)


def _dot_tn(a, b):
    return lax.dot_general(a, b, (((0,), (0,)), ((), ())), preferred_element_type=F32)


def _mesh_pos():
    return lax.axis_index("x"), lax.axis_index("y"), lax.axis_index("c")


def _ag_phases(dests, src, outs, send_sems, recv_sems, local_sems):
    n = len(src)
    x, y, c = _mesh_pos()
    me, sibling = (x, y, c), (x, y, 1 - c)
    chips = [(1 - x, y), (x, 1 - y), (1 - x, 1 - y)]

    def slot(i, dev):
        oi, prefix = dests[i]
        px, py, pc = dev
        return outs[oi].at[prefix + (4 * px + 2 * py + pc,)]

    def copy(i, k, block, to, from_src=False):
        return pltpu.make_async_remote_copy(
            src_ref=src[i] if from_src else slot(i, block), dst_ref=slot(i, block),
            send_sem=send_sems.at[i, k], recv_sem=recv_sems.at[i, k],
            device_id=to, device_id_type=MESH)

    def mine(i):
        return pltpu.make_async_copy(src[i], slot(i, me), local_sems.at[i])

    def first(i):
        return [copy(i, 0, me, sibling, from_src=True)] + [
            copy(i, 1 + j, me, (*chip, c), from_src=True) for j, chip in enumerate(chips)]

    def passed(i, j):
        return copy(i, 4 + j, (*chips[j], c), sibling)

    def start():
        for i in range(n):
            mine(i).start()
        for i in range(n):
            for cp in first(i):
                cp.start()

    def forward():
        for j, chip in enumerate(chips):
            for i in range(n):
                copy(i, 1 + j, (*chip, c), me).wait_recv()
                passed(i, j).start()

    def finish():
        for i in range(n):
            copy(i, 0, sibling, me).wait_recv()
            for j, chip in enumerate(chips):
                copy(i, 4 + j, (*chip, 1 - c), me).wait_recv()
        for i in range(n):
            for cp in first(i) + [passed(i, j) for j in range(3)]:
                cp.wait_send()
            mine(i).wait()

    return start, forward, finish


def _ag_scratch(n):
    return [pltpu.SemaphoreType.DMA((n, 7)), pltpu.SemaphoreType.DMA((n, 7)), pltpu.SemaphoreType.DMA((n,))]


def _all_gather(srcs, out_shapes, dests, name):
    n = len(srcs)

    def body(*refs):
        src = refs[:n]
        outs = refs[n:n + len(out_shapes)]
        start, forward, finish = _ag_phases(dests, src, outs, *refs[n + len(out_shapes):])
        start()
        forward()
        finish()

    any_spec = pl.BlockSpec(memory_space=pl.ANY)
    return pl.pallas_call(
        body, name=name,
        out_shape=tuple(out_shapes),
        in_specs=[any_spec] * n,
        out_specs=tuple([any_spec] * len(out_shapes)),
        scratch_shapes=_ag_scratch(n),
    )(*srcs)


def _rs_phases(shapes, src, dst, send_sems, recv_sems):
    x, y, c = _mesh_pos()

    def copies():
        out = []
        n = 0
        for i, shp in enumerate(shapes):
            for m in range(shp[0]):
                for k in range(1, N_DEV):
                    px, py, pc = x ^ (k >> 2), y ^ ((k >> 1) & 1), c ^ (k & 1)
                    out.append(pltpu.make_async_remote_copy(
                        src_ref=src[i].at[m, 4 * px + 2 * py + pc], dst_ref=dst[i].at[m, k - 1],
                        send_sem=send_sems.at[n], recv_sem=recv_sems.at[n],
                        device_id=(px, py, pc), device_id_type=MESH))
                    n += 1
        return out

    def start():
        for cp in copies():
            cp.start()

    def finish():
        for cp in copies():
            cp.wait_send()
        for cp in copies():
            cp.wait_recv()

    return start, finish


def _rs_out(sends):
    return [jax.ShapeDtypeStruct((s.shape[0], N_DEV - 1) + s.shape[2:], s.dtype) for s in sends]


def _rs_scratch(sends):
    total = sum((N_DEV - 1) * s.shape[0] for s in sends)
    return [pltpu.SemaphoreType.DMA((total,)), pltpu.SemaphoreType.DMA((total,))]


def _rs_final_adamw(mine, recv, slab, w, m, v, name, transpose=False):
    _, _, r, cdim = mine.shape
    x, y, c = _mesh_pos()
    me = jnp.reshape(4 * x + 2 * y + c, (1,)).astype(jnp.int32)

    def body(me_ref, p_ref, r_ref, w_ref, m_ref, v_ref, g_ref, d_ref, nm_ref, nv_ref):
        del me_ref
        g = p_ref[...]
        for k in range(N_DEV - 1):
            g = g + r_ref[k].astype(F32)
        if transpose:
            g = g.T
        g_ref[...] = g
        d_ref[...], nm_ref[...], nv_ref[...] = _adamw_math(w_ref[...], g, m_ref[...], v_ref[...])

    full = pl.BlockSpec(w.shape, lambda i, s: (0, 0))
    sds = jax.ShapeDtypeStruct(w.shape, F32)
    return pl.pallas_call(
        body, name=name, out_shape=(sds, sds, sds, sds),
        grid_spec=pltpu.PrefetchScalarGridSpec(
            num_scalar_prefetch=1, grid=(1,),
            in_specs=[pl.BlockSpec((None, None, r, cdim), lambda i, s: (slab, s[0], 0, 0)),
                      pl.BlockSpec((None, N_DEV - 1, r, cdim), lambda i, s: (slab, 0, 0, 0)), full, full, full],
            out_specs=(full, full, full, full)),
        compiler_params=_params(),
    )(me, mine, recv, w, m, v)


def _matmul(a, b, mode, out_dtype, tm, tn, tk, name, bf16_copy=False, rs_sends=(), ag=None):
    ga = a.shape[0] if a.ndim == 3 else None
    gb = b.shape[0] if b.ndim == 3 else None
    a2, b2 = a.shape[-2:], b.shape[-2:]
    if mode == "nn":
        (m, k), n = a2, b2[1]
    elif mode == "nt":
        (m, k), n = a2, b2[0]
    else:
        (k, m), n = a2, b2[1]
    assert m % tm == 0 and n % tn == 0 and k % tk == 0, (name, m, n, k)
    nk = k // tk
    g_n = ga or 1
    batch_out = mode == "tn" and ga is not None
    n_red = nk if batch_out else nk * g_n
    dot = {"nn": _dot_nn, "nt": _dot_nt, "tn": _dot_tn}[mode]
    acc_in_out = out_dtype == F32

    n_rs = len(rs_sends)
    rs_shapes = [r.shape for r in rs_sends]
    ag_srcs, ag_out_shapes, ag_dests = ag if ag is not None else ((), (), ())
    n_ag, n_ag_out = len(ag_srcs), len(ag_out_shapes)
    n_out = 2 if bf16_copy else 1
    assert not bf16_copy or acc_in_out
    assert not (n_rs and n_ag)

    def body(a_ref, b_ref, *rest):
        rs_src, rest = rest[:n_rs], rest[n_rs:]
        ag_src, rest = rest[:n_ag], rest[n_ag:]
        o_ref = rest[0]
        copy_ref = rest[1] if bf16_copy else None
        rs_dst, rest = rest[n_out:n_out + n_rs], rest[n_out + n_rs:]
        ag_out, scratch = rest[:n_ag_out], rest[n_ag_out:]
        first = functools.reduce(jnp.logical_and, [pl.program_id(ax) == 0 for ax in range(4)])
        last = functools.reduce(jnp.logical_and, [pl.program_id(ax) == grid[ax] - 1 for ax in range(4)])
        if n_rs:
            rs_start, rs_finish = _rs_phases(rs_shapes, rs_src, rs_dst, *scratch[-2:])
            pl.when(first)(rs_start)
        if n_ag:
            ag_start, ag_forward, ag_finish = _ag_phases(ag_dests, ag_src, ag_out, *scratch[-3:])
            pl.when(first)(ag_start)
        p = dot(a_ref[...], b_ref[...])
        kk = pl.program_id(3) if batch_out else pl.program_id(2) * nk + pl.program_id(3)
        if n_red == 1:
            o_ref[...] = p.astype(out_dtype)
            if bf16_copy:
                copy_ref[...] = p.astype(BF16)
        else:
            acc = o_ref if acc_in_out else scratch[0]

            @pl.when(kk == 0)
            def _():
                acc[...] = p

            @pl.when(kk > 0)
            def _():
                acc[...] += p

            @pl.when(kk == n_red - 1)
            def _():
                if not acc_in_out:
                    o_ref[...] = acc[...].astype(out_dtype)
                if bf16_copy:
                    copy_ref[...] = acc[...].astype(BF16)

        if n_rs:
            pl.when(last)(rs_finish)
        if n_ag:
            @pl.when(last)
            def _():
                ag_forward()
                ag_finish()

    def order(ids):
        return ids if batch_out else (ids[2], ids[0], ids[1], ids[3])

    def a_idx(*ids):
        g, i, j, kq = order(ids)
        blk = {"nn": (i, kq), "nt": (i, kq), "tn": (kq, i)}[mode]
        return (g,) + blk if ga is not None else blk

    def b_idx(*ids):
        g, i, j, kq = order(ids)
        blk = {"nn": (kq, j), "nt": (j, kq), "tn": (kq, j)}[mode]
        return (g,) + blk if gb is not None else blk

    def o_idx(*ids):
        g, i, j, kq = order(ids)
        return (g, i, j) if batch_out else (i, j)

    a_blk = {"nn": (tm, tk), "nt": (tm, tk), "tn": (tk, tm)}[mode]
    b_blk = {"nn": (tk, tn), "nt": (tn, tk), "tn": (tk, tn)}[mode]
    if ga is not None:
        a_blk = (None,) + a_blk
    if gb is not None:
        b_blk = (None,) + b_blk
    if batch_out:
        out_shape = jax.ShapeDtypeStruct((g_n, m, n), out_dtype)
        o_blk = (None, tm, tn)
        grid = (g_n, m // tm, n // tn, nk)
    else:
        out_shape = jax.ShapeDtypeStruct((m, n), out_dtype)
        o_blk = (tm, tn)
        grid = (m // tm, n // tn, g_n, nk)
    scratch = [] if (acc_in_out or n_red == 1) else [pltpu.VMEM((tm, tn), F32)]
    any_spec = pl.BlockSpec(memory_space=pl.ANY)
    out_shapes = [out_shape] + ([jax.ShapeDtypeStruct(out_shape.shape, BF16)] if bf16_copy else [])
    res = pl.pallas_call(
        body, name=name, out_shape=tuple(out_shapes + _rs_out(rs_sends) + list(ag_out_shapes)), grid=grid,
        in_specs=[pl.BlockSpec(a_blk, a_idx), pl.BlockSpec(b_blk, b_idx)] + [any_spec] * (n_rs + n_ag),
        out_specs=tuple([pl.BlockSpec(o_blk, o_idx)] * n_out + [any_spec] * (n_rs + n_ag_out)),
        scratch_shapes=scratch + (_rs_scratch(rs_sends) if n_rs else []) + (_ag_scratch(n_ag) if n_ag else []),
        compiler_params=_params(),
    )(a, b, *rs_sends, *ag_srcs)
    return res if len(res) > 1 else res[0]


EW_TILE = 256
ROW_TILE = 512
EPILOGUE_CHUNKS = 8
MXU_WIDTH = 256


def _rms(v):
    return lax.rsqrt(jnp.mean(v * v, axis=-1, keepdims=True) + EPS)


def _rms_bwd(dhat, vh, r):
    return r * (dhat - vh * jnp.mean(dhat * vh, axis=-1, keepdims=True))


def _tok_spec(tm, d):
    return pl.BlockSpec((tm, d), lambda i: (i, 0))


def _vec_spec(d):
    return pl.BlockSpec((1, d), lambda i: (0, 0))


def _mod_spec(tiles_per_seq, d):
    return pl.BlockSpec((None, N_MOD, d), lambda i: (i // tiles_per_seq, 0, 0))


def _seq_acc_spec(tiles_per_seq, d):
    return pl.BlockSpec((None, 1, d), lambda i: (i // tiles_per_seq, 0, 0))


def _acc(ref, val, first):
    if first is False:
        ref[...] += val
        return

    @pl.when(first)
    def _():
        ref[...] = val

    @pl.when(jnp.logical_not(first))
    def _():
        ref[...] += val


def _colsum(v):
    return jnp.sum(v, axis=0, keepdims=True)


def _pre_mix(x2, g_pre, mod, seq):
    t, d = x2.shape
    tm = EW_TILE

    def body(x_ref, g_ref, mod_ref, h_ref):
        xv = x_ref[...]
        n = xv * _rms(xv) * g_ref[...]
        h_ref[...] = (n * (1.0 + mod_ref[1:2, :]) + mod_ref[0:1, :]).astype(BF16)

    return pl.pallas_call(
        body, name="pre_mix", out_shape=jax.ShapeDtypeStruct((t, d), BF16), grid=(t // tm,),
        in_specs=[_tok_spec(tm, d), _vec_spec(d), _mod_spec(seq // tm, d)],
        out_specs=_tok_spec(tm, d), compiler_params=_params(),
    )(x2, g_pre, mod)


def _matmul_rows(a, b, tm, seq, name, epilogue, ep_in, ep_in_kinds, ep_out, ep_out_kinds, rs_sends=()):
    g_n = a.shape[0] if a.ndim == 3 else None
    (m, k), n = a.shape[-2:], b.shape[-1]
    tps = seq // tm
    n_i = m // tm
    n_rs = len(rs_sends)
    rs_shapes = [r.shape for r in rs_sends]
    n_in, n_out = len(ep_in), len(ep_out)
    n_cols = n // MXU_WIDTH
    rc, cw = tm // EPILOGUE_CHUNKS, n // n_cols

    def prev(i):
        return jnp.maximum(i - 1, 0)

    def spec(kind):
        return {"tok": pl.BlockSpec((tm, n), lambda i: (prev(i), 0)),
                "vec": pl.BlockSpec((1, n), lambda i: (0, 0)),
                "mod": pl.BlockSpec((None, N_MOD, n), lambda i: (prev(i) // tps, 0, 0)),
                "seq": pl.BlockSpec((None, 1, n), lambda i: (prev(i) // tps, 0, 0)),
                "loss": pl.BlockSpec((1, LANES), lambda i: (0, 0))}[kind]

    def body(a_ref, b_ref, *rest):
        in_refs, rest = rest[:n_in], rest[n_in:]
        rs_src, rest = rest[:n_rs], rest[n_rs:]
        out_refs, rest = rest[:n_out], rest[n_out:]
        rs_dst, rest = rest[:n_rs], rest[n_rs:]
        fin = rest[0]
        i = pl.program_id(0)
        if n_rs:
            rs_start, rs_finish = _rs_phases(rs_shapes, rs_src, rs_dst, *rest[1:])
            pl.when(i == 0)(rs_start)

        def product(cols):
            if g_n is None:
                return _dot_nn(a_ref[...], b_ref[:, cols])
            p = _dot_nn(a_ref[0], b_ref[0, :, cols])
            for g in range(1, g_n):
                p = p + _dot_nn(a_ref[g], b_ref[g, :, cols])
            return p

        def step(with_epilogue, with_matmul):
            parts = []
            for c in range(EPILOGUE_CHUNKS):
                if with_epilogue:
                    rows = pl.ds(c * rc, rc)
                    epilogue(fin[rows, :], i - 1, tps, in_refs, out_refs, rows, c)
                while with_matmul and len(parts) < (c + 1) * n_cols // EPILOGUE_CHUNKS:
                    cols = slice(len(parts) * cw, (len(parts) + 1) * cw)
                    parts.append((cols, product(cols)))
            for cols, v in parts:
                fin[:, cols] = v

        pl.when(i == 0)(functools.partial(step, False, True))
        pl.when(jnp.logical_and(i > 0, i < n_i))(functools.partial(step, True, True))
        pl.when(i == n_i)(functools.partial(step, True, False))

        if n_rs:
            pl.when(i == n_i)(rs_finish)

    def row(i):
        return jnp.minimum(i, n_i - 1)

    if g_n is None:
        a_spec = pl.BlockSpec((tm, k), lambda i: (row(i), 0))
        b_spec = pl.BlockSpec(b.shape, lambda i: (0, 0), pipeline_mode=pl.Buffered(1))
    else:
        a_spec = pl.BlockSpec((g_n, tm, k), lambda i: (0, row(i), 0))
        b_spec = pl.BlockSpec(b.shape, lambda i: (0, 0, 0), pipeline_mode=pl.Buffered(1))
    any_spec = pl.BlockSpec(memory_space=pl.ANY)
    res = pl.pallas_call(
        body, name=name, grid=(n_i + 1,), out_shape=tuple(list(ep_out) + _rs_out(rs_sends)),
        in_specs=[a_spec, b_spec] + [spec(kd) for kd in ep_in_kinds] + [any_spec] * n_rs,
        out_specs=tuple([spec(kd) for kd in ep_out_kinds] + [any_spec] * n_rs),
        scratch_shapes=[pltpu.VMEM((tm, n), F32)] + (_rs_scratch(rs_sends) if n_rs else []),
        compiler_params=_params(),
    )(a, b, *ep_in, *rs_sends)
    return res


def _first(cond, chunk):
    return cond if chunk == 0 else False


def _mid_epilogue(mv, i, tps, in_refs, out_refs, rows, chunk):
    x_ref, gpost_ref, gpre_ref, mod_ref = in_refs
    mix_ref, x1_ref, h2_ref = out_refs
    mix_ref[rows, :] = mv
    x1 = x_ref[rows, :] + mod_ref[2:3, :] * (mv * _rms(mv) * gpost_ref[...])
    x1_ref[rows, :] = x1
    n = x1 * _rms(x1) * gpre_ref[...]
    h2_ref[rows, :] = (n * (1.0 + mod_ref[4:5, :]) + mod_ref[3:4, :]).astype(BF16)


def _post_epilogue(fv, i, tps, in_refs, out_refs, rows, chunk):
    x1_ref, tgt_ref, g_ref, mod_ref = in_refs
    loss_ref, dy_ref, df_ref, dgate_ref, gg_ref = out_refs
    d = fv.shape[1]
    r = _rms(fv)
    fh = fv * r
    nf = fh * g_ref[...]
    gate = mod_ref[5:6, :]
    err = x1_ref[rows, :] + gate * nf - tgt_ref[rows, :]
    _acc(loss_ref, jnp.sum(_colsum(err * err), axis=1, keepdims=True) * jnp.ones((1, LANES), F32),
         _first(i == 0, chunk))
    dy = err * (1.0 / d)
    dy_ref[rows, :] = dy
    _acc(dgate_ref, _colsum(dy * nf), _first(i % tps == 0, chunk))
    dn = dy * gate
    _acc(gg_ref, _colsum(dn * fh), _first(i == 0, chunk))
    df_ref[rows, :] = _rms_bwd(dn * g_ref[...], fh, r).astype(BF16)


def _bwd_mid_epilogue(dh, i, tps, in_refs, out_refs, rows, chunk):
    dy_ref, x1_ref, mix_ref, gpre_ref, gpost_ref, mod_ref = in_refs
    dx1_ref, dmix_ref, dshift_ref, dscale_ref, dgate_ref, ggpre_ref, ggpost_ref = out_refs
    seq_first, first = _first(i % tps == 0, chunk), _first(i == 0, chunk)
    x1 = x1_ref[rows, :]
    r = _rms(x1)
    xh = x1 * r
    gpre = gpre_ref[...]
    _acc(dshift_ref, _colsum(dh), seq_first)
    _acc(dscale_ref, _colsum(dh * xh * gpre), seq_first)
    dn = dh * (1.0 + mod_ref[4:5, :])
    _acc(ggpre_ref, _colsum(dn * xh), first)
    dx1 = dy_ref[rows, :] + _rms_bwd(dn * gpre, xh, r)
    dx1_ref[rows, :] = dx1
    mv = mix_ref[rows, :]
    rm = _rms(mv)
    mh = mv * rm
    gpost = gpost_ref[...]
    _acc(dgate_ref, _colsum(dx1 * mh * gpost), seq_first)
    dnm = dx1 * mod_ref[2:3, :]
    _acc(ggpost_ref, _colsum(dnm * mh), first)
    dmix_ref[rows, :] = _rms_bwd(dnm * gpost, mh, rm).astype(BF16)


def _bwd_pre_epilogue(dh, i, tps, in_refs, out_refs, rows, chunk):
    dx1_ref, x_ref, g_ref, mod_ref = in_refs
    gx_ref, dshift_ref, dscale_ref, gg_ref = out_refs
    seq_first = _first(i % tps == 0, chunk)
    xv = x_ref[rows, :]
    r = _rms(xv)
    xh = xv * r
    g = g_ref[...]
    _acc(dshift_ref, _colsum(dh), seq_first)
    _acc(dscale_ref, _colsum(dh * xh * g), seq_first)
    dn = dh * (1.0 + mod_ref[1:2, :])
    _acc(gg_ref, _colsum(dn * xh), _first(i == 0, chunk))
    gx_ref[rows, :] = dx1_ref[rows, :] + _rms_bwd(dn * g, xh, r)


def _ffn_up(h2, wgu, tm, tn):
    t, d = h2.shape
    f = wgu.shape[1]

    def body(h_ref, w_ref, gu_ref, act_ref):
        h = h_ref[...]
        g = _dot_nt(h, w_ref[0])
        u = _dot_nt(h, w_ref[1])
        gu_ref[0] = g.astype(BF16)
        gu_ref[1] = u.astype(BF16)
        act_ref[...] = (g * jax.nn.sigmoid(g) * u).astype(BF16)

    return pl.pallas_call(
        body, name="ffn_up", grid=(f // tn, t // tm),
        out_shape=(jax.ShapeDtypeStruct((2, t, f), BF16), jax.ShapeDtypeStruct((t, f), BF16)),
        in_specs=[pl.BlockSpec((tm, d), lambda j, i: (i, 0)), pl.BlockSpec((2, tn, d), lambda j, i: (0, j, 0))],
        out_specs=(pl.BlockSpec((2, tm, tn), lambda j, i: (0, i, j)), pl.BlockSpec((tm, tn), lambda j, i: (i, j))),
        compiler_params=_params(),
    )(h2, wgu)


def _ffn_act_bwd(df, wd, gu, tm, tn):
    t, d = df.shape
    f = wd.shape[0]

    def body(df_ref, w_ref, gu_ref, dgu_ref):
        da = _dot_nt(df_ref[...], w_ref[...])
        g = gu_ref[0].astype(F32)
        u = gu_ref[1].astype(F32)
        s = jax.nn.sigmoid(g)
        silu = g * s
        dgu_ref[0] = (da * u * (s + silu * (1.0 - s))).astype(BF16)
        dgu_ref[1] = (da * silu).astype(BF16)

    return pl.pallas_call(
        body, name="ffn_act_bwd", grid=(f // tn, t // tm),
        out_shape=jax.ShapeDtypeStruct((2, t, f), BF16),
        in_specs=[pl.BlockSpec((tm, d), lambda j, i: (i, 0)), pl.BlockSpec((tn, d), lambda j, i: (j, 0)),
                  pl.BlockSpec((2, tm, tn), lambda j, i: (0, i, j))],
        out_specs=pl.BlockSpec((2, tm, tn), lambda j, i: (0, i, j)),
        compiler_params=_params(),
    )(df, wd, gu)


SIGN_BIT = 0x80000000
Q_SCALE = 1.0 / math.sqrt(HEAD_DIM)


def _softplus(z):
    neg_abs = lax.bitcast_convert_type(lax.bitcast_convert_type(z, jnp.uint32) | jnp.uint32(SIGN_BIT), F32)
    return jnp.maximum(z, 0.0) + jnp.log(1.0 + jnp.exp(neg_abs))


def _hi_lo(v):
    hi = v.astype(BF16)
    return jnp.concatenate([hi, (v - hi.astype(F32)).astype(BF16)], axis=1)


def _emit_skewed(chains, lag=1):
    for t in range(max(len(ch) for ch in chains) + lag * (len(chains) - 1)):
        for c, ch in enumerate(chains):
            if 0 <= t - lag * c < len(ch):
                ch[t - lag * c]()


def _fwd_chain(blk, qs, k_ref, v_ref, c0, kb, cols, mask, ntri, lane, tq):
    st = {}

    def scores():
        st["z"] = _dot_nt(qs, k_ref[pl.ds(c0, tq), cols])

    def soft():
        sp = _softplus(st["z"])
        if mask is not None:
            sp = jnp.where(mask, sp, 0.0)
        st["parts"] = _hi_lo(sp)
        st["cur"] = blk["cur"]
        blk["cm"] = jnp.where(lane == kb, blk["cur"], blk["cm"])
        blk["cur"] = blk["cur"] - jnp.sum(sp, axis=1, keepdims=True)

    def sums():
        st["s"] = _dot_nn(st["parts"], ntri)

    def weights():
        w = jnp.exp(st["z"] + st["s"] + st["cur"])
        if mask is not None:
            w = jnp.where(mask, w, 0.0)
        st["w"] = w.astype(BF16)

    def out():
        p = _dot_nn(st["w"], v_ref[pl.ds(c0, tq), cols])
        blk["pv"] = p if blk["pv"] is None else blk["pv"] + p

    return [scores, soft, sums, weights, out]


def _bwd_chain(blk, qs, dos, cs, k_ref, v_ref, dk_ref, dv_ref, c0, kb, cols, mask, ntri, tri_i, lane, tq):
    st = {}

    def scores():
        st["z"] = _dot_nt(qs, k_ref[pl.ds(c0, tq), cols])
        st["dw"] = _dot_nt(dos, v_ref[pl.ds(c0, tq), cols])

    def soft():
        sp = _softplus(st["z"])
        if mask is not None:
            sp = jnp.where(mask, sp, 0.0)
        st["sp"] = sp
        st["parts"] = _hi_lo(sp)
        st["cur"] = jnp.sum(jnp.where(lane == kb, cs, 0.0), axis=1, keepdims=True)

    def sums():
        st["s"] = _dot_nn(st["parts"], ntri)

    def weights():
        w = jnp.exp(st["z"] + st["s"] + st["cur"])
        if mask is not None:
            w = jnp.where(mask, w, 0.0)
        ee = w * st["dw"]
        st["w"], st["ee"], st["ec"] = w.astype(BF16), ee, blk["ec"]
        blk["ec"] = blk["ec"] + jnp.sum(ee, axis=1, keepdims=True)

    def prefix():
        st["einc"] = _dot_nn(st["ee"].astype(BF16), tri_i)

    def dz():
        v = st["ee"] - jnp.exp(st["z"] - st["sp"]) * (st["einc"] + st["ec"])
        if mask is not None:
            v = jnp.where(mask, v, 0.0)
        st["dz"] = v.astype(BF16)

    def grads():
        p = _dot_nn(st["dz"], k_ref[pl.ds(c0, tq), cols])
        blk["dq"] = p if blk["dq"] is None else blk["dq"] + p
        dk_ref[pl.ds(c0, tq), :] += _dot_tn(st["dz"], qs)
        dv_ref[pl.ds(c0, tq), :] += _dot_tn(st["w"], dos)

    return [scores, soft, sums, weights, prefix, dz, grads]


def _stack_heads(v, lane, scale=None):
    if scale is not None:
        v = v * jnp.asarray(scale, v.dtype)
    zero = jnp.zeros_like(v)
    return jnp.concatenate([jnp.where(lane < HEAD_DIM, v, zero), jnp.where(lane >= HEAD_DIM, v, zero)], axis=0)


def _diag_mask(tq):
    row = lax.broadcasted_iota(jnp.int32, (2 * tq, tq), 0)
    col = lax.broadcasted_iota(jnp.int32, (2 * tq, tq), 1)
    return col < jnp.where(row >= tq, row - tq, row)


def _attn_fwd(proj, tri_after, n_seq, seq, ag_srcs, ag_out_shapes, ag_dests):
    t = proj.shape[0]
    tq = ATT_TILE
    npp = ATT_PAIRS
    n_blk = (proj.shape[1] // 4) // (npp * LANES)
    n_ag, n_ag_out = len(ag_srcs), len(ag_out_shapes)
    n_steps = n_seq * n_blk

    def body(q_ref, k_ref, v_ref, tri_ref, *rest):
        ag_src, rest = rest[:n_ag], rest[n_ag:]
        o_ref, cs_ref = rest[:2]
        ag_out, rest = rest[2:2 + n_ag_out], rest[2 + n_ag_out:]
        oacc, cmat, carry = rest[:3]
        ag_start, ag_forward, ag_finish = _ag_phases(ag_dests, ag_src, ag_out, *rest[3:])
        step = pl.program_id(0) * n_blk + pl.program_id(1)
        pl.when(step == 0)(ag_start)
        pl.when(step == (3 * n_steps) // 4)(ag_forward)
        lane = lax.broadcasted_iota(jnp.int32, (1, LANES), 1)
        ntri = tri_ref[...]
        diag = _diag_mask(tq)

        def q_tile(qi, _):
            r0 = pl.multiple_of(qi * tq, tq)
            qs = [_stack_heads(q_ref[pl.ds(r0, tq), pp * LANES:(pp + 1) * LANES], lane, Q_SCALE)
                  for pp in range(npp)]
            carry[...] = jnp.zeros_like(carry)
            cmat[...] = jnp.zeros_like(cmat)
            oacc[...] = jnp.zeros_like(oacc)

            def run_tiles(tiles):
                blocks = [dict(cur=carry[pp], cm=cmat[pp], pv=None) for pp in range(npp)]
                chains = []
                for kb, mask in tiles:
                    c0 = pl.multiple_of(kb * tq, tq)
                    for pp in range(npp):
                        chains.append(_fwd_chain(blocks[pp], qs[pp], k_ref, v_ref, c0, kb,
                                                 slice(pp * LANES, (pp + 1) * LANES), mask, ntri, lane, tq))
                _emit_skewed(chains)
                for pp in range(npp):
                    oacc[pp] += blocks[pp]["pv"]
                    cmat[pp] = blocks[pp]["cm"]
                    carry[pp] = blocks[pp]["cur"]

            odd = qi % 2

            @pl.when(odd == 0)
            def _():
                run_tiles([(qi, diag)])

            @pl.when(odd == 1)
            def _():
                run_tiles([(qi, diag), (qi - 1, None)])

            def pair(j, _):
                kb = qi - 1 - odd - 2 * j
                run_tiles([(kb, None), (kb - 1, None)])
                return 0

            lax.fori_loop(0, qi // 2, pair, 0)
            for pp in range(npp):
                c_off = 2 * pp * LANES
                cs_ref[pl.ds(r0, tq), c_off:c_off + LANES] = cmat[pp, 0:tq, :]
                cs_ref[pl.ds(r0, tq), c_off + LANES:c_off + 2 * LANES] = cmat[pp, tq:2 * tq, :]
                o_ref[pl.ds(r0, tq), pp * LANES:(pp + 1) * LANES] = jnp.where(
                    lane < HEAD_DIM, oacc[pp, 0:tq, :], oacc[pp, tq:2 * tq, :]).astype(BF16)
            return 0

        lax.fori_loop(0, seq // tq, q_tile, 0)
        pl.when(step == n_steps - 1)(ag_finish)

    wid = npp * LANES
    blk = lambda off: pl.BlockSpec((seq, wid), lambda b, p: (b, off + p))
    any_spec = pl.BlockSpec(memory_space=pl.ANY)
    return pl.pallas_call(
        body, name="attn_fwd", grid=(n_seq, n_blk),
        out_shape=(jax.ShapeDtypeStruct((2, t, n_blk * wid), BF16),
                   jax.ShapeDtypeStruct((t, n_blk * 2 * wid), F32), *ag_out_shapes),
        in_specs=[blk(0), blk(n_blk), blk(2 * n_blk), pl.BlockSpec((2 * tq, tq), lambda b, p: (0, 0))]
        + [any_spec] * n_ag,
        out_specs=(pl.BlockSpec((None, seq, wid), lambda b, p: (0, b, p)),
                   pl.BlockSpec((seq, 2 * wid), lambda b, p: (b, p)), *([any_spec] * n_ag_out)),
        scratch_shapes=[pltpu.VMEM((npp, 2 * tq, LANES), F32), pltpu.VMEM((npp, 2 * tq, LANES), F32),
                        pltpu.VMEM((npp, 2 * tq, 1), F32)] + _ag_scratch(n_ag),
        compiler_params=_params(),
    )(proj, proj, proj, tri_after, *ag_srcs)


def _attn_bwd(proj, dcat, cstats, tri_after, tri_incl, n_seq, seq, rs_sends):
    t = proj.shape[0]
    tq = ATT_TILE
    npp = ATT_PAIRS
    width = proj.shape[1] // 4
    n_blk = width // (npp * LANES)
    n_rs = len(rs_sends)
    rs_shapes = [r.shape for r in rs_sends]
    n_steps = n_seq * n_blk

    def body(q_ref, k_ref, v_ref, do_ref, cs_ref, tria_ref, trii_ref, *rest):
        rs_src, rest = rest[:n_rs], rest[n_rs:]
        out_ref = rest[0]
        rs_dst, rest = rest[1:1 + n_rs], rest[1 + n_rs:]
        dq_acc, dk_acc, dv_acc, ecarry = rest[:4]
        rs_start, rs_finish = _rs_phases(rs_shapes, rs_src, rs_dst, *rest[4:])
        step = pl.program_id(0) * n_blk + pl.program_id(1)
        pl.when(step == 0)(rs_start)
        lane = lax.broadcasted_iota(jnp.int32, (1, LANES), 1)
        ntri = tria_ref[...]
        tri_i = trii_ref[...]
        diag = _diag_mask(tq)
        dk_acc[...] = jnp.zeros_like(dk_acc)
        dv_acc[...] = jnp.zeros_like(dv_acc)

        def q_tile(qi, _):
            r0 = pl.multiple_of(qi * tq, tq)
            qs, dos, cs = [], [], []
            for pp in range(npp):
                cols = slice(pp * LANES, (pp + 1) * LANES)
                qs.append(_stack_heads(q_ref[pl.ds(r0, tq), cols], lane, Q_SCALE))
                dos.append(_stack_heads(do_ref[pl.ds(r0, tq), cols], lane))
                c_off = 2 * pp * LANES
                cs.append(jnp.concatenate([cs_ref[pl.ds(r0, tq), c_off:c_off + LANES],
                                           cs_ref[pl.ds(r0, tq), c_off + LANES:c_off + 2 * LANES]], axis=0))
            ecarry[...] = jnp.zeros_like(ecarry)
            dq_acc[...] = jnp.zeros_like(dq_acc)

            def run_tiles(tiles):
                blocks = [dict(ec=ecarry[pp], dq=None) for pp in range(npp)]
                chains = []
                for kb, mask in tiles:
                    c0 = pl.multiple_of(kb * tq, tq)
                    for pp in range(npp):
                        chains.append(_bwd_chain(
                            blocks[pp], qs[pp], dos[pp], cs[pp], k_ref, v_ref, dk_acc.at[pp], dv_acc.at[pp],
                            c0, kb, slice(pp * LANES, (pp + 1) * LANES), mask, ntri, tri_i, lane, tq))
                _emit_skewed(chains)
                for pp in range(npp):
                    dq_acc[pp] += blocks[pp]["dq"]
                    ecarry[pp] = blocks[pp]["ec"]

            def pair(j, _):
                run_tiles([(2 * j, None), (2 * j + 1, None)])
                return 0

            lax.fori_loop(0, qi // 2, pair, 0)
            odd = qi % 2

            @pl.when(odd == 0)
            def _():
                run_tiles([(qi, diag)])

            @pl.when(odd == 1)
            def _():
                run_tiles([(qi - 1, None), (qi, diag)])

            for pp in range(npp):
                dq = jnp.where(lane < HEAD_DIM, dq_acc[pp, 0:tq, :], dq_acc[pp, tq:2 * tq, :])
                out_ref[0, pl.ds(r0, tq), pp * LANES:(pp + 1) * LANES] = (dq * Q_SCALE).astype(BF16)
            return 0

        lax.fori_loop(0, seq // tq, q_tile, 0)
        for pp in range(npp):
            cols = slice(pp * LANES, (pp + 1) * LANES)
            out_ref[1, :, cols] = dk_acc[pp].astype(BF16)
            out_ref[2, :, cols] = dv_acc[pp].astype(BF16)
        pl.when(step == n_steps - 1)(rs_finish)

    wid = npp * LANES
    blk = lambda off: pl.BlockSpec((seq, wid), lambda b, p: (b, off + p))
    tri_spec = pl.BlockSpec((2 * tq, tq), lambda b, p: (0, 0))
    any_spec = pl.BlockSpec(memory_space=pl.ANY)
    return pl.pallas_call(
        body, name="attn_bwd", grid=(n_seq, n_blk),
        out_shape=(jax.ShapeDtypeStruct((4, t, width), BF16), *_rs_out(rs_sends)),
        in_specs=[blk(0), blk(n_blk), blk(2 * n_blk), pl.BlockSpec((seq, wid), lambda b, p: (b, p)),
                  pl.BlockSpec((seq, 2 * wid), lambda b, p: (b, p)), tri_spec,
                  pl.BlockSpec((tq, tq), lambda b, p: (0, 0))] + [any_spec] * n_rs,
        out_specs=(pl.BlockSpec((3, seq, wid), lambda b, p: (0, b, p)), *([any_spec] * n_rs)),
        scratch_shapes=[pltpu.VMEM((npp, 2 * tq, LANES), F32), pltpu.VMEM((npp, seq, LANES), F32),
                        pltpu.VMEM((npp, seq, LANES), F32), pltpu.VMEM((npp, 2 * tq, 1), F32)]
        + _rs_scratch(rs_sends),
        compiler_params=_params(),
    )(proj, proj, proj, dcat, cstats, tri_after, tri_incl, *rs_sends)


def _window_sum(v, g, rows, forward):
    s_len = v.shape[0]
    s = v
    for step in range(g + 1):
        sh = 1 << step
        if forward:
            s = s + jnp.where(rows < s_len - sh, pltpu.roll(s, s_len - sh, axis=0), 0.0)
        else:
            s = s + jnp.where(rows >= sh, pltpu.roll(s, sh, axis=0), 0.0)
    return s


def _window_count(g, rows):
    return jnp.minimum(rows + 1, POOL_WINDOWS[g]).astype(F32)


def _pooled(u, g, rows):
    return _window_sum(u, g, rows, forward=False) / _window_count(g, rows) - u


def _group_cols(g):
    return slice(g * POOL_GROUP_DIM, (g + 1) * POOL_GROUP_DIM)


def _pool_fwd(proj, w_pool, pool_scale, cat, n_seq, seq):
    n_grp = len(POOL_WINDOWS)
    width = n_grp * POOL_GROUP_DIM
    assert [1 << (g + 1) for g in range(n_grp)] == list(POOL_WINDOWS)

    def body(u_ref, w_ref, s_ref, alias_ref, o_ref):
        del alias_ref
        rows = lax.broadcasted_iota(jnp.int32, (seq, 1), 0)
        for g in range(n_grp):
            cols = _group_cols(g)
            pooled = _pooled(u_ref[:, cols].astype(F32), g, rows)
            y = _dot_nn(pooled.astype(BF16), w_ref[g].astype(BF16))
            o_ref[:, cols] = (y * s_ref[:, cols]).astype(BF16)

    return pl.pallas_call(
        body, name="pool_fwd", grid=(n_seq,),
        out_shape=jax.ShapeDtypeStruct(cat.shape, BF16),
        in_specs=[pl.BlockSpec((seq, width), lambda b: (b, 3)),
                  pl.BlockSpec((n_grp, POOL_GROUP_DIM, POOL_GROUP_DIM), lambda b: (0, 0, 0)),
                  pl.BlockSpec((1, width), lambda b: (0, 0)),
                  pl.BlockSpec(memory_space=pl.ANY)],
        out_specs=pl.BlockSpec((None, seq, width), lambda b: (1, b, 0)),
        input_output_aliases={3: 0},
        compiler_params=_params(),
    )(proj, w_pool, pool_scale, cat)


def _pool_bwd(proj, dcat, w_pool, pool_scale, dqkv, n_seq, seq):
    n_grp = len(POOL_WINDOWS)
    width = n_grp * POOL_GROUP_DIM

    def body(u_ref, dp_ref, w_ref, s_ref, alias_ref, du_ref, gw_ref, gs_ref):
        del alias_ref
        b = pl.program_id(0)
        rows = lax.broadcasted_iota(jnp.int32, (seq, 1), 0)
        for g in range(n_grp):
            cols = _group_cols(g)
            pb = _pooled(u_ref[:, cols].astype(F32), g, rows).astype(BF16)
            wb = w_ref[g].astype(BF16)
            z = _dot_nn(pb, wb)
            dp = dp_ref[:, cols].astype(F32)
            _acc(gs_ref.at[:, cols], _colsum(dp * z), b == 0)
            dys = (dp * s_ref[:, cols]).astype(BF16)
            _acc(gw_ref.at[g], _dot_tn(pb, dys), b == 0)
            dpooled = _dot_nt(dys, wb)
            du = _window_sum(dpooled / _window_count(g, rows), g, rows, forward=True) - dpooled
            du_ref[:, cols] = du.astype(BF16)

    return pl.pallas_call(
        body, name="pool_bwd", grid=(n_seq,),
        out_shape=(jax.ShapeDtypeStruct(dqkv.shape, BF16),
                   jax.ShapeDtypeStruct((n_grp, POOL_GROUP_DIM, POOL_GROUP_DIM), F32),
                   jax.ShapeDtypeStruct((1, width), F32)),
        in_specs=[pl.BlockSpec((seq, width), lambda b: (b, 3)),
                  pl.BlockSpec((seq, width), lambda b: (b, 1)),
                  pl.BlockSpec((n_grp, POOL_GROUP_DIM, POOL_GROUP_DIM), lambda b: (0, 0, 0)),
                  pl.BlockSpec((1, width), lambda b: (0, 0)),
                  pl.BlockSpec(memory_space=pl.ANY)],
        out_specs=(pl.BlockSpec((None, seq, width), lambda b: (3, b, 0)),
                   pl.BlockSpec((n_grp, POOL_GROUP_DIM, POOL_GROUP_DIM), lambda b: (0, 0, 0)),
                   pl.BlockSpec((1, width), lambda b: (0, 0))),
        input_output_aliases={4: 0},
        compiler_params=_params(),
    )(proj, dcat, w_pool, pool_scale, dqkv)


SUBLANES = 8


def _cond_prologue(c_pad, w_cond, b_cols, win_t):
    rows_c, d = c_pad.shape
    cols = w_cond.shape[1]
    n_all = N_DEV * rows_c

    def body(c_ref, w_ref, b_ref, win_ref, call_ref, modg_ref, wing_ref, blk, *sems):
        c_start, c_forward, c_finish = _ag_phases([(0, ())], [c_ref], [call_ref], *sems[0:3])
        w_start, w_forward, w_finish = _ag_phases([(0, ())], [win_ref], [wing_ref], *sems[3:6])
        m_start, m_forward, m_finish = _ag_phases([(0, ())], [blk], [modg_ref], *sems[6:9])
        c_start()
        w_start()
        c_forward()
        c_finish()
        cv = call_ref[...].reshape(n_all, d)
        a = cv * jax.nn.sigmoid(cv)
        blk[...] = jnp.dot(a, w_ref[...], preferred_element_type=F32, precision=lax.Precision.HIGHEST) + b_ref[...]
        m_start()
        m_forward()
        m_finish()
        w_forward()
        w_finish()

    vmem = pl.BlockSpec(memory_space=pltpu.VMEM)
    any_spec = pl.BlockSpec(memory_space=pl.ANY)
    return pl.pallas_call(
        body, name="cond_prologue",
        out_shape=(jax.ShapeDtypeStruct((N_DEV, rows_c, d), F32), jax.ShapeDtypeStruct((N_DEV, n_all, cols), F32),
                   jax.ShapeDtypeStruct((N_DEV,) + win_t.shape, win_t.dtype)),
        in_specs=[vmem, vmem, vmem, any_spec], out_specs=(vmem, vmem, any_spec),
        scratch_shapes=[pltpu.VMEM((n_all, cols), F32)] + _ag_scratch(1) * 3,
        compiler_params=_params(),
    )(c_pad, w_cond, b_cols, win_t)


def _cond_bwd_adamw(c_all, dmod_all, dmod_cols, w, m_w, v_w, b, m_b, v_b):
    def body(c_ref, dm_ref, dmc_ref, w_ref, mw_ref, vw_ref, b_ref, mb_ref, vb_ref,
             gw_ref, dw_ref, nmw_ref, nvw_ref, gb_ref, db_ref, nmb_ref, nvb_ref):
        cv = c_ref[...]
        a = cv * jax.nn.sigmoid(cv)
        gw = lax.dot_general(a, dmc_ref[...], (((0,), (0,)), ((), ())),
                             preferred_element_type=F32, precision=lax.Precision.HIGHEST)
        gw_ref[...] = gw
        dw_ref[...], nmw_ref[...], nvw_ref[...] = _adamw_math(w_ref[...], gw, mw_ref[...], vw_ref[...])
        gb = _colsum(dm_ref[...])
        gb_ref[...] = gb
        db_ref[...], nmb_ref[...], nvb_ref[...] = _adamw_math(b_ref[...], gb, mb_ref[...], vb_ref[...])

    w_sds, b_sds = jax.ShapeDtypeStruct(w.shape, F32), jax.ShapeDtypeStruct(b.shape, F32)
    outs = pl.pallas_call(
        body, name="cond_bwd_adamw", out_shape=(w_sds,) * 4 + (b_sds,) * 4, compiler_params=_params(),
    )(c_all, dmod_all, dmod_cols, w, m_w, v_w, b, m_b, v_b)
    return outs[:4], outs[4:]


def _adamw_math(w, g, m, v):
    m = ADAM_B1 * m + (1.0 - ADAM_B1) * g
    v = ADAM_B2 * v + (1.0 - ADAM_B2) * (g * g)
    m_hat = m / (1.0 - ADAM_B1 ** ADAM_STEP)
    v_hat = v / (1.0 - ADAM_B2 ** ADAM_STEP)
    delta = -ADAM_LR * (m_hat / (jnp.sqrt(v_hat) + ADAM_EPS) + ADAM_WD * w)
    return delta, m, v


def _adamw_small(ws, gparts, ms, vs, name):
    n = len(ws)

    def body(*refs):
        w_r, g_r, m_r, v_r = refs[:n], refs[n:2 * n], refs[2 * n:3 * n], refs[3 * n:4 * n]
        outs = refs[4 * n:]
        for i in range(n):
            g = g_r[i][0]
            for dev in range(1, g_r[i].shape[0]):
                g = g + g_r[i][dev]
            delta, m, v = _adamw_math(w_r[i][...], g, m_r[i][...], v_r[i][...])
            outs[i][...] = g
            outs[n + i][...] = delta
            outs[2 * n + i][...] = m
            outs[3 * n + i][...] = v

    sds = [jax.ShapeDtypeStruct(w.shape, F32) for w in ws]
    return pl.pallas_call(
        body, name=name, out_shape=tuple(sds * 4), compiler_params=_params(),
    )(*ws, *gparts, *ms, *vs)


def kernel(x, c, w_cond, b_cond, g_mix_pre, g_mix_post, w_in, w_pool, pool_scale, w_out, g_ffn_pre, g_ffn_post, w_gate, w_up, w_down, loss_target, m_w_cond, m_b_cond, m_g_mix_pre, m_g_mix_post, m_w_in, m_w_pool, m_pool_scale, m_w_out, m_g_ffn_pre, m_g_ffn_post, m_w_gate, m_w_up, m_w_down, v_w_cond, v_b_cond, v_g_mix_pre, v_g_mix_post, v_w_in, v_w_pool, v_pool_scale, v_w_out, v_g_ffn_pre, v_g_ffn_post, v_w_gate, v_w_up, v_w_down):
    n_seq, seq, d = x.shape
    t = n_seq * seq
    xi, yi, ci = _mesh_pos()
    me = 4 * xi + 2 * yi + ci
    x2 = x.reshape(t, d)
    tgt2 = loss_target.reshape(t, d)
    in_rows = w_in.shape[2]
    out_rows = w_out.shape[1]
    ff_rows = w_gate.shape[2]
    ff = N_DEV * ff_rows
    cond_cols = w_cond.shape[2]

    win_t = w_in[0].T.astype(BF16)
    wout_s = w_out[0].astype(BF16)
    wg_t = w_gate[0].T.astype(BF16)
    wu_t = w_up[0].T.astype(BF16)
    wd_s = w_down[0].astype(BF16)

    b_cols = lax.dynamic_slice_in_dim(b_cond, me * cond_cols, cond_cols, axis=1)
    c_all8, mod_g, win_g = _cond_prologue(jnp.pad(c, ((0, SUBLANES - n_seq), (0, 0))), w_cond[0], b_cols, win_t)
    c_all = c_all8[:, :n_seq].reshape(N_DEV * n_seq, d)
    mod_mine = lax.dynamic_slice_in_dim(mod_g, me * SUBLANES, n_seq, axis=1)
    mod = jnp.transpose(mod_mine, (1, 0, 2)).reshape(n_seq, N_MOD, d)
    win_full = win_g.reshape(N_DEV * in_rows, d)

    h1 = _pre_mix(x2, g_mix_pre, mod, seq)
    proj = _matmul(h1, win_full, "nt", BF16, 512, N_DEV * in_rows, d, "proj")
    tq = ATT_TILE
    ids = jnp.arange(tq)
    tri_after = jnp.tile(-(ids[:, None] >= ids[None, :]).astype(BF16), (2, 1))
    tri_incl = (ids[:, None] <= ids[None, :]).astype(BF16)
    attn, cstats, wout_g, wgu_g, wd_g = _attn_fwd(
        proj, tri_after, n_seq, seq, [wout_s, wg_t, wu_t, wd_s],
        [jax.ShapeDtypeStruct((N_DEV, out_rows, d), BF16), jax.ShapeDtypeStruct((2, N_DEV, ff_rows, d), BF16),
         jax.ShapeDtypeStruct((N_DEV, ff_rows, d), BF16)],
        [(0, ()), (1, (0,)), (1, (1,)), (2, ())])
    wout_full = wout_g.reshape(N_DEV * out_rows, d)
    wgu_full = wgu_g.reshape(2, ff, d)
    wd_full = wd_g.reshape(ff, d)
    cat = _pool_fwd(proj, w_pool[0], pool_scale, attn, n_seq, seq)
    tok_f32, tok_bf16 = jax.ShapeDtypeStruct((t, d), F32), jax.ShapeDtypeStruct((t, d), BF16)
    seq_sds, vec_sds = jax.ShapeDtypeStruct((n_seq, 1, d), F32), jax.ShapeDtypeStruct((1, d), F32)
    mix, x1, h2 = _matmul_rows(
        cat, wout_full.reshape(2, d // 2, d), ROW_TILE, seq, "mix_mid", _mid_epilogue,
        [x2, g_mix_post, g_ffn_pre, mod], ["tok", "vec", "vec", "mod"],
        [tok_f32, tok_f32, tok_bf16], ["tok", "tok", "tok"])
    gu, act = _ffn_up(h2, wgu_full, 512, ff // 2)
    loss_sum, dy, df, dgate_f, gg_ffn_post = _matmul_rows(
        act, wd_full, ROW_TILE, seq, "ffn_down_post", _post_epilogue,
        [x1, tgt2, g_ffn_post, mod], ["tok", "tok", "vec", "mod"],
        [jax.ShapeDtypeStruct((1, LANES), F32), tok_f32, tok_bf16, seq_sds, vec_sds],
        ["loss", "tok", "tok", "seq", "vec"])

    dgu = _ffn_act_bwd(df, wd_full, gu, 512, ff // 2)
    gwd, gwd_b = _matmul(act, df, "tn", F32, ff // 2, d // 2, t, "grad_w_down", bf16_copy=True)
    gwgu, gwgu_b = _matmul(dgu, h2, "tn", F32, ff // 2, d // 2, t, "grad_w_gate_up", bf16_copy=True)
    dx1, dmix, dshift_f, dscale_f, dgate_m, gg_ffn_pre, gg_mix_post = _matmul_rows(
        dgu, wgu_full, ROW_TILE, seq, "dh2_bwd_mid", _bwd_mid_epilogue,
        [dy, x1, mix, g_ffn_pre, g_mix_post, mod], ["tok", "tok", "tok", "vec", "vec", "mod"],
        [tok_f32, tok_bf16, seq_sds, seq_sds, seq_sds, vec_sds, vec_sds],
        ["tok", "tok", "seq", "seq", "seq", "vec", "vec"])
    dcat = _matmul(dmix, wout_full, "nt", BF16, 512, d, d, "dcat")
    gwout, gwout_b = _matmul(cat, dmix, "tn", F32, d // 2, d, t, "grad_w_out", bf16_copy=True)
    dqkv, rv_wgu, rv_wd, rv_wout = _attn_bwd(
        proj, dcat, cstats, tri_after, tri_incl, n_seq, seq,
        [gwgu_b.reshape(2, N_DEV, ff_rows, d), gwd_b.reshape(1, N_DEV, ff_rows, d),
         gwout_b.reshape(1, N_DEV, out_rows, d)])
    dproj, gw_pool, gs_pool = _pool_bwd(proj, dcat, w_pool[0], pool_scale, dqkv, n_seq, seq)
    pad_d = lambda v: jnp.pad(v, ((0, 0), (0, d - v.shape[1])))
    n_gw = gw_pool.size // d
    early = jnp.concatenate(
        [gg_mix_post, gg_ffn_pre, gg_ffn_post, pad_d(gs_pool), pad_d(loss_sum), jnp.zeros((3, d), F32),
         gw_pool.reshape(n_gw, d),
         jnp.concatenate([dgate_m, dshift_f, dscale_f, dgate_f], axis=1).reshape(n_seq * 4, d)], axis=0)
    gwin, gwin_b, early_g = _matmul(
        dproj, h1, "tn", F32, d // 2, d, t, "grad_w_in", bf16_copy=True,
        ag=([early], [jax.ShapeDtypeStruct((N_DEV,) + early.shape, F32)], [(0, ())]))
    grad_x, dshift_m, dscale_m, gg_mix_pre, rv_win = _matmul_rows(
        dproj, win_full.reshape(4, d // 2, d), ROW_TILE, seq, "dh1_bwd_pre", _bwd_pre_epilogue,
        [dx1, x2, g_mix_pre, mod], ["tok", "tok", "vec", "mod"],
        [tok_f32, seq_sds, seq_sds, vec_sds], ["tok", "seq", "seq", "vec"],
        rs_sends=[gwin_b.reshape(1, N_DEV, in_rows, d)])


    late = jnp.concatenate([gg_mix_pre, dshift_m.reshape(n_seq, d), dscale_m.reshape(n_seq, d),
                            jnp.zeros((8 - 1 - 2 * n_seq, d), F32)], axis=0)
    (late_g,) = _all_gather([late], [jax.ShapeDtypeStruct((N_DEV,) + late.shape, F32)], [(0, ())], "ag_late")
    loss = jnp.sum(early_g[:, 4, 0]) * (0.5 / d)
    dmod_all = jnp.concatenate(
        [late_g[:, 1:1 + n_seq, None, :], late_g[:, 1 + n_seq:1 + 2 * n_seq, None, :],
         early_g[:, 8 + n_gw:, :].reshape(N_DEV, n_seq, 4, d)], axis=2).reshape(N_DEV * n_seq, N_MOD * d)
    dmod_cols = lax.dynamic_slice_in_dim(dmod_all, me * cond_cols, cond_cols, axis=1)
    o_cond, o_bcond = _cond_bwd_adamw(c_all, dmod_all, dmod_cols, w_cond[0], m_w_cond[0], v_w_cond[0],
                                      b_cond, m_b_cond, v_b_cond)
    o_cond = tuple(o[None] for o in o_cond)

    small_ws = [g_mix_pre, g_mix_post, g_ffn_pre, g_ffn_post, pool_scale, w_pool.reshape(-1, POOL_GROUP_DIM)]
    small_ms = [m_g_mix_pre, m_g_mix_post, m_g_ffn_pre, m_g_ffn_post, m_pool_scale, m_w_pool.reshape(-1, POOL_GROUP_DIM)]
    small_vs = [v_g_mix_pre, v_g_mix_post, v_g_ffn_pre, v_g_ffn_post, v_pool_scale, v_w_pool.reshape(-1, POOL_GROUP_DIM)]
    small_gparts = [late_g[:, 0:1, :], early_g[:, 0:1, :], early_g[:, 1:2, :], early_g[:, 2:3, :],
                    early_g[:, 3:4, :pool_scale.shape[1]],
                    early_g[:, 8:8 + n_gw, :].reshape(N_DEV, -1, POOL_GROUP_DIM)]
    so = _adamw_small(small_ws, small_gparts, small_ms, small_vs, "adamw_small")
    ns = len(small_ws)
    sg, sdl, sm, sv = so[:ns], so[ns:2 * ns], so[2 * ns:3 * ns], so[3 * ns:]
    pool_shape = w_pool.shape
    fix = lambda lst: [lst[0], lst[1], lst[2], lst[3], lst[4], lst[5].reshape(pool_shape)]
    sg, sdl, sm, sv = fix(sg), fix(sdl), fix(sm), fix(sv)


    def reduced(mine, recv, slab, w, m, v, name, transposed=False, transpose=False):
        turn = (lambda u: u.T) if transposed else (lambda u: u)
        outs = _rs_final_adamw(mine, recv, slab, turn(w[0]), turn(m[0]), turn(v[0]), name, transpose)
        return tuple(turn(o)[None] for o in outs)

    o_in = reduced(gwin.reshape(1, N_DEV, in_rows, d), rv_win, 0, w_in, m_w_in, v_w_in, "adamw_w_in",
                   transpose=True)
    o_out = reduced(gwout.reshape(1, N_DEV, out_rows, d), rv_wout, 0, w_out, m_w_out, v_w_out, "adamw_w_out")
    gwgu8 = gwgu.reshape(2, N_DEV, ff_rows, d)
    o_gate = reduced(gwgu8, rv_wgu, 0, w_gate, m_w_gate, v_w_gate, "adamw_w_gate", transposed=True)
    o_up = reduced(gwgu8, rv_wgu, 1, w_up, m_w_up, v_w_up, "adamw_w_up", transposed=True)
    o_down = reduced(gwd.reshape(1, N_DEV, ff_rows, d), rv_wd, 0, w_down, m_w_down, v_w_down, "adamw_w_down")

    def pick(k):
        small_k = [sg, sdl, sm, sv][k]
        return [o_cond[k], o_bcond[k], small_k[0], small_k[1], o_in[k], small_k[5], small_k[4], o_out[k],
                small_k[2], small_k[3], o_gate[k], o_up[k], o_down[k]]

    return (loss, grad_x.reshape(n_seq, seq, d), *pick(0), *pick(1), *pick(2), *pick(3))
```

```python
import functools
import math

import jax
import jax.numpy as jnp
from jax import lax
from jax.experimental import pallas as pl
from jax.experimental.pallas import tpu as pltpu

F32 = jnp.float32
BF16 = jnp.bfloat16
MESH = pl.DeviceIdType.MESH

N_DEV = 8
HEAD_DIM = 64
LANES = 128
POOL_WINDOWS = (2, 4, 8, 16)
POOL_GROUP_DIM = 128
N_MOD = 6
EPS = 1e-6
ATT_TILE = 256
ATT_PAIRS = 2
VMEM_LIMIT = 56 * 1024 * 1024

ADAM_LR = 0.001
ADAM_B1 = 0.9
ADAM_B2 = 0.999
ADAM_EPS = 1e-08
ADAM_WD = 0.01
ADAM_STEP = 10


def _params(**kw):
    return pltpu.CompilerParams(vmem_limit_bytes=VMEM_LIMIT, **kw)


def _dot_nn(a, b):
    return jnp.dot(a, b, preferred_element_type=F32)


def _dot_nt(a, b):
    return lax.dot_general(a, b, (((1,), (1,)), ((), ())), preferred_element_type=F32)


def _dot_tn(a, b):
    return lax.dot_general(a, b, (((0,), (0,)), ((), ())), preferred_element_type=F32)


def _mesh_pos():
    return lax.axis_index("x"), lax.axis_index("y"), lax.axis_index("c")


def _ag_phases(dests, src, outs, send_sems, recv_sems, local_sems):
    n = len(src)
    x, y, c = _mesh_pos()
    me, sibling = (x, y, c), (x, y, 1 - c)
    chips = [(1 - x, y), (x, 1 - y), (1 - x, 1 - y)]

    def slot(i, dev):
        oi, prefix = dests[i]
        px, py, pc = dev
        return outs[oi].at[prefix + (4 * px + 2 * py + pc,)]

    def copy(i, k, block, to, from_src=False):
        return pltpu.make_async_remote_copy(
            src_ref=src[i] if from_src else slot(i, block), dst_ref=slot(i, block),
            send_sem=send_sems.at[i, k], recv_sem=recv_sems.at[i, k],
            device_id=to, device_id_type=MESH)

    def mine(i):
        return pltpu.make_async_copy(src[i], slot(i, me), local_sems.at[i])

    def first(i):
        return [copy(i, 0, me, sibling, from_src=True)] + [
            copy(i, 1 + j, me, (*chip, c), from_src=True) for j, chip in enumerate(chips)]

    def passed(i, j):
        return copy(i, 4 + j, (*chips[j], c), sibling)

    def start():
        for i in range(n):
            mine(i).start()
        for i in range(n):
            for cp in first(i):
                cp.start()

    def forward():
        for j, chip in enumerate(chips):
            for i in range(n):
                copy(i, 1 + j, (*chip, c), me).wait_recv()
                passed(i, j).start()

    def finish():
        for i in range(n):
            copy(i, 0, sibling, me).wait_recv()
            for j, chip in enumerate(chips):
                copy(i, 4 + j, (*chip, 1 - c), me).wait_recv()
        for i in range(n):
            for cp in first(i) + [passed(i, j) for j in range(3)]:
                cp.wait_send()
            mine(i).wait()

    return start, forward, finish


def _ag_scratch(n):
    return [pltpu.SemaphoreType.DMA((n, 7)), pltpu.SemaphoreType.DMA((n, 7)), pltpu.SemaphoreType.DMA((n,))]


def _all_gather(srcs, out_shapes, dests, name):
    n = len(srcs)

    def body(*refs):
        src = refs[:n]
        outs = refs[n:n + len(out_shapes)]
        start, forward, finish = _ag_phases(dests, src, outs, *refs[n + len(out_shapes):])
        start()
        forward()
        finish()

    any_spec = pl.BlockSpec(memory_space=pl.ANY)
    return pl.pallas_call(
        body, name=name,
        out_shape=tuple(out_shapes),
        in_specs=[any_spec] * n,
        out_specs=tuple([any_spec] * len(out_shapes)),
        scratch_shapes=_ag_scratch(n),
    )(*srcs)


def _rs_phases(shapes, src, dst, send_sems, recv_sems, only_c=None):
    x, y, c = _mesh_pos()

    def copies():
        out = []
        n = 0
        for i, shp in enumerate(shapes):
            for m in range(shp[0]):
                for k in range(1, N_DEV):
                    px, py, pc = x ^ (k >> 2), y ^ ((k >> 1) & 1), c ^ (k & 1)
                    blk = 4 * px + 2 * py + pc if only_c is None else 2 * px + py
                    out.append((pc, pltpu.make_async_remote_copy(
                        src_ref=src[i].at[m, blk], dst_ref=dst[i].at[m, k - 1],
                        send_sem=send_sems.at[n], recv_sem=recv_sems.at[n],
                        device_id=(px, py, pc), device_id_type=MESH)))
                    n += 1
        return out

    def when(cond, fn):
        if only_c is None:
            fn()
        else:
            pl.when(cond)(fn)

    def start():
        for pc, cp in copies():
            when(pc == only_c, cp.start)

    def finish():
        for pc, cp in copies():
            when(pc == only_c, cp.wait_send)
        for _, cp in copies():
            when(c == only_c, cp.wait_recv)

    return start, finish


def _rs_out(sends):
    return [jax.ShapeDtypeStruct((s.shape[0], N_DEV - 1) + s.shape[2:], s.dtype) for s in sends]


def _rs_scratch(sends):
    total = sum((N_DEV - 1) * s.shape[0] for s in sends)
    return [pltpu.SemaphoreType.DMA((total,)), pltpu.SemaphoreType.DMA((total,))]


def _rs_final_adamw_halves(mine, recv, w, m, v, name):
    _, _, r, cdim = mine[0].shape
    x, y, c = _mesh_pos()
    pos = jnp.stack([2 * x + y, c]).astype(jnp.int32)

    def body(pos_ref, p0_ref, p1_ref, r0_ref, r1_ref, w_ref, m_ref, v_ref, g_ref, d_ref, nm_ref, nv_ref, acc):
        for h, (p_ref, r_ref) in enumerate(((p0_ref, r0_ref), (p1_ref, r1_ref))):
            @pl.when(pos_ref[1] == h)
            def _(p_ref=p_ref, r_ref=r_ref):
                g = p_ref[...]
                for k in range(N_DEV - 1):
                    g = g + r_ref[k].astype(F32)
                acc[...] = g
        g = acc[...].T
        g_ref[...] = g
        d_ref[...], nm_ref[...], nv_ref[...] = _adamw_math(w_ref[...], g, m_ref[...], v_ref[...])

    full = pl.BlockSpec(w.shape, lambda i, s: (0, 0))
    mine_spec = pl.BlockSpec((None, None, r, cdim), lambda i, s: (0, s[0], 0, 0))
    recv_spec = pl.BlockSpec((None, N_DEV - 1, r, cdim), lambda i, s: (0, 0, 0, 0))
    sds = jax.ShapeDtypeStruct(w.shape, F32)
    return pl.pallas_call(
        body, name=name, out_shape=(sds, sds, sds, sds),
        grid_spec=pltpu.PrefetchScalarGridSpec(
            num_scalar_prefetch=1, grid=(1,),
            in_specs=[mine_spec, mine_spec, recv_spec, recv_spec, full, full, full],
            out_specs=(full, full, full, full), scratch_shapes=[pltpu.VMEM((r, cdim), F32)]),
        compiler_params=_params(),
    )(pos, mine[0], mine[1], recv[0], recv[1], w, m, v)


def _rs_final_adamw(mine, recv, slab, w, m, v, name, transpose=False):
    _, _, r, cdim = mine.shape
    x, y, c = _mesh_pos()
    me = jnp.reshape(4 * x + 2 * y + c, (1,)).astype(jnp.int32)

    def body(me_ref, p_ref, r_ref, w_ref, m_ref, v_ref, g_ref, d_ref, nm_ref, nv_ref):
        del me_ref
        g = p_ref[...]
        for k in range(N_DEV - 1):
            g = g + r_ref[k].astype(F32)
        if transpose:
            g = g.T
        g_ref[...] = g
        d_ref[...], nm_ref[...], nv_ref[...] = _adamw_math(w_ref[...], g, m_ref[...], v_ref[...])

    full = pl.BlockSpec(w.shape, lambda i, s: (0, 0))
    sds = jax.ShapeDtypeStruct(w.shape, F32)
    return pl.pallas_call(
        body, name=name, out_shape=(sds, sds, sds, sds),
        grid_spec=pltpu.PrefetchScalarGridSpec(
            num_scalar_prefetch=1, grid=(1,),
            in_specs=[pl.BlockSpec((None, None, r, cdim), lambda i, s: (slab, s[0], 0, 0)),
                      pl.BlockSpec((None, N_DEV - 1, r, cdim), lambda i, s: (slab, 0, 0, 0)), full, full, full],
            out_specs=(full, full, full, full)),
        compiler_params=_params(),
    )(me, mine, recv, w, m, v)


def _matmul(a, b, mode, out_dtype, tm, tn, tk, name, bf16_copy=False, rs_sends=(), ag=None, rs_only_c=None,
            m_tiles=None):
    ga = a.shape[0] if a.ndim == 3 else None
    gb = b.shape[0] if b.ndim == 3 else None
    a2, b2 = a.shape[-2:], b.shape[-2:]
    if mode == "nn":
        (m, k), n = a2, b2[1]
    elif mode == "nt":
        (m, k), n = a2, b2[0]
    else:
        (k, m), n = a2, b2[1]
    assert m % tm == 0 and n % tn == 0 and k % tk == 0, (name, m, n, k)
    m_first, m_count = m_tiles if m_tiles is not None else (0, m // tm)
    m = m_count * tm
    nk = k // tk
    g_n = ga or 1
    batch_out = mode == "tn" and ga is not None
    n_red = nk if batch_out else nk * g_n
    dot = {"nn": _dot_nn, "nt": _dot_nt, "tn": _dot_tn}[mode]
    acc_in_out = out_dtype == F32

    n_rs = len(rs_sends)
    rs_shapes = [r.shape for r in rs_sends]
    ag_srcs, ag_out_shapes, ag_dests = ag if ag is not None else ((), (), ())
    n_ag, n_ag_out = len(ag_srcs), len(ag_out_shapes)
    n_out = 2 if bf16_copy else 1
    assert not bf16_copy or acc_in_out
    assert not (n_rs and n_ag)

    def body(a_ref, b_ref, *rest):
        rs_src, rest = rest[:n_rs], rest[n_rs:]
        ag_src, rest = rest[:n_ag], rest[n_ag:]
        o_ref = rest[0]
        copy_ref = rest[1] if bf16_copy else None
        rs_dst, rest = rest[n_out:n_out + n_rs], rest[n_out + n_rs:]
        ag_out, scratch = rest[:n_ag_out], rest[n_ag_out:]
        first = functools.reduce(jnp.logical_and, [pl.program_id(ax) == 0 for ax in range(4)])
        last = functools.reduce(jnp.logical_and, [pl.program_id(ax) == grid[ax] - 1 for ax in range(4)])
        if n_rs:
            rs_start, rs_finish = _rs_phases(rs_shapes, rs_src, rs_dst, *scratch[-2:], only_c=rs_only_c)
            pl.when(first)(rs_start)
        if n_ag:
            ag_start, ag_forward, ag_finish = _ag_phases(ag_dests, ag_src, ag_out, *scratch[-3:])
            pl.when(first)(ag_start)
        p = dot(a_ref[...], b_ref[...])
        kk = pl.program_id(3) if batch_out else pl.program_id(2) * nk + pl.program_id(3)
        if n_red == 1:
            o_ref[...] = p.astype(out_dtype)
            if bf16_copy:
                copy_ref[...] = p.astype(BF16)
        else:
            acc = o_ref if acc_in_out else scratch[0]

            @pl.when(kk == 0)
            def _():
                acc[...] = p

            @pl.when(kk > 0)
            def _():
                acc[...] += p

            @pl.when(kk == n_red - 1)
            def _():
                if not acc_in_out:
                    o_ref[...] = acc[...].astype(out_dtype)
                if bf16_copy:
                    copy_ref[...] = acc[...].astype(BF16)

        if n_rs:
            pl.when(last)(rs_finish)
        if n_ag:
            @pl.when(last)
            def _():
                ag_forward()
                ag_finish()

    def order(ids):
        return ids if batch_out else (ids[2], ids[0], ids[1], ids[3])

    def a_idx(*ids):
        g, i, j, kq = order(ids)
        i = i + m_first
        blk = {"nn": (i, kq), "nt": (i, kq), "tn": (kq, i)}[mode]
        return (g,) + blk if ga is not None else blk

    def b_idx(*ids):
        g, i, j, kq = order(ids)
        blk = {"nn": (kq, j), "nt": (j, kq), "tn": (kq, j)}[mode]
        return (g,) + blk if gb is not None else blk

    def o_idx(*ids):
        g, i, j, kq = order(ids)
        return (g, i, j) if batch_out else (i, j)

    a_blk = {"nn": (tm, tk), "nt": (tm, tk), "tn": (tk, tm)}[mode]
    b_blk = {"nn": (tk, tn), "nt": (tn, tk), "tn": (tk, tn)}[mode]
    if ga is not None:
        a_blk = (None,) + a_blk
    if gb is not None:
        b_blk = (None,) + b_blk
    if batch_out:
        out_shape = jax.ShapeDtypeStruct((g_n, m, n), out_dtype)
        o_blk = (None, tm, tn)
        grid = (g_n, m // tm, n // tn, nk)
    else:
        out_shape = jax.ShapeDtypeStruct((m, n), out_dtype)
        o_blk = (tm, tn)
        grid = (m // tm, n // tn, g_n, nk)
    scratch = [] if (acc_in_out or n_red == 1) else [pltpu.VMEM((tm, tn), F32)]
    any_spec = pl.BlockSpec(memory_space=pl.ANY)
    out_shapes = [out_shape] + ([jax.ShapeDtypeStruct(out_shape.shape, BF16)] if bf16_copy else [])
    res = pl.pallas_call(
        body, name=name, out_shape=tuple(out_shapes + _rs_out(rs_sends) + list(ag_out_shapes)), grid=grid,
        in_specs=[pl.BlockSpec(a_blk, a_idx), pl.BlockSpec(b_blk, b_idx)] + [any_spec] * (n_rs + n_ag),
        out_specs=tuple([pl.BlockSpec(o_blk, o_idx)] * n_out + [any_spec] * (n_rs + n_ag_out)),
        scratch_shapes=scratch + (_rs_scratch(rs_sends) if n_rs else []) + (_ag_scratch(n_ag) if n_ag else []),
        compiler_params=_params(),
    )(a, b, *rs_sends, *ag_srcs)
    return res if len(res) > 1 else res[0]


EW_TILE = 256
ROW_TILE = 512
EPILOGUE_CHUNKS = 8
MXU_WIDTH = 256


def _rms(v):
    return lax.rsqrt(jnp.mean(v * v, axis=-1, keepdims=True) + EPS)


def _rms_bwd(dhat, vh, r):
    return r * (dhat - vh * jnp.mean(dhat * vh, axis=-1, keepdims=True))


def _tok_spec(tm, d):
    return pl.BlockSpec((tm, d), lambda i: (i, 0))


def _vec_spec(d):
    return pl.BlockSpec((1, d), lambda i: (0, 0))


def _mod_spec(tiles_per_seq, d):
    return pl.BlockSpec((None, N_MOD, d), lambda i: (i // tiles_per_seq, 0, 0))


def _seq_acc_spec(tiles_per_seq, d):
    return pl.BlockSpec((None, 1, d), lambda i: (i // tiles_per_seq, 0, 0))


def _acc(ref, val, first):
    if first is False:
        ref[...] += val
        return

    @pl.when(first)
    def _():
        ref[...] = val

    @pl.when(jnp.logical_not(first))
    def _():
        ref[...] += val


def _colsum(v):
    return jnp.sum(v, axis=0, keepdims=True)


def _pre_mix(x2, g_pre, mod, seq, ag_srcs, ag_out_shapes, ag_dests):
    t, d = x2.shape
    tm = EW_TILE
    n_steps = t // tm
    n_ag, n_ag_out = len(ag_srcs), len(ag_out_shapes)

    def body(x_ref, g_ref, mod_ref, *rest):
        ag_src, h_ref = rest[:n_ag], rest[n_ag]
        ag_out, sems = rest[n_ag + 1:n_ag + 1 + n_ag_out], rest[n_ag + 1 + n_ag_out:]
        ag_start, ag_forward, ag_finish = _ag_phases(ag_dests, ag_src, ag_out, *sems)
        step = pl.program_id(0)
        pl.when(step == 0)(ag_start)
        xv = x_ref[...]
        n = xv * _rms(xv) * g_ref[...]
        h_ref[...] = (n * (1.0 + mod_ref[1:2, :]) + mod_ref[0:1, :]).astype(BF16)

        @pl.when(step == n_steps - 1)
        def _():
            ag_forward()
            ag_finish()

    any_spec = pl.BlockSpec(memory_space=pl.ANY)
    return pl.pallas_call(
        body, name="pre_mix", out_shape=(jax.ShapeDtypeStruct((t, d), BF16), *ag_out_shapes), grid=(n_steps,),
        in_specs=[_tok_spec(tm, d), _vec_spec(d), _mod_spec(seq // tm, d)] + [any_spec] * n_ag,
        out_specs=(_tok_spec(tm, d), *([any_spec] * n_ag_out)),
        scratch_shapes=_ag_scratch(n_ag), compiler_params=_params(),
    )(x2, g_pre, mod, *ag_srcs)


def _matmul_rows(a, b, tm, seq, name, epilogue, ep_in, ep_in_kinds, ep_out, ep_out_kinds, rs_sends=(),
                 rs_only_c=None):
    g_n = a.shape[0] if a.ndim == 3 else None
    (m, k), n = a.shape[-2:], b.shape[-1]
    tps = seq // tm
    n_i = m // tm
    n_rs = len(rs_sends)
    rs_shapes = [r.shape for r in rs_sends]
    n_in, n_out = len(ep_in), len(ep_out)
    n_cols = n // MXU_WIDTH
    rc, cw = tm // EPILOGUE_CHUNKS, n // n_cols

    def prev(i):
        return jnp.maximum(i - 1, 0)

    def spec(kind):
        return {"tok": pl.BlockSpec((tm, n), lambda i: (prev(i), 0)),
                "vec": pl.BlockSpec((1, n), lambda i: (0, 0)),
                "mod": pl.BlockSpec((None, N_MOD, n), lambda i: (prev(i) // tps, 0, 0)),
                "seq": pl.BlockSpec((None, 1, n), lambda i: (prev(i) // tps, 0, 0)),
                "loss": pl.BlockSpec((1, LANES), lambda i: (0, 0))}[kind]

    def body(a_ref, b_ref, *rest):
        in_refs, rest = rest[:n_in], rest[n_in:]
        rs_src, rest = rest[:n_rs], rest[n_rs:]
        out_refs, rest = rest[:n_out], rest[n_out:]
        rs_dst, rest = rest[:n_rs], rest[n_rs:]
        fin = rest[0]
        i = pl.program_id(0)
        if n_rs:
            rs_start, rs_finish = _rs_phases(rs_shapes, rs_src, rs_dst, *rest[1:], only_c=rs_only_c)
            pl.when(i == 0)(rs_start)

        def product(cols):
            if g_n is None:
                return _dot_nn(a_ref[...], b_ref[:, cols])
            p = _dot_nn(a_ref[0], b_ref[0, :, cols])
            for g in range(1, g_n):
                p = p + _dot_nn(a_ref[g], b_ref[g, :, cols])
            return p

        def step(with_epilogue, with_matmul):
            parts = []
            for c in range(EPILOGUE_CHUNKS):
                if with_epilogue:
                    rows = pl.ds(c * rc, rc)
                    epilogue(fin[rows, :], i - 1, tps, in_refs, out_refs, rows, c)
                while with_matmul and len(parts) < (c + 1) * n_cols // EPILOGUE_CHUNKS:
                    cols = slice(len(parts) * cw, (len(parts) + 1) * cw)
                    parts.append((cols, product(cols)))
            for cols, v in parts:
                fin[:, cols] = v

        pl.when(i == 0)(functools.partial(step, False, True))
        pl.when(jnp.logical_and(i > 0, i < n_i))(functools.partial(step, True, True))
        pl.when(i == n_i)(functools.partial(step, True, False))

        if n_rs:
            pl.when(i == n_i)(rs_finish)

    def row(i):
        return jnp.minimum(i, n_i - 1)

    if g_n is None:
        a_spec = pl.BlockSpec((tm, k), lambda i: (row(i), 0))
        b_spec = pl.BlockSpec(b.shape, lambda i: (0, 0), pipeline_mode=pl.Buffered(1))
    else:
        a_spec = pl.BlockSpec((g_n, tm, k), lambda i: (0, row(i), 0))
        b_spec = pl.BlockSpec(b.shape, lambda i: (0, 0, 0), pipeline_mode=pl.Buffered(1))
    any_spec = pl.BlockSpec(memory_space=pl.ANY)
    res = pl.pallas_call(
        body, name=name, grid=(n_i + 1,), out_shape=tuple(list(ep_out) + _rs_out(rs_sends)),
        in_specs=[a_spec, b_spec] + [spec(kd) for kd in ep_in_kinds] + [any_spec] * n_rs,
        out_specs=tuple([spec(kd) for kd in ep_out_kinds] + [any_spec] * n_rs),
        scratch_shapes=[pltpu.VMEM((tm, n), F32)] + (_rs_scratch(rs_sends) if n_rs else []),
        compiler_params=_params(),
    )(a, b, *ep_in, *rs_sends)
    return res


def _first(cond, chunk):
    return cond if chunk == 0 else False


def _mid_epilogue(mv, i, tps, in_refs, out_refs, rows, chunk):
    x_ref, gpost_ref, gpre_ref, mod_ref = in_refs
    mix_ref, x1_ref, h2_ref = out_refs
    mix_ref[rows, :] = mv
    x1 = x_ref[rows, :] + mod_ref[2:3, :] * (mv * _rms(mv) * gpost_ref[...])
    x1_ref[rows, :] = x1
    n = x1 * _rms(x1) * gpre_ref[...]
    h2_ref[rows, :] = (n * (1.0 + mod_ref[4:5, :]) + mod_ref[3:4, :]).astype(BF16)


def _post_epilogue(fv, i, tps, in_refs, out_refs, rows, chunk):
    x1_ref, tgt_ref, g_ref, mod_ref = in_refs
    loss_ref, dy_ref, df_ref, dgate_ref, gg_ref = out_refs
    d = fv.shape[1]
    r = _rms(fv)
    fh = fv * r
    nf = fh * g_ref[...]
    gate = mod_ref[5:6, :]
    err = x1_ref[rows, :] + gate * nf - tgt_ref[rows, :]
    _acc(loss_ref, jnp.sum(_colsum(err * err), axis=1, keepdims=True) * jnp.ones((1, LANES), F32),
         _first(i == 0, chunk))
    dy = err * (1.0 / d)
    dy_ref[rows, :] = dy
    _acc(dgate_ref, _colsum(dy * nf), _first(i % tps == 0, chunk))
    dn = dy * gate
    _acc(gg_ref, _colsum(dn * fh), _first(i == 0, chunk))
    df_ref[rows, :] = _rms_bwd(dn * g_ref[...], fh, r).astype(BF16)


def _bwd_mid_epilogue(dh, i, tps, in_refs, out_refs, rows, chunk):
    dy_ref, x1_ref, mix_ref, gpre_ref, gpost_ref, mod_ref = in_refs
    dx1_ref, dmix_ref, dshift_ref, dscale_ref, dgate_ref, ggpre_ref, ggpost_ref = out_refs
    seq_first, first = _first(i % tps == 0, chunk), _first(i == 0, chunk)
    x1 = x1_ref[rows, :]
    r = _rms(x1)
    xh = x1 * r
    gpre = gpre_ref[...]
    _acc(dshift_ref, _colsum(dh), seq_first)
    _acc(dscale_ref, _colsum(dh * xh * gpre), seq_first)
    dn = dh * (1.0 + mod_ref[4:5, :])
    _acc(ggpre_ref, _colsum(dn * xh), first)
    dx1 = dy_ref[rows, :] + _rms_bwd(dn * gpre, xh, r)
    dx1_ref[rows, :] = dx1
    mv = mix_ref[rows, :]
    rm = _rms(mv)
    mh = mv * rm
    gpost = gpost_ref[...]
    _acc(dgate_ref, _colsum(dx1 * mh * gpost), seq_first)
    dnm = dx1 * mod_ref[2:3, :]
    _acc(ggpost_ref, _colsum(dnm * mh), first)
    dmix_ref[rows, :] = _rms_bwd(dnm * gpost, mh, rm).astype(BF16)


def _bwd_pre_epilogue(dh, i, tps, in_refs, out_refs, rows, chunk):
    dx1_ref, x_ref, g_ref, mod_ref = in_refs
    gx_ref, dshift_ref, dscale_ref, gg_ref = out_refs
    seq_first = _first(i % tps == 0, chunk)
    xv = x_ref[rows, :]
    r = _rms(xv)
    xh = xv * r
    g = g_ref[...]
    _acc(dshift_ref, _colsum(dh), seq_first)
    _acc(dscale_ref, _colsum(dh * xh * g), seq_first)
    dn = dh * (1.0 + mod_ref[1:2, :])
    _acc(gg_ref, _colsum(dn * xh), _first(i == 0, chunk))
    gx_ref[rows, :] = dx1_ref[rows, :] + _rms_bwd(dn * g, xh, r)


def _ffn_up(h2, wgu, tm, tn):
    t, d = h2.shape
    f = wgu.shape[1]

    def body(h_ref, w_ref, gu_ref, act_ref):
        h = h_ref[...]
        g = _dot_nt(h, w_ref[0])
        u = _dot_nt(h, w_ref[1])
        gu_ref[0] = g.astype(BF16)
        gu_ref[1] = u.astype(BF16)
        act_ref[...] = (g * jax.nn.sigmoid(g) * u).astype(BF16)

    return pl.pallas_call(
        body, name="ffn_up", grid=(f // tn, t // tm),
        out_shape=(jax.ShapeDtypeStruct((2, t, f), BF16), jax.ShapeDtypeStruct((t, f), BF16)),
        in_specs=[pl.BlockSpec((tm, d), lambda j, i: (i, 0)), pl.BlockSpec((2, tn, d), lambda j, i: (0, j, 0))],
        out_specs=(pl.BlockSpec((2, tm, tn), lambda j, i: (0, i, j)), pl.BlockSpec((tm, tn), lambda j, i: (i, j))),
        compiler_params=_params(),
    )(h2, wgu)


def _ffn_act_bwd(df, wd, gu, tm, tn):
    t, d = df.shape
    f = wd.shape[0]

    def body(df_ref, w_ref, gu_ref, dgu_ref):
        da = _dot_nt(df_ref[...], w_ref[...])
        g = gu_ref[0].astype(F32)
        u = gu_ref[1].astype(F32)
        s = jax.nn.sigmoid(g)
        silu = g * s
        dgu_ref[0] = (da * u * (s + silu * (1.0 - s))).astype(BF16)
        dgu_ref[1] = (da * silu).astype(BF16)

    return pl.pallas_call(
        body, name="ffn_act_bwd", grid=(f // tn, t // tm),
        out_shape=jax.ShapeDtypeStruct((2, t, f), BF16),
        in_specs=[pl.BlockSpec((tm, d), lambda j, i: (i, 0)), pl.BlockSpec((tn, d), lambda j, i: (j, 0)),
                  pl.BlockSpec((2, tm, tn), lambda j, i: (0, i, j))],
        out_specs=pl.BlockSpec((2, tm, tn), lambda j, i: (0, i, j)),
        compiler_params=_params(),
    )(df, wd, gu)


SIGN_BIT = 0x80000000
Q_SCALE = 1.0 / math.sqrt(HEAD_DIM)


def _softplus(z):
    neg_abs = lax.bitcast_convert_type(lax.bitcast_convert_type(z, jnp.uint32) | jnp.uint32(SIGN_BIT), F32)
    return jnp.maximum(z, 0.0) + jnp.log(1.0 + jnp.exp(neg_abs))


def _hi_lo(v):
    hi = v.astype(BF16)
    return jnp.concatenate([hi, (v - hi.astype(F32)).astype(BF16)], axis=1)


def _emit_skewed(chains, lag=1):
    for t in range(max(len(ch) for ch in chains) + lag * (len(chains) - 1)):
        for c, ch in enumerate(chains):
            if 0 <= t - lag * c < len(ch):
                ch[t - lag * c]()


def _fwd_chain(blk, qs, k_ref, v_ref, c0, kb, cols, mask, ntri, lane, tq):
    st = {}

    def scores():
        st["z"] = _dot_nt(qs, k_ref[pl.ds(c0, tq), cols])

    def soft():
        sp = _softplus(st["z"])
        if mask is not None:
            sp = jnp.where(mask, sp, 0.0)
        st["parts"] = _hi_lo(sp)
        st["cur"] = blk["cur"]
        blk["cm"] = jnp.where(lane == kb, blk["cur"], blk["cm"])
        blk["cur"] = blk["cur"] - jnp.sum(sp, axis=1, keepdims=True)

    def sums():
        st["s"] = _dot_nn(st["parts"], ntri)

    def weights():
        w = jnp.exp(st["z"] + st["s"] + st["cur"])
        if mask is not None:
            w = jnp.where(mask, w, 0.0)
        st["w"] = w.astype(BF16)

    def out():
        p = _dot_nn(st["w"], v_ref[pl.ds(c0, tq), cols])
        blk["pv"] = p if blk["pv"] is None else blk["pv"] + p

    return [scores, soft, sums, weights, out]


def _bwd_chain(blk, qs, dos, cs, k_ref, v_ref, dk_ref, dv_ref, c0, kb, cols, mask, ntri, tri_i, lane, tq):
    st = {}

    def scores():
        st["z"] = _dot_nt(qs, k_ref[pl.ds(c0, tq), cols])
        st["dw"] = _dot_nt(dos, v_ref[pl.ds(c0, tq), cols])

    def soft():
        sp = _softplus(st["z"])
        if mask is not None:
            sp = jnp.where(mask, sp, 0.0)
        st["sp"] = sp
        st["parts"] = _hi_lo(sp)
        st["cur"] = jnp.sum(jnp.where(lane == kb, cs, 0.0), axis=1, keepdims=True)

    def sums():
        st["s"] = _dot_nn(st["parts"], ntri)

    def weights():
        w = jnp.exp(st["z"] + st["s"] + st["cur"])
        if mask is not None:
            w = jnp.where(mask, w, 0.0)
        ee = w * st["dw"]
        st["w"], st["ee"], st["ec"] = w.astype(BF16), ee, blk["ec"]
        blk["ec"] = blk["ec"] + jnp.sum(ee, axis=1, keepdims=True)

    def prefix():
        st["einc"] = _dot_nn(st["ee"].astype(BF16), tri_i)

    def dz():
        v = st["ee"] - jnp.exp(st["z"] - st["sp"]) * (st["einc"] + st["ec"])
        if mask is not None:
            v = jnp.where(mask, v, 0.0)
        st["dz"] = v.astype(BF16)

    def grads():
        p = _dot_nn(st["dz"], k_ref[pl.ds(c0, tq), cols])
        blk["dq"] = p if blk["dq"] is None else blk["dq"] + p
        dk_ref[pl.ds(c0, tq), :] += _dot_tn(st["dz"], qs)
        dv_ref[pl.ds(c0, tq), :] += _dot_tn(st["w"], dos)

    return [scores, soft, sums, weights, prefix, dz, grads]


def _stack_heads(v, lane, scale=None):
    if scale is not None:
        v = v * jnp.asarray(scale, v.dtype)
    zero = jnp.zeros_like(v)
    return jnp.concatenate([jnp.where(lane < HEAD_DIM, v, zero), jnp.where(lane >= HEAD_DIM, v, zero)], axis=0)


def _diag_mask(tq):
    row = lax.broadcasted_iota(jnp.int32, (2 * tq, tq), 0)
    col = lax.broadcasted_iota(jnp.int32, (2 * tq, tq), 1)
    return col < jnp.where(row >= tq, row - tq, row)


def _attn_fwd(proj, tri_after, n_seq, seq, ag_srcs, ag_out_shapes, ag_dests):
    t = proj.shape[0]
    tq = ATT_TILE
    npp = ATT_PAIRS
    n_blk = (proj.shape[1] // 4) // (npp * LANES)
    n_ag, n_ag_out = len(ag_srcs), len(ag_out_shapes)
    n_steps = n_seq * n_blk

    def body(q_ref, k_ref, v_ref, tri_ref, *rest):
        ag_src, rest = rest[:n_ag], rest[n_ag:]
        o_ref, cs_ref = rest[:2]
        ag_out, rest = rest[2:2 + n_ag_out], rest[2 + n_ag_out:]
        oacc, cmat, carry = rest[:3]
        ag_start, ag_forward, ag_finish = _ag_phases(ag_dests, ag_src, ag_out, *rest[3:])
        step = pl.program_id(0) * n_blk + pl.program_id(1)
        pl.when(step == 0)(ag_start)
        pl.when(step == (3 * n_steps) // 4)(ag_forward)
        lane = lax.broadcasted_iota(jnp.int32, (1, LANES), 1)
        ntri = tri_ref[...]
        diag = _diag_mask(tq)

        def q_tile(qi, _):
            r0 = pl.multiple_of(qi * tq, tq)
            qs = [_stack_heads(q_ref[pl.ds(r0, tq), pp * LANES:(pp + 1) * LANES], lane, Q_SCALE)
                  for pp in range(npp)]
            carry[...] = jnp.zeros_like(carry)
            cmat[...] = jnp.zeros_like(cmat)
            oacc[...] = jnp.zeros_like(oacc)

            def run_tiles(tiles):
                blocks = [dict(cur=carry[pp], cm=cmat[pp], pv=None) for pp in range(npp)]
                chains = []
                for kb, mask in tiles:
                    c0 = pl.multiple_of(kb * tq, tq)
                    for pp in range(npp):
                        chains.append(_fwd_chain(blocks[pp], qs[pp], k_ref, v_ref, c0, kb,
                                                 slice(pp * LANES, (pp + 1) * LANES), mask, ntri, lane, tq))
                _emit_skewed(chains)
                for pp in range(npp):
                    oacc[pp] += blocks[pp]["pv"]
                    cmat[pp] = blocks[pp]["cm"]
                    carry[pp] = blocks[pp]["cur"]

            odd = qi % 2

            @pl.when(odd == 0)
            def _():
                run_tiles([(qi, diag)])

            @pl.when(odd == 1)
            def _():
                run_tiles([(qi, diag), (qi - 1, None)])

            def pair(j, _):
                kb = qi - 1 - odd - 2 * j
                run_tiles([(kb, None), (kb - 1, None)])
                return 0

            lax.fori_loop(0, qi // 2, pair, 0)
            for pp in range(npp):
                c_off = 2 * pp * LANES
                cs_ref[pl.ds(r0, tq), c_off:c_off + LANES] = cmat[pp, 0:tq, :]
                cs_ref[pl.ds(r0, tq), c_off + LANES:c_off + 2 * LANES] = cmat[pp, tq:2 * tq, :]
                o_ref[pl.ds(r0, tq), pp * LANES:(pp + 1) * LANES] = jnp.where(
                    lane < HEAD_DIM, oacc[pp, 0:tq, :], oacc[pp, tq:2 * tq, :]).astype(BF16)
            return 0

        lax.fori_loop(0, seq // tq, q_tile, 0)
        pl.when(step == n_steps - 1)(ag_finish)

    wid = npp * LANES
    blk = lambda off: pl.BlockSpec((seq, wid), lambda b, p: (b, off + p))
    any_spec = pl.BlockSpec(memory_space=pl.ANY)
    return pl.pallas_call(
        body, name="attn_fwd", grid=(n_seq, n_blk),
        out_shape=(jax.ShapeDtypeStruct((2, t, n_blk * wid), BF16),
                   jax.ShapeDtypeStruct((t, n_blk * 2 * wid), F32), *ag_out_shapes),
        in_specs=[blk(0), blk(n_blk), blk(2 * n_blk), pl.BlockSpec((2 * tq, tq), lambda b, p: (0, 0))]
        + [any_spec] * n_ag,
        out_specs=(pl.BlockSpec((None, seq, wid), lambda b, p: (0, b, p)),
                   pl.BlockSpec((seq, 2 * wid), lambda b, p: (b, p)), *([any_spec] * n_ag_out)),
        scratch_shapes=[pltpu.VMEM((npp, 2 * tq, LANES), F32), pltpu.VMEM((npp, 2 * tq, LANES), F32),
                        pltpu.VMEM((npp, 2 * tq, 1), F32)] + _ag_scratch(n_ag),
        compiler_params=_params(),
    )(proj, proj, proj, tri_after, *ag_srcs)


def _attn_bwd(proj, dcat, cstats, tri_after, tri_incl, n_seq, seq, rs_sends):
    t = proj.shape[0]
    tq = ATT_TILE
    npp = ATT_PAIRS
    width = proj.shape[1] // 4
    n_blk = width // (npp * LANES)
    n_rs = len(rs_sends)
    rs_shapes = [r.shape for r in rs_sends]
    n_steps = n_seq * n_blk

    def body(q_ref, k_ref, v_ref, do_ref, cs_ref, tria_ref, trii_ref, *rest):
        rs_src, rest = rest[:n_rs], rest[n_rs:]
        out_ref = rest[0]
        rs_dst, rest = rest[1:1 + n_rs], rest[1 + n_rs:]
        dq_acc, dk_acc, dv_acc, ecarry = rest[:4]
        rs_start, rs_finish = _rs_phases(rs_shapes, rs_src, rs_dst, *rest[4:])
        step = pl.program_id(0) * n_blk + pl.program_id(1)
        pl.when(step == 0)(rs_start)
        lane = lax.broadcasted_iota(jnp.int32, (1, LANES), 1)
        ntri = tria_ref[...]
        tri_i = trii_ref[...]
        diag = _diag_mask(tq)
        dk_acc[...] = jnp.zeros_like(dk_acc)
        dv_acc[...] = jnp.zeros_like(dv_acc)

        def q_tile(qi, _):
            r0 = pl.multiple_of(qi * tq, tq)
            qs, dos, cs = [], [], []
            for pp in range(npp):
                cols = slice(pp * LANES, (pp + 1) * LANES)
                qs.append(_stack_heads(q_ref[pl.ds(r0, tq), cols], lane, Q_SCALE))
                dos.append(_stack_heads(do_ref[pl.ds(r0, tq), cols], lane))
                c_off = 2 * pp * LANES
                cs.append(jnp.concatenate([cs_ref[pl.ds(r0, tq), c_off:c_off + LANES],
                                           cs_ref[pl.ds(r0, tq), c_off + LANES:c_off + 2 * LANES]], axis=0))
            ecarry[...] = jnp.zeros_like(ecarry)
            dq_acc[...] = jnp.zeros_like(dq_acc)

            def run_tiles(tiles):
                blocks = [dict(ec=ecarry[pp], dq=None) for pp in range(npp)]
                chains = []
                for kb, mask in tiles:
                    c0 = pl.multiple_of(kb * tq, tq)
                    for pp in range(npp):
                        chains.append(_bwd_chain(
                            blocks[pp], qs[pp], dos[pp], cs[pp], k_ref, v_ref, dk_acc.at[pp], dv_acc.at[pp],
                            c0, kb, slice(pp * LANES, (pp + 1) * LANES), mask, ntri, tri_i, lane, tq))
                _emit_skewed(chains)
                for pp in range(npp):
                    dq_acc[pp] += blocks[pp]["dq"]
                    ecarry[pp] = blocks[pp]["ec"]

            def pair(j, _):
                run_tiles([(2 * j, None), (2 * j + 1, None)])
                return 0

            lax.fori_loop(0, qi // 2, pair, 0)
            odd = qi % 2

            @pl.when(odd == 0)
            def _():
                run_tiles([(qi, diag)])

            @pl.when(odd == 1)
            def _():
                run_tiles([(qi - 1, None), (qi, diag)])

            for pp in range(npp):
                dq = jnp.where(lane < HEAD_DIM, dq_acc[pp, 0:tq, :], dq_acc[pp, tq:2 * tq, :])
                out_ref[0, pl.ds(r0, tq), pp * LANES:(pp + 1) * LANES] = (dq * Q_SCALE).astype(BF16)
            return 0

        lax.fori_loop(0, seq // tq, q_tile, 0)
        for pp in range(npp):
            cols = slice(pp * LANES, (pp + 1) * LANES)
            out_ref[1, :, cols] = dk_acc[pp].astype(BF16)
            out_ref[2, :, cols] = dv_acc[pp].astype(BF16)
        pl.when(step == n_steps - 1)(rs_finish)

    wid = npp * LANES
    blk = lambda off: pl.BlockSpec((seq, wid), lambda b, p: (b, off + p))
    tri_spec = pl.BlockSpec((2 * tq, tq), lambda b, p: (0, 0))
    any_spec = pl.BlockSpec(memory_space=pl.ANY)
    return pl.pallas_call(
        body, name="attn_bwd", grid=(n_seq, n_blk),
        out_shape=(jax.ShapeDtypeStruct((4, t, width), BF16), *_rs_out(rs_sends)),
        in_specs=[blk(0), blk(n_blk), blk(2 * n_blk), pl.BlockSpec((seq, wid), lambda b, p: (b, p)),
                  pl.BlockSpec((seq, 2 * wid), lambda b, p: (b, p)), tri_spec,
                  pl.BlockSpec((tq, tq), lambda b, p: (0, 0))] + [any_spec] * n_rs,
        out_specs=(pl.BlockSpec((3, seq, wid), lambda b, p: (0, b, p)), *([any_spec] * n_rs)),
        scratch_shapes=[pltpu.VMEM((npp, 2 * tq, LANES), F32), pltpu.VMEM((npp, seq, LANES), F32),
                        pltpu.VMEM((npp, seq, LANES), F32), pltpu.VMEM((npp, 2 * tq, 1), F32)]
        + _rs_scratch(rs_sends),
        compiler_params=_params(),
    )(proj, proj, proj, dcat, cstats, tri_after, tri_incl, *rs_sends)


def _window_sum(v, g, rows, forward):
    s_len = v.shape[0]
    s = v
    for step in range(g + 1):
        sh = 1 << step
        if forward:
            s = s + jnp.where(rows < s_len - sh, pltpu.roll(s, s_len - sh, axis=0), 0.0)
        else:
            s = s + jnp.where(rows >= sh, pltpu.roll(s, sh, axis=0), 0.0)
    return s


def _window_count(g, rows):
    return jnp.minimum(rows + 1, POOL_WINDOWS[g]).astype(F32)


def _pooled(u, g, rows):
    return _window_sum(u, g, rows, forward=False) / _window_count(g, rows) - u


def _group_cols(g):
    return slice(g * POOL_GROUP_DIM, (g + 1) * POOL_GROUP_DIM)


def _pool_fwd(proj, w_pool, pool_scale, cat, n_seq, seq):
    n_grp = len(POOL_WINDOWS)
    width = n_grp * POOL_GROUP_DIM
    assert [1 << (g + 1) for g in range(n_grp)] == list(POOL_WINDOWS)

    def body(u_ref, w_ref, s_ref, alias_ref, o_ref):
        del alias_ref
        rows = lax.broadcasted_iota(jnp.int32, (seq, 1), 0)
        for g in range(n_grp):
            cols = _group_cols(g)
            pooled = _pooled(u_ref[:, cols].astype(F32), g, rows)
            y = _dot_nn(pooled.astype(BF16), w_ref[g].astype(BF16))
            o_ref[:, cols] = (y * s_ref[:, cols]).astype(BF16)

    return pl.pallas_call(
        body, name="pool_fwd", grid=(n_seq,),
        out_shape=jax.ShapeDtypeStruct(cat.shape, BF16),
        in_specs=[pl.BlockSpec((seq, width), lambda b: (b, 3)),
                  pl.BlockSpec((n_grp, POOL_GROUP_DIM, POOL_GROUP_DIM), lambda b: (0, 0, 0)),
                  pl.BlockSpec((1, width), lambda b: (0, 0)),
                  pl.BlockSpec(memory_space=pl.ANY)],
        out_specs=pl.BlockSpec((None, seq, width), lambda b: (1, b, 0)),
        input_output_aliases={3: 0},
        compiler_params=_params(),
    )(proj, w_pool, pool_scale, cat)


def _pool_bwd(proj, dcat, w_pool, pool_scale, dqkv, n_seq, seq):
    n_grp = len(POOL_WINDOWS)
    width = n_grp * POOL_GROUP_DIM

    def body(u_ref, dp_ref, w_ref, s_ref, alias_ref, du_ref, gw_ref, gs_ref):
        del alias_ref
        b = pl.program_id(0)
        rows = lax.broadcasted_iota(jnp.int32, (seq, 1), 0)
        for g in range(n_grp):
            cols = _group_cols(g)
            pb = _pooled(u_ref[:, cols].astype(F32), g, rows).astype(BF16)
            wb = w_ref[g].astype(BF16)
            z = _dot_nn(pb, wb)
            dp = dp_ref[:, cols].astype(F32)
            _acc(gs_ref.at[:, cols], _colsum(dp * z), b == 0)
            dys = (dp * s_ref[:, cols]).astype(BF16)
            _acc(gw_ref.at[g], _dot_tn(pb, dys), b == 0)
            dpooled = _dot_nt(dys, wb)
            du = _window_sum(dpooled / _window_count(g, rows), g, rows, forward=True) - dpooled
            du_ref[:, cols] = du.astype(BF16)

    return pl.pallas_call(
        body, name="pool_bwd", grid=(n_seq,),
        out_shape=(jax.ShapeDtypeStruct(dqkv.shape, BF16),
                   jax.ShapeDtypeStruct((n_grp, POOL_GROUP_DIM, POOL_GROUP_DIM), F32),
                   jax.ShapeDtypeStruct((1, width), F32)),
        in_specs=[pl.BlockSpec((seq, width), lambda b: (b, 3)),
                  pl.BlockSpec((seq, width), lambda b: (b, 1)),
                  pl.BlockSpec((n_grp, POOL_GROUP_DIM, POOL_GROUP_DIM), lambda b: (0, 0, 0)),
                  pl.BlockSpec((1, width), lambda b: (0, 0)),
                  pl.BlockSpec(memory_space=pl.ANY)],
        out_specs=(pl.BlockSpec((None, seq, width), lambda b: (3, b, 0)),
                   pl.BlockSpec((n_grp, POOL_GROUP_DIM, POOL_GROUP_DIM), lambda b: (0, 0, 0)),
                   pl.BlockSpec((1, width), lambda b: (0, 0))),
        input_output_aliases={4: 0},
        compiler_params=_params(),
    )(proj, dcat, w_pool, pool_scale, dqkv)


def _cond_fwd(c_all, w_cond, b_cols):
    n, _ = c_all.shape
    cols = w_cond.shape[1]

    def body(c_ref, w_ref, b_ref, o_ref):
        cv = c_ref[...]
        a = cv * jax.nn.sigmoid(cv)
        o_ref[...] = jnp.dot(a, w_ref[...], preferred_element_type=F32,
                             precision=lax.Precision.HIGHEST) + b_ref[...]

    return pl.pallas_call(
        body, name="cond_fwd", out_shape=jax.ShapeDtypeStruct((n, cols), F32),
        compiler_params=_params(),
    )(c_all, w_cond, b_cols)


def _cond_bwd_adamw(c_all, dmod_all, dmod_cols, w, m_w, v_w, b, m_b, v_b):
    def body(c_ref, dm_ref, dmc_ref, w_ref, mw_ref, vw_ref, b_ref, mb_ref, vb_ref,
             gw_ref, dw_ref, nmw_ref, nvw_ref, gb_ref, db_ref, nmb_ref, nvb_ref):
        cv = c_ref[...]
        a = cv * jax.nn.sigmoid(cv)
        gw = lax.dot_general(a, dmc_ref[...], (((0,), (0,)), ((), ())),
                             preferred_element_type=F32, precision=lax.Precision.HIGHEST)
        gw_ref[...] = gw
        dw_ref[...], nmw_ref[...], nvw_ref[...] = _adamw_math(w_ref[...], gw, mw_ref[...], vw_ref[...])
        gb = _colsum(dm_ref[...])
        gb_ref[...] = gb
        db_ref[...], nmb_ref[...], nvb_ref[...] = _adamw_math(b_ref[...], gb, mb_ref[...], vb_ref[...])

    w_sds, b_sds = jax.ShapeDtypeStruct(w.shape, F32), jax.ShapeDtypeStruct(b.shape, F32)
    outs = pl.pallas_call(
        body, name="cond_bwd_adamw", out_shape=(w_sds,) * 4 + (b_sds,) * 4, compiler_params=_params(),
    )(c_all, dmod_all, dmod_cols, w, m_w, v_w, b, m_b, v_b)
    return outs[:4], outs[4:]


def _adamw_math(w, g, m, v):
    m = ADAM_B1 * m + (1.0 - ADAM_B1) * g
    v = ADAM_B2 * v + (1.0 - ADAM_B2) * (g * g)
    m_hat = m / (1.0 - ADAM_B1 ** ADAM_STEP)
    v_hat = v / (1.0 - ADAM_B2 ** ADAM_STEP)
    delta = -ADAM_LR * (m_hat / (jnp.sqrt(v_hat) + ADAM_EPS) + ADAM_WD * w)
    return delta, m, v


def _adamw_small(ws, gparts, ms, vs, name):
    n = len(ws)

    def body(*refs):
        w_r, g_r, m_r, v_r = refs[:n], refs[n:2 * n], refs[2 * n:3 * n], refs[3 * n:4 * n]
        outs = refs[4 * n:]
        for i in range(n):
            g = g_r[i][0]
            for dev in range(1, g_r[i].shape[0]):
                g = g + g_r[i][dev]
            delta, m, v = _adamw_math(w_r[i][...], g, m_r[i][...], v_r[i][...])
            outs[i][...] = g
            outs[n + i][...] = delta
            outs[2 * n + i][...] = m
            outs[3 * n + i][...] = v

    sds = [jax.ShapeDtypeStruct(w.shape, F32) for w in ws]
    return pl.pallas_call(
        body, name=name, out_shape=tuple(sds * 4), compiler_params=_params(),
    )(*ws, *gparts, *ms, *vs)


def kernel(x, c, w_cond, b_cond, g_mix_pre, g_mix_post, w_in, w_pool, pool_scale, w_out, g_ffn_pre, g_ffn_post, w_gate, w_up, w_down, loss_target, m_w_cond, m_b_cond, m_g_mix_pre, m_g_mix_post, m_w_in, m_w_pool, m_pool_scale, m_w_out, m_g_ffn_pre, m_g_ffn_post, m_w_gate, m_w_up, m_w_down, v_w_cond, v_b_cond, v_g_mix_pre, v_g_mix_post, v_w_in, v_w_pool, v_pool_scale, v_w_out, v_g_ffn_pre, v_g_ffn_post, v_w_gate, v_w_up, v_w_down):
    n_seq, seq, d = x.shape
    t = n_seq * seq
    xi, yi, ci = _mesh_pos()
    me = 4 * xi + 2 * yi + ci
    x2 = x.reshape(t, d)
    tgt2 = loss_target.reshape(t, d)
    in_rows = w_in.shape[2]
    out_rows = w_out.shape[1]
    ff_rows = w_gate.shape[2]
    ff = N_DEV * ff_rows
    cond_cols = w_cond.shape[2]

    win_t = w_in[0].T.astype(BF16)
    wout_s = w_out[0].astype(BF16)
    wg_t = w_gate[0].T.astype(BF16)
    wu_t = w_up[0].T.astype(BF16)
    wd_s = w_down[0].astype(BF16)
    (c_all,) = _all_gather([c], [jax.ShapeDtypeStruct((N_DEV, n_seq, d), F32)], [(0, ())], "ag_c")
    c_all = c_all.reshape(N_DEV * n_seq, d)

    b_cols = lax.dynamic_slice_in_dim(b_cond, me * cond_cols, cond_cols, axis=1)
    mod_cols = _cond_fwd(c_all, w_cond[0], b_cols)
    (mod_g,) = _all_gather([mod_cols], [jax.ShapeDtypeStruct((N_DEV,) + mod_cols.shape, F32)], [(0, ())], "ag_mod")
    mod_mine = lax.dynamic_slice_in_dim(mod_g, me * n_seq, n_seq, axis=1)
    mod = jnp.transpose(mod_mine, (1, 0, 2)).reshape(n_seq, N_MOD, d)

    h1, win_g = _pre_mix(x2, g_mix_pre, mod, seq, [win_t],
                         [jax.ShapeDtypeStruct((N_DEV, in_rows, d), BF16)], [(0, ())])
    win_full = win_g.reshape(N_DEV * in_rows, d)
    proj = _matmul(h1, win_full, "nt", BF16, 512, N_DEV * in_rows, d, "proj")
    tq = ATT_TILE
    ids = jnp.arange(tq)
    tri_after = jnp.tile(-(ids[:, None] >= ids[None, :]).astype(BF16), (2, 1))
    tri_incl = (ids[:, None] <= ids[None, :]).astype(BF16)
    attn, cstats, wout_g, wgu_g, wd_g = _attn_fwd(
        proj, tri_after, n_seq, seq, [wout_s, wg_t, wu_t, wd_s],
        [jax.ShapeDtypeStruct((N_DEV, out_rows, d), BF16), jax.ShapeDtypeStruct((2, N_DEV, ff_rows, d), BF16),
         jax.ShapeDtypeStruct((N_DEV, ff_rows, d), BF16)],
        [(0, ()), (1, (0,)), (1, (1,)), (2, ())])
    wout_full = wout_g.reshape(N_DEV * out_rows, d)
    wgu_full = wgu_g.reshape(2, ff, d)
    wd_full = wd_g.reshape(ff, d)
    cat = _pool_fwd(proj, w_pool[0], pool_scale, attn, n_seq, seq)
    tok_f32, tok_bf16 = jax.ShapeDtypeStruct((t, d), F32), jax.ShapeDtypeStruct((t, d), BF16)
    seq_sds, vec_sds = jax.ShapeDtypeStruct((n_seq, 1, d), F32), jax.ShapeDtypeStruct((1, d), F32)
    mix, x1, h2 = _matmul_rows(
        cat, wout_full.reshape(2, d // 2, d), ROW_TILE, seq, "mix_mid", _mid_epilogue,
        [x2, g_mix_post, g_ffn_pre, mod], ["tok", "vec", "vec", "mod"],
        [tok_f32, tok_f32, tok_bf16], ["tok", "tok", "tok"])
    gu, act = _ffn_up(h2, wgu_full, 512, ff // 2)
    loss_sum, dy, df, dgate_f, gg_ffn_post = _matmul_rows(
        act, wd_full, ROW_TILE, seq, "ffn_down_post", _post_epilogue,
        [x1, tgt2, g_ffn_post, mod], ["tok", "tok", "vec", "mod"],
        [jax.ShapeDtypeStruct((1, LANES), F32), tok_f32, tok_bf16, seq_sds, vec_sds],
        ["loss", "tok", "tok", "seq", "vec"])

    dgu = _ffn_act_bwd(df, wd_full, gu, 512, ff // 2)
    gwd, gwd_b = _matmul(act, df, "tn", F32, ff // 2, d // 2, t, "grad_w_down", bf16_copy=True)
    gwgu, gwgu_b = _matmul(dgu, h2, "tn", F32, ff // 2, d // 2, t, "grad_w_gate_up", bf16_copy=True)
    dx1, dmix, dshift_f, dscale_f, dgate_m, gg_ffn_pre, gg_mix_post = _matmul_rows(
        dgu, wgu_full, ROW_TILE, seq, "dh2_bwd_mid", _bwd_mid_epilogue,
        [dy, x1, mix, g_ffn_pre, g_mix_post, mod], ["tok", "tok", "tok", "vec", "vec", "mod"],
        [tok_f32, tok_bf16, seq_sds, seq_sds, seq_sds, vec_sds, vec_sds],
        ["tok", "tok", "seq", "seq", "seq", "vec", "vec"])
    dcat = _matmul(dmix, wout_full, "nt", BF16, 512, d, d, "dcat")
    gwout, gwout_b = _matmul(cat, dmix, "tn", F32, d // 2, d, t, "grad_w_out", bf16_copy=True)
    dqkv, rv_wgu, rv_wd, rv_wout = _attn_bwd(
        proj, dcat, cstats, tri_after, tri_incl, n_seq, seq,
        [gwgu_b.reshape(2, N_DEV, ff_rows, d), gwd_b.reshape(1, N_DEV, ff_rows, d),
         gwout_b.reshape(1, N_DEV, out_rows, d)])
    dproj, gw_pool, gs_pool = _pool_bwd(proj, dcat, w_pool[0], pool_scale, dqkv, n_seq, seq)
    pad_d = lambda v: jnp.pad(v, ((0, 0), (0, d - v.shape[1])))
    n_gw = gw_pool.size // d
    early = jnp.concatenate(
        [gg_mix_post, gg_ffn_pre, gg_ffn_post, pad_d(gs_pool), pad_d(loss_sum), jnp.zeros((3, d), F32),
         gw_pool.reshape(n_gw, d),
         jnp.concatenate([dgate_m, dshift_f, dscale_f, dgate_f], axis=1).reshape(n_seq * 4, d)], axis=0)
    gwin0, gwin0_b, early_g = _matmul(
        dproj, h1, "tn", F32, in_rows, d, t, "grad_w_in_south", bf16_copy=True, m_tiles=(0, 1),
        ag=([early], [jax.ShapeDtypeStruct((N_DEV,) + early.shape, F32)], [(0, ())]))
    gwin1, gwin1_b, rv_win0 = _matmul(
        dproj, h1, "tn", F32, in_rows, d, t, "grad_w_in_north", bf16_copy=True, m_tiles=(1, 1),
        rs_sends=[gwin0_b[None]], rs_only_c=0)
    grad_x, dshift_m, dscale_m, gg_mix_pre, rv_win1 = _matmul_rows(
        dproj, win_full.reshape(4, d // 2, d), ROW_TILE, seq, "dh1_bwd_pre", _bwd_pre_epilogue,
        [dx1, x2, g_mix_pre, mod], ["tok", "tok", "vec", "mod"],
        [tok_f32, seq_sds, seq_sds, vec_sds], ["tok", "seq", "seq", "vec"],
        rs_sends=[gwin1_b[None]], rs_only_c=1)


    late = jnp.concatenate([gg_mix_pre, dshift_m.reshape(n_seq, d), dscale_m.reshape(n_seq, d),
                            jnp.zeros((8 - 1 - 2 * n_seq, d), F32)], axis=0)
    (late_g,) = _all_gather([late], [jax.ShapeDtypeStruct((N_DEV,) + late.shape, F32)], [(0, ())], "ag_late")
    loss = jnp.sum(early_g[:, 4, 0]) * (0.5 / d)
    dmod_all = jnp.concatenate(
        [late_g[:, 1:1 + n_seq, None, :], late_g[:, 1 + n_seq:1 + 2 * n_seq, None, :],
         early_g[:, 8 + n_gw:, :].reshape(N_DEV, n_seq, 4, d)], axis=2).reshape(N_DEV * n_seq, N_MOD * d)
    dmod_cols = lax.dynamic_slice_in_dim(dmod_all, me * cond_cols, cond_cols, axis=1)
    o_cond, o_bcond = _cond_bwd_adamw(c_all, dmod_all, dmod_cols, w_cond[0], m_w_cond[0], v_w_cond[0],
                                      b_cond, m_b_cond, v_b_cond)
    o_cond = tuple(o[None] for o in o_cond)

    small_ws = [g_mix_pre, g_mix_post, g_ffn_pre, g_ffn_post, pool_scale, w_pool.reshape(-1, POOL_GROUP_DIM)]
    small_ms = [m_g_mix_pre, m_g_mix_post, m_g_ffn_pre, m_g_ffn_post, m_pool_scale, m_w_pool.reshape(-1, POOL_GROUP_DIM)]
    small_vs = [v_g_mix_pre, v_g_mix_post, v_g_ffn_pre, v_g_ffn_post, v_pool_scale, v_w_pool.reshape(-1, POOL_GROUP_DIM)]
    small_gparts = [late_g[:, 0:1, :], early_g[:, 0:1, :], early_g[:, 1:2, :], early_g[:, 2:3, :],
                    early_g[:, 3:4, :pool_scale.shape[1]],
                    early_g[:, 8:8 + n_gw, :].reshape(N_DEV, -1, POOL_GROUP_DIM)]
    so = _adamw_small(small_ws, small_gparts, small_ms, small_vs, "adamw_small")
    ns = len(small_ws)
    sg, sdl, sm, sv = so[:ns], so[ns:2 * ns], so[2 * ns:3 * ns], so[3 * ns:]
    pool_shape = w_pool.shape
    fix = lambda lst: [lst[0], lst[1], lst[2], lst[3], lst[4], lst[5].reshape(pool_shape)]
    sg, sdl, sm, sv = fix(sg), fix(sdl), fix(sm), fix(sv)


    def reduced(mine, recv, slab, w, m, v, name, transposed=False, transpose=False):
        turn = (lambda u: u.T) if transposed else (lambda u: u)
        outs = _rs_final_adamw(mine, recv, slab, turn(w[0]), turn(m[0]), turn(v[0]), name, transpose)
        return tuple(turn(o)[None] for o in outs)

    o_in = tuple(o[None] for o in _rs_final_adamw_halves(
        (gwin0[None], gwin1[None]), (rv_win0, rv_win1), w_in[0], m_w_in[0], v_w_in[0], "adamw_w_in"))
    o_out = reduced(gwout.reshape(1, N_DEV, out_rows, d), rv_wout, 0, w_out, m_w_out, v_w_out, "adamw_w_out")
    gwgu8 = gwgu.reshape(2, N_DEV, ff_rows, d)
    o_gate = reduced(gwgu8, rv_wgu, 0, w_gate, m_w_gate, v_w_gate, "adamw_w_gate", transposed=True)
    o_up = reduced(gwgu8, rv_wgu, 1, w_up, m_w_up, v_w_up, "adamw_w_up", transposed=True)
    o_down = reduced(gwd.reshape(1, N_DEV, ff_rows, d), rv_wd, 0, w_down, m_w_down, v_w_down, "adamw_w_down")

    def pick(k):
        small_k = [sg, sdl, sm, sv][k]
        return [o_cond[k], o_bcond[k], small_k[0], small_k[1], o_in[k], small_k[5], small_k[4], o_out[k],
                small_k[2], small_k[3], o_gate[k], o_up[k], o_down[k]]

    return (loss, grad_x.reshape(n_seq, seq, d), *pick(0), *pick(1), *pick(2), *pick(3))
```

```python
import functools
import math

import jax
import jax.numpy as jnp
from jax import lax
from jax.experimental import pallas as pl
from jax.experimental.pallas import tpu as pltpu

F32 = jnp.float32
BF16 = jnp.bfloat16
MESH = pl.DeviceIdType.MESH

N_DEV = 8
HEAD_DIM = 64
LANES = 128
POOL_WINDOWS = (2, 4, 8, 16)
POOL_GROUP_DIM = 128
N_MOD = 6
EPS = 1e-6
ATT_TILE = 256
ATT_PAIRS = 2
VMEM_LIMIT = 56 * 1024 * 1024
ADAMW_COL_TILE = 256

ADAM_LR = 0.001
ADAM_B1 = 0.9
ADAM_B2 = 0.999
ADAM_EPS = 1e-08
ADAM_WD = 0.01
ADAM_STEP = 10


def _params(**kw):
    return pltpu.CompilerParams(vmem_limit_bytes=VMEM_LIMIT, **kw)


def _dot_nn(a, b):
    return jnp.dot(a, b, preferred_element_type=F32)


def _dot_nt(a, b):
    return lax.dot_general(a, b, (((1,), (1,)), ((), ())), preferred_element_type=F32)


def _dot_tn(a, b):
    return lax.dot_general(a, b, (((0,), (0,)), ((), ())), preferred_element_type=F32)


def _mesh_pos():
    return lax.axis_index("x"), lax.axis_index("y"), lax.axis_index("c")


def _ag_phases(dests, src, outs, send_sems, recv_sems, local_sems):
    n = len(src)
    x, y, c = _mesh_pos()
    me, sibling = (x, y, c), (x, y, 1 - c)
    chips = [(1 - x, y), (x, 1 - y), (1 - x, 1 - y)]

    def slot(i, dev):
        oi, prefix = dests[i]
        px, py, pc = dev
        return outs[oi].at[prefix + (4 * px + 2 * py + pc,)]

    def copy(i, k, block, to, from_src=False):
        return pltpu.make_async_remote_copy(
            src_ref=src[i] if from_src else slot(i, block), dst_ref=slot(i, block),
            send_sem=send_sems.at[i, k], recv_sem=recv_sems.at[i, k],
            device_id=to, device_id_type=MESH)

    def mine(i):
        return pltpu.make_async_copy(src[i], slot(i, me), local_sems.at[i])

    def first(i):
        return [copy(i, 0, me, sibling, from_src=True)] + [
            copy(i, 1 + j, me, (*chip, c), from_src=True) for j, chip in enumerate(chips)]

    def passed(i, j):
        return copy(i, 4 + j, (*chips[j], c), sibling)

    def start():
        for i in range(n):
            mine(i).start()
        for i in range(n):
            for cp in first(i):
                cp.start()

    def forward():
        for j, chip in enumerate(chips):
            for i in range(n):
                copy(i, 1 + j, (*chip, c), me).wait_recv()
                passed(i, j).start()

    def finish():
        for i in range(n):
            copy(i, 0, sibling, me).wait_recv()
            for j, chip in enumerate(chips):
                copy(i, 4 + j, (*chip, 1 - c), me).wait_recv()
        for i in range(n):
            for cp in first(i) + [passed(i, j) for j in range(3)]:
                cp.wait_send()
            mine(i).wait()

    return start, forward, finish


def _ag_scratch(n):
    return [pltpu.SemaphoreType.DMA((n, 7)), pltpu.SemaphoreType.DMA((n, 7)), pltpu.SemaphoreType.DMA((n,))]


def _all_gather(srcs, out_shapes, dests, name):
    n = len(srcs)

    def body(*refs):
        src = refs[:n]
        outs = refs[n:n + len(out_shapes)]
        start, forward, finish = _ag_phases(dests, src, outs, *refs[n + len(out_shapes):])
        start()
        forward()
        finish()

    any_spec = pl.BlockSpec(memory_space=pl.ANY)
    return pl.pallas_call(
        body, name=name,
        out_shape=tuple(out_shapes),
        in_specs=[any_spec] * n,
        out_specs=tuple([any_spec] * len(out_shapes)),
        scratch_shapes=_ag_scratch(n),
    )(*srcs)


def _rs_phases(shapes, src, dst, send_sems, recv_sems):
    x, y, c = _mesh_pos()

    def copies():
        out = []
        n = 0
        for i, shp in enumerate(shapes):
            for m in range(shp[0]):
                for k in range(1, N_DEV):
                    px, py, pc = x ^ (k >> 2), y ^ ((k >> 1) & 1), c ^ (k & 1)
                    out.append(pltpu.make_async_remote_copy(
                        src_ref=src[i].at[m, 4 * px + 2 * py + pc], dst_ref=dst[i].at[m, k - 1],
                        send_sem=send_sems.at[n], recv_sem=recv_sems.at[n],
                        device_id=(px, py, pc), device_id_type=MESH))
                    n += 1
        return out

    def start():
        for cp in copies():
            cp.start()

    def finish():
        for cp in copies():
            cp.wait_send()
        for cp in copies():
            cp.wait_recv()

    return start, finish


def _rs_out(sends):
    return [jax.ShapeDtypeStruct((s.shape[0], N_DEV - 1) + s.shape[2:], s.dtype) for s in sends]


def _rs_scratch(sends):
    total = sum((N_DEV - 1) * s.shape[0] for s in sends)
    return [pltpu.SemaphoreType.DMA((total,)), pltpu.SemaphoreType.DMA((total,))]


def _rs_final_adamw(mine, recv, slab, w, m, v, name, transpose=False):
    _, _, r, cdim = mine.shape
    tc = ADAMW_COL_TILE
    assert cdim % tc == 0 and w.shape == ((cdim, r) if transpose else (r, cdim)), (name, w.shape)
    x, y, c = _mesh_pos()
    me = jnp.reshape(4 * x + 2 * y + c, (1,)).astype(jnp.int32)

    def body(me_ref, p_ref, r_ref, w_ref, m_ref, v_ref, g_ref, d_ref, nm_ref, nv_ref):
        del me_ref
        g = p_ref[...]
        for k in range(N_DEV - 1):
            g = g + r_ref[k].astype(F32)
        if transpose:
            g = g.T
        g_ref[...] = g
        d_ref[...], nm_ref[...], nv_ref[...] = _adamw_math(w_ref[...], g, m_ref[...], v_ref[...])

    if transpose:
        w_spec = pl.BlockSpec((tc, r), lambda j, s: (j, 0))
    else:
        w_spec = pl.BlockSpec((r, tc), lambda j, s: (0, j))
    sds = jax.ShapeDtypeStruct(w.shape, F32)
    return pl.pallas_call(
        body, name=name, out_shape=(sds, sds, sds, sds),
        grid_spec=pltpu.PrefetchScalarGridSpec(
            num_scalar_prefetch=1, grid=(cdim // tc,),
            in_specs=[pl.BlockSpec((None, None, r, tc), lambda j, s: (slab, s[0], 0, j)),
                      pl.BlockSpec((None, N_DEV - 1, r, tc), lambda j, s: (slab, 0, 0, j)),
                      w_spec, w_spec, w_spec],
            out_specs=(w_spec, w_spec, w_spec, w_spec)),
        compiler_params=_params(),
    )(me, mine, recv, w, m, v)


def _matmul(a, b, mode, out_dtype, tm, tn, tk, name, bf16_copy=False, rs_sends=(), ag=None):
    ga = a.shape[0] if a.ndim == 3 else None
    gb = b.shape[0] if b.ndim == 3 else None
    a2, b2 = a.shape[-2:], b.shape[-2:]
    if mode == "nn":
        (m, k), n = a2, b2[1]
    elif mode == "nt":
        (m, k), n = a2, b2[0]
    else:
        (k, m), n = a2, b2[1]
    assert m % tm == 0 and n % tn == 0 and k % tk == 0, (name, m, n, k)
    nk = k // tk
    g_n = ga or 1
    batch_out = mode == "tn" and ga is not None
    n_red = nk if batch_out else nk * g_n
    dot = {"nn": _dot_nn, "nt": _dot_nt, "tn": _dot_tn}[mode]
    acc_in_out = out_dtype == F32

    n_rs = len(rs_sends)
    rs_shapes = [r.shape for r in rs_sends]
    ag_srcs, ag_out_shapes, ag_dests = ag if ag is not None else ((), (), ())
    n_ag, n_ag_out = len(ag_srcs), len(ag_out_shapes)
    n_out = 2 if bf16_copy else 1
    assert not bf16_copy or acc_in_out
    assert not (n_rs and n_ag)

    def body(a_ref, b_ref, *rest):
        rs_src, rest = rest[:n_rs], rest[n_rs:]
        ag_src, rest = rest[:n_ag], rest[n_ag:]
        o_ref = rest[0]
        copy_ref = rest[1] if bf16_copy else None
        rs_dst, rest = rest[n_out:n_out + n_rs], rest[n_out + n_rs:]
        ag_out, scratch = rest[:n_ag_out], rest[n_ag_out:]
        first = functools.reduce(jnp.logical_and, [pl.program_id(ax) == 0 for ax in range(4)])
        last = functools.reduce(jnp.logical_and, [pl.program_id(ax) == grid[ax] - 1 for ax in range(4)])
        if n_rs:
            rs_start, rs_finish = _rs_phases(rs_shapes, rs_src, rs_dst, *scratch[-2:])
            pl.when(first)(rs_start)
        if n_ag:
            ag_start, ag_forward, ag_finish = _ag_phases(ag_dests, ag_src, ag_out, *scratch[-3:])
            pl.when(first)(ag_start)
        p = dot(a_ref[...], b_ref[...])
        kk = pl.program_id(3) if batch_out else pl.program_id(2) * nk + pl.program_id(3)
        if n_red == 1:
            o_ref[...] = p.astype(out_dtype)
            if bf16_copy:
                copy_ref[...] = p.astype(BF16)
        else:
            acc = o_ref if acc_in_out else scratch[0]

            @pl.when(kk == 0)
            def _():
                acc[...] = p

            @pl.when(kk > 0)
            def _():
                acc[...] += p

            @pl.when(kk == n_red - 1)
            def _():
                if not acc_in_out:
                    o_ref[...] = acc[...].astype(out_dtype)
                if bf16_copy:
                    copy_ref[...] = acc[...].astype(BF16)

        if n_rs:
            pl.when(last)(rs_finish)
        if n_ag:
            @pl.when(last)
            def _():
                ag_forward()
                ag_finish()

    def order(ids):
        return ids if batch_out else (ids[2], ids[0], ids[1], ids[3])

    def a_idx(*ids):
        g, i, j, kq = order(ids)
        blk = {"nn": (i, kq), "nt": (i, kq), "tn": (kq, i)}[mode]
        return (g,) + blk if ga is not None else blk

    def b_idx(*ids):
        g, i, j, kq = order(ids)
        blk = {"nn": (kq, j), "nt": (j, kq), "tn": (kq, j)}[mode]
        return (g,) + blk if gb is not None else blk

    def o_idx(*ids):
        g, i, j, kq = order(ids)
        return (g, i, j) if batch_out else (i, j)

    a_blk = {"nn": (tm, tk), "nt": (tm, tk), "tn": (tk, tm)}[mode]
    b_blk = {"nn": (tk, tn), "nt": (tn, tk), "tn": (tk, tn)}[mode]
    if ga is not None:
        a_blk = (None,) + a_blk
    if gb is not None:
        b_blk = (None,) + b_blk
    if batch_out:
        out_shape = jax.ShapeDtypeStruct((g_n, m, n), out_dtype)
        o_blk = (None, tm, tn)
        grid = (g_n, m // tm, n // tn, nk)
    else:
        out_shape = jax.ShapeDtypeStruct((m, n), out_dtype)
        o_blk = (tm, tn)
        grid = (m // tm, n // tn, g_n, nk)
    scratch = [] if (acc_in_out or n_red == 1) else [pltpu.VMEM((tm, tn), F32)]
    any_spec = pl.BlockSpec(memory_space=pl.ANY)
    out_shapes = [out_shape] + ([jax.ShapeDtypeStruct(out_shape.shape, BF16)] if bf16_copy else [])
    res = pl.pallas_call(
        body, name=name, out_shape=tuple(out_shapes + _rs_out(rs_sends) + list(ag_out_shapes)), grid=grid,
        in_specs=[pl.BlockSpec(a_blk, a_idx), pl.BlockSpec(b_blk, b_idx)] + [any_spec] * (n_rs + n_ag),
        out_specs=tuple([pl.BlockSpec(o_blk, o_idx)] * n_out + [any_spec] * (n_rs + n_ag_out)),
        scratch_shapes=scratch + (_rs_scratch(rs_sends) if n_rs else []) + (_ag_scratch(n_ag) if n_ag else []),
        compiler_params=_params(),
    )(a, b, *rs_sends, *ag_srcs)
    return res if len(res) > 1 else res[0]


EW_TILE = 256
ROW_TILE = 512
EPILOGUE_CHUNKS = 8
MXU_WIDTH = 256


def _rms(v):
    return lax.rsqrt(jnp.mean(v * v, axis=-1, keepdims=True) + EPS)


def _rms_bwd(dhat, vh, r):
    return r * (dhat - vh * jnp.mean(dhat * vh, axis=-1, keepdims=True))


def _tok_spec(tm, d):
    return pl.BlockSpec((tm, d), lambda i: (i, 0))


def _vec_spec(d):
    return pl.BlockSpec((1, d), lambda i: (0, 0))


def _mod_spec(tiles_per_seq, d):
    return pl.BlockSpec((None, N_MOD, d), lambda i: (i // tiles_per_seq, 0, 0))


def _seq_acc_spec(tiles_per_seq, d):
    return pl.BlockSpec((None, 1, d), lambda i: (i // tiles_per_seq, 0, 0))


def _acc(ref, val, first):
    if first is False:
        ref[...] += val
        return

    @pl.when(first)
    def _():
        ref[...] = val

    @pl.when(jnp.logical_not(first))
    def _():
        ref[...] += val


def _colsum(v):
    return jnp.sum(v, axis=0, keepdims=True)


def _pre_mix(x2, g_pre, mod, seq, ag_srcs, ag_out_shapes, ag_dests):
    t, d = x2.shape
    tm = EW_TILE
    n_steps = t // tm
    n_ag, n_ag_out = len(ag_srcs), len(ag_out_shapes)

    def body(x_ref, g_ref, mod_ref, *rest):
        ag_src, h_ref = rest[:n_ag], rest[n_ag]
        ag_out, sems = rest[n_ag + 1:n_ag + 1 + n_ag_out], rest[n_ag + 1 + n_ag_out:]
        ag_start, ag_forward, ag_finish = _ag_phases(ag_dests, ag_src, ag_out, *sems)
        step = pl.program_id(0)
        pl.when(step == 0)(ag_start)
        xv = x_ref[...]
        n = xv * _rms(xv) * g_ref[...]
        h_ref[...] = (n * (1.0 + mod_ref[1:2, :]) + mod_ref[0:1, :]).astype(BF16)

        @pl.when(step == n_steps - 1)
        def _():
            ag_forward()
            ag_finish()

    any_spec = pl.BlockSpec(memory_space=pl.ANY)
    return pl.pallas_call(
        body, name="pre_mix", out_shape=(jax.ShapeDtypeStruct((t, d), BF16), *ag_out_shapes), grid=(n_steps,),
        in_specs=[_tok_spec(tm, d), _vec_spec(d), _mod_spec(seq // tm, d)] + [any_spec] * n_ag,
        out_specs=(_tok_spec(tm, d), *([any_spec] * n_ag_out)),
        scratch_shapes=_ag_scratch(n_ag), compiler_params=_params(),
    )(x2, g_pre, mod, *ag_srcs)


def _matmul_rows(a, b, tm, seq, name, epilogue, ep_in, ep_in_kinds, ep_out, ep_out_kinds, rs_sends=()):
    g_n = a.shape[0] if a.ndim == 3 else None
    (m, k), n = a.shape[-2:], b.shape[-1]
    tps = seq // tm
    n_i = m // tm
    n_rs = len(rs_sends)
    rs_shapes = [r.shape for r in rs_sends]
    n_in, n_out = len(ep_in), len(ep_out)
    n_cols = n // MXU_WIDTH
    rc, cw = tm // EPILOGUE_CHUNKS, n // n_cols

    def prev(i):
        return jnp.maximum(i - 1, 0)

    def spec(kind):
        return {"tok": pl.BlockSpec((tm, n), lambda i: (prev(i), 0)),
                "vec": pl.BlockSpec((1, n), lambda i: (0, 0)),
                "mod": pl.BlockSpec((None, N_MOD, n), lambda i: (prev(i) // tps, 0, 0)),
                "seq": pl.BlockSpec((None, 1, n), lambda i: (prev(i) // tps, 0, 0)),
                "loss": pl.BlockSpec((1, LANES), lambda i: (0, 0))}[kind]

    def body(a_ref, b_ref, *rest):
        in_refs, rest = rest[:n_in], rest[n_in:]
        rs_src, rest = rest[:n_rs], rest[n_rs:]
        out_refs, rest = rest[:n_out], rest[n_out:]
        rs_dst, rest = rest[:n_rs], rest[n_rs:]
        fin = rest[0]
        i = pl.program_id(0)
        if n_rs:
            rs_start, rs_finish = _rs_phases(rs_shapes, rs_src, rs_dst, *rest[1:])
            pl.when(i == 0)(rs_start)

        def product(cols):
            if g_n is None:
                return _dot_nn(a_ref[...], b_ref[:, cols])
            p = _dot_nn(a_ref[0], b_ref[0, :, cols])
            for g in range(1, g_n):
                p = p + _dot_nn(a_ref[g], b_ref[g, :, cols])
            return p

        def step(with_epilogue, with_matmul):
            parts = []
            for c in range(EPILOGUE_CHUNKS):
                if with_epilogue:
                    rows = pl.ds(c * rc, rc)
                    epilogue(fin[rows, :], i - 1, tps, in_refs, out_refs, rows, c)
                while with_matmul and len(parts) < (c + 1) * n_cols // EPILOGUE_CHUNKS:
                    cols = slice(len(parts) * cw, (len(parts) + 1) * cw)
                    parts.append((cols, product(cols)))
            for cols, v in parts:
                fin[:, cols] = v

        pl.when(i == 0)(functools.partial(step, False, True))
        pl.when(jnp.logical_and(i > 0, i < n_i))(functools.partial(step, True, True))
        pl.when(i == n_i)(functools.partial(step, True, False))

        if n_rs:
            pl.when(i == n_i)(rs_finish)

    def row(i):
        return jnp.minimum(i, n_i - 1)

    if g_n is None:
        a_spec = pl.BlockSpec((tm, k), lambda i: (row(i), 0))
        b_spec = pl.BlockSpec(b.shape, lambda i: (0, 0), pipeline_mode=pl.Buffered(1))
    else:
        a_spec = pl.BlockSpec((g_n, tm, k), lambda i: (0, row(i), 0))
        b_spec = pl.BlockSpec(b.shape, lambda i: (0, 0, 0), pipeline_mode=pl.Buffered(1))
    any_spec = pl.BlockSpec(memory_space=pl.ANY)
    res = pl.pallas_call(
        body, name=name, grid=(n_i + 1,), out_shape=tuple(list(ep_out) + _rs_out(rs_sends)),
        in_specs=[a_spec, b_spec] + [spec(kd) for kd in ep_in_kinds] + [any_spec] * n_rs,
        out_specs=tuple([spec(kd) for kd in ep_out_kinds] + [any_spec] * n_rs),
        scratch_shapes=[pltpu.VMEM((tm, n), F32)] + (_rs_scratch(rs_sends) if n_rs else []),
        compiler_params=_params(),
    )(a, b, *ep_in, *rs_sends)
    return res


def _first(cond, chunk):
    return cond if chunk == 0 else False


def _mid_epilogue(mv, i, tps, in_refs, out_refs, rows, chunk):
    x_ref, gpost_ref, gpre_ref, mod_ref = in_refs
    mix_ref, x1_ref, h2_ref = out_refs
    mix_ref[rows, :] = mv
    x1 = x_ref[rows, :] + mod_ref[2:3, :] * (mv * _rms(mv) * gpost_ref[...])
    x1_ref[rows, :] = x1
    n = x1 * _rms(x1) * gpre_ref[...]
    h2_ref[rows, :] = (n * (1.0 + mod_ref[4:5, :]) + mod_ref[3:4, :]).astype(BF16)


def _post_epilogue(fv, i, tps, in_refs, out_refs, rows, chunk):
    x1_ref, tgt_ref, g_ref, mod_ref = in_refs
    loss_ref, dy_ref, df_ref, dgate_ref, gg_ref = out_refs
    d = fv.shape[1]
    r = _rms(fv)
    fh = fv * r
    nf = fh * g_ref[...]
    gate = mod_ref[5:6, :]
    err = x1_ref[rows, :] + gate * nf - tgt_ref[rows, :]
    _acc(loss_ref, jnp.sum(_colsum(err * err), axis=1, keepdims=True) * jnp.ones((1, LANES), F32),
         _first(i == 0, chunk))
    dy = err * (1.0 / d)
    dy_ref[rows, :] = dy
    _acc(dgate_ref, _colsum(dy * nf), _first(i % tps == 0, chunk))
    dn = dy * gate
    _acc(gg_ref, _colsum(dn * fh), _first(i == 0, chunk))
    df_ref[rows, :] = _rms_bwd(dn * g_ref[...], fh, r).astype(BF16)


def _bwd_mid_epilogue(dh, i, tps, in_refs, out_refs, rows, chunk):
    dy_ref, x1_ref, mix_ref, gpre_ref, gpost_ref, mod_ref = in_refs
    dx1_ref, dmix_ref, dshift_ref, dscale_ref, dgate_ref, ggpre_ref, ggpost_ref = out_refs
    seq_first, first = _first(i % tps == 0, chunk), _first(i == 0, chunk)
    x1 = x1_ref[rows, :]
    r = _rms(x1)
    xh = x1 * r
    gpre = gpre_ref[...]
    _acc(dshift_ref, _colsum(dh), seq_first)
    _acc(dscale_ref, _colsum(dh * xh * gpre), seq_first)
    dn = dh * (1.0 + mod_ref[4:5, :])
    _acc(ggpre_ref, _colsum(dn * xh), first)
    dx1 = dy_ref[rows, :] + _rms_bwd(dn * gpre, xh, r)
    dx1_ref[rows, :] = dx1
    mv = mix_ref[rows, :]
    rm = _rms(mv)
    mh = mv * rm
    gpost = gpost_ref[...]
    _acc(dgate_ref, _colsum(dx1 * mh * gpost), seq_first)
    dnm = dx1 * mod_ref[2:3, :]
    _acc(ggpost_ref, _colsum(dnm * mh), first)
    dmix_ref[rows, :] = _rms_bwd(dnm * gpost, mh, rm).astype(BF16)


def _bwd_pre_epilogue(dh, i, tps, in_refs, out_refs, rows, chunk):
    dx1_ref, x_ref, g_ref, mod_ref = in_refs
    gx_ref, dshift_ref, dscale_ref, gg_ref = out_refs
    seq_first = _first(i % tps == 0, chunk)
    xv = x_ref[rows, :]
    r = _rms(xv)
    xh = xv * r
    g = g_ref[...]
    _acc(dshift_ref, _colsum(dh), seq_first)
    _acc(dscale_ref, _colsum(dh * xh * g), seq_first)
    dn = dh * (1.0 + mod_ref[1:2, :])
    _acc(gg_ref, _colsum(dn * xh), _first(i == 0, chunk))
    gx_ref[rows, :] = dx1_ref[rows, :] + _rms_bwd(dn * g, xh, r)


def _ffn_up(h2, wgu, tm, tn):
    t, d = h2.shape
    f = wgu.shape[1]

    def body(h_ref, w_ref, gu_ref, act_ref):
        h = h_ref[...]
        g = _dot_nt(h, w_ref[0])
        u = _dot_nt(h, w_ref[1])
        gu_ref[0] = g.astype(BF16)
        gu_ref[1] = u.astype(BF16)
        act_ref[...] = (g * jax.nn.sigmoid(g) * u).astype(BF16)

    return pl.pallas_call(
        body, name="ffn_up", grid=(f // tn, t // tm),
        out_shape=(jax.ShapeDtypeStruct((2, t, f), BF16), jax.ShapeDtypeStruct((t, f), BF16)),
        in_specs=[pl.BlockSpec((tm, d), lambda j, i: (i, 0)), pl.BlockSpec((2, tn, d), lambda j, i: (0, j, 0))],
        out_specs=(pl.BlockSpec((2, tm, tn), lambda j, i: (0, i, j)), pl.BlockSpec((tm, tn), lambda j, i: (i, j))),
        compiler_params=_params(),
    )(h2, wgu)


def _ffn_act_bwd(df, wd, gu, tm, tn):
    t, d = df.shape
    f = wd.shape[0]

    def body(df_ref, w_ref, gu_ref, dgu_ref):
        da = _dot_nt(df_ref[...], w_ref[...])
        g = gu_ref[0].astype(F32)
        u = gu_ref[1].astype(F32)
        s = jax.nn.sigmoid(g)
        silu = g * s
        dgu_ref[0] = (da * u * (s + silu * (1.0 - s))).astype(BF16)
        dgu_ref[1] = (da * silu).astype(BF16)

    return pl.pallas_call(
        body, name="ffn_act_bwd", grid=(f // tn, t // tm),
        out_shape=jax.ShapeDtypeStruct((2, t, f), BF16),
        in_specs=[pl.BlockSpec((tm, d), lambda j, i: (i, 0)), pl.BlockSpec((tn, d), lambda j, i: (j, 0)),
                  pl.BlockSpec((2, tm, tn), lambda j, i: (0, i, j))],
        out_specs=pl.BlockSpec((2, tm, tn), lambda j, i: (0, i, j)),
        compiler_params=_params(),
    )(df, wd, gu)


SIGN_BIT = 0x80000000
Q_SCALE = 1.0 / math.sqrt(HEAD_DIM)


def _softplus(z):
    neg_abs = lax.bitcast_convert_type(lax.bitcast_convert_type(z, jnp.uint32) | jnp.uint32(SIGN_BIT), F32)
    return jnp.maximum(z, 0.0) + jnp.log(1.0 + jnp.exp(neg_abs))


def _hi_lo(v):
    hi = v.astype(BF16)
    return jnp.concatenate([hi, (v - hi.astype(F32)).astype(BF16)], axis=1)


def _emit_skewed(chains, lag=1):
    for t in range(max(len(ch) for ch in chains) + lag * (len(chains) - 1)):
        for c, ch in enumerate(chains):
            if 0 <= t - lag * c < len(ch):
                ch[t - lag * c]()


def _fwd_chain(blk, qs, k_ref, v_ref, c0, kb, cols, mask, ntri, lane, tq):
    st = {}

    def scores():
        st["z"] = _dot_nt(qs, k_ref[pl.ds(c0, tq), cols])

    def soft():
        sp = _softplus(st["z"])
        if mask is not None:
            sp = jnp.where(mask, sp, 0.0)
        st["parts"] = _hi_lo(sp)
        st["cur"] = blk["cur"]
        blk["cm"] = jnp.where(lane == kb, blk["cur"], blk["cm"])
        blk["cur"] = blk["cur"] - jnp.sum(sp, axis=1, keepdims=True)

    def sums():
        st["s"] = _dot_nn(st["parts"], ntri)

    def weights():
        w = jnp.exp(st["z"] + st["s"] + st["cur"])
        if mask is not None:
            w = jnp.where(mask, w, 0.0)
        st["w"] = w.astype(BF16)

    def out():
        p = _dot_nn(st["w"], v_ref[pl.ds(c0, tq), cols])
        blk["pv"] = p if blk["pv"] is None else blk["pv"] + p

    return [scores, soft, sums, weights, out]


def _bwd_chain(blk, qs, dos, cs, k_ref, v_ref, dk_ref, dv_ref, c0, kb, cols, mask, ntri, tri_i, lane, tq):
    st = {}

    def scores():
        st["z"] = _dot_nt(qs, k_ref[pl.ds(c0, tq), cols])
        st["dw"] = _dot_nt(dos, v_ref[pl.ds(c0, tq), cols])

    def soft():
        sp = _softplus(st["z"])
        if mask is not None:
            sp = jnp.where(mask, sp, 0.0)
        st["sp"] = sp
        st["parts"] = _hi_lo(sp)
        st["cur"] = jnp.sum(jnp.where(lane == kb, cs, 0.0), axis=1, keepdims=True)

    def sums():
        st["s"] = _dot_nn(st["parts"], ntri)

    def weights():
        w = jnp.exp(st["z"] + st["s"] + st["cur"])
        if mask is not None:
            w = jnp.where(mask, w, 0.0)
        ee = w * st["dw"]
        st["w"], st["ee"], st["ec"] = w.astype(BF16), ee, blk["ec"]
        blk["ec"] = blk["ec"] + jnp.sum(ee, axis=1, keepdims=True)

    def prefix():
        st["einc"] = _dot_nn(st["ee"].astype(BF16), tri_i)

    def dz():
        v = st["ee"] - jnp.exp(st["z"] - st["sp"]) * (st["einc"] + st["ec"])
        if mask is not None:
            v = jnp.where(mask, v, 0.0)
        st["dz"] = v.astype(BF16)

    def grads():
        p = _dot_nn(st["dz"], k_ref[pl.ds(c0, tq), cols])
        blk["dq"] = p if blk["dq"] is None else blk["dq"] + p
        dk_ref[pl.ds(c0, tq), :] += _dot_tn(st["dz"], qs)
        dv_ref[pl.ds(c0, tq), :] += _dot_tn(st["w"], dos)

    return [scores, soft, sums, weights, prefix, dz, grads]


def _stack_heads(v, lane, scale=None):
    if scale is not None:
        v = v * jnp.asarray(scale, v.dtype)
    zero = jnp.zeros_like(v)
    return jnp.concatenate([jnp.where(lane < HEAD_DIM, v, zero), jnp.where(lane >= HEAD_DIM, v, zero)], axis=0)


def _diag_mask(tq):
    row = lax.broadcasted_iota(jnp.int32, (2 * tq, tq), 0)
    col = lax.broadcasted_iota(jnp.int32, (2 * tq, tq), 1)
    return col < jnp.where(row >= tq, row - tq, row)


def _attn_fwd(proj, tri_after, n_seq, seq, ag_srcs, ag_out_shapes, ag_dests):
    t = proj.shape[0]
    tq = ATT_TILE
    npp = ATT_PAIRS
    n_blk = (proj.shape[1] // 4) // (npp * LANES)
    n_ag, n_ag_out = len(ag_srcs), len(ag_out_shapes)
    n_steps = n_seq * n_blk

    def body(q_ref, k_ref, v_ref, tri_ref, *rest):
        ag_src, rest = rest[:n_ag], rest[n_ag:]
        o_ref, cs_ref = rest[:2]
        ag_out, rest = rest[2:2 + n_ag_out], rest[2 + n_ag_out:]
        oacc, cmat, carry = rest[:3]
        ag_start, ag_forward, ag_finish = _ag_phases(ag_dests, ag_src, ag_out, *rest[3:])
        step = pl.program_id(0) * n_blk + pl.program_id(1)
        pl.when(step == 0)(ag_start)
        pl.when(step == (3 * n_steps) // 4)(ag_forward)
        lane = lax.broadcasted_iota(jnp.int32, (1, LANES), 1)
        ntri = tri_ref[...]
        diag = _diag_mask(tq)

        def q_tile(qi, _):
            r0 = pl.multiple_of(qi * tq, tq)
            qs = [_stack_heads(q_ref[pl.ds(r0, tq), pp * LANES:(pp + 1) * LANES], lane, Q_SCALE)
                  for pp in range(npp)]
            carry[...] = jnp.zeros_like(carry)
            cmat[...] = jnp.zeros_like(cmat)
            oacc[...] = jnp.zeros_like(oacc)

            def run_tiles(tiles):
                blocks = [dict(cur=carry[pp], cm=cmat[pp], pv=None) for pp in range(npp)]
                chains = []
                for kb, mask in tiles:
                    c0 = pl.multiple_of(kb * tq, tq)
                    for pp in range(npp):
                        chains.append(_fwd_chain(blocks[pp], qs[pp], k_ref, v_ref, c0, kb,
                                                 slice(pp * LANES, (pp + 1) * LANES), mask, ntri, lane, tq))
                _emit_skewed(chains)
                for pp in range(npp):
                    oacc[pp] += blocks[pp]["pv"]
                    cmat[pp] = blocks[pp]["cm"]
                    carry[pp] = blocks[pp]["cur"]

            odd = qi % 2

            @pl.when(odd == 0)
            def _():
                run_tiles([(qi, diag)])

            @pl.when(odd == 1)
            def _():
                run_tiles([(qi, diag), (qi - 1, None)])

            def pair(j, _):
                kb = qi - 1 - odd - 2 * j
                run_tiles([(kb, None), (kb - 1, None)])
                return 0

            lax.fori_loop(0, qi // 2, pair, 0)
            for pp in range(npp):
                c_off = 2 * pp * LANES
                cs_ref[pl.ds(r0, tq), c_off:c_off + LANES] = cmat[pp, 0:tq, :]
                cs_ref[pl.ds(r0, tq), c_off + LANES:c_off + 2 * LANES] = cmat[pp, tq:2 * tq, :]
                o_ref[pl.ds(r0, tq), pp * LANES:(pp + 1) * LANES] = jnp.where(
                    lane < HEAD_DIM, oacc[pp, 0:tq, :], oacc[pp, tq:2 * tq, :]).astype(BF16)
            return 0

        lax.fori_loop(0, seq // tq, q_tile, 0)
        pl.when(step == n_steps - 1)(ag_finish)

    wid = npp * LANES
    blk = lambda off: pl.BlockSpec((seq, wid), lambda b, p: (b, off + p))
    any_spec = pl.BlockSpec(memory_space=pl.ANY)
    return pl.pallas_call(
        body, name="attn_fwd", grid=(n_seq, n_blk),
        out_shape=(jax.ShapeDtypeStruct((2, t, n_blk * wid), BF16),
                   jax.ShapeDtypeStruct((t, n_blk * 2 * wid), F32), *ag_out_shapes),
        in_specs=[blk(0), blk(n_blk), blk(2 * n_blk), pl.BlockSpec((2 * tq, tq), lambda b, p: (0, 0))]
        + [any_spec] * n_ag,
        out_specs=(pl.BlockSpec((None, seq, wid), lambda b, p: (0, b, p)),
                   pl.BlockSpec((seq, 2 * wid), lambda b, p: (b, p)), *([any_spec] * n_ag_out)),
        scratch_shapes=[pltpu.VMEM((npp, 2 * tq, LANES), F32), pltpu.VMEM((npp, 2 * tq, LANES), F32),
                        pltpu.VMEM((npp, 2 * tq, 1), F32)] + _ag_scratch(n_ag),
        compiler_params=_params(),
    )(proj, proj, proj, tri_after, *ag_srcs)


def _attn_bwd(proj, dcat, cstats, tri_after, tri_incl, n_seq, seq, rs_sends):
    t = proj.shape[0]
    tq = ATT_TILE
    npp = ATT_PAIRS
    width = proj.shape[1] // 4
    n_blk = width // (npp * LANES)
    n_rs = len(rs_sends)
    rs_shapes = [r.shape for r in rs_sends]
    n_steps = n_seq * n_blk

    def body(q_ref, k_ref, v_ref, do_ref, cs_ref, tria_ref, trii_ref, *rest):
        rs_src, rest = rest[:n_rs], rest[n_rs:]
        out_ref = rest[0]
        rs_dst, rest = rest[1:1 + n_rs], rest[1 + n_rs:]
        dq_acc, dk_acc, dv_acc, ecarry = rest[:4]
        rs_start, rs_finish = _rs_phases(rs_shapes, rs_src, rs_dst, *rest[4:])
        step = pl.program_id(0) * n_blk + pl.program_id(1)
        pl.when(step == 0)(rs_start)
        lane = lax.broadcasted_iota(jnp.int32, (1, LANES), 1)
        ntri = tria_ref[...]
        tri_i = trii_ref[...]
        diag = _diag_mask(tq)
        dk_acc[...] = jnp.zeros_like(dk_acc)
        dv_acc[...] = jnp.zeros_like(dv_acc)

        def q_tile(qi, _):
            r0 = pl.multiple_of(qi * tq, tq)
            qs, dos, cs = [], [], []
            for pp in range(npp):
                cols = slice(pp * LANES, (pp + 1) * LANES)
                qs.append(_stack_heads(q_ref[pl.ds(r0, tq), cols], lane, Q_SCALE))
                dos.append(_stack_heads(do_ref[pl.ds(r0, tq), cols], lane))
                c_off = 2 * pp * LANES
                cs.append(jnp.concatenate([cs_ref[pl.ds(r0, tq), c_off:c_off + LANES],
                                           cs_ref[pl.ds(r0, tq), c_off + LANES:c_off + 2 * LANES]], axis=0))
            ecarry[...] = jnp.zeros_like(ecarry)
            dq_acc[...] = jnp.zeros_like(dq_acc)

            def run_tiles(tiles):
                blocks = [dict(ec=ecarry[pp], dq=None) for pp in range(npp)]
                chains = []
                for kb, mask in tiles:
                    c0 = pl.multiple_of(kb * tq, tq)
                    for pp in range(npp):
                        chains.append(_bwd_chain(
                            blocks[pp], qs[pp], dos[pp], cs[pp], k_ref, v_ref, dk_acc.at[pp], dv_acc.at[pp],
                            c0, kb, slice(pp * LANES, (pp + 1) * LANES), mask, ntri, tri_i, lane, tq))
                _emit_skewed(chains)
                for pp in range(npp):
                    dq_acc[pp] += blocks[pp]["dq"]
                    ecarry[pp] = blocks[pp]["ec"]

            def pair(j, _):
                run_tiles([(2 * j, None), (2 * j + 1, None)])
                return 0

            lax.fori_loop(0, qi // 2, pair, 0)
            odd = qi % 2

            @pl.when(odd == 0)
            def _():
                run_tiles([(qi, diag)])

            @pl.when(odd == 1)
            def _():
                run_tiles([(qi - 1, None), (qi, diag)])

            for pp in range(npp):
                dq = jnp.where(lane < HEAD_DIM, dq_acc[pp, 0:tq, :], dq_acc[pp, tq:2 * tq, :])
                out_ref[0, pl.ds(r0, tq), pp * LANES:(pp + 1) * LANES] = (dq * Q_SCALE).astype(BF16)
            return 0

        lax.fori_loop(0, seq // tq, q_tile, 0)
        for pp in range(npp):
            cols = slice(pp * LANES, (pp + 1) * LANES)
            out_ref[1, :, cols] = dk_acc[pp].astype(BF16)
            out_ref[2, :, cols] = dv_acc[pp].astype(BF16)
        pl.when(step == n_steps - 1)(rs_finish)

    wid = npp * LANES
    blk = lambda off: pl.BlockSpec((seq, wid), lambda b, p: (b, off + p))
    tri_spec = pl.BlockSpec((2 * tq, tq), lambda b, p: (0, 0))
    any_spec = pl.BlockSpec(memory_space=pl.ANY)
    return pl.pallas_call(
        body, name="attn_bwd", grid=(n_seq, n_blk),
        out_shape=(jax.ShapeDtypeStruct((4, t, width), BF16), *_rs_out(rs_sends)),
        in_specs=[blk(0), blk(n_blk), blk(2 * n_blk), pl.BlockSpec((seq, wid), lambda b, p: (b, p)),
                  pl.BlockSpec((seq, 2 * wid), lambda b, p: (b, p)), tri_spec,
                  pl.BlockSpec((tq, tq), lambda b, p: (0, 0))] + [any_spec] * n_rs,
        out_specs=(pl.BlockSpec((3, seq, wid), lambda b, p: (0, b, p)), *([any_spec] * n_rs)),
        scratch_shapes=[pltpu.VMEM((npp, 2 * tq, LANES), F32), pltpu.VMEM((npp, seq, LANES), F32),
                        pltpu.VMEM((npp, seq, LANES), F32), pltpu.VMEM((npp, 2 * tq, 1), F32)]
        + _rs_scratch(rs_sends),
        compiler_params=_params(),
    )(proj, proj, proj, dcat, cstats, tri_after, tri_incl, *rs_sends)


def _window_sum(v, g, rows, forward):
    s_len = v.shape[0]
    s = v
    for step in range(g + 1):
        sh = 1 << step
        if forward:
            s = s + jnp.where(rows < s_len - sh, pltpu.roll(s, s_len - sh, axis=0), 0.0)
        else:
            s = s + jnp.where(rows >= sh, pltpu.roll(s, sh, axis=0), 0.0)
    return s


def _window_count(g, rows):
    return jnp.minimum(rows + 1, POOL_WINDOWS[g]).astype(F32)


def _pooled(u, g, rows):
    return _window_sum(u, g, rows, forward=False) / _window_count(g, rows) - u


def _group_cols(g):
    return slice(g * POOL_GROUP_DIM, (g + 1) * POOL_GROUP_DIM)


def _pool_fwd(proj, w_pool, pool_scale, cat, n_seq, seq):
    n_grp = len(POOL_WINDOWS)
    width = n_grp * POOL_GROUP_DIM
    assert [1 << (g + 1) for g in range(n_grp)] == list(POOL_WINDOWS)

    def body(u_ref, w_ref, s_ref, alias_ref, o_ref):
        del alias_ref
        rows = lax.broadcasted_iota(jnp.int32, (seq, 1), 0)
        for g in range(n_grp):
            cols = _group_cols(g)
            pooled = _pooled(u_ref[:, cols].astype(F32), g, rows)
            y = _dot_nn(pooled.astype(BF16), w_ref[g].astype(BF16))
            o_ref[:, cols] = (y * s_ref[:, cols]).astype(BF16)

    return pl.pallas_call(
        body, name="pool_fwd", grid=(n_seq,),
        out_shape=jax.ShapeDtypeStruct(cat.shape, BF16),
        in_specs=[pl.BlockSpec((seq, width), lambda b: (b, 3)),
                  pl.BlockSpec((n_grp, POOL_GROUP_DIM, POOL_GROUP_DIM), lambda b: (0, 0, 0)),
                  pl.BlockSpec((1, width), lambda b: (0, 0)),
                  pl.BlockSpec(memory_space=pl.ANY)],
        out_specs=pl.BlockSpec((None, seq, width), lambda b: (1, b, 0)),
        input_output_aliases={3: 0},
        compiler_params=_params(),
    )(proj, w_pool, pool_scale, cat)


def _pool_bwd(proj, dcat, w_pool, pool_scale, dqkv, n_seq, seq):
    n_grp = len(POOL_WINDOWS)
    width = n_grp * POOL_GROUP_DIM

    def body(u_ref, dp_ref, w_ref, s_ref, alias_ref, du_ref, gw_ref, gs_ref):
        del alias_ref
        b = pl.program_id(0)
        rows = lax.broadcasted_iota(jnp.int32, (seq, 1), 0)
        for g in range(n_grp):
            cols = _group_cols(g)
            pb = _pooled(u_ref[:, cols].astype(F32), g, rows).astype(BF16)
            wb = w_ref[g].astype(BF16)
            z = _dot_nn(pb, wb)
            dp = dp_ref[:, cols].astype(F32)
            _acc(gs_ref.at[:, cols], _colsum(dp * z), b == 0)
            dys = (dp * s_ref[:, cols]).astype(BF16)
            _acc(gw_ref.at[g], _dot_tn(pb, dys), b == 0)
            dpooled = _dot_nt(dys, wb)
            du = _window_sum(dpooled / _window_count(g, rows), g, rows, forward=True) - dpooled
            du_ref[:, cols] = du.astype(BF16)

    return pl.pallas_call(
        body, name="pool_bwd", grid=(n_seq,),
        out_shape=(jax.ShapeDtypeStruct(dqkv.shape, BF16),
                   jax.ShapeDtypeStruct((n_grp, POOL_GROUP_DIM, POOL_GROUP_DIM), F32),
                   jax.ShapeDtypeStruct((1, width), F32)),
        in_specs=[pl.BlockSpec((seq, width), lambda b: (b, 3)),
                  pl.BlockSpec((seq, width), lambda b: (b, 1)),
                  pl.BlockSpec((n_grp, POOL_GROUP_DIM, POOL_GROUP_DIM), lambda b: (0, 0, 0)),
                  pl.BlockSpec((1, width), lambda b: (0, 0)),
                  pl.BlockSpec(memory_space=pl.ANY)],
        out_specs=(pl.BlockSpec((None, seq, width), lambda b: (3, b, 0)),
                   pl.BlockSpec((n_grp, POOL_GROUP_DIM, POOL_GROUP_DIM), lambda b: (0, 0, 0)),
                   pl.BlockSpec((1, width), lambda b: (0, 0))),
        input_output_aliases={4: 0},
        compiler_params=_params(),
    )(proj, dcat, w_pool, pool_scale, dqkv)


def _cond_fwd(c_all, w_cond, b_cols):
    n, _ = c_all.shape
    cols = w_cond.shape[1]

    def body(c_ref, w_ref, b_ref, o_ref):
        cv = c_ref[...]
        a = cv * jax.nn.sigmoid(cv)
        o_ref[...] = jnp.dot(a, w_ref[...], preferred_element_type=F32,
                             precision=lax.Precision.HIGHEST) + b_ref[...]

    return pl.pallas_call(
        body, name="cond_fwd", out_shape=jax.ShapeDtypeStruct((n, cols), F32),
        compiler_params=_params(),
    )(c_all, w_cond, b_cols)


def _cond_bwd_adamw(c_all, dmod_all, dmod_cols, w, m_w, v_w, b, m_b, v_b):
    def body(c_ref, dm_ref, dmc_ref, w_ref, mw_ref, vw_ref, b_ref, mb_ref, vb_ref,
             gw_ref, dw_ref, nmw_ref, nvw_ref, gb_ref, db_ref, nmb_ref, nvb_ref):
        cv = c_ref[...]
        a = cv * jax.nn.sigmoid(cv)
        gw = lax.dot_general(a, dmc_ref[...], (((0,), (0,)), ((), ())),
                             preferred_element_type=F32, precision=lax.Precision.HIGHEST)
        gw_ref[...] = gw
        dw_ref[...], nmw_ref[...], nvw_ref[...] = _adamw_math(w_ref[...], gw, mw_ref[...], vw_ref[...])
        gb = _colsum(dm_ref[...])
        gb_ref[...] = gb
        db_ref[...], nmb_ref[...], nvb_ref[...] = _adamw_math(b_ref[...], gb, mb_ref[...], vb_ref[...])

    w_sds, b_sds = jax.ShapeDtypeStruct(w.shape, F32), jax.ShapeDtypeStruct(b.shape, F32)
    outs = pl.pallas_call(
        body, name="cond_bwd_adamw", out_shape=(w_sds,) * 4 + (b_sds,) * 4, compiler_params=_params(),
    )(c_all, dmod_all, dmod_cols, w, m_w, v_w, b, m_b, v_b)
    return outs[:4], outs[4:]


def _adamw_math(w, g, m, v):
    m = ADAM_B1 * m + (1.0 - ADAM_B1) * g
    v = ADAM_B2 * v + (1.0 - ADAM_B2) * (g * g)
    m_hat = m / (1.0 - ADAM_B1 ** ADAM_STEP)
    v_hat = v / (1.0 - ADAM_B2 ** ADAM_STEP)
    delta = -ADAM_LR * (m_hat / (jnp.sqrt(v_hat) + ADAM_EPS) + ADAM_WD * w)
    return delta, m, v


def _adamw_small(ws, gparts, ms, vs, name):
    n = len(ws)

    def body(*refs):
        w_r, g_r, m_r, v_r = refs[:n], refs[n:2 * n], refs[2 * n:3 * n], refs[3 * n:4 * n]
        outs = refs[4 * n:]
        for i in range(n):
            g = g_r[i][0]
            for dev in range(1, g_r[i].shape[0]):
                g = g + g_r[i][dev]
            delta, m, v = _adamw_math(w_r[i][...], g, m_r[i][...], v_r[i][...])
            outs[i][...] = g
            outs[n + i][...] = delta
            outs[2 * n + i][...] = m
            outs[3 * n + i][...] = v

    sds = [jax.ShapeDtypeStruct(w.shape, F32) for w in ws]
    return pl.pallas_call(
        body, name=name, out_shape=tuple(sds * 4), compiler_params=_params(),
    )(*ws, *gparts, *ms, *vs)


def kernel(x, c, w_cond, b_cond, g_mix_pre, g_mix_post, w_in, w_pool, pool_scale, w_out, g_ffn_pre, g_ffn_post, w_gate, w_up, w_down, loss_target, m_w_cond, m_b_cond, m_g_mix_pre, m_g_mix_post, m_w_in, m_w_pool, m_pool_scale, m_w_out, m_g_ffn_pre, m_g_ffn_post, m_w_gate, m_w_up, m_w_down, v_w_cond, v_b_cond, v_g_mix_pre, v_g_mix_post, v_w_in, v_w_pool, v_pool_scale, v_w_out, v_g_ffn_pre, v_g_ffn_post, v_w_gate, v_w_up, v_w_down):
    n_seq, seq, d = x.shape
    t = n_seq * seq
    xi, yi, ci = _mesh_pos()
    me = 4 * xi + 2 * yi + ci
    x2 = x.reshape(t, d)
    tgt2 = loss_target.reshape(t, d)
    in_rows = w_in.shape[2]
    out_rows = w_out.shape[1]
    ff_rows = w_gate.shape[2]
    ff = N_DEV * ff_rows
    cond_cols = w_cond.shape[2]

    win_t = w_in[0].T.astype(BF16)
    wout_s = w_out[0].astype(BF16)
    wg_t = w_gate[0].T.astype(BF16)
    wu_t = w_up[0].T.astype(BF16)
    wd_s = w_down[0].astype(BF16)
    (c_all,) = _all_gather([c], [jax.ShapeDtypeStruct((N_DEV, n_seq, d), F32)], [(0, ())], "ag_c")
    c_all = c_all.reshape(N_DEV * n_seq, d)

    b_cols = lax.dynamic_slice_in_dim(b_cond, me * cond_cols, cond_cols, axis=1)
    mod_cols = _cond_fwd(c_all, w_cond[0], b_cols)
    (mod_g,) = _all_gather([mod_cols], [jax.ShapeDtypeStruct((N_DEV,) + mod_cols.shape, F32)], [(0, ())], "ag_mod")
    mod_mine = lax.dynamic_slice_in_dim(mod_g, me * n_seq, n_seq, axis=1)
    mod = jnp.transpose(mod_mine, (1, 0, 2)).reshape(n_seq, N_MOD, d)

    h1, win_g = _pre_mix(x2, g_mix_pre, mod, seq, [win_t],
                         [jax.ShapeDtypeStruct((N_DEV, in_rows, d), BF16)], [(0, ())])
    win_full = win_g.reshape(N_DEV * in_rows, d)
    proj = _matmul(h1, win_full, "nt", BF16, 512, N_DEV * in_rows, d, "proj")
    tq = ATT_TILE
    ids = jnp.arange(tq)
    tri_after = jnp.tile(-(ids[:, None] >= ids[None, :]).astype(BF16), (2, 1))
    tri_incl = (ids[:, None] <= ids[None, :]).astype(BF16)
    attn, cstats, wout_g, wgu_g, wd_g = _attn_fwd(
        proj, tri_after, n_seq, seq, [wout_s, wg_t, wu_t, wd_s],
        [jax.ShapeDtypeStruct((N_DEV, out_rows, d), BF16), jax.ShapeDtypeStruct((2, N_DEV, ff_rows, d), BF16),
         jax.ShapeDtypeStruct((N_DEV, ff_rows, d), BF16)],
        [(0, ()), (1, (0,)), (1, (1,)), (2, ())])
    wout_full = wout_g.reshape(N_DEV * out_rows, d)
    wgu_full = wgu_g.reshape(2, ff, d)
    wd_full = wd_g.reshape(ff, d)
    cat = _pool_fwd(proj, w_pool[0], pool_scale, attn, n_seq, seq)
    tok_f32, tok_bf16 = jax.ShapeDtypeStruct((t, d), F32), jax.ShapeDtypeStruct((t, d), BF16)
    seq_sds, vec_sds = jax.ShapeDtypeStruct((n_seq, 1, d), F32), jax.ShapeDtypeStruct((1, d), F32)
    mix, x1, h2 = _matmul_rows(
        cat, wout_full.reshape(2, d // 2, d), ROW_TILE, seq, "mix_mid", _mid_epilogue,
        [x2, g_mix_post, g_ffn_pre, mod], ["tok", "vec", "vec", "mod"],
        [tok_f32, tok_f32, tok_bf16], ["tok", "tok", "tok"])
    gu, act = _ffn_up(h2, wgu_full, 512, ff // 2)
    loss_sum, dy, df, dgate_f, gg_ffn_post = _matmul_rows(
        act, wd_full, ROW_TILE, seq, "ffn_down_post", _post_epilogue,
        [x1, tgt2, g_ffn_post, mod], ["tok", "tok", "vec", "mod"],
        [jax.ShapeDtypeStruct((1, LANES), F32), tok_f32, tok_bf16, seq_sds, vec_sds],
        ["loss", "tok", "tok", "seq", "vec"])

    dgu = _ffn_act_bwd(df, wd_full, gu, 512, ff // 2)
    gwd, gwd_b = _matmul(act, df, "tn", F32, ff // 2, d // 2, t, "grad_w_down", bf16_copy=True)
    gwgu, gwgu_b = _matmul(dgu, h2, "tn", F32, ff // 2, d // 2, t, "grad_w_gate_up", bf16_copy=True)
    dx1, dmix, dshift_f, dscale_f, dgate_m, gg_ffn_pre, gg_mix_post = _matmul_rows(
        dgu, wgu_full, ROW_TILE, seq, "dh2_bwd_mid", _bwd_mid_epilogue,
        [dy, x1, mix, g_ffn_pre, g_mix_post, mod], ["tok", "tok", "tok", "vec", "vec", "mod"],
        [tok_f32, tok_bf16, seq_sds, seq_sds, seq_sds, vec_sds, vec_sds],
        ["tok", "tok", "seq", "seq", "seq", "vec", "vec"])
    dcat = _matmul(dmix, wout_full, "nt", BF16, 512, d, d, "dcat")
    gwout, gwout_b = _matmul(cat, dmix, "tn", F32, d // 2, d, t, "grad_w_out", bf16_copy=True)
    dqkv, rv_wgu, rv_wd, rv_wout = _attn_bwd(
        proj, dcat, cstats, tri_after, tri_incl, n_seq, seq,
        [gwgu_b.reshape(2, N_DEV, ff_rows, d), gwd_b.reshape(1, N_DEV, ff_rows, d),
         gwout_b.reshape(1, N_DEV, out_rows, d)])
    dproj, gw_pool, gs_pool = _pool_bwd(proj, dcat, w_pool[0], pool_scale, dqkv, n_seq, seq)
    pad_d = lambda v: jnp.pad(v, ((0, 0), (0, d - v.shape[1])))
    n_gw = gw_pool.size // d
    early = jnp.concatenate(
        [gg_mix_post, gg_ffn_pre, gg_ffn_post, pad_d(gs_pool), pad_d(loss_sum), jnp.zeros((3, d), F32),
         gw_pool.reshape(n_gw, d),
         jnp.concatenate([dgate_m, dshift_f, dscale_f, dgate_f], axis=1).reshape(n_seq * 4, d)], axis=0)
    gwin, gwin_b, early_g = _matmul(
        dproj, h1, "tn", F32, d // 2, d, t, "grad_w_in", bf16_copy=True,
        ag=([early], [jax.ShapeDtypeStruct((N_DEV,) + early.shape, F32)], [(0, ())]))
    grad_x, dshift_m, dscale_m, gg_mix_pre, rv_win = _matmul_rows(
        dproj, win_full.reshape(4, d // 2, d), ROW_TILE, seq, "dh1_bwd_pre", _bwd_pre_epilogue,
        [dx1, x2, g_mix_pre, mod], ["tok", "tok", "vec", "mod"],
        [tok_f32, seq_sds, seq_sds, vec_sds], ["tok", "seq", "seq", "vec"],
        rs_sends=[gwin_b.reshape(1, N_DEV, in_rows, d)])


    late = jnp.concatenate([gg_mix_pre, dshift_m.reshape(n_seq, d), dscale_m.reshape(n_seq, d),
                            jnp.zeros((8 - 1 - 2 * n_seq, d), F32)], axis=0)
    (late_g,) = _all_gather([late], [jax.ShapeDtypeStruct((N_DEV,) + late.shape, F32)], [(0, ())], "ag_late")
    loss = jnp.sum(early_g[:, 4, 0]) * (0.5 / d)
    dmod_all = jnp.concatenate(
        [late_g[:, 1:1 + n_seq, None, :], late_g[:, 1 + n_seq:1 + 2 * n_seq, None, :],
         early_g[:, 8 + n_gw:, :].reshape(N_DEV, n_seq, 4, d)], axis=2).reshape(N_DEV * n_seq, N_MOD * d)
    dmod_cols = lax.dynamic_slice_in_dim(dmod_all, me * cond_cols, cond_cols, axis=1)
    o_cond, o_bcond = _cond_bwd_adamw(c_all, dmod_all, dmod_cols, w_cond[0], m_w_cond[0], v_w_cond[0],
                                      b_cond, m_b_cond, v_b_cond)
    o_cond = tuple(o[None] for o in o_cond)

    small_ws = [g_mix_pre, g_mix_post, g_ffn_pre, g_ffn_post, pool_scale, w_pool.reshape(-1, POOL_GROUP_DIM)]
    small_ms = [m_g_mix_pre, m_g_mix_post, m_g_ffn_pre, m_g_ffn_post, m_pool_scale, m_w_pool.reshape(-1, POOL_GROUP_DIM)]
    small_vs = [v_g_mix_pre, v_g_mix_post, v_g_ffn_pre, v_g_ffn_post, v_pool_scale, v_w_pool.reshape(-1, POOL_GROUP_DIM)]
    small_gparts = [late_g[:, 0:1, :], early_g[:, 0:1, :], early_g[:, 1:2, :], early_g[:, 2:3, :],
                    early_g[:, 3:4, :pool_scale.shape[1]],
                    early_g[:, 8:8 + n_gw, :].reshape(N_DEV, -1, POOL_GROUP_DIM)]
    so = _adamw_small(small_ws, small_gparts, small_ms, small_vs, "adamw_small")
    ns = len(small_ws)
    sg, sdl, sm, sv = so[:ns], so[ns:2 * ns], so[2 * ns:3 * ns], so[3 * ns:]
    pool_shape = w_pool.shape
    fix = lambda lst: [lst[0], lst[1], lst[2], lst[3], lst[4], lst[5].reshape(pool_shape)]
    sg, sdl, sm, sv = fix(sg), fix(sdl), fix(sm), fix(sv)


    def reduced(mine, recv, slab, w, m, v, name, transposed=False, transpose=False):
        turn = (lambda u: u.T) if transposed else (lambda u: u)
        outs = _rs_final_adamw(mine, recv, slab, turn(w[0]), turn(m[0]), turn(v[0]), name, transpose)
        return tuple(turn(o)[None] for o in outs)

    o_in = reduced(gwin.reshape(1, N_DEV, in_rows, d), rv_win, 0, w_in, m_w_in, v_w_in, "adamw_w_in",
                   transpose=True)
    o_out = reduced(gwout.reshape(1, N_DEV, out_rows, d), rv_wout, 0, w_out, m_w_out, v_w_out, "adamw_w_out")
    gwgu8 = gwgu.reshape(2, N_DEV, ff_rows, d)
    o_gate = reduced(gwgu8, rv_wgu, 0, w_gate, m_w_gate, v_w_gate, "adamw_w_gate", transposed=True)
    o_up = reduced(gwgu8, rv_wgu, 1, w_up, m_w_up, v_w_up, "adamw_w_up", transposed=True)
    o_down = reduced(gwd.reshape(1, N_DEV, ff_rows, d), rv_wd, 0, w_down, m_w_down, v_w_down, "adamw_w_down")

    def pick(k):
        small_k = [sg, sdl, sm, sv][k]
        return [o_cond[k], o_bcond[k], small_k[0], small_k[1], o_in[k], small_k[5], small_k[4], o_out[k],
                small_k[2], small_k[3], o_gate[k], o_up[k], o_down[k]]

    return (loss, grad_x.reshape(n_seq, seq, d), *pick(0), *pick(1), *pick(2), *pick(3))
```

```python
import functools
import math

import jax
import jax.numpy as jnp
from jax import lax
from jax.experimental import pallas as pl
from jax.experimental.pallas import tpu as pltpu

F32 = jnp.float32
BF16 = jnp.bfloat16
MESH = pl.DeviceIdType.MESH

N_DEV = 8
HEAD_DIM = 64
LANES = 128
POOL_WINDOWS = (2, 4, 8, 16)
POOL_GROUP_DIM = 128
N_MOD = 6
EPS = 1e-6
ATT_TILE = 256
ATT_PAIRS = 2
VMEM_LIMIT = 56 * 1024 * 1024
ADAMW_COL_TILE = 256

ADAM_LR = 0.001
ADAM_B1 = 0.9
ADAM_B2 = 0.999
ADAM_EPS = 1e-08
ADAM_WD = 0.01
ADAM_STEP = 10


def _params(**kw):
    return pltpu.CompilerParams(vmem_limit_bytes=VMEM_LIMIT, **kw)


def _dot_nn(a, b):
    return jnp.dot(a, b, preferred_element_type=F32)


def _dot_nt(a, b):
    return lax.dot_general(a, b, (((1,), (1,)), ((), ())), preferred_element_type=F32)


def _dot_tn(a, b):
    return lax.dot_general(a, b, (((0,), (0,)), ((), ())), preferred_element_type=F32)


def _mesh_pos():
    return lax.axis_index("x"), lax.axis_index("y"), lax.axis_index("c")


def _ag_phases(dests, src, outs, send_sems, recv_sems, local_sems):
    n = len(src)
    x, y, c = _mesh_pos()
    me, sibling = (x, y, c), (x, y, 1 - c)
    chips = [(1 - x, y), (x, 1 - y), (1 - x, 1 - y)]

    def slot(i, dev):
        oi, prefix = dests[i]
        px, py, pc = dev
        return outs[oi].at[prefix + (4 * px + 2 * py + pc,)]

    def copy(i, k, block, to, from_src=False):
        return pltpu.make_async_remote_copy(
            src_ref=src[i] if from_src else slot(i, block), dst_ref=slot(i, block),
            send_sem=send_sems.at[i, k], recv_sem=recv_sems.at[i, k],
            device_id=to, device_id_type=MESH)

    def mine(i):
        return pltpu.make_async_copy(src[i], slot(i, me), local_sems.at[i])

    def first(i):
        return [copy(i, 0, me, sibling, from_src=True)] + [
            copy(i, 1 + j, me, (*chip, c), from_src=True) for j, chip in enumerate(chips)]

    def passed(i, j):
        return copy(i, 4 + j, (*chips[j], c), sibling)

    def start():
        for i in range(n):
            mine(i).start()
        for i in range(n):
            for cp in first(i):
                cp.start()

    def forward():
        for j, chip in enumerate(chips):
            for i in range(n):
                copy(i, 1 + j, (*chip, c), me).wait_recv()
                passed(i, j).start()

    def finish():
        for i in range(n):
            copy(i, 0, sibling, me).wait_recv()
            for j, chip in enumerate(chips):
                copy(i, 4 + j, (*chip, 1 - c), me).wait_recv()
        for i in range(n):
            for cp in first(i) + [passed(i, j) for j in range(3)]:
                cp.wait_send()
            mine(i).wait()

    return start, forward, finish


def _ag_scratch(n):
    return [pltpu.SemaphoreType.DMA((n, 7)), pltpu.SemaphoreType.DMA((n, 7)), pltpu.SemaphoreType.DMA((n,))]


def _all_gather(src, name):
    def body(src_ref, out_ref, send_sems, recv_sems, local_sem):
        x, y, c = _mesh_pos()

        def copy(k, block):
            px, py, pc = x ^ (k >> 2), y ^ ((k >> 1) & 1), c ^ (k & 1)
            bx, by, bc = (x, y, c) if block == "mine" else (px, py, pc)
            return pltpu.make_async_remote_copy(
                src_ref=src_ref, dst_ref=out_ref.at[4 * bx + 2 * by + bc],
                send_sem=send_sems.at[k - 1], recv_sem=recv_sems.at[k - 1],
                device_id=(px, py, pc), device_id_type=MESH)

        local = pltpu.make_async_copy(src_ref, out_ref.at[4 * x + 2 * y + c], local_sem.at[0])
        local.start()
        for k in range(1, N_DEV):
            copy(k, "mine").start()
        for k in range(1, N_DEV):
            copy(k, "mine").wait_send()
        for k in range(1, N_DEV):
            copy(k, "theirs").wait_recv()
        local.wait()

    any_spec = pl.BlockSpec(memory_space=pl.ANY)
    return pl.pallas_call(
        body, name=name,
        out_shape=jax.ShapeDtypeStruct((N_DEV,) + src.shape, src.dtype),
        in_specs=[any_spec], out_specs=any_spec,
        scratch_shapes=[pltpu.SemaphoreType.DMA((N_DEV - 1,)), pltpu.SemaphoreType.DMA((N_DEV - 1,)),
                        pltpu.SemaphoreType.DMA((1,))],
    )(src)


def _rs_phases(shapes, src, dst, send_sems, recv_sems):
    x, y, c = _mesh_pos()

    def copies():
        out = []
        n = 0
        for i, shp in enumerate(shapes):
            for m in range(shp[0]):
                for k in range(1, N_DEV):
                    px, py, pc = x ^ (k >> 2), y ^ ((k >> 1) & 1), c ^ (k & 1)
                    out.append(pltpu.make_async_remote_copy(
                        src_ref=src[i].at[m, 4 * px + 2 * py + pc], dst_ref=dst[i].at[m, k - 1],
                        send_sem=send_sems.at[n], recv_sem=recv_sems.at[n],
                        device_id=(px, py, pc), device_id_type=MESH))
                    n += 1
        return out

    def start():
        for cp in copies():
            cp.start()

    def finish():
        for cp in copies():
            cp.wait_send()
        for cp in copies():
            cp.wait_recv()

    return start, finish


def _rs_out(sends):
    return [jax.ShapeDtypeStruct((s.shape[0], N_DEV - 1) + s.shape[2:], s.dtype) for s in sends]


def _rs_scratch(sends):
    total = sum((N_DEV - 1) * s.shape[0] for s in sends)
    return [pltpu.SemaphoreType.DMA((total,)), pltpu.SemaphoreType.DMA((total,))]


def _rs_final_adamw(mine, recv, slab, w, m, v, name, transpose=False):
    _, _, r, cdim = mine.shape
    tc = ADAMW_COL_TILE
    assert cdim % tc == 0 and w.shape == ((cdim, r) if transpose else (r, cdim)), (name, w.shape)
    x, y, c = _mesh_pos()
    me = jnp.reshape(4 * x + 2 * y + c, (1,)).astype(jnp.int32)

    def body(me_ref, p_ref, r_ref, w_ref, m_ref, v_ref, g_ref, d_ref, nm_ref, nv_ref):
        del me_ref
        g = p_ref[...]
        for k in range(N_DEV - 1):
            g = g + r_ref[k].astype(F32)
        if transpose:
            g = g.T
        g_ref[...] = g
        d_ref[...], nm_ref[...], nv_ref[...] = _adamw_math(w_ref[...], g, m_ref[...], v_ref[...])

    if transpose:
        w_spec = pl.BlockSpec((tc, r), lambda j, s: (j, 0))
    else:
        w_spec = pl.BlockSpec((r, tc), lambda j, s: (0, j))
    sds = jax.ShapeDtypeStruct(w.shape, F32)
    return pl.pallas_call(
        body, name=name, out_shape=(sds, sds, sds, sds),
        grid_spec=pltpu.PrefetchScalarGridSpec(
            num_scalar_prefetch=1, grid=(cdim // tc,),
            in_specs=[pl.BlockSpec((None, None, r, tc), lambda j, s: (slab, s[0], 0, j)),
                      pl.BlockSpec((None, N_DEV - 1, r, tc), lambda j, s: (slab, 0, 0, j)),
                      w_spec, w_spec, w_spec],
            out_specs=(w_spec, w_spec, w_spec, w_spec)),
        compiler_params=_params(),
    )(me, mine, recv, w, m, v)


def _matmul(a, b, mode, out_dtype, tm, tn, tk, name, bf16_copy=False, rs_sends=(), ag=None):
    ga = a.shape[0] if a.ndim == 3 else None
    gb = b.shape[0] if b.ndim == 3 else None
    a2, b2 = a.shape[-2:], b.shape[-2:]
    if mode == "nn":
        (m, k), n = a2, b2[1]
    elif mode == "nt":
        (m, k), n = a2, b2[0]
    else:
        (k, m), n = a2, b2[1]
    assert m % tm == 0 and n % tn == 0 and k % tk == 0, (name, m, n, k)
    nk = k // tk
    g_n = ga or 1
    batch_out = mode == "tn" and ga is not None
    n_red = nk if batch_out else nk * g_n
    dot = {"nn": _dot_nn, "nt": _dot_nt, "tn": _dot_tn}[mode]
    acc_in_out = out_dtype == F32

    n_rs = len(rs_sends)
    rs_shapes = [r.shape for r in rs_sends]
    ag_srcs, ag_out_shapes, ag_dests = ag if ag is not None else ((), (), ())
    n_ag, n_ag_out = len(ag_srcs), len(ag_out_shapes)
    n_out = 2 if bf16_copy else 1
    assert not bf16_copy or acc_in_out
    assert not (n_rs and n_ag)

    def body(a_ref, b_ref, *rest):
        rs_src, rest = rest[:n_rs], rest[n_rs:]
        ag_src, rest = rest[:n_ag], rest[n_ag:]
        o_ref = rest[0]
        copy_ref = rest[1] if bf16_copy else None
        rs_dst, rest = rest[n_out:n_out + n_rs], rest[n_out + n_rs:]
        ag_out, scratch = rest[:n_ag_out], rest[n_ag_out:]
        first = functools.reduce(jnp.logical_and, [pl.program_id(ax) == 0 for ax in range(4)])
        last = functools.reduce(jnp.logical_and, [pl.program_id(ax) == grid[ax] - 1 for ax in range(4)])
        if n_rs:
            rs_start, rs_finish = _rs_phases(rs_shapes, rs_src, rs_dst, *scratch[-2:])
            pl.when(first)(rs_start)
        if n_ag:
            ag_start, ag_forward, ag_finish = _ag_phases(ag_dests, ag_src, ag_out, *scratch[-3:])
            pl.when(first)(ag_start)
        p = dot(a_ref[...], b_ref[...])
        kk = pl.program_id(3) if batch_out else pl.program_id(2) * nk + pl.program_id(3)
        if n_red == 1:
            o_ref[...] = p.astype(out_dtype)
            if bf16_copy:
                copy_ref[...] = p.astype(BF16)
        else:
            acc = o_ref if acc_in_out else scratch[0]

            @pl.when(kk == 0)
            def _():
                acc[...] = p

            @pl.when(kk > 0)
            def _():
                acc[...] += p

            @pl.when(kk == n_red - 1)
            def _():
                if not acc_in_out:
                    o_ref[...] = acc[...].astype(out_dtype)
                if bf16_copy:
                    copy_ref[...] = acc[...].astype(BF16)

        if n_rs:
            pl.when(last)(rs_finish)
        if n_ag:
            @pl.when(last)
            def _():
                ag_forward()
                ag_finish()

    def order(ids):
        return ids if batch_out else (ids[2], ids[0], ids[1], ids[3])

    def a_idx(*ids):
        g, i, j, kq = order(ids)
        blk = {"nn": (i, kq), "nt": (i, kq), "tn": (kq, i)}[mode]
        return (g,) + blk if ga is not None else blk

    def b_idx(*ids):
        g, i, j, kq = order(ids)
        blk = {"nn": (kq, j), "nt": (j, kq), "tn": (kq, j)}[mode]
        return (g,) + blk if gb is not None else blk

    def o_idx(*ids):
        g, i, j, kq = order(ids)
        return (g, i, j) if batch_out else (i, j)

    a_blk = {"nn": (tm, tk), "nt": (tm, tk), "tn": (tk, tm)}[mode]
    b_blk = {"nn": (tk, tn), "nt": (tn, tk), "tn": (tk, tn)}[mode]
    if ga is not None:
        a_blk = (None,) + a_blk
    if gb is not None:
        b_blk = (None,) + b_blk
    if batch_out:
        out_shape = jax.ShapeDtypeStruct((g_n, m, n), out_dtype)
        o_blk = (None, tm, tn)
        grid = (g_n, m // tm, n // tn, nk)
    else:
        out_shape = jax.ShapeDtypeStruct((m, n), out_dtype)
        o_blk = (tm, tn)
        grid = (m // tm, n // tn, g_n, nk)
    scratch = [] if (acc_in_out or n_red == 1) else [pltpu.VMEM((tm, tn), F32)]
    any_spec = pl.BlockSpec(memory_space=pl.ANY)
    out_shapes = [out_shape] + ([jax.ShapeDtypeStruct(out_shape.shape, BF16)] if bf16_copy else [])
    res = pl.pallas_call(
        body, name=name, out_shape=tuple(out_shapes + _rs_out(rs_sends) + list(ag_out_shapes)), grid=grid,
        in_specs=[pl.BlockSpec(a_blk, a_idx), pl.BlockSpec(b_blk, b_idx)] + [any_spec] * (n_rs + n_ag),
        out_specs=tuple([pl.BlockSpec(o_blk, o_idx)] * n_out + [any_spec] * (n_rs + n_ag_out)),
        scratch_shapes=scratch + (_rs_scratch(rs_sends) if n_rs else []) + (_ag_scratch(n_ag) if n_ag else []),
        compiler_params=_params(),
    )(a, b, *rs_sends, *ag_srcs)
    return res if len(res) > 1 else res[0]


EW_TILE = 256
ROW_TILE = 512
EPILOGUE_CHUNKS = 8
MXU_WIDTH = 256


def _rms(v):
    return lax.rsqrt(jnp.mean(v * v, axis=-1, keepdims=True) + EPS)


def _rms_bwd(dhat, vh, r):
    return r * (dhat - vh * jnp.mean(dhat * vh, axis=-1, keepdims=True))


def _tok_spec(tm, d):
    return pl.BlockSpec((tm, d), lambda i: (i, 0))


def _vec_spec(d):
    return pl.BlockSpec((1, d), lambda i: (0, 0))


def _mod_spec(tiles_per_seq, d):
    return pl.BlockSpec((None, N_MOD, d), lambda i: (i // tiles_per_seq, 0, 0))


def _seq_acc_spec(tiles_per_seq, d):
    return pl.BlockSpec((None, 1, d), lambda i: (i // tiles_per_seq, 0, 0))


def _acc(ref, val, first):
    if first is False:
        ref[...] += val
        return

    @pl.when(first)
    def _():
        ref[...] = val

    @pl.when(jnp.logical_not(first))
    def _():
        ref[...] += val


def _colsum(v):
    return jnp.sum(v, axis=0, keepdims=True)


def _pre_mix(x2, g_pre, mod, seq, ag_srcs, ag_out_shapes, ag_dests):
    t, d = x2.shape
    tm = EW_TILE
    n_steps = t // tm
    n_ag, n_ag_out = len(ag_srcs), len(ag_out_shapes)

    def body(x_ref, g_ref, mod_ref, *rest):
        ag_src, h_ref = rest[:n_ag], rest[n_ag]
        ag_out, sems = rest[n_ag + 1:n_ag + 1 + n_ag_out], rest[n_ag + 1 + n_ag_out:]
        ag_start, ag_forward, ag_finish = _ag_phases(ag_dests, ag_src, ag_out, *sems)
        step = pl.program_id(0)
        pl.when(step == 0)(ag_start)
        xv = x_ref[...]
        n = xv * _rms(xv) * g_ref[...]
        h_ref[...] = (n * (1.0 + mod_ref[1:2, :]) + mod_ref[0:1, :]).astype(BF16)

        @pl.when(step == n_steps - 1)
        def _():
            ag_forward()
            ag_finish()

    any_spec = pl.BlockSpec(memory_space=pl.ANY)
    return pl.pallas_call(
        body, name="pre_mix", out_shape=(jax.ShapeDtypeStruct((t, d), BF16), *ag_out_shapes), grid=(n_steps,),
        in_specs=[_tok_spec(tm, d), _vec_spec(d), _mod_spec(seq // tm, d)] + [any_spec] * n_ag,
        out_specs=(_tok_spec(tm, d), *([any_spec] * n_ag_out)),
        scratch_shapes=_ag_scratch(n_ag), compiler_params=_params(),
    )(x2, g_pre, mod, *ag_srcs)


def _matmul_rows(a, b, tm, seq, name, epilogue, ep_in, ep_in_kinds, ep_out, ep_out_kinds, rs_sends=()):
    g_n = a.shape[0] if a.ndim == 3 else None
    (m, k), n = a.shape[-2:], b.shape[-1]
    tps = seq // tm
    n_i = m // tm
    n_rs = len(rs_sends)
    rs_shapes = [r.shape for r in rs_sends]
    n_in, n_out = len(ep_in), len(ep_out)
    n_cols = n // MXU_WIDTH
    rc, cw = tm // EPILOGUE_CHUNKS, n // n_cols

    def prev(i):
        return jnp.maximum(i - 1, 0)

    def spec(kind):
        return {"tok": pl.BlockSpec((tm, n), lambda i: (prev(i), 0)),
                "vec": pl.BlockSpec((1, n), lambda i: (0, 0)),
                "mod": pl.BlockSpec((None, N_MOD, n), lambda i: (prev(i) // tps, 0, 0)),
                "seq": pl.BlockSpec((None, 1, n), lambda i: (prev(i) // tps, 0, 0)),
                "loss": pl.BlockSpec((1, LANES), lambda i: (0, 0))}[kind]

    def body(a_ref, b_ref, *rest):
        in_refs, rest = rest[:n_in], rest[n_in:]
        rs_src, rest = rest[:n_rs], rest[n_rs:]
        out_refs, rest = rest[:n_out], rest[n_out:]
        rs_dst, rest = rest[:n_rs], rest[n_rs:]
        fin = rest[0]
        i = pl.program_id(0)
        if n_rs:
            rs_start, rs_finish = _rs_phases(rs_shapes, rs_src, rs_dst, *rest[1:])
            pl.when(i == 0)(rs_start)

        def product(cols):
            if g_n is None:
                return _dot_nn(a_ref[...], b_ref[:, cols])
            p = _dot_nn(a_ref[0], b_ref[0, :, cols])
            for g in range(1, g_n):
                p = p + _dot_nn(a_ref[g], b_ref[g, :, cols])
            return p

        def step(with_epilogue, with_matmul):
            parts = []
            for c in range(EPILOGUE_CHUNKS):
                if with_epilogue:
                    rows = pl.ds(c * rc, rc)
                    epilogue(fin[rows, :], i - 1, tps, in_refs, out_refs, rows, c)
                while with_matmul and len(parts) < (c + 1) * n_cols // EPILOGUE_CHUNKS:
                    cols = slice(len(parts) * cw, (len(parts) + 1) * cw)
                    parts.append((cols, product(cols)))
            for cols, v in parts:
                fin[:, cols] = v

        pl.when(i == 0)(functools.partial(step, False, True))
        pl.when(jnp.logical_and(i > 0, i < n_i))(functools.partial(step, True, True))
        pl.when(i == n_i)(functools.partial(step, True, False))

        if n_rs:
            pl.when(i == n_i)(rs_finish)

    def row(i):
        return jnp.minimum(i, n_i - 1)

    if g_n is None:
        a_spec = pl.BlockSpec((tm, k), lambda i: (row(i), 0))
        b_spec = pl.BlockSpec(b.shape, lambda i: (0, 0), pipeline_mode=pl.Buffered(1))
    else:
        a_spec = pl.BlockSpec((g_n, tm, k), lambda i: (0, row(i), 0))
        b_spec = pl.BlockSpec(b.shape, lambda i: (0, 0, 0), pipeline_mode=pl.Buffered(1))
    any_spec = pl.BlockSpec(memory_space=pl.ANY)
    res = pl.pallas_call(
        body, name=name, grid=(n_i + 1,), out_shape=tuple(list(ep_out) + _rs_out(rs_sends)),
        in_specs=[a_spec, b_spec] + [spec(kd) for kd in ep_in_kinds] + [any_spec] * n_rs,
        out_specs=tuple([spec(kd) for kd in ep_out_kinds] + [any_spec] * n_rs),
        scratch_shapes=[pltpu.VMEM((tm, n), F32)] + (_rs_scratch(rs_sends) if n_rs else []),
        compiler_params=_params(),
    )(a, b, *ep_in, *rs_sends)
    return res


def _first(cond, chunk):
    return cond if chunk == 0 else False


def _mid_epilogue(mv, i, tps, in_refs, out_refs, rows, chunk):
    x_ref, gpost_ref, gpre_ref, mod_ref = in_refs
    mix_ref, x1_ref, h2_ref = out_refs
    mix_ref[rows, :] = mv
    x1 = x_ref[rows, :] + mod_ref[2:3, :] * (mv * _rms(mv) * gpost_ref[...])
    x1_ref[rows, :] = x1
    n = x1 * _rms(x1) * gpre_ref[...]
    h2_ref[rows, :] = (n * (1.0 + mod_ref[4:5, :]) + mod_ref[3:4, :]).astype(BF16)


def _post_epilogue(fv, i, tps, in_refs, out_refs, rows, chunk):
    x1_ref, tgt_ref, g_ref, mod_ref = in_refs
    loss_ref, dy_ref, df_ref, dgate_ref, gg_ref = out_refs
    d = fv.shape[1]
    r = _rms(fv)
    fh = fv * r
    nf = fh * g_ref[...]
    gate = mod_ref[5:6, :]
    err = x1_ref[rows, :] + gate * nf - tgt_ref[rows, :]
    _acc(loss_ref, jnp.sum(_colsum(err * err), axis=1, keepdims=True) * jnp.ones((1, LANES), F32),
         _first(i == 0, chunk))
    dy = err * (1.0 / d)
    dy_ref[rows, :] = dy
    _acc(dgate_ref, _colsum(dy * nf), _first(i % tps == 0, chunk))
    dn = dy * gate
    _acc(gg_ref, _colsum(dn * fh), _first(i == 0, chunk))
    df_ref[rows, :] = _rms_bwd(dn * g_ref[...], fh, r).astype(BF16)


def _bwd_mid_epilogue(dh, i, tps, in_refs, out_refs, rows, chunk):
    dy_ref, x1_ref, mix_ref, gpre_ref, gpost_ref, mod_ref = in_refs
    dx1_ref, dmix_ref, dshift_ref, dscale_ref, dgate_ref, ggpre_ref, ggpost_ref = out_refs
    seq_first, first = _first(i % tps == 0, chunk), _first(i == 0, chunk)
    x1 = x1_ref[rows, :]
    r = _rms(x1)
    xh = x1 * r
    gpre = gpre_ref[...]
    _acc(dshift_ref, _colsum(dh), seq_first)
    _acc(dscale_ref, _colsum(dh * xh * gpre), seq_first)
    dn = dh * (1.0 + mod_ref[4:5, :])
    _acc(ggpre_ref, _colsum(dn * xh), first)
    dx1 = dy_ref[rows, :] + _rms_bwd(dn * gpre, xh, r)
    dx1_ref[rows, :] = dx1
    mv = mix_ref[rows, :]
    rm = _rms(mv)
    mh = mv * rm
    gpost = gpost_ref[...]
    _acc(dgate_ref, _colsum(dx1 * mh * gpost), seq_first)
    dnm = dx1 * mod_ref[2:3, :]
    _acc(ggpost_ref, _colsum(dnm * mh), first)
    dmix_ref[rows, :] = _rms_bwd(dnm * gpost, mh, rm).astype(BF16)


def _bwd_pre_epilogue(dh, i, tps, in_refs, out_refs, rows, chunk):
    dx1_ref, x_ref, g_ref, mod_ref = in_refs
    gx_ref, dshift_ref, dscale_ref, gg_ref = out_refs
    seq_first = _first(i % tps == 0, chunk)
    xv = x_ref[rows, :]
    r = _rms(xv)
    xh = xv * r
    g = g_ref[...]
    _acc(dshift_ref, _colsum(dh), seq_first)
    _acc(dscale_ref, _colsum(dh * xh * g), seq_first)
    dn = dh * (1.0 + mod_ref[1:2, :])
    _acc(gg_ref, _colsum(dn * xh), _first(i == 0, chunk))
    gx_ref[rows, :] = dx1_ref[rows, :] + _rms_bwd(dn * g, xh, r)


def _ffn_up(h2, wgu, tm, tn):
    t, d = h2.shape
    f = wgu.shape[1]

    def body(h_ref, w_ref, gu_ref, act_ref):
        h = h_ref[...]
        g = _dot_nt(h, w_ref[0])
        u = _dot_nt(h, w_ref[1])
        gu_ref[0] = g.astype(BF16)
        gu_ref[1] = u.astype(BF16)
        act_ref[...] = (g * jax.nn.sigmoid(g) * u).astype(BF16)

    return pl.pallas_call(
        body, name="ffn_up", grid=(f // tn, t // tm),
        out_shape=(jax.ShapeDtypeStruct((2, t, f), BF16), jax.ShapeDtypeStruct((t, f), BF16)),
        in_specs=[pl.BlockSpec((tm, d), lambda j, i: (i, 0)), pl.BlockSpec((2, tn, d), lambda j, i: (0, j, 0))],
        out_specs=(pl.BlockSpec((2, tm, tn), lambda j, i: (0, i, j)), pl.BlockSpec((tm, tn), lambda j, i: (i, j))),
        compiler_params=_params(),
    )(h2, wgu)


def _ffn_act_bwd(df, wd, gu, tm, tn):
    t, d = df.shape
    f = wd.shape[0]

    def body(df_ref, w_ref, gu_ref, dgu_ref):
        da = _dot_nt(df_ref[...], w_ref[...])
        g = gu_ref[0].astype(F32)
        u = gu_ref[1].astype(F32)
        s = jax.nn.sigmoid(g)
        silu = g * s
        dgu_ref[0] = (da * u * (s + silu * (1.0 - s))).astype(BF16)
        dgu_ref[1] = (da * silu).astype(BF16)

    return pl.pallas_call(
        body, name="ffn_act_bwd", grid=(f // tn, t // tm),
        out_shape=jax.ShapeDtypeStruct((2, t, f), BF16),
        in_specs=[pl.BlockSpec((tm, d), lambda j, i: (i, 0)), pl.BlockSpec((tn, d), lambda j, i: (j, 0)),
                  pl.BlockSpec((2, tm, tn), lambda j, i: (0, i, j))],
        out_specs=pl.BlockSpec((2, tm, tn), lambda j, i: (0, i, j)),
        compiler_params=_params(),
    )(df, wd, gu)


SIGN_BIT = 0x80000000
Q_SCALE = 1.0 / math.sqrt(HEAD_DIM)


def _softplus(z):
    neg_abs = lax.bitcast_convert_type(lax.bitcast_convert_type(z, jnp.uint32) | jnp.uint32(SIGN_BIT), F32)
    return jnp.maximum(z, 0.0) + jnp.log(1.0 + jnp.exp(neg_abs))


def _hi_lo(v):
    hi = v.astype(BF16)
    return jnp.concatenate([hi, (v - hi.astype(F32)).astype(BF16)], axis=1)


def _emit_skewed(chains, lag=1):
    for t in range(max(len(ch) for ch in chains) + lag * (len(chains) - 1)):
        for c, ch in enumerate(chains):
            if 0 <= t - lag * c < len(ch):
                ch[t - lag * c]()


def _fwd_chain(blk, qs, k_ref, v_ref, c0, kb, cols, mask, ntri, lane, tq):
    st = {}

    def scores():
        st["z"] = _dot_nt(qs, k_ref[pl.ds(c0, tq), cols])

    def soft():
        sp = _softplus(st["z"])
        if mask is not None:
            sp = jnp.where(mask, sp, 0.0)
        st["parts"] = _hi_lo(sp)
        st["cur"] = blk["cur"]
        blk["cm"] = jnp.where(lane == kb, blk["cur"], blk["cm"])
        blk["cur"] = blk["cur"] - jnp.sum(sp, axis=1, keepdims=True)

    def sums():
        st["s"] = _dot_nn(st["parts"], ntri)

    def weights():
        w = jnp.exp(st["z"] + st["s"] + st["cur"])
        if mask is not None:
            w = jnp.where(mask, w, 0.0)
        st["w"] = w.astype(BF16)

    def out():
        p = _dot_nn(st["w"], v_ref[pl.ds(c0, tq), cols])
        blk["pv"] = p if blk["pv"] is None else blk["pv"] + p

    return [scores, soft, sums, weights, out]


def _bwd_chain(blk, qs, dos, cs, k_ref, v_ref, dk_ref, dv_ref, c0, kb, cols, mask, ntri, tri_i, lane, tq):
    st = {}

    def scores():
        st["z"] = _dot_nt(qs, k_ref[pl.ds(c0, tq), cols])
        st["dw"] = _dot_nt(dos, v_ref[pl.ds(c0, tq), cols])

    def soft():
        sp = _softplus(st["z"])
        if mask is not None:
            sp = jnp.where(mask, sp, 0.0)
        st["sp"] = sp
        st["parts"] = _hi_lo(sp)
        st["cur"] = jnp.sum(jnp.where(lane == kb, cs, 0.0), axis=1, keepdims=True)

    def sums():
        st["s"] = _dot_nn(st["parts"], ntri)

    def weights():
        w = jnp.exp(st["z"] + st["s"] + st["cur"])
        if mask is not None:
            w = jnp.where(mask, w, 0.0)
        ee = w * st["dw"]
        st["w"], st["ee"], st["ec"] = w.astype(BF16), ee, blk["ec"]
        blk["ec"] = blk["ec"] + jnp.sum(ee, axis=1, keepdims=True)

    def prefix():
        st["einc"] = _dot_nn(st["ee"].astype(BF16), tri_i)

    def dz():
        v = st["ee"] - jnp.exp(st["z"] - st["sp"]) * (st["einc"] + st["ec"])
        if mask is not None:
            v = jnp.where(mask, v, 0.0)
        st["dz"] = v.astype(BF16)

    def grads():
        p = _dot_nn(st["dz"], k_ref[pl.ds(c0, tq), cols])
        blk["dq"] = p if blk["dq"] is None else blk["dq"] + p
        dk_ref[pl.ds(c0, tq), :] += _dot_tn(st["dz"], qs)
        dv_ref[pl.ds(c0, tq), :] += _dot_tn(st["w"], dos)

    return [scores, soft, sums, weights, prefix, dz, grads]


def _stack_heads(v, lane, scale=None):
    if scale is not None:
        v = v * jnp.asarray(scale, v.dtype)
    zero = jnp.zeros_like(v)
    return jnp.concatenate([jnp.where(lane < HEAD_DIM, v, zero), jnp.where(lane >= HEAD_DIM, v, zero)], axis=0)


def _diag_mask(tq):
    row = lax.broadcasted_iota(jnp.int32, (2 * tq, tq), 0)
    col = lax.broadcasted_iota(jnp.int32, (2 * tq, tq), 1)
    return col < jnp.where(row >= tq, row - tq, row)


def _attn_fwd(proj, tri_after, n_seq, seq, ag_srcs, ag_out_shapes, ag_dests):
    t = proj.shape[0]
    tq = ATT_TILE
    npp = ATT_PAIRS
    n_blk = (proj.shape[1] // 4) // (npp * LANES)
    n_ag, n_ag_out = len(ag_srcs), len(ag_out_shapes)
    n_steps = n_seq * n_blk

    def body(q_ref, k_ref, v_ref, tri_ref, *rest):
        ag_src, rest = rest[:n_ag], rest[n_ag:]
        o_ref, cs_ref = rest[:2]
        ag_out, rest = rest[2:2 + n_ag_out], rest[2 + n_ag_out:]
        oacc, cmat, carry = rest[:3]
        ag_start, ag_forward, ag_finish = _ag_phases(ag_dests, ag_src, ag_out, *rest[3:])
        step = pl.program_id(0) * n_blk + pl.program_id(1)
        pl.when(step == 0)(ag_start)
        pl.when(step == (3 * n_steps) // 4)(ag_forward)
        lane = lax.broadcasted_iota(jnp.int32, (1, LANES), 1)
        ntri = tri_ref[...]
        diag = _diag_mask(tq)

        def q_tile(qi, _):
            r0 = pl.multiple_of(qi * tq, tq)
            qs = [_stack_heads(q_ref[pl.ds(r0, tq), pp * LANES:(pp + 1) * LANES], lane, Q_SCALE)
                  for pp in range(npp)]
            carry[...] = jnp.zeros_like(carry)
            cmat[...] = jnp.zeros_like(cmat)
            oacc[...] = jnp.zeros_like(oacc)

            def run_tiles(tiles):
                blocks = [dict(cur=carry[pp], cm=cmat[pp], pv=None) for pp in range(npp)]
                chains = []
                for kb, mask in tiles:
                    c0 = pl.multiple_of(kb * tq, tq)
                    for pp in range(npp):
                        chains.append(_fwd_chain(blocks[pp], qs[pp], k_ref, v_ref, c0, kb,
                                                 slice(pp * LANES, (pp + 1) * LANES), mask, ntri, lane, tq))
                _emit_skewed(chains)
                for pp in range(npp):
                    oacc[pp] += blocks[pp]["pv"]
                    cmat[pp] = blocks[pp]["cm"]
                    carry[pp] = blocks[pp]["cur"]

            odd = qi % 2

            @pl.when(odd == 0)
            def _():
                run_tiles([(qi, diag)])

            @pl.when(odd == 1)
            def _():
                run_tiles([(qi, diag), (qi - 1, None)])

            def pair(j, _):
                kb = qi - 1 - odd - 2 * j
                run_tiles([(kb, None), (kb - 1, None)])
                return 0

            lax.fori_loop(0, qi // 2, pair, 0)
            for pp in range(npp):
                c_off = 2 * pp * LANES
                cs_ref[pl.ds(r0, tq), c_off:c_off + LANES] = cmat[pp, 0:tq, :]
                cs_ref[pl.ds(r0, tq), c_off + LANES:c_off + 2 * LANES] = cmat[pp, tq:2 * tq, :]
                o_ref[pl.ds(r0, tq), pp * LANES:(pp + 1) * LANES] = jnp.where(
                    lane < HEAD_DIM, oacc[pp, 0:tq, :], oacc[pp, tq:2 * tq, :]).astype(BF16)
            return 0

        lax.fori_loop(0, seq // tq, q_tile, 0)
        pl.when(step == n_steps - 1)(ag_finish)

    wid = npp * LANES
    blk = lambda off: pl.BlockSpec((seq, wid), lambda b, p: (b, off + p))
    any_spec = pl.BlockSpec(memory_space=pl.ANY)
    return pl.pallas_call(
        body, name="attn_fwd", grid=(n_seq, n_blk),
        out_shape=(jax.ShapeDtypeStruct((2, t, n_blk * wid), BF16),
                   jax.ShapeDtypeStruct((t, n_blk * 2 * wid), F32), *ag_out_shapes),
        in_specs=[blk(0), blk(n_blk), blk(2 * n_blk), pl.BlockSpec((2 * tq, tq), lambda b, p: (0, 0))]
        + [any_spec] * n_ag,
        out_specs=(pl.BlockSpec((None, seq, wid), lambda b, p: (0, b, p)),
                   pl.BlockSpec((seq, 2 * wid), lambda b, p: (b, p)), *([any_spec] * n_ag_out)),
        scratch_shapes=[pltpu.VMEM((npp, 2 * tq, LANES), F32), pltpu.VMEM((npp, 2 * tq, LANES), F32),
                        pltpu.VMEM((npp, 2 * tq, 1), F32)] + _ag_scratch(n_ag),
        compiler_params=_params(),
    )(proj, proj, proj, tri_after, *ag_srcs)


def _attn_bwd(proj, dcat, cstats, tri_after, tri_incl, n_seq, seq, rs_sends):
    t = proj.shape[0]
    tq = ATT_TILE
    npp = ATT_PAIRS
    width = proj.shape[1] // 4
    n_blk = width // (npp * LANES)
    n_rs = len(rs_sends)
    rs_shapes = [r.shape for r in rs_sends]
    n_steps = n_seq * n_blk

    def body(q_ref, k_ref, v_ref, do_ref, cs_ref, tria_ref, trii_ref, *rest):
        rs_src, rest = rest[:n_rs], rest[n_rs:]
        out_ref = rest[0]
        rs_dst, rest = rest[1:1 + n_rs], rest[1 + n_rs:]
        dq_acc, dk_acc, dv_acc, ecarry = rest[:4]
        rs_start, rs_finish = _rs_phases(rs_shapes, rs_src, rs_dst, *rest[4:])
        step = pl.program_id(0) * n_blk + pl.program_id(1)
        pl.when(step == 0)(rs_start)
        lane = lax.broadcasted_iota(jnp.int32, (1, LANES), 1)
        ntri = tria_ref[...]
        tri_i = trii_ref[...]
        diag = _diag_mask(tq)
        dk_acc[...] = jnp.zeros_like(dk_acc)
        dv_acc[...] = jnp.zeros_like(dv_acc)

        def q_tile(qi, _):
            r0 = pl.multiple_of(qi * tq, tq)
            qs, dos, cs = [], [], []
            for pp in range(npp):
                cols = slice(pp * LANES, (pp + 1) * LANES)
                qs.append(_stack_heads(q_ref[pl.ds(r0, tq), cols], lane, Q_SCALE))
                dos.append(_stack_heads(do_ref[pl.ds(r0, tq), cols], lane))
                c_off = 2 * pp * LANES
                cs.append(jnp.concatenate([cs_ref[pl.ds(r0, tq), c_off:c_off + LANES],
                                           cs_ref[pl.ds(r0, tq), c_off + LANES:c_off + 2 * LANES]], axis=0))
            ecarry[...] = jnp.zeros_like(ecarry)
            dq_acc[...] = jnp.zeros_like(dq_acc)

            def run_tiles(tiles):
                blocks = [dict(ec=ecarry[pp], dq=None) for pp in range(npp)]
                chains = []
                for kb, mask in tiles:
                    c0 = pl.multiple_of(kb * tq, tq)
                    for pp in range(npp):
                        chains.append(_bwd_chain(
                            blocks[pp], qs[pp], dos[pp], cs[pp], k_ref, v_ref, dk_acc.at[pp], dv_acc.at[pp],
                            c0, kb, slice(pp * LANES, (pp + 1) * LANES), mask, ntri, tri_i, lane, tq))
                _emit_skewed(chains)
                for pp in range(npp):
                    dq_acc[pp] += blocks[pp]["dq"]
                    ecarry[pp] = blocks[pp]["ec"]

            def pair(j, _):
                run_tiles([(2 * j, None), (2 * j + 1, None)])
                return 0

            lax.fori_loop(0, qi // 2, pair, 0)
            odd = qi % 2

            @pl.when(odd == 0)
            def _():
                run_tiles([(qi, diag)])

            @pl.when(odd == 1)
            def _():
                run_tiles([(qi - 1, None), (qi, diag)])

            for pp in range(npp):
                dq = jnp.where(lane < HEAD_DIM, dq_acc[pp, 0:tq, :], dq_acc[pp, tq:2 * tq, :])
                out_ref[0, pl.ds(r0, tq), pp * LANES:(pp + 1) * LANES] = (dq * Q_SCALE).astype(BF16)
            return 0

        lax.fori_loop(0, seq // tq, q_tile, 0)
        for pp in range(npp):
            cols = slice(pp * LANES, (pp + 1) * LANES)
            out_ref[1, :, cols] = dk_acc[pp].astype(BF16)
            out_ref[2, :, cols] = dv_acc[pp].astype(BF16)
        pl.when(step == n_steps - 1)(rs_finish)

    wid = npp * LANES
    blk = lambda off: pl.BlockSpec((seq, wid), lambda b, p: (b, off + p))
    tri_spec = pl.BlockSpec((2 * tq, tq), lambda b, p: (0, 0))
    any_spec = pl.BlockSpec(memory_space=pl.ANY)
    return pl.pallas_call(
        body, name="attn_bwd", grid=(n_seq, n_blk),
        out_shape=(jax.ShapeDtypeStruct((4, t, width), BF16), *_rs_out(rs_sends)),
        in_specs=[blk(0), blk(n_blk), blk(2 * n_blk), pl.BlockSpec((seq, wid), lambda b, p: (b, p)),
                  pl.BlockSpec((seq, 2 * wid), lambda b, p: (b, p)), tri_spec,
                  pl.BlockSpec((tq, tq), lambda b, p: (0, 0))] + [any_spec] * n_rs,
        out_specs=(pl.BlockSpec((3, seq, wid), lambda b, p: (0, b, p)), *([any_spec] * n_rs)),
        scratch_shapes=[pltpu.VMEM((npp, 2 * tq, LANES), F32), pltpu.VMEM((npp, seq, LANES), F32),
                        pltpu.VMEM((npp, seq, LANES), F32), pltpu.VMEM((npp, 2 * tq, 1), F32)]
        + _rs_scratch(rs_sends),
        compiler_params=_params(),
    )(proj, proj, proj, dcat, cstats, tri_after, tri_incl, *rs_sends)


def _window_sum(v, g, rows, forward):
    s_len = v.shape[0]
    s = v
    for step in range(g + 1):
        sh = 1 << step
        if forward:
            s = s + jnp.where(rows < s_len - sh, pltpu.roll(s, s_len - sh, axis=0), 0.0)
        else:
            s = s + jnp.where(rows >= sh, pltpu.roll(s, sh, axis=0), 0.0)
    return s


def _window_count(g, rows):
    return jnp.minimum(rows + 1, POOL_WINDOWS[g]).astype(F32)


def _pooled(u, g, rows):
    return _window_sum(u, g, rows, forward=False) / _window_count(g, rows) - u


def _group_cols(g):
    return slice(g * POOL_GROUP_DIM, (g + 1) * POOL_GROUP_DIM)


def _pool_fwd(proj, w_pool, pool_scale, cat, n_seq, seq):
    n_grp = len(POOL_WINDOWS)
    width = n_grp * POOL_GROUP_DIM
    assert [1 << (g + 1) for g in range(n_grp)] == list(POOL_WINDOWS)

    def body(u_ref, w_ref, s_ref, alias_ref, o_ref):
        del alias_ref
        rows = lax.broadcasted_iota(jnp.int32, (seq, 1), 0)
        for g in range(n_grp):
            cols = _group_cols(g)
            pooled = _pooled(u_ref[:, cols].astype(F32), g, rows)
            y = _dot_nn(pooled.astype(BF16), w_ref[g].astype(BF16))
            o_ref[:, cols] = (y * s_ref[:, cols]).astype(BF16)

    return pl.pallas_call(
        body, name="pool_fwd", grid=(n_seq,),
        out_shape=jax.ShapeDtypeStruct(cat.shape, BF16),
        in_specs=[pl.BlockSpec((seq, width), lambda b: (b, 3)),
                  pl.BlockSpec((n_grp, POOL_GROUP_DIM, POOL_GROUP_DIM), lambda b: (0, 0, 0)),
                  pl.BlockSpec((1, width), lambda b: (0, 0)),
                  pl.BlockSpec(memory_space=pl.ANY)],
        out_specs=pl.BlockSpec((None, seq, width), lambda b: (1, b, 0)),
        input_output_aliases={3: 0},
        compiler_params=_params(),
    )(proj, w_pool, pool_scale, cat)


def _pool_bwd(proj, dcat, w_pool, pool_scale, dqkv, n_seq, seq):
    n_grp = len(POOL_WINDOWS)
    width = n_grp * POOL_GROUP_DIM

    def body(u_ref, dp_ref, w_ref, s_ref, alias_ref, du_ref, gw_ref, gs_ref):
        del alias_ref
        b = pl.program_id(0)
        rows = lax.broadcasted_iota(jnp.int32, (seq, 1), 0)
        for g in range(n_grp):
            cols = _group_cols(g)
            pb = _pooled(u_ref[:, cols].astype(F32), g, rows).astype(BF16)
            wb = w_ref[g].astype(BF16)
            z = _dot_nn(pb, wb)
            dp = dp_ref[:, cols].astype(F32)
            _acc(gs_ref.at[:, cols], _colsum(dp * z), b == 0)
            dys = (dp * s_ref[:, cols]).astype(BF16)
            _acc(gw_ref.at[g], _dot_tn(pb, dys), b == 0)
            dpooled = _dot_nt(dys, wb)
            du = _window_sum(dpooled / _window_count(g, rows), g, rows, forward=True) - dpooled
            du_ref[:, cols] = du.astype(BF16)

    return pl.pallas_call(
        body, name="pool_bwd", grid=(n_seq,),
        out_shape=(jax.ShapeDtypeStruct(dqkv.shape, BF16),
                   jax.ShapeDtypeStruct((n_grp, POOL_GROUP_DIM, POOL_GROUP_DIM), F32),
                   jax.ShapeDtypeStruct((1, width), F32)),
        in_specs=[pl.BlockSpec((seq, width), lambda b: (b, 3)),
                  pl.BlockSpec((seq, width), lambda b: (b, 1)),
                  pl.BlockSpec((n_grp, POOL_GROUP_DIM, POOL_GROUP_DIM), lambda b: (0, 0, 0)),
                  pl.BlockSpec((1, width), lambda b: (0, 0)),
                  pl.BlockSpec(memory_space=pl.ANY)],
        out_specs=(pl.BlockSpec((None, seq, width), lambda b: (3, b, 0)),
                   pl.BlockSpec((n_grp, POOL_GROUP_DIM, POOL_GROUP_DIM), lambda b: (0, 0, 0)),
                   pl.BlockSpec((1, width), lambda b: (0, 0))),
        input_output_aliases={4: 0},
        compiler_params=_params(),
    )(proj, dcat, w_pool, pool_scale, dqkv)


def _cond_fwd(c_all, w_cond, b_cols):
    n, _ = c_all.shape
    cols = w_cond.shape[1]

    def body(c_ref, w_ref, b_ref, o_ref):
        cv = c_ref[...]
        a = cv * jax.nn.sigmoid(cv)
        o_ref[...] = jnp.dot(a, w_ref[...], preferred_element_type=F32,
                             precision=lax.Precision.HIGHEST) + b_ref[...]

    return pl.pallas_call(
        body, name="cond_fwd", out_shape=jax.ShapeDtypeStruct((n, cols), F32),
        compiler_params=_params(),
    )(c_all, w_cond, b_cols)


def _cond_bwd_adamw(c_all, dmod_all, dmod_cols, w, m_w, v_w, b, m_b, v_b):
    def body(c_ref, dm_ref, dmc_ref, w_ref, mw_ref, vw_ref, b_ref, mb_ref, vb_ref,
             gw_ref, dw_ref, nmw_ref, nvw_ref, gb_ref, db_ref, nmb_ref, nvb_ref):
        cv = c_ref[...]
        a = cv * jax.nn.sigmoid(cv)
        gw = lax.dot_general(a, dmc_ref[...], (((0,), (0,)), ((), ())),
                             preferred_element_type=F32, precision=lax.Precision.HIGHEST)
        gw_ref[...] = gw
        dw_ref[...], nmw_ref[...], nvw_ref[...] = _adamw_math(w_ref[...], gw, mw_ref[...], vw_ref[...])
        gb = _colsum(dm_ref[...])
        gb_ref[...] = gb
        db_ref[...], nmb_ref[...], nvb_ref[...] = _adamw_math(b_ref[...], gb, mb_ref[...], vb_ref[...])

    w_sds, b_sds = jax.ShapeDtypeStruct(w.shape, F32), jax.ShapeDtypeStruct(b.shape, F32)
    outs = pl.pallas_call(
        body, name="cond_bwd_adamw", out_shape=(w_sds,) * 4 + (b_sds,) * 4, compiler_params=_params(),
    )(c_all, dmod_all, dmod_cols, w, m_w, v_w, b, m_b, v_b)
    return outs[:4], outs[4:]


def _adamw_math(w, g, m, v):
    m = ADAM_B1 * m + (1.0 - ADAM_B1) * g
    v = ADAM_B2 * v + (1.0 - ADAM_B2) * (g * g)
    m_hat = m / (1.0 - ADAM_B1 ** ADAM_STEP)
    v_hat = v / (1.0 - ADAM_B2 ** ADAM_STEP)
    delta = -ADAM_LR * (m_hat / (jnp.sqrt(v_hat) + ADAM_EPS) + ADAM_WD * w)
    return delta, m, v


def _adamw_small(ws, gparts, ms, vs, name):
    n = len(ws)

    def body(*refs):
        w_r, g_r, m_r, v_r = refs[:n], refs[n:2 * n], refs[2 * n:3 * n], refs[3 * n:4 * n]
        outs = refs[4 * n:]
        for i in range(n):
            g = g_r[i][0]
            for dev in range(1, g_r[i].shape[0]):
                g = g + g_r[i][dev]
            delta, m, v = _adamw_math(w_r[i][...], g, m_r[i][...], v_r[i][...])
            outs[i][...] = g
            outs[n + i][...] = delta
            outs[2 * n + i][...] = m
            outs[3 * n + i][...] = v

    sds = [jax.ShapeDtypeStruct(w.shape, F32) for w in ws]
    return pl.pallas_call(
        body, name=name, out_shape=tuple(sds * 4), compiler_params=_params(),
    )(*ws, *gparts, *ms, *vs)


def kernel(x, c, w_cond, b_cond, g_mix_pre, g_mix_post, w_in, w_pool, pool_scale, w_out, g_ffn_pre, g_ffn_post, w_gate, w_up, w_down, loss_target, m_w_cond, m_b_cond, m_g_mix_pre, m_g_mix_post, m_w_in, m_w_pool, m_pool_scale, m_w_out, m_g_ffn_pre, m_g_ffn_post, m_w_gate, m_w_up, m_w_down, v_w_cond, v_b_cond, v_g_mix_pre, v_g_mix_post, v_w_in, v_w_pool, v_pool_scale, v_w_out, v_g_ffn_pre, v_g_ffn_post, v_w_gate, v_w_up, v_w_down):
    n_seq, seq, d = x.shape
    t = n_seq * seq
    xi, yi, ci = _mesh_pos()
    me = 4 * xi + 2 * yi + ci
    x2 = x.reshape(t, d)
    tgt2 = loss_target.reshape(t, d)
    in_rows = w_in.shape[2]
    out_rows = w_out.shape[1]
    ff_rows = w_gate.shape[2]
    ff = N_DEV * ff_rows
    cond_cols = w_cond.shape[2]

    win_t = w_in[0].T.astype(BF16)
    wout_s = w_out[0].astype(BF16)
    wg_t = w_gate[0].T.astype(BF16)
    wu_t = w_up[0].T.astype(BF16)
    wd_s = w_down[0].astype(BF16)
    c_all = _all_gather(c, "ag_c").reshape(N_DEV * n_seq, d)

    b_cols = lax.dynamic_slice_in_dim(b_cond, me * cond_cols, cond_cols, axis=1)
    mod_cols = _cond_fwd(c_all, w_cond[0], b_cols)
    mod_g = _all_gather(mod_cols, "ag_mod")
    mod_mine = lax.dynamic_slice_in_dim(mod_g, me * n_seq, n_seq, axis=1)
    mod = jnp.transpose(mod_mine, (1, 0, 2)).reshape(n_seq, N_MOD, d)

    h1, win_g = _pre_mix(x2, g_mix_pre, mod, seq, [win_t],
                         [jax.ShapeDtypeStruct((N_DEV, in_rows, d), BF16)], [(0, ())])
    win_full = win_g.reshape(N_DEV * in_rows, d)
    proj = _matmul(h1, win_full, "nt", BF16, 512, N_DEV * in_rows, d, "proj")
    tq = ATT_TILE
    ids = jnp.arange(tq)
    tri_after = jnp.tile(-(ids[:, None] >= ids[None, :]).astype(BF16), (2, 1))
    tri_incl = (ids[:, None] <= ids[None, :]).astype(BF16)
    attn, cstats, wout_g, wgu_g, wd_g = _attn_fwd(
        proj, tri_after, n_seq, seq, [wout_s, wg_t, wu_t, wd_s],
        [jax.ShapeDtypeStruct((N_DEV, out_rows, d), BF16), jax.ShapeDtypeStruct((2, N_DEV, ff_rows, d), BF16),
         jax.ShapeDtypeStruct((N_DEV, ff_rows, d), BF16)],
        [(0, ()), (1, (0,)), (1, (1,)), (2, ())])
    wout_full = wout_g.reshape(N_DEV * out_rows, d)
    wgu_full = wgu_g.reshape(2, ff, d)
    wd_full = wd_g.reshape(ff, d)
    cat = _pool_fwd(proj, w_pool[0], pool_scale, attn, n_seq, seq)
    tok_f32, tok_bf16 = jax.ShapeDtypeStruct((t, d), F32), jax.ShapeDtypeStruct((t, d), BF16)
    seq_sds, vec_sds = jax.ShapeDtypeStruct((n_seq, 1, d), F32), jax.ShapeDtypeStruct((1, d), F32)
    mix, x1, h2 = _matmul_rows(
        cat, wout_full.reshape(2, d // 2, d), ROW_TILE, seq, "mix_mid", _mid_epilogue,
        [x2, g_mix_post, g_ffn_pre, mod], ["tok", "vec", "vec", "mod"],
        [tok_f32, tok_f32, tok_bf16], ["tok", "tok", "tok"])
    gu, act = _ffn_up(h2, wgu_full, 512, ff // 2)
    loss_sum, dy, df, dgate_f, gg_ffn_post = _matmul_rows(
        act, wd_full, ROW_TILE, seq, "ffn_down_post", _post_epilogue,
        [x1, tgt2, g_ffn_post, mod], ["tok", "tok", "vec", "mod"],
        [jax.ShapeDtypeStruct((1, LANES), F32), tok_f32, tok_bf16, seq_sds, vec_sds],
        ["loss", "tok", "tok", "seq", "vec"])

    dgu = _ffn_act_bwd(df, wd_full, gu, 512, ff // 2)
    gwd, gwd_b = _matmul(act, df, "tn", F32, ff // 2, d // 2, t, "grad_w_down", bf16_copy=True)
    gwgu, gwgu_b = _matmul(dgu, h2, "tn", F32, ff // 2, d // 2, t, "grad_w_gate_up", bf16_copy=True)
    dx1, dmix, dshift_f, dscale_f, dgate_m, gg_ffn_pre, gg_mix_post = _matmul_rows(
        dgu, wgu_full, ROW_TILE, seq, "dh2_bwd_mid", _bwd_mid_epilogue,
        [dy, x1, mix, g_ffn_pre, g_mix_post, mod], ["tok", "tok", "tok", "vec", "vec", "mod"],
        [tok_f32, tok_bf16, seq_sds, seq_sds, seq_sds, vec_sds, vec_sds],
        ["tok", "tok", "seq", "seq", "seq", "vec", "vec"])
    dcat = _matmul(dmix, wout_full, "nt", BF16, 512, d, d, "dcat")
    gwout, gwout_b = _matmul(cat, dmix, "tn", F32, d // 2, d, t, "grad_w_out", bf16_copy=True)
    dqkv, rv_wgu, rv_wd, rv_wout = _attn_bwd(
        proj, dcat, cstats, tri_after, tri_incl, n_seq, seq,
        [gwgu_b.reshape(2, N_DEV, ff_rows, d), gwd_b.reshape(1, N_DEV, ff_rows, d),
         gwout_b.reshape(1, N_DEV, out_rows, d)])
    dproj, gw_pool, gs_pool = _pool_bwd(proj, dcat, w_pool[0], pool_scale, dqkv, n_seq, seq)
    pad_d = lambda v: jnp.pad(v, ((0, 0), (0, d - v.shape[1])))
    n_gw = gw_pool.size // d
    early = jnp.concatenate(
        [gg_mix_post, gg_ffn_pre, gg_ffn_post, pad_d(gs_pool), pad_d(loss_sum), jnp.zeros((3, d), F32),
         gw_pool.reshape(n_gw, d),
         jnp.concatenate([dgate_m, dshift_f, dscale_f, dgate_f], axis=1).reshape(n_seq * 4, d)], axis=0)
    gwin, gwin_b, early_g = _matmul(
        dproj, h1, "tn", F32, d // 2, d, t, "grad_w_in", bf16_copy=True,
        ag=([early], [jax.ShapeDtypeStruct((N_DEV,) + early.shape, F32)], [(0, ())]))
    grad_x, dshift_m, dscale_m, gg_mix_pre, rv_win = _matmul_rows(
        dproj, win_full.reshape(4, d // 2, d), ROW_TILE, seq, "dh1_bwd_pre", _bwd_pre_epilogue,
        [dx1, x2, g_mix_pre, mod], ["tok", "tok", "vec", "mod"],
        [tok_f32, seq_sds, seq_sds, vec_sds], ["tok", "seq", "seq", "vec"],
        rs_sends=[gwin_b.reshape(1, N_DEV, in_rows, d)])


    late = jnp.concatenate([gg_mix_pre, dshift_m.reshape(n_seq, d), dscale_m.reshape(n_seq, d),
                            jnp.zeros((8 - 1 - 2 * n_seq, d), F32)], axis=0)
    late_g = _all_gather(late, "ag_late")
    loss = jnp.sum(early_g[:, 4, 0]) * (0.5 / d)
    dmod_all = jnp.concatenate(
        [late_g[:, 1:1 + n_seq, None, :], late_g[:, 1 + n_seq:1 + 2 * n_seq, None, :],
         early_g[:, 8 + n_gw:, :].reshape(N_DEV, n_seq, 4, d)], axis=2).reshape(N_DEV * n_seq, N_MOD * d)
    dmod_cols = lax.dynamic_slice_in_dim(dmod_all, me * cond_cols, cond_cols, axis=1)
    o_cond, o_bcond = _cond_bwd_adamw(c_all, dmod_all, dmod_cols, w_cond[0], m_w_cond[0], v_w_cond[0],
                                      b_cond, m_b_cond, v_b_cond)
    o_cond = tuple(o[None] for o in o_cond)

    small_ws = [g_mix_pre, g_mix_post, g_ffn_pre, g_ffn_post, pool_scale, w_pool.reshape(-1, POOL_GROUP_DIM)]
    small_ms = [m_g_mix_pre, m_g_mix_post, m_g_ffn_pre, m_g_ffn_post, m_pool_scale, m_w_pool.reshape(-1, POOL_GROUP_DIM)]
    small_vs = [v_g_mix_pre, v_g_mix_post, v_g_ffn_pre, v_g_ffn_post, v_pool_scale, v_w_pool.reshape(-1, POOL_GROUP_DIM)]
    small_gparts = [late_g[:, 0:1, :], early_g[:, 0:1, :], early_g[:, 1:2, :], early_g[:, 2:3, :],
                    early_g[:, 3:4, :pool_scale.shape[1]],
                    early_g[:, 8:8 + n_gw, :].reshape(N_DEV, -1, POOL_GROUP_DIM)]
    so = _adamw_small(small_ws, small_gparts, small_ms, small_vs, "adamw_small")
    ns = len(small_ws)
    sg, sdl, sm, sv = so[:ns], so[ns:2 * ns], so[2 * ns:3 * ns], so[3 * ns:]
    pool_shape = w_pool.shape
    fix = lambda lst: [lst[0], lst[1], lst[2], lst[3], lst[4], lst[5].reshape(pool_shape)]
    sg, sdl, sm, sv = fix(sg), fix(sdl), fix(sm), fix(sv)


    def reduced(mine, recv, slab, w, m, v, name, transposed=False, transpose=False):
        turn = (lambda u: u.T) if transposed else (lambda u: u)
        outs = _rs_final_adamw(mine, recv, slab, turn(w[0]), turn(m[0]), turn(v[0]), name, transpose)
        return tuple(turn(o)[None] for o in outs)

    o_in = reduced(gwin.reshape(1, N_DEV, in_rows, d), rv_win, 0, w_in, m_w_in, v_w_in, "adamw_w_in",
                   transpose=True)
    o_out = reduced(gwout.reshape(1, N_DEV, out_rows, d), rv_wout, 0, w_out, m_w_out, v_w_out, "adamw_w_out")
    gwgu8 = gwgu.reshape(2, N_DEV, ff_rows, d)
    o_gate = reduced(gwgu8, rv_wgu, 0, w_gate, m_w_gate, v_w_gate, "adamw_w_gate", transposed=True)
    o_up = reduced(gwgu8, rv_wgu, 1, w_up, m_w_up, v_w_up, "adamw_w_up", transposed=True)
    o_down = reduced(gwd.reshape(1, N_DEV, ff_rows, d), rv_wd, 0, w_down, m_w_down, v_w_down, "adamw_w_down")

    def pick(k):
        small_k = [sg, sdl, sm, sv][k]
        return [o_cond[k], o_bcond[k], small_k[0], small_k[1], o_in[k], small_k[5], small_k[4], o_out[k],
                small_k[2], small_k[3], o_gate[k], o_up[k], o_down[k]]

    return (loss, grad_x.reshape(n_seq, seq, d), *pick(0), *pick(1), *pick(2), *pick(3))
```

```python
import functools
import math

import jax
import jax.numpy as jnp
from jax import lax
from jax.experimental import pallas as pl
from jax.experimental.pallas import tpu as pltpu

F32 = jnp.float32
BF16 = jnp.bfloat16
MESH = pl.DeviceIdType.MESH

N_DEV = 8
HEAD_DIM = 64
LANES = 128
POOL_WINDOWS = (2, 4, 8, 16)
POOL_GROUP_DIM = 128
N_MOD = 6
EPS = 1e-6
ATT_TILE = 256
ATT_PAIRS = 2
VMEM_LIMIT = 56 * 1024 * 1024
ADAMW_COL_TILE = 256

ADAM_LR = 0.001
ADAM_B1 = 0.9
ADAM_B2 = 0.999
ADAM_EPS = 1e-08
ADAM_WD = 0.01
ADAM_STEP = 10


def _params(**kw):
    return pltpu.CompilerParams(vmem_limit_bytes=VMEM_LIMIT, **kw)


def _dot_nn(a, b):
    return jnp.dot(a, b, preferred_element_type=F32)


def _dot_nt(a, b):
    return lax.dot_general(a, b, (((1,), (1,)), ((), ())), preferred_element_type=F32)


def _dot_tn(a, b):
    return lax.dot_general(a, b, (((0,), (0,)), ((), ())), preferred_element_type=F32)


def _mesh_pos():
    return lax.axis_index("x"), lax.axis_index("y"), lax.axis_index("c")


def _ag_phases(dests, src, outs, send_sems, recv_sems, local_sems):
    n = len(src)
    x, y, c = _mesh_pos()
    me, sibling = (x, y, c), (x, y, 1 - c)
    chips = [(1 - x, y), (x, 1 - y), (1 - x, 1 - y)]

    def slot(i, dev):
        oi, prefix = dests[i]
        px, py, pc = dev
        return outs[oi].at[prefix + (4 * px + 2 * py + pc,)]

    def copy(i, k, block, to, from_src=False):
        return pltpu.make_async_remote_copy(
            src_ref=src[i] if from_src else slot(i, block), dst_ref=slot(i, block),
            send_sem=send_sems.at[i, k], recv_sem=recv_sems.at[i, k],
            device_id=to, device_id_type=MESH)

    def mine(i):
        return pltpu.make_async_copy(src[i], slot(i, me), local_sems.at[i])

    def first(i):
        return [copy(i, 0, me, sibling, from_src=True)] + [
            copy(i, 1 + j, me, (*chip, c), from_src=True) for j, chip in enumerate(chips)]

    def passed(i, j):
        return copy(i, 4 + j, (*chips[j], c), sibling)

    def start():
        for i in range(n):
            mine(i).start()
        for i in range(n):
            for cp in first(i):
                cp.start()

    def forward():
        for j, chip in enumerate(chips):
            for i in range(n):
                copy(i, 1 + j, (*chip, c), me).wait_recv()
                passed(i, j).start()

    def finish():
        for i in range(n):
            copy(i, 0, sibling, me).wait_recv()
            for j, chip in enumerate(chips):
                copy(i, 4 + j, (*chip, 1 - c), me).wait_recv()
        for i in range(n):
            for cp in first(i) + [passed(i, j) for j in range(3)]:
                cp.wait_send()
            mine(i).wait()

    return start, forward, finish


def _ag_scratch(n):
    return [pltpu.SemaphoreType.DMA((n, 7)), pltpu.SemaphoreType.DMA((n, 7)), pltpu.SemaphoreType.DMA((n,))]


def _gather_short(src_ref, out_ref, send_sems, recv_sems, local_sems, i=0):
    x, y, c = _mesh_pos()

    def copy(k, block):
        px, py, pc = x ^ (k >> 2), y ^ ((k >> 1) & 1), c ^ (k & 1)
        bx, by, bc = (x, y, c) if block == "mine" else (px, py, pc)
        return pltpu.make_async_remote_copy(
            src_ref=src_ref, dst_ref=out_ref.at[4 * bx + 2 * by + bc],
            send_sem=send_sems.at[i, k - 1], recv_sem=recv_sems.at[i, k - 1],
            device_id=(px, py, pc), device_id_type=MESH)

    local = pltpu.make_async_copy(src_ref, out_ref.at[4 * x + 2 * y + c], local_sems.at[i])
    local.start()
    for k in range(1, N_DEV):
        copy(k, "mine").start()
    for k in range(1, N_DEV):
        copy(k, "mine").wait_send()
    for k in range(1, N_DEV):
        copy(k, "theirs").wait_recv()
    local.wait()


def _gather_short_scratch(n):
    return [pltpu.SemaphoreType.DMA((n, N_DEV - 1)), pltpu.SemaphoreType.DMA((n, N_DEV - 1)),
            pltpu.SemaphoreType.DMA((n,))]


def _all_gather(src, name):
    def body(src_ref, out_ref, *sems):
        _gather_short(src_ref, out_ref, *sems)

    any_spec = pl.BlockSpec(memory_space=pl.ANY)
    return pl.pallas_call(
        body, name=name,
        out_shape=jax.ShapeDtypeStruct((N_DEV,) + src.shape, src.dtype),
        in_specs=[any_spec], out_specs=any_spec,
        scratch_shapes=_gather_short_scratch(1),
    )(src)


def _rs_phases(shapes, src, dst, send_sems, recv_sems):
    x, y, c = _mesh_pos()

    def copies():
        out = []
        n = 0
        for i, shp in enumerate(shapes):
            for m in range(shp[0]):
                for k in range(1, N_DEV):
                    px, py, pc = x ^ (k >> 2), y ^ ((k >> 1) & 1), c ^ (k & 1)
                    out.append(pltpu.make_async_remote_copy(
                        src_ref=src[i].at[m, 4 * px + 2 * py + pc], dst_ref=dst[i].at[m, k - 1],
                        send_sem=send_sems.at[n], recv_sem=recv_sems.at[n],
                        device_id=(px, py, pc), device_id_type=MESH))
                    n += 1
        return out

    def start():
        for cp in copies():
            cp.start()

    def finish():
        for cp in copies():
            cp.wait_send()
        for cp in copies():
            cp.wait_recv()

    return start, finish


def _rs_out(sends):
    return [jax.ShapeDtypeStruct((s.shape[0], N_DEV - 1) + s.shape[2:], s.dtype) for s in sends]


def _rs_scratch(sends):
    total = sum((N_DEV - 1) * s.shape[0] for s in sends)
    return [pltpu.SemaphoreType.DMA((total,)), pltpu.SemaphoreType.DMA((total,))]


def _rs_final_adamw(mine, recv, slab, w, m, v, name, transpose=False):
    _, _, r, cdim = mine.shape
    tc = ADAMW_COL_TILE
    assert cdim % tc == 0 and w.shape == ((cdim, r) if transpose else (r, cdim)), (name, w.shape)
    x, y, c = _mesh_pos()
    me = jnp.reshape(4 * x + 2 * y + c, (1,)).astype(jnp.int32)

    def body(me_ref, p_ref, r_ref, w_ref, m_ref, v_ref, g_ref, d_ref, nm_ref, nv_ref):
        del me_ref
        g = p_ref[...]
        for k in range(N_DEV - 1):
            g = g + r_ref[k].astype(F32)
        if transpose:
            g = g.T
        g_ref[...] = g
        d_ref[...], nm_ref[...], nv_ref[...] = _adamw_math(w_ref[...], g, m_ref[...], v_ref[...])

    if transpose:
        w_spec = pl.BlockSpec((tc, r), lambda j, s: (j, 0))
    else:
        w_spec = pl.BlockSpec((r, tc), lambda j, s: (0, j))
    sds = jax.ShapeDtypeStruct(w.shape, F32)
    return pl.pallas_call(
        body, name=name, out_shape=(sds, sds, sds, sds),
        grid_spec=pltpu.PrefetchScalarGridSpec(
            num_scalar_prefetch=1, grid=(cdim // tc,),
            in_specs=[pl.BlockSpec((None, None, r, tc), lambda j, s: (slab, s[0], 0, j)),
                      pl.BlockSpec((None, N_DEV - 1, r, tc), lambda j, s: (slab, 0, 0, j)),
                      w_spec, w_spec, w_spec],
            out_specs=(w_spec, w_spec, w_spec, w_spec)),
        compiler_params=_params(),
    )(me, mine, recv, w, m, v)


def _matmul(a, b, mode, out_dtype, tm, tn, tk, name, bf16_copy=False, rs_sends=(), ag=None):
    ga = a.shape[0] if a.ndim == 3 else None
    gb = b.shape[0] if b.ndim == 3 else None
    a2, b2 = a.shape[-2:], b.shape[-2:]
    if mode == "nn":
        (m, k), n = a2, b2[1]
    elif mode == "nt":
        (m, k), n = a2, b2[0]
    else:
        (k, m), n = a2, b2[1]
    assert m % tm == 0 and n % tn == 0 and k % tk == 0, (name, m, n, k)
    nk = k // tk
    g_n = ga or 1
    batch_out = mode == "tn" and ga is not None
    n_red = nk if batch_out else nk * g_n
    dot = {"nn": _dot_nn, "nt": _dot_nt, "tn": _dot_tn}[mode]
    acc_in_out = out_dtype == F32

    n_rs = len(rs_sends)
    rs_shapes = [r.shape for r in rs_sends]
    ag_srcs, ag_out_shapes, ag_dests = ag if ag is not None else ((), (), ())
    n_ag, n_ag_out = len(ag_srcs), len(ag_out_shapes)
    n_out = 2 if bf16_copy else 1
    assert not bf16_copy or acc_in_out
    assert not (n_rs and n_ag)

    def body(a_ref, b_ref, *rest):
        rs_src, rest = rest[:n_rs], rest[n_rs:]
        ag_src, rest = rest[:n_ag], rest[n_ag:]
        o_ref = rest[0]
        copy_ref = rest[1] if bf16_copy else None
        rs_dst, rest = rest[n_out:n_out + n_rs], rest[n_out + n_rs:]
        ag_out, scratch = rest[:n_ag_out], rest[n_ag_out:]
        first = functools.reduce(jnp.logical_and, [pl.program_id(ax) == 0 for ax in range(4)])
        last = functools.reduce(jnp.logical_and, [pl.program_id(ax) == grid[ax] - 1 for ax in range(4)])
        if n_rs:
            rs_start, rs_finish = _rs_phases(rs_shapes, rs_src, rs_dst, *scratch[-2:])
            pl.when(first)(rs_start)
        if n_ag:
            ag_start, ag_forward, ag_finish = _ag_phases(ag_dests, ag_src, ag_out, *scratch[-3:])
            pl.when(first)(ag_start)
        p = dot(a_ref[...], b_ref[...])
        kk = pl.program_id(3) if batch_out else pl.program_id(2) * nk + pl.program_id(3)
        if n_red == 1:
            o_ref[...] = p.astype(out_dtype)
            if bf16_copy:
                copy_ref[...] = p.astype(BF16)
        else:
            acc = o_ref if acc_in_out else scratch[0]

            @pl.when(kk == 0)
            def _():
                acc[...] = p

            @pl.when(kk > 0)
            def _():
                acc[...] += p

            @pl.when(kk == n_red - 1)
            def _():
                if not acc_in_out:
                    o_ref[...] = acc[...].astype(out_dtype)
                if bf16_copy:
                    copy_ref[...] = acc[...].astype(BF16)

        if n_rs:
            pl.when(last)(rs_finish)
        if n_ag:
            @pl.when(last)
            def _():
                ag_forward()
                ag_finish()

    def order(ids):
        return ids if batch_out else (ids[2], ids[0], ids[1], ids[3])

    def a_idx(*ids):
        g, i, j, kq = order(ids)
        blk = {"nn": (i, kq), "nt": (i, kq), "tn": (kq, i)}[mode]
        return (g,) + blk if ga is not None else blk

    def b_idx(*ids):
        g, i, j, kq = order(ids)
        blk = {"nn": (kq, j), "nt": (j, kq), "tn": (kq, j)}[mode]
        return (g,) + blk if gb is not None else blk

    def o_idx(*ids):
        g, i, j, kq = order(ids)
        return (g, i, j) if batch_out else (i, j)

    a_blk = {"nn": (tm, tk), "nt": (tm, tk), "tn": (tk, tm)}[mode]
    b_blk = {"nn": (tk, tn), "nt": (tn, tk), "tn": (tk, tn)}[mode]
    if ga is not None:
        a_blk = (None,) + a_blk
    if gb is not None:
        b_blk = (None,) + b_blk
    if batch_out:
        out_shape = jax.ShapeDtypeStruct((g_n, m, n), out_dtype)
        o_blk = (None, tm, tn)
        grid = (g_n, m // tm, n // tn, nk)
    else:
        out_shape = jax.ShapeDtypeStruct((m, n), out_dtype)
        o_blk = (tm, tn)
        grid = (m // tm, n // tn, g_n, nk)
    scratch = [] if (acc_in_out or n_red == 1) else [pltpu.VMEM((tm, tn), F32)]
    any_spec = pl.BlockSpec(memory_space=pl.ANY)
    out_shapes = [out_shape] + ([jax.ShapeDtypeStruct(out_shape.shape, BF16)] if bf16_copy else [])
    res = pl.pallas_call(
        body, name=name, out_shape=tuple(out_shapes + _rs_out(rs_sends) + list(ag_out_shapes)), grid=grid,
        in_specs=[pl.BlockSpec(a_blk, a_idx), pl.BlockSpec(b_blk, b_idx)] + [any_spec] * (n_rs + n_ag),
        out_specs=tuple([pl.BlockSpec(o_blk, o_idx)] * n_out + [any_spec] * (n_rs + n_ag_out)),
        scratch_shapes=scratch + (_rs_scratch(rs_sends) if n_rs else []) + (_ag_scratch(n_ag) if n_ag else []),
        compiler_params=_params(),
    )(a, b, *rs_sends, *ag_srcs)
    return res if len(res) > 1 else res[0]


EW_TILE = 256
ROW_TILE = 512
EPILOGUE_CHUNKS = 8
MXU_WIDTH = 256


def _rms(v):
    return lax.rsqrt(jnp.mean(v * v, axis=-1, keepdims=True) + EPS)


def _rms_bwd(dhat, vh, r):
    return r * (dhat - vh * jnp.mean(dhat * vh, axis=-1, keepdims=True))


def _tok_spec(tm, d):
    return pl.BlockSpec((tm, d), lambda i: (i, 0))


def _vec_spec(d):
    return pl.BlockSpec((1, d), lambda i: (0, 0))


def _mod_spec(tiles_per_seq, d):
    return pl.BlockSpec((None, N_MOD, d), lambda i: (i // tiles_per_seq, 0, 0))


def _seq_acc_spec(tiles_per_seq, d):
    return pl.BlockSpec((None, 1, d), lambda i: (i // tiles_per_seq, 0, 0))


def _acc(ref, val, first):
    if first is False:
        ref[...] += val
        return

    @pl.when(first)
    def _():
        ref[...] = val

    @pl.when(jnp.logical_not(first))
    def _():
        ref[...] += val


def _colsum(v):
    return jnp.sum(v, axis=0, keepdims=True)


def _pre_mix(x2, g_pre, mod, seq, ag_srcs, ag_out_shapes, ag_dests):
    t, d = x2.shape
    tm = EW_TILE
    n_steps = t // tm
    n_ag, n_ag_out = len(ag_srcs), len(ag_out_shapes)

    def body(x_ref, g_ref, mod_ref, *rest):
        ag_src, h_ref = rest[:n_ag], rest[n_ag]
        ag_out, sems = rest[n_ag + 1:n_ag + 1 + n_ag_out], rest[n_ag + 1 + n_ag_out:]
        ag_start, ag_forward, ag_finish = _ag_phases(ag_dests, ag_src, ag_out, *sems)
        step = pl.program_id(0)
        pl.when(step == 0)(ag_start)
        xv = x_ref[...]
        n = xv * _rms(xv) * g_ref[...]
        h_ref[...] = (n * (1.0 + mod_ref[1:2, :]) + mod_ref[0:1, :]).astype(BF16)

        @pl.when(step == n_steps - 1)
        def _():
            ag_forward()
            ag_finish()

    any_spec = pl.BlockSpec(memory_space=pl.ANY)
    return pl.pallas_call(
        body, name="pre_mix", out_shape=(jax.ShapeDtypeStruct((t, d), BF16), *ag_out_shapes), grid=(n_steps,),
        in_specs=[_tok_spec(tm, d), _vec_spec(d), _mod_spec(seq // tm, d)] + [any_spec] * n_ag,
        out_specs=(_tok_spec(tm, d), *([any_spec] * n_ag_out)),
        scratch_shapes=_ag_scratch(n_ag), compiler_params=_params(),
    )(x2, g_pre, mod, *ag_srcs)


def _matmul_rows(a, b, tm, seq, name, epilogue, ep_in, ep_in_kinds, ep_out, ep_out_kinds, rs_sends=()):
    g_n = a.shape[0] if a.ndim == 3 else None
    (m, k), n = a.shape[-2:], b.shape[-1]
    tps = seq // tm
    n_i = m // tm
    n_rs = len(rs_sends)
    rs_shapes = [r.shape for r in rs_sends]
    n_in, n_out = len(ep_in), len(ep_out)
    n_cols = n // MXU_WIDTH
    rc, cw = tm // EPILOGUE_CHUNKS, n // n_cols

    def prev(i):
        return jnp.maximum(i - 1, 0)

    def spec(kind):
        return {"tok": pl.BlockSpec((tm, n), lambda i: (prev(i), 0)),
                "vec": pl.BlockSpec((1, n), lambda i: (0, 0)),
                "mod": pl.BlockSpec((None, N_MOD, n), lambda i: (prev(i) // tps, 0, 0)),
                "seq": pl.BlockSpec((None, 1, n), lambda i: (prev(i) // tps, 0, 0)),
                "loss": pl.BlockSpec((1, LANES), lambda i: (0, 0))}[kind]

    def body(a_ref, b_ref, *rest):
        in_refs, rest = rest[:n_in], rest[n_in:]
        rs_src, rest = rest[:n_rs], rest[n_rs:]
        out_refs, rest = rest[:n_out], rest[n_out:]
        rs_dst, rest = rest[:n_rs], rest[n_rs:]
        fin = rest[0]
        i = pl.program_id(0)
        if n_rs:
            rs_start, rs_finish = _rs_phases(rs_shapes, rs_src, rs_dst, *rest[1:])
            pl.when(i == 0)(rs_start)

        def product(cols):
            if g_n is None:
                return _dot_nn(a_ref[...], b_ref[:, cols])
            p = _dot_nn(a_ref[0], b_ref[0, :, cols])
            for g in range(1, g_n):
                p = p + _dot_nn(a_ref[g], b_ref[g, :, cols])
            return p

        def step(with_epilogue, with_matmul):
            parts = []
            for c in range(EPILOGUE_CHUNKS):
                if with_epilogue:
                    rows = pl.ds(c * rc, rc)
                    epilogue(fin[rows, :], i - 1, tps, in_refs, out_refs, rows, c)
                while with_matmul and len(parts) < (c + 1) * n_cols // EPILOGUE_CHUNKS:
                    cols = slice(len(parts) * cw, (len(parts) + 1) * cw)
                    parts.append((cols, product(cols)))
            for cols, v in parts:
                fin[:, cols] = v

        pl.when(i == 0)(functools.partial(step, False, True))
        pl.when(jnp.logical_and(i > 0, i < n_i))(functools.partial(step, True, True))
        pl.when(i == n_i)(functools.partial(step, True, False))

        if n_rs:
            pl.when(i == n_i)(rs_finish)

    def row(i):
        return jnp.minimum(i, n_i - 1)

    if g_n is None:
        a_spec = pl.BlockSpec((tm, k), lambda i: (row(i), 0))
        b_spec = pl.BlockSpec(b.shape, lambda i: (0, 0), pipeline_mode=pl.Buffered(1))
    else:
        a_spec = pl.BlockSpec((g_n, tm, k), lambda i: (0, row(i), 0))
        b_spec = pl.BlockSpec(b.shape, lambda i: (0, 0, 0), pipeline_mode=pl.Buffered(1))
    any_spec = pl.BlockSpec(memory_space=pl.ANY)
    res = pl.pallas_call(
        body, name=name, grid=(n_i + 1,), out_shape=tuple(list(ep_out) + _rs_out(rs_sends)),
        in_specs=[a_spec, b_spec] + [spec(kd) for kd in ep_in_kinds] + [any_spec] * n_rs,
        out_specs=tuple([spec(kd) for kd in ep_out_kinds] + [any_spec] * n_rs),
        scratch_shapes=[pltpu.VMEM((tm, n), F32)] + (_rs_scratch(rs_sends) if n_rs else []),
        compiler_params=_params(),
    )(a, b, *ep_in, *rs_sends)
    return res


def _first(cond, chunk):
    return cond if chunk == 0 else False


def _mid_epilogue(mv, i, tps, in_refs, out_refs, rows, chunk):
    x_ref, gpost_ref, gpre_ref, mod_ref = in_refs
    mix_ref, x1_ref, h2_ref = out_refs
    mix_ref[rows, :] = mv
    x1 = x_ref[rows, :] + mod_ref[2:3, :] * (mv * _rms(mv) * gpost_ref[...])
    x1_ref[rows, :] = x1
    n = x1 * _rms(x1) * gpre_ref[...]
    h2_ref[rows, :] = (n * (1.0 + mod_ref[4:5, :]) + mod_ref[3:4, :]).astype(BF16)


def _post_epilogue(fv, i, tps, in_refs, out_refs, rows, chunk):
    x1_ref, tgt_ref, g_ref, mod_ref = in_refs
    loss_ref, dy_ref, df_ref, dgate_ref, gg_ref = out_refs
    d = fv.shape[1]
    r = _rms(fv)
    fh = fv * r
    nf = fh * g_ref[...]
    gate = mod_ref[5:6, :]
    err = x1_ref[rows, :] + gate * nf - tgt_ref[rows, :]
    _acc(loss_ref, jnp.sum(_colsum(err * err), axis=1, keepdims=True) * jnp.ones((1, LANES), F32),
         _first(i == 0, chunk))
    dy = err * (1.0 / d)
    dy_ref[rows, :] = dy
    _acc(dgate_ref, _colsum(dy * nf), _first(i % tps == 0, chunk))
    dn = dy * gate
    _acc(gg_ref, _colsum(dn * fh), _first(i == 0, chunk))
    df_ref[rows, :] = _rms_bwd(dn * g_ref[...], fh, r).astype(BF16)


def _bwd_mid_epilogue(dh, i, tps, in_refs, out_refs, rows, chunk):
    dy_ref, x1_ref, mix_ref, gpre_ref, gpost_ref, mod_ref = in_refs
    dx1_ref, dmix_ref, dshift_ref, dscale_ref, dgate_ref, ggpre_ref, ggpost_ref = out_refs
    seq_first, first = _first(i % tps == 0, chunk), _first(i == 0, chunk)
    x1 = x1_ref[rows, :]
    r = _rms(x1)
    xh = x1 * r
    gpre = gpre_ref[...]
    _acc(dshift_ref, _colsum(dh), seq_first)
    _acc(dscale_ref, _colsum(dh * xh * gpre), seq_first)
    dn = dh * (1.0 + mod_ref[4:5, :])
    _acc(ggpre_ref, _colsum(dn * xh), first)
    dx1 = dy_ref[rows, :] + _rms_bwd(dn * gpre, xh, r)
    dx1_ref[rows, :] = dx1
    mv = mix_ref[rows, :]
    rm = _rms(mv)
    mh = mv * rm
    gpost = gpost_ref[...]
    _acc(dgate_ref, _colsum(dx1 * mh * gpost), seq_first)
    dnm = dx1 * mod_ref[2:3, :]
    _acc(ggpost_ref, _colsum(dnm * mh), first)
    dmix_ref[rows, :] = _rms_bwd(dnm * gpost, mh, rm).astype(BF16)


def _bwd_pre_epilogue(dh, i, tps, in_refs, out_refs, rows, chunk):
    dx1_ref, x_ref, g_ref, mod_ref = in_refs
    gx_ref, dshift_ref, dscale_ref, gg_ref = out_refs
    seq_first = _first(i % tps == 0, chunk)
    xv = x_ref[rows, :]
    r = _rms(xv)
    xh = xv * r
    g = g_ref[...]
    _acc(dshift_ref, _colsum(dh), seq_first)
    _acc(dscale_ref, _colsum(dh * xh * g), seq_first)
    dn = dh * (1.0 + mod_ref[1:2, :])
    _acc(gg_ref, _colsum(dn * xh), _first(i == 0, chunk))
    gx_ref[rows, :] = dx1_ref[rows, :] + _rms_bwd(dn * g, xh, r)


def _ffn_up(h2, wgu, tm, tn):
    t, d = h2.shape
    f = wgu.shape[1]

    def body(h_ref, w_ref, gu_ref, act_ref):
        h = h_ref[...]
        g = _dot_nt(h, w_ref[0])
        u = _dot_nt(h, w_ref[1])
        gu_ref[0] = g.astype(BF16)
        gu_ref[1] = u.astype(BF16)
        act_ref[...] = (g * jax.nn.sigmoid(g) * u).astype(BF16)

    return pl.pallas_call(
        body, name="ffn_up", grid=(f // tn, t // tm),
        out_shape=(jax.ShapeDtypeStruct((2, t, f), BF16), jax.ShapeDtypeStruct((t, f), BF16)),
        in_specs=[pl.BlockSpec((tm, d), lambda j, i: (i, 0)), pl.BlockSpec((2, tn, d), lambda j, i: (0, j, 0))],
        out_specs=(pl.BlockSpec((2, tm, tn), lambda j, i: (0, i, j)), pl.BlockSpec((tm, tn), lambda j, i: (i, j))),
        compiler_params=_params(),
    )(h2, wgu)


def _ffn_act_bwd(df, wd, gu, tm, tn):
    t, d = df.shape
    f = wd.shape[0]

    def body(df_ref, w_ref, gu_ref, dgu_ref):
        da = _dot_nt(df_ref[...], w_ref[...])
        g = gu_ref[0].astype(F32)
        u = gu_ref[1].astype(F32)
        s = jax.nn.sigmoid(g)
        silu = g * s
        dgu_ref[0] = (da * u * (s + silu * (1.0 - s))).astype(BF16)
        dgu_ref[1] = (da * silu).astype(BF16)

    return pl.pallas_call(
        body, name="ffn_act_bwd", grid=(f // tn, t // tm),
        out_shape=jax.ShapeDtypeStruct((2, t, f), BF16),
        in_specs=[pl.BlockSpec((tm, d), lambda j, i: (i, 0)), pl.BlockSpec((tn, d), lambda j, i: (j, 0)),
                  pl.BlockSpec((2, tm, tn), lambda j, i: (0, i, j))],
        out_specs=pl.BlockSpec((2, tm, tn), lambda j, i: (0, i, j)),
        compiler_params=_params(),
    )(df, wd, gu)


SIGN_BIT = 0x80000000
Q_SCALE = 1.0 / math.sqrt(HEAD_DIM)


def _softplus(z):
    neg_abs = lax.bitcast_convert_type(lax.bitcast_convert_type(z, jnp.uint32) | jnp.uint32(SIGN_BIT), F32)
    return jnp.maximum(z, 0.0) + jnp.log(1.0 + jnp.exp(neg_abs))


def _hi_lo(v):
    hi = v.astype(BF16)
    return jnp.concatenate([hi, (v - hi.astype(F32)).astype(BF16)], axis=1)


def _emit_skewed(chains, lag=1):
    for t in range(max(len(ch) for ch in chains) + lag * (len(chains) - 1)):
        for c, ch in enumerate(chains):
            if 0 <= t - lag * c < len(ch):
                ch[t - lag * c]()


def _fwd_chain(blk, qs, k_ref, v_ref, c0, kb, cols, mask, ntri, lane, tq):
    st = {}

    def scores():
        st["z"] = _dot_nt(qs, k_ref[pl.ds(c0, tq), cols])

    def soft():
        sp = _softplus(st["z"])
        if mask is not None:
            sp = jnp.where(mask, sp, 0.0)
        st["parts"] = _hi_lo(sp)
        st["cur"] = blk["cur"]
        blk["cm"] = jnp.where(lane == kb, blk["cur"], blk["cm"])
        blk["cur"] = blk["cur"] - jnp.sum(sp, axis=1, keepdims=True)

    def sums():
        st["s"] = _dot_nn(st["parts"], ntri)

    def weights():
        w = jnp.exp(st["z"] + st["s"] + st["cur"])
        if mask is not None:
            w = jnp.where(mask, w, 0.0)
        st["w"] = w.astype(BF16)

    def out():
        p = _dot_nn(st["w"], v_ref[pl.ds(c0, tq), cols])
        blk["pv"] = p if blk["pv"] is None else blk["pv"] + p

    return [scores, soft, sums, weights, out]


def _bwd_chain(blk, qs, dos, cs, k_ref, v_ref, dk_ref, dv_ref, c0, kb, cols, mask, ntri, tri_i, lane, tq):
    st = {}

    def scores():
        st["z"] = _dot_nt(qs, k_ref[pl.ds(c0, tq), cols])
        st["dw"] = _dot_nt(dos, v_ref[pl.ds(c0, tq), cols])

    def soft():
        sp = _softplus(st["z"])
        if mask is not None:
            sp = jnp.where(mask, sp, 0.0)
        st["sp"] = sp
        st["parts"] = _hi_lo(sp)
        st["cur"] = jnp.sum(jnp.where(lane == kb, cs, 0.0), axis=1, keepdims=True)

    def sums():
        st["s"] = _dot_nn(st["parts"], ntri)

    def weights():
        w = jnp.exp(st["z"] + st["s"] + st["cur"])
        if mask is not None:
            w = jnp.where(mask, w, 0.0)
        ee = w * st["dw"]
        st["w"], st["ee"], st["ec"] = w.astype(BF16), ee, blk["ec"]
        blk["ec"] = blk["ec"] + jnp.sum(ee, axis=1, keepdims=True)

    def prefix():
        st["einc"] = _dot_nn(st["ee"].astype(BF16), tri_i)

    def dz():
        v = st["ee"] - jnp.exp(st["z"] - st["sp"]) * (st["einc"] + st["ec"])
        if mask is not None:
            v = jnp.where(mask, v, 0.0)
        st["dz"] = v.astype(BF16)

    def grads():
        p = _dot_nn(st["dz"], k_ref[pl.ds(c0, tq), cols])
        blk["dq"] = p if blk["dq"] is None else blk["dq"] + p
        dk_ref[pl.ds(c0, tq), :] += _dot_tn(st["dz"], qs)
        dv_ref[pl.ds(c0, tq), :] += _dot_tn(st["w"], dos)

    return [scores, soft, sums, weights, prefix, dz, grads]


def _stack_heads(v, lane, scale=None):
    if scale is not None:
        v = v * jnp.asarray(scale, v.dtype)
    zero = jnp.zeros_like(v)
    return jnp.concatenate([jnp.where(lane < HEAD_DIM, v, zero), jnp.where(lane >= HEAD_DIM, v, zero)], axis=0)


def _diag_mask(tq):
    row = lax.broadcasted_iota(jnp.int32, (2 * tq, tq), 0)
    col = lax.broadcasted_iota(jnp.int32, (2 * tq, tq), 1)
    return col < jnp.where(row >= tq, row - tq, row)


def _attn_fwd(proj, tri_after, n_seq, seq, ag_srcs, ag_out_shapes, ag_dests):
    t = proj.shape[0]
    tq = ATT_TILE
    npp = ATT_PAIRS
    n_blk = (proj.shape[1] // 4) // (npp * LANES)
    n_ag, n_ag_out = len(ag_srcs), len(ag_out_shapes)
    n_steps = n_seq * n_blk

    def body(q_ref, k_ref, v_ref, tri_ref, *rest):
        ag_src, rest = rest[:n_ag], rest[n_ag:]
        o_ref, cs_ref = rest[:2]
        ag_out, rest = rest[2:2 + n_ag_out], rest[2 + n_ag_out:]
        oacc, cmat, carry = rest[:3]
        ag_start, ag_forward, ag_finish = _ag_phases(ag_dests, ag_src, ag_out, *rest[3:])
        step = pl.program_id(0) * n_blk + pl.program_id(1)
        pl.when(step == 0)(ag_start)
        pl.when(step == (3 * n_steps) // 4)(ag_forward)
        lane = lax.broadcasted_iota(jnp.int32, (1, LANES), 1)
        ntri = tri_ref[...]
        diag = _diag_mask(tq)

        def q_tile(qi, _):
            r0 = pl.multiple_of(qi * tq, tq)
            qs = [_stack_heads(q_ref[pl.ds(r0, tq), pp * LANES:(pp + 1) * LANES], lane, Q_SCALE)
                  for pp in range(npp)]
            carry[...] = jnp.zeros_like(carry)
            cmat[...] = jnp.zeros_like(cmat)
            oacc[...] = jnp.zeros_like(oacc)

            def run_tiles(tiles):
                blocks = [dict(cur=carry[pp], cm=cmat[pp], pv=None) for pp in range(npp)]
                chains = []
                for kb, mask in tiles:
                    c0 = pl.multiple_of(kb * tq, tq)
                    for pp in range(npp):
                        chains.append(_fwd_chain(blocks[pp], qs[pp], k_ref, v_ref, c0, kb,
                                                 slice(pp * LANES, (pp + 1) * LANES), mask, ntri, lane, tq))
                _emit_skewed(chains)
                for pp in range(npp):
                    oacc[pp] += blocks[pp]["pv"]
                    cmat[pp] = blocks[pp]["cm"]
                    carry[pp] = blocks[pp]["cur"]

            odd = qi % 2

            @pl.when(odd == 0)
            def _():
                run_tiles([(qi, diag)])

            @pl.when(odd == 1)
            def _():
                run_tiles([(qi, diag), (qi - 1, None)])

            def pair(j, _):
                kb = qi - 1 - odd - 2 * j
                run_tiles([(kb, None), (kb - 1, None)])
                return 0

            lax.fori_loop(0, qi // 2, pair, 0)
            for pp in range(npp):
                c_off = 2 * pp * LANES
                cs_ref[pl.ds(r0, tq), c_off:c_off + LANES] = cmat[pp, 0:tq, :]
                cs_ref[pl.ds(r0, tq), c_off + LANES:c_off + 2 * LANES] = cmat[pp, tq:2 * tq, :]
                o_ref[pl.ds(r0, tq), pp * LANES:(pp + 1) * LANES] = jnp.where(
                    lane < HEAD_DIM, oacc[pp, 0:tq, :], oacc[pp, tq:2 * tq, :]).astype(BF16)
            return 0

        lax.fori_loop(0, seq // tq, q_tile, 0)
        pl.when(step == n_steps - 1)(ag_finish)

    wid = npp * LANES
    blk = lambda off: pl.BlockSpec((seq, wid), lambda b, p: (b, off + p))
    any_spec = pl.BlockSpec(memory_space=pl.ANY)
    return pl.pallas_call(
        body, name="attn_fwd", grid=(n_seq, n_blk),
        out_shape=(jax.ShapeDtypeStruct((2, t, n_blk * wid), BF16),
                   jax.ShapeDtypeStruct((t, n_blk * 2 * wid), F32), *ag_out_shapes),
        in_specs=[blk(0), blk(n_blk), blk(2 * n_blk), pl.BlockSpec((2 * tq, tq), lambda b, p: (0, 0))]
        + [any_spec] * n_ag,
        out_specs=(pl.BlockSpec((None, seq, wid), lambda b, p: (0, b, p)),
                   pl.BlockSpec((seq, 2 * wid), lambda b, p: (b, p)), *([any_spec] * n_ag_out)),
        scratch_shapes=[pltpu.VMEM((npp, 2 * tq, LANES), F32), pltpu.VMEM((npp, 2 * tq, LANES), F32),
                        pltpu.VMEM((npp, 2 * tq, 1), F32)] + _ag_scratch(n_ag),
        compiler_params=_params(),
    )(proj, proj, proj, tri_after, *ag_srcs)


def _attn_bwd(proj, dcat, cstats, tri_after, tri_incl, n_seq, seq, rs_sends):
    t = proj.shape[0]
    tq = ATT_TILE
    npp = ATT_PAIRS
    width = proj.shape[1] // 4
    n_blk = width // (npp * LANES)
    n_rs = len(rs_sends)
    rs_shapes = [r.shape for r in rs_sends]
    n_steps = n_seq * n_blk

    def body(q_ref, k_ref, v_ref, do_ref, cs_ref, tria_ref, trii_ref, *rest):
        rs_src, rest = rest[:n_rs], rest[n_rs:]
        out_ref = rest[0]
        rs_dst, rest = rest[1:1 + n_rs], rest[1 + n_rs:]
        dq_acc, dk_acc, dv_acc, ecarry = rest[:4]
        rs_start, rs_finish = _rs_phases(rs_shapes, rs_src, rs_dst, *rest[4:])
        step = pl.program_id(0) * n_blk + pl.program_id(1)
        pl.when(step == 0)(rs_start)
        lane = lax.broadcasted_iota(jnp.int32, (1, LANES), 1)
        ntri = tria_ref[...]
        tri_i = trii_ref[...]
        diag = _diag_mask(tq)
        dk_acc[...] = jnp.zeros_like(dk_acc)
        dv_acc[...] = jnp.zeros_like(dv_acc)

        def q_tile(qi, _):
            r0 = pl.multiple_of(qi * tq, tq)
            qs, dos, cs = [], [], []
            for pp in range(npp):
                cols = slice(pp * LANES, (pp + 1) * LANES)
                qs.append(_stack_heads(q_ref[pl.ds(r0, tq), cols], lane, Q_SCALE))
                dos.append(_stack_heads(do_ref[pl.ds(r0, tq), cols], lane))
                c_off = 2 * pp * LANES
                cs.append(jnp.concatenate([cs_ref[pl.ds(r0, tq), c_off:c_off + LANES],
                                           cs_ref[pl.ds(r0, tq), c_off + LANES:c_off + 2 * LANES]], axis=0))
            ecarry[...] = jnp.zeros_like(ecarry)
            dq_acc[...] = jnp.zeros_like(dq_acc)

            def run_tiles(tiles):
                blocks = [dict(ec=ecarry[pp], dq=None) for pp in range(npp)]
                chains = []
                for kb, mask in tiles:
                    c0 = pl.multiple_of(kb * tq, tq)
                    for pp in range(npp):
                        chains.append(_bwd_chain(
                            blocks[pp], qs[pp], dos[pp], cs[pp], k_ref, v_ref, dk_acc.at[pp], dv_acc.at[pp],
                            c0, kb, slice(pp * LANES, (pp + 1) * LANES), mask, ntri, tri_i, lane, tq))
                _emit_skewed(chains)
                for pp in range(npp):
                    dq_acc[pp] += blocks[pp]["dq"]
                    ecarry[pp] = blocks[pp]["ec"]

            def pair(j, _):
                run_tiles([(2 * j, None), (2 * j + 1, None)])
                return 0

            lax.fori_loop(0, qi // 2, pair, 0)
            odd = qi % 2

            @pl.when(odd == 0)
            def _():
                run_tiles([(qi, diag)])

            @pl.when(odd == 1)
            def _():
                run_tiles([(qi - 1, None), (qi, diag)])

            for pp in range(npp):
                dq = jnp.where(lane < HEAD_DIM, dq_acc[pp, 0:tq, :], dq_acc[pp, tq:2 * tq, :])
                out_ref[0, pl.ds(r0, tq), pp * LANES:(pp + 1) * LANES] = (dq * Q_SCALE).astype(BF16)
            return 0

        lax.fori_loop(0, seq // tq, q_tile, 0)
        for pp in range(npp):
            cols = slice(pp * LANES, (pp + 1) * LANES)
            out_ref[1, :, cols] = dk_acc[pp].astype(BF16)
            out_ref[2, :, cols] = dv_acc[pp].astype(BF16)
        pl.when(step == n_steps - 1)(rs_finish)

    wid = npp * LANES
    blk = lambda off: pl.BlockSpec((seq, wid), lambda b, p: (b, off + p))
    tri_spec = pl.BlockSpec((2 * tq, tq), lambda b, p: (0, 0))
    any_spec = pl.BlockSpec(memory_space=pl.ANY)
    return pl.pallas_call(
        body, name="attn_bwd", grid=(n_seq, n_blk),
        out_shape=(jax.ShapeDtypeStruct((4, t, width), BF16), *_rs_out(rs_sends)),
        in_specs=[blk(0), blk(n_blk), blk(2 * n_blk), pl.BlockSpec((seq, wid), lambda b, p: (b, p)),
                  pl.BlockSpec((seq, 2 * wid), lambda b, p: (b, p)), tri_spec,
                  pl.BlockSpec((tq, tq), lambda b, p: (0, 0))] + [any_spec] * n_rs,
        out_specs=(pl.BlockSpec((3, seq, wid), lambda b, p: (0, b, p)), *([any_spec] * n_rs)),
        scratch_shapes=[pltpu.VMEM((npp, 2 * tq, LANES), F32), pltpu.VMEM((npp, seq, LANES), F32),
                        pltpu.VMEM((npp, seq, LANES), F32), pltpu.VMEM((npp, 2 * tq, 1), F32)]
        + _rs_scratch(rs_sends),
        compiler_params=_params(),
    )(proj, proj, proj, dcat, cstats, tri_after, tri_incl, *rs_sends)


def _window_sum(v, g, rows, forward):
    s_len = v.shape[0]
    s = v
    for step in range(g + 1):
        sh = 1 << step
        if forward:
            s = s + jnp.where(rows < s_len - sh, pltpu.roll(s, s_len - sh, axis=0), 0.0)
        else:
            s = s + jnp.where(rows >= sh, pltpu.roll(s, sh, axis=0), 0.0)
    return s


def _window_count(g, rows):
    return jnp.minimum(rows + 1, POOL_WINDOWS[g]).astype(F32)


def _pooled(u, g, rows):
    return _window_sum(u, g, rows, forward=False) / _window_count(g, rows) - u


def _group_cols(g):
    return slice(g * POOL_GROUP_DIM, (g + 1) * POOL_GROUP_DIM)


def _pool_fwd(proj, w_pool, pool_scale, cat, n_seq, seq):
    n_grp = len(POOL_WINDOWS)
    width = n_grp * POOL_GROUP_DIM
    assert [1 << (g + 1) for g in range(n_grp)] == list(POOL_WINDOWS)

    def body(u_ref, w_ref, s_ref, alias_ref, o_ref):
        del alias_ref
        rows = lax.broadcasted_iota(jnp.int32, (seq, 1), 0)
        for g in range(n_grp):
            cols = _group_cols(g)
            pooled = _pooled(u_ref[:, cols].astype(F32), g, rows)
            y = _dot_nn(pooled.astype(BF16), w_ref[g].astype(BF16))
            o_ref[:, cols] = (y * s_ref[:, cols]).astype(BF16)

    return pl.pallas_call(
        body, name="pool_fwd", grid=(n_seq,),
        out_shape=jax.ShapeDtypeStruct(cat.shape, BF16),
        in_specs=[pl.BlockSpec((seq, width), lambda b: (b, 3)),
                  pl.BlockSpec((n_grp, POOL_GROUP_DIM, POOL_GROUP_DIM), lambda b: (0, 0, 0)),
                  pl.BlockSpec((1, width), lambda b: (0, 0)),
                  pl.BlockSpec(memory_space=pl.ANY)],
        out_specs=pl.BlockSpec((None, seq, width), lambda b: (1, b, 0)),
        input_output_aliases={3: 0},
        compiler_params=_params(),
    )(proj, w_pool, pool_scale, cat)


def _pool_bwd(proj, dcat, w_pool, pool_scale, dqkv, n_seq, seq):
    n_grp = len(POOL_WINDOWS)
    width = n_grp * POOL_GROUP_DIM

    def body(u_ref, dp_ref, w_ref, s_ref, alias_ref, du_ref, gw_ref, gs_ref):
        del alias_ref
        b = pl.program_id(0)
        rows = lax.broadcasted_iota(jnp.int32, (seq, 1), 0)
        for g in range(n_grp):
            cols = _group_cols(g)
            pb = _pooled(u_ref[:, cols].astype(F32), g, rows).astype(BF16)
            wb = w_ref[g].astype(BF16)
            z = _dot_nn(pb, wb)
            dp = dp_ref[:, cols].astype(F32)
            _acc(gs_ref.at[:, cols], _colsum(dp * z), b == 0)
            dys = (dp * s_ref[:, cols]).astype(BF16)
            _acc(gw_ref.at[g], _dot_tn(pb, dys), b == 0)
            dpooled = _dot_nt(dys, wb)
            du = _window_sum(dpooled / _window_count(g, rows), g, rows, forward=True) - dpooled
            du_ref[:, cols] = du.astype(BF16)

    return pl.pallas_call(
        body, name="pool_bwd", grid=(n_seq,),
        out_shape=(jax.ShapeDtypeStruct(dqkv.shape, BF16),
                   jax.ShapeDtypeStruct((n_grp, POOL_GROUP_DIM, POOL_GROUP_DIM), F32),
                   jax.ShapeDtypeStruct((1, width), F32)),
        in_specs=[pl.BlockSpec((seq, width), lambda b: (b, 3)),
                  pl.BlockSpec((seq, width), lambda b: (b, 1)),
                  pl.BlockSpec((n_grp, POOL_GROUP_DIM, POOL_GROUP_DIM), lambda b: (0, 0, 0)),
                  pl.BlockSpec((1, width), lambda b: (0, 0)),
                  pl.BlockSpec(memory_space=pl.ANY)],
        out_specs=(pl.BlockSpec((None, seq, width), lambda b: (3, b, 0)),
                   pl.BlockSpec((n_grp, POOL_GROUP_DIM, POOL_GROUP_DIM), lambda b: (0, 0, 0)),
                   pl.BlockSpec((1, width), lambda b: (0, 0))),
        input_output_aliases={4: 0},
        compiler_params=_params(),
    )(proj, dcat, w_pool, pool_scale, dqkv)


SUBLANES = 8


def _cond_fwd(c_pad, n_seq, w_cond, b_cols):
    rows_c, d = c_pad.shape
    cols = w_cond.shape[1]
    n_all = N_DEV * n_seq

    def body(c_ref, w_ref, b_ref, call_ref, modg_ref, blk, *sems):
        _gather_short(c_ref, call_ref, *sems, i=0)
        cv = jnp.concatenate([call_ref[dev, :n_seq] for dev in range(N_DEV)], axis=0)
        a = cv * jax.nn.sigmoid(cv)
        blk[...] = jnp.dot(a, w_ref[...], preferred_element_type=F32, precision=lax.Precision.HIGHEST) + b_ref[...]
        _gather_short(blk, modg_ref, *sems, i=1)

    vmem = pl.BlockSpec(memory_space=pltpu.VMEM)
    return pl.pallas_call(
        body, name="cond_fwd",
        out_shape=(jax.ShapeDtypeStruct((N_DEV, rows_c, d), F32), jax.ShapeDtypeStruct((N_DEV, n_all, cols), F32)),
        in_specs=[vmem, vmem, vmem], out_specs=(vmem, vmem),
        scratch_shapes=[pltpu.VMEM((n_all, cols), F32)] + _gather_short_scratch(2),
        compiler_params=_params(),
    )(c_pad, w_cond, b_cols)


def _cond_bwd_adamw(c_all, dmod_all, dmod_cols, w, m_w, v_w, b, m_b, v_b):
    def body(c_ref, dm_ref, dmc_ref, w_ref, mw_ref, vw_ref, b_ref, mb_ref, vb_ref,
             gw_ref, dw_ref, nmw_ref, nvw_ref, gb_ref, db_ref, nmb_ref, nvb_ref):
        cv = c_ref[...]
        a = cv * jax.nn.sigmoid(cv)
        gw = lax.dot_general(a, dmc_ref[...], (((0,), (0,)), ((), ())),
                             preferred_element_type=F32, precision=lax.Precision.HIGHEST)
        gw_ref[...] = gw
        dw_ref[...], nmw_ref[...], nvw_ref[...] = _adamw_math(w_ref[...], gw, mw_ref[...], vw_ref[...])
        gb = _colsum(dm_ref[...])
        gb_ref[...] = gb
        db_ref[...], nmb_ref[...], nvb_ref[...] = _adamw_math(b_ref[...], gb, mb_ref[...], vb_ref[...])

    w_sds, b_sds = jax.ShapeDtypeStruct(w.shape, F32), jax.ShapeDtypeStruct(b.shape, F32)
    outs = pl.pallas_call(
        body, name="cond_bwd_adamw", out_shape=(w_sds,) * 4 + (b_sds,) * 4, compiler_params=_params(),
    )(c_all, dmod_all, dmod_cols, w, m_w, v_w, b, m_b, v_b)
    return outs[:4], outs[4:]


def _adamw_math(w, g, m, v):
    m = ADAM_B1 * m + (1.0 - ADAM_B1) * g
    v = ADAM_B2 * v + (1.0 - ADAM_B2) * (g * g)
    m_hat = m / (1.0 - ADAM_B1 ** ADAM_STEP)
    v_hat = v / (1.0 - ADAM_B2 ** ADAM_STEP)
    delta = -ADAM_LR * (m_hat / (jnp.sqrt(v_hat) + ADAM_EPS) + ADAM_WD * w)
    return delta, m, v


def _adamw_small(ws, gparts, ms, vs, name):
    n = len(ws)

    def body(*refs):
        w_r, g_r, m_r, v_r = refs[:n], refs[n:2 * n], refs[2 * n:3 * n], refs[3 * n:4 * n]
        outs = refs[4 * n:]
        for i in range(n):
            g = g_r[i][0]
            for dev in range(1, g_r[i].shape[0]):
                g = g + g_r[i][dev]
            delta, m, v = _adamw_math(w_r[i][...], g, m_r[i][...], v_r[i][...])
            outs[i][...] = g
            outs[n + i][...] = delta
            outs[2 * n + i][...] = m
            outs[3 * n + i][...] = v

    sds = [jax.ShapeDtypeStruct(w.shape, F32) for w in ws]
    return pl.pallas_call(
        body, name=name, out_shape=tuple(sds * 4), compiler_params=_params(),
    )(*ws, *gparts, *ms, *vs)


def kernel(x, c, w_cond, b_cond, g_mix_pre, g_mix_post, w_in, w_pool, pool_scale, w_out, g_ffn_pre, g_ffn_post, w_gate, w_up, w_down, loss_target, m_w_cond, m_b_cond, m_g_mix_pre, m_g_mix_post, m_w_in, m_w_pool, m_pool_scale, m_w_out, m_g_ffn_pre, m_g_ffn_post, m_w_gate, m_w_up, m_w_down, v_w_cond, v_b_cond, v_g_mix_pre, v_g_mix_post, v_w_in, v_w_pool, v_pool_scale, v_w_out, v_g_ffn_pre, v_g_ffn_post, v_w_gate, v_w_up, v_w_down):
    n_seq, seq, d = x.shape
    t = n_seq * seq
    xi, yi, ci = _mesh_pos()
    me = 4 * xi + 2 * yi + ci
    x2 = x.reshape(t, d)
    tgt2 = loss_target.reshape(t, d)
    in_rows = w_in.shape[2]
    out_rows = w_out.shape[1]
    ff_rows = w_gate.shape[2]
    ff = N_DEV * ff_rows
    cond_cols = w_cond.shape[2]

    win_t = w_in[0].T.astype(BF16)
    wout_s = w_out[0].astype(BF16)
    wg_t = w_gate[0].T.astype(BF16)
    wu_t = w_up[0].T.astype(BF16)
    wd_s = w_down[0].astype(BF16)

    b_cols = lax.dynamic_slice_in_dim(b_cond, me * cond_cols, cond_cols, axis=1)
    c_all8, mod_g = _cond_fwd(jnp.pad(c, ((0, SUBLANES - n_seq), (0, 0))), n_seq, w_cond[0], b_cols)
    c_all = c_all8[:, :n_seq].reshape(N_DEV * n_seq, d)
    mod_mine = lax.dynamic_slice_in_dim(mod_g, me * n_seq, n_seq, axis=1)
    mod = jnp.transpose(mod_mine, (1, 0, 2)).reshape(n_seq, N_MOD, d)

    h1, win_g = _pre_mix(x2, g_mix_pre, mod, seq, [win_t],
                         [jax.ShapeDtypeStruct((N_DEV, in_rows, d), BF16)], [(0, ())])
    win_full = win_g.reshape(N_DEV * in_rows, d)
    proj = _matmul(h1, win_full, "nt", BF16, 512, N_DEV * in_rows, d, "proj")
    tq = ATT_TILE
    ids = jnp.arange(tq)
    tri_after = jnp.tile(-(ids[:, None] >= ids[None, :]).astype(BF16), (2, 1))
    tri_incl = (ids[:, None] <= ids[None, :]).astype(BF16)
    attn, cstats, wout_g, wgu_g, wd_g = _attn_fwd(
        proj, tri_after, n_seq, seq, [wout_s, wg_t, wu_t, wd_s],
        [jax.ShapeDtypeStruct((N_DEV, out_rows, d), BF16), jax.ShapeDtypeStruct((2, N_DEV, ff_rows, d), BF16),
         jax.ShapeDtypeStruct((N_DEV, ff_rows, d), BF16)],
        [(0, ()), (1, (0,)), (1, (1,)), (2, ())])
    wout_full = wout_g.reshape(N_DEV * out_rows, d)
    wgu_full = wgu_g.reshape(2, ff, d)
    wd_full = wd_g.reshape(ff, d)
    cat = _pool_fwd(proj, w_pool[0], pool_scale, attn, n_seq, seq)
    tok_f32, tok_bf16 = jax.ShapeDtypeStruct((t, d), F32), jax.ShapeDtypeStruct((t, d), BF16)
    seq_sds, vec_sds = jax.ShapeDtypeStruct((n_seq, 1, d), F32), jax.ShapeDtypeStruct((1, d), F32)
    mix, x1, h2 = _matmul_rows(
        cat, wout_full.reshape(2, d // 2, d), ROW_TILE, seq, "mix_mid", _mid_epilogue,
        [x2, g_mix_post, g_ffn_pre, mod], ["tok", "vec", "vec", "mod"],
        [tok_f32, tok_f32, tok_bf16], ["tok", "tok", "tok"])
    gu, act = _ffn_up(h2, wgu_full, 512, ff // 2)
    loss_sum, dy, df, dgate_f, gg_ffn_post = _matmul_rows(
        act, wd_full, ROW_TILE, seq, "ffn_down_post", _post_epilogue,
        [x1, tgt2, g_ffn_post, mod], ["tok", "tok", "vec", "mod"],
        [jax.ShapeDtypeStruct((1, LANES), F32), tok_f32, tok_bf16, seq_sds, vec_sds],
        ["loss", "tok", "tok", "seq", "vec"])

    dgu = _ffn_act_bwd(df, wd_full, gu, 512, ff // 2)
    gwd, gwd_b = _matmul(act, df, "tn", F32, ff // 2, d // 2, t, "grad_w_down", bf16_copy=True)
    gwgu, gwgu_b = _matmul(dgu, h2, "tn", F32, ff // 2, d // 2, t, "grad_w_gate_up", bf16_copy=True)
    dx1, dmix, dshift_f, dscale_f, dgate_m, gg_ffn_pre, gg_mix_post = _matmul_rows(
        dgu, wgu_full, ROW_TILE, seq, "dh2_bwd_mid", _bwd_mid_epilogue,
        [dy, x1, mix, g_ffn_pre, g_mix_post, mod], ["tok", "tok", "tok", "vec", "vec", "mod"],
        [tok_f32, tok_bf16, seq_sds, seq_sds, seq_sds, vec_sds, vec_sds],
        ["tok", "tok", "seq", "seq", "seq", "vec", "vec"])
    dcat = _matmul(dmix, wout_full, "nt", BF16, 512, d, d, "dcat")
    gwout, gwout_b = _matmul(cat, dmix, "tn", F32, d // 2, d, t, "grad_w_out", bf16_copy=True)
    dqkv, rv_wgu, rv_wd, rv_wout = _attn_bwd(
        proj, dcat, cstats, tri_after, tri_incl, n_seq, seq,
        [gwgu_b.reshape(2, N_DEV, ff_rows, d), gwd_b.reshape(1, N_DEV, ff_rows, d),
         gwout_b.reshape(1, N_DEV, out_rows, d)])
    dproj, gw_pool, gs_pool = _pool_bwd(proj, dcat, w_pool[0], pool_scale, dqkv, n_seq, seq)
    pad_d = lambda v: jnp.pad(v, ((0, 0), (0, d - v.shape[1])))
    n_gw = gw_pool.size // d
    early = jnp.concatenate(
        [gg_mix_post, gg_ffn_pre, gg_ffn_post, pad_d(gs_pool), pad_d(loss_sum), jnp.zeros((3, d), F32),
         gw_pool.reshape(n_gw, d),
         jnp.concatenate([dgate_m, dshift_f, dscale_f, dgate_f], axis=1).reshape(n_seq * 4, d)], axis=0)
    gwin, gwin_b, early_g = _matmul(
        dproj, h1, "tn", F32, d // 2, d, t, "grad_w_in", bf16_copy=True,
        ag=([early], [jax.ShapeDtypeStruct((N_DEV,) + early.shape, F32)], [(0, ())]))
    grad_x, dshift_m, dscale_m, gg_mix_pre, rv_win = _matmul_rows(
        dproj, win_full.reshape(4, d // 2, d), ROW_TILE, seq, "dh1_bwd_pre", _bwd_pre_epilogue,
        [dx1, x2, g_mix_pre, mod], ["tok", "tok", "vec", "mod"],
        [tok_f32, seq_sds, seq_sds, vec_sds], ["tok", "seq", "seq", "vec"],
        rs_sends=[gwin_b.reshape(1, N_DEV, in_rows, d)])


    late = jnp.concatenate([gg_mix_pre, dshift_m.reshape(n_seq, d), dscale_m.reshape(n_seq, d),
                            jnp.zeros((8 - 1 - 2 * n_seq, d), F32)], axis=0)
    late_g = _all_gather(late, "ag_late")
    loss = jnp.sum(early_g[:, 4, 0]) * (0.5 / d)
    dmod_all = jnp.concatenate(
        [late_g[:, 1:1 + n_seq, None, :], late_g[:, 1 + n_seq:1 + 2 * n_seq, None, :],
         early_g[:, 8 + n_gw:, :].reshape(N_DEV, n_seq, 4, d)], axis=2).reshape(N_DEV * n_seq, N_MOD * d)
    dmod_cols = lax.dynamic_slice_in_dim(dmod_all, me * cond_cols, cond_cols, axis=1)
    o_cond, o_bcond = _cond_bwd_adamw(c_all, dmod_all, dmod_cols, w_cond[0], m_w_cond[0], v_w_cond[0],
                                      b_cond, m_b_cond, v_b_cond)
    o_cond = tuple(o[None] for o in o_cond)

    small_ws = [g_mix_pre, g_mix_post, g_ffn_pre, g_ffn_post, pool_scale, w_pool.reshape(-1, POOL_GROUP_DIM)]
    small_ms = [m_g_mix_pre, m_g_mix_post, m_g_ffn_pre, m_g_ffn_post, m_pool_scale, m_w_pool.reshape(-1, POOL_GROUP_DIM)]
    small_vs = [v_g_mix_pre, v_g_mix_post, v_g_ffn_pre, v_g_ffn_post, v_pool_scale, v_w_pool.reshape(-1, POOL_GROUP_DIM)]
    small_gparts = [late_g[:, 0:1, :], early_g[:, 0:1, :], early_g[:, 1:2, :], early_g[:, 2:3, :],
                    early_g[:, 3:4, :pool_scale.shape[1]],
                    early_g[:, 8:8 + n_gw, :].reshape(N_DEV, -1, POOL_GROUP_DIM)]
    so = _adamw_small(small_ws, small_gparts, small_ms, small_vs, "adamw_small")
    ns = len(small_ws)
    sg, sdl, sm, sv = so[:ns], so[ns:2 * ns], so[2 * ns:3 * ns], so[3 * ns:]
    pool_shape = w_pool.shape
    fix = lambda lst: [lst[0], lst[1], lst[2], lst[3], lst[4], lst[5].reshape(pool_shape)]
    sg, sdl, sm, sv = fix(sg), fix(sdl), fix(sm), fix(sv)


    def reduced(mine, recv, slab, w, m, v, name, transposed=False, transpose=False):
        turn = (lambda u: u.T) if transposed else (lambda u: u)
        outs = _rs_final_adamw(mine, recv, slab, turn(w[0]), turn(m[0]), turn(v[0]), name, transpose)
        return tuple(turn(o)[None] for o in outs)

    o_in = reduced(gwin.reshape(1, N_DEV, in_rows, d), rv_win, 0, w_in, m_w_in, v_w_in, "adamw_w_in",
                   transpose=True)
    o_out = reduced(gwout.reshape(1, N_DEV, out_rows, d), rv_wout, 0, w_out, m_w_out, v_w_out, "adamw_w_out")
    gwgu8 = gwgu.reshape(2, N_DEV, ff_rows, d)
    o_gate = reduced(gwgu8, rv_wgu, 0, w_gate, m_w_gate, v_w_gate, "adamw_w_gate", transposed=True)
    o_up = reduced(gwgu8, rv_wgu, 1, w_up, m_w_up, v_w_up, "adamw_w_up", transposed=True)
    o_down = reduced(gwd.reshape(1, N_DEV, ff_rows, d), rv_wd, 0, w_down, m_w_down, v_w_down, "adamw_w_down")

    def pick(k):
        small_k = [sg, sdl, sm, sv][k]
        return [o_cond[k], o_bcond[k], small_k[0], small_k[1], o_in[k], small_k[5], small_k[4], o_out[k],
                small_k[2], small_k[3], o_gate[k], o_up[k], o_down[k]]

    return (loss, grad_x.reshape(n_seq, seq, d), *pick(0), *pick(1), *pick(2), *pick(3))
```

```python
import functools
import math

import jax
import jax.numpy as jnp
from jax import lax
from jax.experimental import pallas as pl
from jax.experimental.pallas import tpu as pltpu

F32 = jnp.float32
BF16 = jnp.bfloat16
MESH = pl.DeviceIdType.MESH

N_DEV = 8
HEAD_DIM = 64
LANES = 128
POOL_WINDOWS = (2, 4, 8, 16)
POOL_GROUP_DIM = 128
N_MOD = 6
EPS = 1e-6
ATT_TILE = 256
ATT_PAIRS = 2
VMEM_LIMIT = 56 * 1024 * 1024
ADAMW_COL_TILE = 256

ADAM_LR = 0.001
ADAM_B1 = 0.9
ADAM_B2 = 0.999
ADAM_EPS = 1e-08
ADAM_WD = 0.01
ADAM_STEP = 10


def _params(**kw):
    return pltpu.CompilerParams(vmem_limit_bytes=VMEM_LIMIT, **kw)


def _dot_nn(a, b):
    return jnp.dot(a, b, preferred_element_type=F32)


def _dot_nt(a, b):
    return lax.dot_general(a, b, (((1,), (1,)), ((), ())), preferred_element_type=F32)


def _dot_tn(a, b):
    return lax.dot_general(a, b, (((0,), (0,)), ((), ())), preferred_element_type=F32)


def _mesh_pos():
    return lax.axis_index("x"), lax.axis_index("y"), lax.axis_index("c")


def _ag_phases(dests, src, outs, send_sems, recv_sems, local_sems):
    n = len(src)
    x, y, c = _mesh_pos()
    me, sibling = (x, y, c), (x, y, 1 - c)
    chips = [(1 - x, y), (x, 1 - y), (1 - x, 1 - y)]

    def slot(i, dev):
        oi, prefix = dests[i]
        px, py, pc = dev
        return outs[oi].at[prefix + (4 * px + 2 * py + pc,)]

    def copy(i, k, block, to, from_src=False):
        return pltpu.make_async_remote_copy(
            src_ref=src[i] if from_src else slot(i, block), dst_ref=slot(i, block),
            send_sem=send_sems.at[i, k], recv_sem=recv_sems.at[i, k],
            device_id=to, device_id_type=MESH)

    def mine(i):
        return pltpu.make_async_copy(src[i], slot(i, me), local_sems.at[i])

    def first(i):
        return [copy(i, 0, me, sibling, from_src=True)] + [
            copy(i, 1 + j, me, (*chip, c), from_src=True) for j, chip in enumerate(chips)]

    def passed(i, j):
        return copy(i, 4 + j, (*chips[j], c), sibling)

    def start():
        for i in range(n):
            mine(i).start()
        for i in range(n):
            for cp in first(i):
                cp.start()

    def forward():
        for j, chip in enumerate(chips):
            for i in range(n):
                copy(i, 1 + j, (*chip, c), me).wait_recv()
                passed(i, j).start()

    def finish():
        for i in range(n):
            copy(i, 0, sibling, me).wait_recv()
            for j, chip in enumerate(chips):
                copy(i, 4 + j, (*chip, 1 - c), me).wait_recv()
        for i in range(n):
            for cp in first(i) + [passed(i, j) for j in range(3)]:
                cp.wait_send()
            mine(i).wait()

    return start, forward, finish


def _ag_scratch(n):
    return [pltpu.SemaphoreType.DMA((n, 7)), pltpu.SemaphoreType.DMA((n, 7)), pltpu.SemaphoreType.DMA((n,))]


def _all_gather(src, name):
    def body(src_ref, out_ref, send_sems, recv_sems, local_sem):
        x, y, c = _mesh_pos()

        def copy(k, block):
            px, py, pc = x ^ (k >> 2), y ^ ((k >> 1) & 1), c ^ (k & 1)
            bx, by, bc = (x, y, c) if block == "mine" else (px, py, pc)
            return pltpu.make_async_remote_copy(
                src_ref=src_ref, dst_ref=out_ref.at[4 * bx + 2 * by + bc],
                send_sem=send_sems.at[k - 1], recv_sem=recv_sems.at[k - 1],
                device_id=(px, py, pc), device_id_type=MESH)

        local = pltpu.make_async_copy(src_ref, out_ref.at[4 * x + 2 * y + c], local_sem.at[0])
        local.start()
        for k in range(1, N_DEV):
            copy(k, "mine").start()
        for k in range(1, N_DEV):
            copy(k, "mine").wait_send()
        for k in range(1, N_DEV):
            copy(k, "theirs").wait_recv()
        local.wait()

    any_spec = pl.BlockSpec(memory_space=pl.ANY)
    return pl.pallas_call(
        body, name=name,
        out_shape=jax.ShapeDtypeStruct((N_DEV,) + src.shape, src.dtype),
        in_specs=[any_spec], out_specs=any_spec,
        scratch_shapes=[pltpu.SemaphoreType.DMA((N_DEV - 1,)), pltpu.SemaphoreType.DMA((N_DEV - 1,)),
                        pltpu.SemaphoreType.DMA((1,))],
    )(src)


def _rs_phases(shapes, src, dst, send_sems, recv_sems):
    x, y, c = _mesh_pos()

    def copies():
        out = []
        n = 0
        for i, shp in enumerate(shapes):
            for m in range(shp[0]):
                for k in range(1, N_DEV):
                    px, py, pc = x ^ (k >> 2), y ^ ((k >> 1) & 1), c ^ (k & 1)
                    out.append(pltpu.make_async_remote_copy(
                        src_ref=src[i].at[m, 4 * px + 2 * py + pc], dst_ref=dst[i].at[m, k - 1],
                        send_sem=send_sems.at[n], recv_sem=recv_sems.at[n],
                        device_id=(px, py, pc), device_id_type=MESH))
                    n += 1
        return out

    def start():
        for cp in copies():
            cp.start()

    def finish():
        for cp in copies():
            cp.wait_send()
        for cp in copies():
            cp.wait_recv()

    return start, finish


def _rs_out(sends):
    return [jax.ShapeDtypeStruct((s.shape[0], N_DEV - 1) + s.shape[2:], s.dtype) for s in sends]


def _rs_scratch(sends):
    total = sum((N_DEV - 1) * s.shape[0] for s in sends)
    return [pltpu.SemaphoreType.DMA((total,)), pltpu.SemaphoreType.DMA((total,))]


def _rs_final_adamw(mine, recv, slab, w, m, v, name, transpose=False):
    _, _, r, cdim = mine.shape
    tc = ADAMW_COL_TILE
    assert cdim % tc == 0 and w.shape == ((cdim, r) if transpose else (r, cdim)), (name, w.shape)
    x, y, c = _mesh_pos()
    me = jnp.reshape(4 * x + 2 * y + c, (1,)).astype(jnp.int32)

    def body(me_ref, p_ref, r_ref, w_ref, m_ref, v_ref, g_ref, d_ref, nm_ref, nv_ref):
        del me_ref
        g = p_ref[...]
        for k in range(N_DEV - 1):
            g = g + r_ref[k].astype(F32)
        if transpose:
            g = g.T
        g_ref[...] = g
        d_ref[...], nm_ref[...], nv_ref[...] = _adamw_math(w_ref[...], g, m_ref[...], v_ref[...])

    if transpose:
        w_spec = pl.BlockSpec((tc, r), lambda j, s: (j, 0))
    else:
        w_spec = pl.BlockSpec((r, tc), lambda j, s: (0, j))
    sds = jax.ShapeDtypeStruct(w.shape, F32)
    return pl.pallas_call(
        body, name=name, out_shape=(sds, sds, sds, sds),
        grid_spec=pltpu.PrefetchScalarGridSpec(
            num_scalar_prefetch=1, grid=(cdim // tc,),
            in_specs=[pl.BlockSpec((None, None, r, tc), lambda j, s: (slab, s[0], 0, j)),
                      pl.BlockSpec((None, N_DEV - 1, r, tc), lambda j, s: (slab, 0, 0, j)),
                      w_spec, w_spec, w_spec],
            out_specs=(w_spec, w_spec, w_spec, w_spec)),
        compiler_params=_params(),
    )(me, mine, recv, w, m, v)


def _matmul(a, b, mode, out_dtype, tm, tn, tk, name, bf16_copy=False, rs_sends=(), ag=None):
    ga = a.shape[0] if a.ndim == 3 else None
    gb = b.shape[0] if b.ndim == 3 else None
    a2, b2 = a.shape[-2:], b.shape[-2:]
    if mode == "nn":
        (m, k), n = a2, b2[1]
    elif mode == "nt":
        (m, k), n = a2, b2[0]
    else:
        (k, m), n = a2, b2[1]
    assert m % tm == 0 and n % tn == 0 and k % tk == 0, (name, m, n, k)
    nk = k // tk
    g_n = ga or 1
    batch_out = mode == "tn" and ga is not None
    n_red = nk if batch_out else nk * g_n
    dot = {"nn": _dot_nn, "nt": _dot_nt, "tn": _dot_tn}[mode]
    acc_in_out = out_dtype == F32

    n_rs = len(rs_sends)
    rs_shapes = [r.shape for r in rs_sends]
    ag_srcs, ag_out_shapes, ag_dests = ag if ag is not None else ((), (), ())
    n_ag, n_ag_out = len(ag_srcs), len(ag_out_shapes)
    n_out = 2 if bf16_copy else 1
    assert not bf16_copy or acc_in_out
    assert not (n_rs and n_ag)

    def body(a_ref, b_ref, *rest):
        rs_src, rest = rest[:n_rs], rest[n_rs:]
        ag_src, rest = rest[:n_ag], rest[n_ag:]
        o_ref = rest[0]
        copy_ref = rest[1] if bf16_copy else None
        rs_dst, rest = rest[n_out:n_out + n_rs], rest[n_out + n_rs:]
        ag_out, scratch = rest[:n_ag_out], rest[n_ag_out:]
        first = functools.reduce(jnp.logical_and, [pl.program_id(ax) == 0 for ax in range(4)])
        last = functools.reduce(jnp.logical_and, [pl.program_id(ax) == grid[ax] - 1 for ax in range(4)])
        if n_rs:
            rs_start, rs_finish = _rs_phases(rs_shapes, rs_src, rs_dst, *scratch[-2:])
            pl.when(first)(rs_start)
        if n_ag:
            ag_start, ag_forward, ag_finish = _ag_phases(ag_dests, ag_src, ag_out, *scratch[-3:])
            pl.when(first)(ag_start)
        p = dot(a_ref[...], b_ref[...])
        kk = pl.program_id(3) if batch_out else pl.program_id(2) * nk + pl.program_id(3)
        if n_red == 1:
            o_ref[...] = p.astype(out_dtype)
            if bf16_copy:
                copy_ref[...] = p.astype(BF16)
        else:
            acc = o_ref if acc_in_out else scratch[0]

            @pl.when(kk == 0)
            def _():
                acc[...] = p

            @pl.when(kk > 0)
            def _():
                acc[...] += p

            @pl.when(kk == n_red - 1)
            def _():
                if not acc_in_out:
                    o_ref[...] = acc[...].astype(out_dtype)
                if bf16_copy:
                    copy_ref[...] = acc[...].astype(BF16)

        if n_rs:
            pl.when(last)(rs_finish)
        if n_ag:
            @pl.when(last)
            def _():
                ag_forward()
                ag_finish()

    def order(ids):
        return ids if batch_out else (ids[2], ids[0], ids[1], ids[3])

    def a_idx(*ids):
        g, i, j, kq = order(ids)
        blk = {"nn": (i, kq), "nt": (i, kq), "tn": (kq, i)}[mode]
        return (g,) + blk if ga is not None else blk

    def b_idx(*ids):
        g, i, j, kq = order(ids)
        blk = {"nn": (kq, j), "nt": (j, kq), "tn": (kq, j)}[mode]
        return (g,) + blk if gb is not None else blk

    def o_idx(*ids):
        g, i, j, kq = order(ids)
        return (g, i, j) if batch_out else (i, j)

    a_blk = {"nn": (tm, tk), "nt": (tm, tk), "tn": (tk, tm)}[mode]
    b_blk = {"nn": (tk, tn), "nt": (tn, tk), "tn": (tk, tn)}[mode]
    if ga is not None:
        a_blk = (None,) + a_blk
    if gb is not None:
        b_blk = (None,) + b_blk
    if batch_out:
        out_shape = jax.ShapeDtypeStruct((g_n, m, n), out_dtype)
        o_blk = (None, tm, tn)
        grid = (g_n, m // tm, n // tn, nk)
    else:
        out_shape = jax.ShapeDtypeStruct((m, n), out_dtype)
        o_blk = (tm, tn)
        grid = (m // tm, n // tn, g_n, nk)
    scratch = [] if (acc_in_out or n_red == 1) else [pltpu.VMEM((tm, tn), F32)]
    any_spec = pl.BlockSpec(memory_space=pl.ANY)
    out_shapes = [out_shape] + ([jax.ShapeDtypeStruct(out_shape.shape, BF16)] if bf16_copy else [])
    res = pl.pallas_call(
        body, name=name, out_shape=tuple(out_shapes + _rs_out(rs_sends) + list(ag_out_shapes)), grid=grid,
        in_specs=[pl.BlockSpec(a_blk, a_idx), pl.BlockSpec(b_blk, b_idx)] + [any_spec] * (n_rs + n_ag),
        out_specs=tuple([pl.BlockSpec(o_blk, o_idx)] * n_out + [any_spec] * (n_rs + n_ag_out)),
        scratch_shapes=scratch + (_rs_scratch(rs_sends) if n_rs else []) + (_ag_scratch(n_ag) if n_ag else []),
        compiler_params=_params(),
    )(a, b, *rs_sends, *ag_srcs)
    return res if len(res) > 1 else res[0]


EW_TILE = 256
ROW_TILE = 512
EPILOGUE_CHUNKS = 8
MXU_WIDTH = 256


def _rms(v):
    return lax.rsqrt(jnp.mean(v * v, axis=-1, keepdims=True) + EPS)


def _rms_bwd(dhat, vh, r):
    return r * (dhat - vh * jnp.mean(dhat * vh, axis=-1, keepdims=True))


def _tok_spec(tm, d):
    return pl.BlockSpec((tm, d), lambda i: (i, 0))


def _copy_tokens(v, name):
    t, d = v.shape

    def body(v_ref, o_ref):
        o_ref[...] = v_ref[...]

    return pl.pallas_call(
        body, name=name, out_shape=jax.ShapeDtypeStruct((t, d), v.dtype), grid=(t // ROW_TILE,),
        in_specs=[_tok_spec(ROW_TILE, d)], out_specs=_tok_spec(ROW_TILE, d), compiler_params=_params(),
    )(v)


def _vec_spec(d):
    return pl.BlockSpec((1, d), lambda i: (0, 0))


def _mod_spec(tiles_per_seq, d):
    return pl.BlockSpec((None, N_MOD, d), lambda i: (i // tiles_per_seq, 0, 0))


def _seq_acc_spec(tiles_per_seq, d):
    return pl.BlockSpec((None, 1, d), lambda i: (i // tiles_per_seq, 0, 0))


def _acc(ref, val, first):
    if first is False:
        ref[...] += val
        return

    @pl.when(first)
    def _():
        ref[...] = val

    @pl.when(jnp.logical_not(first))
    def _():
        ref[...] += val


def _colsum(v):
    return jnp.sum(v, axis=0, keepdims=True)


def _pre_mix(x2, g_pre, mod, seq, ag_srcs, ag_out_shapes, ag_dests):
    t, d = x2.shape
    tm = EW_TILE
    n_steps = t // tm
    n_ag, n_ag_out = len(ag_srcs), len(ag_out_shapes)

    def body(x_ref, g_ref, mod_ref, *rest):
        ag_src, h_ref = rest[:n_ag], rest[n_ag]
        ag_out, sems = rest[n_ag + 1:n_ag + 1 + n_ag_out], rest[n_ag + 1 + n_ag_out:]
        ag_start, ag_forward, ag_finish = _ag_phases(ag_dests, ag_src, ag_out, *sems)
        step = pl.program_id(0)
        pl.when(step == 0)(ag_start)
        xv = x_ref[...]
        n = xv * _rms(xv) * g_ref[...]
        h_ref[...] = (n * (1.0 + mod_ref[1:2, :]) + mod_ref[0:1, :]).astype(BF16)

        @pl.when(step == n_steps - 1)
        def _():
            ag_forward()
            ag_finish()

    any_spec = pl.BlockSpec(memory_space=pl.ANY)
    return pl.pallas_call(
        body, name="pre_mix", out_shape=(jax.ShapeDtypeStruct((t, d), BF16), *ag_out_shapes), grid=(n_steps,),
        in_specs=[_tok_spec(tm, d), _vec_spec(d), _mod_spec(seq // tm, d)] + [any_spec] * n_ag,
        out_specs=(_tok_spec(tm, d), *([any_spec] * n_ag_out)),
        scratch_shapes=_ag_scratch(n_ag), compiler_params=_params(),
    )(x2, g_pre, mod, *ag_srcs)


def _matmul_rows(a, b, tm, seq, name, epilogue, ep_in, ep_in_kinds, ep_out, ep_out_kinds, rs_sends=()):
    g_n = a.shape[0] if a.ndim == 3 else None
    (m, k), n = a.shape[-2:], b.shape[-1]
    tps = seq // tm
    n_i = m // tm
    n_rs = len(rs_sends)
    rs_shapes = [r.shape for r in rs_sends]
    n_in, n_out = len(ep_in), len(ep_out)
    n_cols = n // MXU_WIDTH
    rc, cw = tm // EPILOGUE_CHUNKS, n // n_cols

    def prev(i):
        return jnp.maximum(i - 1, 0)

    def spec(kind):
        return {"tok": pl.BlockSpec((tm, n), lambda i: (prev(i), 0)),
                "vec": pl.BlockSpec((1, n), lambda i: (0, 0)),
                "mod": pl.BlockSpec((None, N_MOD, n), lambda i: (prev(i) // tps, 0, 0)),
                "seq": pl.BlockSpec((None, 1, n), lambda i: (prev(i) // tps, 0, 0)),
                "loss": pl.BlockSpec((1, LANES), lambda i: (0, 0))}[kind]

    def body(a_ref, b_ref, *rest):
        in_refs, rest = rest[:n_in], rest[n_in:]
        rs_src, rest = rest[:n_rs], rest[n_rs:]
        out_refs, rest = rest[:n_out], rest[n_out:]
        rs_dst, rest = rest[:n_rs], rest[n_rs:]
        fin = rest[0]
        i = pl.program_id(0)
        if n_rs:
            rs_start, rs_finish = _rs_phases(rs_shapes, rs_src, rs_dst, *rest[1:])
            pl.when(i == 0)(rs_start)

        def product(cols):
            if g_n is None:
                return _dot_nn(a_ref[...], b_ref[:, cols])
            p = _dot_nn(a_ref[0], b_ref[0, :, cols])
            for g in range(1, g_n):
                p = p + _dot_nn(a_ref[g], b_ref[g, :, cols])
            return p

        def step(with_epilogue, with_matmul):
            parts = []
            for c in range(EPILOGUE_CHUNKS):
                if with_epilogue:
                    rows = pl.ds(c * rc, rc)
                    epilogue(fin[rows, :], i - 1, tps, in_refs, out_refs, rows, c)
                while with_matmul and len(parts) < (c + 1) * n_cols // EPILOGUE_CHUNKS:
                    cols = slice(len(parts) * cw, (len(parts) + 1) * cw)
                    parts.append((cols, product(cols)))
            for cols, v in parts:
                fin[:, cols] = v

        pl.when(i == 0)(functools.partial(step, False, True))
        pl.when(jnp.logical_and(i > 0, i < n_i))(functools.partial(step, True, True))
        pl.when(i == n_i)(functools.partial(step, True, False))

        if n_rs:
            pl.when(i == n_i)(rs_finish)

    def row(i):
        return jnp.minimum(i, n_i - 1)

    if g_n is None:
        a_spec = pl.BlockSpec((tm, k), lambda i: (row(i), 0))
        b_spec = pl.BlockSpec(b.shape, lambda i: (0, 0), pipeline_mode=pl.Buffered(1))
    else:
        a_spec = pl.BlockSpec((g_n, tm, k), lambda i: (0, row(i), 0))
        b_spec = pl.BlockSpec(b.shape, lambda i: (0, 0, 0), pipeline_mode=pl.Buffered(1))
    any_spec = pl.BlockSpec(memory_space=pl.ANY)
    res = pl.pallas_call(
        body, name=name, grid=(n_i + 1,), out_shape=tuple(list(ep_out) + _rs_out(rs_sends)),
        in_specs=[a_spec, b_spec] + [spec(kd) for kd in ep_in_kinds] + [any_spec] * n_rs,
        out_specs=tuple([spec(kd) for kd in ep_out_kinds] + [any_spec] * n_rs),
        scratch_shapes=[pltpu.VMEM((tm, n), F32)] + (_rs_scratch(rs_sends) if n_rs else []),
        compiler_params=_params(),
    )(a, b, *ep_in, *rs_sends)
    return res


def _first(cond, chunk):
    return cond if chunk == 0 else False


def _mid_epilogue(mv, i, tps, in_refs, out_refs, rows, chunk):
    x_ref, gpost_ref, gpre_ref, mod_ref = in_refs
    mix_ref, x1_ref, h2_ref = out_refs
    mix_ref[rows, :] = mv
    x1 = x_ref[rows, :] + mod_ref[2:3, :] * (mv * _rms(mv) * gpost_ref[...])
    x1_ref[rows, :] = x1
    n = x1 * _rms(x1) * gpre_ref[...]
    h2_ref[rows, :] = (n * (1.0 + mod_ref[4:5, :]) + mod_ref[3:4, :]).astype(BF16)


def _post_epilogue(fv, i, tps, in_refs, out_refs, rows, chunk):
    x1_ref, tgt_ref, g_ref, mod_ref = in_refs
    loss_ref, dy_ref, df_ref, dgate_ref, gg_ref = out_refs
    d = fv.shape[1]
    r = _rms(fv)
    fh = fv * r
    nf = fh * g_ref[...]
    gate = mod_ref[5:6, :]
    err = x1_ref[rows, :] + gate * nf - tgt_ref[rows, :]
    _acc(loss_ref, jnp.sum(_colsum(err * err), axis=1, keepdims=True) * jnp.ones((1, LANES), F32),
         _first(i == 0, chunk))
    dy = err * (1.0 / d)
    dy_ref[rows, :] = dy
    _acc(dgate_ref, _colsum(dy * nf), _first(i % tps == 0, chunk))
    dn = dy * gate
    _acc(gg_ref, _colsum(dn * fh), _first(i == 0, chunk))
    df_ref[rows, :] = _rms_bwd(dn * g_ref[...], fh, r).astype(BF16)


def _bwd_mid_epilogue(dh, i, tps, in_refs, out_refs, rows, chunk):
    dy_ref, x1_ref, mix_ref, gpre_ref, gpost_ref, mod_ref = in_refs
    dx1_ref, dmix_ref, dshift_ref, dscale_ref, dgate_ref, ggpre_ref, ggpost_ref = out_refs
    seq_first, first = _first(i % tps == 0, chunk), _first(i == 0, chunk)
    x1 = x1_ref[rows, :]
    r = _rms(x1)
    xh = x1 * r
    gpre = gpre_ref[...]
    _acc(dshift_ref, _colsum(dh), seq_first)
    _acc(dscale_ref, _colsum(dh * xh * gpre), seq_first)
    dn = dh * (1.0 + mod_ref[4:5, :])
    _acc(ggpre_ref, _colsum(dn * xh), first)
    dx1 = dy_ref[rows, :] + _rms_bwd(dn * gpre, xh, r)
    dx1_ref[rows, :] = dx1
    mv = mix_ref[rows, :]
    rm = _rms(mv)
    mh = mv * rm
    gpost = gpost_ref[...]
    _acc(dgate_ref, _colsum(dx1 * mh * gpost), seq_first)
    dnm = dx1 * mod_ref[2:3, :]
    _acc(ggpost_ref, _colsum(dnm * mh), first)
    dmix_ref[rows, :] = _rms_bwd(dnm * gpost, mh, rm).astype(BF16)


def _bwd_pre_epilogue(dh, i, tps, in_refs, out_refs, rows, chunk):
    dx1_ref, x_ref, g_ref, mod_ref = in_refs
    gx_ref, dshift_ref, dscale_ref, gg_ref = out_refs
    seq_first = _first(i % tps == 0, chunk)
    xv = x_ref[rows, :]
    r = _rms(xv)
    xh = xv * r
    g = g_ref[...]
    _acc(dshift_ref, _colsum(dh), seq_first)
    _acc(dscale_ref, _colsum(dh * xh * g), seq_first)
    dn = dh * (1.0 + mod_ref[1:2, :])
    _acc(gg_ref, _colsum(dn * xh), _first(i == 0, chunk))
    gx_ref[rows, :] = dx1_ref[rows, :] + _rms_bwd(dn * g, xh, r)


def _ffn_up(h2, wgu, tm, tn):
    t, d = h2.shape
    f = wgu.shape[1]

    def body(h_ref, w_ref, gu_ref, act_ref):
        h = h_ref[...]
        g = _dot_nt(h, w_ref[0])
        u = _dot_nt(h, w_ref[1])
        gu_ref[0] = g.astype(BF16)
        gu_ref[1] = u.astype(BF16)
        act_ref[...] = (g * jax.nn.sigmoid(g) * u).astype(BF16)

    return pl.pallas_call(
        body, name="ffn_up", grid=(f // tn, t // tm),
        out_shape=(jax.ShapeDtypeStruct((2, t, f), BF16), jax.ShapeDtypeStruct((t, f), BF16)),
        in_specs=[pl.BlockSpec((tm, d), lambda j, i: (i, 0)), pl.BlockSpec((2, tn, d), lambda j, i: (0, j, 0))],
        out_specs=(pl.BlockSpec((2, tm, tn), lambda j, i: (0, i, j)), pl.BlockSpec((tm, tn), lambda j, i: (i, j))),
        compiler_params=_params(),
    )(h2, wgu)


def _ffn_act_bwd(df, wd, gu, tm, tn):
    t, d = df.shape
    f = wd.shape[0]

    def body(df_ref, w_ref, gu_ref, dgu_ref):
        da = _dot_nt(df_ref[...], w_ref[...])
        g = gu_ref[0].astype(F32)
        u = gu_ref[1].astype(F32)
        s = jax.nn.sigmoid(g)
        silu = g * s
        dgu_ref[0] = (da * u * (s + silu * (1.0 - s))).astype(BF16)
        dgu_ref[1] = (da * silu).astype(BF16)

    return pl.pallas_call(
        body, name="ffn_act_bwd", grid=(f // tn, t // tm),
        out_shape=jax.ShapeDtypeStruct((2, t, f), BF16),
        in_specs=[pl.BlockSpec((tm, d), lambda j, i: (i, 0)), pl.BlockSpec((tn, d), lambda j, i: (j, 0)),
                  pl.BlockSpec((2, tm, tn), lambda j, i: (0, i, j))],
        out_specs=pl.BlockSpec((2, tm, tn), lambda j, i: (0, i, j)),
        compiler_params=_params(),
    )(df, wd, gu)


SIGN_BIT = 0x80000000
Q_SCALE = 1.0 / math.sqrt(HEAD_DIM)


def _softplus(z):
    neg_abs = lax.bitcast_convert_type(lax.bitcast_convert_type(z, jnp.uint32) | jnp.uint32(SIGN_BIT), F32)
    return jnp.maximum(z, 0.0) + jnp.log(1.0 + jnp.exp(neg_abs))


def _hi_lo(v):
    hi = v.astype(BF16)
    return jnp.concatenate([hi, (v - hi.astype(F32)).astype(BF16)], axis=1)


def _emit_skewed(chains, lag=1):
    for t in range(max(len(ch) for ch in chains) + lag * (len(chains) - 1)):
        for c, ch in enumerate(chains):
            if 0 <= t - lag * c < len(ch):
                ch[t - lag * c]()


def _fwd_chain(blk, qs, k_ref, v_ref, c0, kb, cols, mask, ntri, lane, tq):
    st = {}

    def scores():
        st["z"] = _dot_nt(qs, k_ref[pl.ds(c0, tq), cols])

    def soft():
        sp = _softplus(st["z"])
        if mask is not None:
            sp = jnp.where(mask, sp, 0.0)
        st["parts"] = _hi_lo(sp)
        st["cur"] = blk["cur"]
        blk["cm"] = jnp.where(lane == kb, blk["cur"], blk["cm"])
        blk["cur"] = blk["cur"] - jnp.sum(sp, axis=1, keepdims=True)

    def sums():
        st["s"] = _dot_nn(st["parts"], ntri)

    def weights():
        w = jnp.exp(st["z"] + st["s"] + st["cur"])
        if mask is not None:
            w = jnp.where(mask, w, 0.0)
        st["w"] = w.astype(BF16)

    def out():
        p = _dot_nn(st["w"], v_ref[pl.ds(c0, tq), cols])
        blk["pv"] = p if blk["pv"] is None else blk["pv"] + p

    return [scores, soft, sums, weights, out]


def _bwd_chain(blk, qs, dos, cs, k_ref, v_ref, dk_ref, dv_ref, c0, kb, cols, mask, ntri, tri_i, lane, tq):
    st = {}

    def scores():
        st["z"] = _dot_nt(qs, k_ref[pl.ds(c0, tq), cols])
        st["dw"] = _dot_nt(dos, v_ref[pl.ds(c0, tq), cols])

    def soft():
        sp = _softplus(st["z"])
        if mask is not None:
            sp = jnp.where(mask, sp, 0.0)
        st["sp"] = sp
        st["parts"] = _hi_lo(sp)
        st["cur"] = jnp.sum(jnp.where(lane == kb, cs, 0.0), axis=1, keepdims=True)

    def sums():
        st["s"] = _dot_nn(st["parts"], ntri)

    def weights():
        w = jnp.exp(st["z"] + st["s"] + st["cur"])
        if mask is not None:
            w = jnp.where(mask, w, 0.0)
        ee = w * st["dw"]
        st["w"], st["ee"], st["ec"] = w.astype(BF16), ee, blk["ec"]
        blk["ec"] = blk["ec"] + jnp.sum(ee, axis=1, keepdims=True)

    def prefix():
        st["einc"] = _dot_nn(st["ee"].astype(BF16), tri_i)

    def dz():
        v = st["ee"] - jnp.exp(st["z"] - st["sp"]) * (st["einc"] + st["ec"])
        if mask is not None:
            v = jnp.where(mask, v, 0.0)
        st["dz"] = v.astype(BF16)

    def grads():
        p = _dot_nn(st["dz"], k_ref[pl.ds(c0, tq), cols])
        blk["dq"] = p if blk["dq"] is None else blk["dq"] + p
        dk_ref[pl.ds(c0, tq), :] += _dot_tn(st["dz"], qs)
        dv_ref[pl.ds(c0, tq), :] += _dot_tn(st["w"], dos)

    return [scores, soft, sums, weights, prefix, dz, grads]


def _stack_heads(v, lane, scale=None):
    if scale is not None:
        v = v * jnp.asarray(scale, v.dtype)
    zero = jnp.zeros_like(v)
    return jnp.concatenate([jnp.where(lane < HEAD_DIM, v, zero), jnp.where(lane >= HEAD_DIM, v, zero)], axis=0)


def _diag_mask(tq):
    row = lax.broadcasted_iota(jnp.int32, (2 * tq, tq), 0)
    col = lax.broadcasted_iota(jnp.int32, (2 * tq, tq), 1)
    return col < jnp.where(row >= tq, row - tq, row)


def _attn_fwd(proj, tri_after, n_seq, seq, ag_srcs, ag_out_shapes, ag_dests):
    t = proj.shape[0]
    tq = ATT_TILE
    npp = ATT_PAIRS
    n_blk = (proj.shape[1] // 4) // (npp * LANES)
    n_ag, n_ag_out = len(ag_srcs), len(ag_out_shapes)
    n_steps = n_seq * n_blk

    def body(q_ref, k_ref, v_ref, tri_ref, *rest):
        ag_src, rest = rest[:n_ag], rest[n_ag:]
        o_ref, cs_ref = rest[:2]
        ag_out, rest = rest[2:2 + n_ag_out], rest[2 + n_ag_out:]
        oacc, cmat, carry = rest[:3]
        ag_start, ag_forward, ag_finish = _ag_phases(ag_dests, ag_src, ag_out, *rest[3:])
        step = pl.program_id(0) * n_blk + pl.program_id(1)
        pl.when(step == 0)(ag_start)
        pl.when(step == (3 * n_steps) // 4)(ag_forward)
        lane = lax.broadcasted_iota(jnp.int32, (1, LANES), 1)
        ntri = tri_ref[...]
        diag = _diag_mask(tq)

        def q_tile(qi, _):
            r0 = pl.multiple_of(qi * tq, tq)
            qs = [_stack_heads(q_ref[pl.ds(r0, tq), pp * LANES:(pp + 1) * LANES], lane, Q_SCALE)
                  for pp in range(npp)]
            carry[...] = jnp.zeros_like(carry)
            cmat[...] = jnp.zeros_like(cmat)
            oacc[...] = jnp.zeros_like(oacc)

            def run_tiles(tiles):
                blocks = [dict(cur=carry[pp], cm=cmat[pp], pv=None) for pp in range(npp)]
                chains = []
                for kb, mask in tiles:
                    c0 = pl.multiple_of(kb * tq, tq)
                    for pp in range(npp):
                        chains.append(_fwd_chain(blocks[pp], qs[pp], k_ref, v_ref, c0, kb,
                                                 slice(pp * LANES, (pp + 1) * LANES), mask, ntri, lane, tq))
                _emit_skewed(chains)
                for pp in range(npp):
                    oacc[pp] += blocks[pp]["pv"]
                    cmat[pp] = blocks[pp]["cm"]
                    carry[pp] = blocks[pp]["cur"]

            odd = qi % 2

            @pl.when(odd == 0)
            def _():
                run_tiles([(qi, diag)])

            @pl.when(odd == 1)
            def _():
                run_tiles([(qi, diag), (qi - 1, None)])

            def pair(j, _):
                kb = qi - 1 - odd - 2 * j
                run_tiles([(kb, None), (kb - 1, None)])
                return 0

            lax.fori_loop(0, qi // 2, pair, 0)
            for pp in range(npp):
                c_off = 2 * pp * LANES
                cs_ref[pl.ds(r0, tq), c_off:c_off + LANES] = cmat[pp, 0:tq, :]
                cs_ref[pl.ds(r0, tq), c_off + LANES:c_off + 2 * LANES] = cmat[pp, tq:2 * tq, :]
                o_ref[pl.ds(r0, tq), pp * LANES:(pp + 1) * LANES] = jnp.where(
                    lane < HEAD_DIM, oacc[pp, 0:tq, :], oacc[pp, tq:2 * tq, :]).astype(BF16)
            return 0

        lax.fori_loop(0, seq // tq, q_tile, 0)
        pl.when(step == n_steps - 1)(ag_finish)

    wid = npp * LANES
    blk = lambda off: pl.BlockSpec((seq, wid), lambda b, p: (b, off + p))
    any_spec = pl.BlockSpec(memory_space=pl.ANY)
    return pl.pallas_call(
        body, name="attn_fwd", grid=(n_seq, n_blk),
        out_shape=(jax.ShapeDtypeStruct((2, t, n_blk * wid), BF16),
                   jax.ShapeDtypeStruct((t, n_blk * 2 * wid), F32), *ag_out_shapes),
        in_specs=[blk(0), blk(n_blk), blk(2 * n_blk), pl.BlockSpec((2 * tq, tq), lambda b, p: (0, 0))]
        + [any_spec] * n_ag,
        out_specs=(pl.BlockSpec((None, seq, wid), lambda b, p: (0, b, p)),
                   pl.BlockSpec((seq, 2 * wid), lambda b, p: (b, p)), *([any_spec] * n_ag_out)),
        scratch_shapes=[pltpu.VMEM((npp, 2 * tq, LANES), F32), pltpu.VMEM((npp, 2 * tq, LANES), F32),
                        pltpu.VMEM((npp, 2 * tq, 1), F32)] + _ag_scratch(n_ag),
        compiler_params=_params(),
    )(proj, proj, proj, tri_after, *ag_srcs)


def _attn_bwd(proj, dcat, cstats, tri_after, tri_incl, n_seq, seq, rs_sends):
    t = proj.shape[0]
    tq = ATT_TILE
    npp = ATT_PAIRS
    width = proj.shape[1] // 4
    n_blk = width // (npp * LANES)
    n_rs = len(rs_sends)
    rs_shapes = [r.shape for r in rs_sends]
    n_steps = n_seq * n_blk

    def body(q_ref, k_ref, v_ref, do_ref, cs_ref, tria_ref, trii_ref, *rest):
        rs_src, rest = rest[:n_rs], rest[n_rs:]
        out_ref = rest[0]
        rs_dst, rest = rest[1:1 + n_rs], rest[1 + n_rs:]
        dq_acc, dk_acc, dv_acc, ecarry = rest[:4]
        rs_start, rs_finish = _rs_phases(rs_shapes, rs_src, rs_dst, *rest[4:])
        step = pl.program_id(0) * n_blk + pl.program_id(1)
        pl.when(step == 0)(rs_start)
        lane = lax.broadcasted_iota(jnp.int32, (1, LANES), 1)
        ntri = tria_ref[...]
        tri_i = trii_ref[...]
        diag = _diag_mask(tq)
        dk_acc[...] = jnp.zeros_like(dk_acc)
        dv_acc[...] = jnp.zeros_like(dv_acc)

        def q_tile(qi, _):
            r0 = pl.multiple_of(qi * tq, tq)
            qs, dos, cs = [], [], []
            for pp in range(npp):
                cols = slice(pp * LANES, (pp + 1) * LANES)
                qs.append(_stack_heads(q_ref[pl.ds(r0, tq), cols], lane, Q_SCALE))
                dos.append(_stack_heads(do_ref[pl.ds(r0, tq), cols], lane))
                c_off = 2 * pp * LANES
                cs.append(jnp.concatenate([cs_ref[pl.ds(r0, tq), c_off:c_off + LANES],
                                           cs_ref[pl.ds(r0, tq), c_off + LANES:c_off + 2 * LANES]], axis=0))
            ecarry[...] = jnp.zeros_like(ecarry)
            dq_acc[...] = jnp.zeros_like(dq_acc)

            def run_tiles(tiles):
                blocks = [dict(ec=ecarry[pp], dq=None) for pp in range(npp)]
                chains = []
                for kb, mask in tiles:
                    c0 = pl.multiple_of(kb * tq, tq)
                    for pp in range(npp):
                        chains.append(_bwd_chain(
                            blocks[pp], qs[pp], dos[pp], cs[pp], k_ref, v_ref, dk_acc.at[pp], dv_acc.at[pp],
                            c0, kb, slice(pp * LANES, (pp + 1) * LANES), mask, ntri, tri_i, lane, tq))
                _emit_skewed(chains)
                for pp in range(npp):
                    dq_acc[pp] += blocks[pp]["dq"]
                    ecarry[pp] = blocks[pp]["ec"]

            def pair(j, _):
                run_tiles([(2 * j, None), (2 * j + 1, None)])
                return 0

            lax.fori_loop(0, qi // 2, pair, 0)
            odd = qi % 2

            @pl.when(odd == 0)
            def _():
                run_tiles([(qi, diag)])

            @pl.when(odd == 1)
            def _():
                run_tiles([(qi - 1, None), (qi, diag)])

            for pp in range(npp):
                dq = jnp.where(lane < HEAD_DIM, dq_acc[pp, 0:tq, :], dq_acc[pp, tq:2 * tq, :])
                out_ref[0, pl.ds(r0, tq), pp * LANES:(pp + 1) * LANES] = (dq * Q_SCALE).astype(BF16)
            return 0

        lax.fori_loop(0, seq // tq, q_tile, 0)
        for pp in range(npp):
            cols = slice(pp * LANES, (pp + 1) * LANES)
            out_ref[1, :, cols] = dk_acc[pp].astype(BF16)
            out_ref[2, :, cols] = dv_acc[pp].astype(BF16)
        pl.when(step == n_steps - 1)(rs_finish)

    wid = npp * LANES
    blk = lambda off: pl.BlockSpec((seq, wid), lambda b, p: (b, off + p))
    tri_spec = pl.BlockSpec((2 * tq, tq), lambda b, p: (0, 0))
    any_spec = pl.BlockSpec(memory_space=pl.ANY)
    return pl.pallas_call(
        body, name="attn_bwd", grid=(n_seq, n_blk),
        out_shape=(jax.ShapeDtypeStruct((4, t, width), BF16), *_rs_out(rs_sends)),
        in_specs=[blk(0), blk(n_blk), blk(2 * n_blk), pl.BlockSpec((seq, wid), lambda b, p: (b, p)),
                  pl.BlockSpec((seq, 2 * wid), lambda b, p: (b, p)), tri_spec,
                  pl.BlockSpec((tq, tq), lambda b, p: (0, 0))] + [any_spec] * n_rs,
        out_specs=(pl.BlockSpec((3, seq, wid), lambda b, p: (0, b, p)), *([any_spec] * n_rs)),
        scratch_shapes=[pltpu.VMEM((npp, 2 * tq, LANES), F32), pltpu.VMEM((npp, seq, LANES), F32),
                        pltpu.VMEM((npp, seq, LANES), F32), pltpu.VMEM((npp, 2 * tq, 1), F32)]
        + _rs_scratch(rs_sends),
        compiler_params=_params(),
    )(proj, proj, proj, dcat, cstats, tri_after, tri_incl, *rs_sends)


def _window_sum(v, g, rows, forward):
    s_len = v.shape[0]
    s = v
    for step in range(g + 1):
        sh = 1 << step
        if forward:
            s = s + jnp.where(rows < s_len - sh, pltpu.roll(s, s_len - sh, axis=0), 0.0)
        else:
            s = s + jnp.where(rows >= sh, pltpu.roll(s, sh, axis=0), 0.0)
    return s


def _window_count(g, rows):
    return jnp.minimum(rows + 1, POOL_WINDOWS[g]).astype(F32)


def _pooled(u, g, rows):
    return _window_sum(u, g, rows, forward=False) / _window_count(g, rows) - u


def _group_cols(g):
    return slice(g * POOL_GROUP_DIM, (g + 1) * POOL_GROUP_DIM)


def _pool_fwd(proj, w_pool, pool_scale, cat, n_seq, seq):
    n_grp = len(POOL_WINDOWS)
    width = n_grp * POOL_GROUP_DIM
    assert [1 << (g + 1) for g in range(n_grp)] == list(POOL_WINDOWS)

    def body(u_ref, w_ref, s_ref, alias_ref, o_ref):
        del alias_ref
        rows = lax.broadcasted_iota(jnp.int32, (seq, 1), 0)
        for g in range(n_grp):
            cols = _group_cols(g)
            pooled = _pooled(u_ref[:, cols].astype(F32), g, rows)
            y = _dot_nn(pooled.astype(BF16), w_ref[g].astype(BF16))
            o_ref[:, cols] = (y * s_ref[:, cols]).astype(BF16)

    return pl.pallas_call(
        body, name="pool_fwd", grid=(n_seq,),
        out_shape=jax.ShapeDtypeStruct(cat.shape, BF16),
        in_specs=[pl.BlockSpec((seq, width), lambda b: (b, 3)),
                  pl.BlockSpec((n_grp, POOL_GROUP_DIM, POOL_GROUP_DIM), lambda b: (0, 0, 0)),
                  pl.BlockSpec((1, width), lambda b: (0, 0)),
                  pl.BlockSpec(memory_space=pl.ANY)],
        out_specs=pl.BlockSpec((None, seq, width), lambda b: (1, b, 0)),
        input_output_aliases={3: 0},
        compiler_params=_params(),
    )(proj, w_pool, pool_scale, cat)


def _pool_bwd(proj, dcat, w_pool, pool_scale, dqkv, n_seq, seq):
    n_grp = len(POOL_WINDOWS)
    width = n_grp * POOL_GROUP_DIM

    def body(u_ref, dp_ref, w_ref, s_ref, alias_ref, du_ref, gw_ref, gs_ref):
        del alias_ref
        b = pl.program_id(0)
        rows = lax.broadcasted_iota(jnp.int32, (seq, 1), 0)
        for g in range(n_grp):
            cols = _group_cols(g)
            pb = _pooled(u_ref[:, cols].astype(F32), g, rows).astype(BF16)
            wb = w_ref[g].astype(BF16)
            z = _dot_nn(pb, wb)
            dp = dp_ref[:, cols].astype(F32)
            _acc(gs_ref.at[:, cols], _colsum(dp * z), b == 0)
            dys = (dp * s_ref[:, cols]).astype(BF16)
            _acc(gw_ref.at[g], _dot_tn(pb, dys), b == 0)
            dpooled = _dot_nt(dys, wb)
            du = _window_sum(dpooled / _window_count(g, rows), g, rows, forward=True) - dpooled
            du_ref[:, cols] = du.astype(BF16)

    return pl.pallas_call(
        body, name="pool_bwd", grid=(n_seq,),
        out_shape=(jax.ShapeDtypeStruct(dqkv.shape, BF16),
                   jax.ShapeDtypeStruct((n_grp, POOL_GROUP_DIM, POOL_GROUP_DIM), F32),
                   jax.ShapeDtypeStruct((1, width), F32)),
        in_specs=[pl.BlockSpec((seq, width), lambda b: (b, 3)),
                  pl.BlockSpec((seq, width), lambda b: (b, 1)),
                  pl.BlockSpec((n_grp, POOL_GROUP_DIM, POOL_GROUP_DIM), lambda b: (0, 0, 0)),
                  pl.BlockSpec((1, width), lambda b: (0, 0)),
                  pl.BlockSpec(memory_space=pl.ANY)],
        out_specs=(pl.BlockSpec((None, seq, width), lambda b: (3, b, 0)),
                   pl.BlockSpec((n_grp, POOL_GROUP_DIM, POOL_GROUP_DIM), lambda b: (0, 0, 0)),
                   pl.BlockSpec((1, width), lambda b: (0, 0))),
        input_output_aliases={4: 0},
        compiler_params=_params(),
    )(proj, dcat, w_pool, pool_scale, dqkv)


def _cond_fwd(c_all, w_cond, b_cols):
    n, _ = c_all.shape
    cols = w_cond.shape[1]

    def body(c_ref, w_ref, b_ref, o_ref):
        cv = c_ref[...]
        a = cv * jax.nn.sigmoid(cv)
        o_ref[...] = jnp.dot(a, w_ref[...], preferred_element_type=F32,
                             precision=lax.Precision.HIGHEST) + b_ref[...]

    return pl.pallas_call(
        body, name="cond_fwd", out_shape=jax.ShapeDtypeStruct((n, cols), F32),
        compiler_params=_params(),
    )(c_all, w_cond, b_cols)


def _cond_bwd_adamw(c_all, dmod_all, dmod_cols, w, m_w, v_w, b, m_b, v_b):
    def body(c_ref, dm_ref, dmc_ref, w_ref, mw_ref, vw_ref, b_ref, mb_ref, vb_ref,
             gw_ref, dw_ref, nmw_ref, nvw_ref, gb_ref, db_ref, nmb_ref, nvb_ref):
        cv = c_ref[...]
        a = cv * jax.nn.sigmoid(cv)
        gw = lax.dot_general(a, dmc_ref[...], (((0,), (0,)), ((), ())),
                             preferred_element_type=F32, precision=lax.Precision.HIGHEST)
        gw_ref[...] = gw
        dw_ref[...], nmw_ref[...], nvw_ref[...] = _adamw_math(w_ref[...], gw, mw_ref[...], vw_ref[...])
        gb = _colsum(dm_ref[...])
        gb_ref[...] = gb
        db_ref[...], nmb_ref[...], nvb_ref[...] = _adamw_math(b_ref[...], gb, mb_ref[...], vb_ref[...])

    w_sds, b_sds = jax.ShapeDtypeStruct(w.shape, F32), jax.ShapeDtypeStruct(b.shape, F32)
    outs = pl.pallas_call(
        body, name="cond_bwd_adamw", out_shape=(w_sds,) * 4 + (b_sds,) * 4, compiler_params=_params(),
    )(c_all, dmod_all, dmod_cols, w, m_w, v_w, b, m_b, v_b)
    return outs[:4], outs[4:]


def _adamw_math(w, g, m, v):
    m = ADAM_B1 * m + (1.0 - ADAM_B1) * g
    v = ADAM_B2 * v + (1.0 - ADAM_B2) * (g * g)
    m_hat = m / (1.0 - ADAM_B1 ** ADAM_STEP)
    v_hat = v / (1.0 - ADAM_B2 ** ADAM_STEP)
    delta = -ADAM_LR * (m_hat / (jnp.sqrt(v_hat) + ADAM_EPS) + ADAM_WD * w)
    return delta, m, v


def _adamw_small(ws, gparts, ms, vs, name):
    n = len(ws)

    def body(*refs):
        w_r, g_r, m_r, v_r = refs[:n], refs[n:2 * n], refs[2 * n:3 * n], refs[3 * n:4 * n]
        outs = refs[4 * n:]
        for i in range(n):
            g = g_r[i][0]
            for dev in range(1, g_r[i].shape[0]):
                g = g + g_r[i][dev]
            delta, m, v = _adamw_math(w_r[i][...], g, m_r[i][...], v_r[i][...])
            outs[i][...] = g
            outs[n + i][...] = delta
            outs[2 * n + i][...] = m
            outs[3 * n + i][...] = v

    sds = [jax.ShapeDtypeStruct(w.shape, F32) for w in ws]
    return pl.pallas_call(
        body, name=name, out_shape=tuple(sds * 4), compiler_params=_params(),
    )(*ws, *gparts, *ms, *vs)


def kernel(x, c, w_cond, b_cond, g_mix_pre, g_mix_post, w_in, w_pool, pool_scale, w_out, g_ffn_pre, g_ffn_post, w_gate, w_up, w_down, loss_target, m_w_cond, m_b_cond, m_g_mix_pre, m_g_mix_post, m_w_in, m_w_pool, m_pool_scale, m_w_out, m_g_ffn_pre, m_g_ffn_post, m_w_gate, m_w_up, m_w_down, v_w_cond, v_b_cond, v_g_mix_pre, v_g_mix_post, v_w_in, v_w_pool, v_pool_scale, v_w_out, v_g_ffn_pre, v_g_ffn_post, v_w_gate, v_w_up, v_w_down):
    n_seq, seq, d = x.shape
    t = n_seq * seq
    xi, yi, ci = _mesh_pos()
    me = 4 * xi + 2 * yi + ci
    x2 = x.reshape(t, d)
    tgt2 = loss_target.reshape(t, d)
    in_rows = w_in.shape[2]
    out_rows = w_out.shape[1]
    ff_rows = w_gate.shape[2]
    ff = N_DEV * ff_rows
    cond_cols = w_cond.shape[2]

    win_t = w_in[0].T.astype(BF16)
    wout_s = w_out[0].astype(BF16)
    wg_t = w_gate[0].T.astype(BF16)
    wu_t = w_up[0].T.astype(BF16)
    wd_s = w_down[0].astype(BF16)
    c_all = _all_gather(c, "ag_c").reshape(N_DEV * n_seq, d)

    b_cols = lax.dynamic_slice_in_dim(b_cond, me * cond_cols, cond_cols, axis=1)
    mod_cols = _cond_fwd(c_all, w_cond[0], b_cols)
    mod_g = _all_gather(mod_cols, "ag_mod")
    mod_mine = lax.dynamic_slice_in_dim(mod_g, me * n_seq, n_seq, axis=1)
    mod = jnp.transpose(mod_mine, (1, 0, 2)).reshape(n_seq, N_MOD, d)

    h1, win_g = _pre_mix(x2, g_mix_pre, mod, seq, [win_t],
                         [jax.ShapeDtypeStruct((N_DEV, in_rows, d), BF16)], [(0, ())])
    win_full = win_g.reshape(N_DEV * in_rows, d)
    proj = _matmul(h1, win_full, "nt", BF16, 512, N_DEV * in_rows, d, "proj")
    tq = ATT_TILE
    ids = jnp.arange(tq)
    tri_after = jnp.tile(-(ids[:, None] >= ids[None, :]).astype(BF16), (2, 1))
    tri_incl = (ids[:, None] <= ids[None, :]).astype(BF16)
    attn, cstats, wout_g, wgu_g, wd_g = _attn_fwd(
        proj, tri_after, n_seq, seq, [wout_s, wg_t, wu_t, wd_s],
        [jax.ShapeDtypeStruct((N_DEV, out_rows, d), BF16), jax.ShapeDtypeStruct((2, N_DEV, ff_rows, d), BF16),
         jax.ShapeDtypeStruct((N_DEV, ff_rows, d), BF16)],
        [(0, ()), (1, (0,)), (1, (1,)), (2, ())])
    wout_full = wout_g.reshape(N_DEV * out_rows, d)
    wgu_full = wgu_g.reshape(2, ff, d)
    wd_full = wd_g.reshape(ff, d)
    cat = _pool_fwd(proj, w_pool[0], pool_scale, attn, n_seq, seq)
    tok_f32, tok_bf16 = jax.ShapeDtypeStruct((t, d), F32), jax.ShapeDtypeStruct((t, d), BF16)
    seq_sds, vec_sds = jax.ShapeDtypeStruct((n_seq, 1, d), F32), jax.ShapeDtypeStruct((1, d), F32)
    mix, x1, h2 = _matmul_rows(
        cat, wout_full.reshape(2, d // 2, d), ROW_TILE, seq, "mix_mid", _mid_epilogue,
        [x2, g_mix_post, g_ffn_pre, mod], ["tok", "vec", "vec", "mod"],
        [tok_f32, tok_f32, tok_bf16], ["tok", "tok", "tok"])
    gu, act = _ffn_up(h2, wgu_full, 512, ff // 2)
    loss_sum, dy, df, dgate_f, gg_ffn_post = _matmul_rows(
        act, wd_full, ROW_TILE, seq, "ffn_down_post", _post_epilogue,
        [x1, tgt2, g_ffn_post, mod], ["tok", "tok", "vec", "mod"],
        [jax.ShapeDtypeStruct((1, LANES), F32), tok_f32, tok_bf16, seq_sds, vec_sds],
        ["loss", "tok", "tok", "seq", "vec"])

    dgu = _ffn_act_bwd(df, wd_full, gu, 512, ff // 2)
    gwd, gwd_b = _matmul(act, df, "tn", F32, ff // 2, d // 2, t, "grad_w_down", bf16_copy=True)
    gwgu, gwgu_b = _matmul(dgu, h2, "tn", F32, ff // 2, d // 2, t, "grad_w_gate_up", bf16_copy=True)
    dx1, dmix, dshift_f, dscale_f, dgate_m, gg_ffn_pre, gg_mix_post = _matmul_rows(
        dgu, wgu_full, ROW_TILE, seq, "dh2_bwd_mid", _bwd_mid_epilogue,
        [dy, x1, mix, g_ffn_pre, g_mix_post, mod], ["tok", "tok", "tok", "vec", "vec", "mod"],
        [tok_f32, tok_bf16, seq_sds, seq_sds, seq_sds, vec_sds, vec_sds],
        ["tok", "tok", "seq", "seq", "seq", "vec", "vec"])
    dcat = _matmul(dmix, wout_full, "nt", BF16, 512, d, d, "dcat")
    gwout, gwout_b = _matmul(cat, dmix, "tn", F32, d // 2, d, t, "grad_w_out", bf16_copy=True)
    dqkv, rv_wgu, rv_wd, rv_wout = _attn_bwd(
        proj, dcat, cstats, tri_after, tri_incl, n_seq, seq,
        [gwgu_b.reshape(2, N_DEV, ff_rows, d), gwd_b.reshape(1, N_DEV, ff_rows, d),
         gwout_b.reshape(1, N_DEV, out_rows, d)])
    dproj, gw_pool, gs_pool = _pool_bwd(proj, dcat, w_pool[0], pool_scale, dqkv, n_seq, seq)
    pad_d = lambda v: jnp.pad(v, ((0, 0), (0, d - v.shape[1])))
    n_gw = gw_pool.size // d
    early = jnp.concatenate(
        [gg_mix_post, gg_ffn_pre, gg_ffn_post, pad_d(gs_pool), pad_d(loss_sum), jnp.zeros((3, d), F32),
         gw_pool.reshape(n_gw, d),
         jnp.concatenate([dgate_m, dshift_f, dscale_f, dgate_f], axis=1).reshape(n_seq * 4, d)], axis=0)
    gwin, gwin_b, early_g = _matmul(
        dproj, h1, "tn", F32, d // 2, d, t, "grad_w_in", bf16_copy=True,
        ag=([early], [jax.ShapeDtypeStruct((N_DEV,) + early.shape, F32)], [(0, ())]))
    grad_x, dshift_m, dscale_m, gg_mix_pre, rv_win = _matmul_rows(
        dproj, win_full.reshape(4, d // 2, d), ROW_TILE, seq, "dh1_bwd_pre", _bwd_pre_epilogue,
        [dx1, x2, g_mix_pre, mod], ["tok", "tok", "vec", "mod"],
        [tok_f32, seq_sds, seq_sds, vec_sds], ["tok", "seq", "seq", "vec"],
        rs_sends=[gwin_b.reshape(1, N_DEV, in_rows, d)])


    late = jnp.concatenate([gg_mix_pre, dshift_m.reshape(n_seq, d), dscale_m.reshape(n_seq, d),
                            jnp.zeros((8 - 1 - 2 * n_seq, d), F32)], axis=0)
    late_g = _all_gather(late, "ag_late")
    loss = jnp.sum(early_g[:, 4, 0]) * (0.5 / d)
    dmod_all = jnp.concatenate(
        [late_g[:, 1:1 + n_seq, None, :], late_g[:, 1 + n_seq:1 + 2 * n_seq, None, :],
         early_g[:, 8 + n_gw:, :].reshape(N_DEV, n_seq, 4, d)], axis=2).reshape(N_DEV * n_seq, N_MOD * d)
    dmod_cols = lax.dynamic_slice_in_dim(dmod_all, me * cond_cols, cond_cols, axis=1)
    o_cond, o_bcond = _cond_bwd_adamw(c_all, dmod_all, dmod_cols, w_cond[0], m_w_cond[0], v_w_cond[0],
                                      b_cond, m_b_cond, v_b_cond)
    o_cond = tuple(o[None] for o in o_cond)

    small_ws = [g_mix_pre, g_mix_post, g_ffn_pre, g_ffn_post, pool_scale, w_pool.reshape(-1, POOL_GROUP_DIM)]
    small_ms = [m_g_mix_pre, m_g_mix_post, m_g_ffn_pre, m_g_ffn_post, m_pool_scale, m_w_pool.reshape(-1, POOL_GROUP_DIM)]
    small_vs = [v_g_mix_pre, v_g_mix_post, v_g_ffn_pre, v_g_ffn_post, v_pool_scale, v_w_pool.reshape(-1, POOL_GROUP_DIM)]
    small_gparts = [late_g[:, 0:1, :], early_g[:, 0:1, :], early_g[:, 1:2, :], early_g[:, 2:3, :],
                    early_g[:, 3:4, :pool_scale.shape[1]],
                    early_g[:, 8:8 + n_gw, :].reshape(N_DEV, -1, POOL_GROUP_DIM)]
    so = _adamw_small(small_ws, small_gparts, small_ms, small_vs, "adamw_small")
    ns = len(small_ws)
    sg, sdl, sm, sv = so[:ns], so[ns:2 * ns], so[2 * ns:3 * ns], so[3 * ns:]
    pool_shape = w_pool.shape
    fix = lambda lst: [lst[0], lst[1], lst[2], lst[3], lst[4], lst[5].reshape(pool_shape)]
    sg, sdl, sm, sv = fix(sg), fix(sdl), fix(sm), fix(sv)


    def reduced(mine, recv, slab, w, m, v, name, transposed=False, transpose=False):
        turn = (lambda u: u.T) if transposed else (lambda u: u)
        outs = _rs_final_adamw(mine, recv, slab, turn(w[0]), turn(m[0]), turn(v[0]), name, transpose)
        return tuple(turn(o)[None] for o in outs)

    o_in = reduced(gwin.reshape(1, N_DEV, in_rows, d), rv_win, 0, w_in, m_w_in, v_w_in, "adamw_w_in",
                   transpose=True)
    o_out = reduced(gwout.reshape(1, N_DEV, out_rows, d), rv_wout, 0, w_out, m_w_out, v_w_out, "adamw_w_out")
    gwgu8 = gwgu.reshape(2, N_DEV, ff_rows, d)
    o_gate = reduced(gwgu8, rv_wgu, 0, w_gate, m_w_gate, v_w_gate, "adamw_w_gate", transposed=True)
    o_up = reduced(gwgu8, rv_wgu, 1, w_up, m_w_up, v_w_up, "adamw_w_up", transposed=True)
    o_down = reduced(gwd.reshape(1, N_DEV, ff_rows, d), rv_wd, 0, w_down, m_w_down, v_w_down, "adamw_w_down")

    def pick(k):
        small_k = [sg, sdl, sm, sv][k]
        return [o_cond[k], o_bcond[k], small_k[0], small_k[1], o_in[k], small_k[5], small_k[4], o_out[k],
                small_k[2], small_k[3], o_gate[k], o_up[k], o_down[k]]

    return (loss, _copy_tokens(grad_x, "grad_x_out").reshape(n_seq, seq, d), *pick(0), *pick(1), *pick(2), *pick(3))
```

```python
import functools
import math

import jax
import jax.numpy as jnp
from jax import lax
from jax.experimental import pallas as pl
from jax.experimental.pallas import tpu as pltpu

F32 = jnp.float32
BF16 = jnp.bfloat16
MESH = pl.DeviceIdType.MESH

N_DEV = 8
HEAD_DIM = 64
LANES = 128
POOL_WINDOWS = (2, 4, 8, 16)
POOL_GROUP_DIM = 128
N_MOD = 6
EPS = 1e-6
ATT_TILE = 256
ATT_PAIRS = 2
VMEM_LIMIT = 56 * 1024 * 1024
ADAMW_COL_TILE = 256

ADAM_LR = 0.001
ADAM_B1 = 0.9
ADAM_B2 = 0.999
ADAM_EPS = 1e-08
ADAM_WD = 0.01
ADAM_STEP = 10


def _params(**kw):
    return pltpu.CompilerParams(vmem_limit_bytes=VMEM_LIMIT, **kw)


def _dot_nn(a, b):
    return jnp.dot(a, b, preferred_element_type=F32)


def _dot_nt(a, b):
    return lax.dot_general(a, b, (((1,), (1,)), ((), ())), preferred_element_type=F32)


def _dot_tn(a, b):
    return lax.dot_general(a, b, (((0,), (0,)), ((), ())), preferred_element_type=F32)


def _mesh_pos():
    return lax.axis_index("x"), lax.axis_index("y"), lax.axis_index("c")


def _ag_phases(dests, src, outs, send_sems, recv_sems, local_sems):
    n = len(src)
    x, y, c = _mesh_pos()
    me, sibling = (x, y, c), (x, y, 1 - c)
    chips = [(1 - x, y), (x, 1 - y), (1 - x, 1 - y)]

    def slot(i, dev):
        oi, prefix = dests[i]
        px, py, pc = dev
        return outs[oi].at[prefix + (4 * px + 2 * py + pc,)]

    def copy(i, k, block, to, from_src=False):
        return pltpu.make_async_remote_copy(
            src_ref=src[i] if from_src else slot(i, block), dst_ref=slot(i, block),
            send_sem=send_sems.at[i, k], recv_sem=recv_sems.at[i, k],
            device_id=to, device_id_type=MESH)

    def mine(i):
        return pltpu.make_async_copy(src[i], slot(i, me), local_sems.at[i])

    def first(i):
        return [copy(i, 0, me, sibling, from_src=True)] + [
            copy(i, 1 + j, me, (*chip, c), from_src=True) for j, chip in enumerate(chips)]

    def passed(i, j):
        return copy(i, 4 + j, (*chips[j], c), sibling)

    def start():
        for i in range(n):
            mine(i).start()
        for i in range(n):
            for cp in first(i):
                cp.start()

    def forward():
        for j, chip in enumerate(chips):
            for i in range(n):
                copy(i, 1 + j, (*chip, c), me).wait_recv()
                passed(i, j).start()

    def finish():
        for i in range(n):
            copy(i, 0, sibling, me).wait_recv()
            for j, chip in enumerate(chips):
                copy(i, 4 + j, (*chip, 1 - c), me).wait_recv()
        for i in range(n):
            for cp in first(i) + [passed(i, j) for j in range(3)]:
                cp.wait_send()
            mine(i).wait()

    return start, forward, finish


def _ag_scratch(n):
    return [pltpu.SemaphoreType.DMA((n, 7)), pltpu.SemaphoreType.DMA((n, 7)), pltpu.SemaphoreType.DMA((n,))]


def _all_gather(src, name):
    def body(src_ref, out_ref, send_sems, recv_sems, local_sem):
        x, y, c = _mesh_pos()

        def copy(k, block):
            px, py, pc = x ^ (k >> 2), y ^ ((k >> 1) & 1), c ^ (k & 1)
            bx, by, bc = (x, y, c) if block == "mine" else (px, py, pc)
            return pltpu.make_async_remote_copy(
                src_ref=src_ref, dst_ref=out_ref.at[4 * bx + 2 * by + bc],
                send_sem=send_sems.at[k - 1], recv_sem=recv_sems.at[k - 1],
                device_id=(px, py, pc), device_id_type=MESH)

        local = pltpu.make_async_copy(src_ref, out_ref.at[4 * x + 2 * y + c], local_sem.at[0])
        local.start()
        for k in range(1, N_DEV):
            copy(k, "mine").start()
        for k in range(1, N_DEV):
            copy(k, "mine").wait_send()
        for k in range(1, N_DEV):
            copy(k, "theirs").wait_recv()
        local.wait()

    any_spec = pl.BlockSpec(memory_space=pl.ANY)
    return pl.pallas_call(
        body, name=name,
        out_shape=jax.ShapeDtypeStruct((N_DEV,) + src.shape, src.dtype),
        in_specs=[any_spec], out_specs=any_spec,
        scratch_shapes=[pltpu.SemaphoreType.DMA((N_DEV - 1,)), pltpu.SemaphoreType.DMA((N_DEV - 1,)),
                        pltpu.SemaphoreType.DMA((1,))],
    )(src)


def _rs_phases(shapes, src, dst, send_sems, recv_sems):
    x, y, c = _mesh_pos()

    def copies():
        out = []
        n = 0
        for i, shp in enumerate(shapes):
            for m in range(shp[0]):
                for k in range(1, N_DEV):
                    px, py, pc = x ^ (k >> 2), y ^ ((k >> 1) & 1), c ^ (k & 1)
                    out.append(pltpu.make_async_remote_copy(
                        src_ref=src[i].at[m, 4 * px + 2 * py + pc], dst_ref=dst[i].at[m, k - 1],
                        send_sem=send_sems.at[n], recv_sem=recv_sems.at[n],
                        device_id=(px, py, pc), device_id_type=MESH))
                    n += 1
        return out

    def start():
        for cp in copies():
            cp.start()

    def finish():
        for cp in copies():
            cp.wait_send()
        for cp in copies():
            cp.wait_recv()

    return start, finish


def _rs_out(sends):
    return [jax.ShapeDtypeStruct((s.shape[0], N_DEV - 1) + s.shape[2:], s.dtype) for s in sends]


def _rs_scratch(sends):
    total = sum((N_DEV - 1) * s.shape[0] for s in sends)
    return [pltpu.SemaphoreType.DMA((total,)), pltpu.SemaphoreType.DMA((total,))]


def _rs_final_adamw(mine, recv, slab, w, m, v, name, transpose=False):
    _, _, r, cdim = mine.shape
    tc = ADAMW_COL_TILE
    assert cdim % tc == 0 and w.shape == ((cdim, r) if transpose else (r, cdim)), (name, w.shape)
    x, y, c = _mesh_pos()
    me = jnp.reshape(4 * x + 2 * y + c, (1,)).astype(jnp.int32)

    def body(me_ref, p_ref, r_ref, w_ref, m_ref, v_ref, g_ref, d_ref, nm_ref, nv_ref):
        del me_ref
        g = p_ref[...]
        for k in range(N_DEV - 1):
            g = g + r_ref[k].astype(F32)
        if transpose:
            g = g.T
        g_ref[...] = g
        d_ref[...], nm_ref[...], nv_ref[...] = _adamw_math(w_ref[...], g, m_ref[...], v_ref[...])

    if transpose:
        w_spec = pl.BlockSpec((tc, r), lambda j, s: (j, 0))
    else:
        w_spec = pl.BlockSpec((r, tc), lambda j, s: (0, j))
    sds = jax.ShapeDtypeStruct(w.shape, F32)
    return pl.pallas_call(
        body, name=name, out_shape=(sds, sds, sds, sds),
        grid_spec=pltpu.PrefetchScalarGridSpec(
            num_scalar_prefetch=1, grid=(cdim // tc,),
            in_specs=[pl.BlockSpec((None, None, r, tc), lambda j, s: (slab, s[0], 0, j)),
                      pl.BlockSpec((None, N_DEV - 1, r, tc), lambda j, s: (slab, 0, 0, j)),
                      w_spec, w_spec, w_spec],
            out_specs=(w_spec, w_spec, w_spec, w_spec)),
        compiler_params=_params(),
    )(me, mine, recv, w, m, v)


def _matmul(a, b, mode, out_dtype, tm, tn, tk, name, bf16_copy=False, rs_sends=(), ag=None):
    ga = a.shape[0] if a.ndim == 3 else None
    gb = b.shape[0] if b.ndim == 3 else None
    a2, b2 = a.shape[-2:], b.shape[-2:]
    if mode == "nn":
        (m, k), n = a2, b2[1]
    elif mode == "nt":
        (m, k), n = a2, b2[0]
    else:
        (k, m), n = a2, b2[1]
    assert m % tm == 0 and n % tn == 0 and k % tk == 0, (name, m, n, k)
    nk = k // tk
    g_n = ga or 1
    batch_out = mode == "tn" and ga is not None
    n_red = nk if batch_out else nk * g_n
    dot = {"nn": _dot_nn, "nt": _dot_nt, "tn": _dot_tn}[mode]
    acc_in_out = out_dtype == F32

    n_rs = len(rs_sends)
    rs_shapes = [r.shape for r in rs_sends]
    ag_srcs, ag_out_shapes, ag_dests = ag if ag is not None else ((), (), ())
    n_ag, n_ag_out = len(ag_srcs), len(ag_out_shapes)
    n_out = 2 if bf16_copy else 1
    assert not bf16_copy or acc_in_out
    assert not (n_rs and n_ag)

    def body(a_ref, b_ref, *rest):
        rs_src, rest = rest[:n_rs], rest[n_rs:]
        ag_src, rest = rest[:n_ag], rest[n_ag:]
        o_ref = rest[0]
        copy_ref = rest[1] if bf16_copy else None
        rs_dst, rest = rest[n_out:n_out + n_rs], rest[n_out + n_rs:]
        ag_out, scratch = rest[:n_ag_out], rest[n_ag_out:]
        first = functools.reduce(jnp.logical_and, [pl.program_id(ax) == 0 for ax in range(4)])
        last = functools.reduce(jnp.logical_and, [pl.program_id(ax) == grid[ax] - 1 for ax in range(4)])
        if n_rs:
            rs_start, rs_finish = _rs_phases(rs_shapes, rs_src, rs_dst, *scratch[-2:])
            pl.when(first)(rs_start)
        if n_ag:
            ag_start, ag_forward, ag_finish = _ag_phases(ag_dests, ag_src, ag_out, *scratch[-3:])
            pl.when(first)(ag_start)
        p = dot(a_ref[...], b_ref[...])
        kk = pl.program_id(3) if batch_out else pl.program_id(2) * nk + pl.program_id(3)
        if n_red == 1:
            o_ref[...] = p.astype(out_dtype)
            if bf16_copy:
                copy_ref[...] = p.astype(BF16)
        else:
            acc = o_ref if acc_in_out else scratch[0]

            @pl.when(kk == 0)
            def _():
                acc[...] = p

            @pl.when(kk > 0)
            def _():
                acc[...] += p

            @pl.when(kk == n_red - 1)
            def _():
                if not acc_in_out:
                    o_ref[...] = acc[...].astype(out_dtype)
                if bf16_copy:
                    copy_ref[...] = acc[...].astype(BF16)

        if n_rs:
            pl.when(last)(rs_finish)
        if n_ag:
            @pl.when(last)
            def _():
                ag_forward()
                ag_finish()

    def order(ids):
        return ids if batch_out else (ids[2], ids[0], ids[1], ids[3])

    def a_idx(*ids):
        g, i, j, kq = order(ids)
        blk = {"nn": (i, kq), "nt": (i, kq), "tn": (kq, i)}[mode]
        return (g,) + blk if ga is not None else blk

    def b_idx(*ids):
        g, i, j, kq = order(ids)
        blk = {"nn": (kq, j), "nt": (j, kq), "tn": (kq, j)}[mode]
        return (g,) + blk if gb is not None else blk

    def o_idx(*ids):
        g, i, j, kq = order(ids)
        return (g, i, j) if batch_out else (i, j)

    a_blk = {"nn": (tm, tk), "nt": (tm, tk), "tn": (tk, tm)}[mode]
    b_blk = {"nn": (tk, tn), "nt": (tn, tk), "tn": (tk, tn)}[mode]
    if ga is not None:
        a_blk = (None,) + a_blk
    if gb is not None:
        b_blk = (None,) + b_blk
    if batch_out:
        out_shape = jax.ShapeDtypeStruct((g_n, m, n), out_dtype)
        o_blk = (None, tm, tn)
        grid = (g_n, m // tm, n // tn, nk)
    else:
        out_shape = jax.ShapeDtypeStruct((m, n), out_dtype)
        o_blk = (tm, tn)
        grid = (m // tm, n // tn, g_n, nk)
    scratch = [] if (acc_in_out or n_red == 1) else [pltpu.VMEM((tm, tn), F32)]
    any_spec = pl.BlockSpec(memory_space=pl.ANY)
    out_shapes = [out_shape] + ([jax.ShapeDtypeStruct(out_shape.shape, BF16)] if bf16_copy else [])
    res = pl.pallas_call(
        body, name=name, out_shape=tuple(out_shapes + _rs_out(rs_sends) + list(ag_out_shapes)), grid=grid,
        in_specs=[pl.BlockSpec(a_blk, a_idx), pl.BlockSpec(b_blk, b_idx)] + [any_spec] * (n_rs + n_ag),
        out_specs=tuple([pl.BlockSpec(o_blk, o_idx)] * n_out + [any_spec] * (n_rs + n_ag_out)),
        scratch_shapes=scratch + (_rs_scratch(rs_sends) if n_rs else []) + (_ag_scratch(n_ag) if n_ag else []),
        compiler_params=_params(),
    )(a, b, *rs_sends, *ag_srcs)
    return res if len(res) > 1 else res[0]


EW_TILE = 256
ROW_TILE = 512
EPILOGUE_CHUNKS = 8
MXU_WIDTH = 256


def _rms(v):
    return lax.rsqrt(jnp.mean(v * v, axis=-1, keepdims=True) + EPS)


def _rms_bwd(dhat, vh, r):
    return r * (dhat - vh * jnp.mean(dhat * vh, axis=-1, keepdims=True))


def _tok_spec(tm, d):
    return pl.BlockSpec((tm, d), lambda i: (i, 0))


def _copy_tokens(v, name):
    t, d = v.shape
    n = t // ROW_TILE

    def body(v_ref, o_ref, sems):
        def copy(i):
            rows = pl.ds(i * ROW_TILE, ROW_TILE)
            return pltpu.make_async_copy(v_ref.at[rows], o_ref.at[rows], sems.at[i])

        for i in range(n):
            copy(i).start()
        for i in range(n):
            copy(i).wait()

    any_spec = pl.BlockSpec(memory_space=pl.ANY)
    return pl.pallas_call(
        body, name=name, out_shape=jax.ShapeDtypeStruct((t, d), v.dtype),
        in_specs=[any_spec], out_specs=any_spec, scratch_shapes=[pltpu.SemaphoreType.DMA((n,))],
    )(v)


def _vec_spec(d):
    return pl.BlockSpec((1, d), lambda i: (0, 0))


def _mod_spec(tiles_per_seq, d):
    return pl.BlockSpec((None, N_MOD, d), lambda i: (i // tiles_per_seq, 0, 0))


def _seq_acc_spec(tiles_per_seq, d):
    return pl.BlockSpec((None, 1, d), lambda i: (i // tiles_per_seq, 0, 0))


def _acc(ref, val, first):
    if first is False:
        ref[...] += val
        return

    @pl.when(first)
    def _():
        ref[...] = val

    @pl.when(jnp.logical_not(first))
    def _():
        ref[...] += val


def _colsum(v):
    return jnp.sum(v, axis=0, keepdims=True)


def _pre_mix(x2, g_pre, mod, seq, ag_srcs, ag_out_shapes, ag_dests):
    t, d = x2.shape
    tm = EW_TILE
    n_steps = t // tm
    n_ag, n_ag_out = len(ag_srcs), len(ag_out_shapes)

    def body(x_ref, g_ref, mod_ref, *rest):
        ag_src, h_ref = rest[:n_ag], rest[n_ag]
        ag_out, sems = rest[n_ag + 1:n_ag + 1 + n_ag_out], rest[n_ag + 1 + n_ag_out:]
        ag_start, ag_forward, ag_finish = _ag_phases(ag_dests, ag_src, ag_out, *sems)
        step = pl.program_id(0)
        pl.when(step == 0)(ag_start)
        xv = x_ref[...]
        n = xv * _rms(xv) * g_ref[...]
        h_ref[...] = (n * (1.0 + mod_ref[1:2, :]) + mod_ref[0:1, :]).astype(BF16)

        @pl.when(step == n_steps - 1)
        def _():
            ag_forward()
            ag_finish()

    any_spec = pl.BlockSpec(memory_space=pl.ANY)
    return pl.pallas_call(
        body, name="pre_mix", out_shape=(jax.ShapeDtypeStruct((t, d), BF16), *ag_out_shapes), grid=(n_steps,),
        in_specs=[_tok_spec(tm, d), _vec_spec(d), _mod_spec(seq // tm, d)] + [any_spec] * n_ag,
        out_specs=(_tok_spec(tm, d), *([any_spec] * n_ag_out)),
        scratch_shapes=_ag_scratch(n_ag), compiler_params=_params(),
    )(x2, g_pre, mod, *ag_srcs)


def _matmul_rows(a, b, tm, seq, name, epilogue, ep_in, ep_in_kinds, ep_out, ep_out_kinds, rs_sends=()):
    g_n = a.shape[0] if a.ndim == 3 else None
    (m, k), n = a.shape[-2:], b.shape[-1]
    tps = seq // tm
    n_i = m // tm
    n_rs = len(rs_sends)
    rs_shapes = [r.shape for r in rs_sends]
    n_in, n_out = len(ep_in), len(ep_out)
    n_cols = n // MXU_WIDTH
    rc, cw = tm // EPILOGUE_CHUNKS, n // n_cols

    def prev(i):
        return jnp.maximum(i - 1, 0)

    def spec(kind):
        return {"tok": pl.BlockSpec((tm, n), lambda i: (prev(i), 0)),
                "vec": pl.BlockSpec((1, n), lambda i: (0, 0)),
                "mod": pl.BlockSpec((None, N_MOD, n), lambda i: (prev(i) // tps, 0, 0)),
                "seq": pl.BlockSpec((None, 1, n), lambda i: (prev(i) // tps, 0, 0)),
                "loss": pl.BlockSpec((1, LANES), lambda i: (0, 0))}[kind]

    def body(a_ref, b_ref, *rest):
        in_refs, rest = rest[:n_in], rest[n_in:]
        rs_src, rest = rest[:n_rs], rest[n_rs:]
        out_refs, rest = rest[:n_out], rest[n_out:]
        rs_dst, rest = rest[:n_rs], rest[n_rs:]
        fin = rest[0]
        i = pl.program_id(0)
        if n_rs:
            rs_start, rs_finish = _rs_phases(rs_shapes, rs_src, rs_dst, *rest[1:])
            pl.when(i == 0)(rs_start)

        def product(cols):
            if g_n is None:
                return _dot_nn(a_ref[...], b_ref[:, cols])
            p = _dot_nn(a_ref[0], b_ref[0, :, cols])
            for g in range(1, g_n):
                p = p + _dot_nn(a_ref[g], b_ref[g, :, cols])
            return p

        def step(with_epilogue, with_matmul):
            parts = []
            for c in range(EPILOGUE_CHUNKS):
                if with_epilogue:
                    rows = pl.ds(c * rc, rc)
                    epilogue(fin[rows, :], i - 1, tps, in_refs, out_refs, rows, c)
                while with_matmul and len(parts) < (c + 1) * n_cols // EPILOGUE_CHUNKS:
                    cols = slice(len(parts) * cw, (len(parts) + 1) * cw)
                    parts.append((cols, product(cols)))
            for cols, v in parts:
                fin[:, cols] = v

        pl.when(i == 0)(functools.partial(step, False, True))
        pl.when(jnp.logical_and(i > 0, i < n_i))(functools.partial(step, True, True))
        pl.when(i == n_i)(functools.partial(step, True, False))

        if n_rs:
            pl.when(i == n_i)(rs_finish)

    def row(i):
        return jnp.minimum(i, n_i - 1)

    if g_n is None:
        a_spec = pl.BlockSpec((tm, k), lambda i: (row(i), 0))
        b_spec = pl.BlockSpec(b.shape, lambda i: (0, 0), pipeline_mode=pl.Buffered(1))
    else:
        a_spec = pl.BlockSpec((g_n, tm, k), lambda i: (0, row(i), 0))
        b_spec = pl.BlockSpec(b.shape, lambda i: (0, 0, 0), pipeline_mode=pl.Buffered(1))
    any_spec = pl.BlockSpec(memory_space=pl.ANY)
    res = pl.pallas_call(
        body, name=name, grid=(n_i + 1,), out_shape=tuple(list(ep_out) + _rs_out(rs_sends)),
        in_specs=[a_spec, b_spec] + [spec(kd) for kd in ep_in_kinds] + [any_spec] * n_rs,
        out_specs=tuple([spec(kd) for kd in ep_out_kinds] + [any_spec] * n_rs),
        scratch_shapes=[pltpu.VMEM((tm, n), F32)] + (_rs_scratch(rs_sends) if n_rs else []),
        compiler_params=_params(),
    )(a, b, *ep_in, *rs_sends)
    return res


def _first(cond, chunk):
    return cond if chunk == 0 else False


def _mid_epilogue(mv, i, tps, in_refs, out_refs, rows, chunk):
    x_ref, gpost_ref, gpre_ref, mod_ref = in_refs
    mix_ref, x1_ref, h2_ref = out_refs
    mix_ref[rows, :] = mv
    x1 = x_ref[rows, :] + mod_ref[2:3, :] * (mv * _rms(mv) * gpost_ref[...])
    x1_ref[rows, :] = x1
    n = x1 * _rms(x1) * gpre_ref[...]
    h2_ref[rows, :] = (n * (1.0 + mod_ref[4:5, :]) + mod_ref[3:4, :]).astype(BF16)


def _post_epilogue(fv, i, tps, in_refs, out_refs, rows, chunk):
    x1_ref, tgt_ref, g_ref, mod_ref = in_refs
    loss_ref, dy_ref, df_ref, dgate_ref, gg_ref = out_refs
    d = fv.shape[1]
    r = _rms(fv)
    fh = fv * r
    nf = fh * g_ref[...]
    gate = mod_ref[5:6, :]
    err = x1_ref[rows, :] + gate * nf - tgt_ref[rows, :]
    _acc(loss_ref, jnp.sum(_colsum(err * err), axis=1, keepdims=True) * jnp.ones((1, LANES), F32),
         _first(i == 0, chunk))
    dy = err * (1.0 / d)
    dy_ref[rows, :] = dy
    _acc(dgate_ref, _colsum(dy * nf), _first(i % tps == 0, chunk))
    dn = dy * gate
    _acc(gg_ref, _colsum(dn * fh), _first(i == 0, chunk))
    df_ref[rows, :] = _rms_bwd(dn * g_ref[...], fh, r).astype(BF16)


def _bwd_mid_epilogue(dh, i, tps, in_refs, out_refs, rows, chunk):
    dy_ref, x1_ref, mix_ref, gpre_ref, gpost_ref, mod_ref = in_refs
    dx1_ref, dmix_ref, dshift_ref, dscale_ref, dgate_ref, ggpre_ref, ggpost_ref = out_refs
    seq_first, first = _first(i % tps == 0, chunk), _first(i == 0, chunk)
    x1 = x1_ref[rows, :]
    r = _rms(x1)
    xh = x1 * r
    gpre = gpre_ref[...]
    _acc(dshift_ref, _colsum(dh), seq_first)
    _acc(dscale_ref, _colsum(dh * xh * gpre), seq_first)
    dn = dh * (1.0 + mod_ref[4:5, :])
    _acc(ggpre_ref, _colsum(dn * xh), first)
    dx1 = dy_ref[rows, :] + _rms_bwd(dn * gpre, xh, r)
    dx1_ref[rows, :] = dx1
    mv = mix_ref[rows, :]
    rm = _rms(mv)
    mh = mv * rm
    gpost = gpost_ref[...]
    _acc(dgate_ref, _colsum(dx1 * mh * gpost), seq_first)
    dnm = dx1 * mod_ref[2:3, :]
    _acc(ggpost_ref, _colsum(dnm * mh), first)
    dmix_ref[rows, :] = _rms_bwd(dnm * gpost, mh, rm).astype(BF16)


def _bwd_pre_epilogue(dh, i, tps, in_refs, out_refs, rows, chunk):
    dx1_ref, x_ref, g_ref, mod_ref = in_refs
    gx_ref, dshift_ref, dscale_ref, gg_ref = out_refs
    seq_first = _first(i % tps == 0, chunk)
    xv = x_ref[rows, :]
    r = _rms(xv)
    xh = xv * r
    g = g_ref[...]
    _acc(dshift_ref, _colsum(dh), seq_first)
    _acc(dscale_ref, _colsum(dh * xh * g), seq_first)
    dn = dh * (1.0 + mod_ref[1:2, :])
    _acc(gg_ref, _colsum(dn * xh), _first(i == 0, chunk))
    gx_ref[rows, :] = dx1_ref[rows, :] + _rms_bwd(dn * g, xh, r)


def _ffn_up(h2, wgu, tm, tn):
    t, d = h2.shape
    f = wgu.shape[1]

    def body(h_ref, w_ref, gu_ref, act_ref):
        h = h_ref[...]
        g = _dot_nt(h, w_ref[0])
        u = _dot_nt(h, w_ref[1])
        gu_ref[0] = g.astype(BF16)
        gu_ref[1] = u.astype(BF16)
        act_ref[...] = (g * jax.nn.sigmoid(g) * u).astype(BF16)

    return pl.pallas_call(
        body, name="ffn_up", grid=(f // tn, t // tm),
        out_shape=(jax.ShapeDtypeStruct((2, t, f), BF16), jax.ShapeDtypeStruct((t, f), BF16)),
        in_specs=[pl.BlockSpec((tm, d), lambda j, i: (i, 0)), pl.BlockSpec((2, tn, d), lambda j, i: (0, j, 0))],
        out_specs=(pl.BlockSpec((2, tm, tn), lambda j, i: (0, i, j)), pl.BlockSpec((tm, tn), lambda j, i: (i, j))),
        compiler_params=_params(),
    )(h2, wgu)


def _ffn_act_bwd(df, wd, gu, tm, tn):
    t, d = df.shape
    f = wd.shape[0]

    def body(df_ref, w_ref, gu_ref, dgu_ref):
        da = _dot_nt(df_ref[...], w_ref[...])
        g = gu_ref[0].astype(F32)
        u = gu_ref[1].astype(F32)
        s = jax.nn.sigmoid(g)
        silu = g * s
        dgu_ref[0] = (da * u * (s + silu * (1.0 - s))).astype(BF16)
        dgu_ref[1] = (da * silu).astype(BF16)

    return pl.pallas_call(
        body, name="ffn_act_bwd", grid=(f // tn, t // tm),
        out_shape=jax.ShapeDtypeStruct((2, t, f), BF16),
        in_specs=[pl.BlockSpec((tm, d), lambda j, i: (i, 0)), pl.BlockSpec((tn, d), lambda j, i: (j, 0)),
                  pl.BlockSpec((2, tm, tn), lambda j, i: (0, i, j))],
        out_specs=pl.BlockSpec((2, tm, tn), lambda j, i: (0, i, j)),
        compiler_params=_params(),
    )(df, wd, gu)


SIGN_BIT = 0x80000000
Q_SCALE = 1.0 / math.sqrt(HEAD_DIM)


def _softplus(z):
    neg_abs = lax.bitcast_convert_type(lax.bitcast_convert_type(z, jnp.uint32) | jnp.uint32(SIGN_BIT), F32)
    return jnp.maximum(z, 0.0) + jnp.log(1.0 + jnp.exp(neg_abs))


def _hi_lo(v):
    hi = v.astype(BF16)
    return jnp.concatenate([hi, (v - hi.astype(F32)).astype(BF16)], axis=1)


def _emit_skewed(chains, lag=1):
    for t in range(max(len(ch) for ch in chains) + lag * (len(chains) - 1)):
        for c, ch in enumerate(chains):
            if 0 <= t - lag * c < len(ch):
                ch[t - lag * c]()


def _fwd_chain(blk, qs, k_ref, v_ref, c0, kb, cols, mask, ntri, lane, tq):
    st = {}

    def scores():
        st["z"] = _dot_nt(qs, k_ref[pl.ds(c0, tq), cols])

    def soft():
        sp = _softplus(st["z"])
        if mask is not None:
            sp = jnp.where(mask, sp, 0.0)
        st["parts"] = _hi_lo(sp)
        st["cur"] = blk["cur"]
        blk["cm"] = jnp.where(lane == kb, blk["cur"], blk["cm"])
        blk["cur"] = blk["cur"] - jnp.sum(sp, axis=1, keepdims=True)

    def sums():
        st["s"] = _dot_nn(st["parts"], ntri)

    def weights():
        w = jnp.exp(st["z"] + st["s"] + st["cur"])
        if mask is not None:
            w = jnp.where(mask, w, 0.0)
        st["w"] = w.astype(BF16)

    def out():
        p = _dot_nn(st["w"], v_ref[pl.ds(c0, tq), cols])
        blk["pv"] = p if blk["pv"] is None else blk["pv"] + p

    return [scores, soft, sums, weights, out]


def _bwd_chain(blk, qs, dos, cs, k_ref, v_ref, dk_ref, dv_ref, c0, kb, cols, mask, ntri, tri_i, lane, tq):
    st = {}

    def scores():
        st["z"] = _dot_nt(qs, k_ref[pl.ds(c0, tq), cols])
        st["dw"] = _dot_nt(dos, v_ref[pl.ds(c0, tq), cols])

    def soft():
        sp = _softplus(st["z"])
        if mask is not None:
            sp = jnp.where(mask, sp, 0.0)
        st["sp"] = sp
        st["parts"] = _hi_lo(sp)
        st["cur"] = jnp.sum(jnp.where(lane == kb, cs, 0.0), axis=1, keepdims=True)

    def sums():
        st["s"] = _dot_nn(st["parts"], ntri)

    def weights():
        w = jnp.exp(st["z"] + st["s"] + st["cur"])
        if mask is not None:
            w = jnp.where(mask, w, 0.0)
        ee = w * st["dw"]
        st["w"], st["ee"], st["ec"] = w.astype(BF16), ee, blk["ec"]
        blk["ec"] = blk["ec"] + jnp.sum(ee, axis=1, keepdims=True)

    def prefix():
        st["einc"] = _dot_nn(st["ee"].astype(BF16), tri_i)

    def dz():
        v = st["ee"] - jnp.exp(st["z"] - st["sp"]) * (st["einc"] + st["ec"])
        if mask is not None:
            v = jnp.where(mask, v, 0.0)
        st["dz"] = v.astype(BF16)

    def grads():
        p = _dot_nn(st["dz"], k_ref[pl.ds(c0, tq), cols])
        blk["dq"] = p if blk["dq"] is None else blk["dq"] + p
        dk_ref[pl.ds(c0, tq), :] += _dot_tn(st["dz"], qs)
        dv_ref[pl.ds(c0, tq), :] += _dot_tn(st["w"], dos)

    return [scores, soft, sums, weights, prefix, dz, grads]


def _stack_heads(v, lane, scale=None):
    if scale is not None:
        v = v * jnp.asarray(scale, v.dtype)
    zero = jnp.zeros_like(v)
    return jnp.concatenate([jnp.where(lane < HEAD_DIM, v, zero), jnp.where(lane >= HEAD_DIM, v, zero)], axis=0)


def _diag_mask(tq):
    row = lax.broadcasted_iota(jnp.int32, (2 * tq, tq), 0)
    col = lax.broadcasted_iota(jnp.int32, (2 * tq, tq), 1)
    return col < jnp.where(row >= tq, row - tq, row)


def _attn_fwd(proj, tri_after, n_seq, seq, ag_srcs, ag_out_shapes, ag_dests):
    t = proj.shape[0]
    tq = ATT_TILE
    npp = ATT_PAIRS
    n_blk = (proj.shape[1] // 4) // (npp * LANES)
    n_ag, n_ag_out = len(ag_srcs), len(ag_out_shapes)
    n_steps = n_seq * n_blk

    def body(q_ref, k_ref, v_ref, tri_ref, *rest):
        ag_src, rest = rest[:n_ag], rest[n_ag:]
        o_ref, cs_ref = rest[:2]
        ag_out, rest = rest[2:2 + n_ag_out], rest[2 + n_ag_out:]
        oacc, cmat, carry = rest[:3]
        ag_start, ag_forward, ag_finish = _ag_phases(ag_dests, ag_src, ag_out, *rest[3:])
        step = pl.program_id(0) * n_blk + pl.program_id(1)
        pl.when(step == 0)(ag_start)
        pl.when(step == (3 * n_steps) // 4)(ag_forward)
        lane = lax.broadcasted_iota(jnp.int32, (1, LANES), 1)
        ntri = tri_ref[...]
        diag = _diag_mask(tq)

        def q_tile(qi, _):
            r0 = pl.multiple_of(qi * tq, tq)
            qs = [_stack_heads(q_ref[pl.ds(r0, tq), pp * LANES:(pp + 1) * LANES], lane, Q_SCALE)
                  for pp in range(npp)]
            carry[...] = jnp.zeros_like(carry)
            cmat[...] = jnp.zeros_like(cmat)
            oacc[...] = jnp.zeros_like(oacc)

            def run_tiles(tiles):
                blocks = [dict(cur=carry[pp], cm=cmat[pp], pv=None) for pp in range(npp)]
                chains = []
                for kb, mask in tiles:
                    c0 = pl.multiple_of(kb * tq, tq)
                    for pp in range(npp):
                        chains.append(_fwd_chain(blocks[pp], qs[pp], k_ref, v_ref, c0, kb,
                                                 slice(pp * LANES, (pp + 1) * LANES), mask, ntri, lane, tq))
                _emit_skewed(chains)
                for pp in range(npp):
                    oacc[pp] += blocks[pp]["pv"]
                    cmat[pp] = blocks[pp]["cm"]
                    carry[pp] = blocks[pp]["cur"]

            odd = qi % 2

            @pl.when(odd == 0)
            def _():
                run_tiles([(qi, diag)])

            @pl.when(odd == 1)
            def _():
                run_tiles([(qi, diag), (qi - 1, None)])

            def pair(j, _):
                kb = qi - 1 - odd - 2 * j
                run_tiles([(kb, None), (kb - 1, None)])
                return 0

            lax.fori_loop(0, qi // 2, pair, 0)
            for pp in range(npp):
                c_off = 2 * pp * LANES
                cs_ref[pl.ds(r0, tq), c_off:c_off + LANES] = cmat[pp, 0:tq, :]
                cs_ref[pl.ds(r0, tq), c_off + LANES:c_off + 2 * LANES] = cmat[pp, tq:2 * tq, :]
                o_ref[pl.ds(r0, tq), pp * LANES:(pp + 1) * LANES] = jnp.where(
                    lane < HEAD_DIM, oacc[pp, 0:tq, :], oacc[pp, tq:2 * tq, :]).astype(BF16)
            return 0

        lax.fori_loop(0, seq // tq, q_tile, 0)
        pl.when(step == n_steps - 1)(ag_finish)

    wid = npp * LANES
    blk = lambda off: pl.BlockSpec((seq, wid), lambda b, p: (b, off + p))
    any_spec = pl.BlockSpec(memory_space=pl.ANY)
    return pl.pallas_call(
        body, name="attn_fwd", grid=(n_seq, n_blk),
        out_shape=(jax.ShapeDtypeStruct((2, t, n_blk * wid), BF16),
                   jax.ShapeDtypeStruct((t, n_blk * 2 * wid), F32), *ag_out_shapes),
        in_specs=[blk(0), blk(n_blk), blk(2 * n_blk), pl.BlockSpec((2 * tq, tq), lambda b, p: (0, 0))]
        + [any_spec] * n_ag,
        out_specs=(pl.BlockSpec((None, seq, wid), lambda b, p: (0, b, p)),
                   pl.BlockSpec((seq, 2 * wid), lambda b, p: (b, p)), *([any_spec] * n_ag_out)),
        scratch_shapes=[pltpu.VMEM((npp, 2 * tq, LANES), F32), pltpu.VMEM((npp, 2 * tq, LANES), F32),
                        pltpu.VMEM((npp, 2 * tq, 1), F32)] + _ag_scratch(n_ag),
        compiler_params=_params(),
    )(proj, proj, proj, tri_after, *ag_srcs)


def _attn_bwd(proj, dcat, cstats, tri_after, tri_incl, n_seq, seq, rs_sends):
    t = proj.shape[0]
    tq = ATT_TILE
    npp = ATT_PAIRS
    width = proj.shape[1] // 4
    n_blk = width // (npp * LANES)
    n_rs = len(rs_sends)
    rs_shapes = [r.shape for r in rs_sends]
    n_steps = n_seq * n_blk

    def body(q_ref, k_ref, v_ref, do_ref, cs_ref, tria_ref, trii_ref, *rest):
        rs_src, rest = rest[:n_rs], rest[n_rs:]
        out_ref = rest[0]
        rs_dst, rest = rest[1:1 + n_rs], rest[1 + n_rs:]
        dq_acc, dk_acc, dv_acc, ecarry = rest[:4]
        rs_start, rs_finish = _rs_phases(rs_shapes, rs_src, rs_dst, *rest[4:])
        step = pl.program_id(0) * n_blk + pl.program_id(1)
        pl.when(step == 0)(rs_start)
        lane = lax.broadcasted_iota(jnp.int32, (1, LANES), 1)
        ntri = tria_ref[...]
        tri_i = trii_ref[...]
        diag = _diag_mask(tq)
        dk_acc[...] = jnp.zeros_like(dk_acc)
        dv_acc[...] = jnp.zeros_like(dv_acc)

        def q_tile(qi, _):
            r0 = pl.multiple_of(qi * tq, tq)
            qs, dos, cs = [], [], []
            for pp in range(npp):
                cols = slice(pp * LANES, (pp + 1) * LANES)
                qs.append(_stack_heads(q_ref[pl.ds(r0, tq), cols], lane, Q_SCALE))
                dos.append(_stack_heads(do_ref[pl.ds(r0, tq), cols], lane))
                c_off = 2 * pp * LANES
                cs.append(jnp.concatenate([cs_ref[pl.ds(r0, tq), c_off:c_off + LANES],
                                           cs_ref[pl.ds(r0, tq), c_off + LANES:c_off + 2 * LANES]], axis=0))
            ecarry[...] = jnp.zeros_like(ecarry)
            dq_acc[...] = jnp.zeros_like(dq_acc)

            def run_tiles(tiles):
                blocks = [dict(ec=ecarry[pp], dq=None) for pp in range(npp)]
                chains = []
                for kb, mask in tiles:
                    c0 = pl.multiple_of(kb * tq, tq)
                    for pp in range(npp):
                        chains.append(_bwd_chain(
                            blocks[pp], qs[pp], dos[pp], cs[pp], k_ref, v_ref, dk_acc.at[pp], dv_acc.at[pp],
                            c0, kb, slice(pp * LANES, (pp + 1) * LANES), mask, ntri, tri_i, lane, tq))
                _emit_skewed(chains)
                for pp in range(npp):
                    dq_acc[pp] += blocks[pp]["dq"]
                    ecarry[pp] = blocks[pp]["ec"]

            def pair(j, _):
                run_tiles([(2 * j, None), (2 * j + 1, None)])
                return 0

            lax.fori_loop(0, qi // 2, pair, 0)
            odd = qi % 2

            @pl.when(odd == 0)
            def _():
                run_tiles([(qi, diag)])

            @pl.when(odd == 1)
            def _():
                run_tiles([(qi - 1, None), (qi, diag)])

            for pp in range(npp):
                dq = jnp.where(lane < HEAD_DIM, dq_acc[pp, 0:tq, :], dq_acc[pp, tq:2 * tq, :])
                out_ref[0, pl.ds(r0, tq), pp * LANES:(pp + 1) * LANES] = (dq * Q_SCALE).astype(BF16)
            return 0

        lax.fori_loop(0, seq // tq, q_tile, 0)
        for pp in range(npp):
            cols = slice(pp * LANES, (pp + 1) * LANES)
            out_ref[1, :, cols] = dk_acc[pp].astype(BF16)
            out_ref[2, :, cols] = dv_acc[pp].astype(BF16)
        pl.when(step == n_steps - 1)(rs_finish)

    wid = npp * LANES
    blk = lambda off: pl.BlockSpec((seq, wid), lambda b, p: (b, off + p))
    tri_spec = pl.BlockSpec((2 * tq, tq), lambda b, p: (0, 0))
    any_spec = pl.BlockSpec(memory_space=pl.ANY)
    return pl.pallas_call(
        body, name="attn_bwd", grid=(n_seq, n_blk),
        out_shape=(jax.ShapeDtypeStruct((4, t, width), BF16), *_rs_out(rs_sends)),
        in_specs=[blk(0), blk(n_blk), blk(2 * n_blk), pl.BlockSpec((seq, wid), lambda b, p: (b, p)),
                  pl.BlockSpec((seq, 2 * wid), lambda b, p: (b, p)), tri_spec,
                  pl.BlockSpec((tq, tq), lambda b, p: (0, 0))] + [any_spec] * n_rs,
        out_specs=(pl.BlockSpec((3, seq, wid), lambda b, p: (0, b, p)), *([any_spec] * n_rs)),
        scratch_shapes=[pltpu.VMEM((npp, 2 * tq, LANES), F32), pltpu.VMEM((npp, seq, LANES), F32),
                        pltpu.VMEM((npp, seq, LANES), F32), pltpu.VMEM((npp, 2 * tq, 1), F32)]
        + _rs_scratch(rs_sends),
        compiler_params=_params(),
    )(proj, proj, proj, dcat, cstats, tri_after, tri_incl, *rs_sends)


def _window_sum(v, g, rows, forward):
    s_len = v.shape[0]
    s = v
    for step in range(g + 1):
        sh = 1 << step
        if forward:
            s = s + jnp.where(rows < s_len - sh, pltpu.roll(s, s_len - sh, axis=0), 0.0)
        else:
            s = s + jnp.where(rows >= sh, pltpu.roll(s, sh, axis=0), 0.0)
    return s


def _window_count(g, rows):
    return jnp.minimum(rows + 1, POOL_WINDOWS[g]).astype(F32)


def _pooled(u, g, rows):
    return _window_sum(u, g, rows, forward=False) / _window_count(g, rows) - u


def _group_cols(g):
    return slice(g * POOL_GROUP_DIM, (g + 1) * POOL_GROUP_DIM)


def _pool_fwd(proj, w_pool, pool_scale, cat, n_seq, seq):
    n_grp = len(POOL_WINDOWS)
    width = n_grp * POOL_GROUP_DIM
    assert [1 << (g + 1) for g in range(n_grp)] == list(POOL_WINDOWS)

    def body(u_ref, w_ref, s_ref, alias_ref, o_ref):
        del alias_ref
        rows = lax.broadcasted_iota(jnp.int32, (seq, 1), 0)
        for g in range(n_grp):
            cols = _group_cols(g)
            pooled = _pooled(u_ref[:, cols].astype(F32), g, rows)
            y = _dot_nn(pooled.astype(BF16), w_ref[g].astype(BF16))
            o_ref[:, cols] = (y * s_ref[:, cols]).astype(BF16)

    return pl.pallas_call(
        body, name="pool_fwd", grid=(n_seq,),
        out_shape=jax.ShapeDtypeStruct(cat.shape, BF16),
        in_specs=[pl.BlockSpec((seq, width), lambda b: (b, 3)),
                  pl.BlockSpec((n_grp, POOL_GROUP_DIM, POOL_GROUP_DIM), lambda b: (0, 0, 0)),
                  pl.BlockSpec((1, width), lambda b: (0, 0)),
                  pl.BlockSpec(memory_space=pl.ANY)],
        out_specs=pl.BlockSpec((None, seq, width), lambda b: (1, b, 0)),
        input_output_aliases={3: 0},
        compiler_params=_params(),
    )(proj, w_pool, pool_scale, cat)


def _pool_bwd(proj, dcat, w_pool, pool_scale, dqkv, n_seq, seq):
    n_grp = len(POOL_WINDOWS)
    width = n_grp * POOL_GROUP_DIM

    def body(u_ref, dp_ref, w_ref, s_ref, alias_ref, du_ref, gw_ref, gs_ref):
        del alias_ref
        b = pl.program_id(0)
        rows = lax.broadcasted_iota(jnp.int32, (seq, 1), 0)
        for g in range(n_grp):
            cols = _group_cols(g)
            pb = _pooled(u_ref[:, cols].astype(F32), g, rows).astype(BF16)
            wb = w_ref[g].astype(BF16)
            z = _dot_nn(pb, wb)
            dp = dp_ref[:, cols].astype(F32)
            _acc(gs_ref.at[:, cols], _colsum(dp * z), b == 0)
            dys = (dp * s_ref[:, cols]).astype(BF16)
            _acc(gw_ref.at[g], _dot_tn(pb, dys), b == 0)
            dpooled = _dot_nt(dys, wb)
            du = _window_sum(dpooled / _window_count(g, rows), g, rows, forward=True) - dpooled
            du_ref[:, cols] = du.astype(BF16)

    return pl.pallas_call(
        body, name="pool_bwd", grid=(n_seq,),
        out_shape=(jax.ShapeDtypeStruct(dqkv.shape, BF16),
                   jax.ShapeDtypeStruct((n_grp, POOL_GROUP_DIM, POOL_GROUP_DIM), F32),
                   jax.ShapeDtypeStruct((1, width), F32)),
        in_specs=[pl.BlockSpec((seq, width), lambda b: (b, 3)),
                  pl.BlockSpec((seq, width), lambda b: (b, 1)),
                  pl.BlockSpec((n_grp, POOL_GROUP_DIM, POOL_GROUP_DIM), lambda b: (0, 0, 0)),
                  pl.BlockSpec((1, width), lambda b: (0, 0)),
                  pl.BlockSpec(memory_space=pl.ANY)],
        out_specs=(pl.BlockSpec((None, seq, width), lambda b: (3, b, 0)),
                   pl.BlockSpec((n_grp, POOL_GROUP_DIM, POOL_GROUP_DIM), lambda b: (0, 0, 0)),
                   pl.BlockSpec((1, width), lambda b: (0, 0))),
        input_output_aliases={4: 0},
        compiler_params=_params(),
    )(proj, dcat, w_pool, pool_scale, dqkv)


def _cond_fwd(c_all, w_cond, b_cols):
    n, _ = c_all.shape
    cols = w_cond.shape[1]

    def body(c_ref, w_ref, b_ref, o_ref):
        cv = c_ref[...]
        a = cv * jax.nn.sigmoid(cv)
        o_ref[...] = jnp.dot(a, w_ref[...], preferred_element_type=F32,
                             precision=lax.Precision.HIGHEST) + b_ref[...]

    return pl.pallas_call(
        body, name="cond_fwd", out_shape=jax.ShapeDtypeStruct((n, cols), F32),
        compiler_params=_params(),
    )(c_all, w_cond, b_cols)


def _cond_bwd_adamw(c_all, dmod_all, dmod_cols, w, m_w, v_w, b, m_b, v_b):
    def body(c_ref, dm_ref, dmc_ref, w_ref, mw_ref, vw_ref, b_ref, mb_ref, vb_ref,
             gw_ref, dw_ref, nmw_ref, nvw_ref, gb_ref, db_ref, nmb_ref, nvb_ref):
        cv = c_ref[...]
        a = cv * jax.nn.sigmoid(cv)
        gw = lax.dot_general(a, dmc_ref[...], (((0,), (0,)), ((), ())),
                             preferred_element_type=F32, precision=lax.Precision.HIGHEST)
        gw_ref[...] = gw
        dw_ref[...], nmw_ref[...], nvw_ref[...] = _adamw_math(w_ref[...], gw, mw_ref[...], vw_ref[...])
        gb = _colsum(dm_ref[...])
        gb_ref[...] = gb
        db_ref[...], nmb_ref[...], nvb_ref[...] = _adamw_math(b_ref[...], gb, mb_ref[...], vb_ref[...])

    w_sds, b_sds = jax.ShapeDtypeStruct(w.shape, F32), jax.ShapeDtypeStruct(b.shape, F32)
    outs = pl.pallas_call(
        body, name="cond_bwd_adamw", out_shape=(w_sds,) * 4 + (b_sds,) * 4, compiler_params=_params(),
    )(c_all, dmod_all, dmod_cols, w, m_w, v_w, b, m_b, v_b)
    return outs[:4], outs[4:]


def _adamw_math(w, g, m, v):
    m = ADAM_B1 * m + (1.0 - ADAM_B1) * g
    v = ADAM_B2 * v + (1.0 - ADAM_B2) * (g * g)
    m_hat = m / (1.0 - ADAM_B1 ** ADAM_STEP)
    v_hat = v / (1.0 - ADAM_B2 ** ADAM_STEP)
    delta = -ADAM_LR * (m_hat / (jnp.sqrt(v_hat) + ADAM_EPS) + ADAM_WD * w)
    return delta, m, v


def _adamw_small(ws, gparts, ms, vs, name):
    n = len(ws)

    def body(*refs):
        w_r, g_r, m_r, v_r = refs[:n], refs[n:2 * n], refs[2 * n:3 * n], refs[3 * n:4 * n]
        outs = refs[4 * n:]
        for i in range(n):
            g = g_r[i][0]
            for dev in range(1, g_r[i].shape[0]):
                g = g + g_r[i][dev]
            delta, m, v = _adamw_math(w_r[i][...], g, m_r[i][...], v_r[i][...])
            outs[i][...] = g
            outs[n + i][...] = delta
            outs[2 * n + i][...] = m
            outs[3 * n + i][...] = v

    sds = [jax.ShapeDtypeStruct(w.shape, F32) for w in ws]
    return pl.pallas_call(
        body, name=name, out_shape=tuple(sds * 4), compiler_params=_params(),
    )(*ws, *gparts, *ms, *vs)


def kernel(x, c, w_cond, b_cond, g_mix_pre, g_mix_post, w_in, w_pool, pool_scale, w_out, g_ffn_pre, g_ffn_post, w_gate, w_up, w_down, loss_target, m_w_cond, m_b_cond, m_g_mix_pre, m_g_mix_post, m_w_in, m_w_pool, m_pool_scale, m_w_out, m_g_ffn_pre, m_g_ffn_post, m_w_gate, m_w_up, m_w_down, v_w_cond, v_b_cond, v_g_mix_pre, v_g_mix_post, v_w_in, v_w_pool, v_pool_scale, v_w_out, v_g_ffn_pre, v_g_ffn_post, v_w_gate, v_w_up, v_w_down):
    n_seq, seq, d = x.shape
    t = n_seq * seq
    xi, yi, ci = _mesh_pos()
    me = 4 * xi + 2 * yi + ci
    x2 = x.reshape(t, d)
    tgt2 = loss_target.reshape(t, d)
    in_rows = w_in.shape[2]
    out_rows = w_out.shape[1]
    ff_rows = w_gate.shape[2]
    ff = N_DEV * ff_rows
    cond_cols = w_cond.shape[2]

    win_t = w_in[0].T.astype(BF16)
    wout_s = w_out[0].astype(BF16)
    wg_t = w_gate[0].T.astype(BF16)
    wu_t = w_up[0].T.astype(BF16)
    wd_s = w_down[0].astype(BF16)
    c_all = _all_gather(c, "ag_c").reshape(N_DEV * n_seq, d)

    b_cols = lax.dynamic_slice_in_dim(b_cond, me * cond_cols, cond_cols, axis=1)
    mod_cols = _cond_fwd(c_all, w_cond[0], b_cols)
    mod_g = _all_gather(mod_cols, "ag_mod")
    mod_mine = lax.dynamic_slice_in_dim(mod_g, me * n_seq, n_seq, axis=1)
    mod = jnp.transpose(mod_mine, (1, 0, 2)).reshape(n_seq, N_MOD, d)

    h1, win_g = _pre_mix(x2, g_mix_pre, mod, seq, [win_t],
                         [jax.ShapeDtypeStruct((N_DEV, in_rows, d), BF16)], [(0, ())])
    win_full = win_g.reshape(N_DEV * in_rows, d)
    proj = _matmul(h1, win_full, "nt", BF16, 512, N_DEV * in_rows, d, "proj")
    tq = ATT_TILE
    ids = jnp.arange(tq)
    tri_after = jnp.tile(-(ids[:, None] >= ids[None, :]).astype(BF16), (2, 1))
    tri_incl = (ids[:, None] <= ids[None, :]).astype(BF16)
    attn, cstats, wout_g, wgu_g, wd_g = _attn_fwd(
        proj, tri_after, n_seq, seq, [wout_s, wg_t, wu_t, wd_s],
        [jax.ShapeDtypeStruct((N_DEV, out_rows, d), BF16), jax.ShapeDtypeStruct((2, N_DEV, ff_rows, d), BF16),
         jax.ShapeDtypeStruct((N_DEV, ff_rows, d), BF16)],
        [(0, ()), (1, (0,)), (1, (1,)), (2, ())])
    wout_full = wout_g.reshape(N_DEV * out_rows, d)
    wgu_full = wgu_g.reshape(2, ff, d)
    wd_full = wd_g.reshape(ff, d)
    cat = _pool_fwd(proj, w_pool[0], pool_scale, attn, n_seq, seq)
    tok_f32, tok_bf16 = jax.ShapeDtypeStruct((t, d), F32), jax.ShapeDtypeStruct((t, d), BF16)
    seq_sds, vec_sds = jax.ShapeDtypeStruct((n_seq, 1, d), F32), jax.ShapeDtypeStruct((1, d), F32)
    mix, x1, h2 = _matmul_rows(
        cat, wout_full.reshape(2, d // 2, d), ROW_TILE, seq, "mix_mid", _mid_epilogue,
        [x2, g_mix_post, g_ffn_pre, mod], ["tok", "vec", "vec", "mod"],
        [tok_f32, tok_f32, tok_bf16], ["tok", "tok", "tok"])
    gu, act = _ffn_up(h2, wgu_full, 512, ff // 2)
    loss_sum, dy, df, dgate_f, gg_ffn_post = _matmul_rows(
        act, wd_full, ROW_TILE, seq, "ffn_down_post", _post_epilogue,
        [x1, tgt2, g_ffn_post, mod], ["tok", "tok", "vec", "mod"],
        [jax.ShapeDtypeStruct((1, LANES), F32), tok_f32, tok_bf16, seq_sds, vec_sds],
        ["loss", "tok", "tok", "seq", "vec"])

    dgu = _ffn_act_bwd(df, wd_full, gu, 512, ff // 2)
    gwd, gwd_b = _matmul(act, df, "tn", F32, ff // 2, d // 2, t, "grad_w_down", bf16_copy=True)
    gwgu, gwgu_b = _matmul(dgu, h2, "tn", F32, ff // 2, d // 2, t, "grad_w_gate_up", bf16_copy=True)
    dx1, dmix, dshift_f, dscale_f, dgate_m, gg_ffn_pre, gg_mix_post = _matmul_rows(
        dgu, wgu_full, ROW_TILE, seq, "dh2_bwd_mid", _bwd_mid_epilogue,
        [dy, x1, mix, g_ffn_pre, g_mix_post, mod], ["tok", "tok", "tok", "vec", "vec", "mod"],
        [tok_f32, tok_bf16, seq_sds, seq_sds, seq_sds, vec_sds, vec_sds],
        ["tok", "tok", "seq", "seq", "seq", "vec", "vec"])
    dcat = _matmul(dmix, wout_full, "nt", BF16, 512, d, d, "dcat")
    gwout, gwout_b = _matmul(cat, dmix, "tn", F32, d // 2, d, t, "grad_w_out", bf16_copy=True)
    dqkv, rv_wgu, rv_wd, rv_wout = _attn_bwd(
        proj, dcat, cstats, tri_after, tri_incl, n_seq, seq,
        [gwgu_b.reshape(2, N_DEV, ff_rows, d), gwd_b.reshape(1, N_DEV, ff_rows, d),
         gwout_b.reshape(1, N_DEV, out_rows, d)])
    dproj, gw_pool, gs_pool = _pool_bwd(proj, dcat, w_pool[0], pool_scale, dqkv, n_seq, seq)
    pad_d = lambda v: jnp.pad(v, ((0, 0), (0, d - v.shape[1])))
    n_gw = gw_pool.size // d
    early = jnp.concatenate(
        [gg_mix_post, gg_ffn_pre, gg_ffn_post, pad_d(gs_pool), pad_d(loss_sum), jnp.zeros((3, d), F32),
         gw_pool.reshape(n_gw, d),
         jnp.concatenate([dgate_m, dshift_f, dscale_f, dgate_f], axis=1).reshape(n_seq * 4, d)], axis=0)
    gwin, gwin_b, early_g = _matmul(
        dproj, h1, "tn", F32, d // 2, d, t, "grad_w_in", bf16_copy=True,
        ag=([early], [jax.ShapeDtypeStruct((N_DEV,) + early.shape, F32)], [(0, ())]))
    grad_x, dshift_m, dscale_m, gg_mix_pre, rv_win = _matmul_rows(
        dproj, win_full.reshape(4, d // 2, d), ROW_TILE, seq, "dh1_bwd_pre", _bwd_pre_epilogue,
        [dx1, x2, g_mix_pre, mod], ["tok", "tok", "vec", "mod"],
        [tok_f32, seq_sds, seq_sds, vec_sds], ["tok", "seq", "seq", "vec"],
        rs_sends=[gwin_b.reshape(1, N_DEV, in_rows, d)])


    late = jnp.concatenate([gg_mix_pre, dshift_m.reshape(n_seq, d), dscale_m.reshape(n_seq, d),
                            jnp.zeros((8 - 1 - 2 * n_seq, d), F32)], axis=0)
    late_g = _all_gather(late, "ag_late")
    loss = jnp.sum(early_g[:, 4, 0]) * (0.5 / d)
    dmod_all = jnp.concatenate(
        [late_g[:, 1:1 + n_seq, None, :], late_g[:, 1 + n_seq:1 + 2 * n_seq, None, :],
         early_g[:, 8 + n_gw:, :].reshape(N_DEV, n_seq, 4, d)], axis=2).reshape(N_DEV * n_seq, N_MOD * d)
    dmod_cols = lax.dynamic_slice_in_dim(dmod_all, me * cond_cols, cond_cols, axis=1)
    o_cond, o_bcond = _cond_bwd_adamw(c_all, dmod_all, dmod_cols, w_cond[0], m_w_cond[0], v_w_cond[0],
                                      b_cond, m_b_cond, v_b_cond)
    o_cond = tuple(o[None] for o in o_cond)

    small_ws = [g_mix_pre, g_mix_post, g_ffn_pre, g_ffn_post, pool_scale, w_pool.reshape(-1, POOL_GROUP_DIM)]
    small_ms = [m_g_mix_pre, m_g_mix_post, m_g_ffn_pre, m_g_ffn_post, m_pool_scale, m_w_pool.reshape(-1, POOL_GROUP_DIM)]
    small_vs = [v_g_mix_pre, v_g_mix_post, v_g_ffn_pre, v_g_ffn_post, v_pool_scale, v_w_pool.reshape(-1, POOL_GROUP_DIM)]
    small_gparts = [late_g[:, 0:1, :], early_g[:, 0:1, :], early_g[:, 1:2, :], early_g[:, 2:3, :],
                    early_g[:, 3:4, :pool_scale.shape[1]],
                    early_g[:, 8:8 + n_gw, :].reshape(N_DEV, -1, POOL_GROUP_DIM)]
    so = _adamw_small(small_ws, small_gparts, small_ms, small_vs, "adamw_small")
    ns = len(small_ws)
    sg, sdl, sm, sv = so[:ns], so[ns:2 * ns], so[2 * ns:3 * ns], so[3 * ns:]
    pool_shape = w_pool.shape
    fix = lambda lst: [lst[0], lst[1], lst[2], lst[3], lst[4], lst[5].reshape(pool_shape)]
    sg, sdl, sm, sv = fix(sg), fix(sdl), fix(sm), fix(sv)


    def reduced(mine, recv, slab, w, m, v, name, transposed=False, transpose=False):
        turn = (lambda u: u.T) if transposed else (lambda u: u)
        outs = _rs_final_adamw(mine, recv, slab, turn(w[0]), turn(m[0]), turn(v[0]), name, transpose)
        return tuple(turn(o)[None] for o in outs)

    o_in = reduced(gwin.reshape(1, N_DEV, in_rows, d), rv_win, 0, w_in, m_w_in, v_w_in, "adamw_w_in",
                   transpose=True)
    o_out = reduced(gwout.reshape(1, N_DEV, out_rows, d), rv_wout, 0, w_out, m_w_out, v_w_out, "adamw_w_out")
    gwgu8 = gwgu.reshape(2, N_DEV, ff_rows, d)
    o_gate = reduced(gwgu8, rv_wgu, 0, w_gate, m_w_gate, v_w_gate, "adamw_w_gate", transposed=True)
    o_up = reduced(gwgu8, rv_wgu, 1, w_up, m_w_up, v_w_up, "adamw_w_up", transposed=True)
    o_down = reduced(gwd.reshape(1, N_DEV, ff_rows, d), rv_wd, 0, w_down, m_w_down, v_w_down, "adamw_w_down")

    def pick(k):
        small_k = [sg, sdl, sm, sv][k]
        return [o_cond[k], o_bcond[k], small_k[0], small_k[1], o_in[k], small_k[5], small_k[4], o_out[k],
                small_k[2], small_k[3], o_gate[k], o_up[k], o_down[k]]

    return (loss, _copy_tokens(grad_x, "grad_x_out").reshape(n_seq, seq, d), *pick(0), *pick(1), *pick(2), *pick(3))
```

```python
import functools
import math

import jax
import jax.numpy as jnp
from jax import lax
from jax.experimental import pallas as pl
from jax.experimental.pallas import tpu as pltpu

F32 = jnp.float32
BF16 = jnp.bfloat16
MESH = pl.DeviceIdType.MESH

N_DEV = 8
HEAD_DIM = 64
LANES = 128
POOL_WINDOWS = (2, 4, 8, 16)
POOL_GROUP_DIM = 128
N_MOD = 6
EPS = 1e-6
ATT_TILE = 256
ATT_PAIRS = 2
VMEM_LIMIT = 56 * 1024 * 1024
ADAMW_COL_TILE = 256

ADAM_LR = 0.001
ADAM_B1 = 0.9
ADAM_B2 = 0.999
ADAM_EPS = 1e-08
ADAM_WD = 0.01
ADAM_STEP = 10


def _params(**kw):
    return pltpu.CompilerParams(vmem_limit_bytes=VMEM_LIMIT, **kw)


def _dot_nn(a, b):
    return jnp.dot(a, b, preferred_element_type=F32)


def _dot_nt(a, b):
    return lax.dot_general(a, b, (((1,), (1,)), ((), ())), preferred_element_type=F32)


def _dot_tn(a, b):
    return lax.dot_general(a, b, (((0,), (0,)), ((), ())), preferred_element_type=F32)


def _mesh_pos():
    return lax.axis_index("x"), lax.axis_index("y"), lax.axis_index("c")


def _ag_phases(dests, src, outs, send_sems, recv_sems, local_sems, local_src=None):
    n = len(src)
    x, y, c = _mesh_pos()
    me, sibling = (x, y, c), (x, y, 1 - c)
    chips = [(1 - x, y), (x, 1 - y), (1 - x, 1 - y)]

    def slot(i, dev):
        oi, prefix = dests[i]
        px, py, pc = dev
        return outs[oi].at[prefix + (4 * px + 2 * py + pc,)]

    def copy(i, k, block, to, from_src=False):
        return pltpu.make_async_remote_copy(
            src_ref=src[i] if from_src else slot(i, block), dst_ref=slot(i, block),
            send_sem=send_sems.at[i, k], recv_sem=recv_sems.at[i, k],
            device_id=to, device_id_type=MESH)

    def mine(i):
        return pltpu.make_async_copy((local_src or src)[i], slot(i, me), local_sems.at[i])

    def first(i):
        return [copy(i, 0, me, sibling, from_src=True)] + [
            copy(i, 1 + j, me, (*chip, c), from_src=True) for j, chip in enumerate(chips)]

    def passed(i, j):
        return copy(i, 4 + j, (*chips[j], c), sibling)

    def start():
        for i in range(n):
            mine(i).start()
        for i in range(n):
            for cp in first(i):
                cp.start()

    def forward():
        for j, chip in enumerate(chips):
            for i in range(n):
                copy(i, 1 + j, (*chip, c), me).wait_recv()
                passed(i, j).start()

    def finish():
        for i in range(n):
            copy(i, 0, sibling, me).wait_recv()
            for j, chip in enumerate(chips):
                copy(i, 4 + j, (*chip, 1 - c), me).wait_recv()
        for i in range(n):
            for cp in first(i) + [passed(i, j) for j in range(3)]:
                cp.wait_send()
            mine(i).wait()

    return start, forward, finish


def _ag_scratch(n):
    return [pltpu.SemaphoreType.DMA((n, 7)), pltpu.SemaphoreType.DMA((n, 7)), pltpu.SemaphoreType.DMA((n,))]


def _all_gather(src, name):
    def body(src_ref, out_ref, send_sems, recv_sems, local_sem):
        x, y, c = _mesh_pos()

        def copy(k, block):
            px, py, pc = x ^ (k >> 2), y ^ ((k >> 1) & 1), c ^ (k & 1)
            bx, by, bc = (x, y, c) if block == "mine" else (px, py, pc)
            return pltpu.make_async_remote_copy(
                src_ref=src_ref, dst_ref=out_ref.at[4 * bx + 2 * by + bc],
                send_sem=send_sems.at[k - 1], recv_sem=recv_sems.at[k - 1],
                device_id=(px, py, pc), device_id_type=MESH)

        local = pltpu.make_async_copy(src_ref, out_ref.at[4 * x + 2 * y + c], local_sem.at[0])
        local.start()
        for k in range(1, N_DEV):
            copy(k, "mine").start()
        for k in range(1, N_DEV):
            copy(k, "mine").wait_send()
        for k in range(1, N_DEV):
            copy(k, "theirs").wait_recv()
        local.wait()

    any_spec = pl.BlockSpec(memory_space=pl.ANY)
    return pl.pallas_call(
        body, name=name,
        out_shape=jax.ShapeDtypeStruct((N_DEV,) + src.shape, src.dtype),
        in_specs=[pl.BlockSpec(memory_space=pltpu.VMEM)], out_specs=any_spec,
        scratch_shapes=[pltpu.SemaphoreType.DMA((N_DEV - 1,)), pltpu.SemaphoreType.DMA((N_DEV - 1,)),
                        pltpu.SemaphoreType.DMA((1,))],
    )(src)


def _rs_phases(shapes, src, dst, send_sems, recv_sems):
    x, y, c = _mesh_pos()

    def copies():
        out = []
        n = 0
        for i, shp in enumerate(shapes):
            for m in range(shp[0]):
                for k in range(1, N_DEV):
                    px, py, pc = x ^ (k >> 2), y ^ ((k >> 1) & 1), c ^ (k & 1)
                    out.append(pltpu.make_async_remote_copy(
                        src_ref=src[i].at[m, 4 * px + 2 * py + pc], dst_ref=dst[i].at[m, k - 1],
                        send_sem=send_sems.at[n], recv_sem=recv_sems.at[n],
                        device_id=(px, py, pc), device_id_type=MESH))
                    n += 1
        return out

    def start():
        for cp in copies():
            cp.start()

    def finish():
        for cp in copies():
            cp.wait_send()
        for cp in copies():
            cp.wait_recv()

    return start, finish


def _rs_out(sends):
    return [jax.ShapeDtypeStruct((s.shape[0], N_DEV - 1) + s.shape[2:], s.dtype) for s in sends]


def _rs_scratch(sends):
    total = sum((N_DEV - 1) * s.shape[0] for s in sends)
    return [pltpu.SemaphoreType.DMA((total,)), pltpu.SemaphoreType.DMA((total,))]


def _rs_final_adamw(mine, recv, slab, w, m, v, name, transpose=False):
    _, _, r, cdim = mine.shape
    tc = ADAMW_COL_TILE
    assert cdim % tc == 0 and w.shape == ((cdim, r) if transpose else (r, cdim)), (name, w.shape)
    x, y, c = _mesh_pos()
    me = jnp.reshape(4 * x + 2 * y + c, (1,)).astype(jnp.int32)

    def body(me_ref, p_ref, r_ref, w_ref, m_ref, v_ref, g_ref, d_ref, nm_ref, nv_ref):
        del me_ref
        g = p_ref[...]
        for k in range(N_DEV - 1):
            g = g + r_ref[k].astype(F32)
        if transpose:
            g = g.T
        g_ref[...] = g
        d_ref[...], nm_ref[...], nv_ref[...] = _adamw_math(w_ref[...], g, m_ref[...], v_ref[...])

    if transpose:
        w_spec = pl.BlockSpec((tc, r), lambda j, s: (j, 0))
    else:
        w_spec = pl.BlockSpec((r, tc), lambda j, s: (0, j))
    sds = jax.ShapeDtypeStruct(w.shape, F32)
    return pl.pallas_call(
        body, name=name, out_shape=(sds, sds, sds, sds),
        grid_spec=pltpu.PrefetchScalarGridSpec(
            num_scalar_prefetch=1, grid=(cdim // tc,),
            in_specs=[pl.BlockSpec((None, None, r, tc), lambda j, s: (slab, s[0], 0, j)),
                      pl.BlockSpec((None, N_DEV - 1, r, tc), lambda j, s: (slab, 0, 0, j)),
                      w_spec, w_spec, w_spec],
            out_specs=(w_spec, w_spec, w_spec, w_spec)),
        compiler_params=_params(),
    )(me, mine, recv, w, m, v)


def _matmul(a, b, mode, out_dtype, tm, tn, tk, name, bf16_copy=False, rs_sends=(), ag=None):
    ga = a.shape[0] if a.ndim == 3 else None
    gb = b.shape[0] if b.ndim == 3 else None
    a2, b2 = a.shape[-2:], b.shape[-2:]
    if mode == "nn":
        (m, k), n = a2, b2[1]
    elif mode == "nt":
        (m, k), n = a2, b2[0]
    else:
        (k, m), n = a2, b2[1]
    assert m % tm == 0 and n % tn == 0 and k % tk == 0, (name, m, n, k)
    nk = k // tk
    g_n = ga or 1
    batch_out = mode == "tn" and ga is not None
    n_red = nk if batch_out else nk * g_n
    dot = {"nn": _dot_nn, "nt": _dot_nt, "tn": _dot_tn}[mode]
    acc_in_out = out_dtype == F32

    n_rs = len(rs_sends)
    rs_shapes = [r.shape for r in rs_sends]
    ag_srcs, ag_out_shapes, ag_dests = ag if ag is not None else ((), (), ())
    n_ag, n_ag_out = len(ag_srcs), len(ag_out_shapes)
    n_out = 2 if bf16_copy else 1
    assert not bf16_copy or acc_in_out
    assert not (n_rs and n_ag)

    def body(a_ref, b_ref, *rest):
        rs_src, rest = rest[:n_rs], rest[n_rs:]
        ag_src, rest = rest[:n_ag], rest[n_ag:]
        o_ref = rest[0]
        copy_ref = rest[1] if bf16_copy else None
        rs_dst, rest = rest[n_out:n_out + n_rs], rest[n_out + n_rs:]
        ag_out, scratch = rest[:n_ag_out], rest[n_ag_out:]
        first = functools.reduce(jnp.logical_and, [pl.program_id(ax) == 0 for ax in range(4)])
        last = functools.reduce(jnp.logical_and, [pl.program_id(ax) == grid[ax] - 1 for ax in range(4)])
        if n_rs:
            rs_start, rs_finish = _rs_phases(rs_shapes, rs_src, rs_dst, *scratch[-2:])
            pl.when(first)(rs_start)
        if n_ag:
            ag_start, ag_forward, ag_finish = _ag_phases(ag_dests, ag_src, ag_out, *scratch[-3:])
            pl.when(first)(ag_start)
        p = dot(a_ref[...], b_ref[...])
        kk = pl.program_id(3) if batch_out else pl.program_id(2) * nk + pl.program_id(3)
        if n_red == 1:
            o_ref[...] = p.astype(out_dtype)
            if bf16_copy:
                copy_ref[...] = p.astype(BF16)
        else:
            acc = o_ref if acc_in_out else scratch[0]

            @pl.when(kk == 0)
            def _():
                acc[...] = p

            @pl.when(kk > 0)
            def _():
                acc[...] += p

            @pl.when(kk == n_red - 1)
            def _():
                if not acc_in_out:
                    o_ref[...] = acc[...].astype(out_dtype)
                if bf16_copy:
                    copy_ref[...] = acc[...].astype(BF16)

        if n_rs:
            pl.when(last)(rs_finish)
        if n_ag:
            @pl.when(last)
            def _():
                ag_forward()
                ag_finish()

    def order(ids):
        return ids if batch_out else (ids[2], ids[0], ids[1], ids[3])

    def a_idx(*ids):
        g, i, j, kq = order(ids)
        blk = {"nn": (i, kq), "nt": (i, kq), "tn": (kq, i)}[mode]
        return (g,) + blk if ga is not None else blk

    def b_idx(*ids):
        g, i, j, kq = order(ids)
        blk = {"nn": (kq, j), "nt": (j, kq), "tn": (kq, j)}[mode]
        return (g,) + blk if gb is not None else blk

    def o_idx(*ids):
        g, i, j, kq = order(ids)
        return (g, i, j) if batch_out else (i, j)

    a_blk = {"nn": (tm, tk), "nt": (tm, tk), "tn": (tk, tm)}[mode]
    b_blk = {"nn": (tk, tn), "nt": (tn, tk), "tn": (tk, tn)}[mode]
    if ga is not None:
        a_blk = (None,) + a_blk
    if gb is not None:
        b_blk = (None,) + b_blk
    if batch_out:
        out_shape = jax.ShapeDtypeStruct((g_n, m, n), out_dtype)
        o_blk = (None, tm, tn)
        grid = (g_n, m // tm, n // tn, nk)
    else:
        out_shape = jax.ShapeDtypeStruct((m, n), out_dtype)
        o_blk = (tm, tn)
        grid = (m // tm, n // tn, g_n, nk)
    scratch = [] if (acc_in_out or n_red == 1) else [pltpu.VMEM((tm, tn), F32)]
    any_spec = pl.BlockSpec(memory_space=pl.ANY)
    out_shapes = [out_shape] + ([jax.ShapeDtypeStruct(out_shape.shape, BF16)] if bf16_copy else [])
    res = pl.pallas_call(
        body, name=name, out_shape=tuple(out_shapes + _rs_out(rs_sends) + list(ag_out_shapes)), grid=grid,
        in_specs=[pl.BlockSpec(a_blk, a_idx), pl.BlockSpec(b_blk, b_idx)] + [any_spec] * (n_rs + n_ag),
        out_specs=tuple([pl.BlockSpec(o_blk, o_idx)] * n_out + [any_spec] * (n_rs + n_ag_out)),
        scratch_shapes=scratch + (_rs_scratch(rs_sends) if n_rs else []) + (_ag_scratch(n_ag) if n_ag else []),
        compiler_params=_params(),
    )(a, b, *rs_sends, *ag_srcs)
    return res if len(res) > 1 else res[0]


EW_TILE = 256
ROW_TILE = 512
EPILOGUE_CHUNKS = 8
MXU_WIDTH = 256


def _rms(v):
    return lax.rsqrt(jnp.mean(v * v, axis=-1, keepdims=True) + EPS)


def _rms_bwd(dhat, vh, r):
    return r * (dhat - vh * jnp.mean(dhat * vh, axis=-1, keepdims=True))


def _tok_spec(tm, d):
    return pl.BlockSpec((tm, d), lambda i: (i, 0))


def _vec_spec(d):
    return pl.BlockSpec((1, d), lambda i: (0, 0))


def _mod_spec(tiles_per_seq, d):
    return pl.BlockSpec((None, N_MOD, d), lambda i: (i // tiles_per_seq, 0, 0))


def _seq_acc_spec(tiles_per_seq, d):
    return pl.BlockSpec((None, 1, d), lambda i: (i // tiles_per_seq, 0, 0))


def _acc(ref, val, first):
    if first is False:
        ref[...] += val
        return

    @pl.when(first)
    def _():
        ref[...] = val

    @pl.when(jnp.logical_not(first))
    def _():
        ref[...] += val


def _colsum(v):
    return jnp.sum(v, axis=0, keepdims=True)


def _pre_mix(x2, g_pre, mod, seq, ag_srcs, ag_out_shapes, ag_dests):
    t, d = x2.shape
    tm = EW_TILE
    n_steps = t // tm
    n_ag, n_ag_out = len(ag_srcs), len(ag_out_shapes)

    def body(x_ref, g_ref, mod_ref, *rest):
        ag_src, ag_vmem, h_ref = rest[:n_ag], rest[n_ag:2 * n_ag], rest[2 * n_ag]
        ag_out, sems = rest[2 * n_ag + 1:2 * n_ag + 1 + n_ag_out], rest[2 * n_ag + 1 + n_ag_out:]
        ag_start, ag_forward, ag_finish = _ag_phases(ag_dests, ag_src, ag_out, *sems, local_src=ag_vmem)
        step = pl.program_id(0)
        pl.when(step == 0)(ag_start)
        xv = x_ref[...]
        n = xv * _rms(xv) * g_ref[...]
        h_ref[...] = (n * (1.0 + mod_ref[1:2, :]) + mod_ref[0:1, :]).astype(BF16)

        @pl.when(step == n_steps - 1)
        def _():
            ag_forward()
            ag_finish()

    any_spec = pl.BlockSpec(memory_space=pl.ANY)
    whole = [pl.BlockSpec(s.shape, lambda i, nd=s.ndim: (0,) * nd) for s in ag_srcs]
    return pl.pallas_call(
        body, name="pre_mix", out_shape=(jax.ShapeDtypeStruct((t, d), BF16), *ag_out_shapes), grid=(n_steps,),
        in_specs=[_tok_spec(tm, d), _vec_spec(d), _mod_spec(seq // tm, d)] + [any_spec] * n_ag + whole,
        out_specs=(_tok_spec(tm, d), *([any_spec] * n_ag_out)),
        scratch_shapes=_ag_scratch(n_ag), compiler_params=_params(),
    )(x2, g_pre, mod, *ag_srcs, *ag_srcs)


def _matmul_rows(a, b, tm, seq, name, epilogue, ep_in, ep_in_kinds, ep_out, ep_out_kinds, rs_sends=()):
    g_n = a.shape[0] if a.ndim == 3 else None
    (m, k), n = a.shape[-2:], b.shape[-1]
    tps = seq // tm
    n_i = m // tm
    n_rs = len(rs_sends)
    rs_shapes = [r.shape for r in rs_sends]
    n_in, n_out = len(ep_in), len(ep_out)
    n_cols = n // MXU_WIDTH
    rc, cw = tm // EPILOGUE_CHUNKS, n // n_cols

    def prev(i):
        return jnp.maximum(i - 1, 0)

    def spec(kind):
        return {"tok": pl.BlockSpec((tm, n), lambda i: (prev(i), 0)),
                "vec": pl.BlockSpec((1, n), lambda i: (0, 0)),
                "mod": pl.BlockSpec((None, N_MOD, n), lambda i: (prev(i) // tps, 0, 0)),
                "seq": pl.BlockSpec((None, 1, n), lambda i: (prev(i) // tps, 0, 0)),
                "loss": pl.BlockSpec((1, LANES), lambda i: (0, 0))}[kind]

    def body(a_ref, b_ref, *rest):
        in_refs, rest = rest[:n_in], rest[n_in:]
        rs_src, rest = rest[:n_rs], rest[n_rs:]
        out_refs, rest = rest[:n_out], rest[n_out:]
        rs_dst, rest = rest[:n_rs], rest[n_rs:]
        fin = rest[0]
        i = pl.program_id(0)
        if n_rs:
            rs_start, rs_finish = _rs_phases(rs_shapes, rs_src, rs_dst, *rest[1:])
            pl.when(i == 0)(rs_start)

        def product(cols):
            if g_n is None:
                return _dot_nn(a_ref[...], b_ref[:, cols])
            p = _dot_nn(a_ref[0], b_ref[0, :, cols])
            for g in range(1, g_n):
                p = p + _dot_nn(a_ref[g], b_ref[g, :, cols])
            return p

        def step(with_epilogue, with_matmul):
            parts = []
            for c in range(EPILOGUE_CHUNKS):
                if with_epilogue:
                    rows = pl.ds(c * rc, rc)
                    epilogue(fin[rows, :], i - 1, tps, in_refs, out_refs, rows, c)
                while with_matmul and len(parts) < (c + 1) * n_cols // EPILOGUE_CHUNKS:
                    cols = slice(len(parts) * cw, (len(parts) + 1) * cw)
                    parts.append((cols, product(cols)))
            for cols, v in parts:
                fin[:, cols] = v

        pl.when(i == 0)(functools.partial(step, False, True))
        pl.when(jnp.logical_and(i > 0, i < n_i))(functools.partial(step, True, True))
        pl.when(i == n_i)(functools.partial(step, True, False))

        if n_rs:
            pl.when(i == n_i)(rs_finish)

    def row(i):
        return jnp.minimum(i, n_i - 1)

    if g_n is None:
        a_spec = pl.BlockSpec((tm, k), lambda i: (row(i), 0))
        b_spec = pl.BlockSpec(b.shape, lambda i: (0, 0), pipeline_mode=pl.Buffered(1))
    else:
        a_spec = pl.BlockSpec((g_n, tm, k), lambda i: (0, row(i), 0))
        b_spec = pl.BlockSpec(b.shape, lambda i: (0, 0, 0), pipeline_mode=pl.Buffered(1))
    any_spec = pl.BlockSpec(memory_space=pl.ANY)
    res = pl.pallas_call(
        body, name=name, grid=(n_i + 1,), out_shape=tuple(list(ep_out) + _rs_out(rs_sends)),
        in_specs=[a_spec, b_spec] + [spec(kd) for kd in ep_in_kinds] + [any_spec] * n_rs,
        out_specs=tuple([spec(kd) for kd in ep_out_kinds] + [any_spec] * n_rs),
        scratch_shapes=[pltpu.VMEM((tm, n), F32)] + (_rs_scratch(rs_sends) if n_rs else []),
        compiler_params=_params(),
    )(a, b, *ep_in, *rs_sends)
    return res


def _first(cond, chunk):
    return cond if chunk == 0 else False


def _mid_epilogue(mv, i, tps, in_refs, out_refs, rows, chunk):
    x_ref, gpost_ref, gpre_ref, mod_ref = in_refs
    mix_ref, x1_ref, h2_ref = out_refs
    mix_ref[rows, :] = mv
    x1 = x_ref[rows, :] + mod_ref[2:3, :] * (mv * _rms(mv) * gpost_ref[...])
    x1_ref[rows, :] = x1
    n = x1 * _rms(x1) * gpre_ref[...]
    h2_ref[rows, :] = (n * (1.0 + mod_ref[4:5, :]) + mod_ref[3:4, :]).astype(BF16)


def _post_epilogue(fv, i, tps, in_refs, out_refs, rows, chunk):
    x1_ref, tgt_ref, g_ref, mod_ref = in_refs
    loss_ref, dy_ref, df_ref, dgate_ref, gg_ref = out_refs
    d = fv.shape[1]
    r = _rms(fv)
    fh = fv * r
    nf = fh * g_ref[...]
    gate = mod_ref[5:6, :]
    err = x1_ref[rows, :] + gate * nf - tgt_ref[rows, :]
    _acc(loss_ref, jnp.sum(_colsum(err * err), axis=1, keepdims=True) * jnp.ones((1, LANES), F32),
         _first(i == 0, chunk))
    dy = err * (1.0 / d)
    dy_ref[rows, :] = dy
    _acc(dgate_ref, _colsum(dy * nf), _first(i % tps == 0, chunk))
    dn = dy * gate
    _acc(gg_ref, _colsum(dn * fh), _first(i == 0, chunk))
    df_ref[rows, :] = _rms_bwd(dn * g_ref[...], fh, r).astype(BF16)


def _bwd_mid_epilogue(dh, i, tps, in_refs, out_refs, rows, chunk):
    dy_ref, x1_ref, mix_ref, gpre_ref, gpost_ref, mod_ref = in_refs
    dx1_ref, dmix_ref, dshift_ref, dscale_ref, dgate_ref, ggpre_ref, ggpost_ref = out_refs
    seq_first, first = _first(i % tps == 0, chunk), _first(i == 0, chunk)
    x1 = x1_ref[rows, :]
    r = _rms(x1)
    xh = x1 * r
    gpre = gpre_ref[...]
    _acc(dshift_ref, _colsum(dh), seq_first)
    _acc(dscale_ref, _colsum(dh * xh * gpre), seq_first)
    dn = dh * (1.0 + mod_ref[4:5, :])
    _acc(ggpre_ref, _colsum(dn * xh), first)
    dx1 = dy_ref[rows, :] + _rms_bwd(dn * gpre, xh, r)
    dx1_ref[rows, :] = dx1
    mv = mix_ref[rows, :]
    rm = _rms(mv)
    mh = mv * rm
    gpost = gpost_ref[...]
    _acc(dgate_ref, _colsum(dx1 * mh * gpost), seq_first)
    dnm = dx1 * mod_ref[2:3, :]
    _acc(ggpost_ref, _colsum(dnm * mh), first)
    dmix_ref[rows, :] = _rms_bwd(dnm * gpost, mh, rm).astype(BF16)


def _bwd_pre_epilogue(dh, i, tps, in_refs, out_refs, rows, chunk):
    dx1_ref, x_ref, g_ref, mod_ref = in_refs
    gx_ref, dshift_ref, dscale_ref, gg_ref = out_refs
    seq_first = _first(i % tps == 0, chunk)
    xv = x_ref[rows, :]
    r = _rms(xv)
    xh = xv * r
    g = g_ref[...]
    _acc(dshift_ref, _colsum(dh), seq_first)
    _acc(dscale_ref, _colsum(dh * xh * g), seq_first)
    dn = dh * (1.0 + mod_ref[1:2, :])
    _acc(gg_ref, _colsum(dn * xh), _first(i == 0, chunk))
    gx_ref[rows, :] = dx1_ref[rows, :] + _rms_bwd(dn * g, xh, r)


def _ffn_up(h2, wgu, tm, tn):
    t, d = h2.shape
    f = wgu.shape[1]

    def body(h_ref, w_ref, gu_ref, act_ref):
        h = h_ref[...]
        g = _dot_nt(h, w_ref[0])
        u = _dot_nt(h, w_ref[1])
        gu_ref[0] = g.astype(BF16)
        gu_ref[1] = u.astype(BF16)
        act_ref[...] = (g * jax.nn.sigmoid(g) * u).astype(BF16)

    return pl.pallas_call(
        body, name="ffn_up", grid=(f // tn, t // tm),
        out_shape=(jax.ShapeDtypeStruct((2, t, f), BF16), jax.ShapeDtypeStruct((t, f), BF16)),
        in_specs=[pl.BlockSpec((tm, d), lambda j, i: (i, 0)), pl.BlockSpec((2, tn, d), lambda j, i: (0, j, 0))],
        out_specs=(pl.BlockSpec((2, tm, tn), lambda j, i: (0, i, j)), pl.BlockSpec((tm, tn), lambda j, i: (i, j))),
        compiler_params=_params(),
    )(h2, wgu)


def _ffn_act_bwd(df, wd, gu, tm, tn):
    t, d = df.shape
    f = wd.shape[0]

    def body(df_ref, w_ref, gu_ref, dgu_ref):
        da = _dot_nt(df_ref[...], w_ref[...])
        g = gu_ref[0].astype(F32)
        u = gu_ref[1].astype(F32)
        s = jax.nn.sigmoid(g)
        silu = g * s
        dgu_ref[0] = (da * u * (s + silu * (1.0 - s))).astype(BF16)
        dgu_ref[1] = (da * silu).astype(BF16)

    return pl.pallas_call(
        body, name="ffn_act_bwd", grid=(f // tn, t // tm),
        out_shape=jax.ShapeDtypeStruct((2, t, f), BF16),
        in_specs=[pl.BlockSpec((tm, d), lambda j, i: (i, 0)), pl.BlockSpec((tn, d), lambda j, i: (j, 0)),
                  pl.BlockSpec((2, tm, tn), lambda j, i: (0, i, j))],
        out_specs=pl.BlockSpec((2, tm, tn), lambda j, i: (0, i, j)),
        compiler_params=_params(),
    )(df, wd, gu)


SIGN_BIT = 0x80000000
Q_SCALE = 1.0 / math.sqrt(HEAD_DIM)


def _softplus(z):
    neg_abs = lax.bitcast_convert_type(lax.bitcast_convert_type(z, jnp.uint32) | jnp.uint32(SIGN_BIT), F32)
    return jnp.maximum(z, 0.0) + jnp.log(1.0 + jnp.exp(neg_abs))


def _hi_lo(v):
    hi = v.astype(BF16)
    return jnp.concatenate([hi, (v - hi.astype(F32)).astype(BF16)], axis=1)


def _emit_skewed(chains, lag=1):
    for t in range(max(len(ch) for ch in chains) + lag * (len(chains) - 1)):
        for c, ch in enumerate(chains):
            if 0 <= t - lag * c < len(ch):
                ch[t - lag * c]()


def _fwd_chain(blk, qs, k_ref, v_ref, c0, kb, cols, mask, ntri, lane, tq):
    st = {}

    def scores():
        st["z"] = _dot_nt(qs, k_ref[pl.ds(c0, tq), cols])

    def soft():
        sp = _softplus(st["z"])
        if mask is not None:
            sp = jnp.where(mask, sp, 0.0)
        st["parts"] = _hi_lo(sp)
        st["cur"] = blk["cur"]
        blk["cm"] = jnp.where(lane == kb, blk["cur"], blk["cm"])
        blk["cur"] = blk["cur"] - jnp.sum(sp, axis=1, keepdims=True)

    def sums():
        st["s"] = _dot_nn(st["parts"], ntri)

    def weights():
        w = jnp.exp(st["z"] + st["s"] + st["cur"])
        if mask is not None:
            w = jnp.where(mask, w, 0.0)
        st["w"] = w.astype(BF16)

    def out():
        p = _dot_nn(st["w"], v_ref[pl.ds(c0, tq), cols])
        blk["pv"] = p if blk["pv"] is None else blk["pv"] + p

    return [scores, soft, sums, weights, out]


def _bwd_chain(blk, qs, dos, cs, k_ref, v_ref, dk_ref, dv_ref, c0, kb, cols, mask, ntri, tri_i, lane, tq):
    st = {}

    def scores():
        st["z"] = _dot_nt(qs, k_ref[pl.ds(c0, tq), cols])
        st["dw"] = _dot_nt(dos, v_ref[pl.ds(c0, tq), cols])

    def soft():
        sp = _softplus(st["z"])
        if mask is not None:
            sp = jnp.where(mask, sp, 0.0)
        st["sp"] = sp
        st["parts"] = _hi_lo(sp)
        st["cur"] = jnp.sum(jnp.where(lane == kb, cs, 0.0), axis=1, keepdims=True)

    def sums():
        st["s"] = _dot_nn(st["parts"], ntri)

    def weights():
        w = jnp.exp(st["z"] + st["s"] + st["cur"])
        if mask is not None:
            w = jnp.where(mask, w, 0.0)
        ee = w * st["dw"]
        st["w"], st["ee"], st["ec"] = w.astype(BF16), ee, blk["ec"]
        blk["ec"] = blk["ec"] + jnp.sum(ee, axis=1, keepdims=True)

    def prefix():
        st["einc"] = _dot_nn(st["ee"].astype(BF16), tri_i)

    def dz():
        v = st["ee"] - jnp.exp(st["z"] - st["sp"]) * (st["einc"] + st["ec"])
        if mask is not None:
            v = jnp.where(mask, v, 0.0)
        st["dz"] = v.astype(BF16)

    def grads():
        p = _dot_nn(st["dz"], k_ref[pl.ds(c0, tq), cols])
        blk["dq"] = p if blk["dq"] is None else blk["dq"] + p
        dk_ref[pl.ds(c0, tq), :] += _dot_tn(st["dz"], qs)
        dv_ref[pl.ds(c0, tq), :] += _dot_tn(st["w"], dos)

    return [scores, soft, sums, weights, prefix, dz, grads]


def _stack_heads(v, lane, scale=None):
    if scale is not None:
        v = v * jnp.asarray(scale, v.dtype)
    zero = jnp.zeros_like(v)
    return jnp.concatenate([jnp.where(lane < HEAD_DIM, v, zero), jnp.where(lane >= HEAD_DIM, v, zero)], axis=0)


def _diag_mask(tq):
    row = lax.broadcasted_iota(jnp.int32, (2 * tq, tq), 0)
    col = lax.broadcasted_iota(jnp.int32, (2 * tq, tq), 1)
    return col < jnp.where(row >= tq, row - tq, row)


def _attn_fwd(proj, tri_after, n_seq, seq, ag_srcs, ag_out_shapes, ag_dests):
    t = proj.shape[0]
    tq = ATT_TILE
    npp = ATT_PAIRS
    n_blk = (proj.shape[1] // 4) // (npp * LANES)
    n_ag, n_ag_out = len(ag_srcs), len(ag_out_shapes)
    n_steps = n_seq * n_blk

    def body(q_ref, k_ref, v_ref, tri_ref, *rest):
        ag_src, rest = rest[:n_ag], rest[n_ag:]
        o_ref, cs_ref = rest[:2]
        ag_out, rest = rest[2:2 + n_ag_out], rest[2 + n_ag_out:]
        oacc, cmat, carry = rest[:3]
        ag_start, ag_forward, ag_finish = _ag_phases(ag_dests, ag_src, ag_out, *rest[3:])
        step = pl.program_id(0) * n_blk + pl.program_id(1)
        pl.when(step == 0)(ag_start)
        pl.when(step == (3 * n_steps) // 4)(ag_forward)
        lane = lax.broadcasted_iota(jnp.int32, (1, LANES), 1)
        ntri = tri_ref[...]
        diag = _diag_mask(tq)

        def q_tile(qi, _):
            r0 = pl.multiple_of(qi * tq, tq)
            qs = [_stack_heads(q_ref[pl.ds(r0, tq), pp * LANES:(pp + 1) * LANES], lane, Q_SCALE)
                  for pp in range(npp)]
            carry[...] = jnp.zeros_like(carry)
            cmat[...] = jnp.zeros_like(cmat)
            oacc[...] = jnp.zeros_like(oacc)

            def run_tiles(tiles):
                blocks = [dict(cur=carry[pp], cm=cmat[pp], pv=None) for pp in range(npp)]
                chains = []
                for kb, mask in tiles:
                    c0 = pl.multiple_of(kb * tq, tq)
                    for pp in range(npp):
                        chains.append(_fwd_chain(blocks[pp], qs[pp], k_ref, v_ref, c0, kb,
                                                 slice(pp * LANES, (pp + 1) * LANES), mask, ntri, lane, tq))
                _emit_skewed(chains)
                for pp in range(npp):
                    oacc[pp] += blocks[pp]["pv"]
                    cmat[pp] = blocks[pp]["cm"]
                    carry[pp] = blocks[pp]["cur"]

            odd = qi % 2

            @pl.when(odd == 0)
            def _():
                run_tiles([(qi, diag)])

            @pl.when(odd == 1)
            def _():
                run_tiles([(qi, diag), (qi - 1, None)])

            def pair(j, _):
                kb = qi - 1 - odd - 2 * j
                run_tiles([(kb, None), (kb - 1, None)])
                return 0

            lax.fori_loop(0, qi // 2, pair, 0)
            for pp in range(npp):
                c_off = 2 * pp * LANES
                cs_ref[pl.ds(r0, tq), c_off:c_off + LANES] = cmat[pp, 0:tq, :]
                cs_ref[pl.ds(r0, tq), c_off + LANES:c_off + 2 * LANES] = cmat[pp, tq:2 * tq, :]
                o_ref[pl.ds(r0, tq), pp * LANES:(pp + 1) * LANES] = jnp.where(
                    lane < HEAD_DIM, oacc[pp, 0:tq, :], oacc[pp, tq:2 * tq, :]).astype(BF16)
            return 0

        lax.fori_loop(0, seq // tq, q_tile, 0)
        pl.when(step == n_steps - 1)(ag_finish)

    wid = npp * LANES
    blk = lambda off: pl.BlockSpec((seq, wid), lambda b, p: (b, off + p))
    any_spec = pl.BlockSpec(memory_space=pl.ANY)
    return pl.pallas_call(
        body, name="attn_fwd", grid=(n_seq, n_blk),
        out_shape=(jax.ShapeDtypeStruct((2, t, n_blk * wid), BF16),
                   jax.ShapeDtypeStruct((t, n_blk * 2 * wid), F32), *ag_out_shapes),
        in_specs=[blk(0), blk(n_blk), blk(2 * n_blk), pl.BlockSpec((2 * tq, tq), lambda b, p: (0, 0))]
        + [any_spec] * n_ag,
        out_specs=(pl.BlockSpec((None, seq, wid), lambda b, p: (0, b, p)),
                   pl.BlockSpec((seq, 2 * wid), lambda b, p: (b, p)), *([any_spec] * n_ag_out)),
        scratch_shapes=[pltpu.VMEM((npp, 2 * tq, LANES), F32), pltpu.VMEM((npp, 2 * tq, LANES), F32),
                        pltpu.VMEM((npp, 2 * tq, 1), F32)] + _ag_scratch(n_ag),
        compiler_params=_params(),
    )(proj, proj, proj, tri_after, *ag_srcs)


def _attn_bwd(proj, dcat, cstats, tri_after, tri_incl, n_seq, seq, rs_sends):
    t = proj.shape[0]
    tq = ATT_TILE
    npp = ATT_PAIRS
    width = proj.shape[1] // 4
    n_blk = width // (npp * LANES)
    n_rs = len(rs_sends)
    rs_shapes = [r.shape for r in rs_sends]
    n_steps = n_seq * n_blk

    def body(q_ref, k_ref, v_ref, do_ref, cs_ref, tria_ref, trii_ref, *rest):
        rs_src, rest = rest[:n_rs], rest[n_rs:]
        out_ref = rest[0]
        rs_dst, rest = rest[1:1 + n_rs], rest[1 + n_rs:]
        dq_acc, dk_acc, dv_acc, ecarry = rest[:4]
        rs_start, rs_finish = _rs_phases(rs_shapes, rs_src, rs_dst, *rest[4:])
        step = pl.program_id(0) * n_blk + pl.program_id(1)
        pl.when(step == 0)(rs_start)
        lane = lax.broadcasted_iota(jnp.int32, (1, LANES), 1)
        ntri = tria_ref[...]
        tri_i = trii_ref[...]
        diag = _diag_mask(tq)
        dk_acc[...] = jnp.zeros_like(dk_acc)
        dv_acc[...] = jnp.zeros_like(dv_acc)

        def q_tile(qi, _):
            r0 = pl.multiple_of(qi * tq, tq)
            qs, dos, cs = [], [], []
            for pp in range(npp):
                cols = slice(pp * LANES, (pp + 1) * LANES)
                qs.append(_stack_heads(q_ref[pl.ds(r0, tq), cols], lane, Q_SCALE))
                dos.append(_stack_heads(do_ref[pl.ds(r0, tq), cols], lane))
                c_off = 2 * pp * LANES
                cs.append(jnp.concatenate([cs_ref[pl.ds(r0, tq), c_off:c_off + LANES],
                                           cs_ref[pl.ds(r0, tq), c_off + LANES:c_off + 2 * LANES]], axis=0))
            ecarry[...] = jnp.zeros_like(ecarry)
            dq_acc[...] = jnp.zeros_like(dq_acc)

            def run_tiles(tiles):
                blocks = [dict(ec=ecarry[pp], dq=None) for pp in range(npp)]
                chains = []
                for kb, mask in tiles:
                    c0 = pl.multiple_of(kb * tq, tq)
                    for pp in range(npp):
                        chains.append(_bwd_chain(
                            blocks[pp], qs[pp], dos[pp], cs[pp], k_ref, v_ref, dk_acc.at[pp], dv_acc.at[pp],
                            c0, kb, slice(pp * LANES, (pp + 1) * LANES), mask, ntri, tri_i, lane, tq))
                _emit_skewed(chains)
                for pp in range(npp):
                    dq_acc[pp] += blocks[pp]["dq"]
                    ecarry[pp] = blocks[pp]["ec"]

            def pair(j, _):
                run_tiles([(2 * j, None), (2 * j + 1, None)])
                return 0

            lax.fori_loop(0, qi // 2, pair, 0)
            odd = qi % 2

            @pl.when(odd == 0)
            def _():
                run_tiles([(qi, diag)])

            @pl.when(odd == 1)
            def _():
                run_tiles([(qi - 1, None), (qi, diag)])

            for pp in range(npp):
                dq = jnp.where(lane < HEAD_DIM, dq_acc[pp, 0:tq, :], dq_acc[pp, tq:2 * tq, :])
                out_ref[0, pl.ds(r0, tq), pp * LANES:(pp + 1) * LANES] = (dq * Q_SCALE).astype(BF16)
            return 0

        lax.fori_loop(0, seq // tq, q_tile, 0)
        for pp in range(npp):
            cols = slice(pp * LANES, (pp + 1) * LANES)
            out_ref[1, :, cols] = dk_acc[pp].astype(BF16)
            out_ref[2, :, cols] = dv_acc[pp].astype(BF16)
        pl.when(step == n_steps - 1)(rs_finish)

    wid = npp * LANES
    blk = lambda off: pl.BlockSpec((seq, wid), lambda b, p: (b, off + p))
    tri_spec = pl.BlockSpec((2 * tq, tq), lambda b, p: (0, 0))
    any_spec = pl.BlockSpec(memory_space=pl.ANY)
    return pl.pallas_call(
        body, name="attn_bwd", grid=(n_seq, n_blk),
        out_shape=(jax.ShapeDtypeStruct((4, t, width), BF16), *_rs_out(rs_sends)),
        in_specs=[blk(0), blk(n_blk), blk(2 * n_blk), pl.BlockSpec((seq, wid), lambda b, p: (b, p)),
                  pl.BlockSpec((seq, 2 * wid), lambda b, p: (b, p)), tri_spec,
                  pl.BlockSpec((tq, tq), lambda b, p: (0, 0))] + [any_spec] * n_rs,
        out_specs=(pl.BlockSpec((3, seq, wid), lambda b, p: (0, b, p)), *([any_spec] * n_rs)),
        scratch_shapes=[pltpu.VMEM((npp, 2 * tq, LANES), F32), pltpu.VMEM((npp, seq, LANES), F32),
                        pltpu.VMEM((npp, seq, LANES), F32), pltpu.VMEM((npp, 2 * tq, 1), F32)]
        + _rs_scratch(rs_sends),
        compiler_params=_params(),
    )(proj, proj, proj, dcat, cstats, tri_after, tri_incl, *rs_sends)


def _window_sum(v, g, rows, forward):
    s_len = v.shape[0]
    s = v
    for step in range(g + 1):
        sh = 1 << step
        if forward:
            s = s + jnp.where(rows < s_len - sh, pltpu.roll(s, s_len - sh, axis=0), 0.0)
        else:
            s = s + jnp.where(rows >= sh, pltpu.roll(s, sh, axis=0), 0.0)
    return s


def _window_count(g, rows):
    return jnp.minimum(rows + 1, POOL_WINDOWS[g]).astype(F32)


def _pooled(u, g, rows):
    return _window_sum(u, g, rows, forward=False) / _window_count(g, rows) - u


def _group_cols(g):
    return slice(g * POOL_GROUP_DIM, (g + 1) * POOL_GROUP_DIM)


def _pool_fwd(proj, w_pool, pool_scale, cat, n_seq, seq):
    n_grp = len(POOL_WINDOWS)
    width = n_grp * POOL_GROUP_DIM
    assert [1 << (g + 1) for g in range(n_grp)] == list(POOL_WINDOWS)

    def body(u_ref, w_ref, s_ref, alias_ref, o_ref):
        del alias_ref
        rows = lax.broadcasted_iota(jnp.int32, (seq, 1), 0)
        for g in range(n_grp):
            cols = _group_cols(g)
            pooled = _pooled(u_ref[:, cols].astype(F32), g, rows)
            y = _dot_nn(pooled.astype(BF16), w_ref[g].astype(BF16))
            o_ref[:, cols] = (y * s_ref[:, cols]).astype(BF16)

    return pl.pallas_call(
        body, name="pool_fwd", grid=(n_seq,),
        out_shape=jax.ShapeDtypeStruct(cat.shape, BF16),
        in_specs=[pl.BlockSpec((seq, width), lambda b: (b, 3)),
                  pl.BlockSpec((n_grp, POOL_GROUP_DIM, POOL_GROUP_DIM), lambda b: (0, 0, 0)),
                  pl.BlockSpec((1, width), lambda b: (0, 0)),
                  pl.BlockSpec(memory_space=pl.ANY)],
        out_specs=pl.BlockSpec((None, seq, width), lambda b: (1, b, 0)),
        input_output_aliases={3: 0},
        compiler_params=_params(),
    )(proj, w_pool, pool_scale, cat)


def _pool_bwd(proj, dcat, w_pool, pool_scale, dqkv, n_seq, seq):
    n_grp = len(POOL_WINDOWS)
    width = n_grp * POOL_GROUP_DIM

    def body(u_ref, dp_ref, w_ref, s_ref, alias_ref, du_ref, gw_ref, gs_ref):
        del alias_ref
        b = pl.program_id(0)
        rows = lax.broadcasted_iota(jnp.int32, (seq, 1), 0)
        for g in range(n_grp):
            cols = _group_cols(g)
            pb = _pooled(u_ref[:, cols].astype(F32), g, rows).astype(BF16)
            wb = w_ref[g].astype(BF16)
            z = _dot_nn(pb, wb)
            dp = dp_ref[:, cols].astype(F32)
            _acc(gs_ref.at[:, cols], _colsum(dp * z), b == 0)
            dys = (dp * s_ref[:, cols]).astype(BF16)
            _acc(gw_ref.at[g], _dot_tn(pb, dys), b == 0)
            dpooled = _dot_nt(dys, wb)
            du = _window_sum(dpooled / _window_count(g, rows), g, rows, forward=True) - dpooled
            du_ref[:, cols] = du.astype(BF16)

    return pl.pallas_call(
        body, name="pool_bwd", grid=(n_seq,),
        out_shape=(jax.ShapeDtypeStruct(dqkv.shape, BF16),
                   jax.ShapeDtypeStruct((n_grp, POOL_GROUP_DIM, POOL_GROUP_DIM), F32),
                   jax.ShapeDtypeStruct((1, width), F32)),
        in_specs=[pl.BlockSpec((seq, width), lambda b: (b, 3)),
                  pl.BlockSpec((seq, width), lambda b: (b, 1)),
                  pl.BlockSpec((n_grp, POOL_GROUP_DIM, POOL_GROUP_DIM), lambda b: (0, 0, 0)),
                  pl.BlockSpec((1, width), lambda b: (0, 0)),
                  pl.BlockSpec(memory_space=pl.ANY)],
        out_specs=(pl.BlockSpec((None, seq, width), lambda b: (3, b, 0)),
                   pl.BlockSpec((n_grp, POOL_GROUP_DIM, POOL_GROUP_DIM), lambda b: (0, 0, 0)),
                   pl.BlockSpec((1, width), lambda b: (0, 0))),
        input_output_aliases={4: 0},
        compiler_params=_params(),
    )(proj, dcat, w_pool, pool_scale, dqkv)


def _cond_fwd(c_all, w_cond, b_cols):
    n, _ = c_all.shape
    cols = w_cond.shape[1]

    def body(c_ref, w_ref, b_ref, o_ref):
        cv = c_ref[...]
        a = cv * jax.nn.sigmoid(cv)
        o_ref[...] = jnp.dot(a, w_ref[...], preferred_element_type=F32,
                             precision=lax.Precision.HIGHEST) + b_ref[...]

    return pl.pallas_call(
        body, name="cond_fwd", out_shape=jax.ShapeDtypeStruct((n, cols), F32),
        compiler_params=_params(),
    )(c_all, w_cond, b_cols)


def _cond_bwd_adamw(c_all, dmod_all, dmod_cols, w, m_w, v_w, b, m_b, v_b):
    def body(c_ref, dm_ref, dmc_ref, w_ref, mw_ref, vw_ref, b_ref, mb_ref, vb_ref,
             gw_ref, dw_ref, nmw_ref, nvw_ref, gb_ref, db_ref, nmb_ref, nvb_ref):
        cv = c_ref[...]
        a = cv * jax.nn.sigmoid(cv)
        gw = lax.dot_general(a, dmc_ref[...], (((0,), (0,)), ((), ())),
                             preferred_element_type=F32, precision=lax.Precision.HIGHEST)
        gw_ref[...] = gw
        dw_ref[...], nmw_ref[...], nvw_ref[...] = _adamw_math(w_ref[...], gw, mw_ref[...], vw_ref[...])
        gb = _colsum(dm_ref[...])
        gb_ref[...] = gb
        db_ref[...], nmb_ref[...], nvb_ref[...] = _adamw_math(b_ref[...], gb, mb_ref[...], vb_ref[...])

    w_sds, b_sds = jax.ShapeDtypeStruct(w.shape, F32), jax.ShapeDtypeStruct(b.shape, F32)
    outs = pl.pallas_call(
        body, name="cond_bwd_adamw", out_shape=(w_sds,) * 4 + (b_sds,) * 4, compiler_params=_params(),
    )(c_all, dmod_all, dmod_cols, w, m_w, v_w, b, m_b, v_b)
    return outs[:4], outs[4:]


def _adamw_math(w, g, m, v):
    m = ADAM_B1 * m + (1.0 - ADAM_B1) * g
    v = ADAM_B2 * v + (1.0 - ADAM_B2) * (g * g)
    m_hat = m / (1.0 - ADAM_B1 ** ADAM_STEP)
    v_hat = v / (1.0 - ADAM_B2 ** ADAM_STEP)
    delta = -ADAM_LR * (m_hat / (jnp.sqrt(v_hat) + ADAM_EPS) + ADAM_WD * w)
    return delta, m, v


def _adamw_small(ws, gparts, ms, vs, name):
    n = len(ws)

    def body(*refs):
        w_r, g_r, m_r, v_r = refs[:n], refs[n:2 * n], refs[2 * n:3 * n], refs[3 * n:4 * n]
        outs = refs[4 * n:]
        for i in range(n):
            g = g_r[i][0]
            for dev in range(1, g_r[i].shape[0]):
                g = g + g_r[i][dev]
            delta, m, v = _adamw_math(w_r[i][...], g, m_r[i][...], v_r[i][...])
            outs[i][...] = g
            outs[n + i][...] = delta
            outs[2 * n + i][...] = m
            outs[3 * n + i][...] = v

    sds = [jax.ShapeDtypeStruct(w.shape, F32) for w in ws]
    return pl.pallas_call(
        body, name=name, out_shape=tuple(sds * 4), compiler_params=_params(),
    )(*ws, *gparts, *ms, *vs)


def kernel(x, c, w_cond, b_cond, g_mix_pre, g_mix_post, w_in, w_pool, pool_scale, w_out, g_ffn_pre, g_ffn_post, w_gate, w_up, w_down, loss_target, m_w_cond, m_b_cond, m_g_mix_pre, m_g_mix_post, m_w_in, m_w_pool, m_pool_scale, m_w_out, m_g_ffn_pre, m_g_ffn_post, m_w_gate, m_w_up, m_w_down, v_w_cond, v_b_cond, v_g_mix_pre, v_g_mix_post, v_w_in, v_w_pool, v_pool_scale, v_w_out, v_g_ffn_pre, v_g_ffn_post, v_w_gate, v_w_up, v_w_down):
    n_seq, seq, d = x.shape
    t = n_seq * seq
    xi, yi, ci = _mesh_pos()
    me = 4 * xi + 2 * yi + ci
    x2 = x.reshape(t, d)
    tgt2 = loss_target.reshape(t, d)
    in_rows = w_in.shape[2]
    out_rows = w_out.shape[1]
    ff_rows = w_gate.shape[2]
    ff = N_DEV * ff_rows
    cond_cols = w_cond.shape[2]

    win_t = w_in[0].T.astype(BF16)
    wout_s = w_out[0].astype(BF16)
    wg_t = w_gate[0].T.astype(BF16)
    wu_t = w_up[0].T.astype(BF16)
    wd_s = w_down[0].astype(BF16)
    c_all = _all_gather(c, "ag_c").reshape(N_DEV * n_seq, d)

    b_cols = lax.dynamic_slice_in_dim(b_cond, me * cond_cols, cond_cols, axis=1)
    mod_cols = _cond_fwd(c_all, w_cond[0], b_cols)
    mod_g = _all_gather(mod_cols, "ag_mod")
    mod_mine = lax.dynamic_slice_in_dim(mod_g, me * n_seq, n_seq, axis=1)
    mod = jnp.transpose(mod_mine, (1, 0, 2)).reshape(n_seq, N_MOD, d)

    h1, win_g = _pre_mix(x2, g_mix_pre, mod, seq, [win_t],
                         [jax.ShapeDtypeStruct((N_DEV, in_rows, d), BF16)], [(0, ())])
    win_full = win_g.reshape(N_DEV * in_rows, d)
    proj = _matmul(h1, win_full, "nt", BF16, 512, N_DEV * in_rows, d, "proj")
    tq = ATT_TILE
    ids = jnp.arange(tq)
    tri_after = jnp.tile(-(ids[:, None] >= ids[None, :]).astype(BF16), (2, 1))
    tri_incl = (ids[:, None] <= ids[None, :]).astype(BF16)
    attn, cstats, wout_g, wgu_g, wd_g = _attn_fwd(
        proj, tri_after, n_seq, seq, [wout_s, wg_t, wu_t, wd_s],
        [jax.ShapeDtypeStruct((N_DEV, out_rows, d), BF16), jax.ShapeDtypeStruct((2, N_DEV, ff_rows, d), BF16),
         jax.ShapeDtypeStruct((N_DEV, ff_rows, d), BF16)],
        [(0, ()), (1, (0,)), (1, (1,)), (2, ())])
    wout_full = wout_g.reshape(N_DEV * out_rows, d)
    wgu_full = wgu_g.reshape(2, ff, d)
    wd_full = wd_g.reshape(ff, d)
    cat = _pool_fwd(proj, w_pool[0], pool_scale, attn, n_seq, seq)
    tok_f32, tok_bf16 = jax.ShapeDtypeStruct((t, d), F32), jax.ShapeDtypeStruct((t, d), BF16)
    seq_sds, vec_sds = jax.ShapeDtypeStruct((n_seq, 1, d), F32), jax.ShapeDtypeStruct((1, d), F32)
    mix, x1, h2 = _matmul_rows(
        cat, wout_full.reshape(2, d // 2, d), ROW_TILE, seq, "mix_mid", _mid_epilogue,
        [x2, g_mix_post, g_ffn_pre, mod], ["tok", "vec", "vec", "mod"],
        [tok_f32, tok_f32, tok_bf16], ["tok", "tok", "tok"])
    gu, act = _ffn_up(h2, wgu_full, 512, ff // 2)
    loss_sum, dy, df, dgate_f, gg_ffn_post = _matmul_rows(
        act, wd_full, ROW_TILE, seq, "ffn_down_post", _post_epilogue,
        [x1, tgt2, g_ffn_post, mod], ["tok", "tok", "vec", "mod"],
        [jax.ShapeDtypeStruct((1, LANES), F32), tok_f32, tok_bf16, seq_sds, vec_sds],
        ["loss", "tok", "tok", "seq", "vec"])

    dgu = _ffn_act_bwd(df, wd_full, gu, 512, ff // 2)
    gwd, gwd_b = _matmul(act, df, "tn", F32, ff // 2, d // 2, t, "grad_w_down", bf16_copy=True)
    gwgu, gwgu_b = _matmul(dgu, h2, "tn", F32, ff // 2, d // 2, t, "grad_w_gate_up", bf16_copy=True)
    dx1, dmix, dshift_f, dscale_f, dgate_m, gg_ffn_pre, gg_mix_post = _matmul_rows(
        dgu, wgu_full, ROW_TILE, seq, "dh2_bwd_mid", _bwd_mid_epilogue,
        [dy, x1, mix, g_ffn_pre, g_mix_post, mod], ["tok", "tok", "tok", "vec", "vec", "mod"],
        [tok_f32, tok_bf16, seq_sds, seq_sds, seq_sds, vec_sds, vec_sds],
        ["tok", "tok", "seq", "seq", "seq", "vec", "vec"])
    dcat = _matmul(dmix, wout_full, "nt", BF16, 512, d, d, "dcat")
    gwout, gwout_b = _matmul(cat, dmix, "tn", F32, d // 2, d, t, "grad_w_out", bf16_copy=True)
    dqkv, rv_wgu, rv_wd, rv_wout = _attn_bwd(
        proj, dcat, cstats, tri_after, tri_incl, n_seq, seq,
        [gwgu_b.reshape(2, N_DEV, ff_rows, d), gwd_b.reshape(1, N_DEV, ff_rows, d),
         gwout_b.reshape(1, N_DEV, out_rows, d)])
    dproj, gw_pool, gs_pool = _pool_bwd(proj, dcat, w_pool[0], pool_scale, dqkv, n_seq, seq)
    pad_d = lambda v: jnp.pad(v, ((0, 0), (0, d - v.shape[1])))
    n_gw = gw_pool.size // d
    early = jnp.concatenate(
        [gg_mix_post, gg_ffn_pre, gg_ffn_post, pad_d(gs_pool), pad_d(loss_sum), jnp.zeros((3, d), F32),
         gw_pool.reshape(n_gw, d),
         jnp.concatenate([dgate_m, dshift_f, dscale_f, dgate_f], axis=1).reshape(n_seq * 4, d)], axis=0)
    gwin, gwin_b, early_g = _matmul(
        dproj, h1, "tn", F32, d // 2, d, t, "grad_w_in", bf16_copy=True,
        ag=([early], [jax.ShapeDtypeStruct((N_DEV,) + early.shape, F32)], [(0, ())]))
    grad_x, dshift_m, dscale_m, gg_mix_pre, rv_win = _matmul_rows(
        dproj, win_full.reshape(4, d // 2, d), ROW_TILE, seq, "dh1_bwd_pre", _bwd_pre_epilogue,
        [dx1, x2, g_mix_pre, mod], ["tok", "tok", "vec", "mod"],
        [tok_f32, seq_sds, seq_sds, vec_sds], ["tok", "seq", "seq", "vec"],
        rs_sends=[gwin_b.reshape(1, N_DEV, in_rows, d)])


    late = jnp.concatenate([gg_mix_pre, dshift_m.reshape(n_seq, d), dscale_m.reshape(n_seq, d),
                            jnp.zeros((8 - 1 - 2 * n_seq, d), F32)], axis=0)
    late_g = _all_gather(late, "ag_late")
    loss = jnp.sum(early_g[:, 4, 0]) * (0.5 / d)
    dmod_all = jnp.concatenate(
        [late_g[:, 1:1 + n_seq, None, :], late_g[:, 1 + n_seq:1 + 2 * n_seq, None, :],
         early_g[:, 8 + n_gw:, :].reshape(N_DEV, n_seq, 4, d)], axis=2).reshape(N_DEV * n_seq, N_MOD * d)
    dmod_cols = lax.dynamic_slice_in_dim(dmod_all, me * cond_cols, cond_cols, axis=1)
    o_cond, o_bcond = _cond_bwd_adamw(c_all, dmod_all, dmod_cols, w_cond[0], m_w_cond[0], v_w_cond[0],
                                      b_cond, m_b_cond, v_b_cond)
    o_cond = tuple(o[None] for o in o_cond)

    small_ws = [g_mix_pre, g_mix_post, g_ffn_pre, g_ffn_post, pool_scale, w_pool.reshape(-1, POOL_GROUP_DIM)]
    small_ms = [m_g_mix_pre, m_g_mix_post, m_g_ffn_pre, m_g_ffn_post, m_pool_scale, m_w_pool.reshape(-1, POOL_GROUP_DIM)]
    small_vs = [v_g_mix_pre, v_g_mix_post, v_g_ffn_pre, v_g_ffn_post, v_pool_scale, v_w_pool.reshape(-1, POOL_GROUP_DIM)]
    small_gparts = [late_g[:, 0:1, :], early_g[:, 0:1, :], early_g[:, 1:2, :], early_g[:, 2:3, :],
                    early_g[:, 3:4, :pool_scale.shape[1]],
                    early_g[:, 8:8 + n_gw, :].reshape(N_DEV, -1, POOL_GROUP_DIM)]
    so = _adamw_small(small_ws, small_gparts, small_ms, small_vs, "adamw_small")
    ns = len(small_ws)
    sg, sdl, sm, sv = so[:ns], so[ns:2 * ns], so[2 * ns:3 * ns], so[3 * ns:]
    pool_shape = w_pool.shape
    fix = lambda lst: [lst[0], lst[1], lst[2], lst[3], lst[4], lst[5].reshape(pool_shape)]
    sg, sdl, sm, sv = fix(sg), fix(sdl), fix(sm), fix(sv)


    def reduced(mine, recv, slab, w, m, v, name, transposed=False, transpose=False):
        turn = (lambda u: u.T) if transposed else (lambda u: u)
        outs = _rs_final_adamw(mine, recv, slab, turn(w[0]), turn(m[0]), turn(v[0]), name, transpose)
        return tuple(turn(o)[None] for o in outs)

    o_in = reduced(gwin.reshape(1, N_DEV, in_rows, d), rv_win, 0, w_in, m_w_in, v_w_in, "adamw_w_in",
                   transpose=True)
    o_out = reduced(gwout.reshape(1, N_DEV, out_rows, d), rv_wout, 0, w_out, m_w_out, v_w_out, "adamw_w_out")
    gwgu8 = gwgu.reshape(2, N_DEV, ff_rows, d)
    o_gate = reduced(gwgu8, rv_wgu, 0, w_gate, m_w_gate, v_w_gate, "adamw_w_gate", transposed=True)
    o_up = reduced(gwgu8, rv_wgu, 1, w_up, m_w_up, v_w_up, "adamw_w_up", transposed=True)
    o_down = reduced(gwd.reshape(1, N_DEV, ff_rows, d), rv_wd, 0, w_down, m_w_down, v_w_down, "adamw_w_down")

    def pick(k):
        small_k = [sg, sdl, sm, sv][k]
        return [o_cond[k], o_bcond[k], small_k[0], small_k[1], o_in[k], small_k[5], small_k[4], o_out[k],
                small_k[2], small_k[3], o_gate[k], o_up[k], o_down[k]]

    return (loss, grad_x.reshape(n_seq, seq, d), *pick(0), *pick(1), *pick(2), *pick(3))
```

```python
import functools
import math

import jax
import jax.numpy as jnp
from jax import lax
from jax.experimental import pallas as pl
from jax.experimental.pallas import tpu as pltpu

F32 = jnp.float32
BF16 = jnp.bfloat16
MESH = pl.DeviceIdType.MESH

N_DEV = 8
HEAD_DIM = 64
LANES = 128
POOL_WINDOWS = (2, 4, 8, 16)
POOL_GROUP_DIM = 128
N_MOD = 6
EPS = 1e-6
ATT_TILE = 256
ATT_PAIRS = 2
VMEM_LIMIT = 56 * 1024 * 1024
ADAMW_COL_TILE = 256

ADAM_LR = 0.001
ADAM_B1 = 0.9
ADAM_B2 = 0.999
ADAM_EPS = 1e-08
ADAM_WD = 0.01
ADAM_STEP = 10


def _params(**kw):
    return pltpu.CompilerParams(vmem_limit_bytes=VMEM_LIMIT, **kw)


def _dot_nn(a, b):
    return jnp.dot(a, b, preferred_element_type=F32)


def _dot_nt(a, b):
    return lax.dot_general(a, b, (((1,), (1,)), ((), ())), preferred_element_type=F32)


def _dot_tn(a, b):
    return lax.dot_general(a, b, (((0,), (0,)), ((), ())), preferred_element_type=F32)


def _mesh_pos():
    return lax.axis_index("x"), lax.axis_index("y"), lax.axis_index("c")


def _ag_phases(dests, src, outs, send_sems, recv_sems, local_sems):
    n = len(src)
    x, y, c = _mesh_pos()
    me, sibling = (x, y, c), (x, y, 1 - c)
    chips = [(1 - x, y), (x, 1 - y), (1 - x, 1 - y)]

    def slot(i, dev):
        oi, prefix = dests[i]
        px, py, pc = dev
        return outs[oi].at[prefix + (4 * px + 2 * py + pc,)]

    def copy(i, k, block, to, from_src=False):
        return pltpu.make_async_remote_copy(
            src_ref=src[i] if from_src else slot(i, block), dst_ref=slot(i, block),
            send_sem=send_sems.at[i, k], recv_sem=recv_sems.at[i, k],
            device_id=to, device_id_type=MESH)

    def mine(i):
        return pltpu.make_async_copy(src[i], slot(i, me), local_sems.at[i])

    def first(i):
        return [copy(i, 0, me, sibling, from_src=True)] + [
            copy(i, 1 + j, me, (*chip, c), from_src=True) for j, chip in enumerate(chips)]

    def passed(i, j):
        return copy(i, 4 + j, (*chips[j], c), sibling)

    def start():
        for i in range(n):
            mine(i).start()
        for i in range(n):
            for cp in first(i):
                cp.start()

    def forward():
        for j, chip in enumerate(chips):
            for i in range(n):
                copy(i, 1 + j, (*chip, c), me).wait_recv()
                passed(i, j).start()

    def finish():
        for i in range(n):
            copy(i, 0, sibling, me).wait_recv()
            for j, chip in enumerate(chips):
                copy(i, 4 + j, (*chip, 1 - c), me).wait_recv()
        for i in range(n):
            for cp in first(i) + [passed(i, j) for j in range(3)]:
                cp.wait_send()
            mine(i).wait()

    return start, forward, finish


def _ag_scratch(n):
    return [pltpu.SemaphoreType.DMA((n, 7)), pltpu.SemaphoreType.DMA((n, 7)), pltpu.SemaphoreType.DMA((n,))]


def _all_gather(src, name):
    def body(src_ref, out_ref, send_sems, recv_sems, local_sem):
        x, y, c = _mesh_pos()

        def copy(k, block):
            px, py, pc = x ^ (k >> 2), y ^ ((k >> 1) & 1), c ^ (k & 1)
            bx, by, bc = (x, y, c) if block == "mine" else (px, py, pc)
            return pltpu.make_async_remote_copy(
                src_ref=src_ref, dst_ref=out_ref.at[4 * bx + 2 * by + bc],
                send_sem=send_sems.at[k - 1], recv_sem=recv_sems.at[k - 1],
                device_id=(px, py, pc), device_id_type=MESH)

        local = pltpu.make_async_copy(src_ref, out_ref.at[4 * x + 2 * y + c], local_sem.at[0])
        local.start()
        for k in range(1, N_DEV):
            copy(k, "mine").start()
        for k in range(1, N_DEV):
            copy(k, "mine").wait_send()
        for k in range(1, N_DEV):
            copy(k, "theirs").wait_recv()
        local.wait()

    any_spec = pl.BlockSpec(memory_space=pl.ANY)
    return pl.pallas_call(
        body, name=name,
        out_shape=jax.ShapeDtypeStruct((N_DEV,) + src.shape, src.dtype),
        in_specs=[any_spec], out_specs=any_spec,
        scratch_shapes=[pltpu.SemaphoreType.DMA((N_DEV - 1,)), pltpu.SemaphoreType.DMA((N_DEV - 1,)),
                        pltpu.SemaphoreType.DMA((1,))],
    )(src)


def _rs_phases(shapes, src, dst, send_sems, recv_sems):
    x, y, c = _mesh_pos()

    def copies():
        out = []
        n = 0
        for i, shp in enumerate(shapes):
            for m in range(shp[0]):
                for k in range(1, N_DEV):
                    px, py, pc = x ^ (k >> 2), y ^ ((k >> 1) & 1), c ^ (k & 1)
                    out.append(pltpu.make_async_remote_copy(
                        src_ref=src[i].at[m, 4 * px + 2 * py + pc], dst_ref=dst[i].at[m, k - 1],
                        send_sem=send_sems.at[n], recv_sem=recv_sems.at[n],
                        device_id=(px, py, pc), device_id_type=MESH))
                    n += 1
        return out

    def start():
        for cp in copies():
            cp.start()

    def finish():
        for cp in copies():
            cp.wait_send()
        for cp in copies():
            cp.wait_recv()

    return start, finish


def _rs_out(sends):
    return [jax.ShapeDtypeStruct((s.shape[0], N_DEV - 1) + s.shape[2:], s.dtype) for s in sends]


def _rs_scratch(sends):
    total = sum((N_DEV - 1) * s.shape[0] for s in sends)
    return [pltpu.SemaphoreType.DMA((total,)), pltpu.SemaphoreType.DMA((total,))]


def _rs_final_adamw(mine, recv, slab, w, m, v, name, transpose=False):
    _, _, r, cdim = mine.shape
    tc = ADAMW_COL_TILE
    assert cdim % tc == 0 and w.shape == ((cdim, r) if transpose else (r, cdim)), (name, w.shape)
    x, y, c = _mesh_pos()
    me = jnp.reshape(4 * x + 2 * y + c, (1,)).astype(jnp.int32)

    def body(me_ref, p_ref, r_ref, w_ref, m_ref, v_ref, g_ref, d_ref, nm_ref, nv_ref):
        del me_ref
        g = p_ref[...].astype(F32)
        for k in range(N_DEV - 1):
            g = g + r_ref[k].astype(F32)
        if transpose:
            g = g.T
        g_ref[...] = g
        d_ref[...], nm_ref[...], nv_ref[...] = _adamw_math(w_ref[...], g, m_ref[...], v_ref[...])

    if transpose:
        w_spec = pl.BlockSpec((tc, r), lambda j, s: (j, 0))
    else:
        w_spec = pl.BlockSpec((r, tc), lambda j, s: (0, j))
    sds = jax.ShapeDtypeStruct(w.shape, F32)
    return pl.pallas_call(
        body, name=name, out_shape=(sds, sds, sds, sds),
        grid_spec=pltpu.PrefetchScalarGridSpec(
            num_scalar_prefetch=1, grid=(cdim // tc,),
            in_specs=[pl.BlockSpec((None, None, r, tc), lambda j, s: (slab, s[0], 0, j)),
                      pl.BlockSpec((None, N_DEV - 1, r, tc), lambda j, s: (slab, 0, 0, j)),
                      w_spec, w_spec, w_spec],
            out_specs=(w_spec, w_spec, w_spec, w_spec)),
        compiler_params=_params(),
    )(me, mine, recv, w, m, v)


def _matmul(a, b, mode, out_dtype, tm, tn, tk, name, bf16_copy=False, rs_sends=(), ag=None):
    ga = a.shape[0] if a.ndim == 3 else None
    gb = b.shape[0] if b.ndim == 3 else None
    a2, b2 = a.shape[-2:], b.shape[-2:]
    if mode == "nn":
        (m, k), n = a2, b2[1]
    elif mode == "nt":
        (m, k), n = a2, b2[0]
    else:
        (k, m), n = a2, b2[1]
    assert m % tm == 0 and n % tn == 0 and k % tk == 0, (name, m, n, k)
    nk = k // tk
    g_n = ga or 1
    batch_out = mode == "tn" and ga is not None
    n_red = nk if batch_out else nk * g_n
    dot = {"nn": _dot_nn, "nt": _dot_nt, "tn": _dot_tn}[mode]
    acc_in_out = out_dtype == F32

    n_rs = len(rs_sends)
    rs_shapes = [r.shape for r in rs_sends]
    ag_srcs, ag_out_shapes, ag_dests = ag if ag is not None else ((), (), ())
    n_ag, n_ag_out = len(ag_srcs), len(ag_out_shapes)
    n_out = 2 if bf16_copy else 1
    assert not bf16_copy or acc_in_out
    assert not (n_rs and n_ag)

    def body(a_ref, b_ref, *rest):
        rs_src, rest = rest[:n_rs], rest[n_rs:]
        ag_src, rest = rest[:n_ag], rest[n_ag:]
        o_ref = rest[0]
        copy_ref = rest[1] if bf16_copy else None
        rs_dst, rest = rest[n_out:n_out + n_rs], rest[n_out + n_rs:]
        ag_out, scratch = rest[:n_ag_out], rest[n_ag_out:]
        first = functools.reduce(jnp.logical_and, [pl.program_id(ax) == 0 for ax in range(4)])
        last = functools.reduce(jnp.logical_and, [pl.program_id(ax) == grid[ax] - 1 for ax in range(4)])
        if n_rs:
            rs_start, rs_finish = _rs_phases(rs_shapes, rs_src, rs_dst, *scratch[-2:])
            pl.when(first)(rs_start)
        if n_ag:
            ag_start, ag_forward, ag_finish = _ag_phases(ag_dests, ag_src, ag_out, *scratch[-3:])
            pl.when(first)(ag_start)
        p = dot(a_ref[...], b_ref[...])
        kk = pl.program_id(3) if batch_out else pl.program_id(2) * nk + pl.program_id(3)
        if n_red == 1:
            o_ref[...] = p.astype(out_dtype)
            if bf16_copy:
                copy_ref[...] = p.astype(BF16)
        else:
            acc = o_ref if acc_in_out else scratch[0]

            @pl.when(kk == 0)
            def _():
                acc[...] = p

            @pl.when(kk > 0)
            def _():
                acc[...] += p

            @pl.when(kk == n_red - 1)
            def _():
                if not acc_in_out:
                    o_ref[...] = acc[...].astype(out_dtype)
                if bf16_copy:
                    copy_ref[...] = acc[...].astype(BF16)

        if n_rs:
            pl.when(last)(rs_finish)
        if n_ag:
            @pl.when(last)
            def _():
                ag_forward()
                ag_finish()

    def order(ids):
        return ids if batch_out else (ids[2], ids[0], ids[1], ids[3])

    def a_idx(*ids):
        g, i, j, kq = order(ids)
        blk = {"nn": (i, kq), "nt": (i, kq), "tn": (kq, i)}[mode]
        return (g,) + blk if ga is not None else blk

    def b_idx(*ids):
        g, i, j, kq = order(ids)
        blk = {"nn": (kq, j), "nt": (j, kq), "tn": (kq, j)}[mode]
        return (g,) + blk if gb is not None else blk

    def o_idx(*ids):
        g, i, j, kq = order(ids)
        return (g, i, j) if batch_out else (i, j)

    a_blk = {"nn": (tm, tk), "nt": (tm, tk), "tn": (tk, tm)}[mode]
    b_blk = {"nn": (tk, tn), "nt": (tn, tk), "tn": (tk, tn)}[mode]
    if ga is not None:
        a_blk = (None,) + a_blk
    if gb is not None:
        b_blk = (None,) + b_blk
    if batch_out:
        out_shape = jax.ShapeDtypeStruct((g_n, m, n), out_dtype)
        o_blk = (None, tm, tn)
        grid = (g_n, m // tm, n // tn, nk)
    else:
        out_shape = jax.ShapeDtypeStruct((m, n), out_dtype)
        o_blk = (tm, tn)
        grid = (m // tm, n // tn, g_n, nk)
    scratch = [] if (acc_in_out or n_red == 1) else [pltpu.VMEM((tm, tn), F32)]
    any_spec = pl.BlockSpec(memory_space=pl.ANY)
    out_shapes = [out_shape] + ([jax.ShapeDtypeStruct(out_shape.shape, BF16)] if bf16_copy else [])
    res = pl.pallas_call(
        body, name=name, out_shape=tuple(out_shapes + _rs_out(rs_sends) + list(ag_out_shapes)), grid=grid,
        in_specs=[pl.BlockSpec(a_blk, a_idx), pl.BlockSpec(b_blk, b_idx)] + [any_spec] * (n_rs + n_ag),
        out_specs=tuple([pl.BlockSpec(o_blk, o_idx)] * n_out + [any_spec] * (n_rs + n_ag_out)),
        scratch_shapes=scratch + (_rs_scratch(rs_sends) if n_rs else []) + (_ag_scratch(n_ag) if n_ag else []),
        compiler_params=_params(),
    )(a, b, *rs_sends, *ag_srcs)
    return res if len(res) > 1 else res[0]


EW_TILE = 256
ROW_TILE = 512
EPILOGUE_CHUNKS = 8
MXU_WIDTH = 256


def _rms(v):
    return lax.rsqrt(jnp.mean(v * v, axis=-1, keepdims=True) + EPS)


def _rms_bwd(dhat, vh, r):
    return r * (dhat - vh * jnp.mean(dhat * vh, axis=-1, keepdims=True))


def _tok_spec(tm, d):
    return pl.BlockSpec((tm, d), lambda i: (i, 0))


def _vec_spec(d):
    return pl.BlockSpec((1, d), lambda i: (0, 0))


def _mod_spec(tiles_per_seq, d):
    return pl.BlockSpec((None, N_MOD, d), lambda i: (i // tiles_per_seq, 0, 0))


def _seq_acc_spec(tiles_per_seq, d):
    return pl.BlockSpec((None, 1, d), lambda i: (i // tiles_per_seq, 0, 0))


def _acc(ref, val, first):
    if first is False:
        ref[...] += val
        return

    @pl.when(first)
    def _():
        ref[...] = val

    @pl.when(jnp.logical_not(first))
    def _():
        ref[...] += val


def _colsum(v):
    return jnp.sum(v, axis=0, keepdims=True)


def _pre_mix(x2, g_pre, mod, seq, ag_srcs, ag_out_shapes, ag_dests):
    t, d = x2.shape
    tm = EW_TILE
    n_steps = t // tm
    n_ag, n_ag_out = len(ag_srcs), len(ag_out_shapes)

    def body(x_ref, g_ref, mod_ref, *rest):
        ag_src, h_ref = rest[:n_ag], rest[n_ag]
        ag_out, sems = rest[n_ag + 1:n_ag + 1 + n_ag_out], rest[n_ag + 1 + n_ag_out:]
        ag_start, ag_forward, ag_finish = _ag_phases(ag_dests, ag_src, ag_out, *sems)
        step = pl.program_id(0)
        pl.when(step == 0)(ag_start)
        xv = x_ref[...]
        n = xv * _rms(xv) * g_ref[...]
        h_ref[...] = (n * (1.0 + mod_ref[1:2, :]) + mod_ref[0:1, :]).astype(BF16)

        @pl.when(step == n_steps - 1)
        def _():
            ag_forward()
            ag_finish()

    any_spec = pl.BlockSpec(memory_space=pl.ANY)
    return pl.pallas_call(
        body, name="pre_mix", out_shape=(jax.ShapeDtypeStruct((t, d), BF16), *ag_out_shapes), grid=(n_steps,),
        in_specs=[_tok_spec(tm, d), _vec_spec(d), _mod_spec(seq // tm, d)] + [any_spec] * n_ag,
        out_specs=(_tok_spec(tm, d), *([any_spec] * n_ag_out)),
        scratch_shapes=_ag_scratch(n_ag), compiler_params=_params(),
    )(x2, g_pre, mod, *ag_srcs)


def _matmul_rows(a, b, tm, seq, name, epilogue, ep_in, ep_in_kinds, ep_out, ep_out_kinds, rs_sends=()):
    g_n = a.shape[0] if a.ndim == 3 else None
    (m, k), n = a.shape[-2:], b.shape[-1]
    tps = seq // tm
    n_i = m // tm
    n_rs = len(rs_sends)
    rs_shapes = [r.shape for r in rs_sends]
    n_in, n_out = len(ep_in), len(ep_out)
    n_cols = n // MXU_WIDTH
    rc, cw = tm // EPILOGUE_CHUNKS, n // n_cols

    def prev(i):
        return jnp.maximum(i - 1, 0)

    def spec(kind):
        return {"tok": pl.BlockSpec((tm, n), lambda i: (prev(i), 0)),
                "vec": pl.BlockSpec((1, n), lambda i: (0, 0)),
                "mod": pl.BlockSpec((None, N_MOD, n), lambda i: (prev(i) // tps, 0, 0)),
                "seq": pl.BlockSpec((None, 1, n), lambda i: (prev(i) // tps, 0, 0)),
                "loss": pl.BlockSpec((1, LANES), lambda i: (0, 0))}[kind]

    def body(a_ref, b_ref, *rest):
        in_refs, rest = rest[:n_in], rest[n_in:]
        rs_src, rest = rest[:n_rs], rest[n_rs:]
        out_refs, rest = rest[:n_out], rest[n_out:]
        rs_dst, rest = rest[:n_rs], rest[n_rs:]
        fin = rest[0]
        i = pl.program_id(0)
        if n_rs:
            rs_start, rs_finish = _rs_phases(rs_shapes, rs_src, rs_dst, *rest[1:])
            pl.when(i == 0)(rs_start)

        def product(cols):
            if g_n is None:
                return _dot_nn(a_ref[...], b_ref[:, cols])
            p = _dot_nn(a_ref[0], b_ref[0, :, cols])
            for g in range(1, g_n):
                p = p + _dot_nn(a_ref[g], b_ref[g, :, cols])
            return p

        def step(with_epilogue, with_matmul):
            parts = []
            for c in range(EPILOGUE_CHUNKS):
                if with_epilogue:
                    rows = pl.ds(c * rc, rc)
                    epilogue(fin[rows, :], i - 1, tps, in_refs, out_refs, rows, c)
                while with_matmul and len(parts) < (c + 1) * n_cols // EPILOGUE_CHUNKS:
                    cols = slice(len(parts) * cw, (len(parts) + 1) * cw)
                    parts.append((cols, product(cols)))
            for cols, v in parts:
                fin[:, cols] = v

        pl.when(i == 0)(functools.partial(step, False, True))
        pl.when(jnp.logical_and(i > 0, i < n_i))(functools.partial(step, True, True))
        pl.when(i == n_i)(functools.partial(step, True, False))

        if n_rs:
            pl.when(i == n_i)(rs_finish)

    def row(i):
        return jnp.minimum(i, n_i - 1)

    if g_n is None:
        a_spec = pl.BlockSpec((tm, k), lambda i: (row(i), 0))
        b_spec = pl.BlockSpec(b.shape, lambda i: (0, 0), pipeline_mode=pl.Buffered(1))
    else:
        a_spec = pl.BlockSpec((g_n, tm, k), lambda i: (0, row(i), 0))
        b_spec = pl.BlockSpec(b.shape, lambda i: (0, 0, 0), pipeline_mode=pl.Buffered(1))
    any_spec = pl.BlockSpec(memory_space=pl.ANY)
    res = pl.pallas_call(
        body, name=name, grid=(n_i + 1,), out_shape=tuple(list(ep_out) + _rs_out(rs_sends)),
        in_specs=[a_spec, b_spec] + [spec(kd) for kd in ep_in_kinds] + [any_spec] * n_rs,
        out_specs=tuple([spec(kd) for kd in ep_out_kinds] + [any_spec] * n_rs),
        scratch_shapes=[pltpu.VMEM((tm, n), F32)] + (_rs_scratch(rs_sends) if n_rs else []),
        compiler_params=_params(),
    )(a, b, *ep_in, *rs_sends)
    return res


def _first(cond, chunk):
    return cond if chunk == 0 else False


def _mid_epilogue(mv, i, tps, in_refs, out_refs, rows, chunk):
    x_ref, gpost_ref, gpre_ref, mod_ref = in_refs
    mix_ref, x1_ref, h2_ref = out_refs
    mix_ref[rows, :] = mv
    x1 = x_ref[rows, :] + mod_ref[2:3, :] * (mv * _rms(mv) * gpost_ref[...])
    x1_ref[rows, :] = x1
    n = x1 * _rms(x1) * gpre_ref[...]
    h2_ref[rows, :] = (n * (1.0 + mod_ref[4:5, :]) + mod_ref[3:4, :]).astype(BF16)


def _post_epilogue(fv, i, tps, in_refs, out_refs, rows, chunk):
    x1_ref, tgt_ref, g_ref, mod_ref = in_refs
    loss_ref, dy_ref, df_ref, dgate_ref, gg_ref = out_refs
    d = fv.shape[1]
    r = _rms(fv)
    fh = fv * r
    nf = fh * g_ref[...]
    gate = mod_ref[5:6, :]
    err = x1_ref[rows, :] + gate * nf - tgt_ref[rows, :]
    _acc(loss_ref, jnp.sum(_colsum(err * err), axis=1, keepdims=True) * jnp.ones((1, LANES), F32),
         _first(i == 0, chunk))
    dy = err * (1.0 / d)
    dy_ref[rows, :] = dy
    _acc(dgate_ref, _colsum(dy * nf), _first(i % tps == 0, chunk))
    dn = dy * gate
    _acc(gg_ref, _colsum(dn * fh), _first(i == 0, chunk))
    df_ref[rows, :] = _rms_bwd(dn * g_ref[...], fh, r).astype(BF16)


def _bwd_mid_epilogue(dh, i, tps, in_refs, out_refs, rows, chunk):
    dy_ref, x1_ref, mix_ref, gpre_ref, gpost_ref, mod_ref = in_refs
    dx1_ref, dmix_ref, dshift_ref, dscale_ref, dgate_ref, ggpre_ref, ggpost_ref = out_refs
    seq_first, first = _first(i % tps == 0, chunk), _first(i == 0, chunk)
    x1 = x1_ref[rows, :]
    r = _rms(x1)
    xh = x1 * r
    gpre = gpre_ref[...]
    _acc(dshift_ref, _colsum(dh), seq_first)
    _acc(dscale_ref, _colsum(dh * xh * gpre), seq_first)
    dn = dh * (1.0 + mod_ref[4:5, :])
    _acc(ggpre_ref, _colsum(dn * xh), first)
    dx1 = dy_ref[rows, :] + _rms_bwd(dn * gpre, xh, r)
    dx1_ref[rows, :] = dx1
    mv = mix_ref[rows, :]
    rm = _rms(mv)
    mh = mv * rm
    gpost = gpost_ref[...]
    _acc(dgate_ref, _colsum(dx1 * mh * gpost), seq_first)
    dnm = dx1 * mod_ref[2:3, :]
    _acc(ggpost_ref, _colsum(dnm * mh), first)
    dmix_ref[rows, :] = _rms_bwd(dnm * gpost, mh, rm).astype(BF16)


def _bwd_pre_epilogue(dh, i, tps, in_refs, out_refs, rows, chunk):
    dx1_ref, x_ref, g_ref, mod_ref = in_refs
    gx_ref, dshift_ref, dscale_ref, gg_ref = out_refs
    seq_first = _first(i % tps == 0, chunk)
    xv = x_ref[rows, :]
    r = _rms(xv)
    xh = xv * r
    g = g_ref[...]
    _acc(dshift_ref, _colsum(dh), seq_first)
    _acc(dscale_ref, _colsum(dh * xh * g), seq_first)
    dn = dh * (1.0 + mod_ref[1:2, :])
    _acc(gg_ref, _colsum(dn * xh), _first(i == 0, chunk))
    gx_ref[rows, :] = dx1_ref[rows, :] + _rms_bwd(dn * g, xh, r)


def _ffn_up(h2, wgu, tm, tn):
    t, d = h2.shape
    f = wgu.shape[1]

    def body(h_ref, w_ref, gu_ref, act_ref):
        h = h_ref[...]
        g = _dot_nt(h, w_ref[0])
        u = _dot_nt(h, w_ref[1])
        gu_ref[0] = g.astype(BF16)
        gu_ref[1] = u.astype(BF16)
        act_ref[...] = (g * jax.nn.sigmoid(g) * u).astype(BF16)

    return pl.pallas_call(
        body, name="ffn_up", grid=(f // tn, t // tm),
        out_shape=(jax.ShapeDtypeStruct((2, t, f), BF16), jax.ShapeDtypeStruct((t, f), BF16)),
        in_specs=[pl.BlockSpec((tm, d), lambda j, i: (i, 0)), pl.BlockSpec((2, tn, d), lambda j, i: (0, j, 0))],
        out_specs=(pl.BlockSpec((2, tm, tn), lambda j, i: (0, i, j)), pl.BlockSpec((tm, tn), lambda j, i: (i, j))),
        compiler_params=_params(),
    )(h2, wgu)


def _ffn_act_bwd(df, wd, gu, tm, tn):
    t, d = df.shape
    f = wd.shape[0]

    def body(df_ref, w_ref, gu_ref, dgu_ref):
        da = _dot_nt(df_ref[...], w_ref[...])
        g = gu_ref[0].astype(F32)
        u = gu_ref[1].astype(F32)
        s = jax.nn.sigmoid(g)
        silu = g * s
        dgu_ref[0] = (da * u * (s + silu * (1.0 - s))).astype(BF16)
        dgu_ref[1] = (da * silu).astype(BF16)

    return pl.pallas_call(
        body, name="ffn_act_bwd", grid=(f // tn, t // tm),
        out_shape=jax.ShapeDtypeStruct((2, t, f), BF16),
        in_specs=[pl.BlockSpec((tm, d), lambda j, i: (i, 0)), pl.BlockSpec((tn, d), lambda j, i: (j, 0)),
                  pl.BlockSpec((2, tm, tn), lambda j, i: (0, i, j))],
        out_specs=pl.BlockSpec((2, tm, tn), lambda j, i: (0, i, j)),
        compiler_params=_params(),
    )(df, wd, gu)


SIGN_BIT = 0x80000000
Q_SCALE = 1.0 / math.sqrt(HEAD_DIM)


def _softplus(z):
    neg_abs = lax.bitcast_convert_type(lax.bitcast_convert_type(z, jnp.uint32) | jnp.uint32(SIGN_BIT), F32)
    return jnp.maximum(z, 0.0) + jnp.log(1.0 + jnp.exp(neg_abs))


def _hi_lo(v):
    hi = v.astype(BF16)
    return jnp.concatenate([hi, (v - hi.astype(F32)).astype(BF16)], axis=1)


def _emit_skewed(chains, lag=1):
    for t in range(max(len(ch) for ch in chains) + lag * (len(chains) - 1)):
        for c, ch in enumerate(chains):
            if 0 <= t - lag * c < len(ch):
                ch[t - lag * c]()


def _fwd_chain(blk, qs, k_ref, v_ref, c0, kb, cols, mask, ntri, lane, tq):
    st = {}

    def scores():
        st["z"] = _dot_nt(qs, k_ref[pl.ds(c0, tq), cols])

    def soft():
        sp = _softplus(st["z"])
        if mask is not None:
            sp = jnp.where(mask, sp, 0.0)
        st["parts"] = _hi_lo(sp)
        st["cur"] = blk["cur"]
        blk["cm"] = jnp.where(lane == kb, blk["cur"], blk["cm"])
        blk["cur"] = blk["cur"] - jnp.sum(sp, axis=1, keepdims=True)

    def sums():
        st["s"] = _dot_nn(st["parts"], ntri)

    def weights():
        w = jnp.exp(st["z"] + st["s"] + st["cur"])
        if mask is not None:
            w = jnp.where(mask, w, 0.0)
        st["w"] = w.astype(BF16)

    def out():
        p = _dot_nn(st["w"], v_ref[pl.ds(c0, tq), cols])
        blk["pv"] = p if blk["pv"] is None else blk["pv"] + p

    return [scores, soft, sums, weights, out]


def _bwd_chain(blk, qs, dos, cs, k_ref, v_ref, dk_ref, dv_ref, c0, kb, cols, mask, ntri, tri_i, lane, tq):
    st = {}

    def scores():
        st["z"] = _dot_nt(qs, k_ref[pl.ds(c0, tq), cols])
        st["dw"] = _dot_nt(dos, v_ref[pl.ds(c0, tq), cols])

    def soft():
        sp = _softplus(st["z"])
        if mask is not None:
            sp = jnp.where(mask, sp, 0.0)
        st["sp"] = sp
        st["parts"] = _hi_lo(sp)
        st["cur"] = jnp.sum(jnp.where(lane == kb, cs, 0.0), axis=1, keepdims=True)

    def sums():
        st["s"] = _dot_nn(st["parts"], ntri)

    def weights():
        w = jnp.exp(st["z"] + st["s"] + st["cur"])
        if mask is not None:
            w = jnp.where(mask, w, 0.0)
        ee = w * st["dw"]
        st["w"], st["ee"], st["ec"] = w.astype(BF16), ee, blk["ec"]
        blk["ec"] = blk["ec"] + jnp.sum(ee, axis=1, keepdims=True)

    def prefix():
        st["einc"] = _dot_nn(st["ee"].astype(BF16), tri_i)

    def dz():
        v = st["ee"] - jnp.exp(st["z"] - st["sp"]) * (st["einc"] + st["ec"])
        if mask is not None:
            v = jnp.where(mask, v, 0.0)
        st["dz"] = v.astype(BF16)

    def grads():
        p = _dot_nn(st["dz"], k_ref[pl.ds(c0, tq), cols])
        blk["dq"] = p if blk["dq"] is None else blk["dq"] + p
        dk_ref[pl.ds(c0, tq), :] += _dot_tn(st["dz"], qs)
        dv_ref[pl.ds(c0, tq), :] += _dot_tn(st["w"], dos)

    return [scores, soft, sums, weights, prefix, dz, grads]


def _stack_heads(v, lane, scale=None):
    if scale is not None:
        v = v * jnp.asarray(scale, v.dtype)
    zero = jnp.zeros_like(v)
    return jnp.concatenate([jnp.where(lane < HEAD_DIM, v, zero), jnp.where(lane >= HEAD_DIM, v, zero)], axis=0)


def _diag_mask(tq):
    row = lax.broadcasted_iota(jnp.int32, (2 * tq, tq), 0)
    col = lax.broadcasted_iota(jnp.int32, (2 * tq, tq), 1)
    return col < jnp.where(row >= tq, row - tq, row)


def _attn_fwd(proj, tri_after, n_seq, seq, ag_srcs, ag_out_shapes, ag_dests):
    t = proj.shape[0]
    tq = ATT_TILE
    npp = ATT_PAIRS
    n_blk = (proj.shape[1] // 4) // (npp * LANES)
    n_ag, n_ag_out = len(ag_srcs), len(ag_out_shapes)
    n_steps = n_seq * n_blk

    def body(q_ref, k_ref, v_ref, tri_ref, *rest):
        ag_src, rest = rest[:n_ag], rest[n_ag:]
        o_ref, cs_ref = rest[:2]
        ag_out, rest = rest[2:2 + n_ag_out], rest[2 + n_ag_out:]
        oacc, cmat, carry = rest[:3]
        ag_start, ag_forward, ag_finish = _ag_phases(ag_dests, ag_src, ag_out, *rest[3:])
        step = pl.program_id(0) * n_blk + pl.program_id(1)
        pl.when(step == 0)(ag_start)
        pl.when(step == (3 * n_steps) // 4)(ag_forward)
        lane = lax.broadcasted_iota(jnp.int32, (1, LANES), 1)
        ntri = tri_ref[...]
        diag = _diag_mask(tq)

        def q_tile(qi, _):
            r0 = pl.multiple_of(qi * tq, tq)
            qs = [_stack_heads(q_ref[pl.ds(r0, tq), pp * LANES:(pp + 1) * LANES], lane, Q_SCALE)
                  for pp in range(npp)]
            carry[...] = jnp.zeros_like(carry)
            cmat[...] = jnp.zeros_like(cmat)
            oacc[...] = jnp.zeros_like(oacc)

            def run_tiles(tiles):
                blocks = [dict(cur=carry[pp], cm=cmat[pp], pv=None) for pp in range(npp)]
                chains = []
                for kb, mask in tiles:
                    c0 = pl.multiple_of(kb * tq, tq)
                    for pp in range(npp):
                        chains.append(_fwd_chain(blocks[pp], qs[pp], k_ref, v_ref, c0, kb,
                                                 slice(pp * LANES, (pp + 1) * LANES), mask, ntri, lane, tq))
                _emit_skewed(chains)
                for pp in range(npp):
                    oacc[pp] += blocks[pp]["pv"]
                    cmat[pp] = blocks[pp]["cm"]
                    carry[pp] = blocks[pp]["cur"]

            odd = qi % 2

            @pl.when(odd == 0)
            def _():
                run_tiles([(qi, diag)])

            @pl.when(odd == 1)
            def _():
                run_tiles([(qi, diag), (qi - 1, None)])

            def pair(j, _):
                kb = qi - 1 - odd - 2 * j
                run_tiles([(kb, None), (kb - 1, None)])
                return 0

            lax.fori_loop(0, qi // 2, pair, 0)
            for pp in range(npp):
                c_off = 2 * pp * LANES
                cs_ref[pl.ds(r0, tq), c_off:c_off + LANES] = cmat[pp, 0:tq, :]
                cs_ref[pl.ds(r0, tq), c_off + LANES:c_off + 2 * LANES] = cmat[pp, tq:2 * tq, :]
                o_ref[pl.ds(r0, tq), pp * LANES:(pp + 1) * LANES] = jnp.where(
                    lane < HEAD_DIM, oacc[pp, 0:tq, :], oacc[pp, tq:2 * tq, :]).astype(BF16)
            return 0

        lax.fori_loop(0, seq // tq, q_tile, 0)
        pl.when(step == n_steps - 1)(ag_finish)

    wid = npp * LANES
    blk = lambda off: pl.BlockSpec((seq, wid), lambda b, p: (b, off + p))
    any_spec = pl.BlockSpec(memory_space=pl.ANY)
    return pl.pallas_call(
        body, name="attn_fwd", grid=(n_seq, n_blk),
        out_shape=(jax.ShapeDtypeStruct((2, t, n_blk * wid), BF16),
                   jax.ShapeDtypeStruct((t, n_blk * 2 * wid), F32), *ag_out_shapes),
        in_specs=[blk(0), blk(n_blk), blk(2 * n_blk), pl.BlockSpec((2 * tq, tq), lambda b, p: (0, 0))]
        + [any_spec] * n_ag,
        out_specs=(pl.BlockSpec((None, seq, wid), lambda b, p: (0, b, p)),
                   pl.BlockSpec((seq, 2 * wid), lambda b, p: (b, p)), *([any_spec] * n_ag_out)),
        scratch_shapes=[pltpu.VMEM((npp, 2 * tq, LANES), F32), pltpu.VMEM((npp, 2 * tq, LANES), F32),
                        pltpu.VMEM((npp, 2 * tq, 1), F32)] + _ag_scratch(n_ag),
        compiler_params=_params(),
    )(proj, proj, proj, tri_after, *ag_srcs)


def _attn_bwd(proj, dcat, cstats, tri_after, tri_incl, n_seq, seq, rs_sends):
    t = proj.shape[0]
    tq = ATT_TILE
    npp = ATT_PAIRS
    width = proj.shape[1] // 4
    n_blk = width // (npp * LANES)
    n_rs = len(rs_sends)
    rs_shapes = [r.shape for r in rs_sends]
    n_steps = n_seq * n_blk

    def body(q_ref, k_ref, v_ref, do_ref, cs_ref, tria_ref, trii_ref, *rest):
        rs_src, rest = rest[:n_rs], rest[n_rs:]
        out_ref = rest[0]
        rs_dst, rest = rest[1:1 + n_rs], rest[1 + n_rs:]
        dq_acc, dk_acc, dv_acc, ecarry = rest[:4]
        rs_start, rs_finish = _rs_phases(rs_shapes, rs_src, rs_dst, *rest[4:])
        step = pl.program_id(0) * n_blk + pl.program_id(1)
        pl.when(step == 0)(rs_start)
        lane = lax.broadcasted_iota(jnp.int32, (1, LANES), 1)
        ntri = tria_ref[...]
        tri_i = trii_ref[...]
        diag = _diag_mask(tq)
        dk_acc[...] = jnp.zeros_like(dk_acc)
        dv_acc[...] = jnp.zeros_like(dv_acc)

        def q_tile(qi, _):
            r0 = pl.multiple_of(qi * tq, tq)
            qs, dos, cs = [], [], []
            for pp in range(npp):
                cols = slice(pp * LANES, (pp + 1) * LANES)
                qs.append(_stack_heads(q_ref[pl.ds(r0, tq), cols], lane, Q_SCALE))
                dos.append(_stack_heads(do_ref[pl.ds(r0, tq), cols], lane))
                c_off = 2 * pp * LANES
                cs.append(jnp.concatenate([cs_ref[pl.ds(r0, tq), c_off:c_off + LANES],
                                           cs_ref[pl.ds(r0, tq), c_off + LANES:c_off + 2 * LANES]], axis=0))
            ecarry[...] = jnp.zeros_like(ecarry)
            dq_acc[...] = jnp.zeros_like(dq_acc)

            def run_tiles(tiles):
                blocks = [dict(ec=ecarry[pp], dq=None) for pp in range(npp)]
                chains = []
                for kb, mask in tiles:
                    c0 = pl.multiple_of(kb * tq, tq)
                    for pp in range(npp):
                        chains.append(_bwd_chain(
                            blocks[pp], qs[pp], dos[pp], cs[pp], k_ref, v_ref, dk_acc.at[pp], dv_acc.at[pp],
                            c0, kb, slice(pp * LANES, (pp + 1) * LANES), mask, ntri, tri_i, lane, tq))
                _emit_skewed(chains)
                for pp in range(npp):
                    dq_acc[pp] += blocks[pp]["dq"]
                    ecarry[pp] = blocks[pp]["ec"]

            def pair(j, _):
                run_tiles([(2 * j, None), (2 * j + 1, None)])
                return 0

            lax.fori_loop(0, qi // 2, pair, 0)
            odd = qi % 2

            @pl.when(odd == 0)
            def _():
                run_tiles([(qi, diag)])

            @pl.when(odd == 1)
            def _():
                run_tiles([(qi - 1, None), (qi, diag)])

            for pp in range(npp):
                dq = jnp.where(lane < HEAD_DIM, dq_acc[pp, 0:tq, :], dq_acc[pp, tq:2 * tq, :])
                out_ref[0, pl.ds(r0, tq), pp * LANES:(pp + 1) * LANES] = (dq * Q_SCALE).astype(BF16)
            return 0

        lax.fori_loop(0, seq // tq, q_tile, 0)
        for pp in range(npp):
            cols = slice(pp * LANES, (pp + 1) * LANES)
            out_ref[1, :, cols] = dk_acc[pp].astype(BF16)
            out_ref[2, :, cols] = dv_acc[pp].astype(BF16)
        pl.when(step == n_steps - 1)(rs_finish)

    wid = npp * LANES
    blk = lambda off: pl.BlockSpec((seq, wid), lambda b, p: (b, off + p))
    tri_spec = pl.BlockSpec((2 * tq, tq), lambda b, p: (0, 0))
    any_spec = pl.BlockSpec(memory_space=pl.ANY)
    return pl.pallas_call(
        body, name="attn_bwd", grid=(n_seq, n_blk),
        out_shape=(jax.ShapeDtypeStruct((4, t, width), BF16), *_rs_out(rs_sends)),
        in_specs=[blk(0), blk(n_blk), blk(2 * n_blk), pl.BlockSpec((seq, wid), lambda b, p: (b, p)),
                  pl.BlockSpec((seq, 2 * wid), lambda b, p: (b, p)), tri_spec,
                  pl.BlockSpec((tq, tq), lambda b, p: (0, 0))] + [any_spec] * n_rs,
        out_specs=(pl.BlockSpec((3, seq, wid), lambda b, p: (0, b, p)), *([any_spec] * n_rs)),
        scratch_shapes=[pltpu.VMEM((npp, 2 * tq, LANES), F32), pltpu.VMEM((npp, seq, LANES), F32),
                        pltpu.VMEM((npp, seq, LANES), F32), pltpu.VMEM((npp, 2 * tq, 1), F32)]
        + _rs_scratch(rs_sends),
        compiler_params=_params(),
    )(proj, proj, proj, dcat, cstats, tri_after, tri_incl, *rs_sends)


def _window_sum(v, g, rows, forward):
    s_len = v.shape[0]
    s = v
    for step in range(g + 1):
        sh = 1 << step
        if forward:
            s = s + jnp.where(rows < s_len - sh, pltpu.roll(s, s_len - sh, axis=0), 0.0)
        else:
            s = s + jnp.where(rows >= sh, pltpu.roll(s, sh, axis=0), 0.0)
    return s


def _window_count(g, rows):
    return jnp.minimum(rows + 1, POOL_WINDOWS[g]).astype(F32)


def _pooled(u, g, rows):
    return _window_sum(u, g, rows, forward=False) / _window_count(g, rows) - u


def _group_cols(g):
    return slice(g * POOL_GROUP_DIM, (g + 1) * POOL_GROUP_DIM)


def _pool_fwd(proj, w_pool, pool_scale, cat, n_seq, seq):
    n_grp = len(POOL_WINDOWS)
    width = n_grp * POOL_GROUP_DIM
    assert [1 << (g + 1) for g in range(n_grp)] == list(POOL_WINDOWS)

    def body(u_ref, w_ref, s_ref, alias_ref, o_ref):
        del alias_ref
        rows = lax.broadcasted_iota(jnp.int32, (seq, 1), 0)
        for g in range(n_grp):
            cols = _group_cols(g)
            pooled = _pooled(u_ref[:, cols].astype(F32), g, rows)
            y = _dot_nn(pooled.astype(BF16), w_ref[g].astype(BF16))
            o_ref[:, cols] = (y * s_ref[:, cols]).astype(BF16)

    return pl.pallas_call(
        body, name="pool_fwd", grid=(n_seq,),
        out_shape=jax.ShapeDtypeStruct(cat.shape, BF16),
        in_specs=[pl.BlockSpec((seq, width), lambda b: (b, 3)),
                  pl.BlockSpec((n_grp, POOL_GROUP_DIM, POOL_GROUP_DIM), lambda b: (0, 0, 0)),
                  pl.BlockSpec((1, width), lambda b: (0, 0)),
                  pl.BlockSpec(memory_space=pl.ANY)],
        out_specs=pl.BlockSpec((None, seq, width), lambda b: (1, b, 0)),
        input_output_aliases={3: 0},
        compiler_params=_params(),
    )(proj, w_pool, pool_scale, cat)


def _pool_bwd(proj, dcat, w_pool, pool_scale, dqkv, n_seq, seq):
    n_grp = len(POOL_WINDOWS)
    width = n_grp * POOL_GROUP_DIM

    def body(u_ref, dp_ref, w_ref, s_ref, alias_ref, du_ref, gw_ref, gs_ref):
        del alias_ref
        b = pl.program_id(0)
        rows = lax.broadcasted_iota(jnp.int32, (seq, 1), 0)
        for g in range(n_grp):
            cols = _group_cols(g)
            pb = _pooled(u_ref[:, cols].astype(F32), g, rows).astype(BF16)
            wb = w_ref[g].astype(BF16)
            z = _dot_nn(pb, wb)
            dp = dp_ref[:, cols].astype(F32)
            _acc(gs_ref.at[:, cols], _colsum(dp * z), b == 0)
            dys = (dp * s_ref[:, cols]).astype(BF16)
            _acc(gw_ref.at[g], _dot_tn(pb, dys), b == 0)
            dpooled = _dot_nt(dys, wb)
            du = _window_sum(dpooled / _window_count(g, rows), g, rows, forward=True) - dpooled
            du_ref[:, cols] = du.astype(BF16)

    return pl.pallas_call(
        body, name="pool_bwd", grid=(n_seq,),
        out_shape=(jax.ShapeDtypeStruct(dqkv.shape, BF16),
                   jax.ShapeDtypeStruct((n_grp, POOL_GROUP_DIM, POOL_GROUP_DIM), F32),
                   jax.ShapeDtypeStruct((1, width), F32)),
        in_specs=[pl.BlockSpec((seq, width), lambda b: (b, 3)),
                  pl.BlockSpec((seq, width), lambda b: (b, 1)),
                  pl.BlockSpec((n_grp, POOL_GROUP_DIM, POOL_GROUP_DIM), lambda b: (0, 0, 0)),
                  pl.BlockSpec((1, width), lambda b: (0, 0)),
                  pl.BlockSpec(memory_space=pl.ANY)],
        out_specs=(pl.BlockSpec((None, seq, width), lambda b: (3, b, 0)),
                   pl.BlockSpec((n_grp, POOL_GROUP_DIM, POOL_GROUP_DIM), lambda b: (0, 0, 0)),
                   pl.BlockSpec((1, width), lambda b: (0, 0))),
        input_output_aliases={4: 0},
        compiler_params=_params(),
    )(proj, dcat, w_pool, pool_scale, dqkv)


def _cond_fwd(c_all, w_cond, b_cols):
    n, _ = c_all.shape
    cols = w_cond.shape[1]

    def body(c_ref, w_ref, b_ref, o_ref):
        cv = c_ref[...]
        a = cv * jax.nn.sigmoid(cv)
        o_ref[...] = jnp.dot(a, w_ref[...], preferred_element_type=F32,
                             precision=lax.Precision.HIGHEST) + b_ref[...]

    return pl.pallas_call(
        body, name="cond_fwd", out_shape=jax.ShapeDtypeStruct((n, cols), F32),
        compiler_params=_params(),
    )(c_all, w_cond, b_cols)


def _cond_bwd_adamw(c_all, dmod_all, dmod_cols, w, m_w, v_w, b, m_b, v_b):
    def body(c_ref, dm_ref, dmc_ref, w_ref, mw_ref, vw_ref, b_ref, mb_ref, vb_ref,
             gw_ref, dw_ref, nmw_ref, nvw_ref, gb_ref, db_ref, nmb_ref, nvb_ref):
        cv = c_ref[...]
        a = cv * jax.nn.sigmoid(cv)
        gw = lax.dot_general(a, dmc_ref[...], (((0,), (0,)), ((), ())),
                             preferred_element_type=F32, precision=lax.Precision.HIGHEST)
        gw_ref[...] = gw
        dw_ref[...], nmw_ref[...], nvw_ref[...] = _adamw_math(w_ref[...], gw, mw_ref[...], vw_ref[...])
        gb = _colsum(dm_ref[...])
        gb_ref[...] = gb
        db_ref[...], nmb_ref[...], nvb_ref[...] = _adamw_math(b_ref[...], gb, mb_ref[...], vb_ref[...])

    w_sds, b_sds = jax.ShapeDtypeStruct(w.shape, F32), jax.ShapeDtypeStruct(b.shape, F32)
    outs = pl.pallas_call(
        body, name="cond_bwd_adamw", out_shape=(w_sds,) * 4 + (b_sds,) * 4, compiler_params=_params(),
    )(c_all, dmod_all, dmod_cols, w, m_w, v_w, b, m_b, v_b)
    return outs[:4], outs[4:]


def _adamw_math(w, g, m, v):
    m = ADAM_B1 * m + (1.0 - ADAM_B1) * g
    v = ADAM_B2 * v + (1.0 - ADAM_B2) * (g * g)
    m_hat = m / (1.0 - ADAM_B1 ** ADAM_STEP)
    v_hat = v / (1.0 - ADAM_B2 ** ADAM_STEP)
    delta = -ADAM_LR * (m_hat / (jnp.sqrt(v_hat) + ADAM_EPS) + ADAM_WD * w)
    return delta, m, v


def _adamw_small(ws, gparts, ms, vs, name):
    n = len(ws)

    def body(*refs):
        w_r, g_r, m_r, v_r = refs[:n], refs[n:2 * n], refs[2 * n:3 * n], refs[3 * n:4 * n]
        outs = refs[4 * n:]
        for i in range(n):
            g = g_r[i][0]
            for dev in range(1, g_r[i].shape[0]):
                g = g + g_r[i][dev]
            delta, m, v = _adamw_math(w_r[i][...], g, m_r[i][...], v_r[i][...])
            outs[i][...] = g
            outs[n + i][...] = delta
            outs[2 * n + i][...] = m
            outs[3 * n + i][...] = v

    sds = [jax.ShapeDtypeStruct(w.shape, F32) for w in ws]
    return pl.pallas_call(
        body, name=name, out_shape=tuple(sds * 4), compiler_params=_params(),
    )(*ws, *gparts, *ms, *vs)


def kernel(x, c, w_cond, b_cond, g_mix_pre, g_mix_post, w_in, w_pool, pool_scale, w_out, g_ffn_pre, g_ffn_post, w_gate, w_up, w_down, loss_target, m_w_cond, m_b_cond, m_g_mix_pre, m_g_mix_post, m_w_in, m_w_pool, m_pool_scale, m_w_out, m_g_ffn_pre, m_g_ffn_post, m_w_gate, m_w_up, m_w_down, v_w_cond, v_b_cond, v_g_mix_pre, v_g_mix_post, v_w_in, v_w_pool, v_pool_scale, v_w_out, v_g_ffn_pre, v_g_ffn_post, v_w_gate, v_w_up, v_w_down):
    n_seq, seq, d = x.shape
    t = n_seq * seq
    xi, yi, ci = _mesh_pos()
    me = 4 * xi + 2 * yi + ci
    x2 = x.reshape(t, d)
    tgt2 = loss_target.reshape(t, d)
    in_rows = w_in.shape[2]
    out_rows = w_out.shape[1]
    ff_rows = w_gate.shape[2]
    ff = N_DEV * ff_rows
    cond_cols = w_cond.shape[2]

    win_t = w_in[0].T.astype(BF16)
    wout_s = w_out[0].astype(BF16)
    wg_t = w_gate[0].T.astype(BF16)
    wu_t = w_up[0].T.astype(BF16)
    wd_s = w_down[0].astype(BF16)
    c_all = _all_gather(c, "ag_c").reshape(N_DEV * n_seq, d)

    b_cols = lax.dynamic_slice_in_dim(b_cond, me * cond_cols, cond_cols, axis=1)
    mod_cols = _cond_fwd(c_all, w_cond[0], b_cols)
    mod_g = _all_gather(mod_cols, "ag_mod")
    mod_mine = lax.dynamic_slice_in_dim(mod_g, me * n_seq, n_seq, axis=1)
    mod = jnp.transpose(mod_mine, (1, 0, 2)).reshape(n_seq, N_MOD, d)

    h1, win_g = _pre_mix(x2, g_mix_pre, mod, seq, [win_t],
                         [jax.ShapeDtypeStruct((N_DEV, in_rows, d), BF16)], [(0, ())])
    win_full = win_g.reshape(N_DEV * in_rows, d)
    proj = _matmul(h1, win_full, "nt", BF16, 512, N_DEV * in_rows, d, "proj")
    tq = ATT_TILE
    ids = jnp.arange(tq)
    tri_after = jnp.tile(-(ids[:, None] >= ids[None, :]).astype(BF16), (2, 1))
    tri_incl = (ids[:, None] <= ids[None, :]).astype(BF16)
    attn, cstats, wout_g, wgu_g, wd_g = _attn_fwd(
        proj, tri_after, n_seq, seq, [wout_s, wg_t, wu_t, wd_s],
        [jax.ShapeDtypeStruct((N_DEV, out_rows, d), BF16), jax.ShapeDtypeStruct((2, N_DEV, ff_rows, d), BF16),
         jax.ShapeDtypeStruct((N_DEV, ff_rows, d), BF16)],
        [(0, ()), (1, (0,)), (1, (1,)), (2, ())])
    wout_full = wout_g.reshape(N_DEV * out_rows, d)
    wgu_full = wgu_g.reshape(2, ff, d)
    wd_full = wd_g.reshape(ff, d)
    cat = _pool_fwd(proj, w_pool[0], pool_scale, attn, n_seq, seq)
    tok_f32, tok_bf16 = jax.ShapeDtypeStruct((t, d), F32), jax.ShapeDtypeStruct((t, d), BF16)
    seq_sds, vec_sds = jax.ShapeDtypeStruct((n_seq, 1, d), F32), jax.ShapeDtypeStruct((1, d), F32)
    mix, x1, h2 = _matmul_rows(
        cat, wout_full.reshape(2, d // 2, d), ROW_TILE, seq, "mix_mid", _mid_epilogue,
        [x2, g_mix_post, g_ffn_pre, mod], ["tok", "vec", "vec", "mod"],
        [tok_f32, tok_f32, tok_bf16], ["tok", "tok", "tok"])
    gu, act = _ffn_up(h2, wgu_full, 512, ff // 2)
    loss_sum, dy, df, dgate_f, gg_ffn_post = _matmul_rows(
        act, wd_full, ROW_TILE, seq, "ffn_down_post", _post_epilogue,
        [x1, tgt2, g_ffn_post, mod], ["tok", "tok", "vec", "mod"],
        [jax.ShapeDtypeStruct((1, LANES), F32), tok_f32, tok_bf16, seq_sds, vec_sds],
        ["loss", "tok", "tok", "seq", "vec"])

    dgu = _ffn_act_bwd(df, wd_full, gu, 512, ff // 2)
    gwd_b = _matmul(act, df, "tn", BF16, ff // 2, d // 2, t, "grad_w_down")
    gwgu_b = _matmul(dgu, h2, "tn", BF16, ff // 2, d // 2, t, "grad_w_gate_up")
    dx1, dmix, dshift_f, dscale_f, dgate_m, gg_ffn_pre, gg_mix_post = _matmul_rows(
        dgu, wgu_full, ROW_TILE, seq, "dh2_bwd_mid", _bwd_mid_epilogue,
        [dy, x1, mix, g_ffn_pre, g_mix_post, mod], ["tok", "tok", "tok", "vec", "vec", "mod"],
        [tok_f32, tok_bf16, seq_sds, seq_sds, seq_sds, vec_sds, vec_sds],
        ["tok", "tok", "seq", "seq", "seq", "vec", "vec"])
    dcat = _matmul(dmix, wout_full, "nt", BF16, 512, d, d, "dcat")
    gwout_b = _matmul(cat, dmix, "tn", BF16, d // 2, d, t, "grad_w_out")
    dqkv, rv_wgu, rv_wd, rv_wout = _attn_bwd(
        proj, dcat, cstats, tri_after, tri_incl, n_seq, seq,
        [gwgu_b.reshape(2, N_DEV, ff_rows, d), gwd_b.reshape(1, N_DEV, ff_rows, d),
         gwout_b.reshape(1, N_DEV, out_rows, d)])
    dproj, gw_pool, gs_pool = _pool_bwd(proj, dcat, w_pool[0], pool_scale, dqkv, n_seq, seq)
    pad_d = lambda v: jnp.pad(v, ((0, 0), (0, d - v.shape[1])))
    n_gw = gw_pool.size // d
    early = jnp.concatenate(
        [gg_mix_post, gg_ffn_pre, gg_ffn_post, pad_d(gs_pool), pad_d(loss_sum), jnp.zeros((3, d), F32),
         gw_pool.reshape(n_gw, d),
         jnp.concatenate([dgate_m, dshift_f, dscale_f, dgate_f], axis=1).reshape(n_seq * 4, d)], axis=0)
    gwin_b, early_g = _matmul(
        dproj, h1, "tn", BF16, d // 2, d, t, "grad_w_in",
        ag=([early], [jax.ShapeDtypeStruct((N_DEV,) + early.shape, F32)], [(0, ())]))
    grad_x, dshift_m, dscale_m, gg_mix_pre, rv_win = _matmul_rows(
        dproj, win_full.reshape(4, d // 2, d), ROW_TILE, seq, "dh1_bwd_pre", _bwd_pre_epilogue,
        [dx1, x2, g_mix_pre, mod], ["tok", "tok", "vec", "mod"],
        [tok_f32, seq_sds, seq_sds, vec_sds], ["tok", "seq", "seq", "vec"],
        rs_sends=[gwin_b.reshape(1, N_DEV, in_rows, d)])


    late = jnp.concatenate([gg_mix_pre, dshift_m.reshape(n_seq, d), dscale_m.reshape(n_seq, d),
                            jnp.zeros((8 - 1 - 2 * n_seq, d), F32)], axis=0)
    late_g = _all_gather(late, "ag_late")
    loss = jnp.sum(early_g[:, 4, 0]) * (0.5 / d)
    dmod_all = jnp.concatenate(
        [late_g[:, 1:1 + n_seq, None, :], late_g[:, 1 + n_seq:1 + 2 * n_seq, None, :],
         early_g[:, 8 + n_gw:, :].reshape(N_DEV, n_seq, 4, d)], axis=2).reshape(N_DEV * n_seq, N_MOD * d)
    dmod_cols = lax.dynamic_slice_in_dim(dmod_all, me * cond_cols, cond_cols, axis=1)
    o_cond, o_bcond = _cond_bwd_adamw(c_all, dmod_all, dmod_cols, w_cond[0], m_w_cond[0], v_w_cond[0],
                                      b_cond, m_b_cond, v_b_cond)
    o_cond = tuple(o[None] for o in o_cond)

    small_ws = [g_mix_pre, g_mix_post, g_ffn_pre, g_ffn_post, pool_scale, w_pool.reshape(-1, POOL_GROUP_DIM)]
    small_ms = [m_g_mix_pre, m_g_mix_post, m_g_ffn_pre, m_g_ffn_post, m_pool_scale, m_w_pool.reshape(-1, POOL_GROUP_DIM)]
    small_vs = [v_g_mix_pre, v_g_mix_post, v_g_ffn_pre, v_g_ffn_post, v_pool_scale, v_w_pool.reshape(-1, POOL_GROUP_DIM)]
    small_gparts = [late_g[:, 0:1, :], early_g[:, 0:1, :], early_g[:, 1:2, :], early_g[:, 2:3, :],
                    early_g[:, 3:4, :pool_scale.shape[1]],
                    early_g[:, 8:8 + n_gw, :].reshape(N_DEV, -1, POOL_GROUP_DIM)]
    so = _adamw_small(small_ws, small_gparts, small_ms, small_vs, "adamw_small")
    ns = len(small_ws)
    sg, sdl, sm, sv = so[:ns], so[ns:2 * ns], so[2 * ns:3 * ns], so[3 * ns:]
    pool_shape = w_pool.shape
    fix = lambda lst: [lst[0], lst[1], lst[2], lst[3], lst[4], lst[5].reshape(pool_shape)]
    sg, sdl, sm, sv = fix(sg), fix(sdl), fix(sm), fix(sv)


    def reduced(mine, recv, slab, w, m, v, name, transposed=False, transpose=False):
        turn = (lambda u: u.T) if transposed else (lambda u: u)
        outs = _rs_final_adamw(mine, recv, slab, turn(w[0]), turn(m[0]), turn(v[0]), name, transpose)
        return tuple(turn(o)[None] for o in outs)

    o_in = reduced(gwin_b.reshape(1, N_DEV, in_rows, d), rv_win, 0, w_in, m_w_in, v_w_in, "adamw_w_in",
                   transpose=True)
    o_out = reduced(gwout_b.reshape(1, N_DEV, out_rows, d), rv_wout, 0, w_out, m_w_out, v_w_out, "adamw_w_out")
    gwgu8 = gwgu_b.reshape(2, N_DEV, ff_rows, d)
    o_gate = reduced(gwgu8, rv_wgu, 0, w_gate, m_w_gate, v_w_gate, "adamw_w_gate", transposed=True)
    o_up = reduced(gwgu8, rv_wgu, 1, w_up, m_w_up, v_w_up, "adamw_w_up", transposed=True)
    o_down = reduced(gwd_b.reshape(1, N_DEV, ff_rows, d), rv_wd, 0, w_down, m_w_down, v_w_down, "adamw_w_down")

    def pick(k):
        small_k = [sg, sdl, sm, sv][k]
        return [o_cond[k], o_bcond[k], small_k[0], small_k[1], o_in[k], small_k[5], small_k[4], o_out[k],
                small_k[2], small_k[3], o_gate[k], o_up[k], o_down[k]]

    return (loss, grad_x.reshape(n_seq, seq, d), *pick(0), *pick(1), *pick(2), *pick(3))
```
